```python
import math
import jax, jax.numpy as jnp
from jax import lax
import numpy as np

D_MODEL = 1024
BATCH = 32
SEQ = 256
DEPTH = 4
DEC_BATCH = 8
DEC_SEQ = 4096
PAST_LEN = 512

GRID_W = 64
W_HY = 512
HY_ORDER = 2
HY_BANDS = 8
HY_POS_FEAT = 1 + 2 * HY_BANDS
HY_FILT_HID = 64
HY_DECAY_TARGET = 1e-2
HY_FAST_PCT = 0.3
HY_SLOW_PCT = 1.5
ATT_HEADS = 8
ATT_KV_HEADS = 2
ATT_GROUP = ATT_HEADS // ATT_KV_HEADS
HEAD_DIM = 64
W_ATT = ATT_HEADS * HEAD_DIM
W_KV = ATT_KV_HEADS * HEAD_DIM
ATT_WINDOW = 128
BLOCK = 128
RET_HEADS = 8
RET_DIM = 64
W_RET = RET_HEADS * RET_DIM
RET_CHUNK = 128
ROPE_BASE = 10000.0
EPS = 1e-6
NEG = -1e30
N_BRANCH = 3
IN_SPLITS = (3 * W_HY, W_HY, W_ATT, W_KV, W_KV, W_ATT, W_RET, W_RET, W_RET, W_RET)
IN_DIM = sum(IN_SPLITS)

kernel_name = "hybrid_hyena_gqa_retention_diffusion_step"


def rmsnorm(x, w):
    xf = x.astype(jnp.float32)
    return xf * lax.rsqrt(jnp.mean(xf * xf, axis=-1, keepdims=True) + EPS) * w.astype(jnp.float32)


def adaln(cond, w_mod, b_mod):
    mod = (jax.nn.silu(cond) @ w_mod + b_mod).astype(jnp.float32)
    shift, scale, gate = jnp.split(mod[:, None, :], 3, axis=-1)
    return shift, scale, gate


def axial_rope_tables(L):
    rows = L // GRID_W
    row = jnp.repeat(jnp.arange(rows), GRID_W).astype(jnp.float32)
    col = jnp.tile(jnp.arange(GRID_W), rows).astype(jnp.float32)
    quarter = HEAD_DIM // 4
    inv = ROPE_BASE ** (-jnp.arange(quarter, dtype=jnp.float32) / quarter)
    ang = jnp.concatenate([row[:, None] * inv, col[:, None] * inv], axis=-1)
    return jnp.cos(ang), jnp.sin(ang)


def apply_axial_rope(x, rope):
    cos, sin = rope
    q = HEAD_DIM // 4

    def rot(hx, c, s):
        h1, h2 = hx[..., :q], hx[..., q:]
        return jnp.concatenate([h1 * c - h2 * s, h1 * s + h2 * c], axis=-1)

    xf = x.astype(jnp.float32)
    xr = rot(xf[..., :2 * q], cos[:, None, :q], sin[:, None, :q])
    xc = rot(xf[..., 2 * q:], cos[:, None, q:], sin[:, None, q:])
    return jnp.concatenate([xr, xc], axis=-1)


def short_conv(u, w):
    L = u.shape[1]
    up = jnp.pad(u, ((0, 0), (1, 1), (0, 0)))
    return up[:, :L] * w[0] + up[:, 1:L + 1] * w[1] + up[:, 2:] * w[2]


def hyena_filters(L, w1, b1, freq, w2):
    t = jnp.linspace(0.0, 1.0, L, dtype=jnp.float32)[:, None]
    w = 2.0 * math.pi * jnp.arange(L, dtype=jnp.float32)[:, None] / L
    f = jnp.linspace(1e-4, HY_BANDS - 1, HY_BANDS, dtype=jnp.float32)[None, :]
    z = jnp.concatenate([t, jnp.cos(f * w), -jnp.sin(f * w)], axis=-1)
    hid = jnp.sin(freq.astype(jnp.float32) * (z @ w1.astype(jnp.float32) + b1.astype(jnp.float32)))
    h = (hid @ w2.astype(jnp.float32)).reshape(L, HY_ORDER, 2, W_HY)
    max_decay = math.log(HY_DECAY_TARGET) / HY_FAST_PCT
    min_decay = math.log(HY_DECAY_TARGET) / HY_SLOW_PCT
    deltas = jnp.linspace(min_decay, max_decay, W_HY, dtype=jnp.float32)
    window = jnp.exp(-t * jnp.abs(deltas))
    h = h * window[:, None, None, :]
    return h / jnp.sum(jnp.abs(h), axis=(0, 2), keepdims=True)


def fft_conv(u, h):
    L = u.shape[1]
    uf = jnp.fft.rfft(u, n=2 * L, axis=1)
    hf = jnp.fft.rfft(h, n=2 * L, axis=0)
    return jnp.fft.irfft(uf * hf[None], n=2 * L, axis=1)[:, :L]


def bidir_long_conv(u, h_fwd, h_bwd, skip):
    fwd = fft_conv(u, h_fwd)
    bwd = jnp.flip(fft_conv(jnp.flip(u, axis=1), h_bwd), axis=1)
    return fwd + bwd + u * skip.astype(jnp.float32)


def hyena_branch(u, conv_w, w1, b1, freq, w2, skip):
    L = u.shape[1]
    u = short_conv(u, conv_w).astype(jnp.float32)
    v, x1, x2 = jnp.split(u, 3, axis=-1)
    h = hyena_filters(L, w1, b1, freq, w2)
    z = x1 * bidir_long_conv(v, h[:, 0, 0], h[:, 0, 1], skip[0])
    z = x2 * bidir_long_conv(z, h[:, 1, 0], h[:, 1, 1], skip[1])
    return z


def sink_attend(q, k, v, bias, sink):
    s = jnp.einsum('bqhgd,bkhd->bhgqk', q.astype(jnp.float32), k.astype(jnp.float32)) * (HEAD_DIM ** -0.5)
    if bias is not None:
        s = s + bias
    sk = sink.astype(jnp.float32)[None, :, :, None, None]
    m = jnp.maximum(jnp.max(s, axis=-1, keepdims=True), sk)
    p = jnp.exp(s - m)
    denom = jnp.sum(p, axis=-1, keepdims=True) + jnp.exp(sk - m)
    return jnp.einsum('bhgqk,bkhd->bqhgd', p / denom, v.astype(jnp.float32))


def context_attention(q, k, v, sink):
    B, Lc = q.shape[:2]
    nb = Lc // BLOCK
    qb = q.reshape(B, nb, BLOCK, ATT_KV_HEADS, ATT_GROUP, HEAD_DIM).transpose(1, 0, 2, 3, 4, 5)
    out = lax.map(lambda qi: sink_attend(qi, k, v, None, sink), qb)
    return out.transpose(1, 0, 2, 3, 4, 5).reshape(B, Lc, W_ATT)


def latent_attention(q, k, v, k_ctx, v_ctx, sink):
    B, L = q.shape[:2]
    nb = L // BLOCK
    span = BLOCK + 2 * ATT_WINDOW
    q = q.astype(jnp.float32)
    kp = jnp.pad(k.astype(jnp.float32), ((0, 0), (ATT_WINDOW, ATT_WINDOW), (0, 0), (0, 0)))
    vp = jnp.pad(v.astype(jnp.float32), ((0, 0), (ATT_WINDOW, ATT_WINDOW), (0, 0), (0, 0)))
    kc = k_ctx.astype(jnp.float32)
    vc = v_ctx.astype(jnp.float32)
    rel = (jnp.arange(span)[None, :] - ATT_WINDOW) - jnp.arange(BLOCK)[:, None]
    band_ok = jnp.abs(rel) <= ATT_WINDOW
    ctx_bias = jnp.zeros((BLOCK, kc.shape[1]), jnp.float32)

    def blk(i):
        start = i * BLOCK
        qi = lax.dynamic_slice_in_dim(q, start, BLOCK, axis=1)
        ki = lax.dynamic_slice_in_dim(kp, start, span, axis=1)
        vi = lax.dynamic_slice_in_dim(vp, start, span, axis=1)
        key_abs = start - ATT_WINDOW + jnp.arange(span)
        ok = band_ok & ((key_abs >= 0) & (key_abs < L))[None, :]
        bias = jnp.concatenate([jnp.where(ok, 0.0, NEG).astype(jnp.float32), ctx_bias], axis=1)
        kk = jnp.concatenate([ki, kc], axis=1)
        vv = jnp.concatenate([vi, vc], axis=1)
        return sink_attend(qi, kk, vv, bias, sink)

    out = lax.map(blk, jnp.arange(nb))
    return out.transpose(1, 0, 2, 3, 4, 5).reshape(B, L, W_ATT)


def retention_scan(q, k, v, log_g, s0):
    B, L, H, D = q.shape
    C = RET_CHUNK
    nc = L // C

    def chunks(a):
        return a.reshape(B, nc, C, H, D).transpose(1, 0, 3, 2, 4)

    i = jnp.arange(C, dtype=jnp.float32)
    lg = log_g[:, None]
    diff = i[:, None] - i[None, :]
    dmat = jnp.where(diff >= 0, jnp.exp(lg[..., None] * jnp.maximum(diff, 0.0)), 0.0)
    q_dec = jnp.exp(lg * (i + 1.0))[..., None]
    k_dec = jnp.exp(lg * (C - 1.0 - i))[..., None]
    c_dec = jnp.exp(lg * float(C))[..., None]

    def step(s, xs):
        qc, kc, vc = xs
        inner = jnp.einsum('bhij,bhjd->bhid', jnp.einsum('bhid,bhjd->bhij', qc, kc) * dmat, vc)
        cross = jnp.einsum('bhid,bhde->bhie', qc * q_dec, s)
        s_new = c_dec * s + jnp.einsum('bhjd,bhje->bhde', kc * k_dec, vc)
        return s_new, inner + cross

    s_fin, o = lax.scan(step, s0.astype(jnp.float32), (chunks(q), chunks(k), chunks(v)))
    return o.transpose(1, 0, 3, 2, 4).reshape(B, L, H, D), s_fin


def retention_branch(q, k, v, theta, gn, s0, rope):
    B, L = q.shape[:2]
    q = q.reshape(B, L, RET_HEADS, RET_DIM).astype(jnp.float32)
    k = k.reshape(B, L, RET_HEADS, RET_DIM).astype(jnp.float32)
    v = v.reshape(B, L, RET_HEADS, RET_DIM).astype(jnp.float32)
    if rope is not None:
        q = apply_axial_rope(q, rope)
        k = apply_axial_rope(k, rope)
    k = k * (RET_DIM ** -0.5)
    lg = jax.nn.log_sigmoid(theta.astype(jnp.float32))
    o_f, s_f = retention_scan(q, k, v, lg[0], s0[:, 0])
    o_b, s_b = retention_scan(jnp.flip(q, 1), jnp.flip(k, 1), jnp.flip(v, 1), lg[1], s0[:, 1])
    o = o_f + jnp.flip(o_b, 1)
    o = o * lax.rsqrt(jnp.mean(o * o, axis=-1, keepdims=True) + EPS)
    o = o.reshape(B, L, W_RET) * gn.astype(jnp.float32)
    return o, jnp.stack([s_f, s_b], axis=1)


def mixer_layer(x, cond, p, rope, ctx):
    B, L, _ = x.shape
    shift, scale, gate = adaln(cond, p['w_mod'], p['b_mod'])
    h = (rmsnorm(x, p['norm_w']) * (1.0 + scale) + shift).astype(x.dtype)
    u = h @ p['w_in']
    idx = np.cumsum(IN_SPLITS)[:-1].tolist()
    hy_in, hy_g, qa, ka, va, ga, qr, kr, vr, gr = jnp.split(u, idx, axis=-1)

    ya = hyena_branch(hy_in, p['hy_conv'], p['hy_filt_w1'], p['hy_filt_b1'], p['hy_filt_freq'],
                      p['hy_filt_w2'], p['hy_skip']) * jax.nn.silu(hy_g.astype(jnp.float32))

    qa = qa.reshape(B, L, ATT_HEADS, HEAD_DIM)
    ka = ka.reshape(B, L, ATT_KV_HEADS, HEAD_DIM)
    va = va.reshape(B, L, ATT_KV_HEADS, HEAD_DIM)
    sink = p['attn_sink'].reshape(ATT_KV_HEADS, ATT_GROUP)
    if ctx is None:
        qg = qa.reshape(B, L, ATT_KV_HEADS, ATT_GROUP, HEAD_DIM)
        yb = context_attention(qg, ka, va, sink)
        s0 = jnp.zeros((B, 2, RET_HEADS, RET_DIM, RET_DIM), jnp.float32)
    else:
        k_ctx, v_ctx, s0 = ctx
        qg = apply_axial_rope(qa, rope).reshape(B, L, ATT_KV_HEADS, ATT_GROUP, HEAD_DIM)
        kg = apply_axial_rope(ka, rope)
        yb = latent_attention(qg, kg, va, k_ctx, v_ctx, sink)
    yb = yb * jax.nn.silu(ga.astype(jnp.float32))

    yc, s_fin = retention_branch(qr, kr, vr, p['ret_theta'], p['ret_gn'], s0, rope)
    yc = yc * jax.nn.silu(gr.astype(jnp.float32))

    g = jax.nn.sigmoid((h @ p['w_merge'] + p['b_merge']).astype(jnp.float32))
    g_a, g_b, g_c = jnp.split(g, N_BRANCH, axis=-1)
    merged = g_a * (ya @ p['w_branch_a']) + g_b * (yb @ p['w_branch_b']) + g_c * (yc @ p['w_branch_c'])
    out = merged @ p['w_out']
    x_new = (x.astype(jnp.float32) + gate * out).astype(x.dtype)
    return x_new, ka, va, s_fin


def setup_inputs(seed: int = 0) -> dict:
    key = jax.random.key(seed)
    ks = jax.random.split(key, 32)
    n = lambda i, shape: jax.random.normal(ks[i], shape, jnp.float32)
    D = D_MODEL
    gam = 1.0 - 2.0 ** (-5.0 - np.arange(RET_HEADS))
    theta0 = jnp.asarray(np.log(gam / (1.0 - gam)), jnp.float32)
    return {
        "x_prompt": n(0, (BATCH, SEQ, D)),
        "x_sample": n(1, (DEC_BATCH, DEC_SEQ, D)),
        "c": n(2, (DEC_BATCH, D)),
        "cache_k": n(3, (DEC_BATCH, DEPTH, PAST_LEN, ATT_KV_HEADS, HEAD_DIM)),
        "cache_v": n(4, (DEC_BATCH, DEPTH, PAST_LEN, ATT_KV_HEADS, HEAD_DIM)),
        "state_ret": n(5, (DEC_BATCH, DEPTH, 2, RET_HEADS, RET_DIM, RET_DIM)),
        "c_ctx": n(6, (D,)),
        "norm_w": 1.0 + 0.02 * n(7, (DEPTH, D)),
        "w_mod": 0.5 * D ** -0.5 * n(8, (DEPTH, D, 3 * D)),
        "b_mod": 0.01 * n(9, (DEPTH, 3 * D)),
        "w_in": D ** -0.5 * n(10, (DEPTH, D, IN_DIM)),
        "hy_conv": 3 ** -0.5 * n(11, (DEPTH, 3, 3 * W_HY)),
        "hy_filt_w1": HY_POS_FEAT ** -0.5 * n(12, (DEPTH, HY_POS_FEAT, HY_FILT_HID)),
        "hy_filt_b1": 0.1 * n(13, (DEPTH, HY_FILT_HID)),
        "hy_filt_freq": 1.0 + 0.1 * n(14, (DEPTH, HY_FILT_HID)),
        "hy_filt_w2": HY_FILT_HID ** -0.5 * n(15, (DEPTH, HY_FILT_HID, HY_ORDER * 2 * W_HY)),
        "hy_skip": 0.5 * n(16, (DEPTH, HY_ORDER, W_HY)),
        "attn_sink": 0.5 * n(17, (DEPTH, ATT_HEADS)),
        "ret_theta": theta0 + 0.1 * n(18, (DEPTH, 2, RET_HEADS)),
        "ret_gn": 1.0 + 0.02 * n(19, (DEPTH, W_RET)),
        "w_branch_a": W_HY ** -0.5 * n(20, (DEPTH, W_HY, D)),
        "w_branch_b": W_ATT ** -0.5 * n(21, (DEPTH, W_ATT, D)),
        "w_branch_c": W_RET ** -0.5 * n(22, (DEPTH, W_RET, D)),
        "w_merge": D ** -0.5 * n(23, (DEPTH, D, N_BRANCH * D)),
        "b_merge": 0.1 * n(24, (DEPTH, N_BRANCH * D)),
        "w_out": D ** -0.5 * n(25, (DEPTH, D, D)),
        "final_norm_w": 1.0 + 0.02 * n(26, (D,)),
    }


def reference(x_prompt, x_sample, c, cache_k, cache_v, state_ret, c_ctx, norm_w, w_mod, b_mod,
              w_in, hy_conv, hy_filt_w1, hy_filt_b1, hy_filt_freq, hy_filt_w2, hy_skip, attn_sink,
              ret_theta, ret_gn, w_branch_a, w_branch_b, w_branch_c, w_merge, b_merge, w_out,
              final_norm_w):
    rope = axial_rope_tables(x_sample.shape[1])
    xp, xs = x_prompt, x_sample
    cond_ctx = c_ctx[None, :]
    ks_out, vs_out, ss_out = [], [], []
    for l in range(DEPTH):
        p = dict(norm_w=norm_w[l], w_mod=w_mod[l], b_mod=b_mod[l], w_in=w_in[l], hy_conv=hy_conv[l],
                 hy_filt_w1=hy_filt_w1[l], hy_filt_b1=hy_filt_b1[l], hy_filt_freq=hy_filt_freq[l],
                 hy_filt_w2=hy_filt_w2[l], hy_skip=hy_skip[l], attn_sink=attn_sink[l],
                 ret_theta=ret_theta[l], ret_gn=ret_gn[l], w_branch_a=w_branch_a[l],
                 w_branch_b=w_branch_b[l], w_branch_c=w_branch_c[l], w_merge=w_merge[l],
                 b_merge=b_merge[l], w_out=w_out[l])
        xp, k_c, v_c, s_c = mixer_layer(xp, cond_ctx, p, None, None)
        ks_out.append(k_c)
        vs_out.append(v_c)
        ss_out.append(s_c)
        xs, _, _, _ = mixer_layer(xs, c, p, rope, (cache_k[:, l], cache_v[:, l], state_ret[:, l]))
    y_prompt = rmsnorm(xp, final_norm_w).astype(x_prompt.dtype)
    y_sample = rmsnorm(xs, final_norm_w).astype(x_sample.dtype)
    new_cache_k = jnp.stack(ks_out, axis=1).astype(x_prompt.dtype)
    new_cache_v = jnp.stack(vs_out, axis=1).astype(x_prompt.dtype)
    new_state_ret = jnp.stack(ss_out, axis=1).astype(x_prompt.dtype)
    return (y_prompt, y_sample, new_cache_k, new_cache_v, new_state_ret)
```

```python
import functools
import math

import numpy as np
import jax
import jax.numpy as jnp
from jax import lax
from jax.experimental import pallas as pl
from jax.experimental.pallas import tpu as pltpu

F32 = jnp.float32
BF16 = jnp.bfloat16
HIGHEST = lax.Precision.HIGHEST

D_MODEL = 1024
DEPTH = 4
GRID_W = 64
W_HY = 512
HY_BANDS = 8
HY_POS_FEAT = 1 + 2 * HY_BANDS
HY_FILT_HID = 64
HY_DECAY_TARGET = 1e-2
HY_FAST_PCT = 0.3
HY_SLOW_PCT = 1.5
ATT_HEADS = 8
ATT_KV_HEADS = 2
ATT_GROUP = ATT_HEADS // ATT_KV_HEADS
HEAD_DIM = 64
W_ATT = ATT_HEADS * HEAD_DIM
W_KV = ATT_KV_HEADS * HEAD_DIM
ATT_BLOCK = 128
RET_HEADS = 8
RET_DIM = 64
W_RET = RET_HEADS * RET_DIM
RET_CHUNK = 128
ROPE_BASE = 10000.0
EPS = 1e-6
NEG = -1e30

LANES = 128
VMEM_LIMIT = 56 * 1024 * 1024

IN_DIM = 5376
COL_HY = 0
COL_QA = 2048
COL_GA = 2560
COL_QR = 3072
COL_KR = 3584
COL_VR = 4096
COL_GR = 4608
COL_KA = 5120
COL_VA = 5248
_IN_PERM = np.concatenate([
    np.arange(0, 2048), np.arange(2048, 2560), np.arange(2816, 3328), np.arange(3328, 3840),
    np.arange(3840, 4352), np.arange(4352, 4864), np.arange(4864, 5376), np.arange(2560, 2688),
    np.arange(2688, 2816)])

LAT_L = 4096
LAT_N = 2 * LAT_L
NA = 64
NB = 128
Y_PITCH = 136
X_PITCH = 136


def _cparams(sem):
    return pltpu.CompilerParams(dimension_semantics=sem, vmem_limit_bytes=VMEM_LIMIT)


def _split_np(a):
    a32 = np.asarray(a, np.float32)
    hi = a32.astype(BF16)
    lo = (a32 - hi.astype(np.float32)).astype(BF16)
    return jnp.asarray(hi), jnp.asarray(lo)


def _split(x):
    hi = x.astype(BF16)
    lo = (x - hi.astype(F32)).astype(BF16)
    return hi, lo


def _dot(a, b):
    return jnp.dot(a, b, preferred_element_type=F32)


def _dot3c(chi, clo, x):
    xh, xl = _split(x)
    return _dot(chi, xh) + _dot(clo, xh) + _dot(chi, xl)


def _silu(x):
    return x * jax.nn.sigmoid(x)


def _mod_kernel(c_ref, w_ref, b_ref, o_ref):
    s = _silu(c_ref[...])
    o_ref[0] = jnp.dot(s, w_ref[0], precision=HIGHEST, preferred_element_type=F32) + b_ref[0]


def _mod_call(cond, w_mod, b_mod):
    rows, d = cond.shape
    n = w_mod.shape[-1]
    tn = 1024
    return pl.pallas_call(
        _mod_kernel,
        grid=(DEPTH, n // tn),
        in_specs=[pl.BlockSpec((rows, d), lambda l, j: (0, 0)),
                  pl.BlockSpec((1, d, tn), lambda l, j: (l, 0, j)),
                  pl.BlockSpec((1, 1, tn), lambda l, j: (l, 0, j))],
        out_specs=pl.BlockSpec((1, rows, tn), lambda l, j: (l, 0, j)),
        out_shape=jax.ShapeDtypeStruct((DEPTH, rows, n), F32),
        compiler_params=_cparams(("arbitrary", "arbitrary")),
        name="adaln_mod",
    )(cond, w_mod, b_mod.reshape(DEPTH, 1, n))


def _modulated(x, nw, scale, shift):
    ms = jnp.mean(x * x, axis=-1, keepdims=True)
    h = x * lax.rsqrt(ms + EPS) * nw
    return h * (1.0 + scale) + shift


def _rope128(x, cos, sin_signed, first_half):
    up = pltpu.roll(x, LANES - 16, 1)
    dn = pltpu.roll(x, 16, 1)
    return x * cos + jnp.where(first_half, up, dn) * sin_signed


def _in_kernel(x_ref, shift_ref, scale_ref, nw_ref, w_ref, cos_ref, sin_ref, o_ref, *, rope):
    x = x_ref[0]
    tm = x.shape[0]
    hb = _modulated(x, nw_ref[...], scale_ref[0], shift_ref[0]).astype(BF16)
    if rope:
        cos = cos_ref[...]
        sin = sin_ref[...]
        lane = lax.broadcasted_iota(jnp.int32, (tm, LANES), 1)
        first_half = (lane % 32) < 16

    def seg(c0, width):
        return _dot(hb, w_ref[:, c0:c0 + width])

    def put_rope(c0, val, mul):
        for i in range(val.shape[1] // LANES):
            piece = val[:, i * LANES:(i + 1) * LANES]
            if rope:
                piece = _rope128(piece, cos, sin, first_half)
            if mul is not None:
                piece = piece * mul
            o_ref[0, :, c0 + i * LANES:c0 + (i + 1) * LANES] = piece

    for g in range(4):
        o_ref[0, :, COL_HY + g * 512:COL_HY + (g + 1) * 512] = seg(COL_HY + g * 512, 512)
    put_rope(COL_QA, seg(COL_QA, 512), None)
    o_ref[0, :, COL_GA:COL_GA + 512] = _silu(seg(COL_GA, 512))
    put_rope(COL_QR, seg(COL_QR, 512), None)
    put_rope(COL_KR, seg(COL_KR, 512), RET_DIM ** -0.5)
    o_ref[0, :, COL_VR:COL_VR + 512] = seg(COL_VR, 512)
    o_ref[0, :, COL_GR:COL_GR + 512] = _silu(seg(COL_GR, 512))
    put_rope(COL_KA, seg(COL_KA, 128), None)
    o_ref[0, :, COL_VA:COL_VA + 128] = seg(COL_VA, 128)


def _in_call(x, shift, scale, nw, w, cos_t, sin_t, *, rope, tm):
    b, l, d = x.shape
    per_batch = shift.shape[0] > 1
    mod_map = (lambda i, j: (i, 0, 0)) if per_batch else (lambda i, j: (0, 0, 0))
    return pl.pallas_call(
        functools.partial(_in_kernel, rope=rope),
        grid=(b, l // tm),
        in_specs=[pl.BlockSpec((1, tm, d), lambda i, j: (i, j, 0)),
                  pl.BlockSpec((1, 1, d), mod_map),
                  pl.BlockSpec((1, 1, d), mod_map),
                  pl.BlockSpec((1, d), lambda i, j: (0, 0)),
                  pl.BlockSpec((d, IN_DIM), lambda i, j: (0, 0)),
                  pl.BlockSpec((tm, LANES), lambda i, j: (j, 0)),
                  pl.BlockSpec((tm, LANES), lambda i, j: (j, 0))],
        out_specs=pl.BlockSpec((1, tm, IN_DIM), lambda i, j: (i, j, 0)),
        out_shape=jax.ShapeDtypeStruct((b, l, IN_DIM), F32),
        compiler_params=_cparams(("parallel", "parallel")),
        name="in_proj_rope" if rope else "in_proj",
    )(x, shift, scale, nw, w, cos_t, sin_t)


def _out_kernel(x_ref, shift_ref, scale_ref, gate_ref, nw_ref, ya_ref, yb_ref, yc_ref, wm_ref, bm_ref,
                wa_ref, wb_ref, wc_ref, wo_ref, fnw_ref, *out_refs, final):
    x = x_ref[0]
    d = x.shape[1]
    hb = _modulated(x, nw_ref[...], scale_ref[0], shift_ref[0]).astype(BF16)
    merged = None
    for i, (y_ref, w_ref) in enumerate(((ya_ref, wa_ref), (yb_ref, wb_ref), (yc_ref, wc_ref))):
        g = jax.nn.sigmoid(_dot(hb, wm_ref[:, i * d:(i + 1) * d]) + bm_ref[:, i * d:(i + 1) * d])
        term = g * _dot(y_ref[0].astype(BF16), w_ref[...])
        merged = term if merged is None else merged + term
    out = _dot(merged.astype(BF16), wo_ref[...])
    xn = x + gate_ref[0] * out
    out_refs[0][0] = xn
    if final:
        ms = jnp.mean(xn * xn, axis=-1, keepdims=True)
        out_refs[1][0] = xn * lax.rsqrt(ms + EPS) * fnw_ref[...]


def _out_call(x, shift, scale, gate, nw, ya, yb, yc, wm, bm, wa, wb, wc, wo, fnw, *, final, tm):
    b, l, d = x.shape
    per_batch = shift.shape[0] > 1
    mod_map = (lambda i, j: (i, 0, 0)) if per_batch else (lambda i, j: (0, 0, 0))
    tok = lambda w: pl.BlockSpec((1, tm, w), lambda i, j: (i, j, 0))
    full = lambda a: pl.BlockSpec(a.shape, lambda i, j: (0,) * a.ndim)
    n_out = 2 if final else 1
    res = pl.pallas_call(
        functools.partial(_out_kernel, final=final),
        grid=(b, l // tm),
        in_specs=[tok(d), pl.BlockSpec((1, 1, d), mod_map), pl.BlockSpec((1, 1, d), mod_map),
                  pl.BlockSpec((1, 1, d), mod_map), full(nw), tok(W_HY), tok(W_ATT), tok(W_RET),
                  full(wm), full(bm), full(wa), full(wb), full(wc), full(wo), full(fnw)],
        out_specs=[tok(d)] * n_out,
        out_shape=[jax.ShapeDtypeStruct((b, l, d), F32)] * n_out,
        compiler_params=_cparams(("parallel", "parallel")),
        name="merge_out_final" if final else "merge_out",
    )(x, shift, scale, gate, nw, ya, yb, yc, wm, bm, wa, wb, wc, wo, fnw)
    return res


def _sink_softmax_pv(s, sink, vh):
    m = jnp.maximum(jnp.max(s, axis=-1, keepdims=True), sink)
    p = jnp.exp(s - m)
    denom = jnp.sum(p, axis=-1, keepdims=True) + jnp.exp(sink - m)
    return _dot((p / denom).astype(BF16), vh)


_NT = (((1,), (1,)), ((), ()))


def _attn_ctx_kernel(sink_ref, q_ref, k_ref, v_ref, g_ref, o_ref):
    q = q_ref[0]
    k = k_ref[0].astype(BF16)
    v = v_ref[0].astype(BF16)
    for h in range(ATT_HEADS):
        kv = h // ATT_GROUP
        qh = q[:, h * HEAD_DIM:(h + 1) * HEAD_DIM].astype(BF16)
        kh = k[:, kv * HEAD_DIM:(kv + 1) * HEAD_DIM]
        vh = v[:, kv * HEAD_DIM:(kv + 1) * HEAD_DIM]
        s = lax.dot_general(qh, kh, _NT, preferred_element_type=F32) * (HEAD_DIM ** -0.5)
        o = _sink_softmax_pv(s, sink_ref[h], vh)
        o_ref[0, :, h * HEAD_DIM:(h + 1) * HEAD_DIM] = o * g_ref[0, :, h * HEAD_DIM:(h + 1) * HEAD_DIM]


def _attn_ctx_call(u, sink):
    b, l, _ = u.shape
    return pl.pallas_call(
        _attn_ctx_kernel,
        grid=(b,),
        in_specs=[pl.BlockSpec(memory_space=pltpu.SMEM),
                  pl.BlockSpec((1, l, W_ATT), lambda i: (i, 0, COL_QA // W_ATT)),
                  pl.BlockSpec((1, l, W_KV), lambda i: (i, 0, COL_KA // W_KV)),
                  pl.BlockSpec((1, l, W_KV), lambda i: (i, 0, COL_VA // W_KV)),
                  pl.BlockSpec((1, l, W_ATT), lambda i: (i, 0, COL_GA // W_ATT))],
        out_specs=pl.BlockSpec((1, l, W_ATT), lambda i: (i, 0, 0)),
        out_shape=jax.ShapeDtypeStruct((b, l, W_ATT), F32),
        compiler_params=_cparams(("parallel",)),
        name="attn_ctx",
    )(sink, u, u, u, u)


def _attn_lat_kernel(sink_ref, q_ref, kp_ref, kc_ref, kn_ref, vp_ref, vc_ref, vn_ref, kx_ref, vx_ref,
                     g_ref, o_ref):
    i = pl.program_id(1)
    nb = pl.num_programs(1)
    q = q_ref[0]
    kall = jnp.concatenate([kp_ref[0], kc_ref[0], kn_ref[0], kx_ref[0]], axis=0).astype(BF16)
    vall = jnp.concatenate([vp_ref[0], vc_ref[0], vn_ref[0], vx_ref[0]], axis=0).astype(BF16)
    nk = kall.shape[0]
    r = lax.broadcasted_iota(jnp.int32, (ATT_BLOCK, nk), 0)
    c = lax.broadcasted_iota(jnp.int32, (ATT_BLOCK, nk), 1)
    ok_prev = (c < ATT_BLOCK) & (c >= r) & (i > 0)
    ok_cur = (c >= ATT_BLOCK) & (c < 2 * ATT_BLOCK)
    ok_next = (c >= 2 * ATT_BLOCK) & (c < 3 * ATT_BLOCK) & ((c - 2 * ATT_BLOCK) <= r) & (i < nb - 1)
    ok = ok_prev | ok_cur | ok_next | (c >= 3 * ATT_BLOCK)
    for h in range(ATT_HEADS):
        kv = h // ATT_GROUP
        qh = q[:, h * HEAD_DIM:(h + 1) * HEAD_DIM].astype(BF16)
        kh = kall[:, kv * HEAD_DIM:(kv + 1) * HEAD_DIM]
        vh = vall[:, kv * HEAD_DIM:(kv + 1) * HEAD_DIM]
        s = lax.dot_general(qh, kh, _NT, preferred_element_type=F32) * (HEAD_DIM ** -0.5)
        s = jnp.where(ok, s, NEG)
        o = _sink_softmax_pv(s, sink_ref[h], vh)
        o_ref[0, :, h * HEAD_DIM:(h + 1) * HEAD_DIM] = o * g_ref[0, :, h * HEAD_DIM:(h + 1) * HEAD_DIM]


def _attn_lat_call(u, kctx, vctx, sink):
    b, l, _ = u.shape
    nb = l // ATT_BLOCK
    past = kctx.shape[1]
    kcol = COL_KA // W_KV
    vcol = COL_VA // W_KV
    prev = lambda col: pl.BlockSpec((1, ATT_BLOCK, W_KV), lambda i, j: (i, jnp.maximum(j - 1, 0), col))
    cur = lambda col: pl.BlockSpec((1, ATT_BLOCK, W_KV), lambda i, j: (i, j, col))
    nxt = lambda col: pl.BlockSpec((1, ATT_BLOCK, W_KV), lambda i, j: (i, jnp.minimum(j + 1, nb - 1), col))
    ctx = pl.BlockSpec((1, past, W_KV), lambda i, j: (i, 0, 0))
    return pl.pallas_call(
        _attn_lat_kernel,
        grid=(b, nb),
        in_specs=[pl.BlockSpec(memory_space=pltpu.SMEM),
                  pl.BlockSpec((1, ATT_BLOCK, W_ATT), lambda i, j: (i, j, COL_QA // W_ATT)),
                  prev(kcol), cur(kcol), nxt(kcol), prev(vcol), cur(vcol), nxt(vcol), ctx, ctx,
                  pl.BlockSpec((1, ATT_BLOCK, W_ATT), lambda i, j: (i, j, COL_GA // W_ATT))],
        out_specs=pl.BlockSpec((1, ATT_BLOCK, W_ATT), lambda i, j: (i, j, 0)),
        out_shape=jax.ShapeDtypeStruct((b, l, W_ATT), F32),
        compiler_params=_cparams(("parallel", "parallel")),
        name="attn_lat",
    )(sink, u, u, u, u, u, u, u, kctx, vctx, u)


def _log_sigmoid(x):
    return jnp.minimum(x, 0.0) - jnp.log1p(jnp.exp(-jnp.abs(x)))


def _ret_kernel(q_ref, k_ref, v_ref, g_ref, thl_ref, thb_ref, gn_ref, s0_ref, o_ref, sfin_ref, *, nc,
                has_s0):
    hp = pl.program_id(1)
    c = RET_CHUNK
    lane = lax.broadcasted_iota(jnp.int32, (1, LANES), 1)
    lo_head = lane < RET_DIM
    head_masks = (lo_head, jnp.logical_not(lo_head))
    rowf = lax.broadcasted_iota(jnp.int32, (c, LANES), 0).astype(F32)
    ii = lax.broadcasted_iota(jnp.int32, (c, c), 0)
    jj = lax.broadcasted_iota(jnp.int32, (c, c), 1)
    dd = lax.broadcasted_iota(jnp.int32, (LANES, LANES), 0)
    ee = lax.broadcasted_iota(jnp.int32, (LANES, LANES), 1)
    same_head = (dd < RET_DIM) == (ee < RET_DIM)
    gn = gn_ref[...]

    def tables(direction):
        lg_lane = _log_sigmoid(thl_ref[direction])
        dmats = []
        for hh in range(2):
            lg_h = _log_sigmoid(thb_ref[direction, pl.ds(2 * hp + hh, 1), :])
            if direction == 0:
                dist = ii - jj
            else:
                dist = jj - ii
            dmats.append(jnp.where(dist >= 0, jnp.exp(lg_h * jnp.maximum(dist, 0).astype(F32)), 0.0))
        if direction == 0:
            q_dec = jnp.exp(lg_lane * (rowf + 1.0))
            k_dec = jnp.exp(lg_lane * (c - 1.0 - rowf))
        else:
            q_dec = jnp.exp(lg_lane * (c - rowf))
            k_dec = jnp.exp(lg_lane * rowf)
        c_dec = jnp.exp(lg_lane * float(c))
        return dmats, q_dec, k_dec, c_dec

    def chunk(direction, tabs, r0, s):
        dmats, q_dec, k_dec, c_dec = tabs
        qc = q_ref[0, pl.ds(r0, c), :]
        kc = k_ref[0, pl.ds(r0, c), :]
        vc = v_ref[0, pl.ds(r0, c), :]
        kcb = kc.astype(BF16)
        vcb = vc.astype(BF16)
        o = _dot((qc * q_dec).astype(BF16), s.astype(BF16))
        for hh in range(2):
            qm = jnp.where(head_masks[hh], qc, 0.0).astype(BF16)
            sc = lax.dot_general(qm, kcb, _NT, preferred_element_type=F32) * dmats[hh]
            pv = _dot(sc.astype(BF16), vcb)
            o = o + jnp.where(head_masks[hh], pv, 0.0)
        kd_t = (kc * k_dec).T.astype(BF16)
        s_new = c_dec * s + jnp.where(same_head, _dot(kd_t, vcb), 0.0)
        return o, s_new

    def init_state(direction):
        if has_s0:
            return s0_ref[0, direction, 0]
        return jnp.zeros((LANES, LANES), F32)

    tabs_f = tables(0)

    def fwd_body(n, s):
        r0 = pl.multiple_of(n * c, c)
        o, s_new = chunk(0, tabs_f, r0, s)
        o_ref[0, pl.ds(r0, c), :] = o
        return s_new

    s_f = lax.fori_loop(0, nc, fwd_body, init_state(0))
    sfin_ref[0, 0, 0] = s_f

    tabs_b = tables(1)

    def bwd_body(m, s):
        r0 = pl.multiple_of((nc - 1 - m) * c, c)
        o_b, s_new = chunk(1, tabs_b, r0, s)
        o = o_ref[0, pl.ds(r0, c), :] + o_b
        o2 = o * o
        s_lo = jnp.sum(jnp.where(lo_head, o2, 0.0), axis=-1, keepdims=True)
        s_hi = jnp.sum(jnp.where(lo_head, 0.0, o2), axis=-1, keepdims=True)
        ms = jnp.where(lo_head, s_lo, s_hi) * (1.0 / RET_DIM)
        o_ref[0, pl.ds(r0, c), :] = o * lax.rsqrt(ms + EPS) * gn * g_ref[0, pl.ds(r0, c), :]
        return s_new

    s_b = lax.fori_loop(0, nc, bwd_body, init_state(1))
    sfin_ref[0, 1, 0] = s_b


def _ret_call(u, theta, gn, s0bd):
    b, l, _ = u.shape
    nc = l // RET_CHUNK
    npair = RET_HEADS // 2
    has_s0 = s0bd is not None
    if not has_s0:
        s0bd = jnp.zeros((1, 2, 1, LANES, LANES), F32)
        s0_spec = pl.BlockSpec((1, 2, 1, LANES, LANES), lambda i, j: (0, 0, 0, 0, 0))
    else:
        s0_spec = pl.BlockSpec((1, 2, 1, LANES, LANES), lambda i, j: (i, 0, j, 0, 0))
    th_lane = jnp.repeat(theta, RET_DIM, axis=1).reshape(2, 1, W_RET)
    th_bcast = jnp.broadcast_to(theta[:, :, None], (2, RET_HEADS, LANES))
    col = lambda c0: pl.BlockSpec((1, l, LANES), lambda i, j: (i, 0, c0 // LANES + j))
    o, sfin = pl.pallas_call(
        functools.partial(_ret_kernel, nc=nc, has_s0=has_s0),
        grid=(b, npair),
        in_specs=[col(COL_QR), col(COL_KR), col(COL_VR), col(COL_GR),
                  pl.BlockSpec((2, 1, LANES), lambda i, j: (0, 0, j)),
                  pl.BlockSpec((2, RET_HEADS, LANES), lambda i, j: (0, 0, 0)),
                  pl.BlockSpec((1, LANES), lambda i, j: (0, j)),
                  s0_spec],
        out_specs=[pl.BlockSpec((1, l, LANES), lambda i, j: (i, 0, j)),
                   pl.BlockSpec((1, 2, 1, LANES, LANES), lambda i, j: (i, 0, j, 0, 0))],
        out_shape=[jax.ShapeDtypeStruct((b, l, W_RET), F32),
                   jax.ShapeDtypeStruct((b, 2, npair, LANES, LANES), F32)],
        compiler_params=_cparams(("parallel", "parallel")),
        name="retention_s0" if has_s0 else "retention",
    )(u, u, u, u, th_lane, th_bcast, gn.reshape(1, W_RET), s0bd)
    return o, sfin


def _blockdiag_states(s):
    b = s.shape[0]
    sp = s.reshape(b, 2, RET_HEADS // 2, 2, RET_DIM, RET_DIM)
    z = jnp.zeros_like(sp[:, :, :, 0])
    top = jnp.concatenate([sp[:, :, :, 0], z], axis=-1)
    bot = jnp.concatenate([z, sp[:, :, :, 1]], axis=-1)
    return jnp.concatenate([top, bot], axis=-2)


def _diag_states(sbd):
    b = sbd.shape[0]
    s0 = sbd[:, :, :, :RET_DIM, :RET_DIM]
    s1 = sbd[:, :, :, RET_DIM:, RET_DIM:]
    return jnp.stack([s0, s1], axis=3).reshape(b, 2, RET_HEADS, RET_DIM, RET_DIM)


def _filter_positions(l):
    n = np.arange(2 * l)
    pos = np.where(n < l, n, 2 * l - n)
    pos = np.where(n == l, 0, pos)
    t = jnp.linspace(0.0, 1.0, l, dtype=F32)[:, None]
    w = 2.0 * math.pi * jnp.arange(l, dtype=F32)[:, None] / l
    f = jnp.linspace(1e-4, HY_BANDS - 1, HY_BANDS, dtype=F32)[None, :]
    z = jnp.concatenate([t, jnp.cos(f * w), -jnp.sin(f * w)], axis=-1)
    z = jnp.pad(z, ((0, 0), (0, 32 - HY_POS_FEAT)))
    return z[pos]


def _hyena_deltas():
    max_decay = math.log(HY_DECAY_TARGET) / HY_FAST_PCT
    min_decay = math.log(HY_DECAY_TARGET) / HY_SLOW_PCT
    return jnp.abs(jnp.linspace(min_decay, max_decay, W_HY, dtype=F32))[None, :]


def _filter_hidden(z_ref, w1_ref, b1_ref, fr_ref):
    pre = jnp.dot(z_ref[...], w1_ref[0], precision=HIGHEST, preferred_element_type=F32) + b1_ref[0]
    return jnp.sin(fr_ref[0] * pre)


def _filter_raw(hid, w2f, w2b, tp, dl, row0, l):
    win = jnp.exp(-tp * dl)
    row = row0 + lax.broadcasted_iota(jnp.int32, win.shape, 0)
    hf = jnp.dot(hid, w2f, precision=HIGHEST, preferred_element_type=F32) * win
    hb = jnp.dot(hid, w2b, precision=HIGHEST, preferred_element_type=F32) * win
    hf = jnp.where(row < l, hf, 0.0)
    hb = jnp.where((row > l) | (row == 0), hb, 0.0)
    return hf + hb, jnp.sum(jnp.abs(hf) + jnp.abs(hb), axis=0, keepdims=True)


def _with_skip(g, skip):
    row = lax.broadcasted_iota(jnp.int32, g.shape, 0)
    return g + jnp.where(row == 0, skip, 0.0)


def _filt_ctx_kernel(z_ref, w1_ref, b1_ref, fr_ref, w2_ref, dl_ref, sk_ref, fh_ref, fl_ref, g_ref):
    hid = _filter_hidden(z_ref, w1_ref, b1_ref, fr_ref)
    tp = z_ref[:, 0:1]
    for o in range(2):
        w2f = w2_ref[0, :, (2 * o) * W_HY:(2 * o + 1) * W_HY]
        w2b = w2_ref[0, :, (2 * o + 1) * W_HY:(2 * o + 2) * W_HY]
        raw, nrm = _filter_raw(hid, w2f, w2b, tp, dl_ref[...], 0, z_ref.shape[0] // 2)
        g = _with_skip(raw / nrm, sk_ref[0, pl.ds(o, 1), :])
        g_ref[0, o] = _dot3c(fh_ref[...], fl_ref[...], g)


def _ctx_dft_tables(l):
    n = 2 * l
    k = np.arange(n)[:, None]
    t = np.arange(l)[None, :]
    ang = 2.0 * np.pi * k * t / n
    c, s = np.cos(ang), np.sin(ang)
    fwd = np.block([[c, s], [-s, c]])
    inv = np.block([[c.T, -s.T], [s.T, c.T]])
    n_all = np.arange(n)[None, :]
    angg = 2.0 * np.pi * k * n_all / n
    filt = np.concatenate([np.cos(angg), -np.sin(angg)], axis=0) / n
    return _split_np(fwd), _split_np(inv), _split_np(filt)


def _filt_ctx_call(l, w1, b1, freq, w2, skip):
    n = 2 * l
    z_ext = _filter_positions(l)
    _, _, (fh, fl) = _ctx_dft_tables(l)
    w1p = jnp.pad(w1, ((0, 0), (0, 32 - HY_POS_FEAT), (0, 0)))
    lay = lambda *shape: pl.BlockSpec((1,) + shape, lambda d: (d,) + (0,) * len(shape))
    full = lambda a: pl.BlockSpec(a.shape, lambda d: (0,) * a.ndim)
    dl = _hyena_deltas()
    return pl.pallas_call(
        _filt_ctx_kernel,
        grid=(DEPTH,),
        in_specs=[full(z_ext), lay(32, HY_FILT_HID), lay(1, HY_FILT_HID), lay(1, HY_FILT_HID),
                  lay(HY_FILT_HID, 4 * W_HY), full(dl), lay(2, W_HY), full(fh), full(fl)],
        out_specs=pl.BlockSpec((1, 2, 2 * n, W_HY), lambda d: (d, 0, 0, 0)),
        out_shape=jax.ShapeDtypeStruct((DEPTH, 2, 2 * n, W_HY), F32),
        compiler_params=_cparams(("arbitrary",)),
        name="hyena_filter_ctx",
    )(z_ext, w1p, b1.reshape(DEPTH, 1, -1), freq.reshape(DEPTH, 1, -1), w2, dl, skip, fh, fl)


def _lat_dft_tables():
    ka = np.arange(NA)[:, None]
    b = np.arange(NB)[:, None, None]
    kb = np.arange(NB)[:, None]
    bb = np.arange(NB)[None, :]
    a_half = np.arange(NA // 2)[None, :]
    a_full = np.arange(NA)[None, :]
    phi = 2.0 * np.pi * (ka * a_half / NA + b * ka / LAT_N)
    c, s = np.cos(phi), np.sin(phi)
    a_fwd = np.concatenate([c, s], axis=2)
    ct, st = np.swapaxes(c, 1, 2), np.swapaxes(s, 1, 2)
    a_inv = np.concatenate([np.concatenate([ct, -st], axis=2), np.concatenate([st, ct], axis=2)], axis=1)
    phig = 2.0 * np.pi * (ka * a_full / NA + b * ka / LAT_N)
    a_flt = np.concatenate([np.cos(phig), -np.sin(phig)], axis=1) / LAT_N
    ang = 2.0 * np.pi * kb * bb / NB
    c2, s2 = np.cos(ang), np.sin(ang)
    f_fwd = np.block([[c2, s2], [-s2, c2]])
    f_inv = np.block([[c2, -s2], [s2, c2]])
    return (_split_np(a_fwd), _split_np(a_inv), _split_np(a_flt), _split_np(f_fwd), _split_np(f_inv))


def _stage_b_rows(ka):
    re = pl.ds(ka, NB, stride=Y_PITCH)
    im = pl.ds(NA + ka, NB, stride=Y_PITCH)
    return re, im


def _filt_lat_kernel(z_ref, w1_ref, b1_ref, fr_ref, w2f_ref, w2b_ref, dl_ref, sk_ref, ah_ref, al_ref,
                     f2h_ref, f2l_ref, g_ref, hid_ref, gt_ref, y_ref):
    step = pl.program_id(1)
    rch = 1024
    nch = LAT_N // rch
    rows_of = lambda i: pl.ds(pl.multiple_of(i * rch, rch), rch)

    @pl.when(step == 0)
    def _():
        def hid_chunk(i, carry):
            r = rows_of(i)
            pre = jnp.dot(z_ref[r, :], w1_ref[0], precision=HIGHEST, preferred_element_type=F32)
            hid_ref[r, :] = jnp.sin(fr_ref[0] * (pre + b1_ref[0]))
            return carry

        lax.fori_loop(0, nch, hid_chunk, 0)

    def raw_chunk(i, nrm):
        r = rows_of(i)
        raw, part = _filter_raw(hid_ref[r, :], w2f_ref[0], w2b_ref[0], z_ref[r, 0:1], dl_ref[...],
                                i * rch, LAT_L)
        gt_ref[r, :] = raw
        return nrm + part

    nrm = lax.fori_loop(0, nch, raw_chunk, jnp.zeros((1, LANES), F32))

    def norm_chunk(i, carry):
        r = rows_of(i)
        gt_ref[r, :] = gt_ref[r, :] / nrm
        return carry

    lax.fori_loop(0, nch, norm_chunk, 0)
    order = step // (W_HY // LANES)
    gt_ref[0:8, :] = _with_skip(gt_ref[0:8, :], sk_ref[0, pl.ds(order, 1), :])

    def stage_a(b, carry):
        rows = gt_ref[pl.ds(b, NA, stride=NB), :]
        y_ref[pl.ds(pl.multiple_of(b * Y_PITCH, 8), 2 * NA), :] = _dot3c(ah_ref[b], al_ref[b], rows)
        return carry

    lax.fori_loop(0, NB, stage_a, 0)

    def stage_b(ka, carry):
        re, im = _stage_b_rows(ka)
        z = jnp.concatenate([y_ref[re, :], y_ref[im, :]], axis=0)
        g_ref[0, 0, pl.ds(pl.multiple_of(ka * 2 * NB, 2 * NB), 2 * NB), :] = _dot3c(
            f2h_ref[...], f2l_ref[...], z)
        return carry

    lax.fori_loop(0, NA, stage_b, 0)


def _filt_lat_call(w1, b1, freq, w2, skip):
    z_ext = _filter_positions(LAT_L)
    _, _, (ah, al), (f2h, f2l), _ = _lat_dft_tables()
    w1p = jnp.pad(w1, ((0, 0), (0, 32 - HY_POS_FEAT), (0, 0)))
    nct = W_HY // LANES
    one = pl.Buffered(1)
    lay = lambda *shape: pl.BlockSpec((1,) + shape, lambda d, s: (d,) + (0,) * len(shape))
    full = lambda a: pl.BlockSpec(a.shape, lambda d, s: (0,) * a.ndim, pipeline_mode=one)
    dl = _hyena_deltas()
    return pl.pallas_call(
        _filt_lat_kernel,
        grid=(DEPTH, 2 * nct),
        in_specs=[full(z_ext), lay(32, HY_FILT_HID), lay(1, HY_FILT_HID), lay(1, HY_FILT_HID),
                  pl.BlockSpec((1, HY_FILT_HID, LANES), lambda d, s: (d, 0, (s // nct) * 2 * nct + s % nct)),
                  pl.BlockSpec((1, HY_FILT_HID, LANES),
                               lambda d, s: (d, 0, (s // nct) * 2 * nct + nct + s % nct)),
                  pl.BlockSpec((1, LANES), lambda d, s: (0, s % nct)),
                  pl.BlockSpec((1, 2, LANES), lambda d, s: (d, 0, s % nct)),
                  full(ah), full(al), full(f2h), full(f2l)],
        out_specs=pl.BlockSpec((1, 1, NA * 2 * NB, LANES), lambda d, s: (d, s // nct, 0, s % nct)),
        out_shape=jax.ShapeDtypeStruct((DEPTH, 2, NA * 2 * NB, W_HY), F32),
        scratch_shapes=[pltpu.VMEM((LAT_N, HY_FILT_HID), F32), pltpu.VMEM((LAT_N, LANES), F32),
                        pltpu.VMEM((NB * Y_PITCH, LANES), F32)],
        compiler_params=_cparams(("arbitrary", "arbitrary")),
        name="hyena_filter_lat",
    )(z_ext, w1p, b1.reshape(DEPTH, 1, -1), freq.reshape(DEPTH, 1, -1), w2, w2, dl, skip, ah, al, f2h,
      f2l)


def _short_conv_rows(ref, bi, r0, rows, first, last, w):
    total = ref.shape[1]
    cur = ref[bi, pl.ds(r0, rows), :]
    before = ref[bi, pl.ds(jnp.maximum(r0 - 1, 0), 1), :]
    after = ref[bi, pl.ds(jnp.minimum(r0 + rows, total - 1), 1), :]
    before = jnp.where(first, 0.0, before)
    after = jnp.where(last, 0.0, after)
    rid = lax.broadcasted_iota(jnp.int32, cur.shape, 0)
    prev = jnp.where(rid == 0, before, pltpu.roll(cur, 1, 0))
    nxt = jnp.where(rid == rows - 1, after, pltpu.roll(cur, rows - 1, 0))
    return prev * w[0:1] + cur * w[1:2] + nxt * w[2:3]


def _cmul(xr, xi, gr, gi):
    return xr * gr - xi * gi, xr * gi + xi * gr


def _hy_ctx_kernel(v_ref, x1_ref, x2_ref, gt_ref, cw_ref, g_ref, fh_ref, fl_ref, ih_ref, il_ref, o_ref):
    l = v_ref.shape[1]
    n = 2 * l

    def sc(ref, bi, grp):
        w = cw_ref[:, grp * W_HY:(grp + 1) * W_HY]
        return _short_conv_rows(ref, bi, 0, l, True, True, w)

    def conv(zr, zi, order):
        x = _dot3c(fh_ref[...], fl_ref[...], jnp.concatenate([zr, zi], axis=0))
        pr, pi = _cmul(x[:n], x[n:], g_ref[order, :n], g_ref[order, n:])
        y = _dot3c(ih_ref[...], il_ref[...], jnp.concatenate([pr, pi], axis=0))
        return y[:l], y[l:]

    yr, yi = conv(sc(v_ref, 0, 0), sc(v_ref, 1, 0), 0)
    yr, yi = conv(sc(x1_ref, 0, 1) * yr, sc(x1_ref, 1, 1) * yi, 1)
    o_ref[0] = sc(x2_ref, 0, 2) * yr * _silu(gt_ref[0])
    o_ref[1] = sc(x2_ref, 1, 2) * yi * _silu(gt_ref[1])


def _hy_ctx_call(u, conv_w, g_spec):
    b, l, _ = u.shape
    (fh, fl), (ih, il), _ = _ctx_dft_tables(l)
    grp = lambda g: pl.BlockSpec((2, l, W_HY), lambda i: (i, 0, g))
    full = lambda a: pl.BlockSpec(a.shape, lambda i: (0,) * a.ndim)
    return pl.pallas_call(
        _hy_ctx_kernel,
        grid=(b // 2,),
        in_specs=[grp(0), grp(1), grp(2), grp(3), full(conv_w), full(g_spec), full(fh), full(fl),
                  full(ih), full(il)],
        out_specs=pl.BlockSpec((2, l, W_HY), lambda i: (i, 0, 0)),
        out_shape=jax.ShapeDtypeStruct((b, l, W_HY), F32),
        compiler_params=_cparams(("parallel",)),
        name="hyena_ctx",
    )(u, u, u, u, conv_w, g_spec, fh, fl, ih, il)


def _hy_lat_kernel(z_ref, m_ref, gt_ref, cw_ref, g_ref, afh_ref, afl_ref, aih_ref, ail_ref, f2h_ref,
                   f2l_ref, f3h_ref, f3l_ref, o_ref, xr_scr, xi_scr, y_scr, *, conv_in, gated):
    x_scr = (xr_scr, xi_scr)
    na_half = NA // 2
    w_in = cw_ref[0] if conv_in else None
    w_mul = cw_ref[1]

    def load_in(a, carry):
        for bi in range(2):
            r0 = pl.multiple_of(a * NB, NB)
            if conv_in:
                val = _short_conv_rows(z_ref, bi, r0, NB, a == 0, a == na_half - 1, w_in)
            else:
                val = z_ref[bi, pl.ds(r0, NB), :]
            x_scr[bi][pl.ds(pl.multiple_of(a * X_PITCH, 8), NB), :] = val
        return carry

    lax.fori_loop(0, na_half, load_in, 0)

    def stage_a(b, carry):
        zr = xr_scr[pl.ds(b, na_half, stride=X_PITCH), :]
        zi = xi_scr[pl.ds(b, na_half, stride=X_PITCH), :]
        rhs = jnp.concatenate([jnp.concatenate([zr, zi], axis=0), jnp.concatenate([zi, -zr], axis=0)],
                              axis=1)
        y = _dot3c(afh_ref[b], afl_ref[b], rhs)
        r0 = pl.multiple_of(b * Y_PITCH, 8)
        y_scr[pl.ds(r0, NA), :] = y[:, :LANES]
        y_scr[pl.ds(r0 + NA, NA), :] = y[:, LANES:]
        return carry

    lax.fori_loop(0, NB, stage_a, 0)

    def stage_b(ka, carry):
        re, im = _stage_b_rows(ka)
        x = _dot3c(f2h_ref[...], f2l_ref[...], jnp.concatenate([y_scr[re, :], y_scr[im, :]], axis=0))
        g0 = pl.multiple_of(ka * 2 * NB, 2 * NB)
        pr, pi = _cmul(x[:NB], x[NB:], g_ref[pl.ds(g0, NB), :], g_ref[pl.ds(g0 + NB, NB), :])
        u = _dot3c(f3h_ref[...], f3l_ref[...], jnp.concatenate([pr, pi], axis=0))
        y_scr[re, :] = u[:NB]
        y_scr[im, :] = u[NB:]
        return carry

    lax.fori_loop(0, NA, stage_b, 0)

    def stage_c(b, carry):
        ub = y_scr[pl.ds(pl.multiple_of(b * Y_PITCH, 8), 2 * NA), :]
        y = _dot3c(aih_ref[b], ail_ref[b], ub)
        xr_scr[pl.ds(b, na_half, stride=X_PITCH), :] = y[:na_half]
        xi_scr[pl.ds(b, na_half, stride=X_PITCH), :] = y[na_half:]
        return carry

    lax.fori_loop(0, NB, stage_c, 0)

    def store_out(a, carry):
        for bi in range(2):
            r0 = pl.multiple_of(a * NB, NB)
            mul = _short_conv_rows(m_ref, bi, r0, NB, a == 0, a == na_half - 1, w_mul)
            val = x_scr[bi][pl.ds(pl.multiple_of(a * X_PITCH, 8), NB), :] * mul
            if gated:
                val = val * _silu(gt_ref[bi, pl.ds(r0, NB), :])
            o_ref[bi, pl.ds(r0, NB), :] = val
        return carry

    lax.fori_loop(0, na_half, store_out, 0)


def _hy_lat_call(src, src_col, u, mul_col, gate_col, conv_w2, g_spec, *, conv_in, gated):
    b, l, _ = u.shape
    nct = W_HY // LANES
    (afh, afl), (aih, ail), _, (f2h, f2l), (f3h, f3l) = _lat_dft_tables()
    one = pl.Buffered(1)
    blk = lambda col: pl.BlockSpec((2, l, LANES), lambda c, p: (p, 0, col + c), pipeline_mode=one)
    const = lambda a: pl.BlockSpec(a.shape, lambda c, p: (0,) * a.ndim, pipeline_mode=one)
    gate_spec = blk(gate_col) if gated else pl.BlockSpec((2, 8, LANES), lambda c, p: (0, 0, 0))
    return pl.pallas_call(
        functools.partial(_hy_lat_kernel, conv_in=conv_in, gated=gated),
        grid=(nct, b // 2),
        in_specs=[blk(src_col), blk(mul_col), gate_spec,
                  pl.BlockSpec((2, 3, LANES), lambda c, p: (0, 0, c)),
                  pl.BlockSpec((NA * 2 * NB, LANES), lambda c, p: (0, c), pipeline_mode=one),
                  const(afh), const(afl), const(aih), const(ail), const(f2h), const(f2l), const(f3h),
                  const(f3l)],
        out_specs=pl.BlockSpec((2, l, LANES), lambda c, p: (p, 0, c)),
        out_shape=jax.ShapeDtypeStruct((b, l, W_HY), F32),
        scratch_shapes=[pltpu.VMEM(((NA // 2) * X_PITCH, LANES), F32),
                        pltpu.VMEM(((NA // 2) * X_PITCH, LANES), F32),
                        pltpu.VMEM((NB * Y_PITCH, LANES), F32)],
        compiler_params=_cparams(("arbitrary", "arbitrary")),
        name="hyena_lat_o2" if gated else "hyena_lat_o1",
    )(src, u, u, conv_w2, g_spec, afh, afl, aih, ail, f2h, f2l, f3h, f3l)


def _rope_tables(l):
    rows = l // GRID_W
    row = jnp.repeat(jnp.arange(rows), GRID_W).astype(F32)
    col = jnp.tile(jnp.arange(GRID_W), rows).astype(F32)
    quarter = HEAD_DIM // 4
    inv = ROPE_BASE ** (-jnp.arange(quarter, dtype=F32) / quarter)
    ang = jnp.concatenate([row[:, None] * inv, col[:, None] * inv], axis=-1)
    cos, sin = jnp.cos(ang), jnp.sin(ang)
    q = quarter
    cos_h = jnp.concatenate([cos[:, :q], cos[:, :q], cos[:, q:], cos[:, q:]], axis=-1)
    sin_h = jnp.concatenate([-sin[:, :q], sin[:, :q], -sin[:, q:], sin[:, q:]], axis=-1)
    return jnp.tile(cos_h, (1, 2)), jnp.tile(sin_h, (1, 2))


def kernel(x_prompt, x_sample, c, cache_k, cache_v, state_ret, c_ctx, norm_w, w_mod, b_mod, w_in, hy_conv,
           hy_filt_w1, hy_filt_b1, hy_filt_freq, hy_filt_w2, hy_skip, attn_sink, ret_theta, ret_gn,
           w_branch_a, w_branch_b, w_branch_c, w_merge, b_merge, w_out, final_norm_w):
    d = D_MODEL
    bc, lc, _ = x_prompt.shape
    bl, ll, _ = x_sample.shape
    assert ll == LAT_L and bc % 2 == 0 and bl % 2 == 0
    past = cache_k.shape[2]

    cond = jnp.zeros((16, d), F32).at[:bl].set(c).at[bl].set(c_ctx)
    mod = _mod_call(cond, w_mod, b_mod)

    g_ctx = _filt_ctx_call(lc, hy_filt_w1, hy_filt_b1, hy_filt_freq, hy_filt_w2, hy_skip)
    g_lat = _filt_lat_call(hy_filt_w1, hy_filt_b1, hy_filt_freq, hy_filt_w2, hy_skip)

    cos_t, sin_t = _rope_tables(ll)
    w_in_b = w_in[:, :, _IN_PERM].astype(BF16)
    wm_b = w_merge.astype(BF16)
    wa_b = w_branch_a.astype(BF16)
    wb_b = w_branch_b.astype(BF16)
    wc_b = w_branch_c.astype(BF16)
    wo_b = w_out.astype(BF16)
    fnw = final_norm_w.reshape(1, d)
    hy_cols = COL_HY // LANES
    nct = W_HY // LANES

    xp, xs = x_prompt, x_sample
    ks_out, vs_out, ss_out = [], [], []
    for l in range(DEPTH):
        final = l == DEPTH - 1
        nw = norm_w[l].reshape(1, d)
        bm = b_merge[l].reshape(1, -1)
        shift, scale, gate = (mod[l, :, i * d:(i + 1) * d][:, None, :] for i in range(3))
        conv_w = hy_conv[l]
        cw = lambda g: conv_w[:, g * W_HY:(g + 1) * W_HY]

        sl = slice(bl, bl + 1)
        u = _in_call(xp, shift[sl], scale[sl], nw, w_in_b[l], cos_t, sin_t, rope=False, tm=lc)
        ya = _hy_ctx_call(u, conv_w, g_ctx[l])
        yb = _attn_ctx_call(u, attn_sink[l])
        yc, sfin = _ret_call(u, ret_theta[l], ret_gn[l], None)
        res = _out_call(xp, shift[sl], scale[sl], gate[sl], nw, ya, yb, yc, wm_b[l], bm, wa_b[l], wb_b[l],
                        wc_b[l], wo_b[l], fnw, final=final, tm=lc)
        xp = res[0]
        if final:
            y_prompt = res[1]
        ks_out.append(u[:, :, COL_KA:COL_KA + W_KV].reshape(bc, lc, ATT_KV_HEADS, HEAD_DIM))
        vs_out.append(u[:, :, COL_VA:COL_VA + W_KV].reshape(bc, lc, ATT_KV_HEADS, HEAD_DIM))
        ss_out.append(_diag_states(sfin))

        sl = slice(0, bl)
        u = _in_call(xs, shift[sl], scale[sl], nw, w_in_b[l], cos_t, sin_t, rope=True, tm=256)
        z1 = _hy_lat_call(u, hy_cols, u, hy_cols + nct, hy_cols + 3 * nct,
                          jnp.stack([cw(0), cw(1)]), g_lat[l, 0], conv_in=True, gated=False)
        ya = _hy_lat_call(z1, 0, u, hy_cols + 2 * nct, hy_cols + 3 * nct,
                          jnp.stack([cw(2), cw(2)]), g_lat[l, 1], conv_in=False, gated=True)
        yb = _attn_lat_call(u, cache_k[:, l].reshape(bl, past, W_KV), cache_v[:, l].reshape(bl, past, W_KV),
                            attn_sink[l])
        yc, _ = _ret_call(u, ret_theta[l], ret_gn[l], _blockdiag_states(state_ret[:, l]))
        res = _out_call(xs, shift[sl], scale[sl], gate[sl], nw, ya, yb, yc, wm_b[l], bm, wa_b[l], wb_b[l],
                        wc_b[l], wo_b[l], fnw, final=final, tm=256)
        xs = res[0]
        if final:
            y_sample = res[1]

    new_cache_k = jnp.stack(ks_out, axis=1)
    new_cache_v = jnp.stack(vs_out, axis=1)
    new_state_ret = jnp.stack(ss_out, axis=1)
    return (y_prompt, y_sample, new_cache_k, new_cache_v, new_state_ret)
```

```python
import functools
import math

import numpy as np
import jax
import jax.numpy as jnp
from jax import lax
from jax.experimental import pallas as pl
from jax.experimental.pallas import tpu as pltpu

F32 = jnp.float32
BF16 = jnp.bfloat16
HIGHEST = lax.Precision.HIGHEST

D_MODEL = 1024
DEPTH = 4
GRID_W = 64
W_HY = 512
HY_BANDS = 8
HY_POS_FEAT = 1 + 2 * HY_BANDS
HY_FILT_HID = 64
HY_DECAY_TARGET = 1e-2
HY_FAST_PCT = 0.3
HY_SLOW_PCT = 1.5
ATT_HEADS = 8
ATT_KV_HEADS = 2
ATT_GROUP = ATT_HEADS // ATT_KV_HEADS
HEAD_DIM = 64
W_ATT = ATT_HEADS * HEAD_DIM
W_KV = ATT_KV_HEADS * HEAD_DIM
ATT_BLOCK = 128
RET_HEADS = 8
RET_DIM = 64
W_RET = RET_HEADS * RET_DIM
RET_CHUNK = 128
ROPE_BASE = 10000.0
EPS = 1e-6
NEG = -1e30

LANES = 128
VMEM_LIMIT = 56 * 1024 * 1024

IN_DIM = 5376
COL_HY = 0
COL_QA = 2048
COL_GA = 2560
COL_QR = 3072
COL_KR = 3584
COL_VR = 4096
COL_GR = 4608
COL_KA = 5120
COL_VA = 5248
_IN_PERM = np.concatenate([
    np.arange(0, 2048), np.arange(2048, 2560), np.arange(2816, 3328), np.arange(3328, 3840),
    np.arange(3840, 4352), np.arange(4352, 4864), np.arange(4864, 5376), np.arange(2560, 2688),
    np.arange(2688, 2816)])

LAT_L = 4096
LAT_N = 2 * LAT_L
NA = 64
NB = 128
Y_PITCH = 136
X_PITCH = 136


def _cparams(sem):
    return pltpu.CompilerParams(dimension_semantics=sem, vmem_limit_bytes=VMEM_LIMIT)


def _split_np(a):
    a32 = np.asarray(a, np.float32)
    hi = a32.astype(BF16)
    lo = (a32 - hi.astype(np.float32)).astype(BF16)
    return jnp.asarray(hi), jnp.asarray(lo)


def _split(x):
    hi = x.astype(BF16)
    lo = (x - hi.astype(F32)).astype(BF16)
    return hi, lo


def _dot(a, b):
    return jnp.dot(a, b, preferred_element_type=F32)


def _dot3c(chi, clo, x):
    xh, xl = _split(x)
    return _dot(chi, xh) + _dot(clo, xh) + _dot(chi, xl)


def _silu(x):
    return x * jax.nn.sigmoid(x)


def _mod_kernel(c_ref, w_ref, b_ref, o_ref):
    s = _silu(c_ref[...])
    o_ref[0] = jnp.dot(s, w_ref[0], precision=HIGHEST, preferred_element_type=F32) + b_ref[0]


def _mod_call(cond, w_mod, b_mod):
    rows, d = cond.shape
    n = w_mod.shape[-1]
    tn = 1024
    return pl.pallas_call(
        _mod_kernel,
        grid=(DEPTH, n // tn),
        in_specs=[pl.BlockSpec((rows, d), lambda l, j: (0, 0)),
                  pl.BlockSpec((1, d, tn), lambda l, j: (l, 0, j)),
                  pl.BlockSpec((1, 1, tn), lambda l, j: (l, 0, j))],
        out_specs=pl.BlockSpec((1, rows, tn), lambda l, j: (l, 0, j)),
        out_shape=jax.ShapeDtypeStruct((DEPTH, rows, n), F32),
        compiler_params=_cparams(("arbitrary", "arbitrary")),
        name="adaln_mod",
    )(cond, w_mod, b_mod.reshape(DEPTH, 1, n))


def _modulated(x, nw, scale, shift):
    ms = jnp.mean(x * x, axis=-1, keepdims=True)
    h = x * lax.rsqrt(ms + EPS) * nw
    return h * (1.0 + scale) + shift


def _rope128(x, cos, sin_signed, first_half):
    up = pltpu.roll(x, LANES - 16, 1)
    dn = pltpu.roll(x, 16, 1)
    return x * cos + jnp.where(first_half, up, dn) * sin_signed


def _in_kernel(x_ref, shift_ref, scale_ref, nw_ref, w_ref, cos_ref, sin_ref, o_ref, *, rope):
    x = x_ref[0]
    tm = x.shape[0]
    hb = _modulated(x, nw_ref[...], scale_ref[0], shift_ref[0]).astype(BF16)
    if rope:
        cos = cos_ref[...]
        sin = sin_ref[...]
        lane = lax.broadcasted_iota(jnp.int32, (tm, LANES), 1)
        first_half = (lane % 32) < 16

    def seg(c0, width):
        return _dot(hb, w_ref[:, c0:c0 + width])

    def put_rope(c0, val, mul):
        for i in range(val.shape[1] // LANES):
            piece = val[:, i * LANES:(i + 1) * LANES]
            if rope:
                piece = _rope128(piece, cos, sin, first_half)
            if mul is not None:
                piece = piece * mul
            o_ref[0, :, c0 + i * LANES:c0 + (i + 1) * LANES] = piece

    for g in range(4):
        o_ref[0, :, COL_HY + g * 512:COL_HY + (g + 1) * 512] = seg(COL_HY + g * 512, 512)
    put_rope(COL_QA, seg(COL_QA, 512), None)
    o_ref[0, :, COL_GA:COL_GA + 512] = _silu(seg(COL_GA, 512))
    put_rope(COL_QR, seg(COL_QR, 512), None)
    put_rope(COL_KR, seg(COL_KR, 512), RET_DIM ** -0.5)
    o_ref[0, :, COL_VR:COL_VR + 512] = seg(COL_VR, 512)
    o_ref[0, :, COL_GR:COL_GR + 512] = _silu(seg(COL_GR, 512))
    put_rope(COL_KA, seg(COL_KA, 128), None)
    o_ref[0, :, COL_VA:COL_VA + 128] = seg(COL_VA, 128)


def _in_call(x, shift, scale, nw, w, cos_t, sin_t, *, rope, tm):
    b, l, d = x.shape
    per_batch = shift.shape[0] > 1
    mod_map = (lambda i, j: (i, 0, 0)) if per_batch else (lambda i, j: (0, 0, 0))
    return pl.pallas_call(
        functools.partial(_in_kernel, rope=rope),
        grid=(b, l // tm),
        in_specs=[pl.BlockSpec((1, tm, d), lambda i, j: (i, j, 0)),
                  pl.BlockSpec((1, 1, d), mod_map),
                  pl.BlockSpec((1, 1, d), mod_map),
                  pl.BlockSpec((1, d), lambda i, j: (0, 0)),
                  pl.BlockSpec((d, IN_DIM), lambda i, j: (0, 0)),
                  pl.BlockSpec((tm, LANES), lambda i, j: (j, 0)),
                  pl.BlockSpec((tm, LANES), lambda i, j: (j, 0))],
        out_specs=pl.BlockSpec((1, tm, IN_DIM), lambda i, j: (i, j, 0)),
        out_shape=jax.ShapeDtypeStruct((b, l, IN_DIM), F32),
        compiler_params=_cparams(("parallel", "parallel")),
        name="in_proj_rope" if rope else "in_proj",
    )(x, shift, scale, nw, w, cos_t, sin_t)


def _out_kernel(x_ref, shift_ref, scale_ref, gate_ref, nw_ref, ya_ref, yb_ref, yc_ref, wm_ref, bm_ref,
                wa_ref, wb_ref, wc_ref, wo_ref, fnw_ref, *out_refs, final):
    x = x_ref[0]
    d = x.shape[1]
    hb = _modulated(x, nw_ref[...], scale_ref[0], shift_ref[0]).astype(BF16)
    merged = None
    for i, (y_ref, w_ref) in enumerate(((ya_ref, wa_ref), (yb_ref, wb_ref), (yc_ref, wc_ref))):
        g = jax.nn.sigmoid(_dot(hb, wm_ref[:, i * d:(i + 1) * d]) + bm_ref[:, i * d:(i + 1) * d])
        term = g * _dot(y_ref[0].astype(BF16), w_ref[...])
        merged = term if merged is None else merged + term
    out = _dot(merged.astype(BF16), wo_ref[...])
    xn = x + gate_ref[0] * out
    out_refs[0][0] = xn
    if final:
        ms = jnp.mean(xn * xn, axis=-1, keepdims=True)
        out_refs[1][0] = xn * lax.rsqrt(ms + EPS) * fnw_ref[...]


def _out_call(x, shift, scale, gate, nw, ya, yb, yc, wm, bm, wa, wb, wc, wo, fnw, *, final, tm):
    b, l, d = x.shape
    per_batch = shift.shape[0] > 1
    mod_map = (lambda i, j: (i, 0, 0)) if per_batch else (lambda i, j: (0, 0, 0))
    tok = lambda w: pl.BlockSpec((1, tm, w), lambda i, j: (i, j, 0))
    full = lambda a: pl.BlockSpec(a.shape, lambda i, j: (0,) * a.ndim)
    n_out = 2 if final else 1
    res = pl.pallas_call(
        functools.partial(_out_kernel, final=final),
        grid=(b, l // tm),
        in_specs=[tok(d), pl.BlockSpec((1, 1, d), mod_map), pl.BlockSpec((1, 1, d), mod_map),
                  pl.BlockSpec((1, 1, d), mod_map), full(nw), tok(W_HY), tok(W_ATT), tok(W_RET),
                  full(wm), full(bm), full(wa), full(wb), full(wc), full(wo), full(fnw)],
        out_specs=[tok(d)] * n_out,
        out_shape=[jax.ShapeDtypeStruct((b, l, d), F32)] * n_out,
        compiler_params=_cparams(("parallel", "parallel")),
        name="merge_out_final" if final else "merge_out",
    )(x, shift, scale, gate, nw, ya, yb, yc, wm, bm, wa, wb, wc, wo, fnw)
    return res


_NT = (((1,), (1,)), ((), ()))


def _attn_kv_group(sink_ref, q, kh, vh, g_ref, o_ref, kv, mask_fn):
    t = q.shape[0]
    h0 = kv * ATT_GROUP
    qs = jnp.concatenate([q[:, (h0 + g) * HEAD_DIM:(h0 + g + 1) * HEAD_DIM] for g in range(ATT_GROUP)],
                         axis=0).astype(BF16)
    s = lax.dot_general(qs, kh, _NT, preferred_element_type=F32)
    if mask_fn is not None:
        s = mask_fn(s)
    head = lax.broadcasted_iota(jnp.int32, (ATT_GROUP * t, 1), 0) // t
    sink = jnp.full((ATT_GROUP * t, 1), sink_ref[h0], F32)
    for g in range(1, ATT_GROUP):
        sink = jnp.where(head == g, sink_ref[h0 + g], sink)
    m = jnp.maximum(jnp.max(s, axis=-1, keepdims=True), sink)
    p = jnp.exp(s - m)
    denom = jnp.sum(p, axis=-1, keepdims=True) + jnp.exp(sink - m)
    o = _dot(p.astype(BF16), vh) / denom
    for gp in range(ATT_GROUP // 2):
        pair = jnp.concatenate([o[(2 * gp) * t:(2 * gp + 1) * t], o[(2 * gp + 1) * t:(2 * gp + 2) * t]],
                               axis=1)
        c0 = (h0 + 2 * gp) * HEAD_DIM
        o_ref[0, :, c0:c0 + 2 * HEAD_DIM] = pair * g_ref[0, :, c0:c0 + 2 * HEAD_DIM]


def _attn_ctx_kernel(sink_ref, q_ref, k_ref, v_ref, g_ref, o_ref):
    q = q_ref[0] * (HEAD_DIM ** -0.5)
    k = k_ref[0].astype(BF16)
    v = v_ref[0].astype(BF16)
    for kv in range(ATT_KV_HEADS):
        kh = k[:, kv * HEAD_DIM:(kv + 1) * HEAD_DIM]
        vh = v[:, kv * HEAD_DIM:(kv + 1) * HEAD_DIM]
        _attn_kv_group(sink_ref, q, kh, vh, g_ref, o_ref, kv, None)


def _attn_ctx_call(u, sink):
    b, l, _ = u.shape
    return pl.pallas_call(
        _attn_ctx_kernel,
        grid=(b,),
        in_specs=[pl.BlockSpec(memory_space=pltpu.SMEM),
                  pl.BlockSpec((1, l, W_ATT), lambda i: (i, 0, COL_QA // W_ATT)),
                  pl.BlockSpec((1, l, W_KV), lambda i: (i, 0, COL_KA // W_KV)),
                  pl.BlockSpec((1, l, W_KV), lambda i: (i, 0, COL_VA // W_KV)),
                  pl.BlockSpec((1, l, W_ATT), lambda i: (i, 0, COL_GA // W_ATT))],
        out_specs=pl.BlockSpec((1, l, W_ATT), lambda i: (i, 0, 0)),
        out_shape=jax.ShapeDtypeStruct((b, l, W_ATT), F32),
        compiler_params=_cparams(("parallel",)),
        name="attn_ctx",
    )(sink, u, u, u, u)


def _attn_lat_kernel(sink_ref, q_ref, kp_ref, kc_ref, kn_ref, vp_ref, vc_ref, vn_ref, kx_ref, vx_ref,
                     g_ref, o_ref):
    i = pl.program_id(1)
    nb = pl.num_programs(1)
    q = q_ref[0] * (HEAD_DIM ** -0.5)
    kall = jnp.concatenate([kp_ref[0], kc_ref[0], kn_ref[0], kx_ref[0]], axis=0).astype(BF16)
    vall = jnp.concatenate([vp_ref[0], vc_ref[0], vn_ref[0], vx_ref[0]], axis=0).astype(BF16)
    rows = ATT_GROUP * ATT_BLOCK
    r = lax.broadcasted_iota(jnp.int32, (rows, ATT_BLOCK), 0) % ATT_BLOCK
    c = lax.broadcasted_iota(jnp.int32, (rows, ATT_BLOCK), 1)
    ok_prev = (c >= r) & (i > 0)
    ok_next = (c <= r) & (i < nb - 1)

    def band(s):
        b = ATT_BLOCK
        return jnp.concatenate([jnp.where(ok_prev, s[:, :b], NEG), s[:, b:2 * b],
                                jnp.where(ok_next, s[:, 2 * b:3 * b], NEG), s[:, 3 * b:]], axis=1)

    for kv in range(ATT_KV_HEADS):
        kh = kall[:, kv * HEAD_DIM:(kv + 1) * HEAD_DIM]
        vh = vall[:, kv * HEAD_DIM:(kv + 1) * HEAD_DIM]
        _attn_kv_group(sink_ref, q, kh, vh, g_ref, o_ref, kv, band)


def _attn_lat_call(u, kctx, vctx, sink):
    b, l, _ = u.shape
    nb = l // ATT_BLOCK
    past = kctx.shape[1]
    kcol = COL_KA // W_KV
    vcol = COL_VA // W_KV
    prev = lambda col: pl.BlockSpec((1, ATT_BLOCK, W_KV), lambda i, j: (i, jnp.maximum(j - 1, 0), col))
    cur = lambda col: pl.BlockSpec((1, ATT_BLOCK, W_KV), lambda i, j: (i, j, col))
    nxt = lambda col: pl.BlockSpec((1, ATT_BLOCK, W_KV), lambda i, j: (i, jnp.minimum(j + 1, nb - 1), col))
    ctx = pl.BlockSpec((1, past, W_KV), lambda i, j: (i, 0, 0))
    return pl.pallas_call(
        _attn_lat_kernel,
        grid=(b, nb),
        in_specs=[pl.BlockSpec(memory_space=pltpu.SMEM),
                  pl.BlockSpec((1, ATT_BLOCK, W_ATT), lambda i, j: (i, j, COL_QA // W_ATT)),
                  prev(kcol), cur(kcol), nxt(kcol), prev(vcol), cur(vcol), nxt(vcol), ctx, ctx,
                  pl.BlockSpec((1, ATT_BLOCK, W_ATT), lambda i, j: (i, j, COL_GA // W_ATT))],
        out_specs=pl.BlockSpec((1, ATT_BLOCK, W_ATT), lambda i, j: (i, j, 0)),
        out_shape=jax.ShapeDtypeStruct((b, l, W_ATT), F32),
        compiler_params=_cparams(("parallel", "parallel")),
        name="attn_lat",
    )(sink, u, u, u, u, u, u, u, kctx, vctx, u)


def _log_sigmoid(x):
    return jnp.minimum(x, 0.0) - jnp.log1p(jnp.exp(-jnp.abs(x)))


def _ret_kernel(q_ref, k_ref, v_ref, g_ref, thl_ref, thb_ref, gn_ref, s0_ref, o_ref, sfin_ref, ob_ref, *,
                nc, has_s0):
    hp = pl.program_id(1)
    c = RET_CHUNK
    lane = lax.broadcasted_iota(jnp.int32, (1, LANES), 1)
    lo_head = lane < RET_DIM
    rowf = lax.broadcasted_iota(jnp.int32, (c, LANES), 0).astype(F32)
    ii = lax.broadcasted_iota(jnp.int32, (c, c), 0)
    jj = lax.broadcasted_iota(jnp.int32, (c, c), 1)
    dd = lax.broadcasted_iota(jnp.int32, (LANES, LANES), 0)
    ee = lax.broadcasted_iota(jnp.int32, (LANES, LANES), 1)
    same_head = (dd < RET_DIM) == (ee < RET_DIM)
    gn = gn_ref[...]

    def tables(direction):
        lg_lane = _log_sigmoid(thl_ref[direction])
        dmats = []
        for hh in range(2):
            lg_h = _log_sigmoid(thb_ref[direction, pl.ds(2 * hp + hh, 1), :])
            if direction == 0:
                dist = ii - jj
            else:
                dist = jj - ii
            dmats.append(jnp.where(dist >= 0, jnp.exp(lg_h * jnp.maximum(dist, 0).astype(F32)), 0.0))
        if direction == 0:
            q_dec = jnp.exp(lg_lane * (rowf + 1.0))
            k_dec = jnp.exp(lg_lane * (c - 1.0 - rowf))
        else:
            q_dec = jnp.exp(lg_lane * (c - rowf))
            k_dec = jnp.exp(lg_lane * rowf)
        c_dec = jnp.exp(lg_lane * float(c))
        return jnp.concatenate(dmats, axis=0), q_dec, k_dec, c_dec

    def chunk(tabs, r0, s):
        dmat2, q_dec, k_dec, c_dec = tabs
        qc = q_ref[0, pl.ds(r0, c), :]
        kc = k_ref[0, pl.ds(r0, c), :]
        vc = v_ref[0, pl.ds(r0, c), :]
        kcb = kc.astype(BF16)
        vcb = vc.astype(BF16)
        qs = jnp.concatenate([jnp.where(lo_head, qc, 0.0), jnp.where(lo_head, 0.0, qc)], axis=0)
        sc = lax.dot_general(qs.astype(BF16), kcb, _NT, preferred_element_type=F32) * dmat2
        pv = _dot(sc.astype(BF16), vcb)
        o = _dot((qc * q_dec).astype(BF16), s.astype(BF16)) + jnp.where(lo_head, pv[:c], pv[c:])
        kd_t = (kc * k_dec).T.astype(BF16)
        s_new = c_dec * s + jnp.where(same_head, _dot(kd_t, vcb), 0.0)
        return o, s_new

    def init_state(direction):
        if has_s0:
            return s0_ref[0, direction, 0]
        return jnp.zeros((LANES, LANES), F32)

    tabs_f = tables(0)
    tabs_b = tables(1)

    def scan_body(n, carry):
        s_f, s_b = carry
        rf = pl.multiple_of(n * c, c)
        rb = pl.multiple_of((nc - 1 - n) * c, c)
        o_f, s_f = chunk(tabs_f, rf, s_f)
        o_ref[0, pl.ds(rf, c), :] = o_f
        o_b, s_b = chunk(tabs_b, rb, s_b)
        ob_ref[pl.ds(rb, c), :] = o_b
        return s_f, s_b

    s_f, s_b = lax.fori_loop(0, nc, scan_body, (init_state(0), init_state(1)), unroll=2)
    sfin_ref[0, 0, 0] = s_f
    sfin_ref[0, 1, 0] = s_b

    def norm_body(n, carry):
        r0 = pl.multiple_of(n * c, c)
        o = o_ref[0, pl.ds(r0, c), :] + ob_ref[pl.ds(r0, c), :]
        o2 = o * o
        s_lo = jnp.sum(jnp.where(lo_head, o2, 0.0), axis=-1, keepdims=True)
        s_hi = jnp.sum(jnp.where(lo_head, 0.0, o2), axis=-1, keepdims=True)
        ms = jnp.where(lo_head, s_lo, s_hi) * (1.0 / RET_DIM)
        o_ref[0, pl.ds(r0, c), :] = o * lax.rsqrt(ms + EPS) * gn * g_ref[0, pl.ds(r0, c), :]
        return carry

    lax.fori_loop(0, nc, norm_body, 0, unroll=2)


def _ret_call(u, theta, gn, s0bd):
    b, l, _ = u.shape
    nc = l // RET_CHUNK
    npair = RET_HEADS // 2
    has_s0 = s0bd is not None
    if not has_s0:
        s0bd = jnp.zeros((1, 2, 1, LANES, LANES), F32)
        s0_spec = pl.BlockSpec((1, 2, 1, LANES, LANES), lambda i, j: (0, 0, 0, 0, 0))
    else:
        s0_spec = pl.BlockSpec((1, 2, 1, LANES, LANES), lambda i, j: (i, 0, j, 0, 0))
    th_lane = jnp.repeat(theta, RET_DIM, axis=1).reshape(2, 1, W_RET)
    th_bcast = jnp.broadcast_to(theta[:, :, None], (2, RET_HEADS, LANES))
    col = lambda c0: pl.BlockSpec((1, l, LANES), lambda i, j: (i, 0, c0 // LANES + j))
    o, sfin = pl.pallas_call(
        functools.partial(_ret_kernel, nc=nc, has_s0=has_s0),
        grid=(b, npair),
        in_specs=[col(COL_QR), col(COL_KR), col(COL_VR), col(COL_GR),
                  pl.BlockSpec((2, 1, LANES), lambda i, j: (0, 0, j)),
                  pl.BlockSpec((2, RET_HEADS, LANES), lambda i, j: (0, 0, 0)),
                  pl.BlockSpec((1, LANES), lambda i, j: (0, j)),
                  s0_spec],
        out_specs=[pl.BlockSpec((1, l, LANES), lambda i, j: (i, 0, j)),
                   pl.BlockSpec((1, 2, 1, LANES, LANES), lambda i, j: (i, 0, j, 0, 0))],
        out_shape=[jax.ShapeDtypeStruct((b, l, W_RET), F32),
                   jax.ShapeDtypeStruct((b, 2, npair, LANES, LANES), F32)],
        scratch_shapes=[pltpu.VMEM((l, LANES), F32)],
        compiler_params=_cparams(("parallel", "parallel")),
        name="retention_s0" if has_s0 else "retention",
    )(u, u, u, u, th_lane, th_bcast, gn.reshape(1, W_RET), s0bd)
    return o, sfin


def _blockdiag_states(s):
    b = s.shape[0]
    sp = s.reshape(b, 2, RET_HEADS // 2, 2, RET_DIM, RET_DIM)
    z = jnp.zeros_like(sp[:, :, :, 0])
    top = jnp.concatenate([sp[:, :, :, 0], z], axis=-1)
    bot = jnp.concatenate([z, sp[:, :, :, 1]], axis=-1)
    return jnp.concatenate([top, bot], axis=-2)


def _diag_states(sbd):
    b = sbd.shape[0]
    s0 = sbd[:, :, :, :RET_DIM, :RET_DIM]
    s1 = sbd[:, :, :, RET_DIM:, RET_DIM:]
    return jnp.stack([s0, s1], axis=3).reshape(b, 2, RET_HEADS, RET_DIM, RET_DIM)


def _filter_positions(l):
    n = np.arange(2 * l)
    pos = np.where(n < l, n, 2 * l - n)
    pos = np.where(n == l, 0, pos)
    t = jnp.linspace(0.0, 1.0, l, dtype=F32)[:, None]
    w = 2.0 * math.pi * jnp.arange(l, dtype=F32)[:, None] / l
    f = jnp.linspace(1e-4, HY_BANDS - 1, HY_BANDS, dtype=F32)[None, :]
    z = jnp.concatenate([t, jnp.cos(f * w), -jnp.sin(f * w)], axis=-1)
    z = jnp.pad(z, ((0, 0), (0, 32 - HY_POS_FEAT)))
    return z[pos]


def _hyena_deltas():
    max_decay = math.log(HY_DECAY_TARGET) / HY_FAST_PCT
    min_decay = math.log(HY_DECAY_TARGET) / HY_SLOW_PCT
    return jnp.abs(jnp.linspace(min_decay, max_decay, W_HY, dtype=F32))[None, :]


def _filter_hidden(z_ref, w1_ref, b1_ref, fr_ref):
    pre = jnp.dot(z_ref[...], w1_ref[0], precision=HIGHEST, preferred_element_type=F32) + b1_ref[0]
    return jnp.sin(fr_ref[0] * pre)


def _filter_raw(hid, w2f, w2b, tp, dl, row0, l):
    win = jnp.exp(-tp * dl)
    row = row0 + lax.broadcasted_iota(jnp.int32, win.shape, 0)
    hf = jnp.dot(hid, w2f, precision=HIGHEST, preferred_element_type=F32) * win
    hb = jnp.dot(hid, w2b, precision=HIGHEST, preferred_element_type=F32) * win
    hf = jnp.where(row < l, hf, 0.0)
    hb = jnp.where((row > l) | (row == 0), hb, 0.0)
    return hf + hb, jnp.sum(jnp.abs(hf) + jnp.abs(hb), axis=0, keepdims=True)


def _with_skip(g, skip):
    row = lax.broadcasted_iota(jnp.int32, g.shape, 0)
    return g + jnp.where(row == 0, skip, 0.0)


def _filt_ctx_kernel(z_ref, w1_ref, b1_ref, fr_ref, w2_ref, dl_ref, sk_ref, fh_ref, fl_ref, g_ref):
    hid = _filter_hidden(z_ref, w1_ref, b1_ref, fr_ref)
    tp = z_ref[:, 0:1]
    for o in range(2):
        w2f = w2_ref[0, :, (2 * o) * W_HY:(2 * o + 1) * W_HY]
        w2b = w2_ref[0, :, (2 * o + 1) * W_HY:(2 * o + 2) * W_HY]
        raw, nrm = _filter_raw(hid, w2f, w2b, tp, dl_ref[...], 0, z_ref.shape[0] // 2)
        g = _with_skip(raw / nrm, sk_ref[0, pl.ds(o, 1), :])
        g_ref[0, o] = _dot3c(fh_ref[...], fl_ref[...], g)


def _ctx_dft_tables(l):
    n = 2 * l
    k = np.arange(n)[:, None]
    t = np.arange(l)[None, :]
    ang = 2.0 * np.pi * k * t / n
    c, s = np.cos(ang), np.sin(ang)
    fwd = np.block([[c, s], [-s, c]])
    inv = np.block([[c.T, -s.T], [s.T, c.T]])
    n_all = np.arange(n)[None, :]
    angg = 2.0 * np.pi * k * n_all / n
    filt = np.concatenate([np.cos(angg), -np.sin(angg)], axis=0) / n
    return _split_np(fwd), _split_np(inv), _split_np(filt)


def _filt_ctx_call(l, w1, b1, freq, w2, skip):
    n = 2 * l
    z_ext = _filter_positions(l)
    _, _, (fh, fl) = _ctx_dft_tables(l)
    w1p = jnp.pad(w1, ((0, 0), (0, 32 - HY_POS_FEAT), (0, 0)))
    lay = lambda *shape: pl.BlockSpec((1,) + shape, lambda d: (d,) + (0,) * len(shape))
    full = lambda a: pl.BlockSpec(a.shape, lambda d: (0,) * a.ndim)
    dl = _hyena_deltas()
    return pl.pallas_call(
        _filt_ctx_kernel,
        grid=(DEPTH,),
        in_specs=[full(z_ext), lay(32, HY_FILT_HID), lay(1, HY_FILT_HID), lay(1, HY_FILT_HID),
                  lay(HY_FILT_HID, 4 * W_HY), full(dl), lay(2, W_HY), full(fh), full(fl)],
        out_specs=pl.BlockSpec((1, 2, 2 * n, W_HY), lambda d: (d, 0, 0, 0)),
        out_shape=jax.ShapeDtypeStruct((DEPTH, 2, 2 * n, W_HY), F32),
        compiler_params=_cparams(("arbitrary",)),
        name="hyena_filter_ctx",
    )(z_ext, w1p, b1.reshape(DEPTH, 1, -1), freq.reshape(DEPTH, 1, -1), w2, dl, skip, fh, fl)


def _lat_dft_tables():
    ka = np.arange(NA)[:, None]
    b = np.arange(NB)[:, None, None]
    kb = np.arange(NB)[:, None]
    bb = np.arange(NB)[None, :]
    a_half = np.arange(NA // 2)[None, :]
    a_full = np.arange(NA)[None, :]
    phi = 2.0 * np.pi * (ka * a_half / NA + b * ka / LAT_N)
    c, s = np.cos(phi), np.sin(phi)
    a_fwd = np.concatenate([c, s], axis=2)
    ct, st = np.swapaxes(c, 1, 2), np.swapaxes(s, 1, 2)
    a_inv = np.concatenate([ct, st], axis=2)
    phig = 2.0 * np.pi * (ka * a_full / NA + b * ka / LAT_N)
    a_flt = np.concatenate([np.cos(phig), -np.sin(phig)], axis=1) / LAT_N
    ang = 2.0 * np.pi * kb * bb / NB
    c2, s2 = np.cos(ang), np.sin(ang)
    f_fwd = np.block([[c2, s2], [-s2, c2]])
    f_inv = np.block([[c2, -s2], [s2, c2]])
    return (_split_np(a_fwd), _split_np(a_inv), _split_np(a_flt), _split_np(f_fwd), _split_np(f_inv))


def _stage_b_rows(ka):
    re = pl.ds(ka, NB, stride=Y_PITCH)
    im = pl.ds(NA + ka, NB, stride=Y_PITCH)
    return re, im


def _filt_lat_kernel(z_ref, w1_ref, b1_ref, fr_ref, w2f_ref, w2b_ref, dl_ref, sk_ref, ah_ref, al_ref,
                     f2h_ref, f2l_ref, g_ref, hid_ref, gt_ref, y_ref):
    step = pl.program_id(1)
    rch = 1024
    nch = LAT_N // rch
    rows_of = lambda i: pl.ds(pl.multiple_of(i * rch, rch), rch)

    @pl.when(step == 0)
    def _():
        def hid_chunk(i, carry):
            r = rows_of(i)
            pre = jnp.dot(z_ref[r, :], w1_ref[0], precision=HIGHEST, preferred_element_type=F32)
            hid_ref[r, :] = jnp.sin(fr_ref[0] * (pre + b1_ref[0]))
            return carry

        lax.fori_loop(0, nch, hid_chunk, 0)

    def raw_chunk(i, nrm):
        r = rows_of(i)
        raw, part = _filter_raw(hid_ref[r, :], w2f_ref[0], w2b_ref[0], z_ref[r, 0:1], dl_ref[...],
                                i * rch, LAT_L)
        gt_ref[r, :] = raw
        return nrm + part

    nrm = lax.fori_loop(0, nch, raw_chunk, jnp.zeros((1, LANES), F32))

    def norm_chunk(i, carry):
        r = rows_of(i)
        gt_ref[r, :] = gt_ref[r, :] / nrm
        return carry

    lax.fori_loop(0, nch, norm_chunk, 0)
    order = step // (W_HY // LANES)
    gt_ref[0:8, :] = _with_skip(gt_ref[0:8, :], sk_ref[0, pl.ds(order, 1), :])

    def stage_a(b, carry):
        rows = gt_ref[pl.ds(b, NA, stride=NB), :]
        y_ref[pl.ds(pl.multiple_of(b * Y_PITCH, 8), 2 * NA), :] = _dot3c(ah_ref[b], al_ref[b], rows)
        return carry

    lax.fori_loop(0, NB, stage_a, 0, unroll=4)

    def stage_b(ka, carry):
        re, im = _stage_b_rows(ka)
        z = jnp.concatenate([y_ref[re, :], y_ref[im, :]], axis=0)
        g_ref[0, 0, pl.ds(pl.multiple_of(ka * 2 * NB, 2 * NB), 2 * NB), :] = _dot3c(
            f2h_ref[...], f2l_ref[...], z)
        return carry

    lax.fori_loop(0, NA, stage_b, 0, unroll=2)


def _filt_lat_call(w1, b1, freq, w2, skip):
    z_ext = _filter_positions(LAT_L)
    _, _, (ah, al), (f2h, f2l), _ = _lat_dft_tables()
    w1p = jnp.pad(w1, ((0, 0), (0, 32 - HY_POS_FEAT), (0, 0)))
    nct = W_HY // LANES
    one = pl.Buffered(1)
    lay = lambda *shape: pl.BlockSpec((1,) + shape, lambda d, s: (d,) + (0,) * len(shape))
    full = lambda a: pl.BlockSpec(a.shape, lambda d, s: (0,) * a.ndim, pipeline_mode=one)
    dl = _hyena_deltas()
    return pl.pallas_call(
        _filt_lat_kernel,
        grid=(DEPTH, 2 * nct),
        in_specs=[full(z_ext), lay(32, HY_FILT_HID), lay(1, HY_FILT_HID), lay(1, HY_FILT_HID),
                  pl.BlockSpec((1, HY_FILT_HID, LANES), lambda d, s: (d, 0, (s // nct) * 2 * nct + s % nct)),
                  pl.BlockSpec((1, HY_FILT_HID, LANES),
                               lambda d, s: (d, 0, (s // nct) * 2 * nct + nct + s % nct)),
                  pl.BlockSpec((1, LANES), lambda d, s: (0, s % nct)),
                  pl.BlockSpec((1, 2, LANES), lambda d, s: (d, 0, s % nct)),
                  full(ah), full(al), full(f2h), full(f2l)],
        out_specs=pl.BlockSpec((1, 1, NA * 2 * NB, LANES), lambda d, s: (d, s // nct, 0, s % nct)),
        out_shape=jax.ShapeDtypeStruct((DEPTH, 2, NA * 2 * NB, W_HY), F32),
        scratch_shapes=[pltpu.VMEM((LAT_N, HY_FILT_HID), F32), pltpu.VMEM((LAT_N, LANES), F32),
                        pltpu.VMEM((NB * Y_PITCH, LANES), F32)],
        compiler_params=_cparams(("arbitrary", "arbitrary")),
        name="hyena_filter_lat",
    )(z_ext, w1p, b1.reshape(DEPTH, 1, -1), freq.reshape(DEPTH, 1, -1), w2, w2, dl, skip, ah, al, f2h,
      f2l)


def _short_conv_rows(ref, bi, r0, rows, first, last, w):
    total = ref.shape[1]
    cur = ref[bi, pl.ds(r0, rows), :]
    before = ref[bi, pl.ds(jnp.maximum(r0 - 1, 0), 1), :]
    after = ref[bi, pl.ds(jnp.minimum(r0 + rows, total - 1), 1), :]
    before = jnp.where(first, 0.0, before)
    after = jnp.where(last, 0.0, after)
    rid = lax.broadcasted_iota(jnp.int32, cur.shape, 0)
    prev = jnp.where(rid == 0, before, pltpu.roll(cur, 1, 0))
    nxt = jnp.where(rid == rows - 1, after, pltpu.roll(cur, rows - 1, 0))
    return prev * w[0:1] + cur * w[1:2] + nxt * w[2:3]


def _cmul(xr, xi, gr, gi):
    return xr * gr - xi * gi, xr * gi + xi * gr


def _hy_ctx_kernel(v_ref, x1_ref, x2_ref, gt_ref, cw_ref, g_ref, fh_ref, fl_ref, ih_ref, il_ref, o_ref):
    l = v_ref.shape[1]
    n = 2 * l

    def sc(ref, bi, grp):
        w = cw_ref[:, grp * W_HY:(grp + 1) * W_HY]
        return _short_conv_rows(ref, bi, 0, l, True, True, w)

    def conv(zr, zi, order):
        x = _dot3c(fh_ref[...], fl_ref[...], jnp.concatenate([zr, zi], axis=0))
        pr, pi = _cmul(x[:n], x[n:], g_ref[order, :n], g_ref[order, n:])
        y = _dot3c(ih_ref[...], il_ref[...], jnp.concatenate([pr, pi], axis=0))
        return y[:l], y[l:]

    yr, yi = conv(sc(v_ref, 0, 0), sc(v_ref, 1, 0), 0)
    yr, yi = conv(sc(x1_ref, 0, 1) * yr, sc(x1_ref, 1, 1) * yi, 1)
    o_ref[0] = sc(x2_ref, 0, 2) * yr * _silu(gt_ref[0])
    o_ref[1] = sc(x2_ref, 1, 2) * yi * _silu(gt_ref[1])


def _hy_ctx_call(u, conv_w, g_spec):
    b, l, _ = u.shape
    (fh, fl), (ih, il), _ = _ctx_dft_tables(l)
    grp = lambda g: pl.BlockSpec((2, l, W_HY), lambda i: (i, 0, g))
    full = lambda a: pl.BlockSpec(a.shape, lambda i: (0,) * a.ndim)
    return pl.pallas_call(
        _hy_ctx_kernel,
        grid=(b // 2,),
        in_specs=[grp(0), grp(1), grp(2), grp(3), full(conv_w), full(g_spec), full(fh), full(fl),
                  full(ih), full(il)],
        out_specs=pl.BlockSpec((2, l, W_HY), lambda i: (i, 0, 0)),
        out_shape=jax.ShapeDtypeStruct((b, l, W_HY), F32),
        compiler_params=_cparams(("parallel",)),
        name="hyena_ctx",
    )(u, u, u, u, conv_w, g_spec, fh, fl, ih, il)


def _hy_lat_kernel(z_ref, m_ref, gt_ref, cw_ref, g_ref, afh_ref, afl_ref, aih_ref, ail_ref, f2h_ref,
                   f2l_ref, f3h_ref, f3l_ref, o_ref, xr_scr, xi_scr, y_scr, *, conv_in, gated):
    x_scr = (xr_scr, xi_scr)
    na_half = NA // 2
    w_in = cw_ref[0] if conv_in else None
    w_mul = cw_ref[1]

    def load_in(a, carry):
        for bi in range(2):
            r0 = pl.multiple_of(a * NB, NB)
            if conv_in:
                val = _short_conv_rows(z_ref, bi, r0, NB, a == 0, a == na_half - 1, w_in)
            else:
                val = z_ref[bi, pl.ds(r0, NB), :]
            x_scr[bi][pl.ds(pl.multiple_of(a * X_PITCH, 8), NB), :] = val
        return carry

    lax.fori_loop(0, na_half, load_in, 0)

    def stage_a(b, carry):
        zr = xr_scr[pl.ds(b, na_half, stride=X_PITCH), :]
        zi = xi_scr[pl.ds(b, na_half, stride=X_PITCH), :]
        rhs = jnp.concatenate([jnp.concatenate([zr, zi], axis=0), jnp.concatenate([zi, -zr], axis=0)],
                              axis=1)
        y = _dot3c(afh_ref[b], afl_ref[b], rhs)
        r0 = pl.multiple_of(b * Y_PITCH, 8)
        y_scr[pl.ds(r0, NA), :] = y[:, :LANES]
        y_scr[pl.ds(r0 + NA, NA), :] = y[:, LANES:]
        return carry

    lax.fori_loop(0, NB, stage_a, 0, unroll=4)

    def stage_b(ka, carry):
        re, im = _stage_b_rows(ka)
        re1, im1 = _stage_b_rows(ka + 1)
        z = jnp.concatenate([jnp.concatenate([y_scr[re, :], y_scr[im, :]], axis=0),
                             jnp.concatenate([y_scr[re1, :], y_scr[im1, :]], axis=0)], axis=1)
        x = _dot3c(f2h_ref[...], f2l_ref[...], z)
        g0 = pl.multiple_of(ka * 2 * NB, 2 * NB)
        gr = jnp.concatenate([g_ref[pl.ds(g0, NB), :], g_ref[pl.ds(g0 + 2 * NB, NB), :]], axis=1)
        gi = jnp.concatenate([g_ref[pl.ds(g0 + NB, NB), :], g_ref[pl.ds(g0 + 3 * NB, NB), :]], axis=1)
        pr, pi = _cmul(x[:NB], x[NB:], gr, gi)
        u = _dot3c(f3h_ref[...], f3l_ref[...], jnp.concatenate([pr, pi], axis=0))
        y_scr[re, :] = u[:NB, :LANES]
        y_scr[im, :] = u[NB:, :LANES]
        y_scr[re1, :] = u[:NB, LANES:]
        y_scr[im1, :] = u[NB:, LANES:]
        return carry

    lax.fori_loop(0, NA // 2, lambda j, carry: stage_b(2 * j, carry), 0, unroll=2)

    def stage_c(b, carry):
        r0 = pl.multiple_of(b * Y_PITCH, 8)
        ur = y_scr[pl.ds(r0, NA), :]
        ui = y_scr[pl.ds(r0 + NA, NA), :]
        rhs = jnp.concatenate([jnp.concatenate([ur, -ui], axis=0), jnp.concatenate([ui, ur], axis=0)],
                              axis=1)
        y = _dot3c(aih_ref[b], ail_ref[b], rhs)
        xr_scr[pl.ds(b, na_half, stride=X_PITCH), :] = y[:, :LANES]
        xi_scr[pl.ds(b, na_half, stride=X_PITCH), :] = y[:, LANES:]
        return carry

    lax.fori_loop(0, NB, stage_c, 0, unroll=4)

    def store_out(a, carry):
        for bi in range(2):
            r0 = pl.multiple_of(a * NB, NB)
            mul = _short_conv_rows(m_ref, bi, r0, NB, a == 0, a == na_half - 1, w_mul)
            val = x_scr[bi][pl.ds(pl.multiple_of(a * X_PITCH, 8), NB), :] * mul
            if gated:
                val = val * _silu(gt_ref[bi, pl.ds(r0, NB), :])
            o_ref[bi, pl.ds(r0, NB), :] = val
        return carry

    lax.fori_loop(0, na_half, store_out, 0)


def _hy_lat_call(src, src_col, u, mul_col, gate_col, conv_w2, g_spec, *, conv_in, gated):
    b, l, _ = u.shape
    nct = W_HY // LANES
    (afh, afl), (aih, ail), _, (f2h, f2l), (f3h, f3l) = _lat_dft_tables()
    one = pl.Buffered(1)
    blk = lambda col: pl.BlockSpec((2, l, LANES), lambda c, p: (p, 0, col + c), pipeline_mode=one)
    const = lambda a: pl.BlockSpec(a.shape, lambda c, p: (0,) * a.ndim, pipeline_mode=one)
    gate_spec = blk(gate_col) if gated else pl.BlockSpec((2, 8, LANES), lambda c, p: (0, 0, 0))
    return pl.pallas_call(
        functools.partial(_hy_lat_kernel, conv_in=conv_in, gated=gated),
        grid=(nct, b // 2),
        in_specs=[blk(src_col), blk(mul_col), gate_spec,
                  pl.BlockSpec((2, 3, LANES), lambda c, p: (0, 0, c)),
                  pl.BlockSpec((NA * 2 * NB, LANES), lambda c, p: (0, c), pipeline_mode=one),
                  const(afh), const(afl), const(aih), const(ail), const(f2h), const(f2l), const(f3h),
                  const(f3l)],
        out_specs=pl.BlockSpec((2, l, LANES), lambda c, p: (p, 0, c)),
        out_shape=jax.ShapeDtypeStruct((b, l, W_HY), F32),
        scratch_shapes=[pltpu.VMEM(((NA // 2) * X_PITCH, LANES), F32),
                        pltpu.VMEM(((NA // 2) * X_PITCH, LANES), F32),
                        pltpu.VMEM((NB * Y_PITCH, LANES), F32)],
        compiler_params=_cparams(("arbitrary", "arbitrary")),
        name="hyena_lat_o2" if gated else "hyena_lat_o1",
    )(src, u, u, conv_w2, g_spec, afh, afl, aih, ail, f2h, f2l, f3h, f3l)


def _rope_tables(l):
    rows = l // GRID_W
    row = jnp.repeat(jnp.arange(rows), GRID_W).astype(F32)
    col = jnp.tile(jnp.arange(GRID_W), rows).astype(F32)
    quarter = HEAD_DIM // 4
    inv = ROPE_BASE ** (-jnp.arange(quarter, dtype=F32) / quarter)
    ang = jnp.concatenate([row[:, None] * inv, col[:, None] * inv], axis=-1)
    cos, sin = jnp.cos(ang), jnp.sin(ang)
    q = quarter
    cos_h = jnp.concatenate([cos[:, :q], cos[:, :q], cos[:, q:], cos[:, q:]], axis=-1)
    sin_h = jnp.concatenate([-sin[:, :q], sin[:, :q], -sin[:, q:], sin[:, q:]], axis=-1)
    return jnp.tile(cos_h, (1, 2)), jnp.tile(sin_h, (1, 2))


def kernel(x_prompt, x_sample, c, cache_k, cache_v, state_ret, c_ctx, norm_w, w_mod, b_mod, w_in, hy_conv,
           hy_filt_w1, hy_filt_b1, hy_filt_freq, hy_filt_w2, hy_skip, attn_sink, ret_theta, ret_gn,
           w_branch_a, w_branch_b, w_branch_c, w_merge, b_merge, w_out, final_norm_w):
    d = D_MODEL
    bc, lc, _ = x_prompt.shape
    bl, ll, _ = x_sample.shape
    assert ll == LAT_L and bc % 2 == 0 and bl % 2 == 0
    past = cache_k.shape[2]

    cond = jnp.zeros((16, d), F32).at[:bl].set(c).at[bl].set(c_ctx)
    mod = _mod_call(cond, w_mod, b_mod)

    g_ctx = _filt_ctx_call(lc, hy_filt_w1, hy_filt_b1, hy_filt_freq, hy_filt_w2, hy_skip)
    g_lat = _filt_lat_call(hy_filt_w1, hy_filt_b1, hy_filt_freq, hy_filt_w2, hy_skip)

    cos_t, sin_t = _rope_tables(ll)
    w_in_b = w_in[:, :, _IN_PERM].astype(BF16)
    wm_b = w_merge.astype(BF16)
    wa_b = w_branch_a.astype(BF16)
    wb_b = w_branch_b.astype(BF16)
    wc_b = w_branch_c.astype(BF16)
    wo_b = w_out.astype(BF16)
    fnw = final_norm_w.reshape(1, d)
    hy_cols = COL_HY // LANES
    nct = W_HY // LANES

    xp, xs = x_prompt, x_sample
    ks_out, vs_out, ss_out = [], [], []
    for l in range(DEPTH):
        final = l == DEPTH - 1
        nw = norm_w[l].reshape(1, d)
        bm = b_merge[l].reshape(1, -1)
        shift, scale, gate = (mod[l, :, i * d:(i + 1) * d][:, None, :] for i in range(3))
        conv_w = hy_conv[l]
        cw = lambda g: conv_w[:, g * W_HY:(g + 1) * W_HY]

        sl = slice(bl, bl + 1)
        u = _in_call(xp, shift[sl], scale[sl], nw, w_in_b[l], cos_t, sin_t, rope=False, tm=lc)
        ya = _hy_ctx_call(u, conv_w, g_ctx[l])
        yb = _attn_ctx_call(u, attn_sink[l])
        yc, sfin = _ret_call(u, ret_theta[l], ret_gn[l], None)
        res = _out_call(xp, shift[sl], scale[sl], gate[sl], nw, ya, yb, yc, wm_b[l], bm, wa_b[l], wb_b[l],
                        wc_b[l], wo_b[l], fnw, final=final, tm=lc)
        xp = res[0]
        if final:
            y_prompt = res[1]
        ks_out.append(u[:, :, COL_KA:COL_KA + W_KV].reshape(bc, lc, ATT_KV_HEADS, HEAD_DIM))
        vs_out.append(u[:, :, COL_VA:COL_VA + W_KV].reshape(bc, lc, ATT_KV_HEADS, HEAD_DIM))
        ss_out.append(_diag_states(sfin))

        sl = slice(0, bl)
        u = _in_call(xs, shift[sl], scale[sl], nw, w_in_b[l], cos_t, sin_t, rope=True, tm=256)
        z1 = _hy_lat_call(u, hy_cols, u, hy_cols + nct, hy_cols + 3 * nct,
                          jnp.stack([cw(0), cw(1)]), g_lat[l, 0], conv_in=True, gated=False)
        ya = _hy_lat_call(z1, 0, u, hy_cols + 2 * nct, hy_cols + 3 * nct,
                          jnp.stack([cw(2), cw(2)]), g_lat[l, 1], conv_in=False, gated=True)
        yb = _attn_lat_call(u, cache_k[:, l].reshape(bl, past, W_KV), cache_v[:, l].reshape(bl, past, W_KV),
                            attn_sink[l])
        yc, _ = _ret_call(u, ret_theta[l], ret_gn[l], _blockdiag_states(state_ret[:, l]))
        res = _out_call(xs, shift[sl], scale[sl], gate[sl], nw, ya, yb, yc, wm_b[l], bm, wa_b[l], wb_b[l],
                        wc_b[l], wo_b[l], fnw, final=final, tm=256)
        xs = res[0]
        if final:
            y_sample = res[1]

    new_cache_k = jnp.stack(ks_out, axis=1)
    new_cache_v = jnp.stack(vs_out, axis=1)
    new_state_ret = jnp.stack(ss_out, axis=1)
    return (y_prompt, y_sample, new_cache_k, new_cache_v, new_state_ret)
```

```python
import functools
import math

import numpy as np
import jax
import jax.numpy as jnp
from jax import lax
from jax.experimental import pallas as pl
from jax.experimental.pallas import tpu as pltpu

F32 = jnp.float32
BF16 = jnp.bfloat16
HIGHEST = lax.Precision.HIGHEST

D_MODEL = 1024
DEPTH = 4
GRID_W = 64
W_HY = 512
HY_BANDS = 8
HY_POS_FEAT = 1 + 2 * HY_BANDS
HY_FILT_HID = 64
HY_DECAY_TARGET = 1e-2
HY_FAST_PCT = 0.3
HY_SLOW_PCT = 1.5
ATT_HEADS = 8
ATT_KV_HEADS = 2
ATT_GROUP = ATT_HEADS // ATT_KV_HEADS
HEAD_DIM = 64
W_ATT = ATT_HEADS * HEAD_DIM
W_KV = ATT_KV_HEADS * HEAD_DIM
ATT_BLOCK = 128
RET_HEADS = 8
RET_DIM = 64
W_RET = RET_HEADS * RET_DIM
RET_CHUNK = 128
ROPE_BASE = 10000.0
EPS = 1e-6
NEG = -1e30

LANES = 128
VMEM_LIMIT = 56 * 1024 * 1024

IN_DIM = 5376
COL_HY = 0
COL_GH = 1536
COL_QA = 2048
COL_GA = 2560
COL_QR = 3072
COL_KR = 3584
COL_VR = 4096
COL_GR = 4608
COL_KA = 5120
COL_VA = 5248
_IN_PERM = np.concatenate([
    np.arange(0, 2048), np.arange(2048, 2560), np.arange(2816, 3328), np.arange(3328, 3840),
    np.arange(3840, 4352), np.arange(4352, 4864), np.arange(4864, 5376), np.arange(2560, 2688),
    np.arange(2688, 2816)])

LAT_L = 4096
LAT_N = 2 * LAT_L
NA = 64
NB = 128
Y_PITCH = 136
X_PITCH = 136


def _cparams(sem):
    return pltpu.CompilerParams(dimension_semantics=sem, vmem_limit_bytes=VMEM_LIMIT)


def _split_np(a):
    a32 = np.asarray(a, np.float32)
    hi = a32.astype(BF16)
    lo = (a32 - hi.astype(np.float32)).astype(BF16)
    return jnp.asarray(hi), jnp.asarray(lo)


def _split(x):
    hi = x.astype(BF16)
    lo = (x - hi.astype(F32)).astype(BF16)
    return hi, lo


def _dot(a, b):
    return jnp.dot(a, b, preferred_element_type=F32)


def _dot3c(chi, clo, x):
    xh, xl = _split(x)
    return _dot(chi, xh) + _dot(clo, xh) + _dot(chi, xl)


def _dot1c(chi, x):
    return _dot(chi, x.astype(BF16))


def _silu(x):
    return x * jax.nn.sigmoid(x)


def _mod_kernel(c_ref, w_ref, b_ref, o_ref):
    s = _silu(c_ref[...])
    o_ref[0] = jnp.dot(s, w_ref[0], precision=HIGHEST, preferred_element_type=F32) + b_ref[0]


def _mod_call(cond, w_mod, b_mod):
    rows, d = cond.shape
    n = w_mod.shape[-1]
    tn = 1024
    return pl.pallas_call(
        _mod_kernel,
        grid=(DEPTH, n // tn),
        in_specs=[pl.BlockSpec((rows, d), lambda l, j: (0, 0)),
                  pl.BlockSpec((1, d, tn), lambda l, j: (l, 0, j)),
                  pl.BlockSpec((1, 1, tn), lambda l, j: (l, 0, j))],
        out_specs=pl.BlockSpec((1, rows, tn), lambda l, j: (l, 0, j)),
        out_shape=jax.ShapeDtypeStruct((DEPTH, rows, n), F32),
        compiler_params=_cparams(("arbitrary", "arbitrary")),
        name="adaln_mod",
    )(cond, w_mod, b_mod.reshape(DEPTH, 1, n))


def _modulated(x, nw, scale, shift):
    ms = jnp.mean(x * x, axis=-1, keepdims=True)
    h = x * lax.rsqrt(ms + EPS) * nw
    return h * (1.0 + scale) + shift


def _rope128(x, cos, sin_signed, first_half):
    up = pltpu.roll(x, LANES - 16, 1)
    dn = pltpu.roll(x, 16, 1)
    return x * cos + jnp.where(first_half, up, dn) * sin_signed


def _in_kernel(x_ref, shift_ref, scale_ref, nw_ref, w_ref, cos_ref, sin_ref, o_ref, *, rope):
    x = x_ref[0]
    tm = x.shape[0]
    hb = _modulated(x, nw_ref[...], scale_ref[0], shift_ref[0]).astype(BF16)
    if rope:
        cos = cos_ref[...]
        sin = sin_ref[...]
        lane = lax.broadcasted_iota(jnp.int32, (tm, LANES), 1)
        first_half = (lane % 32) < 16

    def seg(c0, width):
        return _dot(hb, w_ref[:, c0:c0 + width])

    def put_rope(c0, val, mul):
        for i in range(val.shape[1] // LANES):
            piece = val[:, i * LANES:(i + 1) * LANES]
            if rope:
                piece = _rope128(piece, cos, sin, first_half)
            if mul is not None:
                piece = piece * mul
            o_ref[0, :, c0 + i * LANES:c0 + (i + 1) * LANES] = piece

    for g in range(3):
        o_ref[0, :, COL_HY + g * 512:COL_HY + (g + 1) * 512] = seg(COL_HY + g * 512, 512)
    o_ref[0, :, COL_GH:COL_GH + 512] = _silu(seg(COL_GH, 512))
    put_rope(COL_QA, seg(COL_QA, 512), None)
    o_ref[0, :, COL_GA:COL_GA + 512] = _silu(seg(COL_GA, 512))
    put_rope(COL_QR, seg(COL_QR, 512), None)
    put_rope(COL_KR, seg(COL_KR, 512), RET_DIM ** -0.5)
    o_ref[0, :, COL_VR:COL_VR + 512] = seg(COL_VR, 512)
    o_ref[0, :, COL_GR:COL_GR + 512] = _silu(seg(COL_GR, 512))
    put_rope(COL_KA, seg(COL_KA, 128), None)
    o_ref[0, :, COL_VA:COL_VA + 128] = seg(COL_VA, 128)


def _in_call(x, shift, scale, nw, w, cos_t, sin_t, *, rope, tm):
    b, l, d = x.shape
    per_batch = shift.shape[0] > 1
    mod_map = (lambda i, j: (i, 0, 0)) if per_batch else (lambda i, j: (0, 0, 0))
    return pl.pallas_call(
        functools.partial(_in_kernel, rope=rope),
        grid=(b, l // tm),
        in_specs=[pl.BlockSpec((1, tm, d), lambda i, j: (i, j, 0)),
                  pl.BlockSpec((1, 1, d), mod_map),
                  pl.BlockSpec((1, 1, d), mod_map),
                  pl.BlockSpec((1, d), lambda i, j: (0, 0)),
                  pl.BlockSpec((d, IN_DIM), lambda i, j: (0, 0)),
                  pl.BlockSpec((tm, LANES), lambda i, j: (j, 0)),
                  pl.BlockSpec((tm, LANES), lambda i, j: (j, 0))],
        out_specs=pl.BlockSpec((1, tm, IN_DIM), lambda i, j: (i, j, 0)),
        out_shape=jax.ShapeDtypeStruct((b, l, IN_DIM), F32),
        compiler_params=_cparams(("parallel", "parallel")),
        name="in_proj_rope" if rope else "in_proj",
    )(x, shift, scale, nw, w, cos_t, sin_t)


def _out_kernel(x_ref, shift_ref, scale_ref, gate_ref, nw_ref, ya_ref, gh_ref, yb_ref, yc_ref, wm_ref, bm_ref,
                wa_ref, wb_ref, wc_ref, wo_ref, fnw_ref, *out_refs, final):
    x = x_ref[0]
    d = x.shape[1]
    hb = _modulated(x, nw_ref[...], scale_ref[0], shift_ref[0]).astype(BF16)
    branches = (ya_ref[0] * gh_ref[0], yb_ref[0], yc_ref[0])
    merged = None
    for i, (y, w_ref) in enumerate(zip(branches, (wa_ref, wb_ref, wc_ref))):
        g = jax.nn.sigmoid(_dot(hb, wm_ref[:, i * d:(i + 1) * d]) + bm_ref[:, i * d:(i + 1) * d])
        term = g * _dot(y.astype(BF16), w_ref[...])
        merged = term if merged is None else merged + term
    out = _dot(merged.astype(BF16), wo_ref[...])
    xn = x + gate_ref[0] * out
    out_refs[0][0] = xn
    if final:
        ms = jnp.mean(xn * xn, axis=-1, keepdims=True)
        out_refs[1][0] = xn * lax.rsqrt(ms + EPS) * fnw_ref[...]


def _out_call(x, shift, scale, gate, nw, ya, u, yb, yc, wm, bm, wa, wb, wc, wo, fnw, *, final, tm):
    b, l, d = x.shape
    per_batch = shift.shape[0] > 1
    mod_map = (lambda i, j: (i, 0, 0)) if per_batch else (lambda i, j: (0, 0, 0))
    tok = lambda w: pl.BlockSpec((1, tm, w), lambda i, j: (i, j, 0))
    full = lambda a: pl.BlockSpec(a.shape, lambda i, j: (0,) * a.ndim)
    n_out = 2 if final else 1
    res = pl.pallas_call(
        functools.partial(_out_kernel, final=final),
        grid=(b, l // tm),
        in_specs=[tok(d), pl.BlockSpec((1, 1, d), mod_map), pl.BlockSpec((1, 1, d), mod_map),
                  pl.BlockSpec((1, 1, d), mod_map), full(nw), tok(W_HY),
                  pl.BlockSpec((1, tm, W_HY), lambda i, j: (i, j, COL_GH // W_HY)), tok(W_ATT), tok(W_RET),
                  full(wm), full(bm), full(wa), full(wb), full(wc), full(wo), full(fnw)],
        out_specs=[tok(d)] * n_out,
        out_shape=[jax.ShapeDtypeStruct((b, l, d), F32)] * n_out,
        compiler_params=_cparams(("parallel", "parallel")),
        name="merge_out_final" if final else "merge_out",
    )(x, shift, scale, gate, nw, ya, u, yb, yc, wm, bm, wa, wb, wc, wo, fnw)
    return res


_NT = (((1,), (1,)), ((), ()))


def _attn_scores(q, kh, kv):
    h0 = kv * ATT_GROUP
    qs = jnp.concatenate([q[:, (h0 + g) * HEAD_DIM:(h0 + g + 1) * HEAD_DIM] for g in range(ATT_GROUP)],
                         axis=0).astype(BF16)
    return lax.dot_general(qs, kh, _NT, preferred_element_type=F32)


def _attn_finish(sink_ref, s, vh, g_ref, o_ref, kv):
    t = s.shape[0] // ATT_GROUP
    h0 = kv * ATT_GROUP
    head = lax.broadcasted_iota(jnp.int32, (ATT_GROUP * t, 1), 0) // t
    sink = jnp.full((ATT_GROUP * t, 1), sink_ref[h0], F32)
    for g in range(1, ATT_GROUP):
        sink = jnp.where(head == g, sink_ref[h0 + g], sink)
    m = jnp.maximum(jnp.max(s, axis=-1, keepdims=True), sink)
    p = jnp.exp(s - m)
    denom = jnp.sum(p, axis=-1, keepdims=True) + jnp.exp(sink - m)
    o = _dot(p.astype(BF16), vh) / denom
    for gp in range(ATT_GROUP // 2):
        pair = jnp.concatenate([o[(2 * gp) * t:(2 * gp + 1) * t], o[(2 * gp + 1) * t:(2 * gp + 2) * t]],
                               axis=1)
        c0 = (h0 + 2 * gp) * HEAD_DIM
        o_ref[0, :, c0:c0 + 2 * HEAD_DIM] = pair * g_ref[0, :, c0:c0 + 2 * HEAD_DIM]


def _attn_ctx_kernel(sink_ref, q_ref, k_ref, v_ref, g_ref, o_ref):
    q = q_ref[0] * (HEAD_DIM ** -0.5)
    k = k_ref[0].astype(BF16)
    v = v_ref[0].astype(BF16)
    scores = [_attn_scores(q, k[:, kv * HEAD_DIM:(kv + 1) * HEAD_DIM], kv) for kv in range(ATT_KV_HEADS)]
    for kv in range(ATT_KV_HEADS):
        _attn_finish(sink_ref, scores[kv], v[:, kv * HEAD_DIM:(kv + 1) * HEAD_DIM], g_ref, o_ref, kv)


def _attn_ctx_call(u, sink):
    b, l, _ = u.shape
    return pl.pallas_call(
        _attn_ctx_kernel,
        grid=(b,),
        in_specs=[pl.BlockSpec(memory_space=pltpu.SMEM),
                  pl.BlockSpec((1, l, W_ATT), lambda i: (i, 0, COL_QA // W_ATT)),
                  pl.BlockSpec((1, l, W_KV), lambda i: (i, 0, COL_KA // W_KV)),
                  pl.BlockSpec((1, l, W_KV), lambda i: (i, 0, COL_VA // W_KV)),
                  pl.BlockSpec((1, l, W_ATT), lambda i: (i, 0, COL_GA // W_ATT))],
        out_specs=pl.BlockSpec((1, l, W_ATT), lambda i: (i, 0, 0)),
        out_shape=jax.ShapeDtypeStruct((b, l, W_ATT), F32),
        compiler_params=_cparams(("parallel",)),
        name="attn_ctx",
    )(sink, u, u, u, u)


def _attn_lat_kernel(sink_ref, q_ref, kp_ref, kc_ref, kn_ref, vp_ref, vc_ref, vn_ref, kx_ref, vx_ref,
                     g_ref, o_ref):
    i = pl.program_id(1)
    nb = pl.num_programs(1)
    q = q_ref[0] * (HEAD_DIM ** -0.5)
    kall = jnp.concatenate([kp_ref[0], kc_ref[0], kn_ref[0], kx_ref[0]], axis=0).astype(BF16)
    vall = jnp.concatenate([vp_ref[0], vc_ref[0], vn_ref[0], vx_ref[0]], axis=0).astype(BF16)
    rows = ATT_GROUP * ATT_BLOCK
    r = lax.broadcasted_iota(jnp.int32, (rows, ATT_BLOCK), 0) % ATT_BLOCK
    c = lax.broadcasted_iota(jnp.int32, (rows, ATT_BLOCK), 1)
    ok_prev = (c >= r) & (i > 0)
    ok_next = (c <= r) & (i < nb - 1)

    def band(s):
        b = ATT_BLOCK
        return jnp.concatenate([jnp.where(ok_prev, s[:, :b], NEG), s[:, b:2 * b],
                                jnp.where(ok_next, s[:, 2 * b:3 * b], NEG), s[:, 3 * b:]], axis=1)

    scores = [band(_attn_scores(q, kall[:, kv * HEAD_DIM:(kv + 1) * HEAD_DIM], kv))
              for kv in range(ATT_KV_HEADS)]
    for kv in range(ATT_KV_HEADS):
        _attn_finish(sink_ref, scores[kv], vall[:, kv * HEAD_DIM:(kv + 1) * HEAD_DIM], g_ref, o_ref, kv)


def _attn_lat_call(u, kctx, vctx, sink, layer):
    b, l, _ = u.shape
    nb = l // ATT_BLOCK
    past = kctx.shape[2]
    kcol = COL_KA // W_KV
    vcol = COL_VA // W_KV
    prev = lambda col: pl.BlockSpec((1, ATT_BLOCK, W_KV), lambda i, j: (i, jnp.maximum(j - 1, 0), col))
    cur = lambda col: pl.BlockSpec((1, ATT_BLOCK, W_KV), lambda i, j: (i, j, col))
    nxt = lambda col: pl.BlockSpec((1, ATT_BLOCK, W_KV), lambda i, j: (i, jnp.minimum(j + 1, nb - 1), col))
    ctx = pl.BlockSpec((1, None, past, W_KV), lambda i, j: (i, layer, 0, 0))
    return pl.pallas_call(
        _attn_lat_kernel,
        grid=(b, nb),
        in_specs=[pl.BlockSpec(memory_space=pltpu.SMEM),
                  pl.BlockSpec((1, ATT_BLOCK, W_ATT), lambda i, j: (i, j, COL_QA // W_ATT)),
                  prev(kcol), cur(kcol), nxt(kcol), prev(vcol), cur(vcol), nxt(vcol), ctx, ctx,
                  pl.BlockSpec((1, ATT_BLOCK, W_ATT), lambda i, j: (i, j, COL_GA // W_ATT))],
        out_specs=pl.BlockSpec((1, ATT_BLOCK, W_ATT), lambda i, j: (i, j, 0)),
        out_shape=jax.ShapeDtypeStruct((b, l, W_ATT), F32),
        compiler_params=_cparams(("parallel", "parallel")),
        name="attn_lat",
    )(sink, u, u, u, u, u, u, u, kctx, vctx, u)


def _log_sigmoid(x):
    return jnp.minimum(x, 0.0) - jnp.log1p(jnp.exp(-jnp.abs(x)))


def _ret_kernel(q_ref, k_ref, v_ref, g_ref, thl_ref, thb_ref, gn_ref, s0_ref, o_ref, sfin_ref, ob_ref, *,
                nc, has_s0):
    hp = pl.program_id(1)
    c = RET_CHUNK
    lane = lax.broadcasted_iota(jnp.int32, (1, LANES), 1)
    lo_head = lane < RET_DIM
    rowf = lax.broadcasted_iota(jnp.int32, (c, LANES), 0).astype(F32)
    ii = lax.broadcasted_iota(jnp.int32, (c, c), 0)
    jj = lax.broadcasted_iota(jnp.int32, (c, c), 1)
    dd = lax.broadcasted_iota(jnp.int32, (LANES, LANES), 0)
    ee = lax.broadcasted_iota(jnp.int32, (LANES, LANES), 1)
    same_head = (dd < RET_DIM) == (ee < RET_DIM)
    gn = gn_ref[...]

    def tables(direction):
        lg_lane = _log_sigmoid(thl_ref[direction])
        dmats = []
        for hh in range(2):
            lg_h = _log_sigmoid(thb_ref[direction, pl.ds(2 * hp + hh, 1), :])
            if direction == 0:
                dist = ii - jj
            else:
                dist = jj - ii
            dmats.append(jnp.where(dist >= 0, jnp.exp(lg_h * jnp.maximum(dist, 0).astype(F32)), 0.0))
        if direction == 0:
            q_dec = jnp.exp(lg_lane * (rowf + 1.0))
            k_dec = jnp.exp(lg_lane * (c - 1.0 - rowf))
        else:
            q_dec = jnp.exp(lg_lane * (c - rowf))
            k_dec = jnp.exp(lg_lane * rowf)
        c_dec = jnp.exp(lg_lane * float(c))
        return jnp.concatenate(dmats, axis=0), q_dec, k_dec, c_dec

    def first_level(tabs, r0):
        dmat2, _, k_dec, _ = tabs
        qc = q_ref[0, pl.ds(r0, c), :]
        kc = k_ref[0, pl.ds(r0, c), :]
        vcb = v_ref[0, pl.ds(r0, c), :].astype(BF16)
        qs = jnp.concatenate([jnp.where(lo_head, qc, 0.0), jnp.where(lo_head, 0.0, qc)], axis=0)
        sc = lax.dot_general(qs.astype(BF16), kc.astype(BF16), _NT, preferred_element_type=F32) * dmat2
        upd = jnp.where(same_head, _dot((kc * k_dec).T.astype(BF16), vcb), 0.0)
        return qc, vcb, sc.astype(BF16), upd

    def second_level(tabs, lvl1, s):
        _, q_dec, _, c_dec = tabs
        qc, vcb, scb, upd = lvl1
        pv = _dot(scb, vcb)
        o = _dot((qc * q_dec).astype(BF16), s.astype(BF16)) + jnp.where(lo_head, pv[:c], pv[c:])
        return o, c_dec * s + upd

    def init_state(direction):
        if has_s0:
            return s0_ref[0, direction, 0]
        return jnp.zeros((LANES, LANES), F32)

    tabs_f = tables(0)
    tabs_b = tables(1)

    def scan_body(n, carry):
        s_f, s_b = carry
        rows_f = [pl.multiple_of((2 * n + j) * c, c) for j in range(2)]
        rows_b = [pl.multiple_of((nc - 1 - 2 * n - j) * c, c) for j in range(2)]
        lvl_f = [first_level(tabs_f, r) for r in rows_f]
        lvl_b = [first_level(tabs_b, r) for r in rows_b]
        for j in range(2):
            o_f, s_f = second_level(tabs_f, lvl_f[j], s_f)
            o_ref[0, pl.ds(rows_f[j], c), :] = o_f
            o_b, s_b = second_level(tabs_b, lvl_b[j], s_b)
            ob_ref[pl.ds(rows_b[j], c), :] = o_b
        return s_f, s_b

    assert nc % 2 == 0
    s_f, s_b = lax.fori_loop(0, nc // 2, scan_body, (init_state(0), init_state(1)))
    sfin_ref[0, 0, 0] = s_f
    sfin_ref[0, 1, 0] = s_b

    def norm_body(n, carry):
        r0 = pl.multiple_of(n * c, c)
        o = o_ref[0, pl.ds(r0, c), :] + ob_ref[pl.ds(r0, c), :]
        o2 = o * o
        s_lo = jnp.sum(jnp.where(lo_head, o2, 0.0), axis=-1, keepdims=True)
        s_hi = jnp.sum(jnp.where(lo_head, 0.0, o2), axis=-1, keepdims=True)
        ms = jnp.where(lo_head, s_lo, s_hi) * (1.0 / RET_DIM)
        o_ref[0, pl.ds(r0, c), :] = o * lax.rsqrt(ms + EPS) * gn * g_ref[0, pl.ds(r0, c), :]
        return carry

    lax.fori_loop(0, nc, norm_body, 0, unroll=2)


def _ret_call(u, theta, gn, s0bd, layer=0):
    b, l, _ = u.shape
    nc = l // RET_CHUNK
    npair = RET_HEADS // 2
    has_s0 = s0bd is not None
    if not has_s0:
        s0bd = jnp.zeros((1, 2, 1, LANES, LANES), F32)
        s0_spec = pl.BlockSpec((1, 2, 1, LANES, LANES), lambda i, j: (0, 0, 0, 0, 0))
    else:
        s0_spec = pl.BlockSpec((1, None, 2, 1, LANES, LANES), lambda i, j: (i, layer, 0, j, 0, 0))
    th_lane = jnp.repeat(theta, RET_DIM, axis=1).reshape(2, 1, W_RET)
    th_bcast = jnp.broadcast_to(theta[:, :, None], (2, RET_HEADS, LANES))
    col = lambda c0: pl.BlockSpec((1, l, LANES), lambda i, j: (i, 0, c0 // LANES + j))
    o, sfin = pl.pallas_call(
        functools.partial(_ret_kernel, nc=nc, has_s0=has_s0),
        grid=(b, npair),
        in_specs=[col(COL_QR), col(COL_KR), col(COL_VR), col(COL_GR),
                  pl.BlockSpec((2, 1, LANES), lambda i, j: (0, 0, j)),
                  pl.BlockSpec((2, RET_HEADS, LANES), lambda i, j: (0, 0, 0)),
                  pl.BlockSpec((1, LANES), lambda i, j: (0, j)),
                  s0_spec],
        out_specs=[pl.BlockSpec((1, l, LANES), lambda i, j: (i, 0, j)),
                   pl.BlockSpec((1, 2, 1, LANES, LANES), lambda i, j: (i, 0, j, 0, 0))],
        out_shape=[jax.ShapeDtypeStruct((b, l, W_RET), F32),
                   jax.ShapeDtypeStruct((b, 2, npair, LANES, LANES), F32)],
        scratch_shapes=[pltpu.VMEM((l, LANES), F32)],
        compiler_params=_cparams(("parallel", "parallel")),
        name="retention_s0" if has_s0 else "retention",
    )(u, u, u, u, th_lane, th_bcast, gn.reshape(1, W_RET), s0bd)
    return o, sfin


def _blockdiag_states(s):
    sp = s.reshape(s.shape[:-3] + (RET_HEADS // 2, 2, RET_DIM, RET_DIM))
    z = jnp.zeros_like(sp[..., 0, :, :])
    top = jnp.concatenate([sp[..., 0, :, :], z], axis=-1)
    bot = jnp.concatenate([z, sp[..., 1, :, :]], axis=-1)
    return jnp.concatenate([top, bot], axis=-2)


def _diag_states(sbd):
    b = sbd.shape[0]
    s0 = sbd[:, :, :, :RET_DIM, :RET_DIM]
    s1 = sbd[:, :, :, RET_DIM:, RET_DIM:]
    return jnp.stack([s0, s1], axis=3).reshape(b, 2, RET_HEADS, RET_DIM, RET_DIM)


def _filter_positions(l):
    t = jnp.linspace(0.0, 1.0, l, dtype=F32)[:, None]
    w = 2.0 * math.pi * jnp.arange(l, dtype=F32)[:, None] / l
    f = jnp.linspace(1e-4, HY_BANDS - 1, HY_BANDS, dtype=F32)[None, :]
    z = jnp.concatenate([t, jnp.cos(f * w), -jnp.sin(f * w)], axis=-1)
    z = jnp.pad(z, ((0, 0), (0, 32 - HY_POS_FEAT)))
    return jnp.concatenate([z, z[:1], jnp.flip(z[1:], axis=0)], axis=0)


def _hyena_deltas():
    max_decay = math.log(HY_DECAY_TARGET) / HY_FAST_PCT
    min_decay = math.log(HY_DECAY_TARGET) / HY_SLOW_PCT
    return jnp.abs(jnp.linspace(min_decay, max_decay, W_HY, dtype=F32))[None, :]


def _filter_hidden(z_ref, w1_ref, b1_ref, fr_ref):
    pre = jnp.dot(z_ref[...], w1_ref[0], precision=HIGHEST, preferred_element_type=F32) + b1_ref[0]
    return jnp.sin(fr_ref[0] * pre)


def _filter_raw(hid, w2f, w2b, tp, dl, row0, l):
    win = jnp.exp(-tp * dl)
    row = row0 + lax.broadcasted_iota(jnp.int32, win.shape, 0)
    hf = jnp.dot(hid, w2f, precision=HIGHEST, preferred_element_type=F32) * win
    hb = jnp.dot(hid, w2b, precision=HIGHEST, preferred_element_type=F32) * win
    hf = jnp.where(row < l, hf, 0.0)
    hb = jnp.where((row > l) | (row == 0), hb, 0.0)
    return hf + hb, jnp.sum(jnp.abs(hf) + jnp.abs(hb), axis=0, keepdims=True)


def _with_skip(g, skip):
    row = lax.broadcasted_iota(jnp.int32, g.shape, 0)
    return g + jnp.where(row == 0, skip, 0.0)


def _filt_ctx_kernel(z_ref, w1_ref, b1_ref, fr_ref, w2_ref, dl_ref, sk_ref, fh_ref, fl_ref, g_ref):
    hid = _filter_hidden(z_ref, w1_ref, b1_ref, fr_ref)
    tp = z_ref[:, 0:1]
    for o in range(2):
        w2f = w2_ref[0, :, (2 * o) * W_HY:(2 * o + 1) * W_HY]
        w2b = w2_ref[0, :, (2 * o + 1) * W_HY:(2 * o + 2) * W_HY]
        raw, nrm = _filter_raw(hid, w2f, w2b, tp, dl_ref[...], 0, z_ref.shape[0] // 2)
        g = _with_skip(raw / nrm, sk_ref[0, pl.ds(o, 1), :])
        g_ref[0, o] = _dot3c(fh_ref[...], fl_ref[...], g)


def _ctx_dft_tables(l):
    n = 2 * l
    k = np.arange(n)[:, None]
    t = np.arange(l)[None, :]
    ang = 2.0 * np.pi * k * t / n
    c, s = np.cos(ang), np.sin(ang)
    fwd = np.block([[c, s], [-s, c]])
    inv = np.block([[c.T, -s.T], [s.T, c.T]])
    n_all = np.arange(n)[None, :]
    angg = 2.0 * np.pi * k * n_all / n
    filt = np.concatenate([np.cos(angg), -np.sin(angg)], axis=0) / n
    return _split_np(fwd), _split_np(inv), _split_np(filt)


def _filt_ctx_call(l, w1, b1, freq, w2, skip):
    n = 2 * l
    z_ext = _filter_positions(l)
    _, _, (fh, fl) = _ctx_dft_tables(l)
    w1p = jnp.pad(w1, ((0, 0), (0, 32 - HY_POS_FEAT), (0, 0)))
    lay = lambda *shape: pl.BlockSpec((1,) + shape, lambda d: (d,) + (0,) * len(shape))
    full = lambda a: pl.BlockSpec(a.shape, lambda d: (0,) * a.ndim)
    dl = _hyena_deltas()
    return pl.pallas_call(
        _filt_ctx_kernel,
        grid=(DEPTH,),
        in_specs=[full(z_ext), lay(32, HY_FILT_HID), lay(1, HY_FILT_HID), lay(1, HY_FILT_HID),
                  lay(HY_FILT_HID, 4 * W_HY), full(dl), lay(2, W_HY), full(fh), full(fl)],
        out_specs=pl.BlockSpec((1, 2, 2 * n, W_HY), lambda d: (d, 0, 0, 0)),
        out_shape=jax.ShapeDtypeStruct((DEPTH, 2, 2 * n, W_HY), F32),
        compiler_params=_cparams(("arbitrary",)),
        name="hyena_filter_ctx",
    )(z_ext, w1p, b1.reshape(DEPTH, 1, -1), freq.reshape(DEPTH, 1, -1), w2, dl, skip, fh, fl)


def _lat_dft_tables():
    ka = np.arange(NA)[:, None]
    b = np.arange(NB)[:, None, None]
    kb = np.arange(NB)[:, None]
    bb = np.arange(NB)[None, :]
    a_half = np.arange(NA // 2)[None, :]
    a_full = np.arange(NA)[None, :]
    phi = 2.0 * np.pi * (ka * a_half / NA + b * ka / LAT_N)
    c, s = np.cos(phi), np.sin(phi)
    a_fwd = np.concatenate([c, s], axis=2)
    ct, st = np.swapaxes(c, 1, 2), np.swapaxes(s, 1, 2)
    a_inv = np.concatenate([ct, st], axis=2)
    phig = 2.0 * np.pi * (ka * a_full / NA + b * ka / LAT_N)
    a_flt = np.concatenate([np.cos(phig), -np.sin(phig)], axis=1) / LAT_N
    ang = 2.0 * np.pi * kb * bb / NB
    c2, s2 = np.cos(ang), np.sin(ang)
    f_fwd = np.block([[c2, s2], [-s2, c2]])
    f_inv = np.block([[c2, -s2], [s2, c2]])
    return (_split_np(a_fwd), _split_np(a_inv), _split_np(a_flt), _split_np(f_fwd), _split_np(f_inv))


def _stage_b_rows(ka):
    re = pl.ds(ka, NB, stride=Y_PITCH)
    im = pl.ds(NA + ka, NB, stride=Y_PITCH)
    return re, im


def _filt_lat_kernel(z_ref, w1_ref, b1_ref, fr_ref, w2f_ref, w2b_ref, dl_ref, sk_ref, ah_ref, al_ref,
                     f2h_ref, f2l_ref, g_ref, hid_ref, gt_ref, y_ref):
    step = pl.program_id(1)
    rch = 1024
    nch = LAT_N // rch
    rows_of = lambda i: pl.ds(pl.multiple_of(i * rch, rch), rch)

    @pl.when(step == 0)
    def _():
        def hid_chunk(i, carry):
            r = rows_of(i)
            pre = jnp.dot(z_ref[r, :], w1_ref[0], precision=HIGHEST, preferred_element_type=F32)
            hid_ref[r, :] = jnp.sin(fr_ref[0] * (pre + b1_ref[0]))
            return carry

        lax.fori_loop(0, nch, hid_chunk, 0)

    def raw_chunk(i, nrm):
        r = rows_of(i)
        raw, part = _filter_raw(hid_ref[r, :], w2f_ref[0], w2b_ref[0], z_ref[r, 0:1], dl_ref[...],
                                i * rch, LAT_L)
        gt_ref[r, :] = raw
        return nrm + part

    nrm = lax.fori_loop(0, nch, raw_chunk, jnp.zeros((1, LANES), F32))

    def norm_chunk(i, carry):
        r = rows_of(i)
        gt_ref[r, :] = gt_ref[r, :] / nrm
        return carry

    lax.fori_loop(0, nch, norm_chunk, 0)
    order = step // (W_HY // LANES)
    gt_ref[0:8, :] = _with_skip(gt_ref[0:8, :], sk_ref[0, pl.ds(order, 1), :])

    def stage_a(b, carry):
        rows = gt_ref[pl.ds(b, NA, stride=NB), :]
        y_ref[pl.ds(pl.multiple_of(b * Y_PITCH, 8), 2 * NA), :] = _dot3c(ah_ref[b], al_ref[b], rows)
        return carry

    lax.fori_loop(0, NB, stage_a, 0, unroll=4)

    def stage_b(ka, carry):
        re, im = _stage_b_rows(ka)
        z = jnp.concatenate([y_ref[re, :], y_ref[im, :]], axis=0)
        g_ref[0, 0, pl.ds(pl.multiple_of(ka * 2 * NB, 2 * NB), 2 * NB), :] = _dot3c(
            f2h_ref[...], f2l_ref[...], z)
        return carry

    lax.fori_loop(0, NA, stage_b, 0, unroll=4)


def _filt_lat_call(w1, b1, freq, w2, skip):
    z_ext = _filter_positions(LAT_L)
    _, _, (ah, al), (f2h, f2l), _ = _lat_dft_tables()
    w1p = jnp.pad(w1, ((0, 0), (0, 32 - HY_POS_FEAT), (0, 0)))
    nct = W_HY // LANES
    one = pl.Buffered(1)
    lay = lambda *shape: pl.BlockSpec((1,) + shape, lambda d, s: (d,) + (0,) * len(shape))
    full = lambda a: pl.BlockSpec(a.shape, lambda d, s: (0,) * a.ndim, pipeline_mode=one)
    dl = _hyena_deltas()
    return pl.pallas_call(
        _filt_lat_kernel,
        grid=(DEPTH, 2 * nct),
        in_specs=[full(z_ext), lay(32, HY_FILT_HID), lay(1, HY_FILT_HID), lay(1, HY_FILT_HID),
                  pl.BlockSpec((1, HY_FILT_HID, LANES), lambda d, s: (d, 0, (s // nct) * 2 * nct + s % nct)),
                  pl.BlockSpec((1, HY_FILT_HID, LANES),
                               lambda d, s: (d, 0, (s // nct) * 2 * nct + nct + s % nct)),
                  pl.BlockSpec((1, LANES), lambda d, s: (0, s % nct)),
                  pl.BlockSpec((1, 2, LANES), lambda d, s: (d, 0, s % nct)),
                  full(ah), full(al), full(f2h), full(f2l)],
        out_specs=pl.BlockSpec((1, 1, NA * 2 * NB, LANES), lambda d, s: (d, s // nct, 0, s % nct)),
        out_shape=jax.ShapeDtypeStruct((DEPTH, 2, NA * 2 * NB, W_HY), F32),
        scratch_shapes=[pltpu.VMEM((LAT_N, HY_FILT_HID), F32), pltpu.VMEM((LAT_N, LANES), F32),
                        pltpu.VMEM((NB * Y_PITCH, LANES), F32)],
        compiler_params=_cparams(("arbitrary", "arbitrary")),
        name="hyena_filter_lat",
    )(z_ext, w1p, b1.reshape(DEPTH, 1, -1), freq.reshape(DEPTH, 1, -1), w2, w2, dl, skip, ah, al, f2h,
      f2l)


def _short_conv_rows(ref, bi, r0, rows, first, last, w):
    total = ref.shape[1]
    cur = ref[bi, pl.ds(r0, rows), :]
    before = ref[bi, pl.ds(jnp.maximum(r0 - 1, 0), 1), :]
    after = ref[bi, pl.ds(jnp.minimum(r0 + rows, total - 1), 1), :]
    before = jnp.where(first, 0.0, before)
    after = jnp.where(last, 0.0, after)
    rid = lax.broadcasted_iota(jnp.int32, cur.shape, 0)
    prev = jnp.where(rid == 0, before, pltpu.roll(cur, 1, 0))
    nxt = jnp.where(rid == rows - 1, after, pltpu.roll(cur, rows - 1, 0))
    return prev * w[0:1] + cur * w[1:2] + nxt * w[2:3]


def _cmul(xr, xi, gr, gi):
    return xr * gr - xi * gi, xr * gi + xi * gr


def _hy_ctx_kernel(v_ref, x1_ref, x2_ref, cw_ref, g_ref, fh_ref, ih_ref, o_ref):
    l = v_ref.shape[1]
    n = 2 * l

    def sc(ref, bi, grp):
        w = cw_ref[:, grp * W_HY:(grp + 1) * W_HY]
        return _short_conv_rows(ref, bi, 0, l, True, True, w)

    def conv(zr, zi, order):
        x = _dot1c(fh_ref[...], jnp.concatenate([zr, zi], axis=0))
        pr, pi = _cmul(x[:n], x[n:], g_ref[order, :n], g_ref[order, n:])
        y = _dot1c(ih_ref[...], jnp.concatenate([pr, pi], axis=0))
        return y[:l], y[l:]

    yr, yi = conv(sc(v_ref, 0, 0), sc(v_ref, 1, 0), 0)
    yr, yi = conv(sc(x1_ref, 0, 1) * yr, sc(x1_ref, 1, 1) * yi, 1)
    o_ref[0] = sc(x2_ref, 0, 2) * yr
    o_ref[1] = sc(x2_ref, 1, 2) * yi


def _hy_ctx_call(u, conv_w, g_spec, layer):
    b, l, _ = u.shape
    (fh, _), (ih, _), _ = _ctx_dft_tables(l)
    grp = lambda g: pl.BlockSpec((2, l, W_HY), lambda i: (i, 0, g))
    full = lambda a: pl.BlockSpec(a.shape, lambda i: (0,) * a.ndim)
    return pl.pallas_call(
        _hy_ctx_kernel,
        grid=(b // 2,),
        in_specs=[grp(0), grp(1), grp(2), full(conv_w),
                  pl.BlockSpec((None,) + g_spec.shape[1:], lambda i: (layer, 0, 0, 0)), full(fh), full(ih)],
        out_specs=pl.BlockSpec((2, l, W_HY), lambda i: (i, 0, 0)),
        out_shape=jax.ShapeDtypeStruct((b, l, W_HY), F32),
        compiler_params=_cparams(("parallel",)),
        name="hyena_ctx",
    )(u, u, u, conv_w, g_spec, fh, ih)


def _hy_lat_kernel(z_ref, m_ref, cw_ref, g_ref, af_ref, ai_ref, f2_ref, f3_ref, o_ref, xr_scr, xi_scr,
                   y_scr, *, conv_in):
    x_scr = (xr_scr, xi_scr)
    na_half = NA // 2
    w_in = cw_ref[0] if conv_in else None
    w_mul = cw_ref[1]

    def load_in(a, carry):
        for bi in range(2):
            r0 = pl.multiple_of(a * NB, NB)
            if conv_in:
                val = _short_conv_rows(z_ref, bi, r0, NB, a == 0, a == na_half - 1, w_in)
            else:
                val = z_ref[bi, pl.ds(r0, NB), :]
            x_scr[bi][pl.ds(pl.multiple_of(a * X_PITCH, 8), NB), :] = val
        return carry

    lax.fori_loop(0, na_half, load_in, 0)

    def stage_a(b, carry):
        zr = xr_scr[pl.ds(b, na_half, stride=X_PITCH), :]
        zi = xi_scr[pl.ds(b, na_half, stride=X_PITCH), :]
        rhs = jnp.concatenate([jnp.concatenate([zr, zi], axis=0), jnp.concatenate([zi, -zr], axis=0)],
                              axis=1)
        y = _dot1c(af_ref[b], rhs)
        r0 = pl.multiple_of(b * Y_PITCH, 8)
        y_scr[pl.ds(r0, NA), :] = y[:, :LANES]
        y_scr[pl.ds(r0 + NA, NA), :] = y[:, LANES:]
        return carry

    lax.fori_loop(0, NB, stage_a, 0, unroll=8)

    def stage_b(spectral, j, carry):
        ka = 2 * j
        re, im = _stage_b_rows(ka)
        re1, im1 = _stage_b_rows(ka + 1)
        z = jnp.concatenate([jnp.concatenate([y_scr[re, :], y_scr[im, :]], axis=0),
                             jnp.concatenate([y_scr[re1, :], y_scr[im1, :]], axis=0)], axis=1)
        if spectral:
            x = _dot1c(f2_ref[...], z)
            g0 = pl.multiple_of(ka * 2 * NB, 2 * NB)
            gr = jnp.concatenate([g_ref[pl.ds(g0, NB), :], g_ref[pl.ds(g0 + 2 * NB, NB), :]], axis=1)
            gi = jnp.concatenate([g_ref[pl.ds(g0 + NB, NB), :], g_ref[pl.ds(g0 + 3 * NB, NB), :]], axis=1)
            ur, ui = _cmul(x[:NB], x[NB:], gr, gi)
        else:
            u = _dot1c(f3_ref[...], z)
            ur, ui = u[:NB], u[NB:]
        y_scr[re, :] = ur[:, :LANES]
        y_scr[im, :] = ui[:, :LANES]
        y_scr[re1, :] = ur[:, LANES:]
        y_scr[im1, :] = ui[:, LANES:]
        return carry

    for spectral in (True, False):
        lax.fori_loop(0, NA // 2, functools.partial(stage_b, spectral), 0, unroll=4)

    def stage_c(b, carry):
        r0 = pl.multiple_of(b * Y_PITCH, 8)
        ur = y_scr[pl.ds(r0, NA), :]
        ui = y_scr[pl.ds(r0 + NA, NA), :]
        rhs = jnp.concatenate([jnp.concatenate([ur, -ui], axis=0), jnp.concatenate([ui, ur], axis=0)],
                              axis=1)
        y = _dot1c(ai_ref[b], rhs)
        xr_scr[pl.ds(b, na_half, stride=X_PITCH), :] = y[:, :LANES]
        xi_scr[pl.ds(b, na_half, stride=X_PITCH), :] = y[:, LANES:]
        return carry

    lax.fori_loop(0, NB, stage_c, 0, unroll=8)

    def store_out(a, carry):
        for bi in range(2):
            r0 = pl.multiple_of(a * NB, NB)
            mul = _short_conv_rows(m_ref, bi, r0, NB, a == 0, a == na_half - 1, w_mul)
            o_ref[bi, pl.ds(r0, NB), :] = x_scr[bi][pl.ds(pl.multiple_of(a * X_PITCH, 8), NB), :] * mul
        return carry

    lax.fori_loop(0, na_half, store_out, 0)


def _hy_lat_call(src, src_col, u, mul_col, conv_w2, g_spec, layer, order, *, conv_in):
    b, l, _ = u.shape
    nct = W_HY // LANES
    (af, _), (ai, _), _, (f2, _), (f3, _) = _lat_dft_tables()
    one = pl.Buffered(1)
    blk = lambda col: pl.BlockSpec((2, l, LANES), lambda c, p: (p, 0, col + c))
    const = lambda a: pl.BlockSpec(a.shape, lambda c, p: (0,) * a.ndim, pipeline_mode=one)
    return pl.pallas_call(
        functools.partial(_hy_lat_kernel, conv_in=conv_in),
        grid=(nct, b // 2),
        in_specs=[blk(src_col), blk(mul_col),
                  pl.BlockSpec((2, 3, LANES), lambda c, p: (0, 0, c)),
                  pl.BlockSpec((None, None, NA * 2 * NB, LANES), lambda c, p: (layer, order, 0, c),
                               pipeline_mode=one),
                  const(af), const(ai), const(f2), const(f3)],
        out_specs=pl.BlockSpec((2, l, LANES), lambda c, p: (p, 0, c)),
        out_shape=jax.ShapeDtypeStruct((b, l, W_HY), F32),
        scratch_shapes=[pltpu.VMEM(((NA // 2) * X_PITCH, LANES), F32),
                        pltpu.VMEM(((NA // 2) * X_PITCH, LANES), F32),
                        pltpu.VMEM((NB * Y_PITCH, LANES), F32)],
        compiler_params=_cparams(("arbitrary", "arbitrary")),
        name="hyena_lat_conv_in" if conv_in else "hyena_lat",
    )(src, u, conv_w2, g_spec, af, ai, f2, f3)


def _rope_tables(l):
    rows = l // GRID_W
    row = jnp.repeat(jnp.arange(rows), GRID_W).astype(F32)
    col = jnp.tile(jnp.arange(GRID_W), rows).astype(F32)
    quarter = HEAD_DIM // 4
    inv = ROPE_BASE ** (-jnp.arange(quarter, dtype=F32) / quarter)
    ang = jnp.concatenate([row[:, None] * inv, col[:, None] * inv], axis=-1)
    cos, sin = jnp.cos(ang), jnp.sin(ang)
    q = quarter
    cos_h = jnp.concatenate([cos[:, :q], cos[:, :q], cos[:, q:], cos[:, q:]], axis=-1)
    sin_h = jnp.concatenate([-sin[:, :q], sin[:, :q], -sin[:, q:], sin[:, q:]], axis=-1)
    return jnp.tile(cos_h, (1, 2)), jnp.tile(sin_h, (1, 2))


def kernel(x_prompt, x_sample, c, cache_k, cache_v, state_ret, c_ctx, norm_w, w_mod, b_mod, w_in, hy_conv,
           hy_filt_w1, hy_filt_b1, hy_filt_freq, hy_filt_w2, hy_skip, attn_sink, ret_theta, ret_gn,
           w_branch_a, w_branch_b, w_branch_c, w_merge, b_merge, w_out, final_norm_w):
    d = D_MODEL
    bc, lc, _ = x_prompt.shape
    bl, ll, _ = x_sample.shape
    assert ll == LAT_L and bc % 2 == 0 and bl % 2 == 0
    past = cache_k.shape[2]

    cond = jnp.zeros((16, d), F32).at[:bl].set(c).at[bl].set(c_ctx)
    mod = _mod_call(cond, w_mod, b_mod)

    g_ctx = _filt_ctx_call(lc, hy_filt_w1, hy_filt_b1, hy_filt_freq, hy_filt_w2, hy_skip)
    g_lat = _filt_lat_call(hy_filt_w1, hy_filt_b1, hy_filt_freq, hy_filt_w2, hy_skip)

    cos_t, sin_t = _rope_tables(ll)
    w_in_b = w_in[:, :, _IN_PERM].astype(BF16)
    wm_b = w_merge.astype(BF16)
    wa_b = w_branch_a.astype(BF16)
    wb_b = w_branch_b.astype(BF16)
    wc_b = w_branch_c.astype(BF16)
    wo_b = w_out.astype(BF16)
    fnw = final_norm_w.reshape(1, d)
    k_ctx = cache_k.reshape(bl, DEPTH, past, W_KV)
    v_ctx = cache_v.reshape(bl, DEPTH, past, W_KV)
    s0_all = _blockdiag_states(state_ret)
    hy_cols = COL_HY // LANES
    nct = W_HY // LANES

    xp, xs = x_prompt, x_sample
    ks_out, vs_out, ss_out = [], [], []
    for l in range(DEPTH):
        final = l == DEPTH - 1
        nw = norm_w[l].reshape(1, d)
        bm = b_merge[l].reshape(1, -1)
        shift, scale, gate = (mod[l, :, i * d:(i + 1) * d][:, None, :] for i in range(3))
        conv_w = hy_conv[l]
        cw = lambda g: conv_w[:, g * W_HY:(g + 1) * W_HY]

        sl = slice(bl, bl + 1)
        u = _in_call(xp, shift[sl], scale[sl], nw, w_in_b[l], cos_t, sin_t, rope=False, tm=lc)
        ya = _hy_ctx_call(u, conv_w, g_ctx, l)
        yb = _attn_ctx_call(u, attn_sink[l])
        yc, sfin = _ret_call(u, ret_theta[l], ret_gn[l], None)
        res = _out_call(xp, shift[sl], scale[sl], gate[sl], nw, ya, u, yb, yc, wm_b[l], bm, wa_b[l], wb_b[l],
                        wc_b[l], wo_b[l], fnw, final=final, tm=lc)
        xp = res[0]
        if final:
            y_prompt = res[1]
        ks_out.append(u[:, :, COL_KA:COL_KA + W_KV].reshape(bc, lc, ATT_KV_HEADS, HEAD_DIM))
        vs_out.append(u[:, :, COL_VA:COL_VA + W_KV].reshape(bc, lc, ATT_KV_HEADS, HEAD_DIM))
        ss_out.append(_diag_states(sfin))

        sl = slice(0, bl)
        u = _in_call(xs, shift[sl], scale[sl], nw, w_in_b[l], cos_t, sin_t, rope=True, tm=256)
        z1 = _hy_lat_call(u, hy_cols, u, hy_cols + nct, jnp.stack([cw(0), cw(1)]), g_lat, l, 0, conv_in=True)
        ya = _hy_lat_call(z1, 0, u, hy_cols + 2 * nct, jnp.stack([cw(2), cw(2)]), g_lat, l, 1, conv_in=False)
        yb = _attn_lat_call(u, k_ctx, v_ctx, attn_sink[l], l)
        yc, _ = _ret_call(u, ret_theta[l], ret_gn[l], s0_all, l)
        res = _out_call(xs, shift[sl], scale[sl], gate[sl], nw, ya, u, yb, yc, wm_b[l], bm, wa_b[l], wb_b[l],
                        wc_b[l], wo_b[l], fnw, final=final, tm=256)
        xs = res[0]
        if final:
            y_sample = res[1]

    new_cache_k = jnp.stack(ks_out, axis=1)
    new_cache_v = jnp.stack(vs_out, axis=1)
    new_state_ret = jnp.stack(ss_out, axis=1)
    return (y_prompt, y_sample, new_cache_k, new_cache_v, new_state_ret)
```

```python
import functools
import math

import numpy as np
import jax
import jax.numpy as jnp
from jax import lax
from jax.experimental import pallas as pl
from jax.experimental.pallas import tpu as pltpu

F32 = jnp.float32
BF16 = jnp.bfloat16
HIGHEST = lax.Precision.HIGHEST

D_MODEL = 1024
DEPTH = 4
GRID_W = 64
W_HY = 512
HY_BANDS = 8
HY_POS_FEAT = 1 + 2 * HY_BANDS
HY_FILT_HID = 64
HY_DECAY_TARGET = 1e-2
HY_FAST_PCT = 0.3
HY_SLOW_PCT = 1.5
ATT_HEADS = 8
ATT_KV_HEADS = 2
ATT_GROUP = ATT_HEADS // ATT_KV_HEADS
HEAD_DIM = 64
W_ATT = ATT_HEADS * HEAD_DIM
W_KV = ATT_KV_HEADS * HEAD_DIM
ATT_BLOCK = 128
RET_HEADS = 8
RET_DIM = 64
W_RET = RET_HEADS * RET_DIM
RET_CHUNK = 128
ROPE_BASE = 10000.0
EPS = 1e-6
NEG = -1e30

LANES = 128
VMEM_LIMIT = 56 * 1024 * 1024

IN_DIM = 5376
COL_HY = 0
COL_GH = 1536
COL_QA = 2048
COL_GA = 2560
COL_QR = 3072
COL_KR = 3584
COL_VR = 4096
COL_GR = 4608
COL_KA = 5120
COL_VA = 5248
_IN_SEGMENTS = ((0, 2560), (2816, 5376), (2560, 2816))


def _pack_w_in(w):
    return jnp.concatenate([w[..., a:b] for a, b in _IN_SEGMENTS], axis=-1).astype(BF16)

LAT_L = 4096
LAT_N = 2 * LAT_L
NA = 64
NB = 128
Y_PITCH = 136
X_PITCH = 136


def _cparams(sem):
    return pltpu.CompilerParams(dimension_semantics=sem, vmem_limit_bytes=VMEM_LIMIT)


def _split_np(a):
    a32 = np.asarray(a, np.float32)
    hi = a32.astype(BF16)
    lo = (a32 - hi.astype(np.float32)).astype(BF16)
    return jnp.asarray(hi), jnp.asarray(lo)


def _split(x):
    hi = x.astype(BF16)
    lo = (x - hi.astype(F32)).astype(BF16)
    return hi, lo


def _dot(a, b):
    return jnp.dot(a, b, preferred_element_type=F32)


def _dot3c(chi, clo, x):
    xh, xl = _split(x)
    return _dot(chi, xh) + _dot(clo, xh) + _dot(chi, xl)


def _dot1c(chi, x):
    return _dot(chi, x.astype(BF16))


def _silu(x):
    return x * jax.nn.sigmoid(x)


def _mod_kernel(c_ref, w_ref, b_ref, o_ref):
    s = _silu(c_ref[...])
    o_ref[0] = jnp.dot(s, w_ref[0], precision=HIGHEST, preferred_element_type=F32) + b_ref[0]


def _mod_call(cond, w_mod, b_mod):
    rows, d = cond.shape
    n = w_mod.shape[-1]
    tn = 1024
    return pl.pallas_call(
        _mod_kernel,
        grid=(DEPTH, n // tn),
        in_specs=[pl.BlockSpec((rows, d), lambda l, j: (0, 0)),
                  pl.BlockSpec((1, d, tn), lambda l, j: (l, 0, j)),
                  pl.BlockSpec((1, 1, tn), lambda l, j: (l, 0, j))],
        out_specs=pl.BlockSpec((1, rows, tn), lambda l, j: (l, 0, j)),
        out_shape=jax.ShapeDtypeStruct((DEPTH, rows, n), F32),
        compiler_params=_cparams(("arbitrary", "arbitrary")),
        name="adaln_mod",
    )(cond, w_mod, b_mod.reshape(DEPTH, 1, n))


def _modulated(x, nw, scale, shift):
    ms = jnp.mean(x * x, axis=-1, keepdims=True)
    h = x * lax.rsqrt(ms + EPS) * nw
    return h * (1.0 + scale) + shift


def _rope128(x, cos, sin_signed, first_half):
    up = pltpu.roll(x, LANES - 16, 1)
    dn = pltpu.roll(x, 16, 1)
    return x * cos + jnp.where(first_half, up, dn) * sin_signed


def _in_kernel(x_ref, shift_ref, scale_ref, nw_ref, w_ref, cos_ref, sin_ref, o_ref, *, rope):
    x = x_ref[0]
    tm = x.shape[0]
    hb = _modulated(x, nw_ref[...], scale_ref[0], shift_ref[0]).astype(BF16)
    if rope:
        cos = cos_ref[...]
        sin = sin_ref[...]
        lane = lax.broadcasted_iota(jnp.int32, (tm, LANES), 1)
        first_half = (lane % 32) < 16

    def seg(c0, width):
        return _dot(hb, w_ref[:, c0:c0 + width])

    def put_rope(c0, val, mul):
        for i in range(val.shape[1] // LANES):
            piece = val[:, i * LANES:(i + 1) * LANES]
            if rope:
                piece = _rope128(piece, cos, sin, first_half)
            if mul is not None:
                piece = piece * mul
            o_ref[0, :, c0 + i * LANES:c0 + (i + 1) * LANES] = piece

    for g in range(3):
        o_ref[0, :, COL_HY + g * 512:COL_HY + (g + 1) * 512] = seg(COL_HY + g * 512, 512)
    o_ref[0, :, COL_GH:COL_GH + 512] = _silu(seg(COL_GH, 512))
    put_rope(COL_QA, seg(COL_QA, 512), None)
    o_ref[0, :, COL_GA:COL_GA + 512] = _silu(seg(COL_GA, 512))
    put_rope(COL_QR, seg(COL_QR, 512), None)
    put_rope(COL_KR, seg(COL_KR, 512), RET_DIM ** -0.5)
    o_ref[0, :, COL_VR:COL_VR + 512] = seg(COL_VR, 512)
    o_ref[0, :, COL_GR:COL_GR + 512] = _silu(seg(COL_GR, 512))
    put_rope(COL_KA, seg(COL_KA, 128), None)
    o_ref[0, :, COL_VA:COL_VA + 128] = seg(COL_VA, 128)


def _in_call(x, shift, scale, nw, w, cos_t, sin_t, *, rope, tm):
    b, l, d = x.shape
    per_batch = shift.shape[0] > 1
    mod_map = (lambda i, j: (i, 0, 0)) if per_batch else (lambda i, j: (0, 0, 0))
    return pl.pallas_call(
        functools.partial(_in_kernel, rope=rope),
        grid=(b, l // tm),
        in_specs=[pl.BlockSpec((1, tm, d), lambda i, j: (i, j, 0)),
                  pl.BlockSpec((1, 1, d), mod_map),
                  pl.BlockSpec((1, 1, d), mod_map),
                  pl.BlockSpec((1, d), lambda i, j: (0, 0)),
                  pl.BlockSpec((d, IN_DIM), lambda i, j: (0, 0)),
                  pl.BlockSpec((tm, LANES), lambda i, j: (j, 0)),
                  pl.BlockSpec((tm, LANES), lambda i, j: (j, 0))],
        out_specs=pl.BlockSpec((1, tm, IN_DIM), lambda i, j: (i, j, 0)),
        out_shape=jax.ShapeDtypeStruct((b, l, IN_DIM), F32),
        compiler_params=_cparams(("parallel", "parallel")),
        name="in_proj_rope" if rope else "in_proj",
    )(x, shift, scale, nw, w, cos_t, sin_t)


def _out_kernel(x_ref, shift_ref, scale_ref, gate_ref, nw_ref, ya_ref, gh_ref, yb_ref, yc_ref, wm_ref, bm_ref,
                wa_ref, wb_ref, wc_ref, wo_ref, fnw_ref, *out_refs, final):
    x = x_ref[0]
    d = x.shape[1]
    hb = _modulated(x, nw_ref[...], scale_ref[0], shift_ref[0]).astype(BF16)
    branches = (ya_ref[0] * gh_ref[0], yb_ref[0], yc_ref[0])
    merged = None
    for i, (y, w_ref) in enumerate(zip(branches, (wa_ref, wb_ref, wc_ref))):
        g = jax.nn.sigmoid(_dot(hb, wm_ref[:, i * d:(i + 1) * d]) + bm_ref[:, i * d:(i + 1) * d])
        term = g * _dot(y.astype(BF16), w_ref[...])
        merged = term if merged is None else merged + term
    out = _dot(merged.astype(BF16), wo_ref[...])
    xn = x + gate_ref[0] * out
    out_refs[0][0] = xn
    if final:
        ms = jnp.mean(xn * xn, axis=-1, keepdims=True)
        out_refs[1][0] = xn * lax.rsqrt(ms + EPS) * fnw_ref[...]


def _out_call(x, shift, scale, gate, nw, ya, u, yb, yc, wm, bm, wa, wb, wc, wo, fnw, *, final, tm):
    b, l, d = x.shape
    per_batch = shift.shape[0] > 1
    mod_map = (lambda i, j: (i, 0, 0)) if per_batch else (lambda i, j: (0, 0, 0))
    tok = lambda w: pl.BlockSpec((1, tm, w), lambda i, j: (i, j, 0))
    full = lambda a: pl.BlockSpec(a.shape, lambda i, j: (0,) * a.ndim)
    n_out = 2 if final else 1
    res = pl.pallas_call(
        functools.partial(_out_kernel, final=final),
        grid=(b, l // tm),
        in_specs=[tok(d), pl.BlockSpec((1, 1, d), mod_map), pl.BlockSpec((1, 1, d), mod_map),
                  pl.BlockSpec((1, 1, d), mod_map), full(nw), tok(W_HY),
                  pl.BlockSpec((1, tm, W_HY), lambda i, j: (i, j, COL_GH // W_HY)), tok(W_ATT), tok(W_RET),
                  full(wm), full(bm), full(wa), full(wb), full(wc), full(wo), full(fnw)],
        out_specs=[tok(d)] * n_out,
        out_shape=[jax.ShapeDtypeStruct((b, l, d), F32)] * n_out,
        compiler_params=_cparams(("parallel", "parallel")),
        name="merge_out_final" if final else "merge_out",
    )(x, shift, scale, gate, nw, ya, u, yb, yc, wm, bm, wa, wb, wc, wo, fnw)
    return res


_NT = (((1,), (1,)), ((), ()))


_TN = (((0,), (0,)), ((), ()))
LOG2E = 1.4426950408889634
Q_SCALE = (HEAD_DIM ** -0.5) * LOG2E


def _attn_scores_t(q, kh, kv):
    h0 = kv * ATT_GROUP
    qs = jnp.concatenate([q[:, (h0 + g) * HEAD_DIM:(h0 + g + 1) * HEAD_DIM] for g in range(ATT_GROUP)],
                         axis=0).astype(BF16)
    return lax.dot_general(kh, qs, _NT, preferred_element_type=F32)


def _attn_finish_t(sink_ref, s, vh, g_ref, o_ref, kv, row0=0):
    tk, cols = s.shape
    t = cols // ATT_GROUP
    h0 = kv * ATT_GROUP
    head = lax.broadcasted_iota(jnp.int32, (1, cols), 1) // t
    sink = jnp.full((1, cols), sink_ref[h0], F32)
    for g in range(1, ATT_GROUP):
        sink = jnp.where(head == g, sink_ref[h0 + g], sink)
    sink = sink * LOG2E
    m = jnp.maximum(jnp.max(s, axis=0, keepdims=True), sink)
    p = jnp.exp2(s - m).astype(BF16)
    v_ext = jnp.concatenate([vh, jnp.ones((tk, HEAD_DIM), BF16)], axis=1)
    o_ext = lax.dot_general(v_ext, p, _TN, preferred_element_type=F32)
    denom = o_ext[HEAD_DIM:HEAD_DIM + 1] + jnp.exp2(sink - m)
    o = o_ext[:HEAD_DIM] / denom
    for gp in range(ATT_GROUP // 2):
        pair = jnp.concatenate([o[:, (2 * gp) * t:(2 * gp + 1) * t], o[:, (2 * gp + 1) * t:(2 * gp + 2) * t]],
                               axis=0)
        c0 = (h0 + 2 * gp) * HEAD_DIM
        o_ref[0, row0:row0 + t, c0:c0 + 2 * HEAD_DIM] = pair.T * g_ref[0, row0:row0 + t, c0:c0 + 2 * HEAD_DIM]


def _attn_ctx_kernel(sink_ref, q_ref, k_ref, v_ref, g_ref, o_ref):
    q = q_ref[0] * Q_SCALE
    k = k_ref[0].astype(BF16)
    v = v_ref[0].astype(BF16)
    scores = [_attn_scores_t(q, k[:, kv * HEAD_DIM:(kv + 1) * HEAD_DIM], kv) for kv in range(ATT_KV_HEADS)]
    for kv in range(ATT_KV_HEADS):
        _attn_finish_t(sink_ref, scores[kv], v[:, kv * HEAD_DIM:(kv + 1) * HEAD_DIM], g_ref, o_ref, kv)


def _attn_ctx_call(u, sink):
    b, l, _ = u.shape
    return pl.pallas_call(
        _attn_ctx_kernel,
        grid=(b,),
        in_specs=[pl.BlockSpec(memory_space=pltpu.SMEM),
                  pl.BlockSpec((1, l, W_ATT), lambda i: (i, 0, COL_QA // W_ATT)),
                  pl.BlockSpec((1, l, W_KV), lambda i: (i, 0, COL_KA // W_KV)),
                  pl.BlockSpec((1, l, W_KV), lambda i: (i, 0, COL_VA // W_KV)),
                  pl.BlockSpec((1, l, W_ATT), lambda i: (i, 0, COL_GA // W_ATT))],
        out_specs=pl.BlockSpec((1, l, W_ATT), lambda i: (i, 0, 0)),
        out_shape=jax.ShapeDtypeStruct((b, l, W_ATT), F32),
        compiler_params=_cparams(("parallel",)),
        name="attn_ctx",
    )(sink, u, u, u, u)


def _attn_lat_kernel(sink_ref, q_ref, kp_ref, kc_ref, kn_ref, vp_ref, vc_ref, vn_ref, kx_ref, vx_ref,
                     g_ref, o_ref):
    j = pl.program_id(1)
    last = pl.num_programs(1) - 1
    b = ATT_BLOCK
    bf = lambda ref: ref[0].astype(BF16)
    kp, kc, kn, kx = bf(kp_ref), bf(kc_ref), bf(kn_ref), bf(kx_ref)
    vp, vc, vn, vx = bf(vp_ref), bf(vc_ref), bf(vn_ref), bf(vx_ref)
    keys = (jnp.concatenate([kp, kc, kx], axis=0), jnp.concatenate([kc, kn, kx], axis=0))
    vals = (jnp.concatenate([vp, vc, vx], axis=0), jnp.concatenate([vc, vn, vx], axis=0))
    cols = ATT_GROUP * b
    c = lax.broadcasted_iota(jnp.int32, (b, cols), 0)
    r = lax.broadcasted_iota(jnp.int32, (b, cols), 1) % b
    ok_prev = (c >= r, c >= r)
    ok_next = (c <= r, c <= r)
    ok_prev = (ok_prev[0] & (j > 0), ok_prev[1])
    ok_next = (ok_next[0], ok_next[1] & (j < last))

    def band(s, t):
        return jnp.concatenate([jnp.where(ok_prev[t], s[:b], NEG), s[b:2 * b],
                                jnp.where(ok_next[t], s[2 * b:3 * b], NEG), s[3 * b:]], axis=0)

    chains = [(t, kv) for t in range(2) for kv in range(ATT_KV_HEADS)]
    scores = []
    for t, kv in chains:
        q = q_ref[0, t * b:(t + 1) * b, :] * Q_SCALE
        scores.append(band(_attn_scores_t(q, keys[t][:, kv * HEAD_DIM:(kv + 1) * HEAD_DIM], kv), t))
    for (t, kv), s in zip(chains, scores):
        _attn_finish_t(sink_ref, s, vals[t][:, kv * HEAD_DIM:(kv + 1) * HEAD_DIM], g_ref, o_ref, kv, t * b)


def _attn_lat_call(u, kctx, vctx, sink, layer):
    b, l, _ = u.shape
    nb = l // ATT_BLOCK
    past = kctx.shape[2]
    kcol = COL_KA // W_KV
    vcol = COL_VA // W_KV
    prev = lambda col: pl.BlockSpec((1, ATT_BLOCK, W_KV), lambda i, j: (i, jnp.maximum(2 * j - 1, 0), col))
    cur = lambda col: pl.BlockSpec((1, 2 * ATT_BLOCK, W_KV), lambda i, j: (i, j, col))
    nxt = lambda col: pl.BlockSpec((1, ATT_BLOCK, W_KV),
                                   lambda i, j: (i, jnp.minimum(2 * j + 2, nb - 1), col))
    ctx = pl.BlockSpec((1, None, past, W_KV), lambda i, j: (i, layer, 0, 0))
    return pl.pallas_call(
        _attn_lat_kernel,
        grid=(b, nb // 2),
        in_specs=[pl.BlockSpec(memory_space=pltpu.SMEM),
                  pl.BlockSpec((1, 2 * ATT_BLOCK, W_ATT), lambda i, j: (i, j, COL_QA // W_ATT)),
                  prev(kcol), cur(kcol), nxt(kcol), prev(vcol), cur(vcol), nxt(vcol), ctx, ctx,
                  pl.BlockSpec((1, 2 * ATT_BLOCK, W_ATT), lambda i, j: (i, j, COL_GA // W_ATT))],
        out_specs=pl.BlockSpec((1, 2 * ATT_BLOCK, W_ATT), lambda i, j: (i, j, 0)),
        out_shape=jax.ShapeDtypeStruct((b, l, W_ATT), F32),
        compiler_params=_cparams(("parallel", "parallel")),
        name="attn_lat",
    )(sink, u, u, u, u, u, u, u, kctx, vctx, u)


def _log_sigmoid(x):
    return jnp.minimum(x, 0.0) - jnp.log1p(jnp.exp(-jnp.abs(x)))


def _ret_kernel(q_ref, k_ref, v_ref, g_ref, thl_ref, thb_ref, gn_ref, s0_ref, o_ref, sfin_ref, ob_ref, *,
                nc, has_s0):
    hp = pl.program_id(1)
    c = RET_CHUNK
    lane = lax.broadcasted_iota(jnp.int32, (1, LANES), 1)
    lo_head = lane < RET_DIM
    rowf = lax.broadcasted_iota(jnp.int32, (c, LANES), 0).astype(F32)
    ii = lax.broadcasted_iota(jnp.int32, (c, c), 0)
    jj = lax.broadcasted_iota(jnp.int32, (c, c), 1)
    dd = lax.broadcasted_iota(jnp.int32, (LANES, LANES), 0)
    ee = lax.broadcasted_iota(jnp.int32, (LANES, LANES), 1)
    same_head = (dd < RET_DIM) == (ee < RET_DIM)
    gn = gn_ref[...]

    def tables(direction):
        lg_lane = _log_sigmoid(thl_ref[direction])
        dmats = []
        for hh in range(2):
            lg_h = _log_sigmoid(thb_ref[direction, pl.ds(2 * hp + hh, 1), :])
            if direction == 0:
                dist = ii - jj
            else:
                dist = jj - ii
            dmats.append(jnp.where(dist >= 0, jnp.exp(lg_h * jnp.maximum(dist, 0).astype(F32)), 0.0))
        if direction == 0:
            q_dec = jnp.exp(lg_lane * (rowf + 1.0))
            k_dec = jnp.exp(lg_lane * (c - 1.0 - rowf))
        else:
            q_dec = jnp.exp(lg_lane * (c - rowf))
            k_dec = jnp.exp(lg_lane * rowf)
        c_dec = jnp.exp(lg_lane * float(c))
        return jnp.concatenate(dmats, axis=0), q_dec, k_dec, c_dec

    def first_level(tabs, r0):
        dmat2, _, k_dec, _ = tabs
        qc = q_ref[0, pl.ds(r0, c), :]
        kc = k_ref[0, pl.ds(r0, c), :]
        vcb = v_ref[0, pl.ds(r0, c), :].astype(BF16)
        qs = jnp.concatenate([jnp.where(lo_head, qc, 0.0), jnp.where(lo_head, 0.0, qc)], axis=0)
        sc = lax.dot_general(qs.astype(BF16), kc.astype(BF16), _NT, preferred_element_type=F32) * dmat2
        upd = jnp.where(same_head, _dot((kc * k_dec).T.astype(BF16), vcb), 0.0)
        return qc, vcb, sc.astype(BF16), upd

    def second_level(tabs, lvl1, s):
        _, q_dec, _, c_dec = tabs
        qc, vcb, scb, upd = lvl1
        pv = _dot(scb, vcb)
        o = _dot((qc * q_dec).astype(BF16), s.astype(BF16)) + jnp.where(lo_head, pv[:c], pv[c:])
        return o, c_dec * s + upd

    def init_state(direction):
        if has_s0:
            return s0_ref[0, direction, 0]
        return jnp.zeros((LANES, LANES), F32)

    tabs_f = tables(0)
    tabs_b = tables(1)

    def scan_body(n, carry):
        s_f, s_b = carry
        rows_f = [pl.multiple_of((2 * n + j) * c, c) for j in range(2)]
        rows_b = [pl.multiple_of((nc - 1 - 2 * n - j) * c, c) for j in range(2)]
        lvl_f = [first_level(tabs_f, r) for r in rows_f]
        lvl_b = [first_level(tabs_b, r) for r in rows_b]
        for j in range(2):
            o_f, s_f = second_level(tabs_f, lvl_f[j], s_f)
            o_ref[0, pl.ds(rows_f[j], c), :] = o_f
            o_b, s_b = second_level(tabs_b, lvl_b[j], s_b)
            ob_ref[pl.ds(rows_b[j], c), :] = o_b
        return s_f, s_b

    assert nc % 2 == 0
    s_f, s_b = lax.fori_loop(0, nc // 2, scan_body, (init_state(0), init_state(1)))
    sfin_ref[0, 0, 0] = s_f
    sfin_ref[0, 1, 0] = s_b

    def norm_body(n, carry):
        r0 = pl.multiple_of(n * c, c)
        o = o_ref[0, pl.ds(r0, c), :] + ob_ref[pl.ds(r0, c), :]
        o2 = o * o
        s_lo = jnp.sum(jnp.where(lo_head, o2, 0.0), axis=-1, keepdims=True)
        s_hi = jnp.sum(jnp.where(lo_head, 0.0, o2), axis=-1, keepdims=True)
        ms = jnp.where(lo_head, s_lo, s_hi) * (1.0 / RET_DIM)
        o_ref[0, pl.ds(r0, c), :] = o * lax.rsqrt(ms + EPS) * gn * g_ref[0, pl.ds(r0, c), :]
        return carry

    lax.fori_loop(0, nc, norm_body, 0, unroll=2)


def _ret_call(u, theta, gn, s0bd, layer=0):
    b, l, _ = u.shape
    nc = l // RET_CHUNK
    npair = RET_HEADS // 2
    has_s0 = s0bd is not None
    if not has_s0:
        s0bd = jnp.zeros((1, 2, 1, LANES, LANES), F32)
        s0_spec = pl.BlockSpec((1, 2, 1, LANES, LANES), lambda i, j: (0, 0, 0, 0, 0))
    else:
        s0_spec = pl.BlockSpec((1, None, 2, 1, LANES, LANES), lambda i, j: (i, layer, 0, j, 0, 0))
    th_lane = jnp.repeat(theta, RET_DIM, axis=1).reshape(2, 1, W_RET)
    th_bcast = jnp.broadcast_to(theta[:, :, None], (2, RET_HEADS, LANES))
    col = lambda c0: pl.BlockSpec((1, l, LANES), lambda i, j: (i, 0, c0 // LANES + j))
    o, sfin = pl.pallas_call(
        functools.partial(_ret_kernel, nc=nc, has_s0=has_s0),
        grid=(b, npair),
        in_specs=[col(COL_QR), col(COL_KR), col(COL_VR), col(COL_GR),
                  pl.BlockSpec((2, 1, LANES), lambda i, j: (0, 0, j)),
                  pl.BlockSpec((2, RET_HEADS, LANES), lambda i, j: (0, 0, 0)),
                  pl.BlockSpec((1, LANES), lambda i, j: (0, j)),
                  s0_spec],
        out_specs=[pl.BlockSpec((1, l, LANES), lambda i, j: (i, 0, j)),
                   pl.BlockSpec((1, 2, 1, LANES, LANES), lambda i, j: (i, 0, j, 0, 0))],
        out_shape=[jax.ShapeDtypeStruct((b, l, W_RET), F32),
                   jax.ShapeDtypeStruct((b, 2, npair, LANES, LANES), F32)],
        scratch_shapes=[pltpu.VMEM((l, LANES), F32)],
        compiler_params=_cparams(("parallel", "parallel")),
        name="retention_s0" if has_s0 else "retention",
    )(u, u, u, u, th_lane, th_bcast, gn.reshape(1, W_RET), s0bd)
    return o, sfin


def _blockdiag_states(s):
    sp = s.reshape(s.shape[:-3] + (RET_HEADS // 2, 2, RET_DIM, RET_DIM))
    z = jnp.zeros_like(sp[..., 0, :, :])
    top = jnp.concatenate([sp[..., 0, :, :], z], axis=-1)
    bot = jnp.concatenate([z, sp[..., 1, :, :]], axis=-1)
    return jnp.concatenate([top, bot], axis=-2)


def _diag_states(sbd):
    b = sbd.shape[0]
    s0 = sbd[:, :, :, :RET_DIM, :RET_DIM]
    s1 = sbd[:, :, :, RET_DIM:, RET_DIM:]
    return jnp.stack([s0, s1], axis=3).reshape(b, 2, RET_HEADS, RET_DIM, RET_DIM)


def _filter_positions(l):
    t = jnp.linspace(0.0, 1.0, l, dtype=F32)[:, None]
    w = 2.0 * math.pi * jnp.arange(l, dtype=F32)[:, None] / l
    f = jnp.linspace(1e-4, HY_BANDS - 1, HY_BANDS, dtype=F32)[None, :]
    z = jnp.concatenate([t, jnp.cos(f * w), -jnp.sin(f * w)], axis=-1)
    z = jnp.pad(z, ((0, 0), (0, 32 - HY_POS_FEAT)))
    return jnp.concatenate([z, z[:1], jnp.flip(z[1:], axis=0)], axis=0)


def _hyena_deltas():
    max_decay = math.log(HY_DECAY_TARGET) / HY_FAST_PCT
    min_decay = math.log(HY_DECAY_TARGET) / HY_SLOW_PCT
    return jnp.abs(jnp.linspace(min_decay, max_decay, W_HY, dtype=F32))[None, :]


def _filter_hidden(z_ref, w1_ref, b1_ref, fr_ref):
    pre = jnp.dot(z_ref[...], w1_ref[0], precision=HIGHEST, preferred_element_type=F32) + b1_ref[0]
    return jnp.sin(fr_ref[0] * pre)


def _filter_raw(hid, w2f, w2b, tp, dl, row0, l):
    win = jnp.exp(-tp * dl)
    row = row0 + lax.broadcasted_iota(jnp.int32, win.shape, 0)
    hf = jnp.dot(hid, w2f, precision=HIGHEST, preferred_element_type=F32) * win
    hb = jnp.dot(hid, w2b, precision=HIGHEST, preferred_element_type=F32) * win
    hf = jnp.where(row < l, hf, 0.0)
    hb = jnp.where((row > l) | (row == 0), hb, 0.0)
    return hf + hb, jnp.sum(jnp.abs(hf) + jnp.abs(hb), axis=0, keepdims=True)


def _with_skip(g, skip):
    row = lax.broadcasted_iota(jnp.int32, g.shape, 0)
    return g + jnp.where(row == 0, skip, 0.0)


def _filt_ctx_kernel(z_ref, w1_ref, b1_ref, fr_ref, w2_ref, dl_ref, sk_ref, fh_ref, fl_ref, g_ref):
    hid = _filter_hidden(z_ref, w1_ref, b1_ref, fr_ref)
    tp = z_ref[:, 0:1]
    for o in range(2):
        w2f = w2_ref[0, :, (2 * o) * W_HY:(2 * o + 1) * W_HY]
        w2b = w2_ref[0, :, (2 * o + 1) * W_HY:(2 * o + 2) * W_HY]
        raw, nrm = _filter_raw(hid, w2f, w2b, tp, dl_ref[...], 0, z_ref.shape[0] // 2)
        g = _with_skip(raw / nrm, sk_ref[0, pl.ds(o, 1), :])
        g_ref[0, o] = _dot3c(fh_ref[...], fl_ref[...], g)


def _ctx_dft_tables(l):
    n = 2 * l
    k = np.arange(n)[:, None]
    t = np.arange(l)[None, :]
    ang = 2.0 * np.pi * k * t / n
    c, s = np.cos(ang), np.sin(ang)
    fwd = np.block([[c, s], [-s, c]])
    inv = np.block([[c.T, -s.T], [s.T, c.T]])
    n_all = np.arange(n)[None, :]
    angg = 2.0 * np.pi * k * n_all / n
    filt = np.concatenate([np.cos(angg), -np.sin(angg)], axis=0) / n
    return _split_np(fwd), _split_np(inv), _split_np(filt)


def _filt_ctx_call(l, w1, b1, freq, w2, skip):
    n = 2 * l
    z_ext = _filter_positions(l)
    _, _, (fh, fl) = _ctx_dft_tables(l)
    w1p = jnp.pad(w1, ((0, 0), (0, 32 - HY_POS_FEAT), (0, 0)))
    lay = lambda *shape: pl.BlockSpec((1,) + shape, lambda d: (d,) + (0,) * len(shape))
    full = lambda a: pl.BlockSpec(a.shape, lambda d: (0,) * a.ndim)
    dl = _hyena_deltas()
    return pl.pallas_call(
        _filt_ctx_kernel,
        grid=(DEPTH,),
        in_specs=[full(z_ext), lay(32, HY_FILT_HID), lay(1, HY_FILT_HID), lay(1, HY_FILT_HID),
                  lay(HY_FILT_HID, 4 * W_HY), full(dl), lay(2, W_HY), full(fh), full(fl)],
        out_specs=pl.BlockSpec((1, 2, 2 * n, W_HY), lambda d: (d, 0, 0, 0)),
        out_shape=jax.ShapeDtypeStruct((DEPTH, 2, 2 * n, W_HY), F32),
        compiler_params=_cparams(("arbitrary",)),
        name="hyena_filter_ctx",
    )(z_ext, w1p, b1.reshape(DEPTH, 1, -1), freq.reshape(DEPTH, 1, -1), w2, dl, skip, fh, fl)


def _lat_dft_tables():
    ka = np.arange(NA)[:, None]
    b = np.arange(NB)[:, None, None]
    kb = np.arange(NB)[:, None]
    bb = np.arange(NB)[None, :]
    a_half = np.arange(NA // 2)[None, :]
    a_full = np.arange(NA)[None, :]
    phi = 2.0 * np.pi * (ka * a_half / NA + b * ka / LAT_N)
    c, s = np.cos(phi), np.sin(phi)
    a_fwd = np.concatenate([c, s], axis=2)
    ct, st = np.swapaxes(c, 1, 2), np.swapaxes(s, 1, 2)
    a_inv = np.concatenate([ct, st], axis=2)
    phig = 2.0 * np.pi * (ka * a_full / NA + b * ka / LAT_N)
    a_flt = np.concatenate([np.cos(phig), -np.sin(phig)], axis=1) / LAT_N
    ang = 2.0 * np.pi * kb * bb / NB
    c2, s2 = np.cos(ang), np.sin(ang)
    f_fwd = np.block([[c2, s2], [-s2, c2]])
    f_inv = np.block([[c2, -s2], [s2, c2]])
    return (_split_np(a_fwd), _split_np(a_inv), _split_np(a_flt), _split_np(f_fwd), _split_np(f_inv))


def _stage_b_rows(ka):
    re = pl.ds(ka, NB, stride=Y_PITCH)
    im = pl.ds(NA + ka, NB, stride=Y_PITCH)
    return re, im


def _filt_lat_kernel(z_ref, w1_ref, b1_ref, fr_ref, w2f_ref, w2b_ref, dl_ref, sk_ref, ah_ref, al_ref,
                     f2h_ref, f2l_ref, g_ref, hid_ref, gt_ref, y_ref):
    step = pl.program_id(1)
    rch = 1024
    nch = LAT_N // rch
    rows_of = lambda i: pl.ds(pl.multiple_of(i * rch, rch), rch)

    @pl.when(step == 0)
    def _():
        def hid_chunk(i, carry):
            r = rows_of(i)
            pre = jnp.dot(z_ref[r, :], w1_ref[0], precision=HIGHEST, preferred_element_type=F32)
            hid_ref[r, :] = jnp.sin(fr_ref[0] * (pre + b1_ref[0]))
            return carry

        lax.fori_loop(0, nch, hid_chunk, 0)

    w2f_hl = _split(w2f_ref[0])
    w2b_hl = _split(w2b_ref[0])

    def raw_chunk(w2_hl, i, nrm):
        r = rows_of(i)
        hh, hl = _split(hid_ref[r, :])
        h = _dot(hh, w2_hl[0]) + _dot(hl, w2_hl[0]) + _dot(hh, w2_hl[1])
        h = h * jnp.exp(-z_ref[r, 0:1] * dl_ref[...])
        row = i * rch + lax.broadcasted_iota(jnp.int32, h.shape, 0)
        h = jnp.where(row == LAT_L, 0.0, h)
        gt_ref[r, :] = h
        return nrm + jnp.sum(jnp.abs(h), axis=0, keepdims=True)

    nrm = lax.fori_loop(0, nch // 2, functools.partial(raw_chunk, w2f_hl), jnp.zeros((1, LANES), F32))
    nrm = lax.fori_loop(nch // 2, nch, functools.partial(raw_chunk, w2b_hl), nrm)
    hh, hl = _split(hid_ref[0:8, :])
    hb0 = _dot(hh, w2b_hl[0]) + _dot(hl, w2b_hl[0]) + _dot(hh, w2b_hl[1])
    hb0 = hb0 * jnp.exp(-z_ref[0:8, 0:1] * dl_ref[...])
    hb0 = jnp.where(lax.broadcasted_iota(jnp.int32, hb0.shape, 0) == 0, hb0, 0.0)
    gt_ref[0:8, :] = gt_ref[0:8, :] + hb0
    nrm = nrm + jnp.sum(jnp.abs(hb0), axis=0, keepdims=True)

    def norm_chunk(i, carry):
        r = rows_of(i)
        gt_ref[r, :] = gt_ref[r, :] / nrm
        return carry

    lax.fori_loop(0, nch, norm_chunk, 0)
    order = step // (W_HY // LANES)
    gt_ref[0:8, :] = _with_skip(gt_ref[0:8, :], sk_ref[0, pl.ds(order, 1), :])

    def stage_a(b, carry):
        rows = gt_ref[pl.ds(b, NA, stride=NB), :]
        y_ref[pl.ds(pl.multiple_of(b * Y_PITCH, 8), 2 * NA), :] = _dot3c(ah_ref[b], al_ref[b], rows)
        return carry

    lax.fori_loop(0, NB, stage_a, 0, unroll=4)

    def stage_b(ka, carry):
        re, im = _stage_b_rows(ka)
        z = jnp.concatenate([y_ref[re, :], y_ref[im, :]], axis=0)
        g_ref[0, 0, pl.ds(pl.multiple_of(ka * 2 * NB, 2 * NB), 2 * NB), :] = _dot3c(
            f2h_ref[...], f2l_ref[...], z)
        return carry

    lax.fori_loop(0, NA, stage_b, 0, unroll=4)


def _filt_lat_call(w1, b1, freq, w2, skip):
    z_ext = _filter_positions(LAT_L)
    _, _, (ah, al), (f2h, f2l), _ = _lat_dft_tables()
    w1p = jnp.pad(w1, ((0, 0), (0, 32 - HY_POS_FEAT), (0, 0)))
    nct = W_HY // LANES
    one = pl.Buffered(1)
    lay = lambda *shape: pl.BlockSpec((1,) + shape, lambda d, s: (d,) + (0,) * len(shape))
    full = lambda a: pl.BlockSpec(a.shape, lambda d, s: (0,) * a.ndim, pipeline_mode=one)
    dl = _hyena_deltas()
    return pl.pallas_call(
        _filt_lat_kernel,
        grid=(DEPTH, 2 * nct),
        in_specs=[full(z_ext), lay(32, HY_FILT_HID), lay(1, HY_FILT_HID), lay(1, HY_FILT_HID),
                  pl.BlockSpec((1, HY_FILT_HID, LANES), lambda d, s: (d, 0, (s // nct) * 2 * nct + s % nct)),
                  pl.BlockSpec((1, HY_FILT_HID, LANES),
                               lambda d, s: (d, 0, (s // nct) * 2 * nct + nct + s % nct)),
                  pl.BlockSpec((1, LANES), lambda d, s: (0, s % nct)),
                  pl.BlockSpec((1, 2, LANES), lambda d, s: (d, 0, s % nct)),
                  full(ah), full(al), full(f2h), full(f2l)],
        out_specs=pl.BlockSpec((1, 1, NA * 2 * NB, LANES), lambda d, s: (d, s // nct, 0, s % nct)),
        out_shape=jax.ShapeDtypeStruct((DEPTH, 2, NA * 2 * NB, W_HY), F32),
        scratch_shapes=[pltpu.VMEM((LAT_N, HY_FILT_HID), F32), pltpu.VMEM((LAT_N, LANES), F32),
                        pltpu.VMEM((NB * Y_PITCH, LANES), F32)],
        compiler_params=_cparams(("arbitrary", "arbitrary")),
        name="hyena_filter_lat",
    )(z_ext, w1p, b1.reshape(DEPTH, 1, -1), freq.reshape(DEPTH, 1, -1), w2, w2, dl, skip, ah, al, f2h,
      f2l)


def _short_conv_rows(ref, bi, r0, rows, first, last, w):
    total = ref.shape[1]
    cur = ref[bi, pl.ds(r0, rows), :]
    before = ref[bi, pl.ds(jnp.maximum(r0 - 1, 0), 1), :]
    after = ref[bi, pl.ds(jnp.minimum(r0 + rows, total - 1), 1), :]
    before = jnp.where(first, 0.0, before)
    after = jnp.where(last, 0.0, after)
    rid = lax.broadcasted_iota(jnp.int32, cur.shape, 0)
    prev = jnp.where(rid == 0, before, pltpu.roll(cur, 1, 0))
    nxt = jnp.where(rid == rows - 1, after, pltpu.roll(cur, rows - 1, 0))
    return prev * w[0:1] + cur * w[1:2] + nxt * w[2:3]


def _cmul(xr, xi, gr, gi):
    return xr * gr - xi * gi, xr * gi + xi * gr


def _hy_ctx_kernel(v_ref, x1_ref, x2_ref, cw_ref, g_ref, fh_ref, ih_ref, o_ref):
    l = v_ref.shape[1]
    n = 2 * l

    def sc(ref, bi, grp):
        w = cw_ref[:, grp * W_HY:(grp + 1) * W_HY]
        return _short_conv_rows(ref, bi, 0, l, True, True, w)

    def conv(zr, zi, order):
        x = _dot1c(fh_ref[...], jnp.concatenate([zr, zi], axis=0))
        pr, pi = _cmul(x[:n], x[n:], g_ref[order, :n], g_ref[order, n:])
        y = _dot1c(ih_ref[...], jnp.concatenate([pr, pi], axis=0))
        return y[:l], y[l:]

    yr, yi = conv(sc(v_ref, 0, 0), sc(v_ref, 1, 0), 0)
    yr, yi = conv(sc(x1_ref, 0, 1) * yr, sc(x1_ref, 1, 1) * yi, 1)
    o_ref[0] = sc(x2_ref, 0, 2) * yr
    o_ref[1] = sc(x2_ref, 1, 2) * yi


def _hy_ctx_call(u, conv_w, g_spec, layer):
    b, l, _ = u.shape
    (fh, _), (ih, _), _ = _ctx_dft_tables(l)
    grp = lambda g: pl.BlockSpec((2, l, W_HY), lambda i: (i, 0, g))
    full = lambda a: pl.BlockSpec(a.shape, lambda i: (0,) * a.ndim)
    return pl.pallas_call(
        _hy_ctx_kernel,
        grid=(b // 2,),
        in_specs=[grp(0), grp(1), grp(2), full(conv_w),
                  pl.BlockSpec((None,) + g_spec.shape[1:], lambda i: (layer, 0, 0, 0)), full(fh), full(ih)],
        out_specs=pl.BlockSpec((2, l, W_HY), lambda i: (i, 0, 0)),
        out_shape=jax.ShapeDtypeStruct((b, l, W_HY), F32),
        compiler_params=_cparams(("parallel",)),
        name="hyena_ctx",
    )(u, u, u, conv_w, g_spec, fh, ih)


def _hy_lat_kernel(z_ref, m_ref, cw_ref, g_ref, af_ref, ai_ref, f2_ref, f3_ref, o_ref, xr_scr, xi_scr,
                   y_scr, *, conv_in):
    x_scr = (xr_scr, xi_scr)
    na_half = NA // 2
    w_in = cw_ref[0] if conv_in else None
    w_mul = cw_ref[1]

    def load_in(a, carry):
        for bi in range(2):
            r0 = pl.multiple_of(a * NB, NB)
            if conv_in:
                val = _short_conv_rows(z_ref, bi, r0, NB, a == 0, a == na_half - 1, w_in)
            else:
                val = z_ref[bi, pl.ds(r0, NB), :]
            x_scr[bi][pl.ds(pl.multiple_of(a * X_PITCH, 8), NB), :] = val
        return carry

    lax.fori_loop(0, na_half, load_in, 0)

    def stage_a(b, carry):
        zr = xr_scr[pl.ds(b, na_half, stride=X_PITCH), :]
        zi = xi_scr[pl.ds(b, na_half, stride=X_PITCH), :]
        rhs = jnp.concatenate([jnp.concatenate([zr, zi], axis=0), jnp.concatenate([zi, -zr], axis=0)],
                              axis=1)
        y = _dot1c(af_ref[b], rhs)
        r0 = pl.multiple_of(b * Y_PITCH, 8)
        y_scr[pl.ds(r0, NA), :] = y[:, :LANES]
        y_scr[pl.ds(r0 + NA, NA), :] = y[:, LANES:]
        return carry

    lax.fori_loop(0, NB, stage_a, 0, unroll=16)

    def stage_b(spectral, j, carry):
        ka = 2 * j
        re, im = _stage_b_rows(ka)
        re1, im1 = _stage_b_rows(ka + 1)
        z = jnp.concatenate([jnp.concatenate([y_scr[re, :], y_scr[im, :]], axis=0),
                             jnp.concatenate([y_scr[re1, :], y_scr[im1, :]], axis=0)], axis=1)
        if spectral:
            x = _dot1c(f2_ref[...], z)
            g0 = pl.multiple_of(ka * 2 * NB, 2 * NB)
            gr = jnp.concatenate([g_ref[pl.ds(g0, NB), :], g_ref[pl.ds(g0 + 2 * NB, NB), :]], axis=1)
            gi = jnp.concatenate([g_ref[pl.ds(g0 + NB, NB), :], g_ref[pl.ds(g0 + 3 * NB, NB), :]], axis=1)
            ur, ui = _cmul(x[:NB], x[NB:], gr, gi)
        else:
            u = _dot1c(f3_ref[...], z)
            ur, ui = u[:NB], u[NB:]
        y_scr[re, :] = ur[:, :LANES]
        y_scr[im, :] = ui[:, :LANES]
        y_scr[re1, :] = ur[:, LANES:]
        y_scr[im1, :] = ui[:, LANES:]
        return carry

    for spectral in (True, False):
        lax.fori_loop(0, NA // 2, functools.partial(stage_b, spectral), 0, unroll=4)

    def stage_c(b, carry):
        r0 = pl.multiple_of(b * Y_PITCH, 8)
        ur = y_scr[pl.ds(r0, NA), :]
        ui = y_scr[pl.ds(r0 + NA, NA), :]
        rhs = jnp.concatenate([jnp.concatenate([ur, -ui], axis=0), jnp.concatenate([ui, ur], axis=0)],
                              axis=1)
        y = _dot1c(ai_ref[b], rhs)
        xr_scr[pl.ds(b, na_half, stride=X_PITCH), :] = y[:, :LANES]
        xi_scr[pl.ds(b, na_half, stride=X_PITCH), :] = y[:, LANES:]
        return carry

    lax.fori_loop(0, NB, stage_c, 0, unroll=16)

    def store_out(a, carry):
        for bi in range(2):
            r0 = pl.multiple_of(a * NB, NB)
            mul = _short_conv_rows(m_ref, bi, r0, NB, a == 0, a == na_half - 1, w_mul)
            o_ref[bi, pl.ds(r0, NB), :] = x_scr[bi][pl.ds(pl.multiple_of(a * X_PITCH, 8), NB), :] * mul
        return carry

    lax.fori_loop(0, na_half, store_out, 0)


def _hy_lat_call(src, src_col, u, mul_col, conv_w2, g_spec, layer, order, *, conv_in):
    b, l, _ = u.shape
    nct = W_HY // LANES
    (af, _), (ai, _), _, (f2, _), (f3, _) = _lat_dft_tables()
    one = pl.Buffered(1)
    blk = lambda col: pl.BlockSpec((2, l, LANES), lambda c, p: (p, 0, col + c))
    const = lambda a: pl.BlockSpec(a.shape, lambda c, p: (0,) * a.ndim, pipeline_mode=one)
    return pl.pallas_call(
        functools.partial(_hy_lat_kernel, conv_in=conv_in),
        grid=(nct, b // 2),
        in_specs=[blk(src_col), blk(mul_col),
                  pl.BlockSpec((2, 3, LANES), lambda c, p: (0, 0, c)),
                  pl.BlockSpec((None, None, NA * 2 * NB, LANES), lambda c, p: (layer, order, 0, c),
                               pipeline_mode=one),
                  const(af), const(ai), const(f2), const(f3)],
        out_specs=pl.BlockSpec((2, l, LANES), lambda c, p: (p, 0, c)),
        out_shape=jax.ShapeDtypeStruct((b, l, W_HY), F32),
        scratch_shapes=[pltpu.VMEM(((NA // 2) * X_PITCH, LANES), F32),
                        pltpu.VMEM(((NA // 2) * X_PITCH, LANES), F32),
                        pltpu.VMEM((NB * Y_PITCH, LANES), F32)],
        compiler_params=_cparams(("arbitrary", "arbitrary")),
        name="hyena_lat_conv_in" if conv_in else "hyena_lat",
    )(src, u, conv_w2, g_spec, af, ai, f2, f3)


def _rope_tables(l):
    rows = l // GRID_W
    row = jnp.repeat(jnp.arange(rows), GRID_W).astype(F32)
    col = jnp.tile(jnp.arange(GRID_W), rows).astype(F32)
    quarter = HEAD_DIM // 4
    inv = ROPE_BASE ** (-jnp.arange(quarter, dtype=F32) / quarter)
    ang = jnp.concatenate([row[:, None] * inv, col[:, None] * inv], axis=-1)
    cos, sin = jnp.cos(ang), jnp.sin(ang)
    q = quarter
    cos_h = jnp.concatenate([cos[:, :q], cos[:, :q], cos[:, q:], cos[:, q:]], axis=-1)
    sin_h = jnp.concatenate([-sin[:, :q], sin[:, :q], -sin[:, q:], sin[:, q:]], axis=-1)
    return jnp.tile(cos_h, (1, 2)), jnp.tile(sin_h, (1, 2))


def kernel(x_prompt, x_sample, c, cache_k, cache_v, state_ret, c_ctx, norm_w, w_mod, b_mod, w_in, hy_conv,
           hy_filt_w1, hy_filt_b1, hy_filt_freq, hy_filt_w2, hy_skip, attn_sink, ret_theta, ret_gn,
           w_branch_a, w_branch_b, w_branch_c, w_merge, b_merge, w_out, final_norm_w):
    d = D_MODEL
    bc, lc, _ = x_prompt.shape
    bl, ll, _ = x_sample.shape
    assert ll == LAT_L and bc % 2 == 0 and bl % 2 == 0
    past = cache_k.shape[2]

    cond = jnp.zeros((16, d), F32).at[:bl].set(c).at[bl].set(c_ctx)
    mod = _mod_call(cond, w_mod, b_mod)

    g_ctx = _filt_ctx_call(lc, hy_filt_w1, hy_filt_b1, hy_filt_freq, hy_filt_w2, hy_skip)
    g_lat = _filt_lat_call(hy_filt_w1, hy_filt_b1, hy_filt_freq, hy_filt_w2, hy_skip)

    cos_t, sin_t = _rope_tables(ll)
    w_in_b = _pack_w_in(w_in)
    wm_b = w_merge.astype(BF16)
    wa_b = w_branch_a.astype(BF16)
    wb_b = w_branch_b.astype(BF16)
    wc_b = w_branch_c.astype(BF16)
    wo_b = w_out.astype(BF16)
    fnw = final_norm_w.reshape(1, d)
    k_ctx = cache_k.reshape(bl, DEPTH, past, W_KV)
    v_ctx = cache_v.reshape(bl, DEPTH, past, W_KV)
    s0_all = _blockdiag_states(state_ret)
    hy_cols = COL_HY // LANES
    nct = W_HY // LANES

    xp, xs = x_prompt, x_sample
    ks_out, vs_out, ss_out = [], [], []
    for l in range(DEPTH):
        final = l == DEPTH - 1
        nw = norm_w[l].reshape(1, d)
        bm = b_merge[l].reshape(1, -1)
        shift, scale, gate = (mod[l, :, i * d:(i + 1) * d][:, None, :] for i in range(3))
        conv_w = hy_conv[l]
        cw = lambda g: conv_w[:, g * W_HY:(g + 1) * W_HY]

        sl = slice(bl, bl + 1)
        u = _in_call(xp, shift[sl], scale[sl], nw, w_in_b[l], cos_t, sin_t, rope=False, tm=lc)
        ya = _hy_ctx_call(u, conv_w, g_ctx, l)
        yb = _attn_ctx_call(u, attn_sink[l])
        yc, sfin = _ret_call(u, ret_theta[l], ret_gn[l], None)
        res = _out_call(xp, shift[sl], scale[sl], gate[sl], nw, ya, u, yb, yc, wm_b[l], bm, wa_b[l], wb_b[l],
                        wc_b[l], wo_b[l], fnw, final=final, tm=lc)
        xp = res[0]
        if final:
            y_prompt = res[1]
        ks_out.append(u[:, :, COL_KA:COL_KA + W_KV].reshape(bc, lc, ATT_KV_HEADS, HEAD_DIM))
        vs_out.append(u[:, :, COL_VA:COL_VA + W_KV].reshape(bc, lc, ATT_KV_HEADS, HEAD_DIM))
        ss_out.append(_diag_states(sfin))

        sl = slice(0, bl)
        u = _in_call(xs, shift[sl], scale[sl], nw, w_in_b[l], cos_t, sin_t, rope=True, tm=256)
        z1 = _hy_lat_call(u, hy_cols, u, hy_cols + nct, jnp.stack([cw(0), cw(1)]), g_lat, l, 0, conv_in=True)
        ya = _hy_lat_call(z1, 0, u, hy_cols + 2 * nct, jnp.stack([cw(2), cw(2)]), g_lat, l, 1, conv_in=False)
        yb = _attn_lat_call(u, k_ctx, v_ctx, attn_sink[l], l)
        yc, _ = _ret_call(u, ret_theta[l], ret_gn[l], s0_all, l)
        res = _out_call(xs, shift[sl], scale[sl], gate[sl], nw, ya, u, yb, yc, wm_b[l], bm, wa_b[l], wb_b[l],
                        wc_b[l], wo_b[l], fnw, final=final, tm=256)
        xs = res[0]
        if final:
            y_sample = res[1]

    new_cache_k = jnp.stack(ks_out, axis=1)
    new_cache_v = jnp.stack(vs_out, axis=1)
    new_state_ret = jnp.stack(ss_out, axis=1)
    return (y_prompt, y_sample, new_cache_k, new_cache_v, new_state_ret)
```

```python
import functools
import math

import numpy as np
import jax
import jax.numpy as jnp
from jax import lax
from jax.experimental import pallas as pl
from jax.experimental.pallas import tpu as pltpu

F32 = jnp.float32
BF16 = jnp.bfloat16
HIGHEST = lax.Precision.HIGHEST

D_MODEL = 1024
DEPTH = 4
GRID_W = 64
W_HY = 512
HY_BANDS = 8
HY_POS_FEAT = 1 + 2 * HY_BANDS
HY_FILT_HID = 64
HY_DECAY_TARGET = 1e-2
HY_FAST_PCT = 0.3
HY_SLOW_PCT = 1.5
ATT_HEADS = 8
ATT_KV_HEADS = 2
ATT_GROUP = ATT_HEADS // ATT_KV_HEADS
HEAD_DIM = 64
W_ATT = ATT_HEADS * HEAD_DIM
W_KV = ATT_KV_HEADS * HEAD_DIM
ATT_BLOCK = 128
RET_HEADS = 8
RET_DIM = 64
W_RET = RET_HEADS * RET_DIM
RET_CHUNK = 128
ROPE_BASE = 10000.0
EPS = 1e-6
NEG = -1e30

LANES = 128
VMEM_LIMIT = 56 * 1024 * 1024

IN_DIM = 5376
COL_HY = 0
COL_GH = 1536
COL_QA = 2048
COL_GA = 2560
COL_QR = 3072
COL_KR = 3584
COL_VR = 4096
COL_GR = 4608
COL_KA = 5120
COL_VA = 5248
_IN_SEGMENTS = ((0, 2560), (2816, 5376), (2560, 2816))


def _pack_w_in(w):
    return jnp.concatenate([w[..., a:b] for a, b in _IN_SEGMENTS], axis=-1).astype(BF16)

LAT_L = 4096
LAT_N = 2 * LAT_L
NA = 64
NB = 128
Y_PITCH = 136
X_PITCH = 136


def _cparams(sem):
    return pltpu.CompilerParams(dimension_semantics=sem, vmem_limit_bytes=VMEM_LIMIT)


def _split_np(a):
    a32 = np.asarray(a, np.float32)
    hi = a32.astype(BF16)
    lo = (a32 - hi.astype(np.float32)).astype(BF16)
    return jnp.asarray(hi), jnp.asarray(lo)


def _split(x):
    hi = x.astype(BF16)
    lo = (x - hi.astype(F32)).astype(BF16)
    return hi, lo


def _dot(a, b):
    return jnp.dot(a, b, preferred_element_type=F32)


def _dot3c(chi, clo, x):
    xh, xl = _split(x)
    return _dot(chi, xh) + _dot(clo, xh) + _dot(chi, xl)


def _dot1c(chi, x):
    return _dot(chi, x.astype(BF16))


def _silu(x):
    return x * jax.nn.sigmoid(x)


def _mod_kernel(c_ref, w_ref, b_ref, o_ref):
    s = _silu(c_ref[...])
    o_ref[0] = jnp.dot(s, w_ref[0], precision=HIGHEST, preferred_element_type=F32) + b_ref[0]


def _mod_call(cond, w_mod, b_mod):
    rows, d = cond.shape
    n = w_mod.shape[-1]
    tn = 1024
    return pl.pallas_call(
        _mod_kernel,
        grid=(DEPTH, n // tn),
        in_specs=[pl.BlockSpec((rows, d), lambda l, j: (0, 0)),
                  pl.BlockSpec((1, d, tn), lambda l, j: (l, 0, j)),
                  pl.BlockSpec((1, 1, tn), lambda l, j: (l, 0, j))],
        out_specs=pl.BlockSpec((1, rows, tn), lambda l, j: (l, 0, j)),
        out_shape=jax.ShapeDtypeStruct((DEPTH, rows, n), F32),
        compiler_params=_cparams(("arbitrary", "arbitrary")),
        name="adaln_mod",
    )(cond, w_mod, b_mod.reshape(DEPTH, 1, n))


def _modulated(x, nw, scale, shift):
    ms = jnp.mean(x * x, axis=-1, keepdims=True)
    h = x * lax.rsqrt(ms + EPS) * nw
    return h * (1.0 + scale) + shift


def _rope128(x, cos, sin_signed, first_half):
    up = pltpu.roll(x, LANES - 16, 1)
    dn = pltpu.roll(x, 16, 1)
    return x * cos + jnp.where(first_half, up, dn) * sin_signed


def _in_kernel(x_ref, shift_ref, scale_ref, nw_ref, w_ref, cos_ref, sin_ref, o_ref, *, rope):
    x = x_ref[0]
    tm = x.shape[0]
    hb = _modulated(x, nw_ref[...], scale_ref[0], shift_ref[0]).astype(BF16)
    if rope:
        cos = cos_ref[...]
        sin = sin_ref[...]
        lane = lax.broadcasted_iota(jnp.int32, (tm, LANES), 1)
        first_half = (lane % 32) < 16

    def seg(c0, width):
        return _dot(hb, w_ref[:, c0:c0 + width])

    def put_rope(c0, val, mul):
        for i in range(val.shape[1] // LANES):
            piece = val[:, i * LANES:(i + 1) * LANES]
            if rope:
                piece = _rope128(piece, cos, sin, first_half)
            if mul is not None:
                piece = piece * mul
            o_ref[0, :, c0 + i * LANES:c0 + (i + 1) * LANES] = piece

    for g in range(3):
        o_ref[0, :, COL_HY + g * 512:COL_HY + (g + 1) * 512] = seg(COL_HY + g * 512, 512)
    o_ref[0, :, COL_GH:COL_GH + 512] = _silu(seg(COL_GH, 512))
    put_rope(COL_QA, seg(COL_QA, 512), None)
    o_ref[0, :, COL_GA:COL_GA + 512] = _silu(seg(COL_GA, 512))
    put_rope(COL_QR, seg(COL_QR, 512), None)
    put_rope(COL_KR, seg(COL_KR, 512), RET_DIM ** -0.5)
    o_ref[0, :, COL_VR:COL_VR + 512] = seg(COL_VR, 512)
    o_ref[0, :, COL_GR:COL_GR + 512] = _silu(seg(COL_GR, 512))
    put_rope(COL_KA, seg(COL_KA, 128), None)
    o_ref[0, :, COL_VA:COL_VA + 128] = seg(COL_VA, 128)


def _in_call(x, shift, scale, nw, w, cos_t, sin_t, *, rope, tm):
    b, l, d = x.shape
    per_batch = shift.shape[0] > 1
    mod_map = (lambda i, j: (i, 0, 0)) if per_batch else (lambda i, j: (0, 0, 0))
    return pl.pallas_call(
        functools.partial(_in_kernel, rope=rope),
        grid=(b, l // tm),
        in_specs=[pl.BlockSpec((1, tm, d), lambda i, j: (i, j, 0)),
                  pl.BlockSpec((1, 1, d), mod_map),
                  pl.BlockSpec((1, 1, d), mod_map),
                  pl.BlockSpec((1, d), lambda i, j: (0, 0)),
                  pl.BlockSpec((d, IN_DIM), lambda i, j: (0, 0)),
                  pl.BlockSpec((tm, LANES), lambda i, j: (j, 0)),
                  pl.BlockSpec((tm, LANES), lambda i, j: (j, 0))],
        out_specs=pl.BlockSpec((1, tm, IN_DIM), lambda i, j: (i, j, 0)),
        out_shape=jax.ShapeDtypeStruct((b, l, IN_DIM), F32),
        compiler_params=_cparams(("parallel", "parallel")),
        name="in_proj_rope" if rope else "in_proj",
    )(x, shift, scale, nw, w, cos_t, sin_t)


def _out_kernel(x_ref, shift_ref, scale_ref, gate_ref, nw_ref, ya_ref, gh_ref, yb_ref, yc_ref, wm_ref, bm_ref,
                wa_ref, wb_ref, wc_ref, wo_ref, fnw_ref, *out_refs, final):
    x = x_ref[0]
    d = x.shape[1]
    hb = _modulated(x, nw_ref[...], scale_ref[0], shift_ref[0]).astype(BF16)
    branches = (ya_ref[0] * gh_ref[0], yb_ref[0], yc_ref[0])
    merged = None
    for i, (y, w_ref) in enumerate(zip(branches, (wa_ref, wb_ref, wc_ref))):
        g = jax.nn.sigmoid(_dot(hb, wm_ref[:, i * d:(i + 1) * d]) + bm_ref[:, i * d:(i + 1) * d])
        term = g * _dot(y.astype(BF16), w_ref[...])
        merged = term if merged is None else merged + term
    out = _dot(merged.astype(BF16), wo_ref[...])
    xn = x + gate_ref[0] * out
    out_refs[0][0] = xn
    if final:
        ms = jnp.mean(xn * xn, axis=-1, keepdims=True)
        out_refs[1][0] = xn * lax.rsqrt(ms + EPS) * fnw_ref[...]


def _out_call(x, shift, scale, gate, nw, ya, u, yb, yc, wm, bm, wa, wb, wc, wo, fnw, *, final, tm):
    b, l, d = x.shape
    per_batch = shift.shape[0] > 1
    mod_map = (lambda i, j: (i, 0, 0)) if per_batch else (lambda i, j: (0, 0, 0))
    tok = lambda w: pl.BlockSpec((1, tm, w), lambda i, j: (i, j, 0))
    full = lambda a: pl.BlockSpec(a.shape, lambda i, j: (0,) * a.ndim)
    n_out = 2 if final else 1
    res = pl.pallas_call(
        functools.partial(_out_kernel, final=final),
        grid=(b, l // tm),
        in_specs=[tok(d), pl.BlockSpec((1, 1, d), mod_map), pl.BlockSpec((1, 1, d), mod_map),
                  pl.BlockSpec((1, 1, d), mod_map), full(nw), tok(W_HY),
                  pl.BlockSpec((1, tm, W_HY), lambda i, j: (i, j, COL_GH // W_HY)), tok(W_ATT), tok(W_RET),
                  full(wm), full(bm), full(wa), full(wb), full(wc), full(wo), full(fnw)],
        out_specs=[tok(d)] * n_out,
        out_shape=[jax.ShapeDtypeStruct((b, l, d), F32)] * n_out,
        compiler_params=_cparams(("parallel", "parallel")),
        name="merge_out_final" if final else "merge_out",
    )(x, shift, scale, gate, nw, ya, u, yb, yc, wm, bm, wa, wb, wc, wo, fnw)
    return res


_NT = (((1,), (1,)), ((), ()))


_TN = (((0,), (0,)), ((), ()))
LOG2E = 1.4426950408889634
Q_SCALE = (HEAD_DIM ** -0.5) * LOG2E


def _attn_scores_t(q, kh, kv):
    h0 = kv * ATT_GROUP
    qs = jnp.concatenate([q[:, (h0 + g) * HEAD_DIM:(h0 + g + 1) * HEAD_DIM] for g in range(ATT_GROUP)],
                         axis=0).astype(BF16)
    return lax.dot_general(kh, qs, _NT, preferred_element_type=F32)


def _attn_finish_t(sink_ref, s, vh, g_ref, o_ref, kv, row0=0):
    tk, cols = s.shape
    t = cols // ATT_GROUP
    h0 = kv * ATT_GROUP
    head = lax.broadcasted_iota(jnp.int32, (1, cols), 1) // t
    sink = jnp.full((1, cols), sink_ref[h0], F32)
    for g in range(1, ATT_GROUP):
        sink = jnp.where(head == g, sink_ref[h0 + g], sink)
    sink = sink * LOG2E
    m = jnp.maximum(jnp.max(s, axis=0, keepdims=True), sink)
    p = jnp.exp2(s - m).astype(BF16)
    v_ext = jnp.concatenate([vh, jnp.ones((tk, HEAD_DIM), BF16)], axis=1)
    o_ext = lax.dot_general(v_ext, p, _TN, preferred_element_type=F32)
    denom = o_ext[HEAD_DIM:HEAD_DIM + 1] + jnp.exp2(sink - m)
    o = o_ext[:HEAD_DIM] / denom
    for gp in range(ATT_GROUP // 2):
        pair = jnp.concatenate([o[:, (2 * gp) * t:(2 * gp + 1) * t], o[:, (2 * gp + 1) * t:(2 * gp + 2) * t]],
                               axis=0)
        c0 = (h0 + 2 * gp) * HEAD_DIM
        o_ref[0, row0:row0 + t, c0:c0 + 2 * HEAD_DIM] = pair.T * g_ref[0, row0:row0 + t, c0:c0 + 2 * HEAD_DIM]


def _attn_ctx_kernel(sink_ref, q_ref, k_ref, v_ref, g_ref, o_ref):
    q = q_ref[0] * Q_SCALE
    k = k_ref[0].astype(BF16)
    v = v_ref[0].astype(BF16)
    scores = [_attn_scores_t(q, k[:, kv * HEAD_DIM:(kv + 1) * HEAD_DIM], kv) for kv in range(ATT_KV_HEADS)]
    for kv in range(ATT_KV_HEADS):
        _attn_finish_t(sink_ref, scores[kv], v[:, kv * HEAD_DIM:(kv + 1) * HEAD_DIM], g_ref, o_ref, kv)


def _attn_ctx_call(u, sink):
    b, l, _ = u.shape
    return pl.pallas_call(
        _attn_ctx_kernel,
        grid=(b,),
        in_specs=[pl.BlockSpec(memory_space=pltpu.SMEM),
                  pl.BlockSpec((1, l, W_ATT), lambda i: (i, 0, COL_QA // W_ATT)),
                  pl.BlockSpec((1, l, W_KV), lambda i: (i, 0, COL_KA // W_KV)),
                  pl.BlockSpec((1, l, W_KV), lambda i: (i, 0, COL_VA // W_KV)),
                  pl.BlockSpec((1, l, W_ATT), lambda i: (i, 0, COL_GA // W_ATT))],
        out_specs=pl.BlockSpec((1, l, W_ATT), lambda i: (i, 0, 0)),
        out_shape=jax.ShapeDtypeStruct((b, l, W_ATT), F32),
        compiler_params=_cparams(("parallel",)),
        name="attn_ctx",
    )(sink, u, u, u, u)


def _attn_lat_kernel(sink_ref, q_ref, kp_ref, kc_ref, kn_ref, vp_ref, vc_ref, vn_ref, kx_ref, vx_ref,
                     g_ref, o_ref):
    j = pl.program_id(1)
    last = pl.num_programs(1) - 1
    b = ATT_BLOCK
    bf = lambda ref: ref[0].astype(BF16)
    kp, kc, kn, kx = bf(kp_ref), bf(kc_ref), bf(kn_ref), bf(kx_ref)
    vp, vc, vn, vx = bf(vp_ref), bf(vc_ref), bf(vn_ref), bf(vx_ref)
    keys = (jnp.concatenate([kp, kc, kx], axis=0), jnp.concatenate([kc, kn, kx], axis=0))
    vals = (jnp.concatenate([vp, vc, vx], axis=0), jnp.concatenate([vc, vn, vx], axis=0))
    cols = ATT_GROUP * b
    c = lax.broadcasted_iota(jnp.int32, (b, cols), 0)
    r = lax.broadcasted_iota(jnp.int32, (b, cols), 1) % b
    ok_prev = (c >= r, c >= r)
    ok_next = (c <= r, c <= r)
    ok_prev = (ok_prev[0] & (j > 0), ok_prev[1])
    ok_next = (ok_next[0], ok_next[1] & (j < last))

    def band(s, t):
        return jnp.concatenate([jnp.where(ok_prev[t], s[:b], NEG), s[b:2 * b],
                                jnp.where(ok_next[t], s[2 * b:3 * b], NEG), s[3 * b:]], axis=0)

    chains = [(t, kv) for t in range(2) for kv in range(ATT_KV_HEADS)]
    scores = []
    for t, kv in chains:
        q = q_ref[0, t * b:(t + 1) * b, :] * Q_SCALE
        scores.append(band(_attn_scores_t(q, keys[t][:, kv * HEAD_DIM:(kv + 1) * HEAD_DIM], kv), t))
    for (t, kv), s in zip(chains, scores):
        _attn_finish_t(sink_ref, s, vals[t][:, kv * HEAD_DIM:(kv + 1) * HEAD_DIM], g_ref, o_ref, kv, t * b)


def _attn_lat_call(u, kctx, vctx, sink, layer):
    b, l, _ = u.shape
    nb = l // ATT_BLOCK
    past = kctx.shape[2]
    kcol = COL_KA // W_KV
    vcol = COL_VA // W_KV
    prev = lambda col: pl.BlockSpec((1, ATT_BLOCK, W_KV), lambda i, j: (i, jnp.maximum(2 * j - 1, 0), col))
    cur = lambda col: pl.BlockSpec((1, 2 * ATT_BLOCK, W_KV), lambda i, j: (i, j, col))
    nxt = lambda col: pl.BlockSpec((1, ATT_BLOCK, W_KV),
                                   lambda i, j: (i, jnp.minimum(2 * j + 2, nb - 1), col))
    ctx = pl.BlockSpec((1, None, past, W_KV), lambda i, j: (i, layer, 0, 0))
    return pl.pallas_call(
        _attn_lat_kernel,
        grid=(b, nb // 2),
        in_specs=[pl.BlockSpec(memory_space=pltpu.SMEM),
                  pl.BlockSpec((1, 2 * ATT_BLOCK, W_ATT), lambda i, j: (i, j, COL_QA // W_ATT)),
                  prev(kcol), cur(kcol), nxt(kcol), prev(vcol), cur(vcol), nxt(vcol), ctx, ctx,
                  pl.BlockSpec((1, 2 * ATT_BLOCK, W_ATT), lambda i, j: (i, j, COL_GA // W_ATT))],
        out_specs=pl.BlockSpec((1, 2 * ATT_BLOCK, W_ATT), lambda i, j: (i, j, 0)),
        out_shape=jax.ShapeDtypeStruct((b, l, W_ATT), F32),
        compiler_params=_cparams(("parallel", "parallel")),
        name="attn_lat",
    )(sink, u, u, u, u, u, u, u, kctx, vctx, u)


def _log_sigmoid(x):
    return jnp.minimum(x, 0.0) - jnp.log1p(jnp.exp(-jnp.abs(x)))


_TAB_DMAT = 0
_TAB_QDEC = 2 * RET_CHUNK
_TAB_KDEC = 3 * RET_CHUNK
_TAB_CDEC = 4 * RET_CHUNK
_TAB_ROWS = 4 * RET_CHUNK + 8


def _ret_kernel(q_ref, k_ref, v_ref, g_ref, thl_ref, thb_ref, gn_ref, s0_ref, o_ref, sfin_ref, ob_ref,
                tab_ref, *, nc, cpt, npairs, has_s0):
    grp = pl.program_id(0)
    c = RET_CHUNK
    lane = lax.broadcasted_iota(jnp.int32, (1, LANES), 1)
    lo_head = lane < RET_DIM
    dd = lax.broadcasted_iota(jnp.int32, (LANES, LANES), 0)
    ee = lax.broadcasted_iota(jnp.int32, (LANES, LANES), 1)
    same_head = (dd < RET_DIM) == (ee < RET_DIM)
    lanes_of = lambda p: slice(p * LANES, (p + 1) * LANES)

    @pl.when(pl.program_id(1) == 0)
    def _():
        rowf = lax.broadcasted_iota(jnp.int32, (c, LANES), 0).astype(F32)
        ii = lax.broadcasted_iota(jnp.int32, (c, c), 0)
        jj = lax.broadcasted_iota(jnp.int32, (c, c), 1)
        for p in range(npairs):
            for d in range(2):
                lg_lane = _log_sigmoid(thl_ref[d, :, lanes_of(p)])
                dist = (ii - jj) if d == 0 else (jj - ii)
                for hh in range(2):
                    head = 2 * (grp * npairs + p) + hh
                    lg_h = _log_sigmoid(thb_ref[d, pl.ds(head, 1), :])
                    dm = jnp.where(dist >= 0, jnp.exp(lg_h * jnp.maximum(dist, 0).astype(F32)), 0.0)
                    tab_ref[p, d, _TAB_DMAT + hh * c:_TAB_DMAT + (hh + 1) * c, :] = dm
                if d == 0:
                    q_dec = jnp.exp(lg_lane * (rowf + 1.0))
                    k_dec = jnp.exp(lg_lane * (c - 1.0 - rowf))
                else:
                    q_dec = jnp.exp(lg_lane * (c - rowf))
                    k_dec = jnp.exp(lg_lane * rowf)
                tab_ref[p, d, _TAB_QDEC:_TAB_QDEC + c, :] = q_dec
                tab_ref[p, d, _TAB_KDEC:_TAB_KDEC + c, :] = k_dec
                tab_ref[p, d, _TAB_CDEC:_TAB_CDEC + 8, :] = jnp.broadcast_to(
                    jnp.exp(lg_lane * float(c)), (8, LANES))

    def first_level(p, d, r0):
        qc = q_ref[0, pl.ds(r0, c), lanes_of(p)]
        kc = k_ref[0, pl.ds(r0, c), lanes_of(p)]
        vcb = v_ref[0, pl.ds(r0, c), lanes_of(p)].astype(BF16)
        qs = jnp.concatenate([jnp.where(lo_head, qc, 0.0), jnp.where(lo_head, 0.0, qc)], axis=0)
        sc = lax.dot_general(qs.astype(BF16), kc.astype(BF16), _NT, preferred_element_type=F32)
        sc = sc * tab_ref[p, d, _TAB_DMAT:_TAB_DMAT + 2 * c, :]
        kd = kc * tab_ref[p, d, _TAB_KDEC:_TAB_KDEC + c, :]
        upd = jnp.where(same_head, _dot(kd.T.astype(BF16), vcb), 0.0)
        return qc, vcb, sc.astype(BF16), upd

    def second_level(p, d, lvl1, s):
        qc, vcb, scb, upd = lvl1
        pv = _dot(scb, vcb)
        qd = qc * tab_ref[p, d, _TAB_QDEC:_TAB_QDEC + c, :]
        o = _dot(qd.astype(BF16), s.astype(BF16)) + jnp.where(lo_head, pv[:c], pv[c:])
        return o, tab_ref[p, d, _TAB_CDEC:_TAB_CDEC + 1, :] * s + upd

    def init_state(p, d):
        if has_s0:
            return s0_ref[0, d, p]
        return jnp.zeros((LANES, LANES), F32)

    units = [(p, d) for p in range(npairs) for d in range(2)]

    def scan_body(n, states):
        def row0(d, j):
            idx = n * cpt + j
            return pl.multiple_of((idx if d == 0 else nc - 1 - idx) * c, c)

        lvl = {(p, d, j): first_level(p, d, row0(d, j)) for j in range(cpt) for p, d in units}
        states = list(states)
        for j in range(cpt):
            for ui, (p, d) in enumerate(units):
                o, states[ui] = second_level(p, d, lvl[(p, d, j)], states[ui])
                if d == 0:
                    o_ref[0, pl.ds(row0(d, j), c), lanes_of(p)] = o
                else:
                    ob_ref[pl.ds(row0(d, j), c), lanes_of(p)] = o
        return tuple(states)

    assert nc % cpt == 0
    states = lax.fori_loop(0, nc // cpt, scan_body, tuple(init_state(p, d) for p, d in units))
    for ui, (p, d) in enumerate(units):
        sfin_ref[0, d, p] = states[ui]

    def norm_body(n, carry):
        r0 = pl.multiple_of(n * c, c)
        for p in range(npairs):
            o = o_ref[0, pl.ds(r0, c), lanes_of(p)] + ob_ref[pl.ds(r0, c), lanes_of(p)]
            o2 = o * o
            s_lo = jnp.sum(jnp.where(lo_head, o2, 0.0), axis=-1, keepdims=True)
            s_hi = jnp.sum(jnp.where(lo_head, 0.0, o2), axis=-1, keepdims=True)
            ms = jnp.where(lo_head, s_lo, s_hi) * (1.0 / RET_DIM)
            o_ref[0, pl.ds(r0, c), lanes_of(p)] = (o * lax.rsqrt(ms + EPS) * gn_ref[:, lanes_of(p)]
                                                   * g_ref[0, pl.ds(r0, c), lanes_of(p)])
        return carry

    lax.fori_loop(0, nc, norm_body, 0, unroll=min(nc, 4))


def _ret_call(u, theta, gn, s0bd, layer=0):
    b, l, _ = u.shape
    nc = l // RET_CHUNK
    has_s0 = s0bd is not None
    cpt = 4 if nc >= 4 else nc
    npairs = max(1, 4 // cpt)
    ngrp = RET_HEADS // 2 // npairs
    w = npairs * LANES
    if not has_s0:
        s0bd = jnp.zeros((1, 2, npairs, LANES, LANES), F32)
        s0_spec = pl.BlockSpec((1, 2, npairs, LANES, LANES), lambda g, i: (0, 0, 0, 0, 0))
    else:
        s0_spec = pl.BlockSpec((1, None, 2, npairs, LANES, LANES), lambda g, i: (i, layer, 0, g, 0, 0))
    th_lane = jnp.repeat(theta, RET_DIM, axis=1).reshape(2, 1, W_RET)
    th_bcast = jnp.broadcast_to(theta[:, :, None], (2, RET_HEADS, LANES))
    col = lambda c0: pl.BlockSpec((1, l, w), lambda g, i: (i, 0, c0 // w + g))
    o, sfin = pl.pallas_call(
        functools.partial(_ret_kernel, nc=nc, cpt=cpt, npairs=npairs, has_s0=has_s0),
        grid=(ngrp, b),
        in_specs=[col(COL_QR), col(COL_KR), col(COL_VR), col(COL_GR),
                  pl.BlockSpec((2, 1, w), lambda g, i: (0, 0, g)),
                  pl.BlockSpec((2, RET_HEADS, LANES), lambda g, i: (0, 0, 0)),
                  pl.BlockSpec((1, w), lambda g, i: (0, g)),
                  s0_spec],
        out_specs=[pl.BlockSpec((1, l, w), lambda g, i: (i, 0, g)),
                   pl.BlockSpec((1, 2, npairs, LANES, LANES), lambda g, i: (i, 0, g, 0, 0))],
        out_shape=[jax.ShapeDtypeStruct((b, l, W_RET), F32),
                   jax.ShapeDtypeStruct((b, 2, RET_HEADS // 2, LANES, LANES), F32)],
        scratch_shapes=[pltpu.VMEM((l, w), F32), pltpu.VMEM((npairs, 2, _TAB_ROWS, LANES), F32)],
        compiler_params=_cparams(("arbitrary", "arbitrary")),
        name="retention_s0" if has_s0 else "retention",
    )(u, u, u, u, th_lane, th_bcast, gn.reshape(1, W_RET), s0bd)
    return o, sfin


def _blockdiag_states(s):
    sp = s.reshape(s.shape[:-3] + (RET_HEADS // 2, 2, RET_DIM, RET_DIM))
    z = jnp.zeros_like(sp[..., 0, :, :])
    top = jnp.concatenate([sp[..., 0, :, :], z], axis=-1)
    bot = jnp.concatenate([z, sp[..., 1, :, :]], axis=-1)
    return jnp.concatenate([top, bot], axis=-2)


def _diag_states(sbd):
    b = sbd.shape[0]
    s0 = sbd[:, :, :, :RET_DIM, :RET_DIM]
    s1 = sbd[:, :, :, RET_DIM:, RET_DIM:]
    return jnp.stack([s0, s1], axis=3).reshape(b, 2, RET_HEADS, RET_DIM, RET_DIM)


def _filter_positions(l):
    t = jnp.linspace(0.0, 1.0, l, dtype=F32)[:, None]
    w = 2.0 * math.pi * jnp.arange(l, dtype=F32)[:, None] / l
    f = jnp.linspace(1e-4, HY_BANDS - 1, HY_BANDS, dtype=F32)[None, :]
    z = jnp.concatenate([t, jnp.cos(f * w), -jnp.sin(f * w)], axis=-1)
    z = jnp.pad(z, ((0, 0), (0, 32 - HY_POS_FEAT)))
    return jnp.concatenate([z, z[:1], jnp.flip(z[1:], axis=0)], axis=0)


def _hyena_deltas():
    max_decay = math.log(HY_DECAY_TARGET) / HY_FAST_PCT
    min_decay = math.log(HY_DECAY_TARGET) / HY_SLOW_PCT
    return jnp.abs(jnp.linspace(min_decay, max_decay, W_HY, dtype=F32))[None, :]


def _filter_hidden(z_ref, w1_ref, b1_ref, fr_ref):
    pre = jnp.dot(z_ref[...], w1_ref[0], precision=HIGHEST, preferred_element_type=F32) + b1_ref[0]
    return jnp.sin(fr_ref[0] * pre)


def _filter_raw(hid, w2f, w2b, tp, dl, row0, l):
    win = jnp.exp(-tp * dl)
    row = row0 + lax.broadcasted_iota(jnp.int32, win.shape, 0)
    hf = jnp.dot(hid, w2f, precision=HIGHEST, preferred_element_type=F32) * win
    hb = jnp.dot(hid, w2b, precision=HIGHEST, preferred_element_type=F32) * win
    hf = jnp.where(row < l, hf, 0.0)
    hb = jnp.where((row > l) | (row == 0), hb, 0.0)
    return hf + hb, jnp.sum(jnp.abs(hf) + jnp.abs(hb), axis=0, keepdims=True)


def _with_skip(g, skip):
    row = lax.broadcasted_iota(jnp.int32, g.shape, 0)
    return g + jnp.where(row == 0, skip, 0.0)


def _filt_ctx_kernel(z_ref, w1_ref, b1_ref, fr_ref, w2_ref, dl_ref, sk_ref, fh_ref, fl_ref, g_ref):
    hid = _filter_hidden(z_ref, w1_ref, b1_ref, fr_ref)
    tp = z_ref[:, 0:1]
    for o in range(2):
        w2f = w2_ref[0, :, (2 * o) * W_HY:(2 * o + 1) * W_HY]
        w2b = w2_ref[0, :, (2 * o + 1) * W_HY:(2 * o + 2) * W_HY]
        raw, nrm = _filter_raw(hid, w2f, w2b, tp, dl_ref[...], 0, z_ref.shape[0] // 2)
        g = _with_skip(raw / nrm, sk_ref[0, pl.ds(o, 1), :])
        g_ref[0, o] = _dot3c(fh_ref[...], fl_ref[...], g)


def _ctx_dft_tables(l):
    n = 2 * l
    k = np.arange(n)[:, None]
    t = np.arange(l)[None, :]
    ang = 2.0 * np.pi * k * t / n
    c, s = np.cos(ang), np.sin(ang)
    fwd = np.block([[c, s], [-s, c]])
    inv = np.block([[c.T, -s.T], [s.T, c.T]])
    n_all = np.arange(n)[None, :]
    angg = 2.0 * np.pi * k * n_all / n
    filt = np.concatenate([np.cos(angg), -np.sin(angg)], axis=0) / n
    return _split_np(fwd), _split_np(inv), _split_np(filt)


def _filt_ctx_call(l, w1, b1, freq, w2, skip):
    n = 2 * l
    z_ext = _filter_positions(l)
    _, _, (fh, fl) = _ctx_dft_tables(l)
    w1p = jnp.pad(w1, ((0, 0), (0, 32 - HY_POS_FEAT), (0, 0)))
    lay = lambda *shape: pl.BlockSpec((1,) + shape, lambda d: (d,) + (0,) * len(shape))
    full = lambda a: pl.BlockSpec(a.shape, lambda d: (0,) * a.ndim)
    dl = _hyena_deltas()
    return pl.pallas_call(
        _filt_ctx_kernel,
        grid=(DEPTH,),
        in_specs=[full(z_ext), lay(32, HY_FILT_HID), lay(1, HY_FILT_HID), lay(1, HY_FILT_HID),
                  lay(HY_FILT_HID, 4 * W_HY), full(dl), lay(2, W_HY), full(fh), full(fl)],
        out_specs=pl.BlockSpec((1, 2, 2 * n, W_HY), lambda d: (d, 0, 0, 0)),
        out_shape=jax.ShapeDtypeStruct((DEPTH, 2, 2 * n, W_HY), F32),
        compiler_params=_cparams(("arbitrary",)),
        name="hyena_filter_ctx",
    )(z_ext, w1p, b1.reshape(DEPTH, 1, -1), freq.reshape(DEPTH, 1, -1), w2, dl, skip, fh, fl)


def _lat_dft_tables():
    ka = np.arange(NA)[:, None]
    b = np.arange(NB)[:, None, None]
    kb = np.arange(NB)[:, None]
    bb = np.arange(NB)[None, :]
    a_half = np.arange(NA // 2)[None, :]
    a_full = np.arange(NA)[None, :]
    phi = 2.0 * np.pi * (ka * a_half / NA + b * ka / LAT_N)
    c, s = np.cos(phi), np.sin(phi)
    a_fwd = np.concatenate([c, s], axis=2)
    ct, st = np.swapaxes(c, 1, 2), np.swapaxes(s, 1, 2)
    a_inv = np.concatenate([ct, st], axis=2)
    phig = 2.0 * np.pi * (ka * a_full / NA + b * ka / LAT_N)
    a_flt = np.concatenate([np.cos(phig), -np.sin(phig)], axis=1) / LAT_N
    ang = 2.0 * np.pi * kb * bb / NB
    c2, s2 = np.cos(ang), np.sin(ang)
    f_fwd = np.block([[c2, s2], [-s2, c2]])
    f_inv = np.block([[c2, -s2], [s2, c2]])
    return (_split_np(a_fwd), _split_np(a_inv), _split_np(a_flt), _split_np(f_fwd), _split_np(f_inv))


def _stage_b_rows(ka):
    re = pl.ds(ka, NB, stride=Y_PITCH)
    im = pl.ds(NA + ka, NB, stride=Y_PITCH)
    return re, im


def _filt_lat_kernel(z_ref, w1_ref, b1_ref, fr_ref, w2f_ref, w2b_ref, dl_ref, sk_ref, ah_ref, al_ref,
                     f2h_ref, f2l_ref, g_ref, hid_ref, gt_ref, y_ref):
    step = pl.program_id(1)
    rch = 1024
    nch = LAT_N // rch
    rows_of = lambda i: pl.ds(pl.multiple_of(i * rch, rch), rch)

    @pl.when(step == 0)
    def _():
        def hid_chunk(i, carry):
            r = rows_of(i)
            pre = jnp.dot(z_ref[r, :], w1_ref[0], precision=HIGHEST, preferred_element_type=F32)
            hid_ref[r, :] = jnp.sin(fr_ref[0] * (pre + b1_ref[0]))
            return carry

        lax.fori_loop(0, nch, hid_chunk, 0)

    w2f_hl = _split(w2f_ref[0])
    w2b_hl = _split(w2b_ref[0])

    def raw_chunk(w2_hl, i, nrm):
        r = rows_of(i)
        hh, hl = _split(hid_ref[r, :])
        h = _dot(hh, w2_hl[0]) + _dot(hl, w2_hl[0]) + _dot(hh, w2_hl[1])
        h = h * jnp.exp(-z_ref[r, 0:1] * dl_ref[...])
        row = i * rch + lax.broadcasted_iota(jnp.int32, h.shape, 0)
        h = jnp.where(row == LAT_L, 0.0, h)
        gt_ref[r, :] = h
        return nrm + jnp.sum(jnp.abs(h), axis=0, keepdims=True)

    nrm = lax.fori_loop(0, nch // 2, functools.partial(raw_chunk, w2f_hl), jnp.zeros((1, LANES), F32))
    nrm = lax.fori_loop(nch // 2, nch, functools.partial(raw_chunk, w2b_hl), nrm)
    hh, hl = _split(hid_ref[0:8, :])
    hb0 = _dot(hh, w2b_hl[0]) + _dot(hl, w2b_hl[0]) + _dot(hh, w2b_hl[1])
    hb0 = hb0 * jnp.exp(-z_ref[0:8, 0:1] * dl_ref[...])
    hb0 = jnp.where(lax.broadcasted_iota(jnp.int32, hb0.shape, 0) == 0, hb0, 0.0)
    gt_ref[0:8, :] = gt_ref[0:8, :] + hb0
    nrm = nrm + jnp.sum(jnp.abs(hb0), axis=0, keepdims=True)

    def norm_chunk(i, carry):
        r = rows_of(i)
        gt_ref[r, :] = gt_ref[r, :] / nrm
        return carry

    lax.fori_loop(0, nch, norm_chunk, 0)
    order = step // (W_HY // LANES)
    gt_ref[0:8, :] = _with_skip(gt_ref[0:8, :], sk_ref[0, pl.ds(order, 1), :])

    def stage_a(b, carry):
        rows = gt_ref[pl.ds(b, NA, stride=NB), :]
        y_ref[pl.ds(pl.multiple_of(b * Y_PITCH, 8), 2 * NA), :] = _dot3c(ah_ref[b], al_ref[b], rows)
        return carry

    lax.fori_loop(0, NB, stage_a, 0, unroll=4)

    def stage_b(ka, carry):
        re, im = _stage_b_rows(ka)
        z = jnp.concatenate([y_ref[re, :], y_ref[im, :]], axis=0)
        g_ref[0, 0, pl.ds(pl.multiple_of(ka * 2 * NB, 2 * NB), 2 * NB), :] = _dot3c(
            f2h_ref[...], f2l_ref[...], z)
        return carry

    lax.fori_loop(0, NA, stage_b, 0, unroll=4)


def _filt_lat_call(w1, b1, freq, w2, skip):
    z_ext = _filter_positions(LAT_L)
    _, _, (ah, al), (f2h, f2l), _ = _lat_dft_tables()
    w1p = jnp.pad(w1, ((0, 0), (0, 32 - HY_POS_FEAT), (0, 0)))
    nct = W_HY // LANES
    one = pl.Buffered(1)
    lay = lambda *shape: pl.BlockSpec((1,) + shape, lambda d, s: (d,) + (0,) * len(shape))
    full = lambda a: pl.BlockSpec(a.shape, lambda d, s: (0,) * a.ndim, pipeline_mode=one)
    dl = _hyena_deltas()
    return pl.pallas_call(
        _filt_lat_kernel,
        grid=(DEPTH, 2 * nct),
        in_specs=[full(z_ext), lay(32, HY_FILT_HID), lay(1, HY_FILT_HID), lay(1, HY_FILT_HID),
                  pl.BlockSpec((1, HY_FILT_HID, LANES), lambda d, s: (d, 0, (s // nct) * 2 * nct + s % nct)),
                  pl.BlockSpec((1, HY_FILT_HID, LANES),
                               lambda d, s: (d, 0, (s // nct) * 2 * nct + nct + s % nct)),
                  pl.BlockSpec((1, LANES), lambda d, s: (0, s % nct)),
                  pl.BlockSpec((1, 2, LANES), lambda d, s: (d, 0, s % nct)),
                  full(ah), full(al), full(f2h), full(f2l)],
        out_specs=pl.BlockSpec((1, 1, NA * 2 * NB, LANES), lambda d, s: (d, s // nct, 0, s % nct)),
        out_shape=jax.ShapeDtypeStruct((DEPTH, 2, NA * 2 * NB, W_HY), F32),
        scratch_shapes=[pltpu.VMEM((LAT_N, HY_FILT_HID), F32), pltpu.VMEM((LAT_N, LANES), F32),
                        pltpu.VMEM((NB * Y_PITCH, LANES), F32)],
        compiler_params=_cparams(("arbitrary", "arbitrary")),
        name="hyena_filter_lat",
    )(z_ext, w1p, b1.reshape(DEPTH, 1, -1), freq.reshape(DEPTH, 1, -1), w2, w2, dl, skip, ah, al, f2h,
      f2l)


def _short_conv_rows(ref, bi, r0, rows, first, last, w):
    total = ref.shape[1]
    cur = ref[bi, pl.ds(r0, rows), :]
    before = ref[bi, pl.ds(jnp.maximum(r0 - 1, 0), 1), :]
    after = ref[bi, pl.ds(jnp.minimum(r0 + rows, total - 1), 1), :]
    before = jnp.where(first, 0.0, before)
    after = jnp.where(last, 0.0, after)
    rid = lax.broadcasted_iota(jnp.int32, cur.shape, 0)
    prev = jnp.where(rid == 0, before, pltpu.roll(cur, 1, 0))
    nxt = jnp.where(rid == rows - 1, after, pltpu.roll(cur, rows - 1, 0))
    return prev * w[0:1] + cur * w[1:2] + nxt * w[2:3]


def _cmul(xr, xi, gr, gi):
    return xr * gr - xi * gi, xr * gi + xi * gr


def _hy_ctx_kernel(v_ref, x1_ref, x2_ref, cw_ref, g_ref, fh_ref, ih_ref, o_ref):
    l = v_ref.shape[1]
    n = 2 * l

    def sc(ref, bi, grp):
        w = cw_ref[:, grp * W_HY:(grp + 1) * W_HY]
        return _short_conv_rows(ref, bi, 0, l, True, True, w)

    def conv(zr, zi, order):
        x = _dot1c(fh_ref[...], jnp.concatenate([zr, zi], axis=0))
        pr, pi = _cmul(x[:n], x[n:], g_ref[order, :n], g_ref[order, n:])
        y = _dot1c(ih_ref[...], jnp.concatenate([pr, pi], axis=0))
        return y[:l], y[l:]

    yr, yi = conv(sc(v_ref, 0, 0), sc(v_ref, 1, 0), 0)
    yr, yi = conv(sc(x1_ref, 0, 1) * yr, sc(x1_ref, 1, 1) * yi, 1)
    o_ref[0] = sc(x2_ref, 0, 2) * yr
    o_ref[1] = sc(x2_ref, 1, 2) * yi


def _hy_ctx_call(u, conv_w, g_spec, layer):
    b, l, _ = u.shape
    (fh, _), (ih, _), _ = _ctx_dft_tables(l)
    grp = lambda g: pl.BlockSpec((2, l, W_HY), lambda i: (i, 0, g))
    full = lambda a: pl.BlockSpec(a.shape, lambda i: (0,) * a.ndim)
    return pl.pallas_call(
        _hy_ctx_kernel,
        grid=(b // 2,),
        in_specs=[grp(0), grp(1), grp(2), full(conv_w),
                  pl.BlockSpec((None,) + g_spec.shape[1:], lambda i: (layer, 0, 0, 0)), full(fh), full(ih)],
        out_specs=pl.BlockSpec((2, l, W_HY), lambda i: (i, 0, 0)),
        out_shape=jax.ShapeDtypeStruct((b, l, W_HY), F32),
        compiler_params=_cparams(("parallel",)),
        name="hyena_ctx",
    )(u, u, u, conv_w, g_spec, fh, ih)


def _hy_lat_kernel(z_ref, m_ref, cw_ref, g_ref, af_ref, ai_ref, f2_ref, f3_ref, o_ref, xr_scr, xi_scr,
                   y_scr, *, conv_in):
    x_scr = (xr_scr, xi_scr)
    na_half = NA // 2
    w_in = cw_ref[0] if conv_in else None
    w_mul = cw_ref[1]

    def load_in(a, carry):
        for bi in range(2):
            r0 = pl.multiple_of(a * NB, NB)
            if conv_in:
                val = _short_conv_rows(z_ref, bi, r0, NB, a == 0, a == na_half - 1, w_in)
            else:
                val = z_ref[bi, pl.ds(r0, NB), :]
            x_scr[bi][pl.ds(pl.multiple_of(a * X_PITCH, 8), NB), :] = val
        return carry

    lax.fori_loop(0, na_half, load_in, 0)

    def stage_a(b, carry):
        zr = xr_scr[pl.ds(b, na_half, stride=X_PITCH), :]
        zi = xi_scr[pl.ds(b, na_half, stride=X_PITCH), :]
        rhs = jnp.concatenate([jnp.concatenate([zr, zi], axis=0), jnp.concatenate([zi, -zr], axis=0)],
                              axis=1)
        y = _dot1c(af_ref[b], rhs)
        r0 = pl.multiple_of(b * Y_PITCH, 8)
        y_scr[pl.ds(r0, NA), :] = y[:, :LANES]
        y_scr[pl.ds(r0 + NA, NA), :] = y[:, LANES:]
        return carry

    lax.fori_loop(0, NB, stage_a, 0, unroll=16)

    def stage_b(spectral, j, carry):
        ka = 2 * j
        re, im = _stage_b_rows(ka)
        re1, im1 = _stage_b_rows(ka + 1)
        z = jnp.concatenate([jnp.concatenate([y_scr[re, :], y_scr[im, :]], axis=0),
                             jnp.concatenate([y_scr[re1, :], y_scr[im1, :]], axis=0)], axis=1)
        if spectral:
            x = _dot1c(f2_ref[...], z)
            g0 = pl.multiple_of(ka * 2 * NB, 2 * NB)
            gr = jnp.concatenate([g_ref[pl.ds(g0, NB), :], g_ref[pl.ds(g0 + 2 * NB, NB), :]], axis=1)
            gi = jnp.concatenate([g_ref[pl.ds(g0 + NB, NB), :], g_ref[pl.ds(g0 + 3 * NB, NB), :]], axis=1)
            ur, ui = _cmul(x[:NB], x[NB:], gr, gi)
        else:
            u = _dot1c(f3_ref[...], z)
            ur, ui = u[:NB], u[NB:]
        y_scr[re, :] = ur[:, :LANES]
        y_scr[im, :] = ui[:, :LANES]
        y_scr[re1, :] = ur[:, LANES:]
        y_scr[im1, :] = ui[:, LANES:]
        return carry

    for spectral in (True, False):
        lax.fori_loop(0, NA // 2, functools.partial(stage_b, spectral), 0, unroll=4)

    def stage_c(b, carry):
        r0 = pl.multiple_of(b * Y_PITCH, 8)
        ur = y_scr[pl.ds(r0, NA), :]
        ui = y_scr[pl.ds(r0 + NA, NA), :]
        rhs = jnp.concatenate([jnp.concatenate([ur, -ui], axis=0), jnp.concatenate([ui, ur], axis=0)],
                              axis=1)
        y = _dot1c(ai_ref[b], rhs)
        xr_scr[pl.ds(b, na_half, stride=X_PITCH), :] = y[:, :LANES]
        xi_scr[pl.ds(b, na_half, stride=X_PITCH), :] = y[:, LANES:]
        return carry

    lax.fori_loop(0, NB, stage_c, 0, unroll=16)

    def store_out(a, carry):
        for bi in range(2):
            r0 = pl.multiple_of(a * NB, NB)
            mul = _short_conv_rows(m_ref, bi, r0, NB, a == 0, a == na_half - 1, w_mul)
            o_ref[bi, pl.ds(r0, NB), :] = x_scr[bi][pl.ds(pl.multiple_of(a * X_PITCH, 8), NB), :] * mul
        return carry

    lax.fori_loop(0, na_half, store_out, 0)


def _hy_lat_call(src, src_col, u, mul_col, conv_w2, g_spec, layer, order, *, conv_in):
    b, l, _ = u.shape
    nct = W_HY // LANES
    (af, _), (ai, _), _, (f2, _), (f3, _) = _lat_dft_tables()
    one = pl.Buffered(1)
    blk = lambda col: pl.BlockSpec((2, l, LANES), lambda c, p: (p, 0, col + c))
    const = lambda a: pl.BlockSpec(a.shape, lambda c, p: (0,) * a.ndim, pipeline_mode=one)
    return pl.pallas_call(
        functools.partial(_hy_lat_kernel, conv_in=conv_in),
        grid=(nct, b // 2),
        in_specs=[blk(src_col), blk(mul_col),
                  pl.BlockSpec((2, 3, LANES), lambda c, p: (0, 0, c)),
                  pl.BlockSpec((None, None, NA * 2 * NB, LANES), lambda c, p: (layer, order, 0, c),
                               pipeline_mode=one),
                  const(af), const(ai), const(f2), const(f3)],
        out_specs=pl.BlockSpec((2, l, LANES), lambda c, p: (p, 0, c)),
        out_shape=jax.ShapeDtypeStruct((b, l, W_HY), F32),
        scratch_shapes=[pltpu.VMEM(((NA // 2) * X_PITCH, LANES), F32),
                        pltpu.VMEM(((NA // 2) * X_PITCH, LANES), F32),
                        pltpu.VMEM((NB * Y_PITCH, LANES), F32)],
        compiler_params=_cparams(("arbitrary", "arbitrary")),
        name="hyena_lat_conv_in" if conv_in else "hyena_lat",
    )(src, u, conv_w2, g_spec, af, ai, f2, f3)


def _rope_tables(l):
    rows = l // GRID_W
    row = jnp.repeat(jnp.arange(rows), GRID_W).astype(F32)
    col = jnp.tile(jnp.arange(GRID_W), rows).astype(F32)
    quarter = HEAD_DIM // 4
    inv = ROPE_BASE ** (-jnp.arange(quarter, dtype=F32) / quarter)
    ang = jnp.concatenate([row[:, None] * inv, col[:, None] * inv], axis=-1)
    cos, sin = jnp.cos(ang), jnp.sin(ang)
    q = quarter
    cos_h = jnp.concatenate([cos[:, :q], cos[:, :q], cos[:, q:], cos[:, q:]], axis=-1)
    sin_h = jnp.concatenate([-sin[:, :q], sin[:, :q], -sin[:, q:], sin[:, q:]], axis=-1)
    return jnp.tile(cos_h, (1, 2)), jnp.tile(sin_h, (1, 2))


def kernel(x_prompt, x_sample, c, cache_k, cache_v, state_ret, c_ctx, norm_w, w_mod, b_mod, w_in, hy_conv,
           hy_filt_w1, hy_filt_b1, hy_filt_freq, hy_filt_w2, hy_skip, attn_sink, ret_theta, ret_gn,
           w_branch_a, w_branch_b, w_branch_c, w_merge, b_merge, w_out, final_norm_w):
    d = D_MODEL
    bc, lc, _ = x_prompt.shape
    bl, ll, _ = x_sample.shape
    assert ll == LAT_L and bc % 2 == 0 and bl % 2 == 0
    past = cache_k.shape[2]

    cond = jnp.zeros((16, d), F32).at[:bl].set(c).at[bl].set(c_ctx)
    mod = _mod_call(cond, w_mod, b_mod)

    g_ctx = _filt_ctx_call(lc, hy_filt_w1, hy_filt_b1, hy_filt_freq, hy_filt_w2, hy_skip)
    g_lat = _filt_lat_call(hy_filt_w1, hy_filt_b1, hy_filt_freq, hy_filt_w2, hy_skip)

    cos_t, sin_t = _rope_tables(ll)
    w_in_b = _pack_w_in(w_in)
    wm_b = w_merge.astype(BF16)
    wa_b = w_branch_a.astype(BF16)
    wb_b = w_branch_b.astype(BF16)
    wc_b = w_branch_c.astype(BF16)
    wo_b = w_out.astype(BF16)
    fnw = final_norm_w.reshape(1, d)
    k_ctx = cache_k.reshape(bl, DEPTH, past, W_KV)
    v_ctx = cache_v.reshape(bl, DEPTH, past, W_KV)
    s0_all = _blockdiag_states(state_ret)
    hy_cols = COL_HY // LANES
    nct = W_HY // LANES

    xp, xs = x_prompt, x_sample
    ks_out, vs_out, ss_out = [], [], []
    for l in range(DEPTH):
        final = l == DEPTH - 1
        nw = norm_w[l].reshape(1, d)
        bm = b_merge[l].reshape(1, -1)
        shift, scale, gate = (mod[l, :, i * d:(i + 1) * d][:, None, :] for i in range(3))
        conv_w = hy_conv[l]
        cw = lambda g: conv_w[:, g * W_HY:(g + 1) * W_HY]

        sl = slice(bl, bl + 1)
        u = _in_call(xp, shift[sl], scale[sl], nw, w_in_b[l], cos_t, sin_t, rope=False, tm=lc)
        ya = _hy_ctx_call(u, conv_w, g_ctx, l)
        yb = _attn_ctx_call(u, attn_sink[l])
        yc, sfin = _ret_call(u, ret_theta[l], ret_gn[l], None)
        res = _out_call(xp, shift[sl], scale[sl], gate[sl], nw, ya, u, yb, yc, wm_b[l], bm, wa_b[l], wb_b[l],
                        wc_b[l], wo_b[l], fnw, final=final, tm=lc)
        xp = res[0]
        if final:
            y_prompt = res[1]
        ks_out.append(u[:, :, COL_KA:COL_KA + W_KV].reshape(bc, lc, ATT_KV_HEADS, HEAD_DIM))
        vs_out.append(u[:, :, COL_VA:COL_VA + W_KV].reshape(bc, lc, ATT_KV_HEADS, HEAD_DIM))
        ss_out.append(_diag_states(sfin))

        sl = slice(0, bl)
        u = _in_call(xs, shift[sl], scale[sl], nw, w_in_b[l], cos_t, sin_t, rope=True, tm=256)
        z1 = _hy_lat_call(u, hy_cols, u, hy_cols + nct, jnp.stack([cw(0), cw(1)]), g_lat, l, 0, conv_in=True)
        ya = _hy_lat_call(z1, 0, u, hy_cols + 2 * nct, jnp.stack([cw(2), cw(2)]), g_lat, l, 1, conv_in=False)
        yb = _attn_lat_call(u, k_ctx, v_ctx, attn_sink[l], l)
        yc, _ = _ret_call(u, ret_theta[l], ret_gn[l], s0_all, l)
        res = _out_call(xs, shift[sl], scale[sl], gate[sl], nw, ya, u, yb, yc, wm_b[l], bm, wa_b[l], wb_b[l],
                        wc_b[l], wo_b[l], fnw, final=final, tm=256)
        xs = res[0]
        if final:
            y_sample = res[1]

    new_cache_k = jnp.stack(ks_out, axis=1)
    new_cache_v = jnp.stack(vs_out, axis=1)
    new_state_ret = jnp.stack(ss_out, axis=1)
    return (y_prompt, y_sample, new_cache_k, new_cache_v, new_state_ret)
```

```python
import functools
import math

import numpy as np
import jax
import jax.numpy as jnp
from jax import lax
from jax.experimental import pallas as pl
from jax.experimental.pallas import tpu as pltpu

F32 = jnp.float32
BF16 = jnp.bfloat16
HIGHEST = lax.Precision.HIGHEST

D_MODEL = 1024
DEPTH = 4
GRID_W = 64
W_HY = 512
HY_BANDS = 8
HY_POS_FEAT = 1 + 2 * HY_BANDS
HY_FILT_HID = 64
HY_DECAY_TARGET = 1e-2
HY_FAST_PCT = 0.3
HY_SLOW_PCT = 1.5
ATT_HEADS = 8
ATT_KV_HEADS = 2
ATT_GROUP = ATT_HEADS // ATT_KV_HEADS
HEAD_DIM = 64
W_ATT = ATT_HEADS * HEAD_DIM
W_KV = ATT_KV_HEADS * HEAD_DIM
ATT_BLOCK = 128
RET_HEADS = 8
RET_DIM = 64
W_RET = RET_HEADS * RET_DIM
RET_CHUNK = 128
ROPE_BASE = 10000.0
EPS = 1e-6
NEG = -1e30

LANES = 128
MXU_ROWS = 512
VMEM_LIMIT = 56 * 1024 * 1024

IN_DIM = 5376
COL_HY = 0
COL_GH = 1536
COL_QA = 2048
COL_GA = 2560
COL_QR = 3072
COL_KR = 3584
COL_VR = 4096
COL_GR = 4608
COL_KA = 5120
COL_VA = 5248
_IN_SEGMENTS = ((0, 2560), (2816, 5376), (2560, 2816))


def _pack_w_in(w):
    return jnp.concatenate([w[..., a:b] for a, b in _IN_SEGMENTS], axis=-1).astype(BF16)

LAT_L = 4096
LAT_N = 2 * LAT_L
NA = 64
NB = 128
Y_PITCH = 136
X_PITCH = 136


def _cparams(sem):
    return pltpu.CompilerParams(dimension_semantics=sem, vmem_limit_bytes=VMEM_LIMIT)


def _split_np(a):
    a32 = np.asarray(a, np.float32)
    hi = a32.astype(BF16)
    lo = (a32 - hi.astype(np.float32)).astype(BF16)
    return jnp.asarray(hi), jnp.asarray(lo)


def _split(x):
    hi = x.astype(BF16)
    lo = (x - hi.astype(F32)).astype(BF16)
    return hi, lo


def _dot(a, b):
    return jnp.dot(a, b, preferred_element_type=F32)


def _dot3c(chi, clo, x):
    xh, xl = _split(x)
    return _dot(chi, xh) + _dot(clo, xh) + _dot(chi, xl)


def _dot1c(chi, x):
    return _dot(chi, x.astype(BF16))


def _silu(x):
    return x * jax.nn.sigmoid(x)


def _mod_kernel(c_ref, w_ref, b_ref, o_ref):
    s = _silu(c_ref[...])
    o_ref[0] = jnp.dot(s, w_ref[0], precision=HIGHEST, preferred_element_type=F32) + b_ref[0]


def _mod_call(cond, w_mod, b_mod):
    rows, d = cond.shape
    n = w_mod.shape[-1]
    tn = 1024
    return pl.pallas_call(
        _mod_kernel,
        grid=(DEPTH, n // tn),
        in_specs=[pl.BlockSpec((rows, d), lambda l, j: (0, 0)),
                  pl.BlockSpec((1, d, tn), lambda l, j: (l, 0, j)),
                  pl.BlockSpec((1, 1, tn), lambda l, j: (l, 0, j))],
        out_specs=pl.BlockSpec((1, rows, tn), lambda l, j: (l, 0, j)),
        out_shape=jax.ShapeDtypeStruct((DEPTH, rows, n), F32),
        compiler_params=_cparams(("arbitrary", "arbitrary")),
        name="adaln_mod",
    )(cond, w_mod, b_mod.reshape(DEPTH, 1, n))


def _modulated(x, nw, scale, shift):
    ms = jnp.mean(x * x, axis=-1, keepdims=True)
    h = x * lax.rsqrt(ms + EPS) * nw
    return h * (1.0 + scale) + shift


def _rope128(x, cos, sin_signed, first_half):
    up = pltpu.roll(x, LANES - 16, 1)
    dn = pltpu.roll(x, 16, 1)
    return x * cos + jnp.where(first_half, up, dn) * sin_signed


def _rows(ref):
    bt, tm, w = ref.shape
    return ref[...].reshape(bt * tm, w)


def _put(ref, c0, val):
    bt, tm, _ = ref.shape
    ref[:, :, c0:c0 + val.shape[1]] = val.reshape(bt, tm, val.shape[1])


def _in_kernel(x_ref, shift_ref, scale_ref, nw_ref, w_ref, cos_ref, sin_ref, o_ref, *, rope):
    x = _rows(x_ref)
    rows = x.shape[0]
    hb = _modulated(x, nw_ref[...], scale_ref[0], shift_ref[0]).astype(BF16)
    if rope:
        cos = cos_ref[...]
        sin = sin_ref[...]
        lane = lax.broadcasted_iota(jnp.int32, (rows, LANES), 1)
        first_half = (lane % 32) < 16

    def seg(c0, width):
        return _dot(hb, w_ref[:, c0:c0 + width])

    def put_rope(c0, val, mul):
        for i in range(val.shape[1] // LANES):
            piece = val[:, i * LANES:(i + 1) * LANES]
            if rope:
                piece = _rope128(piece, cos, sin, first_half)
            if mul is not None:
                piece = piece * mul
            _put(o_ref, c0 + i * LANES, piece)

    for g in range(3):
        _put(o_ref, COL_HY + g * 512, seg(COL_HY + g * 512, 512))
    _put(o_ref, COL_GH, _silu(seg(COL_GH, 512)))
    put_rope(COL_QA, seg(COL_QA, 512), None)
    _put(o_ref, COL_GA, _silu(seg(COL_GA, 512)))
    put_rope(COL_QR, seg(COL_QR, 512), None)
    put_rope(COL_KR, seg(COL_KR, 512), RET_DIM ** -0.5)
    _put(o_ref, COL_VR, seg(COL_VR, 512))
    _put(o_ref, COL_GR, _silu(seg(COL_GR, 512)))
    put_rope(COL_KA, seg(COL_KA, 128), None)
    _put(o_ref, COL_VA, seg(COL_VA, 128))


def _token_tiling(b, l, per_batch):
    if l >= MXU_ROWS:
        return 1, MXU_ROWS
    bt = 1 if per_batch else min(b, MXU_ROWS // l)
    return bt, l


def _in_call(x, shift, scale, nw, w, cos_t, sin_t, *, rope):
    b, l, d = x.shape
    per_batch = shift.shape[0] > 1
    bt, tm = _token_tiling(b, l, per_batch)
    assert not rope or bt == 1
    mod_map = (lambda i, j: (i, 0, 0)) if per_batch else (lambda i, j: (0, 0, 0))
    return pl.pallas_call(
        functools.partial(_in_kernel, rope=rope),
        grid=(b // bt, l // tm),
        in_specs=[pl.BlockSpec((bt, tm, d), lambda i, j: (i, j, 0)),
                  pl.BlockSpec((1, 1, d), mod_map),
                  pl.BlockSpec((1, 1, d), mod_map),
                  pl.BlockSpec((1, d), lambda i, j: (0, 0)),
                  pl.BlockSpec((d, IN_DIM), lambda i, j: (0, 0), pipeline_mode=pl.Buffered(1)),
                  pl.BlockSpec((tm, LANES), lambda i, j: (j, 0)),
                  pl.BlockSpec((tm, LANES), lambda i, j: (j, 0))],
        out_specs=pl.BlockSpec((bt, tm, IN_DIM), lambda i, j: (i, j, 0)),
        out_shape=jax.ShapeDtypeStruct((b, l, IN_DIM), F32),
        compiler_params=_cparams(("parallel", "parallel")),
        name="in_proj_rope" if rope else "in_proj",
    )(x, shift, scale, nw, w, cos_t, sin_t)


def _out_kernel(x_ref, shift_ref, scale_ref, gate_ref, nw_ref, ya_ref, gh_ref, yb_ref, yc_ref, wm_ref, bm_ref,
                wa_ref, wb_ref, wc_ref, wo_ref, fnw_ref, *out_refs, final):
    x = _rows(x_ref)
    d = x.shape[1]
    hb = _modulated(x, nw_ref[...], scale_ref[0], shift_ref[0]).astype(BF16)
    branches = (_rows(ya_ref) * _rows(gh_ref), _rows(yb_ref), _rows(yc_ref))
    merged = None
    for i, (y, w_ref) in enumerate(zip(branches, (wa_ref, wb_ref, wc_ref))):
        g = jax.nn.sigmoid(_dot(hb, wm_ref[:, i * d:(i + 1) * d]) + bm_ref[:, i * d:(i + 1) * d])
        term = g * _dot(y.astype(BF16), w_ref[...])
        merged = term if merged is None else merged + term
    out = _dot(merged.astype(BF16), wo_ref[...])
    xn = x + gate_ref[0] * out
    _put(out_refs[0], 0, xn)
    if final:
        ms = jnp.mean(xn * xn, axis=-1, keepdims=True)
        _put(out_refs[1], 0, xn * lax.rsqrt(ms + EPS) * fnw_ref[...])


def _out_call(x, shift, scale, gate, nw, ya, u, yb, yc, wm, bm, wa, wb, wc, wo, fnw, *, final):
    b, l, d = x.shape
    per_batch = shift.shape[0] > 1
    bt, tm = _token_tiling(b, l, per_batch)
    mod_map = (lambda i, j: (i, 0, 0)) if per_batch else (lambda i, j: (0, 0, 0))
    tok = lambda w: pl.BlockSpec((bt, tm, w), lambda i, j: (i, j, 0))
    full = lambda a: pl.BlockSpec(a.shape, lambda i, j: (0,) * a.ndim, pipeline_mode=pl.Buffered(1))
    n_out = 2 if final else 1
    res = pl.pallas_call(
        functools.partial(_out_kernel, final=final),
        grid=(b // bt, l // tm),
        in_specs=[tok(d), pl.BlockSpec((1, 1, d), mod_map), pl.BlockSpec((1, 1, d), mod_map),
                  pl.BlockSpec((1, 1, d), mod_map), full(nw), tok(W_HY),
                  pl.BlockSpec((bt, tm, W_HY), lambda i, j: (i, j, COL_GH // W_HY)), tok(W_ATT), tok(W_RET),
                  full(wm), full(bm), full(wa), full(wb), full(wc), full(wo), full(fnw)],
        out_specs=[tok(d)] * n_out,
        out_shape=[jax.ShapeDtypeStruct((b, l, d), F32)] * n_out,
        compiler_params=_cparams(("parallel", "parallel")),
        name="merge_out_final" if final else "merge_out",
    )(x, shift, scale, gate, nw, ya, u, yb, yc, wm, bm, wa, wb, wc, wo, fnw)
    return res


_NT = (((1,), (1,)), ((), ()))


_TN = (((0,), (0,)), ((), ()))
LOG2E = 1.4426950408889634
Q_SCALE = (HEAD_DIM ** -0.5) * LOG2E


def _attn_scores_t(q, kh, kv):
    h0 = kv * ATT_GROUP
    qs = jnp.concatenate([q[:, (h0 + g) * HEAD_DIM:(h0 + g + 1) * HEAD_DIM] for g in range(ATT_GROUP)],
                         axis=0).astype(BF16)
    return lax.dot_general(kh, qs, _NT, preferred_element_type=F32)


def _attn_finish_t(sink_ref, s, vh, g_ref, o_ref, kv, row0=0):
    tk, cols = s.shape
    t = cols // ATT_GROUP
    h0 = kv * ATT_GROUP
    head = lax.broadcasted_iota(jnp.int32, (1, cols), 1) // t
    sink = jnp.full((1, cols), sink_ref[h0], F32)
    for g in range(1, ATT_GROUP):
        sink = jnp.where(head == g, sink_ref[h0 + g], sink)
    sink = sink * LOG2E
    m = jnp.maximum(jnp.max(s, axis=0, keepdims=True), sink)
    p = jnp.exp2(s - m).astype(BF16)
    v_ext = jnp.concatenate([vh, jnp.ones((tk, HEAD_DIM), BF16)], axis=1)
    o_ext = lax.dot_general(v_ext, p, _TN, preferred_element_type=F32)
    denom = o_ext[HEAD_DIM:HEAD_DIM + 1] + jnp.exp2(sink - m)
    o = o_ext[:HEAD_DIM] / denom
    for gp in range(ATT_GROUP // 2):
        pair = jnp.concatenate([o[:, (2 * gp) * t:(2 * gp + 1) * t], o[:, (2 * gp + 1) * t:(2 * gp + 2) * t]],
                               axis=0)
        c0 = (h0 + 2 * gp) * HEAD_DIM
        o_ref[0, row0:row0 + t, c0:c0 + 2 * HEAD_DIM] = pair.T * g_ref[0, row0:row0 + t, c0:c0 + 2 * HEAD_DIM]


def _attn_ctx_kernel(sink_ref, q_ref, k_ref, v_ref, g_ref, o_ref):
    q = q_ref[0] * Q_SCALE
    k = k_ref[0].astype(BF16)
    v = v_ref[0].astype(BF16)
    scores = [_attn_scores_t(q, k[:, kv * HEAD_DIM:(kv + 1) * HEAD_DIM], kv) for kv in range(ATT_KV_HEADS)]
    for kv in range(ATT_KV_HEADS):
        _attn_finish_t(sink_ref, scores[kv], v[:, kv * HEAD_DIM:(kv + 1) * HEAD_DIM], g_ref, o_ref, kv)


def _attn_ctx_call(u, sink):
    b, l, _ = u.shape
    return pl.pallas_call(
        _attn_ctx_kernel,
        grid=(b,),
        in_specs=[pl.BlockSpec(memory_space=pltpu.SMEM),
                  pl.BlockSpec((1, l, W_ATT), lambda i: (i, 0, COL_QA // W_ATT)),
                  pl.BlockSpec((1, l, W_KV), lambda i: (i, 0, COL_KA // W_KV)),
                  pl.BlockSpec((1, l, W_KV), lambda i: (i, 0, COL_VA // W_KV)),
                  pl.BlockSpec((1, l, W_ATT), lambda i: (i, 0, COL_GA // W_ATT))],
        out_specs=pl.BlockSpec((1, l, W_ATT), lambda i: (i, 0, 0)),
        out_shape=jax.ShapeDtypeStruct((b, l, W_ATT), F32),
        compiler_params=_cparams(("parallel",)),
        name="attn_ctx",
    )(sink, u, u, u, u)


def _attn_lat_kernel(sink_ref, q_ref, kp_ref, kc_ref, kn_ref, vp_ref, vc_ref, vn_ref, kx_ref, vx_ref,
                     g_ref, o_ref):
    j = pl.program_id(1)
    last = pl.num_programs(1) - 1
    b = ATT_BLOCK
    bf = lambda ref: ref[0].astype(BF16)
    kp, kc, kn, kx = bf(kp_ref), bf(kc_ref), bf(kn_ref), bf(kx_ref)
    vp, vc, vn, vx = bf(vp_ref), bf(vc_ref), bf(vn_ref), bf(vx_ref)
    keys = (jnp.concatenate([kp, kc, kx], axis=0), jnp.concatenate([kc, kn, kx], axis=0))
    vals = (jnp.concatenate([vp, vc, vx], axis=0), jnp.concatenate([vc, vn, vx], axis=0))
    cols = ATT_GROUP * b
    c = lax.broadcasted_iota(jnp.int32, (b, cols), 0)
    r = lax.broadcasted_iota(jnp.int32, (b, cols), 1) % b
    ok_prev = (c >= r, c >= r)
    ok_next = (c <= r, c <= r)
    ok_prev = (ok_prev[0] & (j > 0), ok_prev[1])
    ok_next = (ok_next[0], ok_next[1] & (j < last))

    def band(s, t):
        return jnp.concatenate([jnp.where(ok_prev[t], s[:b], NEG), s[b:2 * b],
                                jnp.where(ok_next[t], s[2 * b:3 * b], NEG), s[3 * b:]], axis=0)

    chains = [(t, kv) for t in range(2) for kv in range(ATT_KV_HEADS)]
    scores = []
    for t, kv in chains:
        q = q_ref[0, t * b:(t + 1) * b, :] * Q_SCALE
        scores.append(band(_attn_scores_t(q, keys[t][:, kv * HEAD_DIM:(kv + 1) * HEAD_DIM], kv), t))
    for (t, kv), s in zip(chains, scores):
        _attn_finish_t(sink_ref, s, vals[t][:, kv * HEAD_DIM:(kv + 1) * HEAD_DIM], g_ref, o_ref, kv, t * b)


def _attn_lat_call(u, kctx, vctx, sink, layer):
    b, l, _ = u.shape
    nb = l // ATT_BLOCK
    past = kctx.shape[2]
    kcol = COL_KA // W_KV
    vcol = COL_VA // W_KV
    prev = lambda col: pl.BlockSpec((1, ATT_BLOCK, W_KV), lambda i, j: (i, jnp.maximum(2 * j - 1, 0), col))
    cur = lambda col: pl.BlockSpec((1, 2 * ATT_BLOCK, W_KV), lambda i, j: (i, j, col))
    nxt = lambda col: pl.BlockSpec((1, ATT_BLOCK, W_KV),
                                   lambda i, j: (i, jnp.minimum(2 * j + 2, nb - 1), col))
    ctx = pl.BlockSpec((1, None, past, W_KV), lambda i, j: (i, layer, 0, 0))
    return pl.pallas_call(
        _attn_lat_kernel,
        grid=(b, nb // 2),
        in_specs=[pl.BlockSpec(memory_space=pltpu.SMEM),
                  pl.BlockSpec((1, 2 * ATT_BLOCK, W_ATT), lambda i, j: (i, j, COL_QA // W_ATT)),
                  prev(kcol), cur(kcol), nxt(kcol), prev(vcol), cur(vcol), nxt(vcol), ctx, ctx,
                  pl.BlockSpec((1, 2 * ATT_BLOCK, W_ATT), lambda i, j: (i, j, COL_GA // W_ATT))],
        out_specs=pl.BlockSpec((1, 2 * ATT_BLOCK, W_ATT), lambda i, j: (i, j, 0)),
        out_shape=jax.ShapeDtypeStruct((b, l, W_ATT), F32),
        compiler_params=_cparams(("parallel", "parallel")),
        name="attn_lat",
    )(sink, u, u, u, u, u, u, u, kctx, vctx, u)


def _log_sigmoid(x):
    return jnp.minimum(x, 0.0) - jnp.log1p(jnp.exp(-jnp.abs(x)))


_TAB_DMAT = 0
_TAB_QDEC = 2 * RET_CHUNK
_TAB_KDEC = 3 * RET_CHUNK
_TAB_CDEC = 4 * RET_CHUNK
_TAB_ROWS = 4 * RET_CHUNK + 8


def _ret_kernel(q_ref, k_ref, v_ref, g_ref, thl_ref, thb_ref, gn_ref, s0_ref, o_ref, sfin_ref, ob_ref,
                tab_ref, *, nc, cpt, npairs, has_s0):
    grp = pl.program_id(0)
    c = RET_CHUNK
    lane = lax.broadcasted_iota(jnp.int32, (1, LANES), 1)
    lo_head = lane < RET_DIM
    dd = lax.broadcasted_iota(jnp.int32, (LANES, LANES), 0)
    ee = lax.broadcasted_iota(jnp.int32, (LANES, LANES), 1)
    same_head = (dd < RET_DIM) == (ee < RET_DIM)
    lanes_of = lambda p: slice(p * LANES, (p + 1) * LANES)

    @pl.when(pl.program_id(1) == 0)
    def _():
        rowf = lax.broadcasted_iota(jnp.int32, (c, LANES), 0).astype(F32)
        ii = lax.broadcasted_iota(jnp.int32, (c, c), 0)
        jj = lax.broadcasted_iota(jnp.int32, (c, c), 1)
        for p in range(npairs):
            for d in range(2):
                lg_lane = _log_sigmoid(thl_ref[d, :, lanes_of(p)])
                dist = (ii - jj) if d == 0 else (jj - ii)
                for hh in range(2):
                    head = 2 * (grp * npairs + p) + hh
                    lg_h = _log_sigmoid(thb_ref[d, pl.ds(head, 1), :])
                    dm = jnp.where(dist >= 0, jnp.exp(lg_h * jnp.maximum(dist, 0).astype(F32)), 0.0)
                    tab_ref[p, d, _TAB_DMAT + hh * c:_TAB_DMAT + (hh + 1) * c, :] = dm
                if d == 0:
                    q_dec = jnp.exp(lg_lane * (rowf + 1.0))
                    k_dec = jnp.exp(lg_lane * (c - 1.0 - rowf))
                else:
                    q_dec = jnp.exp(lg_lane * (c - rowf))
                    k_dec = jnp.exp(lg_lane * rowf)
                tab_ref[p, d, _TAB_QDEC:_TAB_QDEC + c, :] = q_dec
                tab_ref[p, d, _TAB_KDEC:_TAB_KDEC + c, :] = k_dec
                tab_ref[p, d, _TAB_CDEC:_TAB_CDEC + 8, :] = jnp.broadcast_to(
                    jnp.exp(lg_lane * float(c)), (8, LANES))

    def first_level(p, d, r0):
        qc = q_ref[0, pl.ds(r0, c), lanes_of(p)]
        kc = k_ref[0, pl.ds(r0, c), lanes_of(p)]
        vcb = v_ref[0, pl.ds(r0, c), lanes_of(p)].astype(BF16)
        qs = jnp.concatenate([jnp.where(lo_head, qc, 0.0), jnp.where(lo_head, 0.0, qc)], axis=0)
        sc = lax.dot_general(qs.astype(BF16), kc.astype(BF16), _NT, preferred_element_type=F32)
        sc = sc * tab_ref[p, d, _TAB_DMAT:_TAB_DMAT + 2 * c, :]
        kd = kc * tab_ref[p, d, _TAB_KDEC:_TAB_KDEC + c, :]
        upd = jnp.where(same_head, _dot(kd.T.astype(BF16), vcb), 0.0)
        return qc, vcb, sc.astype(BF16), upd

    def second_level(p, d, lvl1, s):
        qc, vcb, scb, upd = lvl1
        pv = _dot(scb, vcb)
        qd = qc * tab_ref[p, d, _TAB_QDEC:_TAB_QDEC + c, :]
        o = _dot(qd.astype(BF16), s.astype(BF16)) + jnp.where(lo_head, pv[:c], pv[c:])
        return o, tab_ref[p, d, _TAB_CDEC:_TAB_CDEC + 1, :] * s + upd

    def init_state(p, d):
        if not has_s0:
            return jnp.zeros((LANES, LANES), F32)
        z = jnp.zeros((RET_DIM, RET_DIM), F32)
        return jnp.concatenate([jnp.concatenate([s0_ref[0, d, 2 * p], z], axis=1),
                                jnp.concatenate([z, s0_ref[0, d, 2 * p + 1]], axis=1)], axis=0)

    units = [(p, d) for p in range(npairs) for d in range(2)]

    def scan_body(n, states):
        def row0(d, j):
            idx = n * cpt + j
            return pl.multiple_of((idx if d == 0 else nc - 1 - idx) * c, c)

        lvl = {(p, d, j): first_level(p, d, row0(d, j)) for j in range(cpt) for p, d in units}
        states = list(states)
        for j in range(cpt):
            for ui, (p, d) in enumerate(units):
                o, states[ui] = second_level(p, d, lvl[(p, d, j)], states[ui])
                if d == 0:
                    o_ref[0, pl.ds(row0(d, j), c), lanes_of(p)] = o
                else:
                    ob_ref[pl.ds(row0(d, j), c), lanes_of(p)] = o
        return tuple(states)

    assert nc % cpt == 0
    states = lax.fori_loop(0, nc // cpt, scan_body, tuple(init_state(p, d) for p, d in units))
    for ui, (p, d) in enumerate(units):
        sfin_ref[0, d, 2 * p] = states[ui][:RET_DIM, :RET_DIM]
        sfin_ref[0, d, 2 * p + 1] = states[ui][RET_DIM:, RET_DIM:]

    def norm_body(n, carry):
        r0 = pl.multiple_of(n * c, c)
        for p in range(npairs):
            o = o_ref[0, pl.ds(r0, c), lanes_of(p)] + ob_ref[pl.ds(r0, c), lanes_of(p)]
            o2 = o * o
            s_lo = jnp.sum(jnp.where(lo_head, o2, 0.0), axis=-1, keepdims=True)
            s_hi = jnp.sum(jnp.where(lo_head, 0.0, o2), axis=-1, keepdims=True)
            ms = jnp.where(lo_head, s_lo, s_hi) * (1.0 / RET_DIM)
            o_ref[0, pl.ds(r0, c), lanes_of(p)] = (o * lax.rsqrt(ms + EPS) * gn_ref[:, lanes_of(p)]
                                                   * g_ref[0, pl.ds(r0, c), lanes_of(p)])
        return carry

    lax.fori_loop(0, nc, norm_body, 0, unroll=min(nc, 4))


def _ret_call(u, theta, gn, s0bd, layer=0):
    b, l, _ = u.shape
    nc = l // RET_CHUNK
    has_s0 = s0bd is not None
    cpt = 4 if nc >= 4 else nc
    npairs = max(1, 4 // cpt)
    ngrp = RET_HEADS // 2 // npairs
    w = npairs * LANES
    st_block = (2, 2 * npairs, RET_DIM, RET_DIM)
    if not has_s0:
        s0bd = jnp.zeros((1,) + st_block, F32)
        s0_spec = pl.BlockSpec((1,) + st_block, lambda g, i: (0, 0, 0, 0, 0))
    else:
        s0_spec = pl.BlockSpec((1, None) + st_block, lambda g, i: (i, layer, 0, g, 0, 0))
    th_lane = jnp.repeat(theta, RET_DIM, axis=1).reshape(2, 1, W_RET)
    th_bcast = jnp.broadcast_to(theta[:, :, None], (2, RET_HEADS, LANES))
    col = lambda c0: pl.BlockSpec((1, l, w), lambda g, i: (i, 0, c0 // w + g))
    o, sfin = pl.pallas_call(
        functools.partial(_ret_kernel, nc=nc, cpt=cpt, npairs=npairs, has_s0=has_s0),
        grid=(ngrp, b),
        in_specs=[col(COL_QR), col(COL_KR), col(COL_VR), col(COL_GR),
                  pl.BlockSpec((2, 1, w), lambda g, i: (0, 0, g)),
                  pl.BlockSpec((2, RET_HEADS, LANES), lambda g, i: (0, 0, 0)),
                  pl.BlockSpec((1, w), lambda g, i: (0, g)),
                  s0_spec],
        out_specs=[pl.BlockSpec((1, l, w), lambda g, i: (i, 0, g)),
                   pl.BlockSpec((1,) + st_block, lambda g, i: (i, 0, g, 0, 0))],
        out_shape=[jax.ShapeDtypeStruct((b, l, W_RET), F32),
                   jax.ShapeDtypeStruct((b, 2, RET_HEADS, RET_DIM, RET_DIM), F32)],
        scratch_shapes=[pltpu.VMEM((l, w), F32), pltpu.VMEM((npairs, 2, _TAB_ROWS, LANES), F32)],
        compiler_params=_cparams(("arbitrary", "arbitrary")),
        name="retention_s0" if has_s0 else "retention",
    )(u, u, u, u, th_lane, th_bcast, gn.reshape(1, W_RET), s0bd)
    return o, sfin


def _filter_positions(l):
    t = jnp.linspace(0.0, 1.0, l, dtype=F32)[:, None]
    w = 2.0 * math.pi * jnp.arange(l, dtype=F32)[:, None] / l
    f = jnp.linspace(1e-4, HY_BANDS - 1, HY_BANDS, dtype=F32)[None, :]
    z = jnp.concatenate([t, jnp.cos(f * w), -jnp.sin(f * w)], axis=-1)
    z = jnp.pad(z, ((0, 0), (0, 32 - HY_POS_FEAT)))
    return jnp.concatenate([z, z[:1], jnp.flip(z[1:], axis=0)], axis=0)


def _hyena_deltas():
    max_decay = math.log(HY_DECAY_TARGET) / HY_FAST_PCT
    min_decay = math.log(HY_DECAY_TARGET) / HY_SLOW_PCT
    return jnp.abs(jnp.linspace(min_decay, max_decay, W_HY, dtype=F32))[None, :]


def _filter_hidden(z_ref, w1_ref, b1_ref, fr_ref):
    pre = jnp.dot(z_ref[...], w1_ref[0], precision=HIGHEST, preferred_element_type=F32) + b1_ref[0]
    return jnp.sin(fr_ref[0] * pre)


def _filter_raw(hid, w2f, w2b, tp, dl, row0, l):
    win = jnp.exp(-tp * dl)
    row = row0 + lax.broadcasted_iota(jnp.int32, win.shape, 0)
    hf = jnp.dot(hid, w2f, precision=HIGHEST, preferred_element_type=F32) * win
    hb = jnp.dot(hid, w2b, precision=HIGHEST, preferred_element_type=F32) * win
    hf = jnp.where(row < l, hf, 0.0)
    hb = jnp.where((row > l) | (row == 0), hb, 0.0)
    return hf + hb, jnp.sum(jnp.abs(hf) + jnp.abs(hb), axis=0, keepdims=True)


def _with_skip(g, skip):
    row = lax.broadcasted_iota(jnp.int32, g.shape, 0)
    return g + jnp.where(row == 0, skip, 0.0)


def _filt_ctx_kernel(z_ref, w1_ref, b1_ref, fr_ref, w2_ref, dl_ref, sk_ref, fh_ref, fl_ref, g_ref):
    hid = _filter_hidden(z_ref, w1_ref, b1_ref, fr_ref)
    tp = z_ref[:, 0:1]
    for o in range(2):
        w2f = w2_ref[0, :, (2 * o) * W_HY:(2 * o + 1) * W_HY]
        w2b = w2_ref[0, :, (2 * o + 1) * W_HY:(2 * o + 2) * W_HY]
        raw, nrm = _filter_raw(hid, w2f, w2b, tp, dl_ref[...], 0, z_ref.shape[0] // 2)
        g = _with_skip(raw / nrm, sk_ref[0, pl.ds(o, 1), :])
        g_ref[0, o] = _dot3c(fh_ref[...], fl_ref[...], g)


def _ctx_dft_tables(l):
    n = 2 * l
    k = np.arange(n)[:, None]
    t = np.arange(l)[None, :]
    ang = 2.0 * np.pi * k * t / n
    c, s = np.cos(ang), np.sin(ang)
    fwd = np.block([[c, s], [-s, c]])
    inv = np.block([[c.T, -s.T], [s.T, c.T]])
    n_all = np.arange(n)[None, :]
    angg = 2.0 * np.pi * k * n_all / n
    filt = np.concatenate([np.cos(angg), -np.sin(angg)], axis=0) / n
    return _split_np(fwd), _split_np(inv), _split_np(filt)


def _filt_ctx_call(l, w1, b1, freq, w2, skip):
    n = 2 * l
    z_ext = _filter_positions(l)
    _, _, (fh, fl) = _ctx_dft_tables(l)
    w1p = jnp.pad(w1, ((0, 0), (0, 32 - HY_POS_FEAT), (0, 0)))
    lay = lambda *shape: pl.BlockSpec((1,) + shape, lambda d: (d,) + (0,) * len(shape))
    full = lambda a: pl.BlockSpec(a.shape, lambda d: (0,) * a.ndim)
    dl = _hyena_deltas()
    return pl.pallas_call(
        _filt_ctx_kernel,
        grid=(DEPTH,),
        in_specs=[full(z_ext), lay(32, HY_FILT_HID), lay(1, HY_FILT_HID), lay(1, HY_FILT_HID),
                  lay(HY_FILT_HID, 4 * W_HY), full(dl), lay(2, W_HY), full(fh), full(fl)],
        out_specs=pl.BlockSpec((1, 2, 2 * n, W_HY), lambda d: (d, 0, 0, 0)),
        out_shape=jax.ShapeDtypeStruct((DEPTH, 2, 2 * n, W_HY), F32),
        compiler_params=_cparams(("arbitrary",)),
        name="hyena_filter_ctx",
    )(z_ext, w1p, b1.reshape(DEPTH, 1, -1), freq.reshape(DEPTH, 1, -1), w2, dl, skip, fh, fl)


def _lat_dft_tables():
    ka = np.arange(NA)[:, None]
    b = np.arange(NB)[:, None, None]
    kb = np.arange(NB)[:, None]
    bb = np.arange(NB)[None, :]
    a_half = np.arange(NA // 2)[None, :]
    a_full = np.arange(NA)[None, :]
    phi = 2.0 * np.pi * (ka * a_half / NA + b * ka / LAT_N)
    c, s = np.cos(phi), np.sin(phi)
    a_fwd = np.concatenate([c, s], axis=2)
    ct, st = np.swapaxes(c, 1, 2), np.swapaxes(s, 1, 2)
    a_inv = np.concatenate([ct, st], axis=2)
    phig = 2.0 * np.pi * (ka * a_full / NA + b * ka / LAT_N)
    a_flt = np.concatenate([np.cos(phig), -np.sin(phig)], axis=1) / LAT_N
    ang = 2.0 * np.pi * kb * bb / NB
    c2, s2 = np.cos(ang), np.sin(ang)
    f_fwd = np.block([[c2, s2], [-s2, c2]])
    f_inv = np.block([[c2, -s2], [s2, c2]])
    return (_split_np(a_fwd), _split_np(a_inv), _split_np(a_flt), _split_np(f_fwd), _split_np(f_inv))


def _stage_b_rows(ka):
    re = pl.ds(ka, NB, stride=Y_PITCH)
    im = pl.ds(NA + ka, NB, stride=Y_PITCH)
    return re, im


def _filt_lat_kernel(z_ref, w1_ref, b1_ref, fr_ref, w2f_ref, w2b_ref, dl_ref, sk_ref, ah_ref, f2h_ref,
                     g_ref, hid_ref, gt_ref, y_ref):
    step = pl.program_id(1)
    rch = 1024
    nch = LAT_N // rch
    rows_of = lambda i: pl.ds(pl.multiple_of(i * rch, rch), rch)

    @pl.when(step == 0)
    def _():
        def hid_chunk(i, carry):
            r = rows_of(i)
            pre = jnp.dot(z_ref[r, :], w1_ref[0], precision=HIGHEST, preferred_element_type=F32)
            hid_ref[r, :] = jnp.sin(fr_ref[0] * (pre + b1_ref[0]))
            return carry

        lax.fori_loop(0, nch, hid_chunk, 0)

    w2f_hl = _split(w2f_ref[0])
    w2b_hl = _split(w2b_ref[0])

    def raw_chunk(w2_hl, i, nrm):
        r = rows_of(i)
        hh, hl = _split(hid_ref[r, :])
        h = _dot(hh, w2_hl[0]) + _dot(hl, w2_hl[0]) + _dot(hh, w2_hl[1])
        h = h * jnp.exp(-z_ref[r, 0:1] * dl_ref[...])
        row = i * rch + lax.broadcasted_iota(jnp.int32, h.shape, 0)
        h = jnp.where(row == LAT_L, 0.0, h)
        gt_ref[r, :] = h
        return nrm + jnp.sum(jnp.abs(h), axis=0, keepdims=True)

    nrm = lax.fori_loop(0, nch // 2, functools.partial(raw_chunk, w2f_hl), jnp.zeros((1, LANES), F32))
    nrm = lax.fori_loop(nch // 2, nch, functools.partial(raw_chunk, w2b_hl), nrm)
    hh, hl = _split(hid_ref[0:8, :])
    hb0 = _dot(hh, w2b_hl[0]) + _dot(hl, w2b_hl[0]) + _dot(hh, w2b_hl[1])
    hb0 = hb0 * jnp.exp(-z_ref[0:8, 0:1] * dl_ref[...])
    hb0 = jnp.where(lax.broadcasted_iota(jnp.int32, hb0.shape, 0) == 0, hb0, 0.0)
    gt_ref[0:8, :] = gt_ref[0:8, :] + hb0
    nrm = nrm + jnp.sum(jnp.abs(hb0), axis=0, keepdims=True)

    def norm_chunk(i, carry):
        r = rows_of(i)
        gt_ref[r, :] = gt_ref[r, :] / nrm
        return carry

    lax.fori_loop(0, nch, norm_chunk, 0)
    order = step // (W_HY // LANES)
    gt_ref[0:8, :] = _with_skip(gt_ref[0:8, :], sk_ref[0, pl.ds(order, 1), :])

    def stage_a(b, carry):
        rows = gt_ref[pl.ds(b, NA, stride=NB), :]
        y_ref[pl.ds(pl.multiple_of(b * Y_PITCH, 8), 2 * NA), :] = _dot1c(ah_ref[b], rows)
        return carry

    lax.fori_loop(0, NB, stage_a, 0, unroll=8)

    def stage_b(j, carry):
        ka = 2 * j
        re, im = _stage_b_rows(ka)
        re1, im1 = _stage_b_rows(ka + 1)
        z = jnp.concatenate([jnp.concatenate([y_ref[re, :], y_ref[im, :]], axis=0),
                             jnp.concatenate([y_ref[re1, :], y_ref[im1, :]], axis=0)], axis=1)
        x = _dot1c(f2h_ref[...], z)
        g0 = pl.multiple_of(ka * 2 * NB, 2 * NB)
        g_ref[0, 0, pl.ds(g0, 2 * NB), :] = x[:, :LANES]
        g_ref[0, 0, pl.ds(g0 + 2 * NB, 2 * NB), :] = x[:, LANES:]
        return carry

    lax.fori_loop(0, NA // 2, stage_b, 0, unroll=4)


def _filt_lat_call(w1, b1, freq, w2, skip):
    z_ext = _filter_positions(LAT_L)
    _, _, (ah, _), (f2h, _), _ = _lat_dft_tables()
    w1p = jnp.pad(w1, ((0, 0), (0, 32 - HY_POS_FEAT), (0, 0)))
    nct = W_HY // LANES
    one = pl.Buffered(1)
    lay = lambda *shape: pl.BlockSpec((1,) + shape, lambda d, s: (d,) + (0,) * len(shape))
    full = lambda a: pl.BlockSpec(a.shape, lambda d, s: (0,) * a.ndim, pipeline_mode=one)
    dl = _hyena_deltas()
    return pl.pallas_call(
        _filt_lat_kernel,
        grid=(DEPTH, 2 * nct),
        in_specs=[full(z_ext), lay(32, HY_FILT_HID), lay(1, HY_FILT_HID), lay(1, HY_FILT_HID),
                  pl.BlockSpec((1, HY_FILT_HID, LANES), lambda d, s: (d, 0, (s // nct) * 2 * nct + s % nct)),
                  pl.BlockSpec((1, HY_FILT_HID, LANES),
                               lambda d, s: (d, 0, (s // nct) * 2 * nct + nct + s % nct)),
                  pl.BlockSpec((1, LANES), lambda d, s: (0, s % nct)),
                  pl.BlockSpec((1, 2, LANES), lambda d, s: (d, 0, s % nct)),
                  full(ah), full(f2h)],
        out_specs=pl.BlockSpec((1, 1, NA * 2 * NB, LANES), lambda d, s: (d, s // nct, 0, s % nct)),
        out_shape=jax.ShapeDtypeStruct((DEPTH, 2, NA * 2 * NB, W_HY), F32),
        scratch_shapes=[pltpu.VMEM((LAT_N, HY_FILT_HID), F32), pltpu.VMEM((LAT_N, LANES), F32),
                        pltpu.VMEM((NB * Y_PITCH, LANES), F32)],
        compiler_params=_cparams(("arbitrary", "arbitrary")),
        name="hyena_filter_lat",
    )(z_ext, w1p, b1.reshape(DEPTH, 1, -1), freq.reshape(DEPTH, 1, -1), w2, w2, dl, skip, ah, f2h)


def _short_conv_rows(ref, bi, r0, rows, first, last, w):
    total = ref.shape[1]
    cur = ref[bi, pl.ds(r0, rows), :]
    before = ref[bi, pl.ds(jnp.maximum(r0 - 1, 0), 1), :]
    after = ref[bi, pl.ds(jnp.minimum(r0 + rows, total - 1), 1), :]
    before = jnp.where(first, 0.0, before)
    after = jnp.where(last, 0.0, after)
    rid = lax.broadcasted_iota(jnp.int32, cur.shape, 0)
    prev = jnp.where(rid == 0, before, pltpu.roll(cur, 1, 0))
    nxt = jnp.where(rid == rows - 1, after, pltpu.roll(cur, rows - 1, 0))
    return prev * w[0:1] + cur * w[1:2] + nxt * w[2:3]


def _cmul(xr, xi, gr, gi):
    return xr * gr - xi * gi, xr * gi + xi * gr


def _hy_ctx_kernel(v_ref, x1_ref, x2_ref, cw_ref, g_ref, fh_ref, ih_ref, o_ref):
    l = v_ref.shape[1]
    n = 2 * l

    def sc(ref, bi, grp):
        w = cw_ref[:, grp * W_HY:(grp + 1) * W_HY]
        return _short_conv_rows(ref, bi, 0, l, True, True, w)

    def conv(zr, zi, order):
        x = _dot1c(fh_ref[...], jnp.concatenate([zr, zi], axis=0))
        pr, pi = _cmul(x[:n], x[n:], g_ref[order, :n], g_ref[order, n:])
        y = _dot1c(ih_ref[...], jnp.concatenate([pr, pi], axis=0))
        return y[:l], y[l:]

    yr, yi = conv(sc(v_ref, 0, 0), sc(v_ref, 1, 0), 0)
    yr, yi = conv(sc(x1_ref, 0, 1) * yr, sc(x1_ref, 1, 1) * yi, 1)
    o_ref[0] = sc(x2_ref, 0, 2) * yr
    o_ref[1] = sc(x2_ref, 1, 2) * yi


def _hy_ctx_call(u, conv_w, g_spec, layer):
    b, l, _ = u.shape
    (fh, _), (ih, _), _ = _ctx_dft_tables(l)
    grp = lambda g: pl.BlockSpec((2, l, W_HY), lambda i: (i, 0, g))
    full = lambda a: pl.BlockSpec(a.shape, lambda i: (0,) * a.ndim)
    return pl.pallas_call(
        _hy_ctx_kernel,
        grid=(b // 2,),
        in_specs=[grp(0), grp(1), grp(2), full(conv_w),
                  pl.BlockSpec((None,) + g_spec.shape[1:], lambda i: (layer, 0, 0, 0)), full(fh), full(ih)],
        out_specs=pl.BlockSpec((2, l, W_HY), lambda i: (i, 0, 0)),
        out_shape=jax.ShapeDtypeStruct((b, l, W_HY), F32),
        compiler_params=_cparams(("parallel",)),
        name="hyena_ctx",
    )(u, u, u, conv_w, g_spec, fh, ih)


def _hy_lat_kernel(z_ref, m_ref, cw_ref, g_ref, af_ref, ai_ref, f2_ref, f3_ref, o_ref, xr_scr, xi_scr,
                   y_scr, *, conv_in):
    x_scr = (xr_scr, xi_scr)
    na_half = NA // 2
    w_in = cw_ref[0] if conv_in else None
    w_mul = cw_ref[1]

    def load_in(a, carry):
        for bi in range(2):
            r0 = pl.multiple_of(a * NB, NB)
            if conv_in:
                val = _short_conv_rows(z_ref, bi, r0, NB, a == 0, a == na_half - 1, w_in)
            else:
                val = z_ref[bi, pl.ds(r0, NB), :]
            x_scr[bi][pl.ds(pl.multiple_of(a * X_PITCH, 8), NB), :] = val
        return carry

    lax.fori_loop(0, na_half, load_in, 0)

    def stage_a(b, carry):
        zr = xr_scr[pl.ds(b, na_half, stride=X_PITCH), :]
        zi = xi_scr[pl.ds(b, na_half, stride=X_PITCH), :]
        rhs = jnp.concatenate([jnp.concatenate([zr, zi], axis=0), jnp.concatenate([zi, -zr], axis=0)],
                              axis=1)
        y = _dot1c(af_ref[b], rhs)
        r0 = pl.multiple_of(b * Y_PITCH, 8)
        y_scr[pl.ds(r0, NA), :] = y[:, :LANES]
        y_scr[pl.ds(r0 + NA, NA), :] = y[:, LANES:]
        return carry

    lax.fori_loop(0, NB, stage_a, 0, unroll=16)

    def stage_b(spectral, j, carry):
        ka = 2 * j
        re, im = _stage_b_rows(ka)
        re1, im1 = _stage_b_rows(ka + 1)
        z = jnp.concatenate([jnp.concatenate([y_scr[re, :], y_scr[im, :]], axis=0),
                             jnp.concatenate([y_scr[re1, :], y_scr[im1, :]], axis=0)], axis=1)
        if spectral:
            x = _dot1c(f2_ref[...], z)
            g0 = pl.multiple_of(ka * 2 * NB, 2 * NB)
            gr = jnp.concatenate([g_ref[pl.ds(g0, NB), :], g_ref[pl.ds(g0 + 2 * NB, NB), :]], axis=1)
            gi = jnp.concatenate([g_ref[pl.ds(g0 + NB, NB), :], g_ref[pl.ds(g0 + 3 * NB, NB), :]], axis=1)
            ur, ui = _cmul(x[:NB], x[NB:], gr, gi)
        else:
            u = _dot1c(f3_ref[...], z)
            ur, ui = u[:NB], u[NB:]
        y_scr[re, :] = ur[:, :LANES]
        y_scr[im, :] = ui[:, :LANES]
        y_scr[re1, :] = ur[:, LANES:]
        y_scr[im1, :] = ui[:, LANES:]
        return carry

    for spectral in (True, False):
        lax.fori_loop(0, NA // 2, functools.partial(stage_b, spectral), 0, unroll=4)

    def stage_c(b, carry):
        r0 = pl.multiple_of(b * Y_PITCH, 8)
        ur = y_scr[pl.ds(r0, NA), :]
        ui = y_scr[pl.ds(r0 + NA, NA), :]
        rhs = jnp.concatenate([jnp.concatenate([ur, -ui], axis=0), jnp.concatenate([ui, ur], axis=0)],
                              axis=1)
        y = _dot1c(ai_ref[b], rhs)
        xr_scr[pl.ds(b, na_half, stride=X_PITCH), :] = y[:, :LANES]
        xi_scr[pl.ds(b, na_half, stride=X_PITCH), :] = y[:, LANES:]
        return carry

    lax.fori_loop(0, NB, stage_c, 0, unroll=16)

    def store_out(a, carry):
        for bi in range(2):
            r0 = pl.multiple_of(a * NB, NB)
            mul = _short_conv_rows(m_ref, bi, r0, NB, a == 0, a == na_half - 1, w_mul)
            o_ref[bi, pl.ds(r0, NB), :] = x_scr[bi][pl.ds(pl.multiple_of(a * X_PITCH, 8), NB), :] * mul
        return carry

    lax.fori_loop(0, na_half, store_out, 0)


def _hy_lat_call(src, src_col, u, mul_col, conv_w2, g_spec, layer, order, *, conv_in):
    b, l, _ = u.shape
    nct = W_HY // LANES
    (af, _), (ai, _), _, (f2, _), (f3, _) = _lat_dft_tables()
    one = pl.Buffered(1)
    blk = lambda col: pl.BlockSpec((2, l, LANES), lambda c, p: (p, 0, col + c))
    const = lambda a: pl.BlockSpec(a.shape, lambda c, p: (0,) * a.ndim, pipeline_mode=one)
    return pl.pallas_call(
        functools.partial(_hy_lat_kernel, conv_in=conv_in),
        grid=(nct, b // 2),
        in_specs=[blk(src_col), blk(mul_col),
                  pl.BlockSpec((2, 3, LANES), lambda c, p: (0, 0, c)),
                  pl.BlockSpec((None, None, NA * 2 * NB, LANES), lambda c, p: (layer, order, 0, c),
                               pipeline_mode=one),
                  const(af), const(ai), const(f2), const(f3)],
        out_specs=pl.BlockSpec((2, l, LANES), lambda c, p: (p, 0, c)),
        out_shape=jax.ShapeDtypeStruct((b, l, W_HY), F32),
        scratch_shapes=[pltpu.VMEM(((NA // 2) * X_PITCH, LANES), F32),
                        pltpu.VMEM(((NA // 2) * X_PITCH, LANES), F32),
                        pltpu.VMEM((NB * Y_PITCH, LANES), F32)],
        compiler_params=_cparams(("arbitrary", "arbitrary")),
        name="hyena_lat_conv_in" if conv_in else "hyena_lat",
    )(src, u, conv_w2, g_spec, af, ai, f2, f3)


def _rope_tables(l):
    rows = l // GRID_W
    row = jnp.repeat(jnp.arange(rows), GRID_W).astype(F32)
    col = jnp.tile(jnp.arange(GRID_W), rows).astype(F32)
    quarter = HEAD_DIM // 4
    inv = ROPE_BASE ** (-jnp.arange(quarter, dtype=F32) / quarter)
    ang = jnp.concatenate([row[:, None] * inv, col[:, None] * inv], axis=-1)
    cos, sin = jnp.cos(ang), jnp.sin(ang)
    q = quarter
    cos_h = jnp.concatenate([cos[:, :q], cos[:, :q], cos[:, q:], cos[:, q:]], axis=-1)
    sin_h = jnp.concatenate([-sin[:, :q], sin[:, :q], -sin[:, q:], sin[:, q:]], axis=-1)
    return jnp.tile(cos_h, (1, 2)), jnp.tile(sin_h, (1, 2))


def kernel(x_prompt, x_sample, c, cache_k, cache_v, state_ret, c_ctx, norm_w, w_mod, b_mod, w_in, hy_conv,
           hy_filt_w1, hy_filt_b1, hy_filt_freq, hy_filt_w2, hy_skip, attn_sink, ret_theta, ret_gn,
           w_branch_a, w_branch_b, w_branch_c, w_merge, b_merge, w_out, final_norm_w):
    d = D_MODEL
    bc, lc, _ = x_prompt.shape
    bl, ll, _ = x_sample.shape
    assert ll == LAT_L and bc % 2 == 0 and bl % 2 == 0
    past = cache_k.shape[2]

    cond = jnp.zeros((16, d), F32).at[:bl].set(c).at[bl].set(c_ctx)
    mod = _mod_call(cond, w_mod, b_mod)

    g_ctx = _filt_ctx_call(lc, hy_filt_w1, hy_filt_b1, hy_filt_freq, hy_filt_w2, hy_skip)
    g_lat = _filt_lat_call(hy_filt_w1, hy_filt_b1, hy_filt_freq, hy_filt_w2, hy_skip)

    cos_t, sin_t = _rope_tables(ll)
    w_in_b = _pack_w_in(w_in)
    wm_b = w_merge.astype(BF16)
    wa_b = w_branch_a.astype(BF16)
    wb_b = w_branch_b.astype(BF16)
    wc_b = w_branch_c.astype(BF16)
    wo_b = w_out.astype(BF16)
    fnw = final_norm_w.reshape(1, d)
    k_ctx = cache_k.reshape(bl, DEPTH, past, W_KV)
    v_ctx = cache_v.reshape(bl, DEPTH, past, W_KV)
    hy_cols = COL_HY // LANES
    nct = W_HY // LANES

    xp, xs = x_prompt, x_sample
    ks_out, vs_out, ss_out = [], [], []
    for l in range(DEPTH):
        final = l == DEPTH - 1
        nw = norm_w[l].reshape(1, d)
        bm = b_merge[l].reshape(1, -1)
        shift, scale, gate = (mod[l, :, i * d:(i + 1) * d][:, None, :] for i in range(3))
        conv_w = hy_conv[l]
        cw = lambda g: conv_w[:, g * W_HY:(g + 1) * W_HY]

        sl = slice(bl, bl + 1)
        u = _in_call(xp, shift[sl], scale[sl], nw, w_in_b[l], cos_t, sin_t, rope=False)
        ya = _hy_ctx_call(u, conv_w, g_ctx, l)
        yb = _attn_ctx_call(u, attn_sink[l])
        yc, sfin = _ret_call(u, ret_theta[l], ret_gn[l], None)
        res = _out_call(xp, shift[sl], scale[sl], gate[sl], nw, ya, u, yb, yc, wm_b[l], bm, wa_b[l], wb_b[l],
                        wc_b[l], wo_b[l], fnw, final=final)
        xp = res[0]
        if final:
            y_prompt = res[1]
        ks_out.append(u[:, :, COL_KA:COL_KA + W_KV].reshape(bc, lc, ATT_KV_HEADS, HEAD_DIM))
        vs_out.append(u[:, :, COL_VA:COL_VA + W_KV].reshape(bc, lc, ATT_KV_HEADS, HEAD_DIM))
        ss_out.append(sfin)

        sl = slice(0, bl)
        u = _in_call(xs, shift[sl], scale[sl], nw, w_in_b[l], cos_t, sin_t, rope=True)
        z1 = _hy_lat_call(u, hy_cols, u, hy_cols + nct, jnp.stack([cw(0), cw(1)]), g_lat, l, 0, conv_in=True)
        ya = _hy_lat_call(z1, 0, u, hy_cols + 2 * nct, jnp.stack([cw(2), cw(2)]), g_lat, l, 1, conv_in=False)
        yb = _attn_lat_call(u, k_ctx, v_ctx, attn_sink[l], l)
        yc, _ = _ret_call(u, ret_theta[l], ret_gn[l], state_ret, l)
        res = _out_call(xs, shift[sl], scale[sl], gate[sl], nw, ya, u, yb, yc, wm_b[l], bm, wa_b[l], wb_b[l],
                        wc_b[l], wo_b[l], fnw, final=final)
        xs = res[0]
        if final:
            y_sample = res[1]

    new_cache_k = jnp.stack(ks_out, axis=1)
    new_cache_v = jnp.stack(vs_out, axis=1)
    new_state_ret = jnp.stack(ss_out, axis=1)
    return (y_prompt, y_sample, new_cache_k, new_cache_v, new_state_ret)
```

```python
import functools
import math

import numpy as np
import jax
import jax.numpy as jnp
from jax import lax
from jax.experimental import pallas as pl
from jax.experimental.pallas import tpu as pltpu

F32 = jnp.float32
BF16 = jnp.bfloat16
HIGHEST = lax.Precision.HIGHEST

D_MODEL = 1024
DEPTH = 4
GRID_W = 64
W_HY = 512
HY_BANDS = 8
HY_POS_FEAT = 1 + 2 * HY_BANDS
HY_FILT_HID = 64
HY_DECAY_TARGET = 1e-2
HY_FAST_PCT = 0.3
HY_SLOW_PCT = 1.5
ATT_HEADS = 8
ATT_KV_HEADS = 2
ATT_GROUP = ATT_HEADS // ATT_KV_HEADS
HEAD_DIM = 64
W_ATT = ATT_HEADS * HEAD_DIM
W_KV = ATT_KV_HEADS * HEAD_DIM
ATT_BLOCK = 128
RET_HEADS = 8
RET_DIM = 64
W_RET = RET_HEADS * RET_DIM
RET_CHUNK = 128
ROPE_BASE = 10000.0
EPS = 1e-6
NEG = -1e30

LANES = 128
MXU_ROWS = 512
VMEM_LIMIT = 56 * 1024 * 1024

IN_DIM = 5376
COL_HY = 0
COL_GH = 1536
COL_QA = 2048
COL_GA = 2560
COL_QR = 3072
COL_KR = 3584
COL_VR = 4096
COL_GR = 4608
COL_KA = 5120
COL_VA = 5248
_W_COL = {COL_HY: 0, COL_HY + 512: 512, COL_HY + 1024: 1024, COL_GH: 1536, COL_QA: 2048, COL_KA: 2560,
          COL_VA: 2688, COL_GA: 2816, COL_QR: 3328, COL_KR: 3840, COL_VR: 4352, COL_GR: 4864}

LAT_L = 4096
LAT_N = 2 * LAT_L
NA = 64
NB = 128
Y_PITCH = 136
X_PITCH = 136


def _cparams(sem):
    return pltpu.CompilerParams(dimension_semantics=sem, vmem_limit_bytes=VMEM_LIMIT)


def _split_np(a):
    a32 = np.asarray(a, np.float32)
    hi = a32.astype(BF16)
    lo = (a32 - hi.astype(np.float32)).astype(BF16)
    return jnp.asarray(hi), jnp.asarray(lo)


def _split(x):
    hi = x.astype(BF16)
    lo = (x - hi.astype(F32)).astype(BF16)
    return hi, lo


def _dot(a, b):
    return jnp.dot(a, b, preferred_element_type=F32)


def _dot3c(chi, clo, x):
    xh, xl = _split(x)
    return _dot(chi, xh) + _dot(clo, xh) + _dot(chi, xl)


def _dot1c(chi, x):
    return _dot(chi, x.astype(BF16))


def _silu(x):
    return x * jax.nn.sigmoid(x)


def _mod_kernel(c_ref, w_ref, b_ref, o_ref):
    s = _silu(c_ref[...])
    o_ref[0] = jnp.dot(s, w_ref[0], precision=HIGHEST, preferred_element_type=F32) + b_ref[0]


def _mod_call(cond, w_mod, b_mod):
    rows, d = cond.shape
    n = w_mod.shape[-1]
    tn = 1024
    return pl.pallas_call(
        _mod_kernel,
        grid=(DEPTH, n // tn),
        in_specs=[pl.BlockSpec((rows, d), lambda l, j: (0, 0)),
                  pl.BlockSpec((1, d, tn), lambda l, j: (l, 0, j)),
                  pl.BlockSpec((1, 1, tn), lambda l, j: (l, 0, j))],
        out_specs=pl.BlockSpec((1, rows, tn), lambda l, j: (l, 0, j)),
        out_shape=jax.ShapeDtypeStruct((DEPTH, rows, n), F32),
        compiler_params=_cparams(("arbitrary", "arbitrary")),
        name="adaln_mod",
    )(cond, w_mod, b_mod.reshape(DEPTH, 1, n))


def _modulated(x, nw, scale, shift):
    ms = jnp.mean(x * x, axis=-1, keepdims=True)
    h = x * lax.rsqrt(ms + EPS) * nw
    return h * (1.0 + scale) + shift


def _rope128(x, cos, sin_signed, first_half):
    up = pltpu.roll(x, LANES - 16, 1)
    dn = pltpu.roll(x, 16, 1)
    return x * cos + jnp.where(first_half, up, dn) * sin_signed


def _rows(ref):
    bt, tm, w = ref.shape
    return ref[...].reshape(bt * tm, w)


def _put(ref, c0, val):
    bt, tm, _ = ref.shape
    ref[:, :, c0:c0 + val.shape[1]] = val.reshape(bt, tm, val.shape[1])


def _in_kernel(x_ref, shift_ref, scale_ref, nw_ref, w_ref, cos_ref, sin_ref, o_ref, *, rope):
    x = _rows(x_ref)
    rows = x.shape[0]
    hb = _modulated(x, nw_ref[...], scale_ref[0], shift_ref[0]).astype(BF16)
    if rope:
        cos = cos_ref[...]
        sin = sin_ref[...]
        lane = lax.broadcasted_iota(jnp.int32, (rows, LANES), 1)
        first_half = (lane % 32) < 16

    def seg(c0, width):
        w0 = _W_COL[c0]
        return _dot(hb, w_ref[:, w0:w0 + width])

    def put_rope(c0, val, mul):
        for i in range(val.shape[1] // LANES):
            piece = val[:, i * LANES:(i + 1) * LANES]
            if rope:
                piece = _rope128(piece, cos, sin, first_half)
            if mul is not None:
                piece = piece * mul
            _put(o_ref, c0 + i * LANES, piece)

    for g in range(3):
        _put(o_ref, COL_HY + g * 512, seg(COL_HY + g * 512, 512))
    _put(o_ref, COL_GH, _silu(seg(COL_GH, 512)))
    put_rope(COL_QA, seg(COL_QA, 512), None)
    _put(o_ref, COL_GA, _silu(seg(COL_GA, 512)))
    put_rope(COL_QR, seg(COL_QR, 512), None)
    put_rope(COL_KR, seg(COL_KR, 512), RET_DIM ** -0.5)
    _put(o_ref, COL_VR, seg(COL_VR, 512))
    _put(o_ref, COL_GR, _silu(seg(COL_GR, 512)))
    put_rope(COL_KA, seg(COL_KA, 128), None)
    _put(o_ref, COL_VA, seg(COL_VA, 128))


def _token_tiling(b, l, per_batch):
    if l >= MXU_ROWS:
        return 1, MXU_ROWS
    bt = 1 if per_batch else min(b, MXU_ROWS // l)
    return bt, l


def _in_call(x, shift, scale, nw, w, cos_t, sin_t, *, rope):
    b, l, d = x.shape
    per_batch = shift.shape[0] > 1
    bt, tm = _token_tiling(b, l, per_batch)
    assert not rope or bt == 1
    mod_map = (lambda i, j: (i, 0, 0)) if per_batch else (lambda i, j: (0, 0, 0))
    return pl.pallas_call(
        functools.partial(_in_kernel, rope=rope),
        grid=(b // bt, l // tm),
        in_specs=[pl.BlockSpec((bt, tm, d), lambda i, j: (i, j, 0)),
                  pl.BlockSpec((1, 1, d), mod_map),
                  pl.BlockSpec((1, 1, d), mod_map),
                  pl.BlockSpec((1, d), lambda i, j: (0, 0)),
                  pl.BlockSpec((d, IN_DIM), lambda i, j: (0, 0), pipeline_mode=pl.Buffered(1)),
                  pl.BlockSpec((tm, LANES), lambda i, j: (j, 0)),
                  pl.BlockSpec((tm, LANES), lambda i, j: (j, 0))],
        out_specs=pl.BlockSpec((bt, tm, IN_DIM), lambda i, j: (i, j, 0)),
        out_shape=jax.ShapeDtypeStruct((b, l, IN_DIM), F32),
        compiler_params=_cparams(("parallel", "parallel")),
        name="in_proj_rope" if rope else "in_proj",
    )(x, shift, scale, nw, w, cos_t, sin_t)


def _out_kernel(x_ref, shift_ref, scale_ref, gate_ref, nw_ref, ya_ref, gh_ref, yb_ref, yc_ref, wm_ref, bm_ref,
                wa_ref, wb_ref, wc_ref, wo_ref, fnw_ref, *out_refs, final):
    x = _rows(x_ref)
    d = x.shape[1]
    hb = _modulated(x, nw_ref[...], scale_ref[0], shift_ref[0]).astype(BF16)
    branches = (_rows(ya_ref) * _rows(gh_ref), _rows(yb_ref), _rows(yc_ref))
    merged = None
    for i, (y, w_ref) in enumerate(zip(branches, (wa_ref, wb_ref, wc_ref))):
        g = jax.nn.sigmoid(_dot(hb, wm_ref[:, i * d:(i + 1) * d]) + bm_ref[:, i * d:(i + 1) * d])
        term = g * _dot(y.astype(BF16), w_ref[...])
        merged = term if merged is None else merged + term
    out = _dot(merged.astype(BF16), wo_ref[...])
    xn = x + gate_ref[0] * out
    _put(out_refs[0], 0, xn)
    if final:
        ms = jnp.mean(xn * xn, axis=-1, keepdims=True)
        _put(out_refs[1], 0, xn * lax.rsqrt(ms + EPS) * fnw_ref[...])


def _out_call(x, shift, scale, gate, nw, ya, u, yb, yc, wm, bm, wa, wb, wc, wo, fnw, *, final):
    b, l, d = x.shape
    per_batch = shift.shape[0] > 1
    bt, tm = _token_tiling(b, l, per_batch)
    mod_map = (lambda i, j: (i, 0, 0)) if per_batch else (lambda i, j: (0, 0, 0))
    tok = lambda w: pl.BlockSpec((bt, tm, w), lambda i, j: (i, j, 0))
    full = lambda a: pl.BlockSpec(a.shape, lambda i, j: (0,) * a.ndim, pipeline_mode=pl.Buffered(1))
    n_out = 2 if final else 1
    res = pl.pallas_call(
        functools.partial(_out_kernel, final=final),
        grid=(b // bt, l // tm),
        in_specs=[tok(d), pl.BlockSpec((1, 1, d), mod_map), pl.BlockSpec((1, 1, d), mod_map),
                  pl.BlockSpec((1, 1, d), mod_map), full(nw), tok(W_HY),
                  pl.BlockSpec((bt, tm, W_HY), lambda i, j: (i, j, COL_GH // W_HY)), tok(W_ATT), tok(W_RET),
                  full(wm), full(bm), full(wa), full(wb), full(wc), full(wo), full(fnw)],
        out_specs=[tok(d)] * n_out,
        out_shape=[jax.ShapeDtypeStruct((b, l, d), F32)] * n_out,
        compiler_params=_cparams(("parallel", "parallel")),
        name="merge_out_final" if final else "merge_out",
    )(x, shift, scale, gate, nw, ya, u, yb, yc, wm, bm, wa, wb, wc, wo, fnw)
    return res


_NT = (((1,), (1,)), ((), ()))


_TN = (((0,), (0,)), ((), ()))
LOG2E = 1.4426950408889634
Q_SCALE = (HEAD_DIM ** -0.5) * LOG2E


def _attn_scores_t(q, kh, kv):
    h0 = kv * ATT_GROUP
    qs = jnp.concatenate([q[:, (h0 + g) * HEAD_DIM:(h0 + g + 1) * HEAD_DIM] for g in range(ATT_GROUP)],
                         axis=0).astype(BF16)
    return lax.dot_general(kh, qs, _NT, preferred_element_type=F32)


def _attn_finish_t(sink_ref, s, vh, g_ref, o_ref, kv, row0=0):
    tk, cols = s.shape
    t = cols // ATT_GROUP
    h0 = kv * ATT_GROUP
    head = lax.broadcasted_iota(jnp.int32, (1, cols), 1) // t
    sink = jnp.full((1, cols), sink_ref[h0], F32)
    for g in range(1, ATT_GROUP):
        sink = jnp.where(head == g, sink_ref[h0 + g], sink)
    sink = sink * LOG2E
    m = jnp.maximum(jnp.max(s, axis=0, keepdims=True), sink)
    p = jnp.exp2(s - m).astype(BF16)
    v_ext = jnp.concatenate([vh, jnp.ones((tk, HEAD_DIM), BF16)], axis=1)
    o_ext = lax.dot_general(v_ext, p, _TN, preferred_element_type=F32)
    denom = o_ext[HEAD_DIM:HEAD_DIM + 1] + jnp.exp2(sink - m)
    o = o_ext[:HEAD_DIM] / denom
    for gp in range(ATT_GROUP // 2):
        pair = jnp.concatenate([o[:, (2 * gp) * t:(2 * gp + 1) * t], o[:, (2 * gp + 1) * t:(2 * gp + 2) * t]],
                               axis=0)
        c0 = (h0 + 2 * gp) * HEAD_DIM
        o_ref[0, row0:row0 + t, c0:c0 + 2 * HEAD_DIM] = pair.T * g_ref[0, row0:row0 + t, c0:c0 + 2 * HEAD_DIM]


def _attn_ctx_kernel(sink_ref, q_ref, k_ref, v_ref, g_ref, o_ref):
    q = q_ref[0] * Q_SCALE
    k = k_ref[0].astype(BF16)
    v = v_ref[0].astype(BF16)
    scores = [_attn_scores_t(q, k[:, kv * HEAD_DIM:(kv + 1) * HEAD_DIM], kv) for kv in range(ATT_KV_HEADS)]
    for kv in range(ATT_KV_HEADS):
        _attn_finish_t(sink_ref, scores[kv], v[:, kv * HEAD_DIM:(kv + 1) * HEAD_DIM], g_ref, o_ref, kv)


def _attn_ctx_call(u, sink):
    b, l, _ = u.shape
    return pl.pallas_call(
        _attn_ctx_kernel,
        grid=(b,),
        in_specs=[pl.BlockSpec(memory_space=pltpu.SMEM),
                  pl.BlockSpec((1, l, W_ATT), lambda i: (i, 0, COL_QA // W_ATT)),
                  pl.BlockSpec((1, l, W_KV), lambda i: (i, 0, COL_KA // W_KV)),
                  pl.BlockSpec((1, l, W_KV), lambda i: (i, 0, COL_VA // W_KV)),
                  pl.BlockSpec((1, l, W_ATT), lambda i: (i, 0, COL_GA // W_ATT))],
        out_specs=pl.BlockSpec((1, l, W_ATT), lambda i: (i, 0, 0)),
        out_shape=jax.ShapeDtypeStruct((b, l, W_ATT), F32),
        compiler_params=_cparams(("parallel",)),
        name="attn_ctx",
    )(sink, u, u, u, u)


def _attn_lat_kernel(sink_ref, q_ref, kp_ref, kc_ref, kn_ref, vp_ref, vc_ref, vn_ref, kx_ref, vx_ref,
                     g_ref, o_ref):
    j = pl.program_id(1)
    last = pl.num_programs(1) - 1
    b = ATT_BLOCK
    bf = lambda ref: ref[0].astype(BF16)
    kp, kc, kn, kx = bf(kp_ref), bf(kc_ref), bf(kn_ref), bf(kx_ref)
    vp, vc, vn, vx = bf(vp_ref), bf(vc_ref), bf(vn_ref), bf(vx_ref)
    keys = (jnp.concatenate([kp, kc, kx], axis=0), jnp.concatenate([kc, kn, kx], axis=0))
    vals = (jnp.concatenate([vp, vc, vx], axis=0), jnp.concatenate([vc, vn, vx], axis=0))
    cols = ATT_GROUP * b
    c = lax.broadcasted_iota(jnp.int32, (b, cols), 0)
    r = lax.broadcasted_iota(jnp.int32, (b, cols), 1) % b
    ok_prev = (c >= r, c >= r)
    ok_next = (c <= r, c <= r)
    ok_prev = (ok_prev[0] & (j > 0), ok_prev[1])
    ok_next = (ok_next[0], ok_next[1] & (j < last))

    def band(s, t):
        return jnp.concatenate([jnp.where(ok_prev[t], s[:b], NEG), s[b:2 * b],
                                jnp.where(ok_next[t], s[2 * b:3 * b], NEG), s[3 * b:]], axis=0)

    chains = [(t, kv) for t in range(2) for kv in range(ATT_KV_HEADS)]
    scores = []
    for t, kv in chains:
        q = q_ref[0, t * b:(t + 1) * b, :] * Q_SCALE
        scores.append(band(_attn_scores_t(q, keys[t][:, kv * HEAD_DIM:(kv + 1) * HEAD_DIM], kv), t))
    for (t, kv), s in zip(chains, scores):
        _attn_finish_t(sink_ref, s, vals[t][:, kv * HEAD_DIM:(kv + 1) * HEAD_DIM], g_ref, o_ref, kv, t * b)


def _attn_lat_call(u, kctx, vctx, sink, layer):
    b, l, _ = u.shape
    nb = l // ATT_BLOCK
    past = kctx.shape[2]
    kcol = COL_KA // W_KV
    vcol = COL_VA // W_KV
    prev = lambda col: pl.BlockSpec((1, ATT_BLOCK, W_KV), lambda i, j: (i, jnp.maximum(2 * j - 1, 0), col))
    cur = lambda col: pl.BlockSpec((1, 2 * ATT_BLOCK, W_KV), lambda i, j: (i, j, col))
    nxt = lambda col: pl.BlockSpec((1, ATT_BLOCK, W_KV),
                                   lambda i, j: (i, jnp.minimum(2 * j + 2, nb - 1), col))
    ctx = pl.BlockSpec((1, None, past, W_KV), lambda i, j: (i, layer, 0, 0))
    return pl.pallas_call(
        _attn_lat_kernel,
        grid=(b, nb // 2),
        in_specs=[pl.BlockSpec(memory_space=pltpu.SMEM),
                  pl.BlockSpec((1, 2 * ATT_BLOCK, W_ATT), lambda i, j: (i, j, COL_QA // W_ATT)),
                  prev(kcol), cur(kcol), nxt(kcol), prev(vcol), cur(vcol), nxt(vcol), ctx, ctx,
                  pl.BlockSpec((1, 2 * ATT_BLOCK, W_ATT), lambda i, j: (i, j, COL_GA // W_ATT))],
        out_specs=pl.BlockSpec((1, 2 * ATT_BLOCK, W_ATT), lambda i, j: (i, j, 0)),
        out_shape=jax.ShapeDtypeStruct((b, l, W_ATT), F32),
        compiler_params=_cparams(("parallel", "parallel")),
        name="attn_lat",
    )(sink, u, u, u, u, u, u, u, kctx, vctx, u)


def _log_sigmoid(x):
    return jnp.minimum(x, 0.0) - jnp.log1p(jnp.exp(-jnp.abs(x)))


_TAB_DMAT = 0
_TAB_QDEC = 2 * RET_CHUNK
_TAB_KDEC = 3 * RET_CHUNK
_TAB_CDEC = 4 * RET_CHUNK
_TAB_ROWS = 4 * RET_CHUNK + 8


def _ret_kernel(q_ref, k_ref, v_ref, g_ref, thl_ref, thb_ref, gn_ref, s0_ref, o_ref, sfin_ref, ob_ref,
                tab_ref, *, nc, cpt, npairs, has_s0):
    grp = pl.program_id(0)
    c = RET_CHUNK
    lane = lax.broadcasted_iota(jnp.int32, (1, LANES), 1)
    lo_head = lane < RET_DIM
    dd = lax.broadcasted_iota(jnp.int32, (LANES, LANES), 0)
    ee = lax.broadcasted_iota(jnp.int32, (LANES, LANES), 1)
    same_head = (dd < RET_DIM) == (ee < RET_DIM)
    lanes_of = lambda p: slice(p * LANES, (p + 1) * LANES)

    @pl.when(pl.program_id(1) == 0)
    def _():
        rowf = lax.broadcasted_iota(jnp.int32, (c, LANES), 0).astype(F32)
        ii = lax.broadcasted_iota(jnp.int32, (c, c), 0)
        jj = lax.broadcasted_iota(jnp.int32, (c, c), 1)
        for p in range(npairs):
            for d in range(2):
                lg_lane = _log_sigmoid(thl_ref[d, :, lanes_of(p)])
                dist = (ii - jj) if d == 0 else (jj - ii)
                for hh in range(2):
                    head = 2 * (grp * npairs + p) + hh
                    lg_h = _log_sigmoid(thb_ref[d, pl.ds(head, 1), :])
                    dm = jnp.where(dist >= 0, jnp.exp(lg_h * jnp.maximum(dist, 0).astype(F32)), 0.0)
                    tab_ref[p, d, _TAB_DMAT + hh * c:_TAB_DMAT + (hh + 1) * c, :] = dm
                if d == 0:
                    q_dec = jnp.exp(lg_lane * (rowf + 1.0))
                    k_dec = jnp.exp(lg_lane * (c - 1.0 - rowf))
                else:
                    q_dec = jnp.exp(lg_lane * (c - rowf))
                    k_dec = jnp.exp(lg_lane * rowf)
                tab_ref[p, d, _TAB_QDEC:_TAB_QDEC + c, :] = q_dec
                tab_ref[p, d, _TAB_KDEC:_TAB_KDEC + c, :] = k_dec
                tab_ref[p, d, _TAB_CDEC:_TAB_CDEC + 8, :] = jnp.broadcast_to(
                    jnp.exp(lg_lane * float(c)), (8, LANES))

    def first_level(p, d, r0):
        qc = q_ref[0, pl.ds(r0, c), lanes_of(p)]
        kc = k_ref[0, pl.ds(r0, c), lanes_of(p)]
        vcb = v_ref[0, pl.ds(r0, c), lanes_of(p)].astype(BF16)
        qs = jnp.concatenate([jnp.where(lo_head, qc, 0.0), jnp.where(lo_head, 0.0, qc)], axis=0)
        sc = lax.dot_general(qs.astype(BF16), kc.astype(BF16), _NT, preferred_element_type=F32)
        sc = sc * tab_ref[p, d, _TAB_DMAT:_TAB_DMAT + 2 * c, :]
        kd = kc * tab_ref[p, d, _TAB_KDEC:_TAB_KDEC + c, :]
        upd = jnp.where(same_head, _dot(kd.T.astype(BF16), vcb), 0.0)
        return qc, vcb, sc.astype(BF16), upd

    def second_level(p, d, lvl1, s):
        qc, vcb, scb, upd = lvl1
        pv = _dot(scb, vcb)
        qd = qc * tab_ref[p, d, _TAB_QDEC:_TAB_QDEC + c, :]
        o = _dot(qd.astype(BF16), s.astype(BF16)) + jnp.where(lo_head, pv[:c], pv[c:])
        return o, tab_ref[p, d, _TAB_CDEC:_TAB_CDEC + 1, :] * s + upd

    def init_state(p, d):
        if not has_s0:
            return jnp.zeros((LANES, LANES), F32)
        z = jnp.zeros((RET_DIM, RET_DIM), F32)
        return jnp.concatenate([jnp.concatenate([s0_ref[0, d, 2 * p], z], axis=1),
                                jnp.concatenate([z, s0_ref[0, d, 2 * p + 1]], axis=1)], axis=0)

    units = [(p, d) for p in range(npairs) for d in range(2)]

    def scan_body(n, states):
        def row0(d, j):
            idx = n * cpt + j
            return pl.multiple_of((idx if d == 0 else nc - 1 - idx) * c, c)

        lvl = {(p, d, j): first_level(p, d, row0(d, j)) for j in range(cpt) for p, d in units}
        states = list(states)
        for j in range(cpt):
            for ui, (p, d) in enumerate(units):
                o, states[ui] = second_level(p, d, lvl[(p, d, j)], states[ui])
                if d == 0:
                    o_ref[0, pl.ds(row0(d, j), c), lanes_of(p)] = o
                else:
                    ob_ref[pl.ds(row0(d, j), c), lanes_of(p)] = o
        return tuple(states)

    assert nc % cpt == 0
    states = lax.fori_loop(0, nc // cpt, scan_body, tuple(init_state(p, d) for p, d in units))
    for ui, (p, d) in enumerate(units):
        sfin_ref[0, d, 2 * p] = states[ui][:RET_DIM, :RET_DIM]
        sfin_ref[0, d, 2 * p + 1] = states[ui][RET_DIM:, RET_DIM:]

    head_mean = jnp.where(same_head, 1.0 / RET_DIM, 0.0).astype(BF16)

    def norm_body(n, carry):
        r0 = pl.multiple_of(n * c, c)
        for p in range(npairs):
            o = o_ref[0, pl.ds(r0, c), lanes_of(p)] + ob_ref[pl.ds(r0, c), lanes_of(p)]
            o2_hi, o2_lo = _split(o * o)
            ms = _dot(o2_hi, head_mean) + _dot(o2_lo, head_mean)
            o_ref[0, pl.ds(r0, c), lanes_of(p)] = (o * lax.rsqrt(ms + EPS) * gn_ref[:, lanes_of(p)]
                                                   * g_ref[0, pl.ds(r0, c), lanes_of(p)])
        return carry

    lax.fori_loop(0, nc, norm_body, 0, unroll=min(nc, 4))


def _ret_call(u, theta, gn, s0bd, layer=0):
    b, l, _ = u.shape
    nc = l // RET_CHUNK
    has_s0 = s0bd is not None
    cpt = 4 if nc >= 4 else nc
    npairs = max(1, 4 // cpt)
    ngrp = RET_HEADS // 2 // npairs
    w = npairs * LANES
    st_block = (2, 2 * npairs, RET_DIM, RET_DIM)
    if not has_s0:
        s0bd = jnp.zeros((1,) + st_block, F32)
        s0_spec = pl.BlockSpec((1,) + st_block, lambda g, i: (0, 0, 0, 0, 0))
    else:
        s0_spec = pl.BlockSpec((1, None) + st_block, lambda g, i: (i, layer, 0, g, 0, 0))
    th_lane = jnp.repeat(theta, RET_DIM, axis=1).reshape(2, 1, W_RET)
    th_bcast = jnp.broadcast_to(theta[:, :, None], (2, RET_HEADS, LANES))
    col = lambda c0: pl.BlockSpec((1, l, w), lambda g, i: (i, 0, c0 // w + g))
    o, sfin = pl.pallas_call(
        functools.partial(_ret_kernel, nc=nc, cpt=cpt, npairs=npairs, has_s0=has_s0),
        grid=(ngrp, b),
        in_specs=[col(COL_QR), col(COL_KR), col(COL_VR), col(COL_GR),
                  pl.BlockSpec((2, 1, w), lambda g, i: (0, 0, g)),
                  pl.BlockSpec((2, RET_HEADS, LANES), lambda g, i: (0, 0, 0)),
                  pl.BlockSpec((1, w), lambda g, i: (0, g)),
                  s0_spec],
        out_specs=[pl.BlockSpec((1, l, w), lambda g, i: (i, 0, g)),
                   pl.BlockSpec((1,) + st_block, lambda g, i: (i, 0, g, 0, 0))],
        out_shape=[jax.ShapeDtypeStruct((b, l, W_RET), F32),
                   jax.ShapeDtypeStruct((b, 2, RET_HEADS, RET_DIM, RET_DIM), F32)],
        scratch_shapes=[pltpu.VMEM((l, w), F32), pltpu.VMEM((npairs, 2, _TAB_ROWS, LANES), F32)],
        compiler_params=_cparams(("arbitrary", "arbitrary")),
        name="retention_s0" if has_s0 else "retention",
    )(u, u, u, u, th_lane, th_bcast, gn.reshape(1, W_RET), s0bd)
    return o, sfin


def _filter_positions(l):
    f32 = np.float32
    t = np.linspace(0.0, 1.0, l, dtype=f32)[:, None]
    w = (f32(2.0 * math.pi) * np.arange(l, dtype=f32)[:, None] / f32(l)).astype(f32)
    f = np.linspace(1e-4, HY_BANDS - 1, HY_BANDS, dtype=f32)[None, :]
    z = np.concatenate([t, np.cos(f * w), -np.sin(f * w)], axis=-1).astype(f32)
    z = np.pad(z, ((0, 0), (0, 32 - HY_POS_FEAT)))
    return jnp.asarray(np.concatenate([z, z[:1], z[1:][::-1]], axis=0))


def _hyena_deltas():
    max_decay = math.log(HY_DECAY_TARGET) / HY_FAST_PCT
    min_decay = math.log(HY_DECAY_TARGET) / HY_SLOW_PCT
    return jnp.asarray(np.abs(np.linspace(min_decay, max_decay, W_HY, dtype=np.float32))[None, :])


def _filter_hidden(z_ref, w1_ref, b1_ref, fr_ref):
    pre = jnp.dot(z_ref[...], w1_ref[0], precision=HIGHEST, preferred_element_type=F32) + b1_ref[0]
    return jnp.sin(fr_ref[0] * pre)


def _filter_raw(hid, w2f, w2b, tp, dl, row0, l):
    win = jnp.exp(-tp * dl)
    row = row0 + lax.broadcasted_iota(jnp.int32, win.shape, 0)
    hf = jnp.dot(hid, w2f, precision=HIGHEST, preferred_element_type=F32) * win
    hb = jnp.dot(hid, w2b, precision=HIGHEST, preferred_element_type=F32) * win
    hf = jnp.where(row < l, hf, 0.0)
    hb = jnp.where((row > l) | (row == 0), hb, 0.0)
    return hf + hb, jnp.sum(jnp.abs(hf) + jnp.abs(hb), axis=0, keepdims=True)


def _with_skip(g, skip):
    row = lax.broadcasted_iota(jnp.int32, g.shape, 0)
    return g + jnp.where(row == 0, skip, 0.0)


def _filt_ctx_kernel(z_ref, w1_ref, b1_ref, fr_ref, w2_ref, dl_ref, sk_ref, fh_ref, fl_ref, g_ref):
    hid = _filter_hidden(z_ref, w1_ref, b1_ref, fr_ref)
    tp = z_ref[:, 0:1]
    for o in range(2):
        w2f = w2_ref[0, :, (2 * o) * W_HY:(2 * o + 1) * W_HY]
        w2b = w2_ref[0, :, (2 * o + 1) * W_HY:(2 * o + 2) * W_HY]
        raw, nrm = _filter_raw(hid, w2f, w2b, tp, dl_ref[...], 0, z_ref.shape[0] // 2)
        g = _with_skip(raw / nrm, sk_ref[0, pl.ds(o, 1), :])
        g_ref[0, o] = _dot3c(fh_ref[...], fl_ref[...], g)


def _ctx_dft_tables(l):
    n = 2 * l
    k = np.arange(n)[:, None]
    t = np.arange(l)[None, :]
    ang = 2.0 * np.pi * k * t / n
    c, s = np.cos(ang), np.sin(ang)
    fwd = np.block([[c, s], [-s, c]])
    inv = np.block([[c.T, -s.T], [s.T, c.T]])
    n_all = np.arange(n)[None, :]
    angg = 2.0 * np.pi * k * n_all / n
    filt = np.concatenate([np.cos(angg), -np.sin(angg)], axis=0) / n
    return _split_np(fwd), _split_np(inv), _split_np(filt)


def _filt_ctx_call(l, w1, b1, freq, w2, skip):
    n = 2 * l
    z_ext = _filter_positions(l)
    _, _, (fh, fl) = _ctx_dft_tables(l)
    w1p = jnp.pad(w1, ((0, 0), (0, 32 - HY_POS_FEAT), (0, 0)))
    lay = lambda *shape: pl.BlockSpec((1,) + shape, lambda d: (d,) + (0,) * len(shape))
    full = lambda a: pl.BlockSpec(a.shape, lambda d: (0,) * a.ndim)
    dl = _hyena_deltas()
    return pl.pallas_call(
        _filt_ctx_kernel,
        grid=(DEPTH,),
        in_specs=[full(z_ext), lay(32, HY_FILT_HID), lay(1, HY_FILT_HID), lay(1, HY_FILT_HID),
                  lay(HY_FILT_HID, 4 * W_HY), full(dl), lay(2, W_HY), full(fh), full(fl)],
        out_specs=pl.BlockSpec((1, 2, 2 * n, W_HY), lambda d: (d, 0, 0, 0)),
        out_shape=jax.ShapeDtypeStruct((DEPTH, 2, 2 * n, W_HY), F32),
        compiler_params=_cparams(("arbitrary",)),
        name="hyena_filter_ctx",
    )(z_ext, w1p, b1.reshape(DEPTH, 1, -1), freq.reshape(DEPTH, 1, -1), w2, dl, skip, fh, fl)


def _lat_dft_tables():
    ka = np.arange(NA)[:, None]
    b = np.arange(NB)[:, None, None]
    kb = np.arange(NB)[:, None]
    bb = np.arange(NB)[None, :]
    a_half = np.arange(NA // 2)[None, :]
    a_full = np.arange(NA)[None, :]
    phi = 2.0 * np.pi * (ka * a_half / NA + b * ka / LAT_N)
    c, s = np.cos(phi), np.sin(phi)
    a_fwd = np.concatenate([c, s], axis=2)
    ct, st = np.swapaxes(c, 1, 2), np.swapaxes(s, 1, 2)
    a_inv = np.concatenate([ct, st], axis=2)
    phig = 2.0 * np.pi * (ka * a_full / NA + b * ka / LAT_N)
    a_flt = np.concatenate([np.cos(phig), -np.sin(phig)], axis=1) / LAT_N
    ang = 2.0 * np.pi * kb * bb / NB
    c2, s2 = np.cos(ang), np.sin(ang)
    f_fwd = np.block([[c2, s2], [-s2, c2]])
    f_inv = np.block([[c2, -s2], [s2, c2]])
    return (_split_np(a_fwd), _split_np(a_inv), _split_np(a_flt), _split_np(f_fwd), _split_np(f_inv))


def _stage_b_rows(ka):
    re = pl.ds(ka, NB, stride=Y_PITCH)
    im = pl.ds(NA + ka, NB, stride=Y_PITCH)
    return re, im


def _filt_lat_kernel(z_ref, w1_ref, b1_ref, fr_ref, w2f_ref, w2b_ref, dl_ref, sk_ref, ah_ref, f2h_ref,
                     g_ref, hid_ref, gt_ref, y_ref):
    step = pl.program_id(1)
    rch = 1024
    nch = LAT_N // rch
    rows_of = lambda i: pl.ds(pl.multiple_of(i * rch, rch), rch)

    @pl.when(step == 0)
    def _():
        def hid_chunk(i, carry):
            r = rows_of(i)
            pre = jnp.dot(z_ref[r, :], w1_ref[0], precision=HIGHEST, preferred_element_type=F32)
            hid_ref[r, :] = jnp.sin(fr_ref[0] * (pre + b1_ref[0]))
            return carry

        lax.fori_loop(0, nch, hid_chunk, 0)

    w2f_hl = _split(w2f_ref[0])
    w2b_hl = _split(w2b_ref[0])

    def raw_chunk(w2_hl, i, nrm):
        r = rows_of(i)
        hh, hl = _split(hid_ref[r, :])
        h = _dot(hh, w2_hl[0]) + _dot(hl, w2_hl[0]) + _dot(hh, w2_hl[1])
        h = h * jnp.exp(-z_ref[r, 0:1] * dl_ref[...])
        row = i * rch + lax.broadcasted_iota(jnp.int32, h.shape, 0)
        h = jnp.where(row == LAT_L, 0.0, h)
        gt_ref[r, :] = h
        return nrm + jnp.sum(jnp.abs(h), axis=0, keepdims=True)

    nrm = lax.fori_loop(0, nch // 2, functools.partial(raw_chunk, w2f_hl), jnp.zeros((1, LANES), F32))
    nrm = lax.fori_loop(nch // 2, nch, functools.partial(raw_chunk, w2b_hl), nrm)
    hh, hl = _split(hid_ref[0:8, :])
    hb0 = _dot(hh, w2b_hl[0]) + _dot(hl, w2b_hl[0]) + _dot(hh, w2b_hl[1])
    hb0 = hb0 * jnp.exp(-z_ref[0:8, 0:1] * dl_ref[...])
    hb0 = jnp.where(lax.broadcasted_iota(jnp.int32, hb0.shape, 0) == 0, hb0, 0.0)
    gt_ref[0:8, :] = gt_ref[0:8, :] + hb0
    nrm = nrm + jnp.sum(jnp.abs(hb0), axis=0, keepdims=True)

    def norm_chunk(i, carry):
        r = rows_of(i)
        gt_ref[r, :] = gt_ref[r, :] / nrm
        return carry

    lax.fori_loop(0, nch, norm_chunk, 0)
    order = step // (W_HY // LANES)
    gt_ref[0:8, :] = _with_skip(gt_ref[0:8, :], sk_ref[0, pl.ds(order, 1), :])

    def stage_a(b, carry):
        rows = gt_ref[pl.ds(b, NA, stride=NB), :]
        y_ref[pl.ds(pl.multiple_of(b * Y_PITCH, 8), 2 * NA), :] = _dot1c(ah_ref[b], rows)
        return carry

    lax.fori_loop(0, NB, stage_a, 0, unroll=8)

    def stage_b(j, carry):
        ka = 2 * j
        re, im = _stage_b_rows(ka)
        re1, im1 = _stage_b_rows(ka + 1)
        z = jnp.concatenate([jnp.concatenate([y_ref[re, :], y_ref[im, :]], axis=0),
                             jnp.concatenate([y_ref[re1, :], y_ref[im1, :]], axis=0)], axis=1)
        x = _dot1c(f2h_ref[...], z)
        g0 = pl.multiple_of(ka * 2 * NB, 2 * NB)
        g_ref[0, 0, pl.ds(g0, 2 * NB), :] = x[:, :LANES]
        g_ref[0, 0, pl.ds(g0 + 2 * NB, 2 * NB), :] = x[:, LANES:]
        return carry

    lax.fori_loop(0, NA // 2, stage_b, 0, unroll=4)


def _filt_lat_call(w1, b1, freq, w2, skip):
    z_ext = _filter_positions(LAT_L)
    _, _, (ah, _), (f2h, _), _ = _lat_dft_tables()
    w1p = jnp.pad(w1, ((0, 0), (0, 32 - HY_POS_FEAT), (0, 0)))
    nct = W_HY // LANES
    one = pl.Buffered(1)
    lay = lambda *shape: pl.BlockSpec((1,) + shape, lambda d, s: (d,) + (0,) * len(shape))
    full = lambda a: pl.BlockSpec(a.shape, lambda d, s: (0,) * a.ndim, pipeline_mode=one)
    dl = _hyena_deltas()
    return pl.pallas_call(
        _filt_lat_kernel,
        grid=(DEPTH, 2 * nct),
        in_specs=[full(z_ext), lay(32, HY_FILT_HID), lay(1, HY_FILT_HID), lay(1, HY_FILT_HID),
                  pl.BlockSpec((1, HY_FILT_HID, LANES), lambda d, s: (d, 0, (s // nct) * 2 * nct + s % nct)),
                  pl.BlockSpec((1, HY_FILT_HID, LANES),
                               lambda d, s: (d, 0, (s // nct) * 2 * nct + nct + s % nct)),
                  pl.BlockSpec((1, LANES), lambda d, s: (0, s % nct)),
                  pl.BlockSpec((1, 2, LANES), lambda d, s: (d, 0, s % nct)),
                  full(ah), full(f2h)],
        out_specs=pl.BlockSpec((1, 1, NA * 2 * NB, LANES), lambda d, s: (d, s // nct, 0, s % nct)),
        out_shape=jax.ShapeDtypeStruct((DEPTH, 2, NA * 2 * NB, W_HY), F32),
        scratch_shapes=[pltpu.VMEM((LAT_N, HY_FILT_HID), F32), pltpu.VMEM((LAT_N, LANES), F32),
                        pltpu.VMEM((NB * Y_PITCH, LANES), F32)],
        compiler_params=_cparams(("arbitrary", "arbitrary")),
        name="hyena_filter_lat",
    )(z_ext, w1p, b1.reshape(DEPTH, 1, -1), freq.reshape(DEPTH, 1, -1), w2, w2, dl, skip, ah, f2h)


def _short_conv_rows(ref, bi, r0, rows, first, last, w):
    total = ref.shape[1]
    cur = ref[bi, pl.ds(r0, rows), :]
    before = ref[bi, pl.ds(jnp.maximum(r0 - 1, 0), 1), :]
    after = ref[bi, pl.ds(jnp.minimum(r0 + rows, total - 1), 1), :]
    before = jnp.where(first, 0.0, before)
    after = jnp.where(last, 0.0, after)
    rid = lax.broadcasted_iota(jnp.int32, cur.shape, 0)
    prev = jnp.where(rid == 0, before, pltpu.roll(cur, 1, 0))
    nxt = jnp.where(rid == rows - 1, after, pltpu.roll(cur, rows - 1, 0))
    return prev * w[0:1] + cur * w[1:2] + nxt * w[2:3]


def _cmul(xr, xi, gr, gi):
    return xr * gr - xi * gi, xr * gi + xi * gr


def _hy_ctx_kernel(v_ref, x1_ref, x2_ref, cw_ref, g_ref, fh_ref, ih_ref, o_ref):
    l = v_ref.shape[1]
    n = 2 * l

    def sc(ref, bi, grp):
        w = cw_ref[:, grp * W_HY:(grp + 1) * W_HY]
        return _short_conv_rows(ref, bi, 0, l, True, True, w)

    def conv(zr, zi, order):
        x = _dot1c(fh_ref[...], jnp.concatenate([zr, zi], axis=0))
        pr, pi = _cmul(x[:n], x[n:], g_ref[order, :n], g_ref[order, n:])
        y = _dot1c(ih_ref[...], jnp.concatenate([pr, pi], axis=0))
        return y[:l], y[l:]

    yr, yi = conv(sc(v_ref, 0, 0), sc(v_ref, 1, 0), 0)
    yr, yi = conv(sc(x1_ref, 0, 1) * yr, sc(x1_ref, 1, 1) * yi, 1)
    o_ref[0] = sc(x2_ref, 0, 2) * yr
    o_ref[1] = sc(x2_ref, 1, 2) * yi


def _hy_ctx_call(u, conv_w, g_spec, layer):
    b, l, _ = u.shape
    (fh, _), (ih, _), _ = _ctx_dft_tables(l)
    grp = lambda g: pl.BlockSpec((2, l, W_HY), lambda i: (i, 0, g))
    full = lambda a: pl.BlockSpec(a.shape, lambda i: (0,) * a.ndim)
    return pl.pallas_call(
        _hy_ctx_kernel,
        grid=(b // 2,),
        in_specs=[grp(0), grp(1), grp(2), full(conv_w),
                  pl.BlockSpec((None,) + g_spec.shape[1:], lambda i: (layer, 0, 0, 0)), full(fh), full(ih)],
        out_specs=pl.BlockSpec((2, l, W_HY), lambda i: (i, 0, 0)),
        out_shape=jax.ShapeDtypeStruct((b, l, W_HY), F32),
        compiler_params=_cparams(("parallel",)),
        name="hyena_ctx",
    )(u, u, u, conv_w, g_spec, fh, ih)


def _hy_lat_kernel(z_ref, m_ref, cw_ref, g_ref, af_ref, ai_ref, f2_ref, f3_ref, o_ref, xr_scr, xi_scr,
                   y_scr, *, conv_in):
    x_scr = (xr_scr, xi_scr)
    na_half = NA // 2
    w_in = cw_ref[0] if conv_in else None
    w_mul = cw_ref[1]

    def load_in(a, carry):
        for bi in range(2):
            r0 = pl.multiple_of(a * NB, NB)
            if conv_in:
                val = _short_conv_rows(z_ref, bi, r0, NB, a == 0, a == na_half - 1, w_in)
            else:
                val = z_ref[bi, pl.ds(r0, NB), :]
            x_scr[bi][pl.ds(pl.multiple_of(a * X_PITCH, 8), NB), :] = val
        return carry

    lax.fori_loop(0, na_half, load_in, 0)

    def stage_a(b, carry):
        zr = xr_scr[pl.ds(b, na_half, stride=X_PITCH), :]
        zi = xi_scr[pl.ds(b, na_half, stride=X_PITCH), :]
        rhs = jnp.concatenate([jnp.concatenate([zr, zi], axis=0), jnp.concatenate([zi, -zr], axis=0)],
                              axis=1)
        y = _dot1c(af_ref[b], rhs)
        r0 = pl.multiple_of(b * Y_PITCH, 8)
        y_scr[pl.ds(r0, NA), :] = y[:, :LANES]
        y_scr[pl.ds(r0 + NA, NA), :] = y[:, LANES:]
        return carry

    lax.fori_loop(0, NB, stage_a, 0, unroll=16)

    def spectrum_product(j):
        ka = 2 * j
        re, im = _stage_b_rows(ka)
        re1, im1 = _stage_b_rows(ka + 1)
        z = jnp.concatenate([jnp.concatenate([y_scr[re, :], y_scr[im, :]], axis=0),
                             jnp.concatenate([y_scr[re1, :], y_scr[im1, :]], axis=0)], axis=1)
        x = _dot1c(f2_ref[...], z)
        g0 = pl.multiple_of(ka * 2 * NB, 2 * NB)
        gr = jnp.concatenate([g_ref[pl.ds(g0, NB), :], g_ref[pl.ds(g0 + 2 * NB, NB), :]], axis=1)
        gi = jnp.concatenate([g_ref[pl.ds(g0 + NB, NB), :], g_ref[pl.ds(g0 + 3 * NB, NB), :]], axis=1)
        pr, pi = _cmul(x[:NB], x[NB:], gr, gi)
        return jnp.concatenate([pr, pi], axis=0).astype(BF16)

    def inverse_b(j, prod):
        ka = 2 * j
        re, im = _stage_b_rows(ka)
        re1, im1 = _stage_b_rows(ka + 1)
        u = _dot(f3_ref[...], prod)
        y_scr[re, :] = u[:NB, :LANES]
        y_scr[im, :] = u[NB:, :LANES]
        y_scr[re1, :] = u[:NB, LANES:]
        y_scr[im1, :] = u[NB:, LANES:]

    def stage_b(j, prod):
        nxt = spectrum_product(j + 1)
        inverse_b(j, prod)
        return nxt

    last = lax.fori_loop(0, NA // 2 - 1, stage_b, spectrum_product(0), unroll=4)
    inverse_b(NA // 2 - 1, last)

    def stage_c(b, carry):
        r0 = pl.multiple_of(b * Y_PITCH, 8)
        ur = y_scr[pl.ds(r0, NA), :]
        ui = y_scr[pl.ds(r0 + NA, NA), :]
        rhs = jnp.concatenate([jnp.concatenate([ur, -ui], axis=0), jnp.concatenate([ui, ur], axis=0)],
                              axis=1)
        y = _dot1c(ai_ref[b], rhs)
        xr_scr[pl.ds(b, na_half, stride=X_PITCH), :] = y[:, :LANES]
        xi_scr[pl.ds(b, na_half, stride=X_PITCH), :] = y[:, LANES:]
        return carry

    lax.fori_loop(0, NB, stage_c, 0, unroll=16)

    def store_out(a, carry):
        for bi in range(2):
            r0 = pl.multiple_of(a * NB, NB)
            mul = _short_conv_rows(m_ref, bi, r0, NB, a == 0, a == na_half - 1, w_mul)
            o_ref[bi, pl.ds(r0, NB), :] = x_scr[bi][pl.ds(pl.multiple_of(a * X_PITCH, 8), NB), :] * mul
        return carry

    lax.fori_loop(0, na_half, store_out, 0)


def _hy_lat_call(src, src_col, u, mul_col, conv_w2, g_spec, layer, order, *, conv_in):
    b, l, _ = u.shape
    nct = W_HY // LANES
    (af, _), (ai, _), _, (f2, _), (f3, _) = _lat_dft_tables()
    one = pl.Buffered(1)
    blk = lambda col: pl.BlockSpec((2, l, LANES), lambda c, p: (p, 0, col + c))
    const = lambda a: pl.BlockSpec(a.shape, lambda c, p: (0,) * a.ndim, pipeline_mode=one)
    return pl.pallas_call(
        functools.partial(_hy_lat_kernel, conv_in=conv_in),
        grid=(nct, b // 2),
        in_specs=[blk(src_col), blk(mul_col),
                  pl.BlockSpec((2, 3, LANES), lambda c, p: (0, 0, c)),
                  pl.BlockSpec((None, None, NA * 2 * NB, LANES), lambda c, p: (layer, order, 0, c),
                               pipeline_mode=one),
                  const(af), const(ai), const(f2), const(f3)],
        out_specs=pl.BlockSpec((2, l, LANES), lambda c, p: (p, 0, c)),
        out_shape=jax.ShapeDtypeStruct((b, l, W_HY), F32),
        scratch_shapes=[pltpu.VMEM(((NA // 2) * X_PITCH, LANES), F32),
                        pltpu.VMEM(((NA // 2) * X_PITCH, LANES), F32),
                        pltpu.VMEM((NB * Y_PITCH, LANES), F32)],
        compiler_params=_cparams(("arbitrary", "arbitrary")),
        name="hyena_lat_conv_in" if conv_in else "hyena_lat",
    )(src, u, conv_w2, g_spec, af, ai, f2, f3)


def _rope_tables(l):
    f32 = np.float32
    rows = l // GRID_W
    row = np.repeat(np.arange(rows), GRID_W).astype(f32)
    col = np.tile(np.arange(GRID_W), rows).astype(f32)
    quarter = HEAD_DIM // 4
    inv = np.power(f32(ROPE_BASE), -np.arange(quarter, dtype=f32) / f32(quarter)).astype(f32)
    ang = np.concatenate([row[:, None] * inv, col[:, None] * inv], axis=-1).astype(f32)
    cos, sin = np.cos(ang), np.sin(ang)
    q = quarter
    cos_h = np.concatenate([cos[:, :q], cos[:, :q], cos[:, q:], cos[:, q:]], axis=-1)
    sin_h = np.concatenate([-sin[:, :q], sin[:, :q], -sin[:, q:], sin[:, q:]], axis=-1)
    return jnp.asarray(np.tile(cos_h, (1, 2))), jnp.asarray(np.tile(sin_h, (1, 2)))


def kernel(x_prompt, x_sample, c, cache_k, cache_v, state_ret, c_ctx, norm_w, w_mod, b_mod, w_in, hy_conv,
           hy_filt_w1, hy_filt_b1, hy_filt_freq, hy_filt_w2, hy_skip, attn_sink, ret_theta, ret_gn,
           w_branch_a, w_branch_b, w_branch_c, w_merge, b_merge, w_out, final_norm_w):
    d = D_MODEL
    bc, lc, _ = x_prompt.shape
    bl, ll, _ = x_sample.shape
    assert ll == LAT_L and bc % 2 == 0 and bl % 2 == 0
    past = cache_k.shape[2]

    cond = jnp.zeros((16, d), F32).at[:bl].set(c).at[bl].set(c_ctx)
    mod = _mod_call(cond, w_mod, b_mod)

    g_ctx = _filt_ctx_call(lc, hy_filt_w1, hy_filt_b1, hy_filt_freq, hy_filt_w2, hy_skip)
    g_lat = _filt_lat_call(hy_filt_w1, hy_filt_b1, hy_filt_freq, hy_filt_w2, hy_skip)

    cos_t, sin_t = _rope_tables(ll)
    w_in_b = w_in.astype(BF16)
    wm_b = w_merge.astype(BF16)
    wa_b = w_branch_a.astype(BF16)
    wb_b = w_branch_b.astype(BF16)
    wc_b = w_branch_c.astype(BF16)
    wo_b = w_out.astype(BF16)
    fnw = final_norm_w.reshape(1, d)
    k_ctx = cache_k.reshape(bl, DEPTH, past, W_KV)
    v_ctx = cache_v.reshape(bl, DEPTH, past, W_KV)
    hy_cols = COL_HY // LANES
    nct = W_HY // LANES

    xp, xs = x_prompt, x_sample
    ks_out, vs_out, ss_out = [], [], []
    for l in range(DEPTH):
        final = l == DEPTH - 1
        nw = norm_w[l].reshape(1, d)
        bm = b_merge[l].reshape(1, -1)
        shift, scale, gate = (mod[l, :, i * d:(i + 1) * d][:, None, :] for i in range(3))
        conv_w = hy_conv[l]
        cw = lambda g: conv_w[:, g * W_HY:(g + 1) * W_HY]

        sl = slice(bl, bl + 1)
        u = _in_call(xp, shift[sl], scale[sl], nw, w_in_b[l], cos_t, sin_t, rope=False)
        ya = _hy_ctx_call(u, conv_w, g_ctx, l)
        yb = _attn_ctx_call(u, attn_sink[l])
        yc, sfin = _ret_call(u, ret_theta[l], ret_gn[l], None)
        res = _out_call(xp, shift[sl], scale[sl], gate[sl], nw, ya, u, yb, yc, wm_b[l], bm, wa_b[l], wb_b[l],
                        wc_b[l], wo_b[l], fnw, final=final)
        xp = res[0]
        if final:
            y_prompt = res[1]
        ks_out.append(u[:, :, COL_KA:COL_KA + W_KV].reshape(bc, lc, ATT_KV_HEADS, HEAD_DIM))
        vs_out.append(u[:, :, COL_VA:COL_VA + W_KV].reshape(bc, lc, ATT_KV_HEADS, HEAD_DIM))
        ss_out.append(sfin)

        sl = slice(0, bl)
        u = _in_call(xs, shift[sl], scale[sl], nw, w_in_b[l], cos_t, sin_t, rope=True)
        z1 = _hy_lat_call(u, hy_cols, u, hy_cols + nct, jnp.stack([cw(0), cw(1)]), g_lat, l, 0, conv_in=True)
        ya = _hy_lat_call(z1, 0, u, hy_cols + 2 * nct, jnp.stack([cw(2), cw(2)]), g_lat, l, 1, conv_in=False)
        yb = _attn_lat_call(u, k_ctx, v_ctx, attn_sink[l], l)
        yc, _ = _ret_call(u, ret_theta[l], ret_gn[l], state_ret, l)
        res = _out_call(xs, shift[sl], scale[sl], gate[sl], nw, ya, u, yb, yc, wm_b[l], bm, wa_b[l], wb_b[l],
                        wc_b[l], wo_b[l], fnw, final=final)
        xs = res[0]
        if final:
            y_sample = res[1]

    new_cache_k = jnp.stack(ks_out, axis=1)
    new_cache_v = jnp.stack(vs_out, axis=1)
    new_state_ret = jnp.stack(ss_out, axis=1)
    return (y_prompt, y_sample, new_cache_k, new_cache_v, new_state_ret)
```

```python
import functools
import math

import numpy as np
import jax
import jax.numpy as jnp
from jax import lax
from jax.experimental import pallas as pl
from jax.experimental.pallas import tpu as pltpu

F32 = jnp.float32
BF16 = jnp.bfloat16
HIGHEST = lax.Precision.HIGHEST

D_MODEL = 1024
DEPTH = 4
GRID_W = 64
W_HY = 512
HY_BANDS = 8
HY_POS_FEAT = 1 + 2 * HY_BANDS
HY_FILT_HID = 64
HY_DECAY_TARGET = 1e-2
HY_FAST_PCT = 0.3
HY_SLOW_PCT = 1.5
ATT_HEADS = 8
ATT_KV_HEADS = 2
ATT_GROUP = ATT_HEADS // ATT_KV_HEADS
HEAD_DIM = 64
W_ATT = ATT_HEADS * HEAD_DIM
W_KV = ATT_KV_HEADS * HEAD_DIM
ATT_BLOCK = 128
RET_HEADS = 8
RET_DIM = 64
W_RET = RET_HEADS * RET_DIM
RET_CHUNK = 128
ROPE_BASE = 10000.0
EPS = 1e-6
NEG = -1e30

LANES = 128
MXU_ROWS = 512
VMEM_LIMIT = 56 * 1024 * 1024

IN_DIM = 5376
COL_HY = 0
COL_GH = 1536
COL_QA = 2048
COL_GA = 2560
COL_QR = 3072
COL_KR = 3584
COL_VR = 4096
COL_GR = 4608
COL_KA = 5120
COL_VA = 5248
_W_COL = {COL_HY: 0, COL_HY + 512: 512, COL_HY + 1024: 1024, COL_GH: 1536, COL_QA: 2048, COL_KA: 2560,
          COL_VA: 2688, COL_GA: 2816, COL_QR: 3328, COL_KR: 3840, COL_VR: 4352, COL_GR: 4864}

LAT_L = 4096
LAT_N = 2 * LAT_L
NA = 64
NB = 128
Y_PITCH = 136
X_PITCH = 136


def _cparams(sem):
    return pltpu.CompilerParams(dimension_semantics=sem, vmem_limit_bytes=VMEM_LIMIT)


def _split_np(a):
    a32 = np.asarray(a, np.float32)
    hi = a32.astype(BF16)
    lo = (a32 - hi.astype(np.float32)).astype(BF16)
    return jnp.asarray(hi), jnp.asarray(lo)


def _split(x):
    hi = x.astype(BF16)
    lo = (x - hi.astype(F32)).astype(BF16)
    return hi, lo


def _dot(a, b):
    return jnp.dot(a, b, preferred_element_type=F32)


def _dot3c(chi, clo, x):
    xh, xl = _split(x)
    return _dot(chi, xh) + _dot(clo, xh) + _dot(chi, xl)


def _dot1c(chi, x):
    return _dot(chi, x.astype(BF16))


def _silu(x):
    return x * jax.nn.sigmoid(x)


def _mod_kernel(c_ref, w_ref, b_ref, o_ref):
    s = _silu(c_ref[...])
    o_ref[0] = jnp.dot(s, w_ref[0], precision=HIGHEST, preferred_element_type=F32) + b_ref[0]


def _mod_call(cond, w_mod, b_mod):
    rows, d = cond.shape
    n = w_mod.shape[-1]
    tn = 1024
    return pl.pallas_call(
        _mod_kernel,
        grid=(DEPTH, n // tn),
        in_specs=[pl.BlockSpec((rows, d), lambda l, j: (0, 0)),
                  pl.BlockSpec((1, d, tn), lambda l, j: (l, 0, j)),
                  pl.BlockSpec((1, 1, tn), lambda l, j: (l, 0, j))],
        out_specs=pl.BlockSpec((1, rows, tn), lambda l, j: (l, 0, j)),
        out_shape=jax.ShapeDtypeStruct((DEPTH, rows, n), F32),
        compiler_params=_cparams(("arbitrary", "arbitrary")),
        name="adaln_mod",
    )(cond, w_mod, b_mod.reshape(DEPTH, 1, n))


def _modulated(x, nw, scale, shift):
    ms = jnp.mean(x * x, axis=-1, keepdims=True)
    h = x * lax.rsqrt(ms + EPS) * nw
    return h * (1.0 + scale) + shift


def _rope128(x, cos, sin_signed, first_half):
    up = pltpu.roll(x, LANES - 16, 1)
    dn = pltpu.roll(x, 16, 1)
    return x * cos + jnp.where(first_half, up, dn) * sin_signed


def _rows(ref):
    bt, tm, w = ref.shape
    return ref[...].reshape(bt * tm, w)


def _put(ref, c0, val):
    bt, tm, _ = ref.shape
    ref[:, :, c0:c0 + val.shape[1]] = val.reshape(bt, tm, val.shape[1])


def _in_kernel(x_ref, shift_ref, scale_ref, nw_ref, w_ref, cos_ref, sin_ref, o_ref, *, rope):
    x = _rows(x_ref)
    rows = x.shape[0]
    hb = _modulated(x, nw_ref[...], scale_ref[0], shift_ref[0]).astype(BF16)
    if rope:
        cos = cos_ref[...]
        sin = sin_ref[...]
        lane = lax.broadcasted_iota(jnp.int32, (rows, LANES), 1)
        first_half = (lane % 32) < 16

    def seg(c0, width):
        w0 = _W_COL[c0]
        return _dot(hb, w_ref[:, w0:w0 + width])

    def put_rope(c0, val, mul):
        for i in range(val.shape[1] // LANES):
            piece = val[:, i * LANES:(i + 1) * LANES]
            if rope:
                piece = _rope128(piece, cos, sin, first_half)
            if mul is not None:
                piece = piece * mul
            _put(o_ref, c0 + i * LANES, piece)

    for g in range(3):
        _put(o_ref, COL_HY + g * 512, seg(COL_HY + g * 512, 512))
    _put(o_ref, COL_GH, _silu(seg(COL_GH, 512)))
    put_rope(COL_QA, seg(COL_QA, 512), None)
    _put(o_ref, COL_GA, _silu(seg(COL_GA, 512)))
    put_rope(COL_QR, seg(COL_QR, 512), None)
    put_rope(COL_KR, seg(COL_KR, 512), RET_DIM ** -0.5)
    _put(o_ref, COL_VR, seg(COL_VR, 512))
    _put(o_ref, COL_GR, _silu(seg(COL_GR, 512)))
    put_rope(COL_KA, seg(COL_KA, 128), None)
    _put(o_ref, COL_VA, seg(COL_VA, 128))


def _token_tiling(b, l, per_batch):
    if l >= MXU_ROWS:
        return 1, MXU_ROWS
    bt = 1 if per_batch else min(b, MXU_ROWS // l)
    return bt, l


def _in_call(x, shift, scale, nw, w, cos_t, sin_t, *, rope):
    b, l, d = x.shape
    per_batch = shift.shape[0] > 1
    bt, tm = _token_tiling(b, l, per_batch)
    assert not rope or bt == 1
    mod_map = (lambda i, j: (i, 0, 0)) if per_batch else (lambda i, j: (0, 0, 0))
    return pl.pallas_call(
        functools.partial(_in_kernel, rope=rope),
        grid=(b // bt, l // tm),
        in_specs=[pl.BlockSpec((bt, tm, d), lambda i, j: (i, j, 0)),
                  pl.BlockSpec((1, 1, d), mod_map),
                  pl.BlockSpec((1, 1, d), mod_map),
                  pl.BlockSpec((1, d), lambda i, j: (0, 0)),
                  pl.BlockSpec((d, IN_DIM), lambda i, j: (0, 0), pipeline_mode=pl.Buffered(1)),
                  pl.BlockSpec((tm, LANES), lambda i, j: (j, 0)),
                  pl.BlockSpec((tm, LANES), lambda i, j: (j, 0))],
        out_specs=pl.BlockSpec((bt, tm, IN_DIM), lambda i, j: (i, j, 0)),
        out_shape=jax.ShapeDtypeStruct((b, l, IN_DIM), F32),
        compiler_params=_cparams(("parallel", "parallel")),
        name="in_proj_rope" if rope else "in_proj",
    )(x, shift, scale, nw, w, cos_t, sin_t)


def _out_kernel(x_ref, shift_ref, scale_ref, gate_ref, nw_ref, ya_ref, gh_ref, yb_ref, yc_ref, wm_ref, bm_ref,
                wa_ref, wb_ref, wc_ref, wo_ref, fnw_ref, *out_refs, final):
    x = _rows(x_ref)
    d = x.shape[1]
    hb = _modulated(x, nw_ref[...], scale_ref[0], shift_ref[0]).astype(BF16)
    branches = (_rows(ya_ref) * _rows(gh_ref), _rows(yb_ref), _rows(yc_ref))
    merged = None
    for i, (y, w_ref) in enumerate(zip(branches, (wa_ref, wb_ref, wc_ref))):
        g = jax.nn.sigmoid(_dot(hb, wm_ref[:, i * d:(i + 1) * d]) + bm_ref[:, i * d:(i + 1) * d])
        term = g * _dot(y.astype(BF16), w_ref[...])
        merged = term if merged is None else merged + term
    out = _dot(merged.astype(BF16), wo_ref[...])
    xn = x + gate_ref[0] * out
    _put(out_refs[0], 0, xn)
    if final:
        ms = jnp.mean(xn * xn, axis=-1, keepdims=True)
        _put(out_refs[1], 0, xn * lax.rsqrt(ms + EPS) * fnw_ref[...])


def _out_call(x, shift, scale, gate, nw, ya, u, yb, yc, wm, bm, wa, wb, wc, wo, fnw, *, final):
    b, l, d = x.shape
    per_batch = shift.shape[0] > 1
    bt, tm = _token_tiling(b, l, per_batch)
    mod_map = (lambda i, j: (i, 0, 0)) if per_batch else (lambda i, j: (0, 0, 0))
    tok = lambda w: pl.BlockSpec((bt, tm, w), lambda i, j: (i, j, 0))
    full = lambda a: pl.BlockSpec(a.shape, lambda i, j: (0,) * a.ndim, pipeline_mode=pl.Buffered(1))
    n_out = 2 if final else 1
    res = pl.pallas_call(
        functools.partial(_out_kernel, final=final),
        grid=(b // bt, l // tm),
        in_specs=[tok(d), pl.BlockSpec((1, 1, d), mod_map), pl.BlockSpec((1, 1, d), mod_map),
                  pl.BlockSpec((1, 1, d), mod_map), full(nw), tok(W_HY),
                  pl.BlockSpec((bt, tm, W_HY), lambda i, j: (i, j, COL_GH // W_HY)), tok(W_ATT), tok(W_RET),
                  full(wm), full(bm), full(wa), full(wb), full(wc), full(wo), full(fnw)],
        out_specs=[tok(d)] * n_out,
        out_shape=[jax.ShapeDtypeStruct((b, l, d), F32)] * n_out,
        compiler_params=_cparams(("parallel", "parallel")),
        name="merge_out_final" if final else "merge_out",
    )(x, shift, scale, gate, nw, ya, u, yb, yc, wm, bm, wa, wb, wc, wo, fnw)
    return res


_NT = (((1,), (1,)), ((), ()))


_TN = (((0,), (0,)), ((), ()))
LOG2E = 1.4426950408889634
Q_SCALE = (HEAD_DIM ** -0.5) * LOG2E


def _attn_scores_t(q, kh, kv):
    h0 = kv * ATT_GROUP
    qs = jnp.concatenate([q[:, (h0 + g) * HEAD_DIM:(h0 + g + 1) * HEAD_DIM] for g in range(ATT_GROUP)],
                         axis=0).astype(BF16)
    return lax.dot_general(kh, qs, _NT, preferred_element_type=F32)


def _attn_finish_t(sink_ref, s, vh, g_ref, o_ref, kv, row0=0):
    tk, cols = s.shape
    t = cols // ATT_GROUP
    h0 = kv * ATT_GROUP
    head = lax.broadcasted_iota(jnp.int32, (1, cols), 1) // t
    sink = jnp.full((1, cols), sink_ref[h0], F32)
    for g in range(1, ATT_GROUP):
        sink = jnp.where(head == g, sink_ref[h0 + g], sink)
    sink = sink * LOG2E
    m = jnp.maximum(jnp.max(s, axis=0, keepdims=True), sink)
    p = jnp.exp2(s - m).astype(BF16)
    v_ext = jnp.concatenate([vh, jnp.ones((tk, HEAD_DIM), BF16)], axis=1)
    o_ext = lax.dot_general(v_ext, p, _TN, preferred_element_type=F32)
    denom = o_ext[HEAD_DIM:HEAD_DIM + 1] + jnp.exp2(sink - m)
    o = o_ext[:HEAD_DIM] / denom
    for gp in range(ATT_GROUP // 2):
        pair = jnp.concatenate([o[:, (2 * gp) * t:(2 * gp + 1) * t], o[:, (2 * gp + 1) * t:(2 * gp + 2) * t]],
                               axis=0)
        c0 = (h0 + 2 * gp) * HEAD_DIM
        o_ref[0, row0:row0 + t, c0:c0 + 2 * HEAD_DIM] = pair.T * g_ref[0, row0:row0 + t, c0:c0 + 2 * HEAD_DIM]


def _attn_ctx_kernel(sink_ref, q_ref, k_ref, v_ref, g_ref, o_ref):
    q = q_ref[0] * Q_SCALE
    k = k_ref[0].astype(BF16)
    v = v_ref[0].astype(BF16)
    scores = [_attn_scores_t(q, k[:, kv * HEAD_DIM:(kv + 1) * HEAD_DIM], kv) for kv in range(ATT_KV_HEADS)]
    for kv in range(ATT_KV_HEADS):
        _attn_finish_t(sink_ref, scores[kv], v[:, kv * HEAD_DIM:(kv + 1) * HEAD_DIM], g_ref, o_ref, kv)


def _attn_ctx_call(u, sink):
    b, l, _ = u.shape
    return pl.pallas_call(
        _attn_ctx_kernel,
        grid=(b,),
        in_specs=[pl.BlockSpec(memory_space=pltpu.SMEM),
                  pl.BlockSpec((1, l, W_ATT), lambda i: (i, 0, COL_QA // W_ATT)),
                  pl.BlockSpec((1, l, W_KV), lambda i: (i, 0, COL_KA // W_KV)),
                  pl.BlockSpec((1, l, W_KV), lambda i: (i, 0, COL_VA // W_KV)),
                  pl.BlockSpec((1, l, W_ATT), lambda i: (i, 0, COL_GA // W_ATT))],
        out_specs=pl.BlockSpec((1, l, W_ATT), lambda i: (i, 0, 0)),
        out_shape=jax.ShapeDtypeStruct((b, l, W_ATT), F32),
        compiler_params=_cparams(("parallel",)),
        name="attn_ctx",
    )(sink, u, u, u, u)


def _attn_lat_kernel(sink_ref, q_ref, kp_ref, kc_ref, kn_ref, vp_ref, vc_ref, vn_ref, kx_ref, vx_ref,
                     g_ref, o_ref):
    j = pl.program_id(1)
    last = pl.num_programs(1) - 1
    b = ATT_BLOCK
    bf = lambda ref: ref[0].astype(BF16)
    kp, kc, kn, kx = bf(kp_ref), bf(kc_ref), bf(kn_ref), bf(kx_ref)
    vp, vc, vn, vx = bf(vp_ref), bf(vc_ref), bf(vn_ref), bf(vx_ref)
    keys = (jnp.concatenate([kp, kc, kx], axis=0), jnp.concatenate([kc, kn, kx], axis=0))
    vals = (jnp.concatenate([vp, vc, vx], axis=0), jnp.concatenate([vc, vn, vx], axis=0))
    cols = ATT_GROUP * b
    c = lax.broadcasted_iota(jnp.int32, (b, cols), 0)
    r = lax.broadcasted_iota(jnp.int32, (b, cols), 1) % b
    ok_prev = (c >= r, c >= r)
    ok_next = (c <= r, c <= r)
    ok_prev = (ok_prev[0] & (j > 0), ok_prev[1])
    ok_next = (ok_next[0], ok_next[1] & (j < last))

    def band(s, t):
        return jnp.concatenate([jnp.where(ok_prev[t], s[:b], NEG), s[b:2 * b],
                                jnp.where(ok_next[t], s[2 * b:3 * b], NEG), s[3 * b:]], axis=0)

    chains = [(t, kv) for t in range(2) for kv in range(ATT_KV_HEADS)]
    scores = []
    for t, kv in chains:
        q = q_ref[0, t * b:(t + 1) * b, :] * Q_SCALE
        scores.append(band(_attn_scores_t(q, keys[t][:, kv * HEAD_DIM:(kv + 1) * HEAD_DIM], kv), t))
    for (t, kv), s in zip(chains, scores):
        _attn_finish_t(sink_ref, s, vals[t][:, kv * HEAD_DIM:(kv + 1) * HEAD_DIM], g_ref, o_ref, kv, t * b)


def _attn_lat_call(u, kctx, vctx, sink, layer):
    b, l, _ = u.shape
    nb = l // ATT_BLOCK
    past = kctx.shape[2]
    kcol = COL_KA // W_KV
    vcol = COL_VA // W_KV
    prev = lambda col: pl.BlockSpec((1, ATT_BLOCK, W_KV), lambda i, j: (i, jnp.maximum(2 * j - 1, 0), col))
    cur = lambda col: pl.BlockSpec((1, 2 * ATT_BLOCK, W_KV), lambda i, j: (i, j, col))
    nxt = lambda col: pl.BlockSpec((1, ATT_BLOCK, W_KV),
                                   lambda i, j: (i, jnp.minimum(2 * j + 2, nb - 1), col))
    ctx = pl.BlockSpec((1, None, past, W_KV), lambda i, j: (i, layer, 0, 0))
    return pl.pallas_call(
        _attn_lat_kernel,
        grid=(b, nb // 2),
        in_specs=[pl.BlockSpec(memory_space=pltpu.SMEM),
                  pl.BlockSpec((1, 2 * ATT_BLOCK, W_ATT), lambda i, j: (i, j, COL_QA // W_ATT)),
                  prev(kcol), cur(kcol), nxt(kcol), prev(vcol), cur(vcol), nxt(vcol), ctx, ctx,
                  pl.BlockSpec((1, 2 * ATT_BLOCK, W_ATT), lambda i, j: (i, j, COL_GA // W_ATT))],
        out_specs=pl.BlockSpec((1, 2 * ATT_BLOCK, W_ATT), lambda i, j: (i, j, 0)),
        out_shape=jax.ShapeDtypeStruct((b, l, W_ATT), F32),
        compiler_params=_cparams(("parallel", "parallel")),
        name="attn_lat",
    )(sink, u, u, u, u, u, u, u, kctx, vctx, u)


def _log_sigmoid(x):
    return jnp.minimum(x, 0.0) - jnp.log1p(jnp.exp(-jnp.abs(x)))


_TAB_DMAT = 0
_TAB_QDEC = 2 * RET_CHUNK
_TAB_KDEC = 3 * RET_CHUNK
_TAB_CDEC = 4 * RET_CHUNK
_TAB_ROWS = 4 * RET_CHUNK + 8


def _ret_kernel(q_ref, k_ref, v_ref, g_ref, thl_ref, thb_ref, gn_ref, s0_ref, o_ref, sfin_ref, ob_ref,
                tab_ref, *, nc, cpt, npairs, has_s0):
    grp = pl.program_id(0)
    c = RET_CHUNK
    lane = lax.broadcasted_iota(jnp.int32, (1, LANES), 1)
    lo_head = lane < RET_DIM
    dd = lax.broadcasted_iota(jnp.int32, (LANES, LANES), 0)
    ee = lax.broadcasted_iota(jnp.int32, (LANES, LANES), 1)
    same_head = (dd < RET_DIM) == (ee < RET_DIM)
    lanes_of = lambda p: slice(p * LANES, (p + 1) * LANES)

    @pl.when(pl.program_id(1) == 0)
    def _():
        rowf = lax.broadcasted_iota(jnp.int32, (c, LANES), 0).astype(F32)
        ii = lax.broadcasted_iota(jnp.int32, (c, c), 0)
        jj = lax.broadcasted_iota(jnp.int32, (c, c), 1)
        for p in range(npairs):
            for d in range(2):
                lg_lane = _log_sigmoid(thl_ref[d, :, lanes_of(p)])
                dist = (ii - jj) if d == 0 else (jj - ii)
                for hh in range(2):
                    head = 2 * (grp * npairs + p) + hh
                    lg_h = _log_sigmoid(thb_ref[d, pl.ds(head, 1), :])
                    dm = jnp.where(dist >= 0, jnp.exp(lg_h * jnp.maximum(dist, 0).astype(F32)), 0.0)
                    tab_ref[p, d, _TAB_DMAT + hh * c:_TAB_DMAT + (hh + 1) * c, :] = dm
                if d == 0:
                    q_dec = jnp.exp(lg_lane * (rowf + 1.0))
                    k_dec = jnp.exp(lg_lane * (c - 1.0 - rowf))
                else:
                    q_dec = jnp.exp(lg_lane * (c - rowf))
                    k_dec = jnp.exp(lg_lane * rowf)
                tab_ref[p, d, _TAB_QDEC:_TAB_QDEC + c, :] = q_dec
                tab_ref[p, d, _TAB_KDEC:_TAB_KDEC + c, :] = k_dec
                tab_ref[p, d, _TAB_CDEC:_TAB_CDEC + 8, :] = jnp.broadcast_to(
                    jnp.exp(lg_lane * float(c)), (8, LANES))

    def first_level(p, d, r0):
        qc = q_ref[0, pl.ds(r0, c), lanes_of(p)]
        kc = k_ref[0, pl.ds(r0, c), lanes_of(p)]
        vcb = v_ref[0, pl.ds(r0, c), lanes_of(p)].astype(BF16)
        qs = jnp.concatenate([jnp.where(lo_head, qc, 0.0), jnp.where(lo_head, 0.0, qc)], axis=0)
        sc = lax.dot_general(qs.astype(BF16), kc.astype(BF16), _NT, preferred_element_type=F32)
        sc = sc * tab_ref[p, d, _TAB_DMAT:_TAB_DMAT + 2 * c, :]
        kd = kc * tab_ref[p, d, _TAB_KDEC:_TAB_KDEC + c, :]
        upd = jnp.where(same_head, _dot(kd.T.astype(BF16), vcb), 0.0)
        return qc, vcb, sc.astype(BF16), upd

    def second_level(p, d, lvl1, s):
        qc, vcb, scb, upd = lvl1
        pv = _dot(scb, vcb)
        qd = qc * tab_ref[p, d, _TAB_QDEC:_TAB_QDEC + c, :]
        o = _dot(qd.astype(BF16), s.astype(BF16)) + jnp.where(lo_head, pv[:c], pv[c:])
        return o, tab_ref[p, d, _TAB_CDEC:_TAB_CDEC + 1, :] * s + upd

    def init_state(p, d):
        if not has_s0:
            return jnp.zeros((LANES, LANES), F32)
        z = jnp.zeros((RET_DIM, RET_DIM), F32)
        return jnp.concatenate([jnp.concatenate([s0_ref[0, d, 2 * p], z], axis=1),
                                jnp.concatenate([z, s0_ref[0, d, 2 * p + 1]], axis=1)], axis=0)

    units = [(p, d) for p in range(npairs) for d in range(2)]

    def scan_body(n, states):
        def row0(d, j):
            idx = n * cpt + j
            return pl.multiple_of((idx if d == 0 else nc - 1 - idx) * c, c)

        lvl = {(p, d, j): first_level(p, d, row0(d, j)) for j in range(cpt) for p, d in units}
        states = list(states)
        for j in range(cpt):
            for ui, (p, d) in enumerate(units):
                o, states[ui] = second_level(p, d, lvl[(p, d, j)], states[ui])
                if d == 0:
                    o_ref[0, pl.ds(row0(d, j), c), lanes_of(p)] = o
                else:
                    ob_ref[pl.ds(row0(d, j), c), lanes_of(p)] = o
        return tuple(states)

    assert nc % cpt == 0
    states = lax.fori_loop(0, nc // cpt, scan_body, tuple(init_state(p, d) for p, d in units))
    for ui, (p, d) in enumerate(units):
        sfin_ref[0, d, 2 * p] = states[ui][:RET_DIM, :RET_DIM]
        sfin_ref[0, d, 2 * p + 1] = states[ui][RET_DIM:, RET_DIM:]

    head_mean = jnp.where(same_head, 1.0 / RET_DIM, 0.0).astype(BF16)

    def norm_body(n, carry):
        r0 = pl.multiple_of(n * c, c)
        for p in range(npairs):
            o = o_ref[0, pl.ds(r0, c), lanes_of(p)] + ob_ref[pl.ds(r0, c), lanes_of(p)]
            o2_hi, o2_lo = _split(o * o)
            ms = _dot(o2_hi, head_mean) + _dot(o2_lo, head_mean)
            o_ref[0, pl.ds(r0, c), lanes_of(p)] = (o * lax.rsqrt(ms + EPS) * gn_ref[:, lanes_of(p)]
                                                   * g_ref[0, pl.ds(r0, c), lanes_of(p)])
        return carry

    lax.fori_loop(0, nc, norm_body, 0, unroll=min(nc, 4))


def _ret_call(u, theta, gn, s0bd, layer=0):
    b, l, _ = u.shape
    nc = l // RET_CHUNK
    has_s0 = s0bd is not None
    cpt = 4 if nc >= 4 else nc
    npairs = max(1, 4 // cpt)
    ngrp = RET_HEADS // 2 // npairs
    w = npairs * LANES
    st_block = (2, 2 * npairs, RET_DIM, RET_DIM)
    if not has_s0:
        s0bd = jnp.zeros((1,) + st_block, F32)
        s0_spec = pl.BlockSpec((1,) + st_block, lambda g, i: (0, 0, 0, 0, 0))
    else:
        s0_spec = pl.BlockSpec((1, None) + st_block, lambda g, i: (i, layer, 0, g, 0, 0))
    th_lane = jnp.repeat(theta, RET_DIM, axis=1).reshape(2, 1, W_RET)
    th_bcast = jnp.broadcast_to(theta[:, :, None], (2, RET_HEADS, LANES))
    col = lambda c0: pl.BlockSpec((1, l, w), lambda g, i: (i, 0, c0 // w + g))
    o, sfin = pl.pallas_call(
        functools.partial(_ret_kernel, nc=nc, cpt=cpt, npairs=npairs, has_s0=has_s0),
        grid=(ngrp, b),
        in_specs=[col(COL_QR), col(COL_KR), col(COL_VR), col(COL_GR),
                  pl.BlockSpec((2, 1, w), lambda g, i: (0, 0, g)),
                  pl.BlockSpec((2, RET_HEADS, LANES), lambda g, i: (0, 0, 0)),
                  pl.BlockSpec((1, w), lambda g, i: (0, g)),
                  s0_spec],
        out_specs=[pl.BlockSpec((1, l, w), lambda g, i: (i, 0, g)),
                   pl.BlockSpec((1,) + st_block, lambda g, i: (i, 0, g, 0, 0))],
        out_shape=[jax.ShapeDtypeStruct((b, l, W_RET), F32),
                   jax.ShapeDtypeStruct((b, 2, RET_HEADS, RET_DIM, RET_DIM), F32)],
        scratch_shapes=[pltpu.VMEM((l, w), F32), pltpu.VMEM((npairs, 2, _TAB_ROWS, LANES), F32)],
        compiler_params=_cparams(("arbitrary", "arbitrary")),
        name="retention_s0" if has_s0 else "retention",
    )(u, u, u, u, th_lane, th_bcast, gn.reshape(1, W_RET), s0bd)
    return o, sfin


def _filter_positions(l):
    f32 = np.float32
    t = np.linspace(0.0, 1.0, l, dtype=f32)[:, None]
    w = (f32(2.0 * math.pi) * np.arange(l, dtype=f32)[:, None] / f32(l)).astype(f32)
    f = np.linspace(1e-4, HY_BANDS - 1, HY_BANDS, dtype=f32)[None, :]
    z = np.concatenate([t, np.cos(f * w), -np.sin(f * w)], axis=-1).astype(f32)
    z = np.pad(z, ((0, 0), (0, 32 - HY_POS_FEAT)))
    return jnp.asarray(np.concatenate([z, z[:1], z[1:][::-1]], axis=0))


def _hyena_deltas():
    max_decay = math.log(HY_DECAY_TARGET) / HY_FAST_PCT
    min_decay = math.log(HY_DECAY_TARGET) / HY_SLOW_PCT
    return jnp.asarray(np.abs(np.linspace(min_decay, max_decay, W_HY, dtype=np.float32))[None, :])


def _filter_hidden(z_ref, w1_ref, b1_ref, fr_ref):
    pre = jnp.dot(z_ref[...], w1_ref[0], precision=HIGHEST, preferred_element_type=F32) + b1_ref[0]
    return jnp.sin(fr_ref[0] * pre)


def _filter_raw(hid, w2f, w2b, tp, dl, row0, l):
    win = jnp.exp(-tp * dl)
    row = row0 + lax.broadcasted_iota(jnp.int32, win.shape, 0)
    hf = jnp.dot(hid, w2f, precision=HIGHEST, preferred_element_type=F32) * win
    hb = jnp.dot(hid, w2b, precision=HIGHEST, preferred_element_type=F32) * win
    hf = jnp.where(row < l, hf, 0.0)
    hb = jnp.where((row > l) | (row == 0), hb, 0.0)
    return hf + hb, jnp.sum(jnp.abs(hf) + jnp.abs(hb), axis=0, keepdims=True)


def _with_skip(g, skip):
    row = lax.broadcasted_iota(jnp.int32, g.shape, 0)
    return g + jnp.where(row == 0, skip, 0.0)


def _filt_ctx_kernel(z_ref, w1_ref, b1_ref, fr_ref, w2_ref, dl_ref, sk_ref, fh_ref, fl_ref, g_ref):
    hid = _filter_hidden(z_ref, w1_ref, b1_ref, fr_ref)
    tp = z_ref[:, 0:1]
    for o in range(2):
        w2f = w2_ref[0, :, (2 * o) * W_HY:(2 * o + 1) * W_HY]
        w2b = w2_ref[0, :, (2 * o + 1) * W_HY:(2 * o + 2) * W_HY]
        raw, nrm = _filter_raw(hid, w2f, w2b, tp, dl_ref[...], 0, z_ref.shape[0] // 2)
        g = _with_skip(raw / nrm, sk_ref[0, pl.ds(o, 1), :])
        g_ref[0, o] = _dot3c(fh_ref[...], fl_ref[...], g)


def _ctx_dft_tables(l):
    n = 2 * l
    k = np.arange(n)[:, None]
    t = np.arange(l)[None, :]
    ang = 2.0 * np.pi * k * t / n
    c, s = np.cos(ang), np.sin(ang)
    fwd = np.block([[c, s], [-s, c]])
    inv = np.block([[c.T, -s.T], [s.T, c.T]])
    n_all = np.arange(n)[None, :]
    angg = 2.0 * np.pi * k * n_all / n
    filt = np.concatenate([np.cos(angg), -np.sin(angg)], axis=0) / n
    return _split_np(fwd), _split_np(inv), _split_np(filt)


def _filt_ctx_call(l, w1, b1, freq, w2, skip):
    n = 2 * l
    z_ext = _filter_positions(l)
    _, _, (fh, fl) = _ctx_dft_tables(l)
    w1p = jnp.pad(w1, ((0, 0), (0, 32 - HY_POS_FEAT), (0, 0)))
    lay = lambda *shape: pl.BlockSpec((1,) + shape, lambda d: (d,) + (0,) * len(shape))
    full = lambda a: pl.BlockSpec(a.shape, lambda d: (0,) * a.ndim)
    dl = _hyena_deltas()
    return pl.pallas_call(
        _filt_ctx_kernel,
        grid=(DEPTH,),
        in_specs=[full(z_ext), lay(32, HY_FILT_HID), lay(1, HY_FILT_HID), lay(1, HY_FILT_HID),
                  lay(HY_FILT_HID, 4 * W_HY), full(dl), lay(2, W_HY), full(fh), full(fl)],
        out_specs=pl.BlockSpec((1, 2, 2 * n, W_HY), lambda d: (d, 0, 0, 0)),
        out_shape=jax.ShapeDtypeStruct((DEPTH, 2, 2 * n, W_HY), F32),
        compiler_params=_cparams(("arbitrary",)),
        name="hyena_filter_ctx",
    )(z_ext, w1p, b1.reshape(DEPTH, 1, -1), freq.reshape(DEPTH, 1, -1), w2, dl, skip, fh, fl)


def _lat_dft_tables():
    ka = np.arange(NA)[:, None]
    b = np.arange(NB)[:, None, None]
    kb = np.arange(NB)[:, None]
    bb = np.arange(NB)[None, :]
    a_half = np.arange(NA // 2)[None, :]
    a_full = np.arange(NA)[None, :]
    phi = 2.0 * np.pi * (ka * a_half / NA + b * ka / LAT_N)
    c, s = np.cos(phi), np.sin(phi)
    a_fwd = np.concatenate([c, s], axis=2)
    ct, st = np.swapaxes(c, 1, 2), np.swapaxes(s, 1, 2)
    a_inv = np.concatenate([ct, st], axis=2)
    phig = 2.0 * np.pi * (ka * a_full / NA + b * ka / LAT_N)
    a_flt = np.concatenate([np.cos(phig), -np.sin(phig)], axis=1) / LAT_N
    ang = 2.0 * np.pi * kb * bb / NB
    c2, s2 = np.cos(ang), np.sin(ang)
    f_fwd = np.block([[c2, s2], [-s2, c2]])
    f_inv = np.block([[c2, -s2], [s2, c2]])
    return (_split_np(a_fwd), _split_np(a_inv), _split_np(a_flt), _split_np(f_fwd), _split_np(f_inv))


def _stage_b_rows(ka):
    re = pl.ds(ka, NB, stride=Y_PITCH)
    im = pl.ds(NA + ka, NB, stride=Y_PITCH)
    return re, im


def _filt_lat_kernel(z_ref, w1_ref, b1_ref, fr_ref, w2f_ref, w2b_ref, dl_ref, sk_ref, ah_ref, f2h_ref,
                     g_ref, hid_ref, gt_ref, y_ref):
    step = pl.program_id(1)
    rch = 1024
    nch = LAT_N // rch
    rows_of = lambda i: pl.ds(pl.multiple_of(i * rch, rch), rch)

    @pl.when(step == 0)
    def _():
        def hid_chunk(i, carry):
            r = rows_of(i)
            pre = jnp.dot(z_ref[r, :], w1_ref[0], precision=HIGHEST, preferred_element_type=F32)
            hid_ref[r, :] = jnp.sin(fr_ref[0] * (pre + b1_ref[0]))
            return carry

        lax.fori_loop(0, nch, hid_chunk, 0)

    w2f_hl = _split(w2f_ref[0])
    w2b_hl = _split(w2b_ref[0])

    def raw_chunk(w2_hl, i, nrm):
        r = rows_of(i)
        hh, hl = _split(hid_ref[r, :])
        h = _dot(hh, w2_hl[0]) + _dot(hl, w2_hl[0]) + _dot(hh, w2_hl[1])
        h = h * jnp.exp(-z_ref[r, 0:1] * dl_ref[...])
        row = i * rch + lax.broadcasted_iota(jnp.int32, h.shape, 0)
        h = jnp.where(row == LAT_L, 0.0, h)
        gt_ref[r, :] = h
        return nrm + jnp.sum(jnp.abs(h), axis=0, keepdims=True)

    nrm = lax.fori_loop(0, nch // 2, functools.partial(raw_chunk, w2f_hl), jnp.zeros((1, LANES), F32))
    nrm = lax.fori_loop(nch // 2, nch, functools.partial(raw_chunk, w2b_hl), nrm)
    hh, hl = _split(hid_ref[0:8, :])
    hb0 = _dot(hh, w2b_hl[0]) + _dot(hl, w2b_hl[0]) + _dot(hh, w2b_hl[1])
    hb0 = hb0 * jnp.exp(-z_ref[0:8, 0:1] * dl_ref[...])
    hb0 = jnp.where(lax.broadcasted_iota(jnp.int32, hb0.shape, 0) == 0, hb0, 0.0)
    gt_ref[0:8, :] = gt_ref[0:8, :] + hb0
    nrm = nrm + jnp.sum(jnp.abs(hb0), axis=0, keepdims=True)

    def norm_chunk(i, carry):
        r = rows_of(i)
        gt_ref[r, :] = gt_ref[r, :] / nrm
        return carry

    lax.fori_loop(0, nch, norm_chunk, 0)
    order = step // (W_HY // LANES)
    gt_ref[0:8, :] = _with_skip(gt_ref[0:8, :], sk_ref[0, pl.ds(order, 1), :])

    def stage_a(b, carry):
        rows = gt_ref[pl.ds(b, NA, stride=NB), :]
        y_ref[pl.ds(pl.multiple_of(b * Y_PITCH, 8), 2 * NA), :] = _dot1c(ah_ref[b], rows)
        return carry

    lax.fori_loop(0, NB, stage_a, 0, unroll=8)

    def stage_b(j, carry):
        ka = 2 * j
        re, im = _stage_b_rows(ka)
        re1, im1 = _stage_b_rows(ka + 1)
        z = jnp.concatenate([jnp.concatenate([y_ref[re, :], y_ref[im, :]], axis=0),
                             jnp.concatenate([y_ref[re1, :], y_ref[im1, :]], axis=0)], axis=1)
        x = _dot1c(f2h_ref[...], z)
        g0 = pl.multiple_of(ka * 2 * NB, 2 * NB)
        g_ref[0, 0, pl.ds(g0, 2 * NB), :] = x[:, :LANES]
        g_ref[0, 0, pl.ds(g0 + 2 * NB, 2 * NB), :] = x[:, LANES:]
        return carry

    lax.fori_loop(0, NA // 2, stage_b, 0, unroll=4)


def _filt_lat_call(w1, b1, freq, w2, skip):
    z_ext = _filter_positions(LAT_L)
    _, _, (ah, _), (f2h, _), _ = _lat_dft_tables()
    w1p = jnp.pad(w1, ((0, 0), (0, 32 - HY_POS_FEAT), (0, 0)))
    nct = W_HY // LANES
    one = pl.Buffered(1)
    lay = lambda *shape: pl.BlockSpec((1,) + shape, lambda d, s: (d,) + (0,) * len(shape))
    full = lambda a: pl.BlockSpec(a.shape, lambda d, s: (0,) * a.ndim, pipeline_mode=one)
    dl = _hyena_deltas()
    return pl.pallas_call(
        _filt_lat_kernel,
        grid=(DEPTH, 2 * nct),
        in_specs=[full(z_ext), lay(32, HY_FILT_HID), lay(1, HY_FILT_HID), lay(1, HY_FILT_HID),
                  pl.BlockSpec((1, HY_FILT_HID, LANES), lambda d, s: (d, 0, (s // nct) * 2 * nct + s % nct)),
                  pl.BlockSpec((1, HY_FILT_HID, LANES),
                               lambda d, s: (d, 0, (s // nct) * 2 * nct + nct + s % nct)),
                  pl.BlockSpec((1, LANES), lambda d, s: (0, s % nct)),
                  pl.BlockSpec((1, 2, LANES), lambda d, s: (d, 0, s % nct)),
                  full(ah), full(f2h)],
        out_specs=pl.BlockSpec((1, 1, NA * 2 * NB, LANES), lambda d, s: (d, s, 0, 0)),
        out_shape=jax.ShapeDtypeStruct((DEPTH, 2 * nct, NA * 2 * NB, LANES), F32),
        scratch_shapes=[pltpu.VMEM((LAT_N, HY_FILT_HID), F32), pltpu.VMEM((LAT_N, LANES), F32),
                        pltpu.VMEM((NB * Y_PITCH, LANES), F32)],
        compiler_params=_cparams(("arbitrary", "arbitrary")),
        name="hyena_filter_lat",
    )(z_ext, w1p, b1.reshape(DEPTH, 1, -1), freq.reshape(DEPTH, 1, -1), w2, w2, dl, skip, ah, f2h)


def _short_conv_rows(ref, bi, r0, rows, first, last, w):
    total = ref.shape[1]
    cur = ref[bi, pl.ds(r0, rows), :]
    before = ref[bi, pl.ds(jnp.maximum(r0 - 1, 0), 1), :]
    after = ref[bi, pl.ds(jnp.minimum(r0 + rows, total - 1), 1), :]
    before = jnp.where(first, 0.0, before)
    after = jnp.where(last, 0.0, after)
    rid = lax.broadcasted_iota(jnp.int32, cur.shape, 0)
    prev = jnp.where(rid == 0, before, pltpu.roll(cur, 1, 0))
    nxt = jnp.where(rid == rows - 1, after, pltpu.roll(cur, rows - 1, 0))
    return prev * w[0:1] + cur * w[1:2] + nxt * w[2:3]


def _short_conv_interior(ref, bi, r0, rows, w):
    prev = ref[bi, pl.ds(r0 - 1, rows), :]
    cur = ref[bi, pl.ds(r0, rows), :]
    nxt = ref[bi, pl.ds(r0 + 1, rows), :]
    return prev * w[0:1] + cur * w[1:2] + nxt * w[2:3]


def _cmul(xr, xi, gr, gi):
    return xr * gr - xi * gi, xr * gi + xi * gr


def _hy_ctx_kernel(v_ref, x1_ref, x2_ref, cw_ref, g_ref, fh_ref, ih_ref, o_ref):
    l = v_ref.shape[1]
    n = 2 * l

    def sc(ref, bi, grp):
        w = cw_ref[:, grp * W_HY:(grp + 1) * W_HY]
        return _short_conv_rows(ref, bi, 0, l, True, True, w)

    def conv(zr, zi, order):
        x = _dot1c(fh_ref[...], jnp.concatenate([zr, zi], axis=0))
        pr, pi = _cmul(x[:n], x[n:], g_ref[order, :n], g_ref[order, n:])
        y = _dot1c(ih_ref[...], jnp.concatenate([pr, pi], axis=0))
        return y[:l], y[l:]

    yr, yi = conv(sc(v_ref, 0, 0), sc(v_ref, 1, 0), 0)
    yr, yi = conv(sc(x1_ref, 0, 1) * yr, sc(x1_ref, 1, 1) * yi, 1)
    o_ref[0] = sc(x2_ref, 0, 2) * yr
    o_ref[1] = sc(x2_ref, 1, 2) * yi


def _hy_ctx_call(u, conv_w, g_spec, layer):
    b, l, _ = u.shape
    (fh, _), (ih, _), _ = _ctx_dft_tables(l)
    grp = lambda g: pl.BlockSpec((2, l, W_HY), lambda i: (i, 0, g))
    full = lambda a: pl.BlockSpec(a.shape, lambda i: (0,) * a.ndim)
    return pl.pallas_call(
        _hy_ctx_kernel,
        grid=(b // 2,),
        in_specs=[grp(0), grp(1), grp(2), full(conv_w),
                  pl.BlockSpec((None,) + g_spec.shape[1:], lambda i: (layer, 0, 0, 0)), full(fh), full(ih)],
        out_specs=pl.BlockSpec((2, l, W_HY), lambda i: (i, 0, 0)),
        out_shape=jax.ShapeDtypeStruct((b, l, W_HY), F32),
        compiler_params=_cparams(("parallel",)),
        name="hyena_ctx",
    )(u, u, u, conv_w, g_spec, fh, ih)


def _hy_lat_kernel(z_ref, m_ref, cw_ref, g_ref, af_ref, ai_ref, f2_ref, f3_ref, o_ref, xr_scr, xi_scr,
                   y_scr, *, conv_in):
    x_scr = (xr_scr, xi_scr)
    na_half = NA // 2
    w_in = cw_ref[0] if conv_in else None
    w_mul = cw_ref[1]

    def conv_slab(ref, bi, a, w):
        if isinstance(a, int):
            return _short_conv_rows(ref, bi, a * NB, NB, a == 0, a == na_half - 1, w)
        return _short_conv_interior(ref, bi, pl.multiple_of(a * NB, NB), NB, w)

    def edges_then_interior(body):
        body(0, 0)
        body(na_half - 1, 0)
        lax.fori_loop(1, na_half - 1, body, 0, unroll=2)

    def load_in(a, carry):
        for bi in range(2):
            if conv_in:
                val = conv_slab(z_ref, bi, a, w_in)
            else:
                val = z_ref[bi, pl.ds(pl.multiple_of(a * NB, NB), NB), :]
            x_scr[bi][pl.ds(pl.multiple_of(a * X_PITCH, 8), NB), :] = val
        return carry

    edges_then_interior(load_in)

    def stage_a(b, carry):
        zr = xr_scr[pl.ds(b, na_half, stride=X_PITCH), :]
        zi = xi_scr[pl.ds(b, na_half, stride=X_PITCH), :]
        rhs = jnp.concatenate([jnp.concatenate([zr, zi], axis=0), jnp.concatenate([zi, -zr], axis=0)],
                              axis=1)
        y = _dot1c(af_ref[b], rhs)
        r0 = pl.multiple_of(b * Y_PITCH, 8)
        y_scr[pl.ds(r0, NA), :] = y[:, :LANES]
        y_scr[pl.ds(r0 + NA, NA), :] = y[:, LANES:]
        return carry

    lax.fori_loop(0, NB, stage_a, 0, unroll=32)

    def spectrum_product(j):
        ka = 2 * j
        re, im = _stage_b_rows(ka)
        re1, im1 = _stage_b_rows(ka + 1)
        z = jnp.concatenate([jnp.concatenate([y_scr[re, :], y_scr[im, :]], axis=0),
                             jnp.concatenate([y_scr[re1, :], y_scr[im1, :]], axis=0)], axis=1)
        x = _dot1c(f2_ref[...], z)
        g0 = pl.multiple_of(ka * 2 * NB, 2 * NB)
        gr = jnp.concatenate([g_ref[pl.ds(g0, NB), :], g_ref[pl.ds(g0 + 2 * NB, NB), :]], axis=1)
        gi = jnp.concatenate([g_ref[pl.ds(g0 + NB, NB), :], g_ref[pl.ds(g0 + 3 * NB, NB), :]], axis=1)
        pr, pi = _cmul(x[:NB], x[NB:], gr, gi)
        return jnp.concatenate([pr, pi], axis=0).astype(BF16)

    def inverse_b(j, prod):
        ka = 2 * j
        re, im = _stage_b_rows(ka)
        re1, im1 = _stage_b_rows(ka + 1)
        u = _dot(f3_ref[...], prod)
        y_scr[re, :] = u[:NB, :LANES]
        y_scr[im, :] = u[NB:, :LANES]
        y_scr[re1, :] = u[:NB, LANES:]
        y_scr[im1, :] = u[NB:, LANES:]

    def stage_b(j, prod):
        nxt = spectrum_product(j + 1)
        inverse_b(j, prod)
        return nxt

    last = lax.fori_loop(0, NA // 2 - 1, stage_b, spectrum_product(0), unroll=4)
    inverse_b(NA // 2 - 1, last)

    def stage_c(b, carry):
        r0 = pl.multiple_of(b * Y_PITCH, 8)
        ur = y_scr[pl.ds(r0, NA), :]
        ui = y_scr[pl.ds(r0 + NA, NA), :]
        rhs = jnp.concatenate([jnp.concatenate([ur, -ui], axis=0), jnp.concatenate([ui, ur], axis=0)],
                              axis=1)
        y = _dot1c(ai_ref[b], rhs)
        xr_scr[pl.ds(b, na_half, stride=X_PITCH), :] = y[:, :LANES]
        xi_scr[pl.ds(b, na_half, stride=X_PITCH), :] = y[:, LANES:]
        return carry

    lax.fori_loop(0, NB, stage_c, 0, unroll=32)

    def store_out(a, carry):
        for bi in range(2):
            mul = conv_slab(m_ref, bi, a, w_mul)
            o_ref[bi, pl.ds(pl.multiple_of(a * NB, NB), NB), :] = (
                x_scr[bi][pl.ds(pl.multiple_of(a * X_PITCH, 8), NB), :] * mul)
        return carry

    edges_then_interior(store_out)


def _hy_lat_call(src, src_col, u, mul_col, conv_w2, g_spec, layer, order, *, conv_in):
    b, l, _ = u.shape
    nct = W_HY // LANES
    (af, _), (ai, _), _, (f2, _), (f3, _) = _lat_dft_tables()
    one = pl.Buffered(1)
    blk = lambda col: pl.BlockSpec((2, l, LANES), lambda c, p: (p, 0, col + c))
    const = lambda a: pl.BlockSpec(a.shape, lambda c, p: (0,) * a.ndim, pipeline_mode=one)
    return pl.pallas_call(
        functools.partial(_hy_lat_kernel, conv_in=conv_in),
        grid=(nct, b // 2),
        in_specs=[blk(src_col), blk(mul_col),
                  pl.BlockSpec((2, 3, LANES), lambda c, p: (0, 0, c)),
                  pl.BlockSpec((None, None, NA * 2 * NB, LANES), lambda c, p: (layer, order * nct + c, 0, 0),
                               pipeline_mode=one),
                  const(af), const(ai), const(f2), const(f3)],
        out_specs=pl.BlockSpec((2, l, LANES), lambda c, p: (p, 0, c)),
        out_shape=jax.ShapeDtypeStruct((b, l, W_HY), F32),
        scratch_shapes=[pltpu.VMEM(((NA // 2) * X_PITCH, LANES), F32),
                        pltpu.VMEM(((NA // 2) * X_PITCH, LANES), F32),
                        pltpu.VMEM((NB * Y_PITCH, LANES), F32)],
        compiler_params=_cparams(("arbitrary", "arbitrary")),
        name="hyena_lat_conv_in" if conv_in else "hyena_lat",
    )(src, u, conv_w2, g_spec, af, ai, f2, f3)


def _rope_tables(l):
    f32 = np.float32
    rows = l // GRID_W
    row = np.repeat(np.arange(rows), GRID_W).astype(f32)
    col = np.tile(np.arange(GRID_W), rows).astype(f32)
    quarter = HEAD_DIM // 4
    inv = np.power(f32(ROPE_BASE), -np.arange(quarter, dtype=f32) / f32(quarter)).astype(f32)
    ang = np.concatenate([row[:, None] * inv, col[:, None] * inv], axis=-1).astype(f32)
    cos, sin = np.cos(ang), np.sin(ang)
    q = quarter
    cos_h = np.concatenate([cos[:, :q], cos[:, :q], cos[:, q:], cos[:, q:]], axis=-1)
    sin_h = np.concatenate([-sin[:, :q], sin[:, :q], -sin[:, q:], sin[:, q:]], axis=-1)
    return jnp.asarray(np.tile(cos_h, (1, 2))), jnp.asarray(np.tile(sin_h, (1, 2)))


def kernel(x_prompt, x_sample, c, cache_k, cache_v, state_ret, c_ctx, norm_w, w_mod, b_mod, w_in, hy_conv,
           hy_filt_w1, hy_filt_b1, hy_filt_freq, hy_filt_w2, hy_skip, attn_sink, ret_theta, ret_gn,
           w_branch_a, w_branch_b, w_branch_c, w_merge, b_merge, w_out, final_norm_w):
    d = D_MODEL
    bc, lc, _ = x_prompt.shape
    bl, ll, _ = x_sample.shape
    assert ll == LAT_L and bc % 2 == 0 and bl % 2 == 0
    past = cache_k.shape[2]

    cond = jnp.zeros((16, d), F32).at[:bl].set(c).at[bl].set(c_ctx)
    mod = _mod_call(cond, w_mod, b_mod)

    g_ctx = _filt_ctx_call(lc, hy_filt_w1, hy_filt_b1, hy_filt_freq, hy_filt_w2, hy_skip)
    g_lat = _filt_lat_call(hy_filt_w1, hy_filt_b1, hy_filt_freq, hy_filt_w2, hy_skip)

    cos_t, sin_t = _rope_tables(ll)
    w_in_b = w_in.astype(BF16)
    wm_b = w_merge.astype(BF16)
    wa_b = w_branch_a.astype(BF16)
    wb_b = w_branch_b.astype(BF16)
    wc_b = w_branch_c.astype(BF16)
    wo_b = w_out.astype(BF16)
    fnw = final_norm_w.reshape(1, d)
    k_ctx = cache_k.reshape(bl, DEPTH, past, W_KV)
    v_ctx = cache_v.reshape(bl, DEPTH, past, W_KV)
    hy_cols = COL_HY // LANES
    nct = W_HY // LANES

    xp, xs = x_prompt, x_sample
    ks_out, vs_out, ss_out = [], [], []
    for l in range(DEPTH):
        final = l == DEPTH - 1
        nw = norm_w[l].reshape(1, d)
        bm = b_merge[l].reshape(1, -1)
        shift, scale, gate = (mod[l, :, i * d:(i + 1) * d][:, None, :] for i in range(3))
        conv_w = hy_conv[l]
        cw = lambda g: conv_w[:, g * W_HY:(g + 1) * W_HY]

        sl = slice(bl, bl + 1)
        u = _in_call(xp, shift[sl], scale[sl], nw, w_in_b[l], cos_t, sin_t, rope=False)
        ya = _hy_ctx_call(u, conv_w, g_ctx, l)
        yb = _attn_ctx_call(u, attn_sink[l])
        yc, sfin = _ret_call(u, ret_theta[l], ret_gn[l], None)
        res = _out_call(xp, shift[sl], scale[sl], gate[sl], nw, ya, u, yb, yc, wm_b[l], bm, wa_b[l], wb_b[l],
                        wc_b[l], wo_b[l], fnw, final=final)
        xp = res[0]
        if final:
            y_prompt = res[1]
        ks_out.append(u[:, :, COL_KA:COL_KA + W_KV].reshape(bc, lc, ATT_KV_HEADS, HEAD_DIM))
        vs_out.append(u[:, :, COL_VA:COL_VA + W_KV].reshape(bc, lc, ATT_KV_HEADS, HEAD_DIM))
        ss_out.append(sfin)

        sl = slice(0, bl)
        u = _in_call(xs, shift[sl], scale[sl], nw, w_in_b[l], cos_t, sin_t, rope=True)
        z1 = _hy_lat_call(u, hy_cols, u, hy_cols + nct, jnp.stack([cw(0), cw(1)]), g_lat, l, 0, conv_in=True)
        ya = _hy_lat_call(z1, 0, u, hy_cols + 2 * nct, jnp.stack([cw(2), cw(2)]), g_lat, l, 1, conv_in=False)
        yb = _attn_lat_call(u, k_ctx, v_ctx, attn_sink[l], l)
        yc, _ = _ret_call(u, ret_theta[l], ret_gn[l], state_ret, l)
        res = _out_call(xs, shift[sl], scale[sl], gate[sl], nw, ya, u, yb, yc, wm_b[l], bm, wa_b[l], wb_b[l],
                        wc_b[l], wo_b[l], fnw, final=final)
        xs = res[0]
        if final:
            y_sample = res[1]

    new_cache_k = jnp.stack(ks_out, axis=1)
    new_cache_v = jnp.stack(vs_out, axis=1)
    new_state_ret = jnp.stack(ss_out, axis=1)
    return (y_prompt, y_sample, new_cache_k, new_cache_v, new_state_ret)
```

```python
import functools
import math

import numpy as np
import jax
import jax.numpy as jnp
from jax import lax
from jax.experimental import pallas as pl
from jax.experimental.pallas import tpu as pltpu

F32 = jnp.float32
BF16 = jnp.bfloat16
HIGHEST = lax.Precision.HIGHEST

D_MODEL = 1024
DEPTH = 4
GRID_W = 64
W_HY = 512
HY_BANDS = 8
HY_POS_FEAT = 1 + 2 * HY_BANDS
HY_FILT_HID = 64
HY_DECAY_TARGET = 1e-2
HY_FAST_PCT = 0.3
HY_SLOW_PCT = 1.5
ATT_HEADS = 8
ATT_KV_HEADS = 2
ATT_GROUP = ATT_HEADS // ATT_KV_HEADS
HEAD_DIM = 64
W_ATT = ATT_HEADS * HEAD_DIM
W_KV = ATT_KV_HEADS * HEAD_DIM
ATT_BLOCK = 128
ATT_QB = 4
RET_HEADS = 8
RET_DIM = 64
W_RET = RET_HEADS * RET_DIM
RET_CHUNK = 128
ROPE_BASE = 10000.0
EPS = 1e-6
NEG = -1e30

LANES = 128
MXU_ROWS = 512
VMEM_LIMIT = 56 * 1024 * 1024

IN_DIM = 5376
COL_HY = 0
COL_GH = 1536
COL_QA = 2048
COL_GA = 2560
COL_QR = 3072
COL_KR = 3584
COL_VR = 4096
COL_GR = 4608
COL_KA = 5120
COL_VA = 5248
_W_COL = {COL_HY: 0, COL_HY + 512: 512, COL_HY + 1024: 1024, COL_GH: 1536, COL_QA: 2048, COL_KA: 2560,
          COL_VA: 2688, COL_GA: 2816, COL_QR: 3328, COL_KR: 3840, COL_VR: 4352, COL_GR: 4864}

LAT_L = 4096
LAT_N = 2 * LAT_L
NA = 64
NB = 128
Y_PITCH = 136
X_PITCH = 136


def _cparams(sem):
    return pltpu.CompilerParams(dimension_semantics=sem, vmem_limit_bytes=VMEM_LIMIT)


def _split_np(a):
    a32 = np.asarray(a, np.float32)
    hi = a32.astype(BF16)
    lo = (a32 - hi.astype(np.float32)).astype(BF16)
    return jnp.asarray(hi), jnp.asarray(lo)


def _split(x):
    hi = x.astype(BF16)
    lo = (x - hi.astype(F32)).astype(BF16)
    return hi, lo


def _dot(a, b):
    return jnp.dot(a, b, preferred_element_type=F32)


def _dot3c(chi, clo, x):
    xh, xl = _split(x)
    return _dot(chi, xh) + _dot(clo, xh) + _dot(chi, xl)


def _dot1c(chi, x):
    return _dot(chi, x.astype(BF16))


def _silu(x):
    return x * jax.nn.sigmoid(x)


def _mod_kernel(c_ref, w_ref, b_ref, o_ref):
    s = _silu(c_ref[...])
    o_ref[0] = jnp.dot(s, w_ref[0], precision=HIGHEST, preferred_element_type=F32) + b_ref[0]


def _mod_call(cond, w_mod, b_mod):
    rows, d = cond.shape
    n = w_mod.shape[-1]
    tn = 1024
    return pl.pallas_call(
        _mod_kernel,
        grid=(DEPTH, n // tn),
        in_specs=[pl.BlockSpec((rows, d), lambda l, j: (0, 0)),
                  pl.BlockSpec((1, d, tn), lambda l, j: (l, 0, j)),
                  pl.BlockSpec((1, 1, tn), lambda l, j: (l, 0, j))],
        out_specs=pl.BlockSpec((1, rows, tn), lambda l, j: (l, 0, j)),
        out_shape=jax.ShapeDtypeStruct((DEPTH, rows, n), F32),
        compiler_params=_cparams(("arbitrary", "arbitrary")),
        name="adaln_mod",
    )(cond, w_mod, b_mod.reshape(DEPTH, 1, n))


def _modulated(x, nw, scale, shift):
    ms = jnp.mean(x * x, axis=-1, keepdims=True)
    h = x * lax.rsqrt(ms + EPS) * nw
    return h * (1.0 + scale) + shift


def _rope128(x, cos, sin_signed, first_half):
    up = pltpu.roll(x, LANES - 16, 1)
    dn = pltpu.roll(x, 16, 1)
    return x * cos + jnp.where(first_half, up, dn) * sin_signed


def _rows(ref):
    bt, tm, w = ref.shape
    return ref[...].reshape(bt * tm, w)


def _put(ref, c0, val):
    bt, tm, _ = ref.shape
    ref[:, :, c0:c0 + val.shape[1]] = val.reshape(bt, tm, val.shape[1])


def _in_kernel(x_ref, shift_ref, scale_ref, nw_ref, w_ref, cos_ref, sin_ref, o_ref, *, rope):
    x = _rows(x_ref)
    rows = x.shape[0]
    hb = _modulated(x, nw_ref[...], scale_ref[0], shift_ref[0]).astype(BF16)
    if rope:
        cos = cos_ref[...]
        sin = sin_ref[...]
        lane = lax.broadcasted_iota(jnp.int32, (rows, LANES), 1)
        first_half = (lane % 32) < 16

    def seg(c0, width):
        w0 = _W_COL[c0]
        return _dot(hb, w_ref[:, w0:w0 + width])

    def put_rope(c0, val, mul):
        for i in range(val.shape[1] // LANES):
            piece = val[:, i * LANES:(i + 1) * LANES]
            if rope:
                piece = _rope128(piece, cos, sin, first_half)
            if mul is not None:
                piece = piece * mul
            _put(o_ref, c0 + i * LANES, piece)

    for g in range(3):
        _put(o_ref, COL_HY + g * 512, seg(COL_HY + g * 512, 512))
    _put(o_ref, COL_GH, _silu(seg(COL_GH, 512)))
    put_rope(COL_QA, seg(COL_QA, 512), None)
    _put(o_ref, COL_GA, _silu(seg(COL_GA, 512)))
    put_rope(COL_QR, seg(COL_QR, 512), None)
    put_rope(COL_KR, seg(COL_KR, 512), RET_DIM ** -0.5)
    _put(o_ref, COL_VR, seg(COL_VR, 512))
    _put(o_ref, COL_GR, _silu(seg(COL_GR, 512)))
    put_rope(COL_KA, seg(COL_KA, 128), None)
    _put(o_ref, COL_VA, seg(COL_VA, 128))


def _token_tiling(b, l, per_batch):
    if l >= MXU_ROWS:
        return 1, MXU_ROWS
    bt = 1 if per_batch else min(b, MXU_ROWS // l)
    return bt, l


def _in_call(x, shift, scale, nw, w, cos_t, sin_t, *, rope):
    b, l, d = x.shape
    per_batch = shift.shape[0] > 1
    bt, tm = _token_tiling(b, l, per_batch)
    assert not rope or bt == 1
    mod_map = (lambda i, j: (i, 0, 0)) if per_batch else (lambda i, j: (0, 0, 0))
    return pl.pallas_call(
        functools.partial(_in_kernel, rope=rope),
        grid=(b // bt, l // tm),
        in_specs=[pl.BlockSpec((bt, tm, d), lambda i, j: (i, j, 0)),
                  pl.BlockSpec((1, 1, d), mod_map),
                  pl.BlockSpec((1, 1, d), mod_map),
                  pl.BlockSpec((1, d), lambda i, j: (0, 0)),
                  pl.BlockSpec((d, IN_DIM), lambda i, j: (0, 0), pipeline_mode=pl.Buffered(1)),
                  pl.BlockSpec((tm, LANES), lambda i, j: (j, 0)),
                  pl.BlockSpec((tm, LANES), lambda i, j: (j, 0))],
        out_specs=pl.BlockSpec((bt, tm, IN_DIM), lambda i, j: (i, j, 0)),
        out_shape=jax.ShapeDtypeStruct((b, l, IN_DIM), F32),
        compiler_params=_cparams(("parallel", "parallel")),
        name="in_proj_rope" if rope else "in_proj",
    )(x, shift, scale, nw, w, cos_t, sin_t)


def _out_kernel(x_ref, shift_ref, scale_ref, gate_ref, nw_ref, ya_ref, gh_ref, yb_ref, yc_ref, wm_ref, bm_ref,
                wa_ref, wb_ref, wc_ref, wo_ref, fnw_ref, *out_refs, final):
    x = _rows(x_ref)
    d = x.shape[1]
    hb = _modulated(x, nw_ref[...], scale_ref[0], shift_ref[0]).astype(BF16)
    branches = (_rows(ya_ref) * _rows(gh_ref), _rows(yb_ref), _rows(yc_ref))
    merged = None
    for i, (y, w_ref) in enumerate(zip(branches, (wa_ref, wb_ref, wc_ref))):
        g = jax.nn.sigmoid(_dot(hb, wm_ref[:, i * d:(i + 1) * d]) + bm_ref[:, i * d:(i + 1) * d])
        term = g * _dot(y.astype(BF16), w_ref[...])
        merged = term if merged is None else merged + term
    out = _dot(merged.astype(BF16), wo_ref[...])
    xn = x + gate_ref[0] * out
    _put(out_refs[0], 0, xn)
    if final:
        ms = jnp.mean(xn * xn, axis=-1, keepdims=True)
        _put(out_refs[1], 0, xn * lax.rsqrt(ms + EPS) * fnw_ref[...])


def _out_call(x, shift, scale, gate, nw, ya, u, yb, yc, wm, bm, wa, wb, wc, wo, fnw, *, final):
    b, l, d = x.shape
    per_batch = shift.shape[0] > 1
    bt, tm = _token_tiling(b, l, per_batch)
    mod_map = (lambda i, j: (i, 0, 0)) if per_batch else (lambda i, j: (0, 0, 0))
    tok = lambda w: pl.BlockSpec((bt, tm, w), lambda i, j: (i, j, 0))
    full = lambda a: pl.BlockSpec(a.shape, lambda i, j: (0,) * a.ndim, pipeline_mode=pl.Buffered(1))
    n_out = 2 if final else 1
    res = pl.pallas_call(
        functools.partial(_out_kernel, final=final),
        grid=(b // bt, l // tm),
        in_specs=[tok(d), pl.BlockSpec((1, 1, d), mod_map), pl.BlockSpec((1, 1, d), mod_map),
                  pl.BlockSpec((1, 1, d), mod_map), full(nw), tok(W_HY),
                  pl.BlockSpec((bt, tm, W_HY), lambda i, j: (i, j, COL_GH // W_HY)), tok(W_ATT), tok(W_RET),
                  full(wm), full(bm), full(wa), full(wb), full(wc), full(wo), full(fnw)],
        out_specs=[tok(d)] * n_out,
        out_shape=[jax.ShapeDtypeStruct((b, l, d), F32)] * n_out,
        compiler_params=_cparams(("parallel", "parallel")),
        name="merge_out_final" if final else "merge_out",
    )(x, shift, scale, gate, nw, ya, u, yb, yc, wm, bm, wa, wb, wc, wo, fnw)
    return res


_NT = (((1,), (1,)), ((), ()))


_TN = (((0,), (0,)), ((), ()))
LOG2E = 1.4426950408889634
Q_SCALE = (HEAD_DIM ** -0.5) * LOG2E


def _attn_scores_t(q, kh, kv):
    h0 = kv * ATT_GROUP
    qs = jnp.concatenate([q[:, (h0 + g) * HEAD_DIM:(h0 + g + 1) * HEAD_DIM] for g in range(ATT_GROUP)],
                         axis=0).astype(BF16)
    return lax.dot_general(kh, qs, _NT, preferred_element_type=F32)


def _attn_finish_t(sink_ref, s, vh, g_ref, o_ref, kv, row0=0):
    tk, cols = s.shape
    t = cols // ATT_GROUP
    h0 = kv * ATT_GROUP
    head = lax.broadcasted_iota(jnp.int32, (1, cols), 1) // t
    sink = jnp.full((1, cols), sink_ref[h0], F32)
    for g in range(1, ATT_GROUP):
        sink = jnp.where(head == g, sink_ref[h0 + g], sink)
    sink = sink * LOG2E
    m = jnp.maximum(jnp.max(s, axis=0, keepdims=True), sink)
    p = jnp.exp2(s - m).astype(BF16)
    v_ext = jnp.concatenate([vh, jnp.ones((tk, HEAD_DIM), BF16)], axis=1)
    o_ext = lax.dot_general(v_ext, p, _TN, preferred_element_type=F32)
    denom = o_ext[HEAD_DIM:HEAD_DIM + 1] + jnp.exp2(sink - m)
    o = o_ext[:HEAD_DIM] / denom
    for gp in range(ATT_GROUP // 2):
        pair = jnp.concatenate([o[:, (2 * gp) * t:(2 * gp + 1) * t], o[:, (2 * gp + 1) * t:(2 * gp + 2) * t]],
                               axis=0)
        c0 = (h0 + 2 * gp) * HEAD_DIM
        o_ref[0, row0:row0 + t, c0:c0 + 2 * HEAD_DIM] = pair.T * g_ref[0, row0:row0 + t, c0:c0 + 2 * HEAD_DIM]


def _attn_ctx_kernel(sink_ref, q_ref, k_ref, v_ref, g_ref, o_ref):
    q = q_ref[0] * Q_SCALE
    k = k_ref[0].astype(BF16)
    v = v_ref[0].astype(BF16)
    scores = [_attn_scores_t(q, k[:, kv * HEAD_DIM:(kv + 1) * HEAD_DIM], kv) for kv in range(ATT_KV_HEADS)]
    for kv in range(ATT_KV_HEADS):
        _attn_finish_t(sink_ref, scores[kv], v[:, kv * HEAD_DIM:(kv + 1) * HEAD_DIM], g_ref, o_ref, kv)


def _attn_ctx_call(u, sink):
    b, l, _ = u.shape
    return pl.pallas_call(
        _attn_ctx_kernel,
        grid=(b,),
        in_specs=[pl.BlockSpec(memory_space=pltpu.SMEM),
                  pl.BlockSpec((1, l, W_ATT), lambda i: (i, 0, COL_QA // W_ATT)),
                  pl.BlockSpec((1, l, W_KV), lambda i: (i, 0, COL_KA // W_KV)),
                  pl.BlockSpec((1, l, W_KV), lambda i: (i, 0, COL_VA // W_KV)),
                  pl.BlockSpec((1, l, W_ATT), lambda i: (i, 0, COL_GA // W_ATT))],
        out_specs=pl.BlockSpec((1, l, W_ATT), lambda i: (i, 0, 0)),
        out_shape=jax.ShapeDtypeStruct((b, l, W_ATT), F32),
        compiler_params=_cparams(("parallel",)),
        name="attn_ctx",
    )(sink, u, u, u, u)


def _attn_lat_kernel(sink_ref, q_ref, kp_ref, kc_ref, kn_ref, vp_ref, vc_ref, vn_ref, kx_ref, vx_ref,
                     g_ref, o_ref):
    j = pl.program_id(1)
    last = pl.num_programs(1) - 1
    b = ATT_BLOCK
    nq = ATT_QB
    bf = lambda ref: ref[0].astype(BF16)
    kx, vx = bf(kx_ref), bf(vx_ref)
    kblk = [bf(kp_ref)] + [kc_ref[0, t * b:(t + 1) * b, :].astype(BF16) for t in range(nq)] + [bf(kn_ref)]
    vblk = [bf(vp_ref)] + [vc_ref[0, t * b:(t + 1) * b, :].astype(BF16) for t in range(nq)] + [bf(vn_ref)]
    keys = [jnp.concatenate(kblk[t:t + 3] + [kx], axis=0) for t in range(nq)]
    vals = [jnp.concatenate(vblk[t:t + 3] + [vx], axis=0) for t in range(nq)]
    cols = ATT_GROUP * b
    c = lax.broadcasted_iota(jnp.int32, (b, cols), 0)
    r = lax.broadcasted_iota(jnp.int32, (b, cols), 1) % b
    ok_prev = [(c >= r) & (j > 0)] + [c >= r] * (nq - 1)
    ok_next = [c <= r] * (nq - 1) + [(c <= r) & (j < last)]

    def band(s, t):
        return jnp.concatenate([jnp.where(ok_prev[t], s[:b], NEG), s[b:2 * b],
                                jnp.where(ok_next[t], s[2 * b:3 * b], NEG), s[3 * b:]], axis=0)

    chains = [(t, kv) for t in range(nq) for kv in range(ATT_KV_HEADS)]
    scores = []
    for t, kv in chains:
        q = q_ref[0, t * b:(t + 1) * b, :] * Q_SCALE
        scores.append(band(_attn_scores_t(q, keys[t][:, kv * HEAD_DIM:(kv + 1) * HEAD_DIM], kv), t))
    for (t, kv), s in zip(chains, scores):
        _attn_finish_t(sink_ref, s, vals[t][:, kv * HEAD_DIM:(kv + 1) * HEAD_DIM], g_ref, o_ref, kv, t * b)


def _attn_lat_call(u, kctx, vctx, sink, layer):
    b, l, _ = u.shape
    nb = l // ATT_BLOCK
    past = kctx.shape[2]
    kcol = COL_KA // W_KV
    vcol = COL_VA // W_KV
    nq = ATT_QB
    prev = lambda col: pl.BlockSpec((1, ATT_BLOCK, W_KV), lambda i, j: (i, jnp.maximum(nq * j - 1, 0), col))
    cur = lambda col: pl.BlockSpec((1, nq * ATT_BLOCK, W_KV), lambda i, j: (i, j, col))
    nxt = lambda col: pl.BlockSpec((1, ATT_BLOCK, W_KV),
                                   lambda i, j: (i, jnp.minimum(nq * j + nq, nb - 1), col))
    ctx = pl.BlockSpec((1, None, past, W_KV), lambda i, j: (i, layer, 0, 0))
    return pl.pallas_call(
        _attn_lat_kernel,
        grid=(b, nb // nq),
        in_specs=[pl.BlockSpec(memory_space=pltpu.SMEM),
                  pl.BlockSpec((1, nq * ATT_BLOCK, W_ATT), lambda i, j: (i, j, COL_QA // W_ATT)),
                  prev(kcol), cur(kcol), nxt(kcol), prev(vcol), cur(vcol), nxt(vcol), ctx, ctx,
                  pl.BlockSpec((1, nq * ATT_BLOCK, W_ATT), lambda i, j: (i, j, COL_GA // W_ATT))],
        out_specs=pl.BlockSpec((1, nq * ATT_BLOCK, W_ATT), lambda i, j: (i, j, 0)),
        out_shape=jax.ShapeDtypeStruct((b, l, W_ATT), F32),
        compiler_params=_cparams(("parallel", "parallel")),
        name="attn_lat",
    )(sink, u, u, u, u, u, u, u, kctx, vctx, u)


def _log_sigmoid(x):
    return jnp.minimum(x, 0.0) - jnp.log1p(jnp.exp(-jnp.abs(x)))


_TAB_DMAT = 0
_TAB_QDEC = 2 * RET_CHUNK
_TAB_KDEC = 3 * RET_CHUNK
_TAB_CDEC = 4 * RET_CHUNK
_TAB_ROWS = 4 * RET_CHUNK + 8


def _ret_kernel(q_ref, k_ref, v_ref, g_ref, thl_ref, thb_ref, gn_ref, s0_ref, o_ref, sfin_ref, ob_ref,
                tab_ref, *, nc, cpt, npairs, has_s0):
    grp = pl.program_id(0)
    c = RET_CHUNK
    lane = lax.broadcasted_iota(jnp.int32, (1, LANES), 1)
    lo_head = lane < RET_DIM
    dd = lax.broadcasted_iota(jnp.int32, (LANES, LANES), 0)
    ee = lax.broadcasted_iota(jnp.int32, (LANES, LANES), 1)
    same_head = (dd < RET_DIM) == (ee < RET_DIM)
    lanes_of = lambda p: slice(p * LANES, (p + 1) * LANES)

    @pl.when(pl.program_id(1) == 0)
    def _():
        rowf = lax.broadcasted_iota(jnp.int32, (c, LANES), 0).astype(F32)
        ii = lax.broadcasted_iota(jnp.int32, (c, c), 0)
        jj = lax.broadcasted_iota(jnp.int32, (c, c), 1)
        for p in range(npairs):
            for d in range(2):
                lg_lane = _log_sigmoid(thl_ref[d, :, lanes_of(p)])
                dist = (ii - jj) if d == 0 else (jj - ii)
                for hh in range(2):
                    head = 2 * (grp * npairs + p) + hh
                    lg_h = _log_sigmoid(thb_ref[d, pl.ds(head, 1), :])
                    dm = jnp.where(dist >= 0, jnp.exp(lg_h * jnp.maximum(dist, 0).astype(F32)), 0.0)
                    tab_ref[p, d, _TAB_DMAT + hh * c:_TAB_DMAT + (hh + 1) * c, :] = dm
                if d == 0:
                    q_dec = jnp.exp(lg_lane * (rowf + 1.0))
                    k_dec = jnp.exp(lg_lane * (c - 1.0 - rowf))
                else:
                    q_dec = jnp.exp(lg_lane * (c - rowf))
                    k_dec = jnp.exp(lg_lane * rowf)
                tab_ref[p, d, _TAB_QDEC:_TAB_QDEC + c, :] = q_dec
                tab_ref[p, d, _TAB_KDEC:_TAB_KDEC + c, :] = k_dec
                tab_ref[p, d, _TAB_CDEC:_TAB_CDEC + 8, :] = jnp.broadcast_to(
                    jnp.exp(lg_lane * float(c)), (8, LANES))

    def first_level(p, d, r0):
        qc = q_ref[0, pl.ds(r0, c), lanes_of(p)]
        kc = k_ref[0, pl.ds(r0, c), lanes_of(p)]
        vcb = v_ref[0, pl.ds(r0, c), lanes_of(p)].astype(BF16)
        qs = jnp.concatenate([jnp.where(lo_head, qc, 0.0), jnp.where(lo_head, 0.0, qc)], axis=0)
        sc = lax.dot_general(qs.astype(BF16), kc.astype(BF16), _NT, preferred_element_type=F32)
        sc = sc * tab_ref[p, d, _TAB_DMAT:_TAB_DMAT + 2 * c, :]
        kd = kc * tab_ref[p, d, _TAB_KDEC:_TAB_KDEC + c, :]
        upd = jnp.where(same_head, _dot(kd.T.astype(BF16), vcb), 0.0)
        return qc, vcb, sc.astype(BF16), upd

    def second_level(p, d, lvl1, s):
        qc, vcb, scb, upd = lvl1
        pv = _dot(scb, vcb)
        qd = qc * tab_ref[p, d, _TAB_QDEC:_TAB_QDEC + c, :]
        o = _dot(qd.astype(BF16), s.astype(BF16)) + jnp.where(lo_head, pv[:c], pv[c:])
        return o, tab_ref[p, d, _TAB_CDEC:_TAB_CDEC + 1, :] * s + upd

    def init_state(p, d):
        if not has_s0:
            return jnp.zeros((LANES, LANES), F32)
        z = jnp.zeros((RET_DIM, RET_DIM), F32)
        return jnp.concatenate([jnp.concatenate([s0_ref[0, d, 2 * p], z], axis=1),
                                jnp.concatenate([z, s0_ref[0, d, 2 * p + 1]], axis=1)], axis=0)

    units = [(p, d) for p in range(npairs) for d in range(2)]

    def scan_body(n, states):
        def row0(d, j):
            idx = n * cpt + j
            return pl.multiple_of((idx if d == 0 else nc - 1 - idx) * c, c)

        lvl = {(p, d, j): first_level(p, d, row0(d, j)) for j in range(cpt) for p, d in units}
        states = list(states)
        for j in range(cpt):
            for ui, (p, d) in enumerate(units):
                o, states[ui] = second_level(p, d, lvl[(p, d, j)], states[ui])
                if d == 0:
                    o_ref[0, pl.ds(row0(d, j), c), lanes_of(p)] = o
                else:
                    ob_ref[pl.ds(row0(d, j), c), lanes_of(p)] = o
        return tuple(states)

    assert nc % cpt == 0
    states = lax.fori_loop(0, nc // cpt, scan_body, tuple(init_state(p, d) for p, d in units))
    for ui, (p, d) in enumerate(units):
        sfin_ref[0, d, 2 * p] = states[ui][:RET_DIM, :RET_DIM]
        sfin_ref[0, d, 2 * p + 1] = states[ui][RET_DIM:, RET_DIM:]

    head_mean = jnp.where(same_head, 1.0 / RET_DIM, 0.0).astype(BF16)

    def norm_body(n, carry):
        r0 = pl.multiple_of(n * c, c)
        for p in range(npairs):
            o = o_ref[0, pl.ds(r0, c), lanes_of(p)] + ob_ref[pl.ds(r0, c), lanes_of(p)]
            o2_hi, o2_lo = _split(o * o)
            ms = _dot(o2_hi, head_mean) + _dot(o2_lo, head_mean)
            o_ref[0, pl.ds(r0, c), lanes_of(p)] = (o * lax.rsqrt(ms + EPS) * gn_ref[:, lanes_of(p)]
                                                   * g_ref[0, pl.ds(r0, c), lanes_of(p)])
        return carry

    lax.fori_loop(0, nc, norm_body, 0, unroll=min(nc, 4))


def _ret_call(u, theta, gn, s0bd, layer=0):
    b, l, _ = u.shape
    nc = l // RET_CHUNK
    has_s0 = s0bd is not None
    cpt = 8 if nc >= 8 else nc
    npairs = max(1, 4 // cpt)
    ngrp = RET_HEADS // 2 // npairs
    w = npairs * LANES
    st_block = (2, 2 * npairs, RET_DIM, RET_DIM)
    if not has_s0:
        s0bd = jnp.zeros((1,) + st_block, F32)
        s0_spec = pl.BlockSpec((1,) + st_block, lambda g, i: (0, 0, 0, 0, 0))
    else:
        s0_spec = pl.BlockSpec((1, None) + st_block, lambda g, i: (i, layer, 0, g, 0, 0))
    th_lane = jnp.repeat(theta, RET_DIM, axis=1).reshape(2, 1, W_RET)
    th_bcast = jnp.broadcast_to(theta[:, :, None], (2, RET_HEADS, LANES))
    col = lambda c0: pl.BlockSpec((1, l, w), lambda g, i: (i, 0, c0 // w + g))
    o, sfin = pl.pallas_call(
        functools.partial(_ret_kernel, nc=nc, cpt=cpt, npairs=npairs, has_s0=has_s0),
        grid=(ngrp, b),
        in_specs=[col(COL_QR), col(COL_KR), col(COL_VR), col(COL_GR),
                  pl.BlockSpec((2, 1, w), lambda g, i: (0, 0, g)),
                  pl.BlockSpec((2, RET_HEADS, LANES), lambda g, i: (0, 0, 0)),
                  pl.BlockSpec((1, w), lambda g, i: (0, g)),
                  s0_spec],
        out_specs=[pl.BlockSpec((1, l, w), lambda g, i: (i, 0, g)),
                   pl.BlockSpec((1,) + st_block, lambda g, i: (i, 0, g, 0, 0))],
        out_shape=[jax.ShapeDtypeStruct((b, l, W_RET), F32),
                   jax.ShapeDtypeStruct((b, 2, RET_HEADS, RET_DIM, RET_DIM), F32)],
        scratch_shapes=[pltpu.VMEM((l, w), F32), pltpu.VMEM((npairs, 2, _TAB_ROWS, LANES), F32)],
        compiler_params=_cparams(("arbitrary", "arbitrary")),
        name="retention_s0" if has_s0 else "retention",
    )(u, u, u, u, th_lane, th_bcast, gn.reshape(1, W_RET), s0bd)
    return o, sfin


def _filter_positions(l):
    f32 = np.float32
    t = np.linspace(0.0, 1.0, l, dtype=f32)[:, None]
    w = (f32(2.0 * math.pi) * np.arange(l, dtype=f32)[:, None] / f32(l)).astype(f32)
    f = np.linspace(1e-4, HY_BANDS - 1, HY_BANDS, dtype=f32)[None, :]
    z = np.concatenate([t, np.cos(f * w), -np.sin(f * w)], axis=-1).astype(f32)
    z = np.pad(z, ((0, 0), (0, 32 - HY_POS_FEAT)))
    return jnp.asarray(np.concatenate([z, z[:1], z[1:][::-1]], axis=0))


def _hyena_deltas():
    max_decay = math.log(HY_DECAY_TARGET) / HY_FAST_PCT
    min_decay = math.log(HY_DECAY_TARGET) / HY_SLOW_PCT
    return jnp.asarray(np.abs(np.linspace(min_decay, max_decay, W_HY, dtype=np.float32))[None, :])


def _filter_hidden(z_ref, w1_ref, b1_ref, fr_ref):
    pre = jnp.dot(z_ref[...], w1_ref[0], precision=HIGHEST, preferred_element_type=F32) + b1_ref[0]
    return jnp.sin(fr_ref[0] * pre)


def _filter_raw(hid, w2f, w2b, tp, dl, row0, l):
    win = jnp.exp(-tp * dl)
    row = row0 + lax.broadcasted_iota(jnp.int32, win.shape, 0)
    hf = jnp.dot(hid, w2f, precision=HIGHEST, preferred_element_type=F32) * win
    hb = jnp.dot(hid, w2b, precision=HIGHEST, preferred_element_type=F32) * win
    hf = jnp.where(row < l, hf, 0.0)
    hb = jnp.where((row > l) | (row == 0), hb, 0.0)
    return hf + hb, jnp.sum(jnp.abs(hf) + jnp.abs(hb), axis=0, keepdims=True)


def _with_skip(g, skip):
    row = lax.broadcasted_iota(jnp.int32, g.shape, 0)
    return g + jnp.where(row == 0, skip, 0.0)


def _filt_ctx_kernel(z_ref, w1_ref, b1_ref, fr_ref, w2_ref, dl_ref, sk_ref, fh_ref, fl_ref, g_ref):
    hid = _filter_hidden(z_ref, w1_ref, b1_ref, fr_ref)
    tp = z_ref[:, 0:1]
    for o in range(2):
        w2f = w2_ref[0, :, (2 * o) * W_HY:(2 * o + 1) * W_HY]
        w2b = w2_ref[0, :, (2 * o + 1) * W_HY:(2 * o + 2) * W_HY]
        raw, nrm = _filter_raw(hid, w2f, w2b, tp, dl_ref[...], 0, z_ref.shape[0] // 2)
        g = _with_skip(raw / nrm, sk_ref[0, pl.ds(o, 1), :])
        g_ref[0, o] = _dot3c(fh_ref[...], fl_ref[...], g)


def _ctx_dft_tables(l):
    n = 2 * l
    k = np.arange(n)[:, None]
    t = np.arange(l)[None, :]
    ang = 2.0 * np.pi * k * t / n
    c, s = np.cos(ang), np.sin(ang)
    fwd = np.block([[c, s], [-s, c]])
    inv = np.block([[c.T, -s.T], [s.T, c.T]])
    n_all = np.arange(n)[None, :]
    angg = 2.0 * np.pi * k * n_all / n
    filt = np.concatenate([np.cos(angg), -np.sin(angg)], axis=0) / n
    return _split_np(fwd), _split_np(inv), _split_np(filt)


def _filt_ctx_call(l, w1, b1, freq, w2, skip):
    n = 2 * l
    z_ext = _filter_positions(l)
    _, _, (fh, fl) = _ctx_dft_tables(l)
    w1p = jnp.pad(w1, ((0, 0), (0, 32 - HY_POS_FEAT), (0, 0)))
    lay = lambda *shape: pl.BlockSpec((1,) + shape, lambda d: (d,) + (0,) * len(shape))
    full = lambda a: pl.BlockSpec(a.shape, lambda d: (0,) * a.ndim)
    dl = _hyena_deltas()
    return pl.pallas_call(
        _filt_ctx_kernel,
        grid=(DEPTH,),
        in_specs=[full(z_ext), lay(32, HY_FILT_HID), lay(1, HY_FILT_HID), lay(1, HY_FILT_HID),
                  lay(HY_FILT_HID, 4 * W_HY), full(dl), lay(2, W_HY), full(fh), full(fl)],
        out_specs=pl.BlockSpec((1, 2, 2 * n, W_HY), lambda d: (d, 0, 0, 0)),
        out_shape=jax.ShapeDtypeStruct((DEPTH, 2, 2 * n, W_HY), F32),
        compiler_params=_cparams(("arbitrary",)),
        name="hyena_filter_ctx",
    )(z_ext, w1p, b1.reshape(DEPTH, 1, -1), freq.reshape(DEPTH, 1, -1), w2, dl, skip, fh, fl)


def _lat_dft_tables():
    ka = np.arange(NA)[:, None]
    b = np.arange(NB)[:, None, None]
    kb = np.arange(NB)[:, None]
    bb = np.arange(NB)[None, :]
    a_half = np.arange(NA // 2)[None, :]
    a_full = np.arange(NA)[None, :]
    phi = 2.0 * np.pi * (ka * a_half / NA + b * ka / LAT_N)
    c, s = np.cos(phi), np.sin(phi)
    a_fwd = np.concatenate([c, s], axis=2)
    ct, st = np.swapaxes(c, 1, 2), np.swapaxes(s, 1, 2)
    a_inv = np.concatenate([ct, st], axis=2)
    phig = 2.0 * np.pi * (ka * a_full / NA + b * ka / LAT_N)
    a_flt = np.concatenate([np.cos(phig), -np.sin(phig)], axis=1) / LAT_N
    ang = 2.0 * np.pi * kb * bb / NB
    c2, s2 = np.cos(ang), np.sin(ang)
    f_fwd = np.block([[c2, s2], [-s2, c2]])
    f_inv = np.block([[c2, -s2], [s2, c2]])
    return (_split_np(a_fwd), _split_np(a_inv), _split_np(a_flt), _split_np(f_fwd), _split_np(f_inv))


def _stage_b_rows(ka):
    re = pl.ds(ka, NB, stride=Y_PITCH)
    im = pl.ds(NA + ka, NB, stride=Y_PITCH)
    return re, im


def _filt_lat_kernel(z_ref, w1_ref, b1_ref, fr_ref, w2f_ref, w2b_ref, dl_ref, sk_ref, ah_ref, f2h_ref,
                     g_ref, hid_ref, gt_ref, y_ref):
    step = pl.program_id(1)
    rch = 1024
    nch = LAT_N // rch
    rows_of = lambda i: pl.ds(pl.multiple_of(i * rch, rch), rch)

    @pl.when(step == 0)
    def _():
        def hid_chunk(i, carry):
            r = rows_of(i)
            pre = jnp.dot(z_ref[r, :], w1_ref[0], precision=HIGHEST, preferred_element_type=F32)
            hid_ref[r, :] = jnp.sin(fr_ref[0] * (pre + b1_ref[0]))
            return carry

        lax.fori_loop(0, nch, hid_chunk, 0)

    w2f_hl = _split(w2f_ref[0])
    w2b_hl = _split(w2b_ref[0])

    def raw_chunk(w2_hl, i, nrm):
        r = rows_of(i)
        hh, hl = _split(hid_ref[r, :])
        h = _dot(hh, w2_hl[0]) + _dot(hl, w2_hl[0]) + _dot(hh, w2_hl[1])
        h = h * jnp.exp(-z_ref[r, 0:1] * dl_ref[...])
        row = i * rch + lax.broadcasted_iota(jnp.int32, h.shape, 0)
        h = jnp.where(row == LAT_L, 0.0, h)
        for s in range(rch // NB):
            slab = pl.ds(pl.multiple_of((i * (rch // NB) + s) * X_PITCH, 8), NB)
            gt_ref[slab, :] = h[s * NB:(s + 1) * NB]
        return nrm + jnp.sum(jnp.abs(h), axis=0, keepdims=True)

    nrm = lax.fori_loop(0, nch // 2, functools.partial(raw_chunk, w2f_hl), jnp.zeros((1, LANES), F32))
    nrm = lax.fori_loop(nch // 2, nch, functools.partial(raw_chunk, w2b_hl), nrm)
    hh, hl = _split(hid_ref[0:8, :])
    hb0 = _dot(hh, w2b_hl[0]) + _dot(hl, w2b_hl[0]) + _dot(hh, w2b_hl[1])
    hb0 = hb0 * jnp.exp(-z_ref[0:8, 0:1] * dl_ref[...])
    hb0 = jnp.where(lax.broadcasted_iota(jnp.int32, hb0.shape, 0) == 0, hb0, 0.0)
    gt_ref[0:8, :] = gt_ref[0:8, :] + hb0
    nrm = nrm + jnp.sum(jnp.abs(hb0), axis=0, keepdims=True)

    def norm_slab(a, carry):
        slab = pl.ds(pl.multiple_of(a * X_PITCH, 8), NB)
        gt_ref[slab, :] = gt_ref[slab, :] / nrm
        return carry

    lax.fori_loop(0, NA, norm_slab, 0, unroll=8)
    order = step // (W_HY // LANES)
    gt_ref[0:8, :] = _with_skip(gt_ref[0:8, :], sk_ref[0, pl.ds(order, 1), :])

    def stage_a(b, carry):
        rows = gt_ref[pl.ds(b, NA, stride=X_PITCH), :]
        y_ref[pl.ds(pl.multiple_of(b * Y_PITCH, 8), 2 * NA), :] = _dot1c(ah_ref[b], rows)
        return carry

    lax.fori_loop(0, NB, stage_a, 0, unroll=8)

    def stage_b(j, carry):
        ka = 2 * j
        re, im = _stage_b_rows(ka)
        re1, im1 = _stage_b_rows(ka + 1)
        z = jnp.concatenate([jnp.concatenate([y_ref[re, :], y_ref[im, :]], axis=0),
                             jnp.concatenate([y_ref[re1, :], y_ref[im1, :]], axis=0)], axis=1)
        x = _dot1c(f2h_ref[...], z)
        g0 = pl.multiple_of(ka * 2 * NB, 2 * NB)
        g_ref[0, 0, pl.ds(g0, 2 * NB), :] = x[:, :LANES]
        g_ref[0, 0, pl.ds(g0 + 2 * NB, 2 * NB), :] = x[:, LANES:]
        return carry

    lax.fori_loop(0, NA // 2, stage_b, 0, unroll=4)


def _filt_lat_call(w1, b1, freq, w2, skip):
    z_ext = _filter_positions(LAT_L)
    _, _, (ah, _), (f2h, _), _ = _lat_dft_tables()
    w1p = jnp.pad(w1, ((0, 0), (0, 32 - HY_POS_FEAT), (0, 0)))
    nct = W_HY // LANES
    one = pl.Buffered(1)
    lay = lambda *shape: pl.BlockSpec((1,) + shape, lambda d, s: (d,) + (0,) * len(shape))
    full = lambda a: pl.BlockSpec(a.shape, lambda d, s: (0,) * a.ndim, pipeline_mode=one)
    dl = _hyena_deltas()
    return pl.pallas_call(
        _filt_lat_kernel,
        grid=(DEPTH, 2 * nct),
        in_specs=[full(z_ext), lay(32, HY_FILT_HID), lay(1, HY_FILT_HID), lay(1, HY_FILT_HID),
                  pl.BlockSpec((1, HY_FILT_HID, LANES), lambda d, s: (d, 0, (s // nct) * 2 * nct + s % nct)),
                  pl.BlockSpec((1, HY_FILT_HID, LANES),
                               lambda d, s: (d, 0, (s // nct) * 2 * nct + nct + s % nct)),
                  pl.BlockSpec((1, LANES), lambda d, s: (0, s % nct)),
                  pl.BlockSpec((1, 2, LANES), lambda d, s: (d, 0, s % nct)),
                  full(ah), full(f2h)],
        out_specs=pl.BlockSpec((1, 1, NA * 2 * NB, LANES), lambda d, s: (d, s, 0, 0)),
        out_shape=jax.ShapeDtypeStruct((DEPTH, 2 * nct, NA * 2 * NB, LANES), F32),
        scratch_shapes=[pltpu.VMEM((LAT_N, HY_FILT_HID), F32), pltpu.VMEM((NA * X_PITCH, LANES), F32),
                        pltpu.VMEM((NB * Y_PITCH, LANES), F32)],
        compiler_params=_cparams(("arbitrary", "arbitrary")),
        name="hyena_filter_lat",
    )(z_ext, w1p, b1.reshape(DEPTH, 1, -1), freq.reshape(DEPTH, 1, -1), w2, w2, dl, skip, ah, f2h)


def _short_conv_rows(ref, bi, r0, rows, first, last, w):
    total = ref.shape[1]
    cur = ref[bi, pl.ds(r0, rows), :]
    before = ref[bi, pl.ds(jnp.maximum(r0 - 1, 0), 1), :]
    after = ref[bi, pl.ds(jnp.minimum(r0 + rows, total - 1), 1), :]
    before = jnp.where(first, 0.0, before)
    after = jnp.where(last, 0.0, after)
    rid = lax.broadcasted_iota(jnp.int32, cur.shape, 0)
    prev = jnp.where(rid == 0, before, pltpu.roll(cur, 1, 0))
    nxt = jnp.where(rid == rows - 1, after, pltpu.roll(cur, rows - 1, 0))
    return prev * w[0:1] + cur * w[1:2] + nxt * w[2:3]


def _short_conv_interior(ref, bi, r0, rows, w):
    prev = ref[bi, pl.ds(r0 - 1, rows), :]
    cur = ref[bi, pl.ds(r0, rows), :]
    nxt = ref[bi, pl.ds(r0 + 1, rows), :]
    return prev * w[0:1] + cur * w[1:2] + nxt * w[2:3]


def _cmul(xr, xi, gr, gi):
    return xr * gr - xi * gi, xr * gi + xi * gr


def _hy_ctx_kernel(v_ref, x1_ref, x2_ref, cw_ref, g_ref, fh_ref, ih_ref, o_ref):
    l = v_ref.shape[1]
    n = 2 * l

    def sc(ref, bi, grp):
        w = cw_ref[:, grp * W_HY:(grp + 1) * W_HY]
        return _short_conv_rows(ref, bi, 0, l, True, True, w)

    def conv(zr, zi, order):
        x = _dot1c(fh_ref[...], jnp.concatenate([zr, zi], axis=0))
        pr, pi = _cmul(x[:n], x[n:], g_ref[order, :n], g_ref[order, n:])
        y = _dot1c(ih_ref[...], jnp.concatenate([pr, pi], axis=0))
        return y[:l], y[l:]

    yr, yi = conv(sc(v_ref, 0, 0), sc(v_ref, 1, 0), 0)
    yr, yi = conv(sc(x1_ref, 0, 1) * yr, sc(x1_ref, 1, 1) * yi, 1)
    o_ref[0] = sc(x2_ref, 0, 2) * yr
    o_ref[1] = sc(x2_ref, 1, 2) * yi


def _hy_ctx_call(u, conv_w, g_spec, layer):
    b, l, _ = u.shape
    (fh, _), (ih, _), _ = _ctx_dft_tables(l)
    grp = lambda g: pl.BlockSpec((2, l, W_HY), lambda i: (i, 0, g))
    full = lambda a: pl.BlockSpec(a.shape, lambda i: (0,) * a.ndim)
    return pl.pallas_call(
        _hy_ctx_kernel,
        grid=(b // 2,),
        in_specs=[grp(0), grp(1), grp(2), full(conv_w),
                  pl.BlockSpec((None,) + g_spec.shape[1:], lambda i: (layer, 0, 0, 0)), full(fh), full(ih)],
        out_specs=pl.BlockSpec((2, l, W_HY), lambda i: (i, 0, 0)),
        out_shape=jax.ShapeDtypeStruct((b, l, W_HY), F32),
        compiler_params=_cparams(("parallel",)),
        name="hyena_ctx",
    )(u, u, u, conv_w, g_spec, fh, ih)


def _hy_lat_kernel(z_ref, m_ref, cw_ref, g_ref, af_ref, ai_ref, f2_ref, f3_ref, o_ref, xr_scr, xi_scr,
                   y_scr, *, conv_in):
    x_scr = (xr_scr, xi_scr)
    na_half = NA // 2
    w_in = cw_ref[0] if conv_in else None
    w_mul = cw_ref[1]

    def conv_slab(ref, bi, a, w):
        if isinstance(a, int):
            return _short_conv_rows(ref, bi, a * NB, NB, a == 0, a == na_half - 1, w)
        return _short_conv_interior(ref, bi, pl.multiple_of(a * NB, NB), NB, w)

    def edges_then_interior(body):
        body(0, 0)
        body(na_half - 1, 0)
        lax.fori_loop(1, na_half - 1, body, 0, unroll=2)

    def load_in(a, carry):
        for bi in range(2):
            if conv_in:
                val = conv_slab(z_ref, bi, a, w_in)
            else:
                val = z_ref[bi, pl.ds(pl.multiple_of(a * NB, NB), NB), :]
            x_scr[bi][pl.ds(pl.multiple_of(a * X_PITCH, 8), NB), :] = val
        return carry

    edges_then_interior(load_in)

    def stage_a(b, carry):
        zr = xr_scr[pl.ds(b, na_half, stride=X_PITCH), :]
        zi = xi_scr[pl.ds(b, na_half, stride=X_PITCH), :]
        rhs = jnp.concatenate([jnp.concatenate([zr, zi], axis=0), jnp.concatenate([zi, -zr], axis=0)],
                              axis=1)
        y = _dot1c(af_ref[b], rhs)
        r0 = pl.multiple_of(b * Y_PITCH, 8)
        y_scr[pl.ds(r0, NA), :] = y[:, :LANES]
        y_scr[pl.ds(r0 + NA, NA), :] = y[:, LANES:]
        return carry

    lax.fori_loop(0, NB, stage_a, 0, unroll=32)

    def spectrum_product(j):
        ka = 2 * j
        re, im = _stage_b_rows(ka)
        re1, im1 = _stage_b_rows(ka + 1)
        z = jnp.concatenate([jnp.concatenate([y_scr[re, :], y_scr[im, :]], axis=0),
                             jnp.concatenate([y_scr[re1, :], y_scr[im1, :]], axis=0)], axis=1)
        x = _dot1c(f2_ref[...], z)
        g0 = pl.multiple_of(ka * 2 * NB, 2 * NB)
        gr = jnp.concatenate([g_ref[pl.ds(g0, NB), :], g_ref[pl.ds(g0 + 2 * NB, NB), :]], axis=1)
        gi = jnp.concatenate([g_ref[pl.ds(g0 + NB, NB), :], g_ref[pl.ds(g0 + 3 * NB, NB), :]], axis=1)
        pr, pi = _cmul(x[:NB], x[NB:], gr, gi)
        return jnp.concatenate([pr, pi], axis=0).astype(BF16)

    def inverse_b(j, prod):
        ka = 2 * j
        re, im = _stage_b_rows(ka)
        re1, im1 = _stage_b_rows(ka + 1)
        u = _dot(f3_ref[...], prod)
        y_scr[re, :] = u[:NB, :LANES]
        y_scr[im, :] = u[NB:, :LANES]
        y_scr[re1, :] = u[:NB, LANES:]
        y_scr[im1, :] = u[NB:, LANES:]

    def stage_b(j, prod):
        nxt = spectrum_product(j + 1)
        inverse_b(j, prod)
        return nxt

    last = lax.fori_loop(0, NA // 2 - 1, stage_b, spectrum_product(0), unroll=4)
    inverse_b(NA // 2 - 1, last)

    def stage_c(b, carry):
        r0 = pl.multiple_of(b * Y_PITCH, 8)
        ur = y_scr[pl.ds(r0, NA), :]
        ui = y_scr[pl.ds(r0 + NA, NA), :]
        rhs = jnp.concatenate([jnp.concatenate([ur, -ui], axis=0), jnp.concatenate([ui, ur], axis=0)],
                              axis=1)
        y = _dot1c(ai_ref[b], rhs)
        xr_scr[pl.ds(b, na_half, stride=X_PITCH), :] = y[:, :LANES]
        xi_scr[pl.ds(b, na_half, stride=X_PITCH), :] = y[:, LANES:]
        return carry

    lax.fori_loop(0, NB, stage_c, 0, unroll=32)

    def store_out(a, carry):
        for bi in range(2):
            mul = conv_slab(m_ref, bi, a, w_mul)
            o_ref[bi, pl.ds(pl.multiple_of(a * NB, NB), NB), :] = (
                x_scr[bi][pl.ds(pl.multiple_of(a * X_PITCH, 8), NB), :] * mul)
        return carry

    edges_then_interior(store_out)


def _hy_lat_call(src, src_col, u, mul_col, conv_w2, g_spec, layer, order, *, conv_in):
    b, l, _ = u.shape
    nct = W_HY // LANES
    (af, _), (ai, _), _, (f2, _), (f3, _) = _lat_dft_tables()
    one = pl.Buffered(1)
    blk = lambda col: pl.BlockSpec((2, l, LANES), lambda c, p: (p, 0, col + c))
    const = lambda a: pl.BlockSpec(a.shape, lambda c, p: (0,) * a.ndim, pipeline_mode=one)
    return pl.pallas_call(
        functools.partial(_hy_lat_kernel, conv_in=conv_in),
        grid=(nct, b // 2),
        in_specs=[blk(src_col), blk(mul_col),
                  pl.BlockSpec((2, 3, LANES), lambda c, p: (0, 0, c)),
                  pl.BlockSpec((None, None, NA * 2 * NB, LANES), lambda c, p: (layer, order * nct + c, 0, 0),
                               pipeline_mode=one),
                  const(af), const(ai), const(f2), const(f3)],
        out_specs=pl.BlockSpec((2, l, LANES), lambda c, p: (p, 0, c)),
        out_shape=jax.ShapeDtypeStruct((b, l, W_HY), F32),
        scratch_shapes=[pltpu.VMEM(((NA // 2) * X_PITCH, LANES), F32),
                        pltpu.VMEM(((NA // 2) * X_PITCH, LANES), F32),
                        pltpu.VMEM((NB * Y_PITCH, LANES), F32)],
        compiler_params=_cparams(("arbitrary", "arbitrary")),
        name="hyena_lat_conv_in" if conv_in else "hyena_lat",
    )(src, u, conv_w2, g_spec, af, ai, f2, f3)


def _rope_tables(l):
    f32 = np.float32
    rows = l // GRID_W
    row = np.repeat(np.arange(rows), GRID_W).astype(f32)
    col = np.tile(np.arange(GRID_W), rows).astype(f32)
    quarter = HEAD_DIM // 4
    inv = np.power(f32(ROPE_BASE), -np.arange(quarter, dtype=f32) / f32(quarter)).astype(f32)
    ang = np.concatenate([row[:, None] * inv, col[:, None] * inv], axis=-1).astype(f32)
    cos, sin = np.cos(ang), np.sin(ang)
    q = quarter
    cos_h = np.concatenate([cos[:, :q], cos[:, :q], cos[:, q:], cos[:, q:]], axis=-1)
    sin_h = np.concatenate([-sin[:, :q], sin[:, :q], -sin[:, q:], sin[:, q:]], axis=-1)
    return jnp.asarray(np.tile(cos_h, (1, 2))), jnp.asarray(np.tile(sin_h, (1, 2)))


def kernel(x_prompt, x_sample, c, cache_k, cache_v, state_ret, c_ctx, norm_w, w_mod, b_mod, w_in, hy_conv,
           hy_filt_w1, hy_filt_b1, hy_filt_freq, hy_filt_w2, hy_skip, attn_sink, ret_theta, ret_gn,
           w_branch_a, w_branch_b, w_branch_c, w_merge, b_merge, w_out, final_norm_w):
    d = D_MODEL
    bc, lc, _ = x_prompt.shape
    bl, ll, _ = x_sample.shape
    assert ll == LAT_L and bc % 2 == 0 and bl % 2 == 0
    past = cache_k.shape[2]

    cond = jnp.zeros((16, d), F32).at[:bl].set(c).at[bl].set(c_ctx)
    mod = _mod_call(cond, w_mod, b_mod)

    g_ctx = _filt_ctx_call(lc, hy_filt_w1, hy_filt_b1, hy_filt_freq, hy_filt_w2, hy_skip)
    g_lat = _filt_lat_call(hy_filt_w1, hy_filt_b1, hy_filt_freq, hy_filt_w2, hy_skip)

    cos_t, sin_t = _rope_tables(ll)
    w_in_b = w_in.astype(BF16)
    wm_b = w_merge.astype(BF16)
    wa_b = w_branch_a.astype(BF16)
    wb_b = w_branch_b.astype(BF16)
    wc_b = w_branch_c.astype(BF16)
    wo_b = w_out.astype(BF16)
    fnw = final_norm_w.reshape(1, d)
    k_ctx = cache_k.reshape(bl, DEPTH, past, W_KV)
    v_ctx = cache_v.reshape(bl, DEPTH, past, W_KV)
    hy_cols = COL_HY // LANES
    nct = W_HY // LANES

    xp, xs = x_prompt, x_sample
    ks_out, vs_out, ss_out = [], [], []
    for l in range(DEPTH):
        final = l == DEPTH - 1
        nw = norm_w[l].reshape(1, d)
        bm = b_merge[l].reshape(1, -1)
        shift, scale, gate = (mod[l, :, i * d:(i + 1) * d][:, None, :] for i in range(3))
        conv_w = hy_conv[l]
        cw = lambda g: conv_w[:, g * W_HY:(g + 1) * W_HY]

        sl = slice(bl, bl + 1)
        u = _in_call(xp, shift[sl], scale[sl], nw, w_in_b[l], cos_t, sin_t, rope=False)
        ya = _hy_ctx_call(u, conv_w, g_ctx, l)
        yb = _attn_ctx_call(u, attn_sink[l])
        yc, sfin = _ret_call(u, ret_theta[l], ret_gn[l], None)
        res = _out_call(xp, shift[sl], scale[sl], gate[sl], nw, ya, u, yb, yc, wm_b[l], bm, wa_b[l], wb_b[l],
                        wc_b[l], wo_b[l], fnw, final=final)
        xp = res[0]
        if final:
            y_prompt = res[1]
        ks_out.append(u[:, :, COL_KA:COL_KA + W_KV].reshape(bc, lc, ATT_KV_HEADS, HEAD_DIM))
        vs_out.append(u[:, :, COL_VA:COL_VA + W_KV].reshape(bc, lc, ATT_KV_HEADS, HEAD_DIM))
        ss_out.append(sfin)

        sl = slice(0, bl)
        u = _in_call(xs, shift[sl], scale[sl], nw, w_in_b[l], cos_t, sin_t, rope=True)
        z1 = _hy_lat_call(u, hy_cols, u, hy_cols + nct, jnp.stack([cw(0), cw(1)]), g_lat, l, 0, conv_in=True)
        ya = _hy_lat_call(z1, 0, u, hy_cols + 2 * nct, jnp.stack([cw(2), cw(2)]), g_lat, l, 1, conv_in=False)
        yb = _attn_lat_call(u, k_ctx, v_ctx, attn_sink[l], l)
        yc, _ = _ret_call(u, ret_theta[l], ret_gn[l], state_ret, l)
        res = _out_call(xs, shift[sl], scale[sl], gate[sl], nw, ya, u, yb, yc, wm_b[l], bm, wa_b[l], wb_b[l],
                        wc_b[l], wo_b[l], fnw, final=final)
        xs = res[0]
        if final:
            y_sample = res[1]

    new_cache_k = jnp.stack(ks_out, axis=1)
    new_cache_v = jnp.stack(vs_out, axis=1)
    new_state_ret = jnp.stack(ss_out, axis=1)
    return (y_prompt, y_sample, new_cache_k, new_cache_v, new_state_ret)
```

```python
import functools
import math

import numpy as np
import jax
import jax.numpy as jnp
from jax import lax
from jax.experimental import pallas as pl
from jax.experimental.pallas import tpu as pltpu

F32 = jnp.float32
BF16 = jnp.bfloat16
HIGHEST = lax.Precision.HIGHEST

D_MODEL = 1024
DEPTH = 4
GRID_W = 64
W_HY = 512
HY_BANDS = 8
HY_POS_FEAT = 1 + 2 * HY_BANDS
HY_FILT_HID = 64
HY_DECAY_TARGET = 1e-2
HY_FAST_PCT = 0.3
HY_SLOW_PCT = 1.5
ATT_HEADS = 8
ATT_KV_HEADS = 2
ATT_GROUP = ATT_HEADS // ATT_KV_HEADS
HEAD_DIM = 64
W_ATT = ATT_HEADS * HEAD_DIM
W_KV = ATT_KV_HEADS * HEAD_DIM
ATT_BLOCK = 128
ATT_QB = 8
RET_HEADS = 8
RET_DIM = 64
W_RET = RET_HEADS * RET_DIM
RET_CHUNK = 128
ROPE_BASE = 10000.0
EPS = 1e-6
NEG = -1e30

LANES = 128
MXU_ROWS = 512
VMEM_LIMIT = 56 * 1024 * 1024

IN_DIM = 5376
COL_HY = 0
COL_GH = 1536
COL_QA = 2048
COL_GA = 2560
COL_QR = 3072
COL_KR = 3584
COL_VR = 4096
COL_GR = 4608
COL_KA = 5120
COL_VA = 5248
_W_COL = {COL_HY: 0, COL_HY + 512: 512, COL_HY + 1024: 1024, COL_GH: 1536, COL_QA: 2048, COL_KA: 2560,
          COL_VA: 2688, COL_GA: 2816, COL_QR: 3328, COL_KR: 3840, COL_VR: 4352, COL_GR: 4864}

LAT_L = 4096
LAT_N = 2 * LAT_L
NA = 64
NB = 128
Y_PITCH = 136
X_PITCH = 136


def _cparams(sem):
    return pltpu.CompilerParams(dimension_semantics=sem, vmem_limit_bytes=VMEM_LIMIT)


def _split_np(a):
    a32 = np.asarray(a, np.float32)
    hi = a32.astype(BF16)
    lo = (a32 - hi.astype(np.float32)).astype(BF16)
    return jnp.asarray(hi), jnp.asarray(lo)


def _split(x):
    hi = x.astype(BF16)
    lo = (x - hi.astype(F32)).astype(BF16)
    return hi, lo


def _dot(a, b):
    return jnp.dot(a, b, preferred_element_type=F32)


def _dot3c(chi, clo, x):
    xh, xl = _split(x)
    return _dot(chi, xh) + _dot(clo, xh) + _dot(chi, xl)


def _dot1c(chi, x):
    return _dot(chi, x.astype(BF16))


def _silu(x):
    return x * jax.nn.sigmoid(x)


def _mod_kernel(c_ref, w_ref, b_ref, o_ref):
    s = _silu(c_ref[...])
    o_ref[0] = jnp.dot(s, w_ref[0], precision=HIGHEST, preferred_element_type=F32) + b_ref[0]


def _mod_call(cond, w_mod, b_mod):
    rows, d = cond.shape
    n = w_mod.shape[-1]
    tn = 1024
    return pl.pallas_call(
        _mod_kernel,
        grid=(DEPTH, n // tn),
        in_specs=[pl.BlockSpec((rows, d), lambda l, j: (0, 0)),
                  pl.BlockSpec((1, d, tn), lambda l, j: (l, 0, j)),
                  pl.BlockSpec((1, 1, tn), lambda l, j: (l, 0, j))],
        out_specs=pl.BlockSpec((1, rows, tn), lambda l, j: (l, 0, j)),
        out_shape=jax.ShapeDtypeStruct((DEPTH, rows, n), F32),
        compiler_params=_cparams(("arbitrary", "arbitrary")),
        name="adaln_mod",
    )(cond, w_mod, b_mod.reshape(DEPTH, 1, n))


def _modulated(x, nw, scale, shift):
    ms = jnp.mean(x * x, axis=-1, keepdims=True)
    h = x * lax.rsqrt(ms + EPS) * nw
    return h * (1.0 + scale) + shift


def _rope128(x, cos, sin_signed, first_half):
    up = pltpu.roll(x, LANES - 16, 1)
    dn = pltpu.roll(x, 16, 1)
    return x * cos + jnp.where(first_half, up, dn) * sin_signed


def _rows(ref):
    bt, tm, w = ref.shape
    return ref[...].reshape(bt * tm, w)


def _put(ref, c0, val):
    bt, tm, _ = ref.shape
    ref[:, :, c0:c0 + val.shape[1]] = val.reshape(bt, tm, val.shape[1])


def _in_kernel(x_ref, shift_ref, scale_ref, nw_ref, w_ref, cos_ref, sin_ref, o_ref, *, rope):
    x = _rows(x_ref)
    rows = x.shape[0]
    hb = _modulated(x, nw_ref[...], scale_ref[0], shift_ref[0]).astype(BF16)
    if rope:
        cos = cos_ref[...]
        sin = sin_ref[...]
        lane = lax.broadcasted_iota(jnp.int32, (rows, LANES), 1)
        first_half = (lane % 32) < 16

    def seg(c0, width):
        w0 = _W_COL[c0]
        return _dot(hb, w_ref[:, w0:w0 + width])

    def put_rope(c0, val, mul):
        for i in range(val.shape[1] // LANES):
            piece = val[:, i * LANES:(i + 1) * LANES]
            if rope:
                piece = _rope128(piece, cos, sin, first_half)
            if mul is not None:
                piece = piece * mul
            _put(o_ref, c0 + i * LANES, piece)

    for g in range(3):
        _put(o_ref, COL_HY + g * 512, seg(COL_HY + g * 512, 512))
    _put(o_ref, COL_GH, _silu(seg(COL_GH, 512)))
    put_rope(COL_QA, seg(COL_QA, 512), None)
    _put(o_ref, COL_GA, _silu(seg(COL_GA, 512)))
    put_rope(COL_QR, seg(COL_QR, 512), None)
    put_rope(COL_KR, seg(COL_KR, 512), RET_DIM ** -0.5)
    _put(o_ref, COL_VR, seg(COL_VR, 512))
    _put(o_ref, COL_GR, _silu(seg(COL_GR, 512)))
    put_rope(COL_KA, seg(COL_KA, 128), None)
    _put(o_ref, COL_VA, seg(COL_VA, 128))


def _token_tiling(b, l, per_batch):
    if l >= MXU_ROWS:
        return 1, MXU_ROWS
    bt = 1 if per_batch else min(b, MXU_ROWS // l)
    return bt, l


def _in_call(x, shift, scale, nw, w, cos_t, sin_t, *, rope):
    b, l, d = x.shape
    per_batch = shift.shape[0] > 1
    bt, tm = _token_tiling(b, l, per_batch)
    assert not rope or bt == 1
    mod_map = (lambda i, j: (i, 0, 0)) if per_batch else (lambda i, j: (0, 0, 0))
    return pl.pallas_call(
        functools.partial(_in_kernel, rope=rope),
        grid=(b // bt, l // tm),
        in_specs=[pl.BlockSpec((bt, tm, d), lambda i, j: (i, j, 0)),
                  pl.BlockSpec((1, 1, d), mod_map),
                  pl.BlockSpec((1, 1, d), mod_map),
                  pl.BlockSpec((1, d), lambda i, j: (0, 0)),
                  pl.BlockSpec((d, IN_DIM), lambda i, j: (0, 0), pipeline_mode=pl.Buffered(1)),
                  pl.BlockSpec((tm, LANES), lambda i, j: (j, 0)),
                  pl.BlockSpec((tm, LANES), lambda i, j: (j, 0))],
        out_specs=pl.BlockSpec((bt, tm, IN_DIM), lambda i, j: (i, j, 0)),
        out_shape=jax.ShapeDtypeStruct((b, l, IN_DIM), F32),
        compiler_params=_cparams(("parallel", "parallel")),
        name="in_proj_rope" if rope else "in_proj",
    )(x, shift, scale, nw, w, cos_t, sin_t)


def _out_kernel(x_ref, shift_ref, scale_ref, gate_ref, nw_ref, ya_ref, gh_ref, yb_ref, yc_ref, wm_ref, bm_ref,
                wa_ref, wb_ref, wc_ref, wo_ref, fnw_ref, *out_refs, final):
    x = _rows(x_ref)
    d = x.shape[1]
    branches = (_rows(ya_ref) * _rows(gh_ref), _rows(yb_ref), _rows(yc_ref))
    projected = [_dot(y.astype(BF16), w_ref[...]) for y, w_ref in zip(branches, (wa_ref, wb_ref, wc_ref))]
    hb = _modulated(x, nw_ref[...], scale_ref[0], shift_ref[0]).astype(BF16)
    merged = None
    for i, proj in enumerate(projected):
        g = jax.nn.sigmoid(_dot(hb, wm_ref[:, i * d:(i + 1) * d]) + bm_ref[:, i * d:(i + 1) * d])
        term = g * proj
        merged = term if merged is None else merged + term
    out = _dot(merged.astype(BF16), wo_ref[...])
    xn = x + gate_ref[0] * out
    _put(out_refs[0], 0, xn)
    if final:
        ms = jnp.mean(xn * xn, axis=-1, keepdims=True)
        _put(out_refs[1], 0, xn * lax.rsqrt(ms + EPS) * fnw_ref[...])


def _out_call(x, shift, scale, gate, nw, ya, u, yb, yc, wm, bm, wa, wb, wc, wo, fnw, *, final):
    b, l, d = x.shape
    per_batch = shift.shape[0] > 1
    bt, tm = _token_tiling(b, l, per_batch)
    mod_map = (lambda i, j: (i, 0, 0)) if per_batch else (lambda i, j: (0, 0, 0))
    tok = lambda w: pl.BlockSpec((bt, tm, w), lambda i, j: (i, j, 0))
    full = lambda a: pl.BlockSpec(a.shape, lambda i, j: (0,) * a.ndim, pipeline_mode=pl.Buffered(1))
    n_out = 2 if final else 1
    res = pl.pallas_call(
        functools.partial(_out_kernel, final=final),
        grid=(b // bt, l // tm),
        in_specs=[tok(d), pl.BlockSpec((1, 1, d), mod_map), pl.BlockSpec((1, 1, d), mod_map),
                  pl.BlockSpec((1, 1, d), mod_map), full(nw), tok(W_HY),
                  pl.BlockSpec((bt, tm, W_HY), lambda i, j: (i, j, COL_GH // W_HY)), tok(W_ATT), tok(W_RET),
                  full(wm), full(bm), full(wa), full(wb), full(wc), full(wo), full(fnw)],
        out_specs=[tok(d)] * n_out,
        out_shape=[jax.ShapeDtypeStruct((b, l, d), F32)] * n_out,
        compiler_params=_cparams(("parallel", "parallel")),
        name="merge_out_final" if final else "merge_out",
    )(x, shift, scale, gate, nw, ya, u, yb, yc, wm, bm, wa, wb, wc, wo, fnw)
    return res


_NT = (((1,), (1,)), ((), ()))


_TN = (((0,), (0,)), ((), ()))
LOG2E = 1.4426950408889634
Q_SCALE = (HEAD_DIM ** -0.5) * LOG2E


def _attn_scores_t(q, kh, kv):
    h0 = kv * ATT_GROUP
    qs = jnp.concatenate([q[:, (h0 + g) * HEAD_DIM:(h0 + g + 1) * HEAD_DIM] for g in range(ATT_GROUP)],
                         axis=0).astype(BF16)
    return lax.dot_general(kh, qs, _NT, preferred_element_type=F32)


def _attn_finish_t(sink_ref, s, vh, g_ref, o_ref, kv, row0=0):
    tk, cols = s.shape
    t = cols // ATT_GROUP
    h0 = kv * ATT_GROUP
    head = lax.broadcasted_iota(jnp.int32, (1, cols), 1) // t
    sink = jnp.full((1, cols), sink_ref[h0], F32)
    for g in range(1, ATT_GROUP):
        sink = jnp.where(head == g, sink_ref[h0 + g], sink)
    sink = sink * LOG2E
    m = jnp.maximum(jnp.max(s, axis=0, keepdims=True), sink)
    p = jnp.exp2(s - m).astype(BF16)
    v_ext = jnp.concatenate([vh, jnp.ones((tk, HEAD_DIM), BF16)], axis=1)
    o_ext = lax.dot_general(v_ext, p, _TN, preferred_element_type=F32)
    denom = o_ext[HEAD_DIM:HEAD_DIM + 1] + jnp.exp2(sink - m)
    o = o_ext[:HEAD_DIM] / denom
    for gp in range(ATT_GROUP // 2):
        pair = jnp.concatenate([o[:, (2 * gp) * t:(2 * gp + 1) * t], o[:, (2 * gp + 1) * t:(2 * gp + 2) * t]],
                               axis=0)
        c0 = (h0 + 2 * gp) * HEAD_DIM
        o_ref[0, row0:row0 + t, c0:c0 + 2 * HEAD_DIM] = pair.T * g_ref[0, row0:row0 + t, c0:c0 + 2 * HEAD_DIM]


def _attn_ctx_kernel(sink_ref, q_ref, k_ref, v_ref, g_ref, o_ref):
    q = q_ref[0] * Q_SCALE
    k = k_ref[0].astype(BF16)
    v = v_ref[0].astype(BF16)
    scores = [_attn_scores_t(q, k[:, kv * HEAD_DIM:(kv + 1) * HEAD_DIM], kv) for kv in range(ATT_KV_HEADS)]
    for kv in range(ATT_KV_HEADS):
        _attn_finish_t(sink_ref, scores[kv], v[:, kv * HEAD_DIM:(kv + 1) * HEAD_DIM], g_ref, o_ref, kv)


def _attn_ctx_call(u, sink):
    b, l, _ = u.shape
    return pl.pallas_call(
        _attn_ctx_kernel,
        grid=(b,),
        in_specs=[pl.BlockSpec(memory_space=pltpu.SMEM),
                  pl.BlockSpec((1, l, W_ATT), lambda i: (i, 0, COL_QA // W_ATT)),
                  pl.BlockSpec((1, l, W_KV), lambda i: (i, 0, COL_KA // W_KV)),
                  pl.BlockSpec((1, l, W_KV), lambda i: (i, 0, COL_VA // W_KV)),
                  pl.BlockSpec((1, l, W_ATT), lambda i: (i, 0, COL_GA // W_ATT))],
        out_specs=pl.BlockSpec((1, l, W_ATT), lambda i: (i, 0, 0)),
        out_shape=jax.ShapeDtypeStruct((b, l, W_ATT), F32),
        compiler_params=_cparams(("parallel",)),
        name="attn_ctx",
    )(sink, u, u, u, u)


def _attn_lat_kernel(sink_ref, q_ref, kp_ref, kc_ref, kn_ref, vp_ref, vc_ref, vn_ref, kx_ref, vx_ref,
                     g_ref, o_ref):
    j = pl.program_id(1)
    last = pl.num_programs(1) - 1
    b = ATT_BLOCK
    nq = ATT_QB
    bf = lambda ref: ref[0].astype(BF16)
    kx, vx = bf(kx_ref), bf(vx_ref)
    kblk = [bf(kp_ref)] + [kc_ref[0, t * b:(t + 1) * b, :].astype(BF16) for t in range(nq)] + [bf(kn_ref)]
    vblk = [bf(vp_ref)] + [vc_ref[0, t * b:(t + 1) * b, :].astype(BF16) for t in range(nq)] + [bf(vn_ref)]
    keys = [jnp.concatenate(kblk[t:t + 3] + [kx], axis=0) for t in range(nq)]
    vals = [jnp.concatenate(vblk[t:t + 3] + [vx], axis=0) for t in range(nq)]
    cols = ATT_GROUP * b
    c = lax.broadcasted_iota(jnp.int32, (b, cols), 0)
    r = lax.broadcasted_iota(jnp.int32, (b, cols), 1) % b
    ok_prev = [(c >= r) & (j > 0)] + [c >= r] * (nq - 1)
    ok_next = [c <= r] * (nq - 1) + [(c <= r) & (j < last)]

    def band(s, t):
        return jnp.concatenate([jnp.where(ok_prev[t], s[:b], NEG), s[b:2 * b],
                                jnp.where(ok_next[t], s[2 * b:3 * b], NEG), s[3 * b:]], axis=0)

    chains = [(t, kv) for t in range(nq) for kv in range(ATT_KV_HEADS)]
    scores = []
    for t, kv in chains:
        q = q_ref[0, t * b:(t + 1) * b, :] * Q_SCALE
        scores.append(band(_attn_scores_t(q, keys[t][:, kv * HEAD_DIM:(kv + 1) * HEAD_DIM], kv), t))
    for (t, kv), s in zip(chains, scores):
        _attn_finish_t(sink_ref, s, vals[t][:, kv * HEAD_DIM:(kv + 1) * HEAD_DIM], g_ref, o_ref, kv, t * b)


def _attn_lat_call(u, kctx, vctx, sink, layer):
    b, l, _ = u.shape
    nb = l // ATT_BLOCK
    past = kctx.shape[2]
    kcol = COL_KA // W_KV
    vcol = COL_VA // W_KV
    nq = ATT_QB
    prev = lambda col: pl.BlockSpec((1, ATT_BLOCK, W_KV), lambda i, j: (i, jnp.maximum(nq * j - 1, 0), col))
    cur = lambda col: pl.BlockSpec((1, nq * ATT_BLOCK, W_KV), lambda i, j: (i, j, col))
    nxt = lambda col: pl.BlockSpec((1, ATT_BLOCK, W_KV),
                                   lambda i, j: (i, jnp.minimum(nq * j + nq, nb - 1), col))
    ctx = pl.BlockSpec((1, None, past, W_KV), lambda i, j: (i, layer, 0, 0))
    return pl.pallas_call(
        _attn_lat_kernel,
        grid=(b, nb // nq),
        in_specs=[pl.BlockSpec(memory_space=pltpu.SMEM),
                  pl.BlockSpec((1, nq * ATT_BLOCK, W_ATT), lambda i, j: (i, j, COL_QA // W_ATT)),
                  prev(kcol), cur(kcol), nxt(kcol), prev(vcol), cur(vcol), nxt(vcol), ctx, ctx,
                  pl.BlockSpec((1, nq * ATT_BLOCK, W_ATT), lambda i, j: (i, j, COL_GA // W_ATT))],
        out_specs=pl.BlockSpec((1, nq * ATT_BLOCK, W_ATT), lambda i, j: (i, j, 0)),
        out_shape=jax.ShapeDtypeStruct((b, l, W_ATT), F32),
        compiler_params=_cparams(("parallel", "parallel")),
        name="attn_lat",
    )(sink, u, u, u, u, u, u, u, kctx, vctx, u)


def _log_sigmoid(x):
    return jnp.minimum(x, 0.0) - jnp.log1p(jnp.exp(-jnp.abs(x)))


_TAB_DMAT = 0
_TAB_QDEC = 2 * RET_CHUNK
_TAB_KDEC = 3 * RET_CHUNK
_TAB_CDEC = 4 * RET_CHUNK
_TAB_ROWS = 4 * RET_CHUNK + 8


def _ret_kernel(q_ref, k_ref, v_ref, g_ref, thl_ref, thb_ref, gn_ref, s0_ref, o_ref, sfin_ref, ob_ref,
                tab_ref, *, nc, cpt, npairs, has_s0):
    grp = pl.program_id(0)
    c = RET_CHUNK
    lane = lax.broadcasted_iota(jnp.int32, (1, LANES), 1)
    lo_head = lane < RET_DIM
    dd = lax.broadcasted_iota(jnp.int32, (LANES, LANES), 0)
    ee = lax.broadcasted_iota(jnp.int32, (LANES, LANES), 1)
    same_head = (dd < RET_DIM) == (ee < RET_DIM)
    lanes_of = lambda p: slice(p * LANES, (p + 1) * LANES)

    @pl.when(pl.program_id(1) == 0)
    def _():
        rowf = lax.broadcasted_iota(jnp.int32, (c, LANES), 0).astype(F32)
        ii = lax.broadcasted_iota(jnp.int32, (c, c), 0)
        jj = lax.broadcasted_iota(jnp.int32, (c, c), 1)
        for p in range(npairs):
            for d in range(2):
                lg_lane = _log_sigmoid(thl_ref[d, :, lanes_of(p)])
                dist = (ii - jj) if d == 0 else (jj - ii)
                for hh in range(2):
                    head = 2 * (grp * npairs + p) + hh
                    lg_h = _log_sigmoid(thb_ref[d, pl.ds(head, 1), :])
                    dm = jnp.where(dist >= 0, jnp.exp(lg_h * jnp.maximum(dist, 0).astype(F32)), 0.0)
                    tab_ref[p, d, _TAB_DMAT + hh * c:_TAB_DMAT + (hh + 1) * c, :] = dm
                if d == 0:
                    q_dec = jnp.exp(lg_lane * (rowf + 1.0))
                    k_dec = jnp.exp(lg_lane * (c - 1.0 - rowf))
                else:
                    q_dec = jnp.exp(lg_lane * (c - rowf))
                    k_dec = jnp.exp(lg_lane * rowf)
                tab_ref[p, d, _TAB_QDEC:_TAB_QDEC + c, :] = q_dec
                tab_ref[p, d, _TAB_KDEC:_TAB_KDEC + c, :] = k_dec
                tab_ref[p, d, _TAB_CDEC:_TAB_CDEC + 8, :] = jnp.broadcast_to(
                    jnp.exp(lg_lane * float(c)), (8, LANES))

    def first_level(p, d, r0):
        qc = q_ref[0, pl.ds(r0, c), lanes_of(p)]
        kc = k_ref[0, pl.ds(r0, c), lanes_of(p)]
        vcb = v_ref[0, pl.ds(r0, c), lanes_of(p)].astype(BF16)
        qs = jnp.concatenate([jnp.where(lo_head, qc, 0.0), jnp.where(lo_head, 0.0, qc)], axis=0)
        sc = lax.dot_general(qs.astype(BF16), kc.astype(BF16), _NT, preferred_element_type=F32)
        sc = sc * tab_ref[p, d, _TAB_DMAT:_TAB_DMAT + 2 * c, :]
        kd = kc * tab_ref[p, d, _TAB_KDEC:_TAB_KDEC + c, :]
        upd = jnp.where(same_head, _dot(kd.T.astype(BF16), vcb), 0.0)
        return qc, vcb, sc.astype(BF16), upd

    def second_level(p, d, lvl1, s):
        qc, vcb, scb, upd = lvl1
        pv = _dot(scb, vcb)
        qd = qc * tab_ref[p, d, _TAB_QDEC:_TAB_QDEC + c, :]
        o = _dot(qd.astype(BF16), s.astype(BF16)) + jnp.where(lo_head, pv[:c], pv[c:])
        return o, tab_ref[p, d, _TAB_CDEC:_TAB_CDEC + 1, :] * s + upd

    def init_state(p, d):
        if not has_s0:
            return jnp.zeros((LANES, LANES), F32)
        z = jnp.zeros((RET_DIM, RET_DIM), F32)
        return jnp.concatenate([jnp.concatenate([s0_ref[0, d, 2 * p], z], axis=1),
                                jnp.concatenate([z, s0_ref[0, d, 2 * p + 1]], axis=1)], axis=0)

    units = [(p, d) for p in range(npairs) for d in range(2)]

    def scan_body(n, states):
        def row0(d, j):
            idx = n * cpt + j
            return pl.multiple_of((idx if d == 0 else nc - 1 - idx) * c, c)

        lvl = {(p, d, j): first_level(p, d, row0(d, j)) for j in range(cpt) for p, d in units}
        states = list(states)
        for j in range(cpt):
            for ui, (p, d) in enumerate(units):
                o, states[ui] = second_level(p, d, lvl[(p, d, j)], states[ui])
                if d == 0:
                    o_ref[0, pl.ds(row0(d, j), c), lanes_of(p)] = o
                else:
                    ob_ref[pl.ds(row0(d, j), c), lanes_of(p)] = o
        return tuple(states)

    assert nc % cpt == 0
    states = lax.fori_loop(0, nc // cpt, scan_body, tuple(init_state(p, d) for p, d in units))
    for ui, (p, d) in enumerate(units):
        sfin_ref[0, d, 2 * p] = states[ui][:RET_DIM, :RET_DIM]
        sfin_ref[0, d, 2 * p + 1] = states[ui][RET_DIM:, RET_DIM:]

    head_mean = jnp.where(same_head, 1.0 / RET_DIM, 0.0).astype(BF16)

    def norm_body(n, carry):
        r0 = pl.multiple_of(n * c, c)
        for p in range(npairs):
            o = o_ref[0, pl.ds(r0, c), lanes_of(p)] + ob_ref[pl.ds(r0, c), lanes_of(p)]
            o2_hi, o2_lo = _split(o * o)
            ms = _dot(o2_hi, head_mean) + _dot(o2_lo, head_mean)
            o_ref[0, pl.ds(r0, c), lanes_of(p)] = (o * lax.rsqrt(ms + EPS) * gn_ref[:, lanes_of(p)]
                                                   * g_ref[0, pl.ds(r0, c), lanes_of(p)])
        return carry

    lax.fori_loop(0, nc, norm_body, 0, unroll=min(nc, 4))


def _ret_call(u, theta, gn, s0bd, layer=0):
    b, l, _ = u.shape
    nc = l // RET_CHUNK
    has_s0 = s0bd is not None
    cpt = 8 if nc >= 8 else nc
    npairs = max(1, 8 // cpt)
    ngrp = RET_HEADS // 2 // npairs
    w = npairs * LANES
    st_block = (2, 2 * npairs, RET_DIM, RET_DIM)
    if not has_s0:
        s0bd = jnp.zeros((1,) + st_block, F32)
        s0_spec = pl.BlockSpec((1,) + st_block, lambda g, i: (0, 0, 0, 0, 0))
    else:
        s0_spec = pl.BlockSpec((1, None) + st_block, lambda g, i: (i, layer, 0, g, 0, 0))
    th_lane = jnp.repeat(theta, RET_DIM, axis=1).reshape(2, 1, W_RET)
    th_bcast = jnp.broadcast_to(theta[:, :, None], (2, RET_HEADS, LANES))
    col = lambda c0: pl.BlockSpec((1, l, w), lambda g, i: (i, 0, c0 // w + g))
    o, sfin = pl.pallas_call(
        functools.partial(_ret_kernel, nc=nc, cpt=cpt, npairs=npairs, has_s0=has_s0),
        grid=(ngrp, b),
        in_specs=[col(COL_QR), col(COL_KR), col(COL_VR), col(COL_GR),
                  pl.BlockSpec((2, 1, w), lambda g, i: (0, 0, g)),
                  pl.BlockSpec((2, RET_HEADS, LANES), lambda g, i: (0, 0, 0)),
                  pl.BlockSpec((1, w), lambda g, i: (0, g)),
                  s0_spec],
        out_specs=[pl.BlockSpec((1, l, w), lambda g, i: (i, 0, g)),
                   pl.BlockSpec((1,) + st_block, lambda g, i: (i, 0, g, 0, 0))],
        out_shape=[jax.ShapeDtypeStruct((b, l, W_RET), F32),
                   jax.ShapeDtypeStruct((b, 2, RET_HEADS, RET_DIM, RET_DIM), F32)],
        scratch_shapes=[pltpu.VMEM((l, w), F32), pltpu.VMEM((npairs, 2, _TAB_ROWS, LANES), F32)],
        compiler_params=_cparams(("arbitrary", "arbitrary")),
        name="retention_s0" if has_s0 else "retention",
    )(u, u, u, u, th_lane, th_bcast, gn.reshape(1, W_RET), s0bd)
    return o, sfin


def _filter_positions(l):
    f32 = np.float32
    t = np.linspace(0.0, 1.0, l, dtype=f32)[:, None]
    w = (f32(2.0 * math.pi) * np.arange(l, dtype=f32)[:, None] / f32(l)).astype(f32)
    f = np.linspace(1e-4, HY_BANDS - 1, HY_BANDS, dtype=f32)[None, :]
    z = np.concatenate([t, np.cos(f * w), -np.sin(f * w)], axis=-1).astype(f32)
    z = np.pad(z, ((0, 0), (0, 32 - HY_POS_FEAT)))
    return jnp.asarray(np.concatenate([z, z[:1], z[1:][::-1]], axis=0))


def _hyena_deltas():
    max_decay = math.log(HY_DECAY_TARGET) / HY_FAST_PCT
    min_decay = math.log(HY_DECAY_TARGET) / HY_SLOW_PCT
    return jnp.asarray(np.abs(np.linspace(min_decay, max_decay, W_HY, dtype=np.float32))[None, :])


def _filter_hidden(z_ref, w1_ref, b1_ref, fr_ref):
    pre = jnp.dot(z_ref[...], w1_ref[0], precision=HIGHEST, preferred_element_type=F32) + b1_ref[0]
    return jnp.sin(fr_ref[0] * pre)


def _filter_raw(hid, w2f, w2b, tp, dl, row0, l):
    win = jnp.exp(-tp * dl)
    row = row0 + lax.broadcasted_iota(jnp.int32, win.shape, 0)
    hf = jnp.dot(hid, w2f, precision=HIGHEST, preferred_element_type=F32) * win
    hb = jnp.dot(hid, w2b, precision=HIGHEST, preferred_element_type=F32) * win
    hf = jnp.where(row < l, hf, 0.0)
    hb = jnp.where((row > l) | (row == 0), hb, 0.0)
    return hf + hb, jnp.sum(jnp.abs(hf) + jnp.abs(hb), axis=0, keepdims=True)


def _with_skip(g, skip):
    row = lax.broadcasted_iota(jnp.int32, g.shape, 0)
    return g + jnp.where(row == 0, skip, 0.0)


def _filt_ctx_kernel(z_ref, w1_ref, b1_ref, fr_ref, w2_ref, dl_ref, sk_ref, fh_ref, fl_ref, g_ref):
    hid = _filter_hidden(z_ref, w1_ref, b1_ref, fr_ref)
    tp = z_ref[:, 0:1]
    for o in range(2):
        w2f = w2_ref[0, :, (2 * o) * W_HY:(2 * o + 1) * W_HY]
        w2b = w2_ref[0, :, (2 * o + 1) * W_HY:(2 * o + 2) * W_HY]
        raw, nrm = _filter_raw(hid, w2f, w2b, tp, dl_ref[...], 0, z_ref.shape[0] // 2)
        g = _with_skip(raw / nrm, sk_ref[0, pl.ds(o, 1), :])
        g_ref[0, o] = _dot3c(fh_ref[...], fl_ref[...], g)


def _ctx_dft_tables(l):
    n = 2 * l
    k = np.arange(n)[:, None]
    t = np.arange(l)[None, :]
    ang = 2.0 * np.pi * k * t / n
    c, s = np.cos(ang), np.sin(ang)
    fwd = np.block([[c, s], [-s, c]])
    inv = np.block([[c.T, -s.T], [s.T, c.T]])
    n_all = np.arange(n)[None, :]
    angg = 2.0 * np.pi * k * n_all / n
    filt = np.concatenate([np.cos(angg), -np.sin(angg)], axis=0) / n
    return _split_np(fwd), _split_np(inv), _split_np(filt)


def _filt_ctx_call(l, w1, b1, freq, w2, skip):
    n = 2 * l
    z_ext = _filter_positions(l)
    _, _, (fh, fl) = _ctx_dft_tables(l)
    w1p = jnp.pad(w1, ((0, 0), (0, 32 - HY_POS_FEAT), (0, 0)))
    lay = lambda *shape: pl.BlockSpec((1,) + shape, lambda d: (d,) + (0,) * len(shape))
    full = lambda a: pl.BlockSpec(a.shape, lambda d: (0,) * a.ndim)
    dl = _hyena_deltas()
    return pl.pallas_call(
        _filt_ctx_kernel,
        grid=(DEPTH,),
        in_specs=[full(z_ext), lay(32, HY_FILT_HID), lay(1, HY_FILT_HID), lay(1, HY_FILT_HID),
                  lay(HY_FILT_HID, 4 * W_HY), full(dl), lay(2, W_HY), full(fh), full(fl)],
        out_specs=pl.BlockSpec((1, 2, 2 * n, W_HY), lambda d: (d, 0, 0, 0)),
        out_shape=jax.ShapeDtypeStruct((DEPTH, 2, 2 * n, W_HY), F32),
        compiler_params=_cparams(("arbitrary",)),
        name="hyena_filter_ctx",
    )(z_ext, w1p, b1.reshape(DEPTH, 1, -1), freq.reshape(DEPTH, 1, -1), w2, dl, skip, fh, fl)


def _lat_dft_tables():
    ka = np.arange(NA)[:, None]
    b = np.arange(NB)[:, None, None]
    kb = np.arange(NB)[:, None]
    bb = np.arange(NB)[None, :]
    a_half = np.arange(NA // 2)[None, :]
    a_full = np.arange(NA)[None, :]
    phi = 2.0 * np.pi * (ka * a_half / NA + b * ka / LAT_N)
    c, s = np.cos(phi), np.sin(phi)
    a_fwd = np.concatenate([c, s], axis=2)
    ct, st = np.swapaxes(c, 1, 2), np.swapaxes(s, 1, 2)
    a_inv = np.concatenate([ct, st], axis=2)
    phig = 2.0 * np.pi * (ka * a_full / NA + b * ka / LAT_N)
    a_flt = np.concatenate([np.cos(phig), -np.sin(phig)], axis=1) / LAT_N
    ang = 2.0 * np.pi * kb * bb / NB
    c2, s2 = np.cos(ang), np.sin(ang)
    f_fwd = np.block([[c2, s2], [-s2, c2]])
    f_inv = np.block([[c2, -s2], [s2, c2]])
    return (_split_np(a_fwd), _split_np(a_inv), _split_np(a_flt), _split_np(f_fwd), _split_np(f_inv))


def _stage_b_rows(ka):
    re = pl.ds(ka, NB, stride=Y_PITCH)
    im = pl.ds(NA + ka, NB, stride=Y_PITCH)
    return re, im


def _filt_lat_kernel(z_ref, w1_ref, b1_ref, fr_ref, w2f_ref, w2b_ref, dl_ref, sk_ref, ah_ref, f2h_ref,
                     g_ref, hid_ref, gt_ref, y_ref):
    step = pl.program_id(1)
    rch = 1024
    nch = LAT_N // rch
    rows_of = lambda i: pl.ds(pl.multiple_of(i * rch, rch), rch)

    @pl.when(step == 0)
    def _():
        def hid_chunk(i, carry):
            r = rows_of(i)
            pre = jnp.dot(z_ref[r, :], w1_ref[0], precision=HIGHEST, preferred_element_type=F32)
            hid_ref[r, :] = jnp.sin(fr_ref[0] * (pre + b1_ref[0]))
            return carry

        lax.fori_loop(0, nch, hid_chunk, 0)

    w2f_hl = _split(w2f_ref[0])
    w2b_hl = _split(w2b_ref[0])

    def raw_chunk(w2_hl, i, nrm):
        r = rows_of(i)
        hh, hl = _split(hid_ref[r, :])
        h = _dot(hh, w2_hl[0]) + _dot(hl, w2_hl[0]) + _dot(hh, w2_hl[1])
        h = h * jnp.exp(-z_ref[r, 0:1] * dl_ref[...])
        row = i * rch + lax.broadcasted_iota(jnp.int32, h.shape, 0)
        h = jnp.where(row == LAT_L, 0.0, h)
        for s in range(rch // NB):
            slab = pl.ds(pl.multiple_of((i * (rch // NB) + s) * X_PITCH, 8), NB)
            gt_ref[slab, :] = h[s * NB:(s + 1) * NB]
        return nrm + jnp.sum(jnp.abs(h), axis=0, keepdims=True)

    nrm = lax.fori_loop(0, nch // 2, functools.partial(raw_chunk, w2f_hl), jnp.zeros((1, LANES), F32))
    nrm = lax.fori_loop(nch // 2, nch, functools.partial(raw_chunk, w2b_hl), nrm)
    hh, hl = _split(hid_ref[0:8, :])
    hb0 = _dot(hh, w2b_hl[0]) + _dot(hl, w2b_hl[0]) + _dot(hh, w2b_hl[1])
    hb0 = hb0 * jnp.exp(-z_ref[0:8, 0:1] * dl_ref[...])
    hb0 = jnp.where(lax.broadcasted_iota(jnp.int32, hb0.shape, 0) == 0, hb0, 0.0)
    gt_ref[0:8, :] = gt_ref[0:8, :] + hb0
    nrm = nrm + jnp.sum(jnp.abs(hb0), axis=0, keepdims=True)

    def norm_slab(a, carry):
        slab = pl.ds(pl.multiple_of(a * X_PITCH, 8), NB)
        gt_ref[slab, :] = gt_ref[slab, :] / nrm
        return carry

    lax.fori_loop(0, NA, norm_slab, 0, unroll=8)
    order = step // (W_HY // LANES)
    gt_ref[0:8, :] = _with_skip(gt_ref[0:8, :], sk_ref[0, pl.ds(order, 1), :])

    def stage_a(b, carry):
        rows = gt_ref[pl.ds(b, NA, stride=X_PITCH), :]
        y_ref[pl.ds(pl.multiple_of(b * Y_PITCH, 8), 2 * NA), :] = _dot1c(ah_ref[b], rows)
        return carry

    lax.fori_loop(0, NB, stage_a, 0, unroll=8)

    def stage_b(j, carry):
        ka = 2 * j
        re, im = _stage_b_rows(ka)
        re1, im1 = _stage_b_rows(ka + 1)
        z = jnp.concatenate([jnp.concatenate([y_ref[re, :], y_ref[im, :]], axis=0),
                             jnp.concatenate([y_ref[re1, :], y_ref[im1, :]], axis=0)], axis=1)
        x = _dot1c(f2h_ref[...], z)
        g0 = pl.multiple_of(ka * 2 * NB, 2 * NB)
        g_ref[0, 0, pl.ds(g0, 2 * NB), :] = x[:, :LANES]
        g_ref[0, 0, pl.ds(g0 + 2 * NB, 2 * NB), :] = x[:, LANES:]
        return carry

    lax.fori_loop(0, NA // 2, stage_b, 0, unroll=4)


def _filt_lat_call(w1, b1, freq, w2, skip):
    z_ext = _filter_positions(LAT_L)
    _, _, (ah, _), (f2h, _), _ = _lat_dft_tables()
    w1p = jnp.pad(w1, ((0, 0), (0, 32 - HY_POS_FEAT), (0, 0)))
    nct = W_HY // LANES
    one = pl.Buffered(1)
    lay = lambda *shape: pl.BlockSpec((1,) + shape, lambda d, s: (d,) + (0,) * len(shape))
    full = lambda a: pl.BlockSpec(a.shape, lambda d, s: (0,) * a.ndim, pipeline_mode=one)
    dl = _hyena_deltas()
    return pl.pallas_call(
        _filt_lat_kernel,
        grid=(DEPTH, 2 * nct),
        in_specs=[full(z_ext), lay(32, HY_FILT_HID), lay(1, HY_FILT_HID), lay(1, HY_FILT_HID),
                  pl.BlockSpec((1, HY_FILT_HID, LANES), lambda d, s: (d, 0, (s // nct) * 2 * nct + s % nct)),
                  pl.BlockSpec((1, HY_FILT_HID, LANES),
                               lambda d, s: (d, 0, (s // nct) * 2 * nct + nct + s % nct)),
                  pl.BlockSpec((1, LANES), lambda d, s: (0, s % nct)),
                  pl.BlockSpec((1, 2, LANES), lambda d, s: (d, 0, s % nct)),
                  full(ah), full(f2h)],
        out_specs=pl.BlockSpec((1, 1, NA * 2 * NB, LANES), lambda d, s: (d, s, 0, 0)),
        out_shape=jax.ShapeDtypeStruct((DEPTH, 2 * nct, NA * 2 * NB, LANES), F32),
        scratch_shapes=[pltpu.VMEM((LAT_N, HY_FILT_HID), F32), pltpu.VMEM((NA * X_PITCH, LANES), F32),
                        pltpu.VMEM((NB * Y_PITCH, LANES), F32)],
        compiler_params=_cparams(("arbitrary", "arbitrary")),
        name="hyena_filter_lat",
    )(z_ext, w1p, b1.reshape(DEPTH, 1, -1), freq.reshape(DEPTH, 1, -1), w2, w2, dl, skip, ah, f2h)


def _short_conv_rows(ref, bi, r0, rows, first, last, w):
    total = ref.shape[1]
    cur = ref[bi, pl.ds(r0, rows), :]
    before = ref[bi, pl.ds(jnp.maximum(r0 - 1, 0), 1), :]
    after = ref[bi, pl.ds(jnp.minimum(r0 + rows, total - 1), 1), :]
    before = jnp.where(first, 0.0, before)
    after = jnp.where(last, 0.0, after)
    rid = lax.broadcasted_iota(jnp.int32, cur.shape, 0)
    prev = jnp.where(rid == 0, before, pltpu.roll(cur, 1, 0))
    nxt = jnp.where(rid == rows - 1, after, pltpu.roll(cur, rows - 1, 0))
    return prev * w[0:1] + cur * w[1:2] + nxt * w[2:3]


def _short_conv_interior(ref, bi, r0, rows, w):
    prev = ref[bi, pl.ds(r0 - 1, rows), :]
    cur = ref[bi, pl.ds(r0, rows), :]
    nxt = ref[bi, pl.ds(r0 + 1, rows), :]
    return prev * w[0:1] + cur * w[1:2] + nxt * w[2:3]


def _cmul(xr, xi, gr, gi):
    return xr * gr - xi * gi, xr * gi + xi * gr


def _hy_ctx_kernel(v_ref, x1_ref, x2_ref, cw_ref, g_ref, fh_ref, ih_ref, o_ref):
    l = v_ref.shape[1]
    n = 2 * l

    def sc(ref, bi, grp):
        w = cw_ref[:, grp * W_HY:(grp + 1) * W_HY]
        return _short_conv_rows(ref, bi, 0, l, True, True, w)

    def conv(zr, zi, order):
        x = _dot1c(fh_ref[...], jnp.concatenate([zr, zi], axis=0))
        pr, pi = _cmul(x[:n], x[n:], g_ref[order, :n], g_ref[order, n:])
        y = _dot1c(ih_ref[...], jnp.concatenate([pr, pi], axis=0))
        return y[:l], y[l:]

    yr, yi = conv(sc(v_ref, 0, 0), sc(v_ref, 1, 0), 0)
    yr, yi = conv(sc(x1_ref, 0, 1) * yr, sc(x1_ref, 1, 1) * yi, 1)
    o_ref[0] = sc(x2_ref, 0, 2) * yr
    o_ref[1] = sc(x2_ref, 1, 2) * yi


def _hy_ctx_call(u, conv_w, g_spec, layer):
    b, l, _ = u.shape
    (fh, _), (ih, _), _ = _ctx_dft_tables(l)
    grp = lambda g: pl.BlockSpec((2, l, W_HY), lambda i: (i, 0, g))
    full = lambda a: pl.BlockSpec(a.shape, lambda i: (0,) * a.ndim)
    return pl.pallas_call(
        _hy_ctx_kernel,
        grid=(b // 2,),
        in_specs=[grp(0), grp(1), grp(2), full(conv_w),
                  pl.BlockSpec((None,) + g_spec.shape[1:], lambda i: (layer, 0, 0, 0)), full(fh), full(ih)],
        out_specs=pl.BlockSpec((2, l, W_HY), lambda i: (i, 0, 0)),
        out_shape=jax.ShapeDtypeStruct((b, l, W_HY), F32),
        compiler_params=_cparams(("parallel",)),
        name="hyena_ctx",
    )(u, u, u, conv_w, g_spec, fh, ih)


def _hy_lat_kernel(z_ref, m_ref, cw_ref, g_ref, af_ref, ai_ref, f2_ref, f3_ref, o_ref, xr_scr, xi_scr,
                   y_scr, *, conv_in):
    x_scr = (xr_scr, xi_scr)
    na_half = NA // 2
    w_in = cw_ref[0] if conv_in else None
    w_mul = cw_ref[1]

    def conv_slab(ref, bi, a, w):
        if isinstance(a, int):
            return _short_conv_rows(ref, bi, a * NB, NB, a == 0, a == na_half - 1, w)
        return _short_conv_interior(ref, bi, pl.multiple_of(a * NB, NB), NB, w)

    def edges_then_interior(body):
        body(0, 0)
        body(na_half - 1, 0)
        lax.fori_loop(1, na_half - 1, body, 0, unroll=2)

    def load_in(a, carry):
        for bi in range(2):
            if conv_in:
                val = conv_slab(z_ref, bi, a, w_in)
            else:
                val = z_ref[bi, pl.ds(pl.multiple_of(a * NB, NB), NB), :]
            x_scr[bi][pl.ds(pl.multiple_of(a * X_PITCH, 8), NB), :] = val
        return carry

    edges_then_interior(load_in)

    def stage_a(b, carry):
        zr = xr_scr[pl.ds(b, na_half, stride=X_PITCH), :]
        zi = xi_scr[pl.ds(b, na_half, stride=X_PITCH), :]
        rhs = jnp.concatenate([jnp.concatenate([zr, zi], axis=0), jnp.concatenate([zi, -zr], axis=0)],
                              axis=1)
        y = _dot1c(af_ref[b], rhs)
        r0 = pl.multiple_of(b * Y_PITCH, 8)
        y_scr[pl.ds(r0, NA), :] = y[:, :LANES]
        y_scr[pl.ds(r0 + NA, NA), :] = y[:, LANES:]
        return carry

    lax.fori_loop(0, NB, stage_a, 0, unroll=32)

    def spectrum_product(j):
        ka = 2 * j
        re, im = _stage_b_rows(ka)
        re1, im1 = _stage_b_rows(ka + 1)
        z = jnp.concatenate([jnp.concatenate([y_scr[re, :], y_scr[im, :]], axis=0),
                             jnp.concatenate([y_scr[re1, :], y_scr[im1, :]], axis=0)], axis=1)
        x = _dot1c(f2_ref[...], z)
        g0 = pl.multiple_of(ka * 2 * NB, 2 * NB)
        gr = jnp.concatenate([g_ref[pl.ds(g0, NB), :], g_ref[pl.ds(g0 + 2 * NB, NB), :]], axis=1)
        gi = jnp.concatenate([g_ref[pl.ds(g0 + NB, NB), :], g_ref[pl.ds(g0 + 3 * NB, NB), :]], axis=1)
        pr, pi = _cmul(x[:NB], x[NB:], gr, gi)
        return jnp.concatenate([pr, pi], axis=0).astype(BF16)

    def inverse_b(j, prod):
        ka = 2 * j
        re, im = _stage_b_rows(ka)
        re1, im1 = _stage_b_rows(ka + 1)
        u = _dot(f3_ref[...], prod)
        y_scr[re, :] = u[:NB, :LANES]
        y_scr[im, :] = u[NB:, :LANES]
        y_scr[re1, :] = u[:NB, LANES:]
        y_scr[im1, :] = u[NB:, LANES:]

    def stage_b(j, prod):
        nxt = spectrum_product(j + 1)
        inverse_b(j, prod)
        return nxt

    last = lax.fori_loop(0, NA // 2 - 1, stage_b, spectrum_product(0), unroll=4)
    inverse_b(NA // 2 - 1, last)

    def stage_c(b, carry):
        r0 = pl.multiple_of(b * Y_PITCH, 8)
        ur = y_scr[pl.ds(r0, NA), :]
        ui = y_scr[pl.ds(r0 + NA, NA), :]
        rhs = jnp.concatenate([jnp.concatenate([ur, -ui], axis=0), jnp.concatenate([ui, ur], axis=0)],
                              axis=1)
        y = _dot1c(ai_ref[b], rhs)
        xr_scr[pl.ds(b, na_half, stride=X_PITCH), :] = y[:, :LANES]
        xi_scr[pl.ds(b, na_half, stride=X_PITCH), :] = y[:, LANES:]
        return carry

    lax.fori_loop(0, NB, stage_c, 0, unroll=32)

    def store_out(a, carry):
        for bi in range(2):
            mul = conv_slab(m_ref, bi, a, w_mul)
            o_ref[bi, pl.ds(pl.multiple_of(a * NB, NB), NB), :] = (
                x_scr[bi][pl.ds(pl.multiple_of(a * X_PITCH, 8), NB), :] * mul)
        return carry

    edges_then_interior(store_out)


def _hy_lat_call(src, src_col, u, mul_col, conv_w2, g_spec, layer, order, *, conv_in):
    b, l, _ = u.shape
    nct = W_HY // LANES
    (af, _), (ai, _), _, (f2, _), (f3, _) = _lat_dft_tables()
    one = pl.Buffered(1)
    blk = lambda col: pl.BlockSpec((2, l, LANES), lambda c, p: (p, 0, col + c))
    const = lambda a: pl.BlockSpec(a.shape, lambda c, p: (0,) * a.ndim, pipeline_mode=one)
    return pl.pallas_call(
        functools.partial(_hy_lat_kernel, conv_in=conv_in),
        grid=(nct, b // 2),
        in_specs=[blk(src_col), blk(mul_col),
                  pl.BlockSpec((2, 3, LANES), lambda c, p: (0, 0, c)),
                  pl.BlockSpec((None, None, NA * 2 * NB, LANES), lambda c, p: (layer, order * nct + c, 0, 0),
                               pipeline_mode=one),
                  const(af), const(ai), const(f2), const(f3)],
        out_specs=pl.BlockSpec((2, l, LANES), lambda c, p: (p, 0, c)),
        out_shape=jax.ShapeDtypeStruct((b, l, W_HY), F32),
        scratch_shapes=[pltpu.VMEM(((NA // 2) * X_PITCH, LANES), F32),
                        pltpu.VMEM(((NA // 2) * X_PITCH, LANES), F32),
                        pltpu.VMEM((NB * Y_PITCH, LANES), F32)],
        compiler_params=_cparams(("arbitrary", "arbitrary")),
        name="hyena_lat_conv_in" if conv_in else "hyena_lat",
    )(src, u, conv_w2, g_spec, af, ai, f2, f3)


def _rope_tables(l):
    f32 = np.float32
    rows = l // GRID_W
    row = np.repeat(np.arange(rows), GRID_W).astype(f32)
    col = np.tile(np.arange(GRID_W), rows).astype(f32)
    quarter = HEAD_DIM // 4
    inv = np.power(f32(ROPE_BASE), -np.arange(quarter, dtype=f32) / f32(quarter)).astype(f32)
    ang = np.concatenate([row[:, None] * inv, col[:, None] * inv], axis=-1).astype(f32)
    cos, sin = np.cos(ang), np.sin(ang)
    q = quarter
    cos_h = np.concatenate([cos[:, :q], cos[:, :q], cos[:, q:], cos[:, q:]], axis=-1)
    sin_h = np.concatenate([-sin[:, :q], sin[:, :q], -sin[:, q:], sin[:, q:]], axis=-1)
    return jnp.asarray(np.tile(cos_h, (1, 2))), jnp.asarray(np.tile(sin_h, (1, 2)))


def kernel(x_prompt, x_sample, c, cache_k, cache_v, state_ret, c_ctx, norm_w, w_mod, b_mod, w_in, hy_conv,
           hy_filt_w1, hy_filt_b1, hy_filt_freq, hy_filt_w2, hy_skip, attn_sink, ret_theta, ret_gn,
           w_branch_a, w_branch_b, w_branch_c, w_merge, b_merge, w_out, final_norm_w):
    d = D_MODEL
    bc, lc, _ = x_prompt.shape
    bl, ll, _ = x_sample.shape
    assert ll == LAT_L and bc % 2 == 0 and bl % 2 == 0
    past = cache_k.shape[2]

    cond = jnp.zeros((16, d), F32).at[:bl].set(c).at[bl].set(c_ctx)
    mod = _mod_call(cond, w_mod, b_mod)

    g_ctx = _filt_ctx_call(lc, hy_filt_w1, hy_filt_b1, hy_filt_freq, hy_filt_w2, hy_skip)
    g_lat = _filt_lat_call(hy_filt_w1, hy_filt_b1, hy_filt_freq, hy_filt_w2, hy_skip)

    cos_t, sin_t = _rope_tables(ll)
    w_in_b = w_in.astype(BF16)
    wm_b = w_merge.astype(BF16)
    wa_b = w_branch_a.astype(BF16)
    wb_b = w_branch_b.astype(BF16)
    wc_b = w_branch_c.astype(BF16)
    wo_b = w_out.astype(BF16)
    fnw = final_norm_w.reshape(1, d)
    k_ctx = cache_k.reshape(bl, DEPTH, past, W_KV)
    v_ctx = cache_v.reshape(bl, DEPTH, past, W_KV)
    hy_cols = COL_HY // LANES
    nct = W_HY // LANES

    xp, xs = x_prompt, x_sample
    ks_out, vs_out, ss_out = [], [], []
    for l in range(DEPTH):
        final = l == DEPTH - 1
        nw = norm_w[l].reshape(1, d)
        bm = b_merge[l].reshape(1, -1)
        shift, scale, gate = (mod[l, :, i * d:(i + 1) * d][:, None, :] for i in range(3))
        conv_w = hy_conv[l]
        cw = lambda g: conv_w[:, g * W_HY:(g + 1) * W_HY]

        sl = slice(bl, bl + 1)
        u = _in_call(xp, shift[sl], scale[sl], nw, w_in_b[l], cos_t, sin_t, rope=False)
        ya = _hy_ctx_call(u, conv_w, g_ctx, l)
        yb = _attn_ctx_call(u, attn_sink[l])
        yc, sfin = _ret_call(u, ret_theta[l], ret_gn[l], None)
        res = _out_call(xp, shift[sl], scale[sl], gate[sl], nw, ya, u, yb, yc, wm_b[l], bm, wa_b[l], wb_b[l],
                        wc_b[l], wo_b[l], fnw, final=final)
        xp = res[0]
        if final:
            y_prompt = res[1]
        ks_out.append(u[:, :, COL_KA:COL_KA + W_KV])
        vs_out.append(u[:, :, COL_VA:COL_VA + W_KV])
        ss_out.append(sfin)

        sl = slice(0, bl)
        u = _in_call(xs, shift[sl], scale[sl], nw, w_in_b[l], cos_t, sin_t, rope=True)
        z1 = _hy_lat_call(u, hy_cols, u, hy_cols + nct, jnp.stack([cw(0), cw(1)]), g_lat, l, 0, conv_in=True)
        ya = _hy_lat_call(z1, 0, u, hy_cols + 2 * nct, jnp.stack([cw(2), cw(2)]), g_lat, l, 1, conv_in=False)
        yb = _attn_lat_call(u, k_ctx, v_ctx, attn_sink[l], l)
        yc, _ = _ret_call(u, ret_theta[l], ret_gn[l], state_ret, l)
        res = _out_call(xs, shift[sl], scale[sl], gate[sl], nw, ya, u, yb, yc, wm_b[l], bm, wa_b[l], wb_b[l],
                        wc_b[l], wo_b[l], fnw, final=final)
        xs = res[0]
        if final:
            y_sample = res[1]

    kv_shape = (bc, DEPTH, lc, ATT_KV_HEADS, HEAD_DIM)
    new_cache_k = jnp.stack(ks_out, axis=1).reshape(kv_shape)
    new_cache_v = jnp.stack(vs_out, axis=1).reshape(kv_shape)
    new_state_ret = jnp.stack(ss_out, axis=1)
    return (y_prompt, y_sample, new_cache_k, new_cache_v, new_state_ret)
```

```python
import functools
import math

import numpy as np
import jax
import jax.numpy as jnp
from jax import lax
from jax.experimental import pallas as pl
from jax.experimental.pallas import tpu as pltpu

F32 = jnp.float32
BF16 = jnp.bfloat16
HIGHEST = lax.Precision.HIGHEST

D_MODEL = 1024
DEPTH = 4
GRID_W = 64
W_HY = 512
HY_BANDS = 8
HY_POS_FEAT = 1 + 2 * HY_BANDS
HY_FILT_HID = 64
HY_DECAY_TARGET = 1e-2
HY_FAST_PCT = 0.3
HY_SLOW_PCT = 1.5
ATT_HEADS = 8
ATT_KV_HEADS = 2
ATT_GROUP = ATT_HEADS // ATT_KV_HEADS
HEAD_DIM = 64
W_ATT = ATT_HEADS * HEAD_DIM
W_KV = ATT_KV_HEADS * HEAD_DIM
ATT_BLOCK = 128
ATT_QB = 8
RET_HEADS = 8
RET_DIM = 64
W_RET = RET_HEADS * RET_DIM
RET_CHUNK = 128
ROPE_BASE = 10000.0
EPS = 1e-6
NEG = -1e30

LANES = 128
MXU_ROWS = 512
VMEM_LIMIT = 58 * 1024 * 1024

IN_DIM = 5376
COL_HY = 0
COL_GH = 1536
COL_QA = 2048
COL_GA = 2560
COL_QR = 3072
COL_KR = 3584
COL_VR = 4096
COL_GR = 4608
COL_KA = 5120
COL_VA = 5248
_W_COL = {COL_HY: 0, COL_HY + 512: 512, COL_HY + 1024: 1024, COL_GH: 1536, COL_QA: 2048, COL_KA: 2560,
          COL_VA: 2688, COL_GA: 2816, COL_QR: 3328, COL_KR: 3840, COL_VR: 4352, COL_GR: 4864}

LAT_L = 4096
LAT_N = 2 * LAT_L
NA = 64
NB = 128
Y_PITCH = 136
X_PITCH = 136


def _cparams(sem):
    return pltpu.CompilerParams(dimension_semantics=sem, vmem_limit_bytes=VMEM_LIMIT)


def _split_np(a):
    a32 = np.asarray(a, np.float32)
    hi = a32.astype(BF16)
    lo = (a32 - hi.astype(np.float32)).astype(BF16)
    return jnp.asarray(hi), jnp.asarray(lo)


def _split(x):
    hi = x.astype(BF16)
    lo = (x - hi.astype(F32)).astype(BF16)
    return hi, lo


def _dot(a, b):
    return jnp.dot(a, b, preferred_element_type=F32)


def _dot3c(chi, clo, x):
    xh, xl = _split(x)
    return _dot(chi, xh) + _dot(clo, xh) + _dot(chi, xl)


def _dot1c(chi, x):
    return _dot(chi, x.astype(BF16))


def _silu(x):
    return x * jax.nn.sigmoid(x)


def _mod_kernel(c_ref, w_ref, b_ref, o_ref):
    s = _silu(c_ref[...])
    o_ref[0] = jnp.dot(s, w_ref[0], precision=HIGHEST, preferred_element_type=F32) + b_ref[0]


def _mod_call(cond, w_mod, b_mod):
    rows, d = cond.shape
    n = w_mod.shape[-1]
    tn = 1024
    return pl.pallas_call(
        _mod_kernel,
        grid=(DEPTH, n // tn),
        in_specs=[pl.BlockSpec((rows, d), lambda l, j: (0, 0)),
                  pl.BlockSpec((1, d, tn), lambda l, j: (l, 0, j)),
                  pl.BlockSpec((1, 1, tn), lambda l, j: (l, 0, j))],
        out_specs=pl.BlockSpec((1, rows, tn), lambda l, j: (l, 0, j)),
        out_shape=jax.ShapeDtypeStruct((DEPTH, rows, n), F32),
        compiler_params=_cparams(("arbitrary", "arbitrary")),
        name="adaln_mod",
    )(cond, w_mod, b_mod.reshape(DEPTH, 1, n))


def _modulated(x, nw, scale, shift):
    ms = jnp.mean(x * x, axis=-1, keepdims=True)
    h = x * lax.rsqrt(ms + EPS) * nw
    return h * (1.0 + scale) + shift


def _rope128(x, cos, sin_signed, first_half):
    up = pltpu.roll(x, LANES - 16, 1)
    dn = pltpu.roll(x, 16, 1)
    return x * cos + jnp.where(first_half, up, dn) * sin_signed


def _rows(ref):
    bt, tm, w = ref.shape
    return ref[...].reshape(bt * tm, w)


def _put(ref, c0, val):
    bt, tm, _ = ref.shape
    ref[:, :, c0:c0 + val.shape[1]] = val.reshape(bt, tm, val.shape[1])


def _in_kernel(x_ref, shift_ref, scale_ref, nw_ref, w_ref, cos_ref, sin_ref, o_ref, *, rope):
    x = _rows(x_ref)
    rows = x.shape[0]
    hb = _modulated(x, nw_ref[...], scale_ref[0], shift_ref[0]).astype(BF16)
    if rope:
        cos = cos_ref[...]
        sin = sin_ref[...]
        lane = lax.broadcasted_iota(jnp.int32, (rows, LANES), 1)
        first_half = (lane % 32) < 16

    def seg(c0, width):
        w0 = _W_COL[c0]
        return _dot(hb, w_ref[:, w0:w0 + width])

    def put_rope(c0, val, mul):
        for i in range(val.shape[1] // LANES):
            piece = val[:, i * LANES:(i + 1) * LANES]
            if rope:
                piece = _rope128(piece, cos, sin, first_half)
            if mul is not None:
                piece = piece * mul
            _put(o_ref, c0 + i * LANES, piece)

    for g in range(3):
        _put(o_ref, COL_HY + g * 512, seg(COL_HY + g * 512, 512))
    _put(o_ref, COL_GH, _silu(seg(COL_GH, 512)))
    put_rope(COL_QA, seg(COL_QA, 512), None)
    _put(o_ref, COL_GA, _silu(seg(COL_GA, 512)))
    put_rope(COL_QR, seg(COL_QR, 512), None)
    put_rope(COL_KR, seg(COL_KR, 512), RET_DIM ** -0.5)
    _put(o_ref, COL_VR, seg(COL_VR, 512))
    _put(o_ref, COL_GR, _silu(seg(COL_GR, 512)))
    put_rope(COL_KA, seg(COL_KA, 128), None)
    _put(o_ref, COL_VA, seg(COL_VA, 128))


def _token_tiling(b, l, per_batch):
    if l >= MXU_ROWS:
        return 1, MXU_ROWS
    bt = 1 if per_batch else min(b, MXU_ROWS // l)
    return bt, l


def _in_call(x, shift, scale, nw, w, cos_t, sin_t, *, rope):
    b, l, d = x.shape
    per_batch = shift.shape[0] > 1
    bt, tm = _token_tiling(b, l, per_batch)
    assert not rope or bt == 1
    mod_map = (lambda i, j: (i, 0, 0)) if per_batch else (lambda i, j: (0, 0, 0))
    return pl.pallas_call(
        functools.partial(_in_kernel, rope=rope),
        grid=(b // bt, l // tm),
        in_specs=[pl.BlockSpec((bt, tm, d), lambda i, j: (i, j, 0)),
                  pl.BlockSpec((1, 1, d), mod_map),
                  pl.BlockSpec((1, 1, d), mod_map),
                  pl.BlockSpec((1, d), lambda i, j: (0, 0)),
                  pl.BlockSpec((d, IN_DIM), lambda i, j: (0, 0), pipeline_mode=pl.Buffered(1)),
                  pl.BlockSpec((tm, LANES), lambda i, j: (j, 0)),
                  pl.BlockSpec((tm, LANES), lambda i, j: (j, 0))],
        out_specs=pl.BlockSpec((bt, tm, IN_DIM), lambda i, j: (i, j, 0)),
        out_shape=jax.ShapeDtypeStruct((b, l, IN_DIM), F32),
        compiler_params=_cparams(("parallel", "parallel")),
        name="in_proj_rope" if rope else "in_proj",
    )(x, shift, scale, nw, w, cos_t, sin_t)


def _out_kernel(x_ref, shift_ref, scale_ref, gate_ref, nw_ref, ya_ref, gh_ref, yb_ref, yc_ref, wm_ref, bm_ref,
                wa_ref, wb_ref, wc_ref, wo_ref, fnw_ref, *out_refs, final):
    x = _rows(x_ref)
    d = x.shape[1]
    hb = _modulated(x, nw_ref[...], scale_ref[0], shift_ref[0]).astype(BF16)
    branches = (_rows(ya_ref) * _rows(gh_ref), _rows(yb_ref), _rows(yc_ref))
    merged = None
    for i, (y, w_ref) in enumerate(zip(branches, (wa_ref, wb_ref, wc_ref))):
        g = jax.nn.sigmoid(_dot(hb, wm_ref[:, i * d:(i + 1) * d]) + bm_ref[:, i * d:(i + 1) * d])
        term = g * _dot(y.astype(BF16), w_ref[...])
        merged = term if merged is None else merged + term
    out = _dot(merged.astype(BF16), wo_ref[...])
    xn = x + gate_ref[0] * out
    _put(out_refs[0], 0, xn)
    if final:
        ms = jnp.mean(xn * xn, axis=-1, keepdims=True)
        _put(out_refs[1], 0, xn * lax.rsqrt(ms + EPS) * fnw_ref[...])


def _out_call(x, shift, scale, gate, nw, ya, u, yb, yc, wm, bm, wa, wb, wc, wo, fnw, *, final):
    b, l, d = x.shape
    per_batch = shift.shape[0] > 1
    bt, tm = _token_tiling(b, l, per_batch)
    mod_map = (lambda i, j: (i, 0, 0)) if per_batch else (lambda i, j: (0, 0, 0))
    tok = lambda w: pl.BlockSpec((bt, tm, w), lambda i, j: (i, j, 0))
    full = lambda a: pl.BlockSpec(a.shape, lambda i, j: (0,) * a.ndim, pipeline_mode=pl.Buffered(1))
    n_out = 2 if final else 1
    res = pl.pallas_call(
        functools.partial(_out_kernel, final=final),
        grid=(b // bt, l // tm),
        in_specs=[tok(d), pl.BlockSpec((1, 1, d), mod_map), pl.BlockSpec((1, 1, d), mod_map),
                  pl.BlockSpec((1, 1, d), mod_map), full(nw), tok(W_HY),
                  pl.BlockSpec((bt, tm, W_HY), lambda i, j: (i, j, COL_GH // W_HY)), tok(W_ATT), tok(W_RET),
                  full(wm), full(bm), full(wa), full(wb), full(wc), full(wo), full(fnw)],
        out_specs=[tok(d)] * n_out,
        out_shape=[jax.ShapeDtypeStruct((b, l, d), F32)] * n_out,
        compiler_params=_cparams(("parallel", "parallel")),
        name="merge_out_final" if final else "merge_out",
    )(x, shift, scale, gate, nw, ya, u, yb, yc, wm, bm, wa, wb, wc, wo, fnw)
    return res


_NT = (((1,), (1,)), ((), ()))


_TN = (((0,), (0,)), ((), ()))
LOG2E = 1.4426950408889634
Q_SCALE = (HEAD_DIM ** -0.5) * LOG2E


def _attn_scores_t(q, kh, kv):
    h0 = kv * ATT_GROUP
    qs = jnp.concatenate([q[:, (h0 + g) * HEAD_DIM:(h0 + g + 1) * HEAD_DIM] for g in range(ATT_GROUP)],
                         axis=0).astype(BF16)
    return lax.dot_general(kh, qs, _NT, preferred_element_type=F32)


def _attn_finish_t(sink_ref, s, vh, g_ref, o_ref, kv, row0=0):
    tk, cols = s.shape
    t = cols // ATT_GROUP
    h0 = kv * ATT_GROUP
    head = lax.broadcasted_iota(jnp.int32, (1, cols), 1) // t
    sink = jnp.full((1, cols), sink_ref[h0], F32)
    for g in range(1, ATT_GROUP):
        sink = jnp.where(head == g, sink_ref[h0 + g], sink)
    sink = sink * LOG2E
    m = jnp.maximum(jnp.max(s, axis=0, keepdims=True), sink)
    p = jnp.exp2(s - m).astype(BF16)
    v_ext = jnp.concatenate([vh, jnp.ones((tk, HEAD_DIM), BF16)], axis=1)
    o_ext = lax.dot_general(v_ext, p, _TN, preferred_element_type=F32)
    denom = o_ext[HEAD_DIM:HEAD_DIM + 1] + jnp.exp2(sink - m)
    o = o_ext[:HEAD_DIM] / denom
    for gp in range(ATT_GROUP // 2):
        pair = jnp.concatenate([o[:, (2 * gp) * t:(2 * gp + 1) * t], o[:, (2 * gp + 1) * t:(2 * gp + 2) * t]],
                               axis=0)
        c0 = (h0 + 2 * gp) * HEAD_DIM
        o_ref[0, row0:row0 + t, c0:c0 + 2 * HEAD_DIM] = pair.T * g_ref[0, row0:row0 + t, c0:c0 + 2 * HEAD_DIM]


def _attn_ctx_kernel(sink_ref, q_ref, k_ref, v_ref, g_ref, o_ref):
    q = q_ref[0] * Q_SCALE
    k = k_ref[0].astype(BF16)
    v = v_ref[0].astype(BF16)
    scores = [_attn_scores_t(q, k[:, kv * HEAD_DIM:(kv + 1) * HEAD_DIM], kv) for kv in range(ATT_KV_HEADS)]
    for kv in range(ATT_KV_HEADS):
        _attn_finish_t(sink_ref, scores[kv], v[:, kv * HEAD_DIM:(kv + 1) * HEAD_DIM], g_ref, o_ref, kv)


def _attn_ctx_call(u, sink):
    b, l, _ = u.shape
    return pl.pallas_call(
        _attn_ctx_kernel,
        grid=(b,),
        in_specs=[pl.BlockSpec(memory_space=pltpu.SMEM),
                  pl.BlockSpec((1, l, W_ATT), lambda i: (i, 0, COL_QA // W_ATT)),
                  pl.BlockSpec((1, l, W_KV), lambda i: (i, 0, COL_KA // W_KV)),
                  pl.BlockSpec((1, l, W_KV), lambda i: (i, 0, COL_VA // W_KV)),
                  pl.BlockSpec((1, l, W_ATT), lambda i: (i, 0, COL_GA // W_ATT))],
        out_specs=pl.BlockSpec((1, l, W_ATT), lambda i: (i, 0, 0)),
        out_shape=jax.ShapeDtypeStruct((b, l, W_ATT), F32),
        compiler_params=_cparams(("parallel",)),
        name="attn_ctx",
    )(sink, u, u, u, u)


def _attn_lat_kernel(sink_ref, q_ref, kp_ref, kc_ref, kn_ref, vp_ref, vc_ref, vn_ref, kx_ref, vx_ref,
                     g_ref, o_ref):
    j = pl.program_id(1)
    last = pl.num_programs(1) - 1
    b = ATT_BLOCK
    nq = ATT_QB
    bf = lambda ref: ref[0].astype(BF16)
    kx, vx = bf(kx_ref), bf(vx_ref)
    kblk = [bf(kp_ref)] + [kc_ref[0, t * b:(t + 1) * b, :].astype(BF16) for t in range(nq)] + [bf(kn_ref)]
    vblk = [bf(vp_ref)] + [vc_ref[0, t * b:(t + 1) * b, :].astype(BF16) for t in range(nq)] + [bf(vn_ref)]
    keys = [jnp.concatenate(kblk[t:t + 3] + [kx], axis=0) for t in range(nq)]
    vals = [jnp.concatenate(vblk[t:t + 3] + [vx], axis=0) for t in range(nq)]
    cols = ATT_GROUP * b
    c = lax.broadcasted_iota(jnp.int32, (b, cols), 0)
    r = lax.broadcasted_iota(jnp.int32, (b, cols), 1) % b
    ok_prev = [(c >= r) & (j > 0)] + [c >= r] * (nq - 1)
    ok_next = [c <= r] * (nq - 1) + [(c <= r) & (j < last)]

    def band(s, t):
        return jnp.concatenate([jnp.where(ok_prev[t], s[:b], NEG), s[b:2 * b],
                                jnp.where(ok_next[t], s[2 * b:3 * b], NEG), s[3 * b:]], axis=0)

    chains = [(t, kv) for t in range(nq) for kv in range(ATT_KV_HEADS)]
    scores = []
    for t, kv in chains:
        q = q_ref[0, t * b:(t + 1) * b, :] * Q_SCALE
        scores.append(band(_attn_scores_t(q, keys[t][:, kv * HEAD_DIM:(kv + 1) * HEAD_DIM], kv), t))
    for (t, kv), s in zip(chains, scores):
        _attn_finish_t(sink_ref, s, vals[t][:, kv * HEAD_DIM:(kv + 1) * HEAD_DIM], g_ref, o_ref, kv, t * b)


def _attn_lat_call(u, kctx, vctx, sink, layer):
    b, l, _ = u.shape
    nb = l // ATT_BLOCK
    past = kctx.shape[2]
    kcol = COL_KA // W_KV
    vcol = COL_VA // W_KV
    nq = ATT_QB
    prev = lambda col: pl.BlockSpec((1, ATT_BLOCK, W_KV), lambda i, j: (i, jnp.maximum(nq * j - 1, 0), col))
    cur = lambda col: pl.BlockSpec((1, nq * ATT_BLOCK, W_KV), lambda i, j: (i, j, col))
    nxt = lambda col: pl.BlockSpec((1, ATT_BLOCK, W_KV),
                                   lambda i, j: (i, jnp.minimum(nq * j + nq, nb - 1), col))
    ctx = pl.BlockSpec((1, None, past, W_KV), lambda i, j: (i, layer, 0, 0))
    return pl.pallas_call(
        _attn_lat_kernel,
        grid=(b, nb // nq),
        in_specs=[pl.BlockSpec(memory_space=pltpu.SMEM),
                  pl.BlockSpec((1, nq * ATT_BLOCK, W_ATT), lambda i, j: (i, j, COL_QA // W_ATT)),
                  prev(kcol), cur(kcol), nxt(kcol), prev(vcol), cur(vcol), nxt(vcol), ctx, ctx,
                  pl.BlockSpec((1, nq * ATT_BLOCK, W_ATT), lambda i, j: (i, j, COL_GA // W_ATT))],
        out_specs=pl.BlockSpec((1, nq * ATT_BLOCK, W_ATT), lambda i, j: (i, j, 0)),
        out_shape=jax.ShapeDtypeStruct((b, l, W_ATT), F32),
        compiler_params=_cparams(("parallel", "parallel")),
        name="attn_lat",
    )(sink, u, u, u, u, u, u, u, kctx, vctx, u)


def _log_sigmoid(x):
    return jnp.minimum(x, 0.0) - jnp.log1p(jnp.exp(-jnp.abs(x)))


_TAB_DMAT = 0
_TAB_QDEC = 2 * RET_CHUNK
_TAB_KDEC = 3 * RET_CHUNK
_TAB_CDEC = 4 * RET_CHUNK
_TAB_ROWS = 4 * RET_CHUNK + 8


def _ret_kernel(q_ref, k_ref, v_ref, g_ref, thl_ref, thb_ref, gn_ref, s0_ref, o_ref, sfin_ref, ob_ref,
                tab_ref, *, nc, cpt, npairs, has_s0):
    grp = pl.program_id(0)
    c = RET_CHUNK
    lane = lax.broadcasted_iota(jnp.int32, (1, LANES), 1)
    lo_head = lane < RET_DIM
    dd = lax.broadcasted_iota(jnp.int32, (LANES, LANES), 0)
    ee = lax.broadcasted_iota(jnp.int32, (LANES, LANES), 1)
    same_head = (dd < RET_DIM) == (ee < RET_DIM)
    lanes_of = lambda p: slice(p * LANES, (p + 1) * LANES)

    @pl.when(pl.program_id(1) == 0)
    def _():
        rowf = lax.broadcasted_iota(jnp.int32, (c, LANES), 0).astype(F32)
        ii = lax.broadcasted_iota(jnp.int32, (c, c), 0)
        jj = lax.broadcasted_iota(jnp.int32, (c, c), 1)
        for p in range(npairs):
            for d in range(2):
                lg_lane = _log_sigmoid(thl_ref[d, :, lanes_of(p)])
                dist = (ii - jj) if d == 0 else (jj - ii)
                for hh in range(2):
                    head = 2 * (grp * npairs + p) + hh
                    lg_h = _log_sigmoid(thb_ref[d, pl.ds(head, 1), :])
                    dm = jnp.where(dist >= 0, jnp.exp(lg_h * jnp.maximum(dist, 0).astype(F32)), 0.0)
                    tab_ref[p, d, _TAB_DMAT + hh * c:_TAB_DMAT + (hh + 1) * c, :] = dm
                if d == 0:
                    q_dec = jnp.exp(lg_lane * (rowf + 1.0))
                    k_dec = jnp.exp(lg_lane * (c - 1.0 - rowf))
                else:
                    q_dec = jnp.exp(lg_lane * (c - rowf))
                    k_dec = jnp.exp(lg_lane * rowf)
                tab_ref[p, d, _TAB_QDEC:_TAB_QDEC + c, :] = q_dec
                tab_ref[p, d, _TAB_KDEC:_TAB_KDEC + c, :] = k_dec
                tab_ref[p, d, _TAB_CDEC:_TAB_CDEC + 8, :] = jnp.broadcast_to(
                    jnp.exp(lg_lane * float(c)), (8, LANES))

    def first_level(p, d, r0):
        qc = q_ref[0, pl.ds(r0, c), lanes_of(p)]
        kc = k_ref[0, pl.ds(r0, c), lanes_of(p)]
        vcb = v_ref[0, pl.ds(r0, c), lanes_of(p)].astype(BF16)
        qs = jnp.concatenate([jnp.where(lo_head, qc, 0.0), jnp.where(lo_head, 0.0, qc)], axis=0)
        sc = lax.dot_general(qs.astype(BF16), kc.astype(BF16), _NT, preferred_element_type=F32)
        sc = sc * tab_ref[p, d, _TAB_DMAT:_TAB_DMAT + 2 * c, :]
        kd = kc * tab_ref[p, d, _TAB_KDEC:_TAB_KDEC + c, :]
        upd = jnp.where(same_head, _dot(kd.T.astype(BF16), vcb), 0.0)
        return qc, vcb, sc.astype(BF16), upd

    def second_level(p, d, lvl1, s):
        qc, vcb, scb, upd = lvl1
        pv = _dot(scb, vcb)
        qd = qc * tab_ref[p, d, _TAB_QDEC:_TAB_QDEC + c, :]
        o = _dot(qd.astype(BF16), s.astype(BF16)) + jnp.where(lo_head, pv[:c], pv[c:])
        return o, tab_ref[p, d, _TAB_CDEC:_TAB_CDEC + 1, :] * s + upd

    def init_state(p, d):
        if not has_s0:
            return jnp.zeros((LANES, LANES), F32)
        z = jnp.zeros((RET_DIM, RET_DIM), F32)
        return jnp.concatenate([jnp.concatenate([s0_ref[0, d, 2 * p], z], axis=1),
                                jnp.concatenate([z, s0_ref[0, d, 2 * p + 1]], axis=1)], axis=0)

    units = [(p, d) for p in range(npairs) for d in range(2)]

    def scan_body(n, states):
        def row0(d, j):
            idx = n * cpt + j
            return pl.multiple_of((idx if d == 0 else nc - 1 - idx) * c, c)

        lvl = {(p, d, j): first_level(p, d, row0(d, j)) for j in range(cpt) for p, d in units}
        states = list(states)
        for j in range(cpt):
            for ui, (p, d) in enumerate(units):
                o, states[ui] = second_level(p, d, lvl[(p, d, j)], states[ui])
                if d == 0:
                    o_ref[0, pl.ds(row0(d, j), c), lanes_of(p)] = o
                else:
                    ob_ref[pl.ds(row0(d, j), c), lanes_of(p)] = o
        return tuple(states)

    assert nc % cpt == 0
    states = lax.fori_loop(0, nc // cpt, scan_body, tuple(init_state(p, d) for p, d in units))
    for ui, (p, d) in enumerate(units):
        sfin_ref[0, d, 2 * p] = states[ui][:RET_DIM, :RET_DIM]
        sfin_ref[0, d, 2 * p + 1] = states[ui][RET_DIM:, RET_DIM:]

    head_mean = jnp.where(same_head, 1.0 / RET_DIM, 0.0).astype(BF16)

    def norm_body(n, carry):
        r0 = pl.multiple_of(n * c, c)
        for p in range(npairs):
            o = o_ref[0, pl.ds(r0, c), lanes_of(p)] + ob_ref[pl.ds(r0, c), lanes_of(p)]
            o2_hi, o2_lo = _split(o * o)
            ms = _dot(o2_hi, head_mean) + _dot(o2_lo, head_mean)
            o_ref[0, pl.ds(r0, c), lanes_of(p)] = (o * lax.rsqrt(ms + EPS) * gn_ref[:, lanes_of(p)]
                                                   * g_ref[0, pl.ds(r0, c), lanes_of(p)])
        return carry

    lax.fori_loop(0, nc, norm_body, 0, unroll=min(nc, 4))


def _ret_call(u, theta, gn, s0bd, layer=0):
    b, l, _ = u.shape
    nc = l // RET_CHUNK
    has_s0 = s0bd is not None
    cpt = 8 if nc >= 8 else nc
    npairs = max(1, 8 // cpt)
    ngrp = RET_HEADS // 2 // npairs
    w = npairs * LANES
    st_block = (2, 2 * npairs, RET_DIM, RET_DIM)
    if not has_s0:
        s0bd = jnp.zeros((1,) + st_block, F32)
        s0_spec = pl.BlockSpec((1,) + st_block, lambda g, i: (0, 0, 0, 0, 0))
    else:
        s0_spec = pl.BlockSpec((1, None) + st_block, lambda g, i: (i, layer, 0, g, 0, 0))
    th_lane = jnp.repeat(theta, RET_DIM, axis=1).reshape(2, 1, W_RET)
    th_bcast = jnp.broadcast_to(theta[:, :, None], (2, RET_HEADS, LANES))
    col = lambda c0: pl.BlockSpec((1, l, w), lambda g, i: (i, 0, c0 // w + g))
    o, sfin = pl.pallas_call(
        functools.partial(_ret_kernel, nc=nc, cpt=cpt, npairs=npairs, has_s0=has_s0),
        grid=(ngrp, b),
        in_specs=[col(COL_QR), col(COL_KR), col(COL_VR), col(COL_GR),
                  pl.BlockSpec((2, 1, w), lambda g, i: (0, 0, g)),
                  pl.BlockSpec((2, RET_HEADS, LANES), lambda g, i: (0, 0, 0)),
                  pl.BlockSpec((1, w), lambda g, i: (0, g)),
                  s0_spec],
        out_specs=[pl.BlockSpec((1, l, w), lambda g, i: (i, 0, g)),
                   pl.BlockSpec((1,) + st_block, lambda g, i: (i, 0, g, 0, 0))],
        out_shape=[jax.ShapeDtypeStruct((b, l, W_RET), F32),
                   jax.ShapeDtypeStruct((b, 2, RET_HEADS, RET_DIM, RET_DIM), F32)],
        scratch_shapes=[pltpu.VMEM((l, w), F32), pltpu.VMEM((npairs, 2, _TAB_ROWS, LANES), F32)],
        compiler_params=_cparams(("arbitrary", "arbitrary")),
        name="retention_s0" if has_s0 else "retention",
    )(u, u, u, u, th_lane, th_bcast, gn.reshape(1, W_RET), s0bd)
    return o, sfin


def _filter_positions(l):
    f32 = np.float32
    t = np.linspace(0.0, 1.0, l, dtype=f32)[:, None]
    w = (f32(2.0 * math.pi) * np.arange(l, dtype=f32)[:, None] / f32(l)).astype(f32)
    f = np.linspace(1e-4, HY_BANDS - 1, HY_BANDS, dtype=f32)[None, :]
    z = np.concatenate([t, np.cos(f * w), -np.sin(f * w)], axis=-1).astype(f32)
    z = np.pad(z, ((0, 0), (0, 32 - HY_POS_FEAT)))
    return jnp.asarray(np.concatenate([z, z[:1], z[1:][::-1]], axis=0))


def _hyena_deltas():
    max_decay = math.log(HY_DECAY_TARGET) / HY_FAST_PCT
    min_decay = math.log(HY_DECAY_TARGET) / HY_SLOW_PCT
    return jnp.asarray(np.abs(np.linspace(min_decay, max_decay, W_HY, dtype=np.float32))[None, :])


def _filter_hidden(z_ref, w1_ref, b1_ref, fr_ref):
    pre = jnp.dot(z_ref[...], w1_ref[0], precision=HIGHEST, preferred_element_type=F32) + b1_ref[0]
    return jnp.sin(fr_ref[0] * pre)


def _filter_raw(hid, w2f, w2b, tp, dl, row0, l):
    win = jnp.exp(-tp * dl)
    row = row0 + lax.broadcasted_iota(jnp.int32, win.shape, 0)
    hf = jnp.dot(hid, w2f, precision=HIGHEST, preferred_element_type=F32) * win
    hb = jnp.dot(hid, w2b, precision=HIGHEST, preferred_element_type=F32) * win
    hf = jnp.where(row < l, hf, 0.0)
    hb = jnp.where((row > l) | (row == 0), hb, 0.0)
    return hf + hb, jnp.sum(jnp.abs(hf) + jnp.abs(hb), axis=0, keepdims=True)


def _with_skip(g, skip):
    row = lax.broadcasted_iota(jnp.int32, g.shape, 0)
    return g + jnp.where(row == 0, skip, 0.0)


def _filt_ctx_kernel(z_ref, w1_ref, b1_ref, fr_ref, w2_ref, dl_ref, sk_ref, fh_ref, fl_ref, g_ref):
    hid = _filter_hidden(z_ref, w1_ref, b1_ref, fr_ref)
    tp = z_ref[:, 0:1]
    for o in range(2):
        w2f = w2_ref[0, :, (2 * o) * W_HY:(2 * o + 1) * W_HY]
        w2b = w2_ref[0, :, (2 * o + 1) * W_HY:(2 * o + 2) * W_HY]
        raw, nrm = _filter_raw(hid, w2f, w2b, tp, dl_ref[...], 0, z_ref.shape[0] // 2)
        g = _with_skip(raw / nrm, sk_ref[0, pl.ds(o, 1), :])
        g_ref[0, o] = _dot3c(fh_ref[...], fl_ref[...], g)


def _ctx_dft_tables(l):
    n = 2 * l
    k = np.arange(n)[:, None]
    t = np.arange(l)[None, :]
    ang = 2.0 * np.pi * k * t / n
    c, s = np.cos(ang), np.sin(ang)
    fwd = np.block([[c, s], [-s, c]])
    inv = np.block([[c.T, -s.T], [s.T, c.T]])
    n_all = np.arange(n)[None, :]
    angg = 2.0 * np.pi * k * n_all / n
    filt = np.concatenate([np.cos(angg), -np.sin(angg)], axis=0) / n
    return _split_np(fwd), _split_np(inv), _split_np(filt)


def _filt_ctx_call(l, w1, b1, freq, w2, skip):
    n = 2 * l
    z_ext = _filter_positions(l)
    _, _, (fh, fl) = _ctx_dft_tables(l)
    w1p = jnp.pad(w1, ((0, 0), (0, 32 - HY_POS_FEAT), (0, 0)))
    lay = lambda *shape: pl.BlockSpec((1,) + shape, lambda d: (d,) + (0,) * len(shape))
    full = lambda a: pl.BlockSpec(a.shape, lambda d: (0,) * a.ndim)
    dl = _hyena_deltas()
    return pl.pallas_call(
        _filt_ctx_kernel,
        grid=(DEPTH,),
        in_specs=[full(z_ext), lay(32, HY_FILT_HID), lay(1, HY_FILT_HID), lay(1, HY_FILT_HID),
                  lay(HY_FILT_HID, 4 * W_HY), full(dl), lay(2, W_HY), full(fh), full(fl)],
        out_specs=pl.BlockSpec((1, 2, 2 * n, W_HY), lambda d: (d, 0, 0, 0)),
        out_shape=jax.ShapeDtypeStruct((DEPTH, 2, 2 * n, W_HY), F32),
        compiler_params=_cparams(("arbitrary",)),
        name="hyena_filter_ctx",
    )(z_ext, w1p, b1.reshape(DEPTH, 1, -1), freq.reshape(DEPTH, 1, -1), w2, dl, skip, fh, fl)


def _lat_dft_tables():
    ka = np.arange(NA)[:, None]
    b = np.arange(NB)[:, None, None]
    kb = np.arange(NB)[:, None]
    bb = np.arange(NB)[None, :]
    a_half = np.arange(NA // 2)[None, :]
    a_full = np.arange(NA)[None, :]
    phi = 2.0 * np.pi * (ka * a_half / NA + b * ka / LAT_N)
    c, s = np.cos(phi), np.sin(phi)
    a_fwd = np.concatenate([c, s], axis=2)
    ct, st = np.swapaxes(c, 1, 2), np.swapaxes(s, 1, 2)
    a_inv = np.concatenate([ct, st], axis=2)
    phig = 2.0 * np.pi * (ka * a_full / NA + b * ka / LAT_N)
    a_flt = np.concatenate([np.cos(phig), -np.sin(phig)], axis=1) / LAT_N
    ang = 2.0 * np.pi * kb * bb / NB
    c2, s2 = np.cos(ang), np.sin(ang)
    f_fwd = np.block([[c2, s2], [-s2, c2]])
    f_inv = np.block([[c2, -s2], [s2, c2]])
    return (_split_np(a_fwd), _split_np(a_inv), _split_np(a_flt), _split_np(f_fwd), _split_np(f_inv))


def _stage_b_rows(ka):
    re = pl.ds(ka, NB, stride=Y_PITCH)
    im = pl.ds(NA + ka, NB, stride=Y_PITCH)
    return re, im


def _filt_lat_kernel(z_ref, w1_ref, b1_ref, fr_ref, w2f_ref, w2b_ref, dl_ref, sk_ref, ah_ref, f2h_ref,
                     g_ref, hid_ref, gt_ref, y_ref):
    step = pl.program_id(1)
    rch = 1024
    nch = LAT_N // rch
    rows_of = lambda i: pl.ds(pl.multiple_of(i * rch, rch), rch)

    @pl.when(step == 0)
    def _():
        def hid_chunk(i, carry):
            r = rows_of(i)
            pre = jnp.dot(z_ref[r, :], w1_ref[0], precision=HIGHEST, preferred_element_type=F32)
            hid_ref[r, :] = jnp.sin(fr_ref[0] * (pre + b1_ref[0]))
            return carry

        lax.fori_loop(0, nch, hid_chunk, 0)

    w2f_hl = _split(w2f_ref[0])
    w2b_hl = _split(w2b_ref[0])

    def raw_chunk(w2_hl, i, nrm):
        r = rows_of(i)
        hh, hl = _split(hid_ref[r, :])
        h = _dot(hh, w2_hl[0]) + _dot(hl, w2_hl[0]) + _dot(hh, w2_hl[1])
        h = h * jnp.exp(-z_ref[r, 0:1] * dl_ref[...])
        row = i * rch + lax.broadcasted_iota(jnp.int32, h.shape, 0)
        h = jnp.where(row == LAT_L, 0.0, h)
        for s in range(rch // NB):
            slab = pl.ds(pl.multiple_of((i * (rch // NB) + s) * X_PITCH, 8), NB)
            gt_ref[slab, :] = h[s * NB:(s + 1) * NB]
        return nrm + jnp.sum(jnp.abs(h), axis=0, keepdims=True)

    nrm = lax.fori_loop(0, nch // 2, functools.partial(raw_chunk, w2f_hl), jnp.zeros((1, LANES), F32))
    nrm = lax.fori_loop(nch // 2, nch, functools.partial(raw_chunk, w2b_hl), nrm)
    hh, hl = _split(hid_ref[0:8, :])
    hb0 = _dot(hh, w2b_hl[0]) + _dot(hl, w2b_hl[0]) + _dot(hh, w2b_hl[1])
    hb0 = hb0 * jnp.exp(-z_ref[0:8, 0:1] * dl_ref[...])
    hb0 = jnp.where(lax.broadcasted_iota(jnp.int32, hb0.shape, 0) == 0, hb0, 0.0)
    gt_ref[0:8, :] = gt_ref[0:8, :] + hb0
    nrm = nrm + jnp.sum(jnp.abs(hb0), axis=0, keepdims=True)

    def norm_slab(a, carry):
        slab = pl.ds(pl.multiple_of(a * X_PITCH, 8), NB)
        gt_ref[slab, :] = gt_ref[slab, :] / nrm
        return carry

    lax.fori_loop(0, NA, norm_slab, 0, unroll=8)
    order = step // (W_HY // LANES)
    gt_ref[0:8, :] = _with_skip(gt_ref[0:8, :], sk_ref[0, pl.ds(order, 1), :])

    def stage_a(b, carry):
        rows = gt_ref[pl.ds(b, NA, stride=X_PITCH), :]
        y_ref[pl.ds(pl.multiple_of(b * Y_PITCH, 8), 2 * NA), :] = _dot1c(ah_ref[b], rows)
        return carry

    lax.fori_loop(0, NB, stage_a, 0, unroll=8)

    def stage_b(j, carry):
        ka = 2 * j
        re, im = _stage_b_rows(ka)
        re1, im1 = _stage_b_rows(ka + 1)
        z = jnp.concatenate([jnp.concatenate([y_ref[re, :], y_ref[im, :]], axis=0),
                             jnp.concatenate([y_ref[re1, :], y_ref[im1, :]], axis=0)], axis=1)
        x = _dot1c(f2h_ref[...], z)
        g0 = pl.multiple_of(ka * 2 * NB, 2 * NB)
        g_ref[0, 0, pl.ds(g0, 2 * NB), :] = x[:, :LANES]
        g_ref[0, 0, pl.ds(g0 + 2 * NB, 2 * NB), :] = x[:, LANES:]
        return carry

    lax.fori_loop(0, NA // 2, stage_b, 0, unroll=4)


def _filt_lat_call(w1, b1, freq, w2, skip):
    z_ext = _filter_positions(LAT_L)
    _, _, (ah, _), (f2h, _), _ = _lat_dft_tables()
    w1p = jnp.pad(w1, ((0, 0), (0, 32 - HY_POS_FEAT), (0, 0)))
    nct = W_HY // LANES
    one = pl.Buffered(1)
    lay = lambda *shape: pl.BlockSpec((1,) + shape, lambda d, s: (d,) + (0,) * len(shape))
    full = lambda a: pl.BlockSpec(a.shape, lambda d, s: (0,) * a.ndim, pipeline_mode=one)
    dl = _hyena_deltas()
    return pl.pallas_call(
        _filt_lat_kernel,
        grid=(DEPTH, 2 * nct),
        in_specs=[full(z_ext), lay(32, HY_FILT_HID), lay(1, HY_FILT_HID), lay(1, HY_FILT_HID),
                  pl.BlockSpec((1, HY_FILT_HID, LANES), lambda d, s: (d, 0, (s // nct) * 2 * nct + s % nct)),
                  pl.BlockSpec((1, HY_FILT_HID, LANES),
                               lambda d, s: (d, 0, (s // nct) * 2 * nct + nct + s % nct)),
                  pl.BlockSpec((1, LANES), lambda d, s: (0, s % nct)),
                  pl.BlockSpec((1, 2, LANES), lambda d, s: (d, 0, s % nct)),
                  full(ah), full(f2h)],
        out_specs=pl.BlockSpec((1, 1, NA * 2 * NB, LANES), lambda d, s: (d, s, 0, 0)),
        out_shape=jax.ShapeDtypeStruct((DEPTH, 2 * nct, NA * 2 * NB, LANES), F32),
        scratch_shapes=[pltpu.VMEM((LAT_N, HY_FILT_HID), F32), pltpu.VMEM((NA * X_PITCH, LANES), F32),
                        pltpu.VMEM((NB * Y_PITCH, LANES), F32)],
        compiler_params=_cparams(("arbitrary", "arbitrary")),
        name="hyena_filter_lat",
    )(z_ext, w1p, b1.reshape(DEPTH, 1, -1), freq.reshape(DEPTH, 1, -1), w2, w2, dl, skip, ah, f2h)


def _short_conv_rows(ref, bi, r0, rows, first, last, w):
    total = ref.shape[1]
    cur = ref[bi, pl.ds(r0, rows), :]
    before = ref[bi, pl.ds(jnp.maximum(r0 - 1, 0), 1), :]
    after = ref[bi, pl.ds(jnp.minimum(r0 + rows, total - 1), 1), :]
    before = jnp.where(first, 0.0, before)
    after = jnp.where(last, 0.0, after)
    rid = lax.broadcasted_iota(jnp.int32, cur.shape, 0)
    prev = jnp.where(rid == 0, before, pltpu.roll(cur, 1, 0))
    nxt = jnp.where(rid == rows - 1, after, pltpu.roll(cur, rows - 1, 0))
    return prev * w[0:1] + cur * w[1:2] + nxt * w[2:3]


def _short_conv_interior(ref, bi, r0, rows, w):
    prev = ref[bi, pl.ds(r0 - 1, rows), :]
    cur = ref[bi, pl.ds(r0, rows), :]
    nxt = ref[bi, pl.ds(r0 + 1, rows), :]
    return prev * w[0:1] + cur * w[1:2] + nxt * w[2:3]


def _cmul(xr, xi, gr, gi):
    return xr * gr - xi * gi, xr * gi + xi * gr


def _hy_ctx_kernel(v_ref, x1_ref, x2_ref, cw_ref, g_ref, fh_ref, ih_ref, o_ref):
    l = v_ref.shape[1]
    n = 2 * l

    def sc(ref, bi, grp):
        w = cw_ref[:, grp * W_HY:(grp + 1) * W_HY]
        return _short_conv_rows(ref, bi, 0, l, True, True, w)

    def conv(zr, zi, order):
        x = _dot1c(fh_ref[...], jnp.concatenate([zr, zi], axis=0))
        pr, pi = _cmul(x[:n], x[n:], g_ref[order, :n], g_ref[order, n:])
        y = _dot1c(ih_ref[...], jnp.concatenate([pr, pi], axis=0))
        return y[:l], y[l:]

    yr, yi = conv(sc(v_ref, 0, 0), sc(v_ref, 1, 0), 0)
    yr, yi = conv(sc(x1_ref, 0, 1) * yr, sc(x1_ref, 1, 1) * yi, 1)
    o_ref[0] = sc(x2_ref, 0, 2) * yr
    o_ref[1] = sc(x2_ref, 1, 2) * yi


def _hy_ctx_call(u, conv_w, g_spec, layer):
    b, l, _ = u.shape
    (fh, _), (ih, _), _ = _ctx_dft_tables(l)
    grp = lambda g: pl.BlockSpec((2, l, W_HY), lambda i: (i, 0, g))
    full = lambda a: pl.BlockSpec(a.shape, lambda i: (0,) * a.ndim)
    return pl.pallas_call(
        _hy_ctx_kernel,
        grid=(b // 2,),
        in_specs=[grp(0), grp(1), grp(2), full(conv_w),
                  pl.BlockSpec((None,) + g_spec.shape[1:], lambda i: (layer, 0, 0, 0)), full(fh), full(ih)],
        out_specs=pl.BlockSpec((2, l, W_HY), lambda i: (i, 0, 0)),
        out_shape=jax.ShapeDtypeStruct((b, l, W_HY), F32),
        compiler_params=_cparams(("parallel",)),
        name="hyena_ctx",
    )(u, u, u, conv_w, g_spec, fh, ih)


def _hy_lat_kernel(z_ref, m_ref, cw_ref, g_ref, af_ref, ai_ref, f2_ref, f3_ref, o_ref, xr_scr, xi_scr,
                   y_scr, *, conv_in):
    x_scr = (xr_scr, xi_scr)
    na_half = NA // 2
    w_in = cw_ref[0] if conv_in else None
    w_mul = cw_ref[1]

    def conv_slab(ref, bi, a, w):
        if isinstance(a, int):
            return _short_conv_rows(ref, bi, a * NB, NB, a == 0, a == na_half - 1, w)
        return _short_conv_interior(ref, bi, pl.multiple_of(a * NB, NB), NB, w)

    def edges_then_interior(body):
        body(0, 0)
        body(na_half - 1, 0)
        lax.fori_loop(1, na_half - 1, body, 0, unroll=2)

    def load_in(a, carry):
        for bi in range(2):
            if conv_in:
                val = conv_slab(z_ref, bi, a, w_in)
            else:
                val = z_ref[bi, pl.ds(pl.multiple_of(a * NB, NB), NB), :]
            x_scr[bi][pl.ds(pl.multiple_of(a * X_PITCH, 8), NB), :] = val
        return carry

    edges_then_interior(load_in)

    def stage_a(b, carry):
        zr = xr_scr[pl.ds(b, na_half, stride=X_PITCH), :]
        zi = xi_scr[pl.ds(b, na_half, stride=X_PITCH), :]
        rhs = jnp.concatenate([jnp.concatenate([zr, zi], axis=0), jnp.concatenate([zi, -zr], axis=0)],
                              axis=1)
        y = _dot1c(af_ref[b], rhs)
        r0 = pl.multiple_of(b * Y_PITCH, 8)
        y_scr[pl.ds(r0, NA), :] = y[:, :LANES]
        y_scr[pl.ds(r0 + NA, NA), :] = y[:, LANES:]
        return carry

    lax.fori_loop(0, NB, stage_a, 0, unroll=32)

    def spectrum_product(j):
        ka = 2 * j
        re, im = _stage_b_rows(ka)
        re1, im1 = _stage_b_rows(ka + 1)
        z = jnp.concatenate([jnp.concatenate([y_scr[re, :], y_scr[im, :]], axis=0),
                             jnp.concatenate([y_scr[re1, :], y_scr[im1, :]], axis=0)], axis=1)
        x = _dot1c(f2_ref[...], z)
        g0 = pl.multiple_of(ka * 2 * NB, 2 * NB)
        gr = jnp.concatenate([g_ref[pl.ds(g0, NB), :], g_ref[pl.ds(g0 + 2 * NB, NB), :]], axis=1)
        gi = jnp.concatenate([g_ref[pl.ds(g0 + NB, NB), :], g_ref[pl.ds(g0 + 3 * NB, NB), :]], axis=1)
        pr, pi = _cmul(x[:NB], x[NB:], gr, gi)
        return jnp.concatenate([pr, pi], axis=0).astype(BF16)

    def inverse_b(j, prod):
        ka = 2 * j
        re, im = _stage_b_rows(ka)
        re1, im1 = _stage_b_rows(ka + 1)
        u = _dot(f3_ref[...], prod)
        y_scr[re, :] = u[:NB, :LANES]
        y_scr[im, :] = u[NB:, :LANES]
        y_scr[re1, :] = u[:NB, LANES:]
        y_scr[im1, :] = u[NB:, LANES:]

    def stage_b(j, prod):
        nxt = spectrum_product(j + 1)
        inverse_b(j, prod)
        return nxt

    last = lax.fori_loop(0, NA // 2 - 1, stage_b, spectrum_product(0), unroll=4)
    inverse_b(NA // 2 - 1, last)

    def stage_c(b, carry):
        r0 = pl.multiple_of(b * Y_PITCH, 8)
        ur = y_scr[pl.ds(r0, NA), :]
        ui = y_scr[pl.ds(r0 + NA, NA), :]
        rhs = jnp.concatenate([jnp.concatenate([ur, -ui], axis=0), jnp.concatenate([ui, ur], axis=0)],
                              axis=1)
        y = _dot1c(ai_ref[b], rhs)
        xr_scr[pl.ds(b, na_half, stride=X_PITCH), :] = y[:, :LANES]
        xi_scr[pl.ds(b, na_half, stride=X_PITCH), :] = y[:, LANES:]
        return carry

    lax.fori_loop(0, NB, stage_c, 0, unroll=32)

    def store_out(a, carry):
        for bi in range(2):
            mul = conv_slab(m_ref, bi, a, w_mul)
            o_ref[bi, pl.ds(pl.multiple_of(a * NB, NB), NB), :] = (
                x_scr[bi][pl.ds(pl.multiple_of(a * X_PITCH, 8), NB), :] * mul)
        return carry

    edges_then_interior(store_out)


def _hy_lat_call(src, src_col, u, mul_col, conv_w2, g_spec, layer, order, *, conv_in):
    b, l, _ = u.shape
    nct = W_HY // LANES
    (af, _), (ai, _), _, (f2, _), (f3, _) = _lat_dft_tables()
    one = pl.Buffered(1)
    blk = lambda col: pl.BlockSpec((2, l, LANES), lambda c, p: (p, 0, col + c))
    const = lambda a: pl.BlockSpec(a.shape, lambda c, p: (0,) * a.ndim, pipeline_mode=one)
    return pl.pallas_call(
        functools.partial(_hy_lat_kernel, conv_in=conv_in),
        grid=(nct, b // 2),
        in_specs=[blk(src_col), blk(mul_col),
                  pl.BlockSpec((2, 3, LANES), lambda c, p: (0, 0, c)),
                  pl.BlockSpec((None, None, NA * 2 * NB, LANES), lambda c, p: (layer, order * nct + c, 0, 0)),
                  const(af), const(ai), const(f2), const(f3)],
        out_specs=pl.BlockSpec((2, l, LANES), lambda c, p: (p, 0, c)),
        out_shape=jax.ShapeDtypeStruct((b, l, W_HY), F32),
        scratch_shapes=[pltpu.VMEM(((NA // 2) * X_PITCH, LANES), F32),
                        pltpu.VMEM(((NA // 2) * X_PITCH, LANES), F32),
                        pltpu.VMEM((NB * Y_PITCH, LANES), F32)],
        compiler_params=_cparams(("arbitrary", "arbitrary")),
        name="hyena_lat_conv_in" if conv_in else "hyena_lat",
    )(src, u, conv_w2, g_spec, af, ai, f2, f3)


def _rope_tables(l):
    f32 = np.float32
    rows = l // GRID_W
    row = np.repeat(np.arange(rows), GRID_W).astype(f32)
    col = np.tile(np.arange(GRID_W), rows).astype(f32)
    quarter = HEAD_DIM // 4
    inv = np.power(f32(ROPE_BASE), -np.arange(quarter, dtype=f32) / f32(quarter)).astype(f32)
    ang = np.concatenate([row[:, None] * inv, col[:, None] * inv], axis=-1).astype(f32)
    cos, sin = np.cos(ang), np.sin(ang)
    q = quarter
    cos_h = np.concatenate([cos[:, :q], cos[:, :q], cos[:, q:], cos[:, q:]], axis=-1)
    sin_h = np.concatenate([-sin[:, :q], sin[:, :q], -sin[:, q:], sin[:, q:]], axis=-1)
    return jnp.asarray(np.tile(cos_h, (1, 2))), jnp.asarray(np.tile(sin_h, (1, 2)))


def kernel(x_prompt, x_sample, c, cache_k, cache_v, state_ret, c_ctx, norm_w, w_mod, b_mod, w_in, hy_conv,
           hy_filt_w1, hy_filt_b1, hy_filt_freq, hy_filt_w2, hy_skip, attn_sink, ret_theta, ret_gn,
           w_branch_a, w_branch_b, w_branch_c, w_merge, b_merge, w_out, final_norm_w):
    d = D_MODEL
    bc, lc, _ = x_prompt.shape
    bl, ll, _ = x_sample.shape
    assert ll == LAT_L and bc % 2 == 0 and bl % 2 == 0
    past = cache_k.shape[2]

    cond = jnp.zeros((16, d), F32).at[:bl].set(c).at[bl].set(c_ctx)
    mod = _mod_call(cond, w_mod, b_mod)

    g_ctx = _filt_ctx_call(lc, hy_filt_w1, hy_filt_b1, hy_filt_freq, hy_filt_w2, hy_skip)
    g_lat = _filt_lat_call(hy_filt_w1, hy_filt_b1, hy_filt_freq, hy_filt_w2, hy_skip)

    cos_t, sin_t = _rope_tables(ll)
    w_in_b = w_in.astype(BF16)
    wm_b = w_merge.astype(BF16)
    wa_b = w_branch_a.astype(BF16)
    wb_b = w_branch_b.astype(BF16)
    wc_b = w_branch_c.astype(BF16)
    wo_b = w_out.astype(BF16)
    fnw = final_norm_w.reshape(1, d)
    k_ctx = cache_k.reshape(bl, DEPTH, past, W_KV)
    v_ctx = cache_v.reshape(bl, DEPTH, past, W_KV)
    hy_cols = COL_HY // LANES
    nct = W_HY // LANES

    xp, xs = x_prompt, x_sample
    ks_out, vs_out, ss_out = [], [], []
    for l in range(DEPTH):
        final = l == DEPTH - 1
        nw = norm_w[l].reshape(1, d)
        bm = b_merge[l].reshape(1, -1)
        shift, scale, gate = (mod[l, :, i * d:(i + 1) * d][:, None, :] for i in range(3))
        conv_w = hy_conv[l]
        cw = lambda g: conv_w[:, g * W_HY:(g + 1) * W_HY]

        sl = slice(bl, bl + 1)
        u = _in_call(xp, shift[sl], scale[sl], nw, w_in_b[l], cos_t, sin_t, rope=False)
        ya = _hy_ctx_call(u, conv_w, g_ctx, l)
        yb = _attn_ctx_call(u, attn_sink[l])
        yc, sfin = _ret_call(u, ret_theta[l], ret_gn[l], None)
        res = _out_call(xp, shift[sl], scale[sl], gate[sl], nw, ya, u, yb, yc, wm_b[l], bm, wa_b[l], wb_b[l],
                        wc_b[l], wo_b[l], fnw, final=final)
        xp = res[0]
        if final:
            y_prompt = res[1]
        ks_out.append(u[:, :, COL_KA:COL_KA + W_KV])
        vs_out.append(u[:, :, COL_VA:COL_VA + W_KV])
        ss_out.append(sfin)

        sl = slice(0, bl)
        u = _in_call(xs, shift[sl], scale[sl], nw, w_in_b[l], cos_t, sin_t, rope=True)
        z1 = _hy_lat_call(u, hy_cols, u, hy_cols + nct, jnp.stack([cw(0), cw(1)]), g_lat, l, 0, conv_in=True)
        ya = _hy_lat_call(z1, 0, u, hy_cols + 2 * nct, jnp.stack([cw(2), cw(2)]), g_lat, l, 1, conv_in=False)
        yb = _attn_lat_call(u, k_ctx, v_ctx, attn_sink[l], l)
        yc, _ = _ret_call(u, ret_theta[l], ret_gn[l], state_ret, l)
        res = _out_call(xs, shift[sl], scale[sl], gate[sl], nw, ya, u, yb, yc, wm_b[l], bm, wa_b[l], wb_b[l],
                        wc_b[l], wo_b[l], fnw, final=final)
        xs = res[0]
        if final:
            y_sample = res[1]

    kv_shape = (bc, DEPTH, lc, ATT_KV_HEADS, HEAD_DIM)
    new_cache_k = jnp.stack(ks_out, axis=1).reshape(kv_shape)
    new_cache_v = jnp.stack(vs_out, axis=1).reshape(kv_shape)
    new_state_ret = jnp.stack(ss_out, axis=1)
    return (y_prompt, y_sample, new_cache_k, new_cache_v, new_state_ret)
```

```python
import functools
import math

import numpy as np
import jax
import jax.numpy as jnp
from jax import lax
from jax.experimental import pallas as pl
from jax.experimental.pallas import tpu as pltpu

F32 = jnp.float32
BF16 = jnp.bfloat16
HIGHEST = lax.Precision.HIGHEST

D_MODEL = 1024
DEPTH = 4
GRID_W = 64
W_HY = 512
HY_BANDS = 8
HY_POS_FEAT = 1 + 2 * HY_BANDS
HY_FILT_HID = 64
HY_DECAY_TARGET = 1e-2
HY_FAST_PCT = 0.3
HY_SLOW_PCT = 1.5
ATT_HEADS = 8
ATT_KV_HEADS = 2
ATT_GROUP = ATT_HEADS // ATT_KV_HEADS
HEAD_DIM = 64
W_ATT = ATT_HEADS * HEAD_DIM
W_KV = ATT_KV_HEADS * HEAD_DIM
ATT_BLOCK = 128
ATT_QB = 8
RET_HEADS = 8
RET_DIM = 64
W_RET = RET_HEADS * RET_DIM
RET_CHUNK = 128
ROPE_BASE = 10000.0
EPS = 1e-6
NEG = -1e30

LANES = 128
MXU_ROWS = 512
VMEM_LIMIT = 58 * 1024 * 1024

IN_DIM = 5376
COL_HY = 0
COL_GH = 1536
COL_QA = 2048
COL_GA = 2560
COL_QR = 3072
COL_KR = 3584
COL_VR = 4096
COL_GR = 4608
COL_KA = 5120
COL_VA = 5248
_W_COL = {COL_HY: 0, COL_HY + 512: 512, COL_HY + 1024: 1024, COL_GH: 1536, COL_QA: 2048, COL_KA: 2560,
          COL_VA: 2688, COL_GA: 2816, COL_QR: 3328, COL_KR: 3840, COL_VR: 4352, COL_GR: 4864}

LAT_L = 4096
LAT_N = 2 * LAT_L
NA = 64
NB = 128
Y_PITCH = 136
X_PITCH = 136


def _cparams(sem):
    return pltpu.CompilerParams(dimension_semantics=sem, vmem_limit_bytes=VMEM_LIMIT)


def _split_np(a):
    a32 = np.asarray(a, np.float32)
    hi = a32.astype(BF16)
    lo = (a32 - hi.astype(np.float32)).astype(BF16)
    return jnp.asarray(hi), jnp.asarray(lo)


def _split(x):
    hi = x.astype(BF16)
    lo = (x - hi.astype(F32)).astype(BF16)
    return hi, lo


def _dot(a, b):
    return jnp.dot(a, b, preferred_element_type=F32)


def _dot3c(chi, clo, x):
    xh, xl = _split(x)
    return _dot(chi, xh) + _dot(clo, xh) + _dot(chi, xl)


def _dot1c(chi, x):
    return _dot(chi, x.astype(BF16))


def _silu(x):
    return x * jax.nn.sigmoid(x)


def _mod_kernel(c_ref, w_ref, b_ref, o_ref):
    s = _silu(c_ref[...])
    o_ref[0] = jnp.dot(s, w_ref[0], precision=HIGHEST, preferred_element_type=F32) + b_ref[0]


def _mod_call(cond, w_mod, b_mod):
    rows, d = cond.shape
    n = w_mod.shape[-1]
    tn = 1024
    return pl.pallas_call(
        _mod_kernel,
        grid=(DEPTH, n // tn),
        in_specs=[pl.BlockSpec((rows, d), lambda l, j: (0, 0)),
                  pl.BlockSpec((1, d, tn), lambda l, j: (l, 0, j)),
                  pl.BlockSpec((1, 1, tn), lambda l, j: (l, 0, j))],
        out_specs=pl.BlockSpec((1, rows, tn), lambda l, j: (l, 0, j)),
        out_shape=jax.ShapeDtypeStruct((DEPTH, rows, n), F32),
        compiler_params=_cparams(("arbitrary", "arbitrary")),
        name="adaln_mod",
    )(cond, w_mod, b_mod.reshape(DEPTH, 1, n))


def _modulated(x, nw, scale, shift):
    ms = jnp.mean(x * x, axis=-1, keepdims=True)
    h = x * lax.rsqrt(ms + EPS) * nw
    return h * (1.0 + scale) + shift


def _rope128(x, cos, sin_signed, first_half):
    up = pltpu.roll(x, LANES - 16, 1)
    dn = pltpu.roll(x, 16, 1)
    return x * cos + jnp.where(first_half, up, dn) * sin_signed


def _rows(ref):
    bt, tm, w = ref.shape
    return ref[...].reshape(bt * tm, w)


def _put(ref, c0, val):
    bt, tm, _ = ref.shape
    ref[:, :, c0:c0 + val.shape[1]] = val.reshape(bt, tm, val.shape[1])


def _in_kernel(x_ref, shift_ref, scale_ref, nw_ref, w_ref, cos_ref, sin_ref, o_ref, *, rope):
    x = _rows(x_ref)
    rows = x.shape[0]
    hb = _modulated(x, nw_ref[...], scale_ref[0], shift_ref[0]).astype(BF16)
    if rope:
        cos = cos_ref[...]
        sin = sin_ref[...]
        lane = lax.broadcasted_iota(jnp.int32, (rows, LANES), 1)
        first_half = (lane % 32) < 16

    def seg(c0, width):
        w0 = _W_COL[c0]
        return _dot(hb, w_ref[:, w0:w0 + width])

    def put_rope(c0, val, mul):
        for i in range(val.shape[1] // LANES):
            piece = val[:, i * LANES:(i + 1) * LANES]
            if rope:
                piece = _rope128(piece, cos, sin, first_half)
            if mul is not None:
                piece = piece * mul
            _put(o_ref, c0 + i * LANES, piece)

    for g in range(3):
        _put(o_ref, COL_HY + g * 512, seg(COL_HY + g * 512, 512))
    _put(o_ref, COL_GH, _silu(seg(COL_GH, 512)))
    put_rope(COL_QA, seg(COL_QA, 512), None)
    _put(o_ref, COL_GA, _silu(seg(COL_GA, 512)))
    put_rope(COL_QR, seg(COL_QR, 512), None)
    put_rope(COL_KR, seg(COL_KR, 512), RET_DIM ** -0.5)
    _put(o_ref, COL_VR, seg(COL_VR, 512))
    _put(o_ref, COL_GR, _silu(seg(COL_GR, 512)))
    put_rope(COL_KA, seg(COL_KA, 128), None)
    _put(o_ref, COL_VA, seg(COL_VA, 128))


def _token_tiling(b, l, per_batch):
    if l >= MXU_ROWS:
        return 1, MXU_ROWS
    bt = 1 if per_batch else min(b, MXU_ROWS // l)
    return bt, l


def _in_call(x, shift, scale, nw, w, cos_t, sin_t, *, rope):
    b, l, d = x.shape
    per_batch = shift.shape[0] > 1
    bt, tm = _token_tiling(b, l, per_batch)
    assert not rope or bt == 1
    mod_map = (lambda i, j: (i, 0, 0)) if per_batch else (lambda i, j: (0, 0, 0))
    return pl.pallas_call(
        functools.partial(_in_kernel, rope=rope),
        grid=(b // bt, l // tm),
        in_specs=[pl.BlockSpec((bt, tm, d), lambda i, j: (i, j, 0)),
                  pl.BlockSpec((1, 1, d), mod_map),
                  pl.BlockSpec((1, 1, d), mod_map),
                  pl.BlockSpec((1, d), lambda i, j: (0, 0)),
                  pl.BlockSpec((d, IN_DIM), lambda i, j: (0, 0), pipeline_mode=pl.Buffered(1)),
                  pl.BlockSpec((tm, LANES), lambda i, j: (j, 0)),
                  pl.BlockSpec((tm, LANES), lambda i, j: (j, 0))],
        out_specs=pl.BlockSpec((bt, tm, IN_DIM), lambda i, j: (i, j, 0)),
        out_shape=jax.ShapeDtypeStruct((b, l, IN_DIM), F32),
        compiler_params=_cparams(("parallel", "parallel")),
        name="in_proj_rope" if rope else "in_proj",
    )(x, shift, scale, nw, w, cos_t, sin_t)


def _retention_post(o, gn, gate):
    lane = lax.broadcasted_iota(jnp.int32, (1, LANES), 1)
    lo_head = lane < RET_DIM
    outs = []
    for t in range(W_RET // LANES):
        sl = slice(t * LANES, (t + 1) * LANES)
        ot = o[:, sl]
        o2 = ot * ot
        s_lo = jnp.sum(jnp.where(lo_head, o2, 0.0), axis=-1, keepdims=True)
        s_hi = jnp.sum(jnp.where(lo_head, 0.0, o2), axis=-1, keepdims=True)
        ms = jnp.where(lo_head, s_lo, s_hi) * (1.0 / RET_DIM)
        outs.append(ot * lax.rsqrt(ms + EPS) * gn[:, sl] * gate[:, sl])
    return jnp.concatenate(outs, axis=1)


def _out_kernel(x_ref, shift_ref, scale_ref, gate_ref, nw_ref, ya_ref, gh_ref, yb_ref, yc_ref, gr_ref, gn_ref,
                wm_ref, bm_ref, wa_ref, wb_ref, wc_ref, wo_ref, fnw_ref, *out_refs, final):
    x = _rows(x_ref)
    d = x.shape[1]
    hb = _modulated(x, nw_ref[...], scale_ref[0], shift_ref[0]).astype(BF16)
    branches = (_rows(ya_ref) * _rows(gh_ref), _rows(yb_ref),
                _retention_post(_rows(yc_ref), gn_ref[...], _rows(gr_ref)))
    merged = None
    for i, (y, w_ref) in enumerate(zip(branches, (wa_ref, wb_ref, wc_ref))):
        g = jax.nn.sigmoid(_dot(hb, wm_ref[:, i * d:(i + 1) * d]) + bm_ref[:, i * d:(i + 1) * d])
        term = g * _dot(y.astype(BF16), w_ref[...])
        merged = term if merged is None else merged + term
    out = _dot(merged.astype(BF16), wo_ref[...])
    xn = x + gate_ref[0] * out
    _put(out_refs[0], 0, xn)
    if final:
        ms = jnp.mean(xn * xn, axis=-1, keepdims=True)
        _put(out_refs[1], 0, xn * lax.rsqrt(ms + EPS) * fnw_ref[...])


def _out_call(x, shift, scale, gate, nw, ya, u, yb, yc, gn, wm, bm, wa, wb, wc, wo, fnw, *, final):
    b, l, d = x.shape
    per_batch = shift.shape[0] > 1
    bt, tm = _token_tiling(b, l, per_batch)
    mod_map = (lambda i, j: (i, 0, 0)) if per_batch else (lambda i, j: (0, 0, 0))
    tok = lambda w: pl.BlockSpec((bt, tm, w), lambda i, j: (i, j, 0))
    full = lambda a: pl.BlockSpec(a.shape, lambda i, j: (0,) * a.ndim, pipeline_mode=pl.Buffered(1))
    n_out = 2 if final else 1
    res = pl.pallas_call(
        functools.partial(_out_kernel, final=final),
        grid=(b // bt, l // tm),
        in_specs=[tok(d), pl.BlockSpec((1, 1, d), mod_map), pl.BlockSpec((1, 1, d), mod_map),
                  pl.BlockSpec((1, 1, d), mod_map), full(nw), tok(W_HY),
                  pl.BlockSpec((bt, tm, W_HY), lambda i, j: (i, j, COL_GH // W_HY)), tok(W_ATT), tok(W_RET),
                  pl.BlockSpec((bt, tm, W_RET), lambda i, j: (i, j, COL_GR // W_RET)), full(gn),
                  full(wm), full(bm), full(wa), full(wb), full(wc), full(wo), full(fnw)],
        out_specs=[tok(d)] * n_out,
        out_shape=[jax.ShapeDtypeStruct((b, l, d), F32)] * n_out,
        compiler_params=_cparams(("parallel", "parallel")),
        name="merge_out_final" if final else "merge_out",
    )(x, shift, scale, gate, nw, ya, u, yb, yc, u, gn, wm, bm, wa, wb, wc, wo, fnw)
    return res


_NT = (((1,), (1,)), ((), ()))


_TN = (((0,), (0,)), ((), ()))
LOG2E = 1.4426950408889634
Q_SCALE = (HEAD_DIM ** -0.5) * LOG2E


def _attn_scores_t(q, kh, kv):
    h0 = kv * ATT_GROUP
    qs = jnp.concatenate([q[:, (h0 + g) * HEAD_DIM:(h0 + g + 1) * HEAD_DIM] for g in range(ATT_GROUP)],
                         axis=0).astype(BF16)
    return lax.dot_general(kh, qs, _NT, preferred_element_type=F32)


def _attn_finish_t(sink_ref, s, vh, g_ref, o_ref, kv, row0=0):
    tk, cols = s.shape
    t = cols // ATT_GROUP
    h0 = kv * ATT_GROUP
    head = lax.broadcasted_iota(jnp.int32, (1, cols), 1) // t
    sink = jnp.full((1, cols), sink_ref[h0], F32)
    for g in range(1, ATT_GROUP):
        sink = jnp.where(head == g, sink_ref[h0 + g], sink)
    sink = sink * LOG2E
    m = jnp.maximum(jnp.max(s, axis=0, keepdims=True), sink)
    p = jnp.exp2(s - m).astype(BF16)
    v_ext = jnp.concatenate([vh, jnp.ones((tk, HEAD_DIM), BF16)], axis=1)
    o_ext = lax.dot_general(v_ext, p, _TN, preferred_element_type=F32)
    denom = o_ext[HEAD_DIM:HEAD_DIM + 1] + jnp.exp2(sink - m)
    o = o_ext[:HEAD_DIM] / denom
    for gp in range(ATT_GROUP // 2):
        pair = jnp.concatenate([o[:, (2 * gp) * t:(2 * gp + 1) * t], o[:, (2 * gp + 1) * t:(2 * gp + 2) * t]],
                               axis=0)
        c0 = (h0 + 2 * gp) * HEAD_DIM
        o_ref[0, row0:row0 + t, c0:c0 + 2 * HEAD_DIM] = pair.T * g_ref[0, row0:row0 + t, c0:c0 + 2 * HEAD_DIM]


def _attn_ctx_kernel(sink_ref, q_ref, k_ref, v_ref, g_ref, o_ref):
    q = q_ref[0] * Q_SCALE
    k = k_ref[0].astype(BF16)
    v = v_ref[0].astype(BF16)
    scores = [_attn_scores_t(q, k[:, kv * HEAD_DIM:(kv + 1) * HEAD_DIM], kv) for kv in range(ATT_KV_HEADS)]
    for kv in range(ATT_KV_HEADS):
        _attn_finish_t(sink_ref, scores[kv], v[:, kv * HEAD_DIM:(kv + 1) * HEAD_DIM], g_ref, o_ref, kv)


def _attn_ctx_call(u, sink):
    b, l, _ = u.shape
    return pl.pallas_call(
        _attn_ctx_kernel,
        grid=(b,),
        in_specs=[pl.BlockSpec(memory_space=pltpu.SMEM),
                  pl.BlockSpec((1, l, W_ATT), lambda i: (i, 0, COL_QA // W_ATT)),
                  pl.BlockSpec((1, l, W_KV), lambda i: (i, 0, COL_KA // W_KV)),
                  pl.BlockSpec((1, l, W_KV), lambda i: (i, 0, COL_VA // W_KV)),
                  pl.BlockSpec((1, l, W_ATT), lambda i: (i, 0, COL_GA // W_ATT))],
        out_specs=pl.BlockSpec((1, l, W_ATT), lambda i: (i, 0, 0)),
        out_shape=jax.ShapeDtypeStruct((b, l, W_ATT), F32),
        compiler_params=_cparams(("parallel",)),
        name="attn_ctx",
    )(sink, u, u, u, u)


def _attn_lat_kernel(sink_ref, q_ref, kp_ref, kc_ref, kn_ref, vp_ref, vc_ref, vn_ref, kx_ref, vx_ref,
                     g_ref, o_ref):
    j = pl.program_id(1)
    last = pl.num_programs(1) - 1
    b = ATT_BLOCK
    nq = ATT_QB
    bf = lambda ref: ref[0].astype(BF16)
    kx, vx = bf(kx_ref), bf(vx_ref)
    kblk = [bf(kp_ref)] + [kc_ref[0, t * b:(t + 1) * b, :].astype(BF16) for t in range(nq)] + [bf(kn_ref)]
    vblk = [bf(vp_ref)] + [vc_ref[0, t * b:(t + 1) * b, :].astype(BF16) for t in range(nq)] + [bf(vn_ref)]
    keys = [jnp.concatenate(kblk[t:t + 3] + [kx], axis=0) for t in range(nq)]
    vals = [jnp.concatenate(vblk[t:t + 3] + [vx], axis=0) for t in range(nq)]
    cols = ATT_GROUP * b
    c = lax.broadcasted_iota(jnp.int32, (b, cols), 0)
    r = lax.broadcasted_iota(jnp.int32, (b, cols), 1) % b
    ok_prev = [(c >= r) & (j > 0)] + [c >= r] * (nq - 1)
    ok_next = [c <= r] * (nq - 1) + [(c <= r) & (j < last)]

    def band(s, t):
        return jnp.concatenate([jnp.where(ok_prev[t], s[:b], NEG), s[b:2 * b],
                                jnp.where(ok_next[t], s[2 * b:3 * b], NEG), s[3 * b:]], axis=0)

    chains = [(t, kv) for t in range(nq) for kv in range(ATT_KV_HEADS)]
    scores = []
    for t, kv in chains:
        q = q_ref[0, t * b:(t + 1) * b, :] * Q_SCALE
        scores.append(band(_attn_scores_t(q, keys[t][:, kv * HEAD_DIM:(kv + 1) * HEAD_DIM], kv), t))
    for (t, kv), s in zip(chains, scores):
        _attn_finish_t(sink_ref, s, vals[t][:, kv * HEAD_DIM:(kv + 1) * HEAD_DIM], g_ref, o_ref, kv, t * b)


def _attn_lat_call(u, kctx, vctx, sink, layer):
    b, l, _ = u.shape
    nb = l // ATT_BLOCK
    past = kctx.shape[2]
    kcol = COL_KA // W_KV
    vcol = COL_VA // W_KV
    nq = ATT_QB
    prev = lambda col: pl.BlockSpec((1, ATT_BLOCK, W_KV), lambda i, j: (i, jnp.maximum(nq * j - 1, 0), col))
    cur = lambda col: pl.BlockSpec((1, nq * ATT_BLOCK, W_KV), lambda i, j: (i, j, col))
    nxt = lambda col: pl.BlockSpec((1, ATT_BLOCK, W_KV),
                                   lambda i, j: (i, jnp.minimum(nq * j + nq, nb - 1), col))
    ctx = pl.BlockSpec((1, None, past, W_KV), lambda i, j: (i, layer, 0, 0))
    return pl.pallas_call(
        _attn_lat_kernel,
        grid=(b, nb // nq),
        in_specs=[pl.BlockSpec(memory_space=pltpu.SMEM),
                  pl.BlockSpec((1, nq * ATT_BLOCK, W_ATT), lambda i, j: (i, j, COL_QA // W_ATT)),
                  prev(kcol), cur(kcol), nxt(kcol), prev(vcol), cur(vcol), nxt(vcol), ctx, ctx,
                  pl.BlockSpec((1, nq * ATT_BLOCK, W_ATT), lambda i, j: (i, j, COL_GA // W_ATT))],
        out_specs=pl.BlockSpec((1, nq * ATT_BLOCK, W_ATT), lambda i, j: (i, j, 0)),
        out_shape=jax.ShapeDtypeStruct((b, l, W_ATT), F32),
        compiler_params=_cparams(("parallel", "parallel")),
        name="attn_lat",
    )(sink, u, u, u, u, u, u, u, kctx, vctx, u)


def _log_sigmoid(x):
    return jnp.minimum(x, 0.0) - jnp.log1p(jnp.exp(-jnp.abs(x)))


_TAB_DMAT = 0
_TAB_QDEC = 2 * RET_CHUNK
_TAB_KDEC = 3 * RET_CHUNK
_TAB_CDEC = 4 * RET_CHUNK
_TAB_ROWS = 4 * RET_CHUNK + 8


def _ret_kernel(q_ref, k_ref, v_ref, thl_ref, thb_ref, s0_ref, o_ref, sfin_ref, tab_ref, *, nc, cpt,
                npairs, has_s0):
    grp = pl.program_id(0)
    c = RET_CHUNK
    lane = lax.broadcasted_iota(jnp.int32, (1, LANES), 1)
    lo_head = lane < RET_DIM
    dd = lax.broadcasted_iota(jnp.int32, (LANES, LANES), 0)
    ee = lax.broadcasted_iota(jnp.int32, (LANES, LANES), 1)
    same_head = (dd < RET_DIM) == (ee < RET_DIM)
    lanes_of = lambda p: slice(p * LANES, (p + 1) * LANES)

    @pl.when(pl.program_id(1) == 0)
    def _():
        rowf = lax.broadcasted_iota(jnp.int32, (c, LANES), 0).astype(F32)
        ii = lax.broadcasted_iota(jnp.int32, (c, c), 0)
        jj = lax.broadcasted_iota(jnp.int32, (c, c), 1)
        for p in range(npairs):
            for d in range(2):
                lg_lane = _log_sigmoid(thl_ref[d, :, lanes_of(p)])
                dist = (ii - jj) if d == 0 else (jj - ii)
                for hh in range(2):
                    head = 2 * (grp * npairs + p) + hh
                    lg_h = _log_sigmoid(thb_ref[d, pl.ds(head, 1), :])
                    dm = jnp.where(dist >= 0, jnp.exp(lg_h * jnp.maximum(dist, 0).astype(F32)), 0.0)
                    tab_ref[p, d, _TAB_DMAT + hh * c:_TAB_DMAT + (hh + 1) * c, :] = dm
                if d == 0:
                    q_dec = jnp.exp(lg_lane * (rowf + 1.0))
                    k_dec = jnp.exp(lg_lane * (c - 1.0 - rowf))
                else:
                    q_dec = jnp.exp(lg_lane * (c - rowf))
                    k_dec = jnp.exp(lg_lane * rowf)
                tab_ref[p, d, _TAB_QDEC:_TAB_QDEC + c, :] = q_dec
                tab_ref[p, d, _TAB_KDEC:_TAB_KDEC + c, :] = k_dec
                tab_ref[p, d, _TAB_CDEC:_TAB_CDEC + 8, :] = jnp.broadcast_to(
                    jnp.exp(lg_lane * float(c)), (8, LANES))

    def first_level(p, d, r0):
        qc = q_ref[0, pl.ds(r0, c), lanes_of(p)]
        kc = k_ref[0, pl.ds(r0, c), lanes_of(p)]
        vcb = v_ref[0, pl.ds(r0, c), lanes_of(p)].astype(BF16)
        qs = jnp.concatenate([jnp.where(lo_head, qc, 0.0), jnp.where(lo_head, 0.0, qc)], axis=0)
        sc = lax.dot_general(qs.astype(BF16), kc.astype(BF16), _NT, preferred_element_type=F32)
        sc = sc * tab_ref[p, d, _TAB_DMAT:_TAB_DMAT + 2 * c, :]
        kd = kc * tab_ref[p, d, _TAB_KDEC:_TAB_KDEC + c, :]
        upd = jnp.where(same_head, _dot(kd.T.astype(BF16), vcb), 0.0)
        return qc, vcb, sc.astype(BF16), upd

    def second_level(p, d, lvl1, s):
        qc, vcb, scb, upd = lvl1
        pv = _dot(scb, vcb)
        qd = qc * tab_ref[p, d, _TAB_QDEC:_TAB_QDEC + c, :]
        o = _dot(qd.astype(BF16), s.astype(BF16)) + jnp.where(lo_head, pv[:c], pv[c:])
        return o, tab_ref[p, d, _TAB_CDEC:_TAB_CDEC + 1, :] * s + upd

    def init_state(p, d):
        if not has_s0:
            return jnp.zeros((LANES, LANES), F32)
        z = jnp.zeros((RET_DIM, RET_DIM), F32)
        return jnp.concatenate([jnp.concatenate([s0_ref[0, d, 2 * p], z], axis=1),
                                jnp.concatenate([z, s0_ref[0, d, 2 * p + 1]], axis=1)], axis=0)

    units = [(p, d) for p in range(npairs) for d in range(2)]

    def scan_body(second, n, states):
        def row0(d, j):
            idx = n * cpt + j
            return pl.multiple_of((idx if d == 0 else nc - 1 - idx) * c, c)

        lvl = {(p, d, j): first_level(p, d, row0(d, j)) for j in range(cpt) for p, d in units}
        states = list(states)
        for j in range(cpt):
            for ui, (p, d) in enumerate(units):
                o, states[ui] = second_level(p, d, lvl[(p, d, j)], states[ui])
                dst = (0, pl.ds(row0(d, j), c), lanes_of(p))
                if second(j):
                    o = o + o_ref[dst]
                o_ref[dst] = o
        return tuple(states)

    assert nc % cpt == 0
    trips = nc // cpt
    states = tuple(init_state(p, d) for p, d in units)
    if trips == 1:
        states = scan_body(lambda j: 2 * j > nc - 1, 0, states)
    else:
        assert trips % 2 == 0
        states = lax.fori_loop(0, trips // 2, functools.partial(scan_body, lambda j: False), states)
        states = lax.fori_loop(trips // 2, trips, functools.partial(scan_body, lambda j: True), states)
    for ui, (p, d) in enumerate(units):
        sfin_ref[0, d, 2 * p] = states[ui][:RET_DIM, :RET_DIM]
        sfin_ref[0, d, 2 * p + 1] = states[ui][RET_DIM:, RET_DIM:]


def _ret_call(u, theta, s0bd, layer=0):
    b, l, _ = u.shape
    nc = l // RET_CHUNK
    has_s0 = s0bd is not None
    cpt = 8 if nc >= 8 else nc
    npairs = max(1, 8 // cpt)
    ngrp = RET_HEADS // 2 // npairs
    w = npairs * LANES
    st_block = (2, 2 * npairs, RET_DIM, RET_DIM)
    if not has_s0:
        s0bd = jnp.zeros((1,) + st_block, F32)
        s0_spec = pl.BlockSpec((1,) + st_block, lambda g, i: (0, 0, 0, 0, 0))
    else:
        s0_spec = pl.BlockSpec((1, None) + st_block, lambda g, i: (i, layer, 0, g, 0, 0))
    th_lane = jnp.repeat(theta, RET_DIM, axis=1).reshape(2, 1, W_RET)
    th_bcast = jnp.broadcast_to(theta[:, :, None], (2, RET_HEADS, LANES))
    col = lambda c0: pl.BlockSpec((1, l, w), lambda g, i: (i, 0, c0 // w + g))
    o, sfin = pl.pallas_call(
        functools.partial(_ret_kernel, nc=nc, cpt=cpt, npairs=npairs, has_s0=has_s0),
        grid=(ngrp, b),
        in_specs=[col(COL_QR), col(COL_KR), col(COL_VR),
                  pl.BlockSpec((2, 1, w), lambda g, i: (0, 0, g)),
                  pl.BlockSpec((2, RET_HEADS, LANES), lambda g, i: (0, 0, 0)),
                  s0_spec],
        out_specs=[pl.BlockSpec((1, l, w), lambda g, i: (i, 0, g)),
                   pl.BlockSpec((1,) + st_block, lambda g, i: (i, 0, g, 0, 0))],
        out_shape=[jax.ShapeDtypeStruct((b, l, W_RET), F32),
                   jax.ShapeDtypeStruct((b, 2, RET_HEADS, RET_DIM, RET_DIM), F32)],
        scratch_shapes=[pltpu.VMEM((npairs, 2, _TAB_ROWS, LANES), F32)],
        compiler_params=_cparams(("arbitrary", "arbitrary")),
        name="retention_s0" if has_s0 else "retention",
    )(u, u, u, th_lane, th_bcast, s0bd)
    return o, sfin


def _filter_positions(l):
    f32 = np.float32
    t = np.linspace(0.0, 1.0, l, dtype=f32)[:, None]
    w = (f32(2.0 * math.pi) * np.arange(l, dtype=f32)[:, None] / f32(l)).astype(f32)
    f = np.linspace(1e-4, HY_BANDS - 1, HY_BANDS, dtype=f32)[None, :]
    z = np.concatenate([t, np.cos(f * w), -np.sin(f * w)], axis=-1).astype(f32)
    z = np.pad(z, ((0, 0), (0, 32 - HY_POS_FEAT)))
    return jnp.asarray(np.concatenate([z, z[:1], z[1:][::-1]], axis=0))


def _hyena_deltas():
    max_decay = math.log(HY_DECAY_TARGET) / HY_FAST_PCT
    min_decay = math.log(HY_DECAY_TARGET) / HY_SLOW_PCT
    return jnp.asarray(np.abs(np.linspace(min_decay, max_decay, W_HY, dtype=np.float32))[None, :])


def _filter_hidden(z_ref, w1_ref, b1_ref, fr_ref):
    pre = jnp.dot(z_ref[...], w1_ref[0], precision=HIGHEST, preferred_element_type=F32) + b1_ref[0]
    return jnp.sin(fr_ref[0] * pre)


def _filter_raw(hid, w2f, w2b, tp, dl, row0, l):
    win = jnp.exp(-tp * dl)
    row = row0 + lax.broadcasted_iota(jnp.int32, win.shape, 0)
    hf = jnp.dot(hid, w2f, precision=HIGHEST, preferred_element_type=F32) * win
    hb = jnp.dot(hid, w2b, precision=HIGHEST, preferred_element_type=F32) * win
    hf = jnp.where(row < l, hf, 0.0)
    hb = jnp.where((row > l) | (row == 0), hb, 0.0)
    return hf + hb, jnp.sum(jnp.abs(hf) + jnp.abs(hb), axis=0, keepdims=True)


def _with_skip(g, skip):
    row = lax.broadcasted_iota(jnp.int32, g.shape, 0)
    return g + jnp.where(row == 0, skip, 0.0)


def _filt_ctx_kernel(z_ref, w1_ref, b1_ref, fr_ref, w2_ref, dl_ref, sk_ref, fh_ref, fl_ref, g_ref):
    hid = _filter_hidden(z_ref, w1_ref, b1_ref, fr_ref)
    tp = z_ref[:, 0:1]
    for o in range(2):
        w2f = w2_ref[0, :, (2 * o) * W_HY:(2 * o + 1) * W_HY]
        w2b = w2_ref[0, :, (2 * o + 1) * W_HY:(2 * o + 2) * W_HY]
        raw, nrm = _filter_raw(hid, w2f, w2b, tp, dl_ref[...], 0, z_ref.shape[0] // 2)
        g = _with_skip(raw / nrm, sk_ref[0, pl.ds(o, 1), :])
        g_ref[0, o] = _dot3c(fh_ref[...], fl_ref[...], g)


def _ctx_dft_tables(l):
    n = 2 * l
    k = np.arange(n)[:, None]
    t = np.arange(l)[None, :]
    ang = 2.0 * np.pi * k * t / n
    c, s = np.cos(ang), np.sin(ang)
    fwd = np.block([[c, s], [-s, c]])
    inv = np.block([[c.T, -s.T], [s.T, c.T]])
    n_all = np.arange(n)[None, :]
    angg = 2.0 * np.pi * k * n_all / n
    filt = np.concatenate([np.cos(angg), -np.sin(angg)], axis=0) / n
    return _split_np(fwd), _split_np(inv), _split_np(filt)


def _filt_ctx_call(l, w1, b1, freq, w2, skip):
    n = 2 * l
    z_ext = _filter_positions(l)
    _, _, (fh, fl) = _ctx_dft_tables(l)
    w1p = jnp.pad(w1, ((0, 0), (0, 32 - HY_POS_FEAT), (0, 0)))
    lay = lambda *shape: pl.BlockSpec((1,) + shape, lambda d: (d,) + (0,) * len(shape))
    full = lambda a: pl.BlockSpec(a.shape, lambda d: (0,) * a.ndim)
    dl = _hyena_deltas()
    return pl.pallas_call(
        _filt_ctx_kernel,
        grid=(DEPTH,),
        in_specs=[full(z_ext), lay(32, HY_FILT_HID), lay(1, HY_FILT_HID), lay(1, HY_FILT_HID),
                  lay(HY_FILT_HID, 4 * W_HY), full(dl), lay(2, W_HY), full(fh), full(fl)],
        out_specs=pl.BlockSpec((1, 2, 2 * n, W_HY), lambda d: (d, 0, 0, 0)),
        out_shape=jax.ShapeDtypeStruct((DEPTH, 2, 2 * n, W_HY), F32),
        compiler_params=_cparams(("arbitrary",)),
        name="hyena_filter_ctx",
    )(z_ext, w1p, b1.reshape(DEPTH, 1, -1), freq.reshape(DEPTH, 1, -1), w2, dl, skip, fh, fl)


def _lat_dft_tables():
    ka = np.arange(NA)[:, None]
    b = np.arange(NB)[:, None, None]
    kb = np.arange(NB)[:, None]
    bb = np.arange(NB)[None, :]
    a_half = np.arange(NA // 2)[None, :]
    a_full = np.arange(NA)[None, :]
    phi = 2.0 * np.pi * (ka * a_half / NA + b * ka / LAT_N)
    c, s = np.cos(phi), np.sin(phi)
    a_fwd = np.concatenate([c, s], axis=2)
    ct, st = np.swapaxes(c, 1, 2), np.swapaxes(s, 1, 2)
    a_inv = np.concatenate([ct, st], axis=2)
    phig = 2.0 * np.pi * (ka * a_full / NA + b * ka / LAT_N)
    a_flt = np.concatenate([np.cos(phig), -np.sin(phig)], axis=1) / LAT_N
    ang = 2.0 * np.pi * kb * bb / NB
    c2, s2 = np.cos(ang), np.sin(ang)
    f_fwd = np.block([[c2, s2], [-s2, c2]])
    f_inv = np.block([[c2, -s2], [s2, c2]])
    return (_split_np(a_fwd), _split_np(a_inv), _split_np(a_flt), _split_np(f_fwd), _split_np(f_inv))


def _stage_b_rows(ka):
    re = pl.ds(ka, NB, stride=Y_PITCH)
    im = pl.ds(NA + ka, NB, stride=Y_PITCH)
    return re, im


def _filt_lat_kernel(z_ref, w1_ref, b1_ref, fr_ref, w2f_ref, w2b_ref, dl_ref, sk_ref, ah_ref, f2h_ref,
                     g_ref, hid_ref, gt_ref, y_ref):
    step = pl.program_id(1)
    rch = 1024
    nch = LAT_N // rch
    rows_of = lambda i: pl.ds(pl.multiple_of(i * rch, rch), rch)

    @pl.when(step == 0)
    def _():
        def hid_chunk(i, carry):
            r = rows_of(i)
            pre = jnp.dot(z_ref[r, :], w1_ref[0], precision=HIGHEST, preferred_element_type=F32)
            hid_ref[r, :] = jnp.sin(fr_ref[0] * (pre + b1_ref[0]))
            return carry

        lax.fori_loop(0, nch, hid_chunk, 0)

    w2f_hl = _split(w2f_ref[0])
    w2b_hl = _split(w2b_ref[0])

    def raw_chunk(w2_hl, i, nrm):
        r = rows_of(i)
        hh, hl = _split(hid_ref[r, :])
        h = _dot(hh, w2_hl[0]) + _dot(hl, w2_hl[0]) + _dot(hh, w2_hl[1])
        h = h * jnp.exp(-z_ref[r, 0:1] * dl_ref[...])
        row = i * rch + lax.broadcasted_iota(jnp.int32, h.shape, 0)
        h = jnp.where(row == LAT_L, 0.0, h)
        for s in range(rch // NB):
            slab = pl.ds(pl.multiple_of((i * (rch // NB) + s) * X_PITCH, 8), NB)
            gt_ref[slab, :] = h[s * NB:(s + 1) * NB]
        return nrm + jnp.sum(jnp.abs(h), axis=0, keepdims=True)

    nrm = lax.fori_loop(0, nch // 2, functools.partial(raw_chunk, w2f_hl), jnp.zeros((1, LANES), F32))
    nrm = lax.fori_loop(nch // 2, nch, functools.partial(raw_chunk, w2b_hl), nrm)
    hh, hl = _split(hid_ref[0:8, :])
    hb0 = _dot(hh, w2b_hl[0]) + _dot(hl, w2b_hl[0]) + _dot(hh, w2b_hl[1])
    hb0 = hb0 * jnp.exp(-z_ref[0:8, 0:1] * dl_ref[...])
    hb0 = jnp.where(lax.broadcasted_iota(jnp.int32, hb0.shape, 0) == 0, hb0, 0.0)
    gt_ref[0:8, :] = gt_ref[0:8, :] + hb0
    nrm = nrm + jnp.sum(jnp.abs(hb0), axis=0, keepdims=True)

    def norm_slab(a, carry):
        slab = pl.ds(pl.multiple_of(a * X_PITCH, 8), NB)
        gt_ref[slab, :] = gt_ref[slab, :] / nrm
        return carry

    lax.fori_loop(0, NA, norm_slab, 0, unroll=8)
    order = step // (W_HY // LANES)
    gt_ref[0:8, :] = _with_skip(gt_ref[0:8, :], sk_ref[0, pl.ds(order, 1), :])

    def stage_a(b, carry):
        rows = gt_ref[pl.ds(b, NA, stride=X_PITCH), :]
        y_ref[pl.ds(pl.multiple_of(b * Y_PITCH, 8), 2 * NA), :] = _dot1c(ah_ref[b], rows)
        return carry

    lax.fori_loop(0, NB, stage_a, 0, unroll=8)

    def stage_b(j, carry):
        ka = 2 * j
        re, im = _stage_b_rows(ka)
        re1, im1 = _stage_b_rows(ka + 1)
        z = jnp.concatenate([jnp.concatenate([y_ref[re, :], y_ref[im, :]], axis=0),
                             jnp.concatenate([y_ref[re1, :], y_ref[im1, :]], axis=0)], axis=1)
        x = _dot1c(f2h_ref[...], z)
        g0 = pl.multiple_of(ka * 2 * NB, 2 * NB)
        g_ref[0, 0, pl.ds(g0, 2 * NB), :] = x[:, :LANES]
        g_ref[0, 0, pl.ds(g0 + 2 * NB, 2 * NB), :] = x[:, LANES:]
        return carry

    lax.fori_loop(0, NA // 2, stage_b, 0, unroll=4)


def _filt_lat_call(w1, b1, freq, w2, skip):
    z_ext = _filter_positions(LAT_L)
    _, _, (ah, _), (f2h, _), _ = _lat_dft_tables()
    w1p = jnp.pad(w1, ((0, 0), (0, 32 - HY_POS_FEAT), (0, 0)))
    nct = W_HY // LANES
    one = pl.Buffered(1)
    lay = lambda *shape: pl.BlockSpec((1,) + shape, lambda d, s: (d,) + (0,) * len(shape))
    full = lambda a: pl.BlockSpec(a.shape, lambda d, s: (0,) * a.ndim, pipeline_mode=one)
    dl = _hyena_deltas()
    return pl.pallas_call(
        _filt_lat_kernel,
        grid=(DEPTH, 2 * nct),
        in_specs=[full(z_ext), lay(32, HY_FILT_HID), lay(1, HY_FILT_HID), lay(1, HY_FILT_HID),
                  pl.BlockSpec((1, HY_FILT_HID, LANES), lambda d, s: (d, 0, (s // nct) * 2 * nct + s % nct)),
                  pl.BlockSpec((1, HY_FILT_HID, LANES),
                               lambda d, s: (d, 0, (s // nct) * 2 * nct + nct + s % nct)),
                  pl.BlockSpec((1, LANES), lambda d, s: (0, s % nct)),
                  pl.BlockSpec((1, 2, LANES), lambda d, s: (d, 0, s % nct)),
                  full(ah), full(f2h)],
        out_specs=pl.BlockSpec((1, 1, NA * 2 * NB, LANES), lambda d, s: (d, s, 0, 0)),
        out_shape=jax.ShapeDtypeStruct((DEPTH, 2 * nct, NA * 2 * NB, LANES), F32),
        scratch_shapes=[pltpu.VMEM((LAT_N, HY_FILT_HID), F32), pltpu.VMEM((NA * X_PITCH, LANES), F32),
                        pltpu.VMEM((NB * Y_PITCH, LANES), F32)],
        compiler_params=_cparams(("arbitrary", "arbitrary")),
        name="hyena_filter_lat",
    )(z_ext, w1p, b1.reshape(DEPTH, 1, -1), freq.reshape(DEPTH, 1, -1), w2, w2, dl, skip, ah, f2h)


def _short_conv_rows(ref, bi, r0, rows, first, last, w):
    total = ref.shape[1]
    cur = ref[bi, pl.ds(r0, rows), :]
    before = ref[bi, pl.ds(jnp.maximum(r0 - 1, 0), 1), :]
    after = ref[bi, pl.ds(jnp.minimum(r0 + rows, total - 1), 1), :]
    before = jnp.where(first, 0.0, before)
    after = jnp.where(last, 0.0, after)
    rid = lax.broadcasted_iota(jnp.int32, cur.shape, 0)
    prev = jnp.where(rid == 0, before, pltpu.roll(cur, 1, 0))
    nxt = jnp.where(rid == rows - 1, after, pltpu.roll(cur, rows - 1, 0))
    return prev * w[0:1] + cur * w[1:2] + nxt * w[2:3]


def _short_conv_interior(ref, bi, r0, rows, w):
    prev = ref[bi, pl.ds(r0 - 1, rows), :]
    cur = ref[bi, pl.ds(r0, rows), :]
    nxt = ref[bi, pl.ds(r0 + 1, rows), :]
    return prev * w[0:1] + cur * w[1:2] + nxt * w[2:3]


def _cmul(xr, xi, gr, gi):
    return xr * gr - xi * gi, xr * gi + xi * gr


def _hy_ctx_kernel(v_ref, x1_ref, x2_ref, cw_ref, g_ref, fh_ref, ih_ref, o_ref):
    l = v_ref.shape[1]
    n = 2 * l

    def sc(ref, bi, grp):
        w = cw_ref[:, grp * W_HY:(grp + 1) * W_HY]
        return _short_conv_rows(ref, bi, 0, l, True, True, w)

    def conv(zr, zi, order):
        x = _dot1c(fh_ref[...], jnp.concatenate([zr, zi], axis=0))
        pr, pi = _cmul(x[:n], x[n:], g_ref[order, :n], g_ref[order, n:])
        y = _dot1c(ih_ref[...], jnp.concatenate([pr, pi], axis=0))
        return y[:l], y[l:]

    yr, yi = conv(sc(v_ref, 0, 0), sc(v_ref, 1, 0), 0)
    yr, yi = conv(sc(x1_ref, 0, 1) * yr, sc(x1_ref, 1, 1) * yi, 1)
    o_ref[0] = sc(x2_ref, 0, 2) * yr
    o_ref[1] = sc(x2_ref, 1, 2) * yi


def _hy_ctx_call(u, conv_w, g_spec, layer):
    b, l, _ = u.shape
    (fh, _), (ih, _), _ = _ctx_dft_tables(l)
    grp = lambda g: pl.BlockSpec((2, l, W_HY), lambda i: (i, 0, g))
    full = lambda a: pl.BlockSpec(a.shape, lambda i: (0,) * a.ndim)
    return pl.pallas_call(
        _hy_ctx_kernel,
        grid=(b // 2,),
        in_specs=[grp(0), grp(1), grp(2), full(conv_w),
                  pl.BlockSpec((None,) + g_spec.shape[1:], lambda i: (layer, 0, 0, 0)), full(fh), full(ih)],
        out_specs=pl.BlockSpec((2, l, W_HY), lambda i: (i, 0, 0)),
        out_shape=jax.ShapeDtypeStruct((b, l, W_HY), F32),
        compiler_params=_cparams(("parallel",)),
        name="hyena_ctx",
    )(u, u, u, conv_w, g_spec, fh, ih)


def _hy_lat_kernel(z_ref, m_ref, cw_ref, g_ref, af_ref, ai_ref, f2_ref, f3_ref, o_ref, xr_scr, xi_scr,
                   y_scr, *, conv_in):
    x_scr = (xr_scr, xi_scr)
    na_half = NA // 2
    w_in = cw_ref[0] if conv_in else None
    w_mul = cw_ref[1]

    def conv_slab(ref, bi, a, w):
        if isinstance(a, int):
            return _short_conv_rows(ref, bi, a * NB, NB, a == 0, a == na_half - 1, w)
        return _short_conv_interior(ref, bi, pl.multiple_of(a * NB, NB), NB, w)

    def edges_then_interior(body):
        body(0, 0)
        body(na_half - 1, 0)
        lax.fori_loop(1, na_half - 1, body, 0, unroll=2)

    def load_in(a, carry):
        for bi in range(2):
            if conv_in:
                val = conv_slab(z_ref, bi, a, w_in)
            else:
                val = z_ref[bi, pl.ds(pl.multiple_of(a * NB, NB), NB), :]
            x_scr[bi][pl.ds(pl.multiple_of(a * X_PITCH, 8), NB), :] = val
        return carry

    edges_then_interior(load_in)

    def stage_a(b, carry):
        zr = xr_scr[pl.ds(b, na_half, stride=X_PITCH), :]
        zi = xi_scr[pl.ds(b, na_half, stride=X_PITCH), :]
        rhs = jnp.concatenate([jnp.concatenate([zr, zi], axis=0), jnp.concatenate([zi, -zr], axis=0)],
                              axis=1)
        y = _dot1c(af_ref[b], rhs)
        r0 = pl.multiple_of(b * Y_PITCH, 8)
        y_scr[pl.ds(r0, NA), :] = y[:, :LANES]
        y_scr[pl.ds(r0 + NA, NA), :] = y[:, LANES:]
        return carry

    lax.fori_loop(0, NB, stage_a, 0, unroll=32)

    def spectrum_product(j):
        ka = 2 * j
        re, im = _stage_b_rows(ka)
        re1, im1 = _stage_b_rows(ka + 1)
        z = jnp.concatenate([jnp.concatenate([y_scr[re, :], y_scr[im, :]], axis=0),
                             jnp.concatenate([y_scr[re1, :], y_scr[im1, :]], axis=0)], axis=1)
        x = _dot1c(f2_ref[...], z)
        g0 = pl.multiple_of(ka * 2 * NB, 2 * NB)
        gr = jnp.concatenate([g_ref[pl.ds(g0, NB), :], g_ref[pl.ds(g0 + 2 * NB, NB), :]], axis=1)
        gi = jnp.concatenate([g_ref[pl.ds(g0 + NB, NB), :], g_ref[pl.ds(g0 + 3 * NB, NB), :]], axis=1)
        pr, pi = _cmul(x[:NB], x[NB:], gr, gi)
        return jnp.concatenate([pr, pi], axis=0).astype(BF16)

    def inverse_b(j, prod):
        ka = 2 * j
        re, im = _stage_b_rows(ka)
        re1, im1 = _stage_b_rows(ka + 1)
        u = _dot(f3_ref[...], prod)
        y_scr[re, :] = u[:NB, :LANES]
        y_scr[im, :] = u[NB:, :LANES]
        y_scr[re1, :] = u[:NB, LANES:]
        y_scr[im1, :] = u[NB:, LANES:]

    def stage_b(j, prod):
        nxt = spectrum_product(j + 1)
        inverse_b(j, prod)
        return nxt

    last = lax.fori_loop(0, NA // 2 - 1, stage_b, spectrum_product(0), unroll=4)
    inverse_b(NA // 2 - 1, last)

    def stage_c(b, carry):
        r0 = pl.multiple_of(b * Y_PITCH, 8)
        ur = y_scr[pl.ds(r0, NA), :]
        ui = y_scr[pl.ds(r0 + NA, NA), :]
        rhs = jnp.concatenate([jnp.concatenate([ur, -ui], axis=0), jnp.concatenate([ui, ur], axis=0)],
                              axis=1)
        y = _dot1c(ai_ref[b], rhs)
        xr_scr[pl.ds(b, na_half, stride=X_PITCH), :] = y[:, :LANES]
        xi_scr[pl.ds(b, na_half, stride=X_PITCH), :] = y[:, LANES:]
        return carry

    lax.fori_loop(0, NB, stage_c, 0, unroll=32)

    def store_out(a, carry):
        for bi in range(2):
            mul = conv_slab(m_ref, bi, a, w_mul)
            o_ref[bi, pl.ds(pl.multiple_of(a * NB, NB), NB), :] = (
                x_scr[bi][pl.ds(pl.multiple_of(a * X_PITCH, 8), NB), :] * mul)
        return carry

    edges_then_interior(store_out)


def _hy_lat_call(src, src_col, u, mul_col, conv_w2, g_spec, layer, order, *, conv_in):
    b, l, _ = u.shape
    nct = W_HY // LANES
    (af, _), (ai, _), _, (f2, _), (f3, _) = _lat_dft_tables()
    one = pl.Buffered(1)
    blk = lambda col: pl.BlockSpec((2, l, LANES), lambda c, p: (p, 0, col + c))
    const = lambda a: pl.BlockSpec(a.shape, lambda c, p: (0,) * a.ndim, pipeline_mode=one)
    return pl.pallas_call(
        functools.partial(_hy_lat_kernel, conv_in=conv_in),
        grid=(nct, b // 2),
        in_specs=[blk(src_col), blk(mul_col),
                  pl.BlockSpec((2, 3, LANES), lambda c, p: (0, 0, c)),
                  pl.BlockSpec((None, None, NA * 2 * NB, LANES), lambda c, p: (layer, order * nct + c, 0, 0)),
                  const(af), const(ai), const(f2), const(f3)],
        out_specs=pl.BlockSpec((2, l, LANES), lambda c, p: (p, 0, c)),
        out_shape=jax.ShapeDtypeStruct((b, l, W_HY), F32),
        scratch_shapes=[pltpu.VMEM(((NA // 2) * X_PITCH, LANES), F32),
                        pltpu.VMEM(((NA // 2) * X_PITCH, LANES), F32),
                        pltpu.VMEM((NB * Y_PITCH, LANES), F32)],
        compiler_params=_cparams(("arbitrary", "arbitrary")),
        name="hyena_lat_conv_in" if conv_in else "hyena_lat",
    )(src, u, conv_w2, g_spec, af, ai, f2, f3)


def _rope_tables(l):
    f32 = np.float32
    rows = l // GRID_W
    row = np.repeat(np.arange(rows), GRID_W).astype(f32)
    col = np.tile(np.arange(GRID_W), rows).astype(f32)
    quarter = HEAD_DIM // 4
    inv = np.power(f32(ROPE_BASE), -np.arange(quarter, dtype=f32) / f32(quarter)).astype(f32)
    ang = np.concatenate([row[:, None] * inv, col[:, None] * inv], axis=-1).astype(f32)
    cos, sin = np.cos(ang), np.sin(ang)
    q = quarter
    cos_h = np.concatenate([cos[:, :q], cos[:, :q], cos[:, q:], cos[:, q:]], axis=-1)
    sin_h = np.concatenate([-sin[:, :q], sin[:, :q], -sin[:, q:], sin[:, q:]], axis=-1)
    return jnp.asarray(np.tile(cos_h, (1, 2))), jnp.asarray(np.tile(sin_h, (1, 2)))


def kernel(x_prompt, x_sample, c, cache_k, cache_v, state_ret, c_ctx, norm_w, w_mod, b_mod, w_in, hy_conv,
           hy_filt_w1, hy_filt_b1, hy_filt_freq, hy_filt_w2, hy_skip, attn_sink, ret_theta, ret_gn,
           w_branch_a, w_branch_b, w_branch_c, w_merge, b_merge, w_out, final_norm_w):
    d = D_MODEL
    bc, lc, _ = x_prompt.shape
    bl, ll, _ = x_sample.shape
    assert ll == LAT_L and bc % 2 == 0 and bl % 2 == 0
    past = cache_k.shape[2]

    cond = jnp.zeros((16, d), F32).at[:bl].set(c).at[bl].set(c_ctx)
    mod = _mod_call(cond, w_mod, b_mod)

    g_ctx = _filt_ctx_call(lc, hy_filt_w1, hy_filt_b1, hy_filt_freq, hy_filt_w2, hy_skip)
    g_lat = _filt_lat_call(hy_filt_w1, hy_filt_b1, hy_filt_freq, hy_filt_w2, hy_skip)

    cos_t, sin_t = _rope_tables(ll)
    w_in_b = w_in.astype(BF16)
    wm_b = w_merge.astype(BF16)
    wa_b = w_branch_a.astype(BF16)
    wb_b = w_branch_b.astype(BF16)
    wc_b = w_branch_c.astype(BF16)
    wo_b = w_out.astype(BF16)
    fnw = final_norm_w.reshape(1, d)
    k_ctx = cache_k.reshape(bl, DEPTH, past, W_KV)
    v_ctx = cache_v.reshape(bl, DEPTH, past, W_KV)
    hy_cols = COL_HY // LANES
    nct = W_HY // LANES

    xp, xs = x_prompt, x_sample
    ks_out, vs_out, ss_out = [], [], []
    for l in range(DEPTH):
        final = l == DEPTH - 1
        nw = norm_w[l].reshape(1, d)
        bm = b_merge[l].reshape(1, -1)
        gn = ret_gn[l].reshape(1, W_RET)
        shift, scale, gate = (mod[l, :, i * d:(i + 1) * d][:, None, :] for i in range(3))
        conv_w = hy_conv[l]
        cw = lambda g: conv_w[:, g * W_HY:(g + 1) * W_HY]

        sl = slice(bl, bl + 1)
        u = _in_call(xp, shift[sl], scale[sl], nw, w_in_b[l], cos_t, sin_t, rope=False)
        ya = _hy_ctx_call(u, conv_w, g_ctx, l)
        yb = _attn_ctx_call(u, attn_sink[l])
        yc, sfin = _ret_call(u, ret_theta[l], None)
        res = _out_call(xp, shift[sl], scale[sl], gate[sl], nw, ya, u, yb, yc, gn, wm_b[l], bm, wa_b[l], wb_b[l],
                        wc_b[l], wo_b[l], fnw, final=final)
        xp = res[0]
        if final:
            y_prompt = res[1]
        ks_out.append(u[:, :, COL_KA:COL_KA + W_KV])
        vs_out.append(u[:, :, COL_VA:COL_VA + W_KV])
        ss_out.append(sfin)

        sl = slice(0, bl)
        u = _in_call(xs, shift[sl], scale[sl], nw, w_in_b[l], cos_t, sin_t, rope=True)
        z1 = _hy_lat_call(u, hy_cols, u, hy_cols + nct, jnp.stack([cw(0), cw(1)]), g_lat, l, 0, conv_in=True)
        ya = _hy_lat_call(z1, 0, u, hy_cols + 2 * nct, jnp.stack([cw(2), cw(2)]), g_lat, l, 1, conv_in=False)
        yb = _attn_lat_call(u, k_ctx, v_ctx, attn_sink[l], l)
        yc, _ = _ret_call(u, ret_theta[l], state_ret, l)
        res = _out_call(xs, shift[sl], scale[sl], gate[sl], nw, ya, u, yb, yc, gn, wm_b[l], bm, wa_b[l], wb_b[l],
                        wc_b[l], wo_b[l], fnw, final=final)
        xs = res[0]
        if final:
            y_sample = res[1]

    kv_shape = (bc, DEPTH, lc, ATT_KV_HEADS, HEAD_DIM)
    new_cache_k = jnp.stack(ks_out, axis=1).reshape(kv_shape)
    new_cache_v = jnp.stack(vs_out, axis=1).reshape(kv_shape)
    new_state_ret = jnp.stack(ss_out, axis=1)
    return (y_prompt, y_sample, new_cache_k, new_cache_v, new_state_ret)
```

```python
import functools
import math

import numpy as np
import jax
import jax.numpy as jnp
from jax import lax
from jax.experimental import pallas as pl
from jax.experimental.pallas import tpu as pltpu

F32 = jnp.float32
BF16 = jnp.bfloat16
HIGHEST = lax.Precision.HIGHEST

D_MODEL = 1024
DEPTH = 4
GRID_W = 64
W_HY = 512
HY_BANDS = 8
HY_POS_FEAT = 1 + 2 * HY_BANDS
HY_FILT_HID = 64
HY_DECAY_TARGET = 1e-2
HY_FAST_PCT = 0.3
HY_SLOW_PCT = 1.5
ATT_HEADS = 8
ATT_KV_HEADS = 2
ATT_GROUP = ATT_HEADS // ATT_KV_HEADS
HEAD_DIM = 64
W_ATT = ATT_HEADS * HEAD_DIM
W_KV = ATT_KV_HEADS * HEAD_DIM
ATT_BLOCK = 128
ATT_QB = 8
RET_HEADS = 8
RET_DIM = 64
W_RET = RET_HEADS * RET_DIM
RET_CHUNK = 128
ROPE_BASE = 10000.0
EPS = 1e-6
NEG = -1e30

LANES = 128
MXU_ROWS = 512
VMEM_LIMIT = 58 * 1024 * 1024

IN_DIM = 5376
COL_HY = 0
COL_GH = 1536
COL_QA = 2048
COL_GA = 2560
COL_QR = 3072
COL_KR = 3584
COL_VR = 4096
COL_GR = 4608
COL_KA = 5120
COL_VA = 5248
_W_COL = {COL_HY: 0, COL_HY + 512: 512, COL_HY + 1024: 1024, COL_GH: 1536, COL_QA: 2048, COL_KA: 2560,
          COL_VA: 2688, COL_GA: 2816, COL_QR: 3328, COL_KR: 3840, COL_VR: 4352, COL_GR: 4864}

LAT_L = 4096
LAT_N = 2 * LAT_L
NA = 64
NB = 128
Y_PITCH = 136
X_PITCH = 136


def _cparams(sem):
    return pltpu.CompilerParams(dimension_semantics=sem, vmem_limit_bytes=VMEM_LIMIT)


def _split_np(a):
    a32 = np.asarray(a, np.float32)
    hi = a32.astype(BF16)
    lo = (a32 - hi.astype(np.float32)).astype(BF16)
    return jnp.asarray(hi), jnp.asarray(lo)


def _split(x):
    hi = x.astype(BF16)
    lo = (x - hi.astype(F32)).astype(BF16)
    return hi, lo


def _dot(a, b):
    return jnp.dot(a, b, preferred_element_type=F32)


def _dot3c(chi, clo, x):
    xh, xl = _split(x)
    return _dot(chi, xh) + _dot(clo, xh) + _dot(chi, xl)


def _dot1c(chi, x):
    return _dot(chi, x.astype(BF16))


def _silu(x):
    return x * jax.nn.sigmoid(x)


def _mod_kernel(c_ref, w_ref, b_ref, o_ref):
    s = _silu(c_ref[...])
    o_ref[0] = jnp.dot(s, w_ref[0], precision=HIGHEST, preferred_element_type=F32) + b_ref[0]


def _mod_call(cond, w_mod, b_mod):
    rows, d = cond.shape
    n = w_mod.shape[-1]
    tn = 1024
    return pl.pallas_call(
        _mod_kernel,
        grid=(DEPTH, n // tn),
        in_specs=[pl.BlockSpec((rows, d), lambda l, j: (0, 0)),
                  pl.BlockSpec((1, d, tn), lambda l, j: (l, 0, j)),
                  pl.BlockSpec((1, 1, tn), lambda l, j: (l, 0, j))],
        out_specs=pl.BlockSpec((1, rows, tn), lambda l, j: (l, 0, j)),
        out_shape=jax.ShapeDtypeStruct((DEPTH, rows, n), F32),
        compiler_params=_cparams(("arbitrary", "arbitrary")),
        name="adaln_mod",
    )(cond, w_mod, b_mod.reshape(DEPTH, 1, n))


def _modulated(x, nw, scale, shift):
    ms = jnp.mean(x * x, axis=-1, keepdims=True)
    h = x * lax.rsqrt(ms + EPS) * nw
    return h * (1.0 + scale) + shift


def _rope128(x, cos, sin_signed, first_half):
    up = pltpu.roll(x, LANES - 16, 1)
    dn = pltpu.roll(x, 16, 1)
    return x * cos + jnp.where(first_half, up, dn) * sin_signed


def _rows(ref):
    bt, tm, w = ref.shape
    return ref[...].reshape(bt * tm, w)


def _put(ref, c0, val):
    bt, tm, _ = ref.shape
    ref[:, :, c0:c0 + val.shape[1]] = val.reshape(bt, tm, val.shape[1])


def _in_kernel(x_ref, shift_ref, scale_ref, nw_ref, w_ref, cos_ref, sin_ref, o_ref, *, rope):
    x = _rows(x_ref)
    rows = x.shape[0]
    half = rows // 2
    hbs = [_modulated(x[r:r + half], nw_ref[...], scale_ref[0], shift_ref[0]).astype(BF16)
           for r in (0, half)]
    if rope:
        cos = cos_ref[...]
        sin = sin_ref[...]
        lane = lax.broadcasted_iota(jnp.int32, (rows, LANES), 1)
        first_half = (lane % 32) < 16

    def seg(c0, width):
        w0 = _W_COL[c0]
        return jnp.concatenate([_dot(hb, w_ref[:, w0:w0 + width]) for hb in hbs], axis=0)

    def put_rope(c0, val, mul):
        for i in range(val.shape[1] // LANES):
            piece = val[:, i * LANES:(i + 1) * LANES]
            if rope:
                piece = _rope128(piece, cos, sin, first_half)
            if mul is not None:
                piece = piece * mul
            _put(o_ref, c0 + i * LANES, piece)

    for g in range(3):
        _put(o_ref, COL_HY + g * 512, seg(COL_HY + g * 512, 512))
    _put(o_ref, COL_GH, _silu(seg(COL_GH, 512)))
    put_rope(COL_QA, seg(COL_QA, 512), None)
    _put(o_ref, COL_GA, _silu(seg(COL_GA, 512)))
    put_rope(COL_QR, seg(COL_QR, 512), None)
    put_rope(COL_KR, seg(COL_KR, 512), RET_DIM ** -0.5)
    _put(o_ref, COL_VR, seg(COL_VR, 512))
    _put(o_ref, COL_GR, _silu(seg(COL_GR, 512)))
    put_rope(COL_KA, seg(COL_KA, 128), None)
    _put(o_ref, COL_VA, seg(COL_VA, 128))


def _token_tiling(b, l, per_batch):
    if l >= MXU_ROWS:
        return 1, MXU_ROWS
    bt = 1 if per_batch else min(b, MXU_ROWS // l)
    return bt, l


def _in_call(x, shift, scale, nw, w, cos_t, sin_t, *, rope):
    b, l, d = x.shape
    per_batch = shift.shape[0] > 1
    bt, tm = _token_tiling(b, l, per_batch)
    assert not rope or bt == 1
    mod_map = (lambda i, j: (i, 0, 0)) if per_batch else (lambda i, j: (0, 0, 0))
    return pl.pallas_call(
        functools.partial(_in_kernel, rope=rope),
        grid=(b // bt, l // tm),
        in_specs=[pl.BlockSpec((bt, tm, d), lambda i, j: (i, j, 0)),
                  pl.BlockSpec((1, 1, d), mod_map),
                  pl.BlockSpec((1, 1, d), mod_map),
                  pl.BlockSpec((1, d), lambda i, j: (0, 0)),
                  pl.BlockSpec((d, IN_DIM), lambda i, j: (0, 0), pipeline_mode=pl.Buffered(1)),
                  pl.BlockSpec((tm, LANES), lambda i, j: (j, 0)),
                  pl.BlockSpec((tm, LANES), lambda i, j: (j, 0))],
        out_specs=pl.BlockSpec((bt, tm, IN_DIM), lambda i, j: (i, j, 0)),
        out_shape=jax.ShapeDtypeStruct((b, l, IN_DIM), F32),
        compiler_params=_cparams(("parallel", "parallel")),
        name="in_proj_rope" if rope else "in_proj",
    )(x, shift, scale, nw, w, cos_t, sin_t)


def _retention_post(o, gn, gate):
    lane = lax.broadcasted_iota(jnp.int32, (1, LANES), 1)
    lo_head = lane < RET_DIM
    outs = []
    for t in range(W_RET // LANES):
        sl = slice(t * LANES, (t + 1) * LANES)
        ot = o[:, sl]
        o2 = ot * ot
        s_lo = jnp.sum(jnp.where(lo_head, o2, 0.0), axis=-1, keepdims=True)
        s_hi = jnp.sum(jnp.where(lo_head, 0.0, o2), axis=-1, keepdims=True)
        ms = jnp.where(lo_head, s_lo, s_hi) * (1.0 / RET_DIM)
        outs.append(ot * lax.rsqrt(ms + EPS) * gn[:, sl] * gate[:, sl])
    return jnp.concatenate(outs, axis=1)


def _out_kernel(x_ref, shift_ref, scale_ref, gate_ref, nw_ref, ya_ref, gh_ref, yb_ref, yc_ref, gr_ref, gn_ref,
                wm_ref, bm_ref, wa_ref, wb_ref, wc_ref, wo_ref, fnw_ref, *out_refs, final):
    x = _rows(x_ref)
    d = x.shape[1]
    hb = _modulated(x, nw_ref[...], scale_ref[0], shift_ref[0]).astype(BF16)
    branches = (_rows(ya_ref) * _rows(gh_ref), _rows(yb_ref),
                _retention_post(_rows(yc_ref), gn_ref[...], _rows(gr_ref)))
    merged = None
    for i, (y, w_ref) in enumerate(zip(branches, (wa_ref, wb_ref, wc_ref))):
        g = jax.nn.sigmoid(_dot(hb, wm_ref[:, i * d:(i + 1) * d]) + bm_ref[:, i * d:(i + 1) * d])
        term = g * _dot(y.astype(BF16), w_ref[...])
        merged = term if merged is None else merged + term
    out = _dot(merged.astype(BF16), wo_ref[...])
    xn = x + gate_ref[0] * out
    _put(out_refs[0], 0, xn)
    if final:
        ms = jnp.mean(xn * xn, axis=-1, keepdims=True)
        _put(out_refs[1], 0, xn * lax.rsqrt(ms + EPS) * fnw_ref[...])


def _out_call(x, shift, scale, gate, nw, ya, u, yb, yc, gn, wm, bm, wa, wb, wc, wo, fnw, *, final):
    b, l, d = x.shape
    per_batch = shift.shape[0] > 1
    bt, tm = _token_tiling(b, l, per_batch)
    mod_map = (lambda i, j: (i, 0, 0)) if per_batch else (lambda i, j: (0, 0, 0))
    tok = lambda w: pl.BlockSpec((bt, tm, w), lambda i, j: (i, j, 0))
    full = lambda a: pl.BlockSpec(a.shape, lambda i, j: (0,) * a.ndim, pipeline_mode=pl.Buffered(1))
    n_out = 2 if final else 1
    res = pl.pallas_call(
        functools.partial(_out_kernel, final=final),
        grid=(b // bt, l // tm),
        in_specs=[tok(d), pl.BlockSpec((1, 1, d), mod_map), pl.BlockSpec((1, 1, d), mod_map),
                  pl.BlockSpec((1, 1, d), mod_map), full(nw), tok(W_HY),
                  pl.BlockSpec((bt, tm, W_HY), lambda i, j: (i, j, COL_GH // W_HY)), tok(W_ATT), tok(W_RET),
                  pl.BlockSpec((bt, tm, W_RET), lambda i, j: (i, j, COL_GR // W_RET)), full(gn),
                  full(wm), full(bm), full(wa), full(wb), full(wc), full(wo), full(fnw)],
        out_specs=[tok(d)] * n_out,
        out_shape=[jax.ShapeDtypeStruct((b, l, d), F32)] * n_out,
        compiler_params=_cparams(("parallel", "parallel")),
        name="merge_out_final" if final else "merge_out",
    )(x, shift, scale, gate, nw, ya, u, yb, yc, u, gn, wm, bm, wa, wb, wc, wo, fnw)
    return res


_NT = (((1,), (1,)), ((), ()))


_TN = (((0,), (0,)), ((), ()))
LOG2E = 1.4426950408889634
Q_SCALE = (HEAD_DIM ** -0.5) * LOG2E


def _attn_scores_t(q, kh, kv):
    h0 = kv * ATT_GROUP
    qs = jnp.concatenate([q[:, (h0 + g) * HEAD_DIM:(h0 + g + 1) * HEAD_DIM] for g in range(ATT_GROUP)],
                         axis=0).astype(BF16)
    return lax.dot_general(kh, qs, _NT, preferred_element_type=F32)


def _attn_finish_t(sink_ref, s, vh, g_ref, o_ref, kv, row0=0):
    tk, cols = s.shape
    t = cols // ATT_GROUP
    h0 = kv * ATT_GROUP
    head = lax.broadcasted_iota(jnp.int32, (1, cols), 1) // t
    sink = jnp.full((1, cols), sink_ref[h0], F32)
    for g in range(1, ATT_GROUP):
        sink = jnp.where(head == g, sink_ref[h0 + g], sink)
    sink = sink * LOG2E
    m = jnp.maximum(jnp.max(s, axis=0, keepdims=True), sink)
    p = jnp.exp2(s - m).astype(BF16)
    v_ext = jnp.concatenate([vh, jnp.ones((tk, HEAD_DIM), BF16)], axis=1)
    o_ext = lax.dot_general(v_ext, p, _TN, preferred_element_type=F32)
    denom = o_ext[HEAD_DIM:HEAD_DIM + 1] + jnp.exp2(sink - m)
    o = o_ext[:HEAD_DIM] / denom
    for gp in range(ATT_GROUP // 2):
        pair = jnp.concatenate([o[:, (2 * gp) * t:(2 * gp + 1) * t], o[:, (2 * gp + 1) * t:(2 * gp + 2) * t]],
                               axis=0)
        c0 = (h0 + 2 * gp) * HEAD_DIM
        o_ref[0, row0:row0 + t, c0:c0 + 2 * HEAD_DIM] = pair.T * g_ref[0, row0:row0 + t, c0:c0 + 2 * HEAD_DIM]


def _attn_ctx_kernel(sink_ref, q_ref, k_ref, v_ref, g_ref, o_ref):
    q = q_ref[0] * Q_SCALE
    k = k_ref[0].astype(BF16)
    v = v_ref[0].astype(BF16)
    scores = [_attn_scores_t(q, k[:, kv * HEAD_DIM:(kv + 1) * HEAD_DIM], kv) for kv in range(ATT_KV_HEADS)]
    for kv in range(ATT_KV_HEADS):
        _attn_finish_t(sink_ref, scores[kv], v[:, kv * HEAD_DIM:(kv + 1) * HEAD_DIM], g_ref, o_ref, kv)


def _attn_ctx_call(u, sink):
    b, l, _ = u.shape
    return pl.pallas_call(
        _attn_ctx_kernel,
        grid=(b,),
        in_specs=[pl.BlockSpec(memory_space=pltpu.SMEM),
                  pl.BlockSpec((1, l, W_ATT), lambda i: (i, 0, COL_QA // W_ATT)),
                  pl.BlockSpec((1, l, W_KV), lambda i: (i, 0, COL_KA // W_KV)),
                  pl.BlockSpec((1, l, W_KV), lambda i: (i, 0, COL_VA // W_KV)),
                  pl.BlockSpec((1, l, W_ATT), lambda i: (i, 0, COL_GA // W_ATT))],
        out_specs=pl.BlockSpec((1, l, W_ATT), lambda i: (i, 0, 0)),
        out_shape=jax.ShapeDtypeStruct((b, l, W_ATT), F32),
        compiler_params=_cparams(("parallel",)),
        name="attn_ctx",
    )(sink, u, u, u, u)


def _attn_lat_kernel(sink_ref, q_ref, kp_ref, kc_ref, kn_ref, vp_ref, vc_ref, vn_ref, kx_ref, vx_ref,
                     g_ref, o_ref):
    j = pl.program_id(1)
    last = pl.num_programs(1) - 1
    b = ATT_BLOCK
    nq = ATT_QB
    bf = lambda ref: ref[0].astype(BF16)
    kx, vx = bf(kx_ref), bf(vx_ref)
    kblk = [bf(kp_ref)] + [kc_ref[0, t * b:(t + 1) * b, :].astype(BF16) for t in range(nq)] + [bf(kn_ref)]
    vblk = [bf(vp_ref)] + [vc_ref[0, t * b:(t + 1) * b, :].astype(BF16) for t in range(nq)] + [bf(vn_ref)]
    keys = [jnp.concatenate(kblk[t:t + 3] + [kx], axis=0) for t in range(nq)]
    vals = [jnp.concatenate(vblk[t:t + 3] + [vx], axis=0) for t in range(nq)]
    cols = ATT_GROUP * b
    c = lax.broadcasted_iota(jnp.int32, (b, cols), 0)
    r = lax.broadcasted_iota(jnp.int32, (b, cols), 1) % b
    ok_prev = [(c >= r) & (j > 0)] + [c >= r] * (nq - 1)
    ok_next = [c <= r] * (nq - 1) + [(c <= r) & (j < last)]

    def band(s, t):
        return jnp.concatenate([jnp.where(ok_prev[t], s[:b], NEG), s[b:2 * b],
                                jnp.where(ok_next[t], s[2 * b:3 * b], NEG), s[3 * b:]], axis=0)

    chains = [(t, kv) for t in range(nq) for kv in range(ATT_KV_HEADS)]
    scores = []
    for t, kv in chains:
        q = q_ref[0, t * b:(t + 1) * b, :] * Q_SCALE
        scores.append(band(_attn_scores_t(q, keys[t][:, kv * HEAD_DIM:(kv + 1) * HEAD_DIM], kv), t))
    for (t, kv), s in zip(chains, scores):
        _attn_finish_t(sink_ref, s, vals[t][:, kv * HEAD_DIM:(kv + 1) * HEAD_DIM], g_ref, o_ref, kv, t * b)


def _attn_lat_call(u, kctx, vctx, sink, layer):
    b, l, _ = u.shape
    nb = l // ATT_BLOCK
    past = kctx.shape[2]
    kcol = COL_KA // W_KV
    vcol = COL_VA // W_KV
    nq = ATT_QB
    prev = lambda col: pl.BlockSpec((1, ATT_BLOCK, W_KV), lambda i, j: (i, jnp.maximum(nq * j - 1, 0), col))
    cur = lambda col: pl.BlockSpec((1, nq * ATT_BLOCK, W_KV), lambda i, j: (i, j, col))
    nxt = lambda col: pl.BlockSpec((1, ATT_BLOCK, W_KV),
                                   lambda i, j: (i, jnp.minimum(nq * j + nq, nb - 1), col))
    ctx = pl.BlockSpec((1, None, past, W_KV), lambda i, j: (i, layer, 0, 0))
    return pl.pallas_call(
        _attn_lat_kernel,
        grid=(b, nb // nq),
        in_specs=[pl.BlockSpec(memory_space=pltpu.SMEM),
                  pl.BlockSpec((1, nq * ATT_BLOCK, W_ATT), lambda i, j: (i, j, COL_QA // W_ATT)),
                  prev(kcol), cur(kcol), nxt(kcol), prev(vcol), cur(vcol), nxt(vcol), ctx, ctx,
                  pl.BlockSpec((1, nq * ATT_BLOCK, W_ATT), lambda i, j: (i, j, COL_GA // W_ATT))],
        out_specs=pl.BlockSpec((1, nq * ATT_BLOCK, W_ATT), lambda i, j: (i, j, 0)),
        out_shape=jax.ShapeDtypeStruct((b, l, W_ATT), F32),
        compiler_params=_cparams(("parallel", "parallel")),
        name="attn_lat",
    )(sink, u, u, u, u, u, u, u, kctx, vctx, u)


def _log_sigmoid(x):
    return jnp.minimum(x, 0.0) - jnp.log1p(jnp.exp(-jnp.abs(x)))


_TAB_DMAT = 0
_TAB_QDEC = 2 * RET_CHUNK
_TAB_KDEC = 3 * RET_CHUNK
_TAB_CDEC = 4 * RET_CHUNK
_TAB_ROWS = 4 * RET_CHUNK + 8


def _ret_kernel(q_ref, k_ref, v_ref, thl_ref, thb_ref, s0_ref, o_ref, sfin_ref, tab_ref, *, nc, cpt,
                npairs, has_s0):
    grp = pl.program_id(0)
    c = RET_CHUNK
    lane = lax.broadcasted_iota(jnp.int32, (1, LANES), 1)
    lo_head = lane < RET_DIM
    dd = lax.broadcasted_iota(jnp.int32, (LANES, LANES), 0)
    ee = lax.broadcasted_iota(jnp.int32, (LANES, LANES), 1)
    same_head = (dd < RET_DIM) == (ee < RET_DIM)
    lanes_of = lambda p: slice(p * LANES, (p + 1) * LANES)

    @pl.when(pl.program_id(1) == 0)
    def _():
        rowf = lax.broadcasted_iota(jnp.int32, (c, LANES), 0).astype(F32)
        ii = lax.broadcasted_iota(jnp.int32, (c, c), 0)
        jj = lax.broadcasted_iota(jnp.int32, (c, c), 1)
        for p in range(npairs):
            for d in range(2):
                lg_lane = _log_sigmoid(thl_ref[d, :, lanes_of(p)])
                dist = (ii - jj) if d == 0 else (jj - ii)
                for hh in range(2):
                    head = 2 * (grp * npairs + p) + hh
                    lg_h = _log_sigmoid(thb_ref[d, pl.ds(head, 1), :])
                    dm = jnp.where(dist >= 0, jnp.exp(lg_h * jnp.maximum(dist, 0).astype(F32)), 0.0)
                    tab_ref[p, d, _TAB_DMAT + hh * c:_TAB_DMAT + (hh + 1) * c, :] = dm
                if d == 0:
                    q_dec = jnp.exp(lg_lane * (rowf + 1.0))
                    k_dec = jnp.exp(lg_lane * (c - 1.0 - rowf))
                else:
                    q_dec = jnp.exp(lg_lane * (c - rowf))
                    k_dec = jnp.exp(lg_lane * rowf)
                tab_ref[p, d, _TAB_QDEC:_TAB_QDEC + c, :] = q_dec
                tab_ref[p, d, _TAB_KDEC:_TAB_KDEC + c, :] = k_dec
                tab_ref[p, d, _TAB_CDEC:_TAB_CDEC + 8, :] = jnp.broadcast_to(
                    jnp.exp(lg_lane * float(c)), (8, LANES))

    def first_level(p, d, r0):
        qc = q_ref[0, pl.ds(r0, c), lanes_of(p)]
        kc = k_ref[0, pl.ds(r0, c), lanes_of(p)]
        vcb = v_ref[0, pl.ds(r0, c), lanes_of(p)].astype(BF16)
        qs = jnp.concatenate([jnp.where(lo_head, qc, 0.0), jnp.where(lo_head, 0.0, qc)], axis=0)
        sc = lax.dot_general(qs.astype(BF16), kc.astype(BF16), _NT, preferred_element_type=F32)
        sc = sc * tab_ref[p, d, _TAB_DMAT:_TAB_DMAT + 2 * c, :]
        kd = kc * tab_ref[p, d, _TAB_KDEC:_TAB_KDEC + c, :]
        upd = jnp.where(same_head, _dot(kd.T.astype(BF16), vcb), 0.0)
        return qc, vcb, sc.astype(BF16), upd

    def second_level(p, d, lvl1, s):
        qc, vcb, scb, upd = lvl1
        pv = _dot(scb, vcb)
        qd = qc * tab_ref[p, d, _TAB_QDEC:_TAB_QDEC + c, :]
        o = _dot(qd.astype(BF16), s.astype(BF16)) + jnp.where(lo_head, pv[:c], pv[c:])
        return o, tab_ref[p, d, _TAB_CDEC:_TAB_CDEC + 1, :] * s + upd

    def init_state(p, d):
        if not has_s0:
            return jnp.zeros((LANES, LANES), F32)
        z = jnp.zeros((RET_DIM, RET_DIM), F32)
        return jnp.concatenate([jnp.concatenate([s0_ref[0, d, 2 * p], z], axis=1),
                                jnp.concatenate([z, s0_ref[0, d, 2 * p + 1]], axis=1)], axis=0)

    units = [(p, d) for p in range(npairs) for d in range(2)]

    def scan_body(second, n, states):
        def row0(d, j):
            idx = n * cpt + j
            return pl.multiple_of((idx if d == 0 else nc - 1 - idx) * c, c)

        lvl = {(p, d, j): first_level(p, d, row0(d, j)) for j in range(cpt) for p, d in units}
        states = list(states)
        for j in range(cpt):
            for ui, (p, d) in enumerate(units):
                o, states[ui] = second_level(p, d, lvl[(p, d, j)], states[ui])
                dst = (0, pl.ds(row0(d, j), c), lanes_of(p))
                if second(j):
                    o = o + o_ref[dst]
                o_ref[dst] = o
        return tuple(states)

    assert nc % cpt == 0
    trips = nc // cpt
    states = tuple(init_state(p, d) for p, d in units)
    if trips == 1:
        states = scan_body(lambda j: 2 * j > nc - 1, 0, states)
    else:
        assert trips % 2 == 0
        states = lax.fori_loop(0, trips // 2, functools.partial(scan_body, lambda j: False), states)
        states = lax.fori_loop(trips // 2, trips, functools.partial(scan_body, lambda j: True), states)
    for ui, (p, d) in enumerate(units):
        sfin_ref[0, d, 2 * p] = states[ui][:RET_DIM, :RET_DIM]
        sfin_ref[0, d, 2 * p + 1] = states[ui][RET_DIM:, RET_DIM:]


def _ret_call(u, theta, s0bd, layer=0):
    b, l, _ = u.shape
    nc = l // RET_CHUNK
    has_s0 = s0bd is not None
    cpt = 8 if nc >= 8 else nc
    npairs = max(1, 8 // cpt)
    ngrp = RET_HEADS // 2 // npairs
    w = npairs * LANES
    st_block = (2, 2 * npairs, RET_DIM, RET_DIM)
    if not has_s0:
        s0bd = jnp.zeros((1,) + st_block, F32)
        s0_spec = pl.BlockSpec((1,) + st_block, lambda g, i: (0, 0, 0, 0, 0))
    else:
        s0_spec = pl.BlockSpec((1, None) + st_block, lambda g, i: (i, layer, 0, g, 0, 0))
    th_lane = jnp.repeat(theta, RET_DIM, axis=1).reshape(2, 1, W_RET)
    th_bcast = jnp.broadcast_to(theta[:, :, None], (2, RET_HEADS, LANES))
    col = lambda c0: pl.BlockSpec((1, l, w), lambda g, i: (i, 0, c0 // w + g))
    o, sfin = pl.pallas_call(
        functools.partial(_ret_kernel, nc=nc, cpt=cpt, npairs=npairs, has_s0=has_s0),
        grid=(ngrp, b),
        in_specs=[col(COL_QR), col(COL_KR), col(COL_VR),
                  pl.BlockSpec((2, 1, w), lambda g, i: (0, 0, g)),
                  pl.BlockSpec((2, RET_HEADS, LANES), lambda g, i: (0, 0, 0)),
                  s0_spec],
        out_specs=[pl.BlockSpec((1, l, w), lambda g, i: (i, 0, g)),
                   pl.BlockSpec((1,) + st_block, lambda g, i: (i, 0, g, 0, 0))],
        out_shape=[jax.ShapeDtypeStruct((b, l, W_RET), F32),
                   jax.ShapeDtypeStruct((b, 2, RET_HEADS, RET_DIM, RET_DIM), F32)],
        scratch_shapes=[pltpu.VMEM((npairs, 2, _TAB_ROWS, LANES), F32)],
        compiler_params=_cparams(("arbitrary", "arbitrary")),
        name="retention_s0" if has_s0 else "retention",
    )(u, u, u, th_lane, th_bcast, s0bd)
    return o, sfin


def _filter_positions(l):
    f32 = np.float32
    t = np.linspace(0.0, 1.0, l, dtype=f32)[:, None]
    w = (f32(2.0 * math.pi) * np.arange(l, dtype=f32)[:, None] / f32(l)).astype(f32)
    f = np.linspace(1e-4, HY_BANDS - 1, HY_BANDS, dtype=f32)[None, :]
    z = np.concatenate([t, np.cos(f * w), -np.sin(f * w)], axis=-1).astype(f32)
    z = np.pad(z, ((0, 0), (0, 32 - HY_POS_FEAT)))
    return jnp.asarray(np.concatenate([z, z[:1], z[1:][::-1]], axis=0))


def _hyena_deltas():
    max_decay = math.log(HY_DECAY_TARGET) / HY_FAST_PCT
    min_decay = math.log(HY_DECAY_TARGET) / HY_SLOW_PCT
    return jnp.asarray(np.abs(np.linspace(min_decay, max_decay, W_HY, dtype=np.float32))[None, :])


def _filter_hidden(z_ref, w1_ref, b1_ref, fr_ref):
    pre = jnp.dot(z_ref[...], w1_ref[0], precision=HIGHEST, preferred_element_type=F32) + b1_ref[0]
    return jnp.sin(fr_ref[0] * pre)


def _filter_raw(hid, w2f, w2b, tp, dl, row0, l):
    win = jnp.exp(-tp * dl)
    row = row0 + lax.broadcasted_iota(jnp.int32, win.shape, 0)
    hf = jnp.dot(hid, w2f, precision=HIGHEST, preferred_element_type=F32) * win
    hb = jnp.dot(hid, w2b, precision=HIGHEST, preferred_element_type=F32) * win
    hf = jnp.where(row < l, hf, 0.0)
    hb = jnp.where((row > l) | (row == 0), hb, 0.0)
    return hf + hb, jnp.sum(jnp.abs(hf) + jnp.abs(hb), axis=0, keepdims=True)


def _with_skip(g, skip):
    row = lax.broadcasted_iota(jnp.int32, g.shape, 0)
    return g + jnp.where(row == 0, skip, 0.0)


def _filt_ctx_kernel(z_ref, w1_ref, b1_ref, fr_ref, w2_ref, dl_ref, sk_ref, fh_ref, fl_ref, g_ref):
    hid = _filter_hidden(z_ref, w1_ref, b1_ref, fr_ref)
    tp = z_ref[:, 0:1]
    for o in range(2):
        w2f = w2_ref[0, :, (2 * o) * W_HY:(2 * o + 1) * W_HY]
        w2b = w2_ref[0, :, (2 * o + 1) * W_HY:(2 * o + 2) * W_HY]
        raw, nrm = _filter_raw(hid, w2f, w2b, tp, dl_ref[...], 0, z_ref.shape[0] // 2)
        g = _with_skip(raw / nrm, sk_ref[0, pl.ds(o, 1), :])
        g_ref[0, o] = _dot3c(fh_ref[...], fl_ref[...], g)


def _ctx_dft_tables(l):
    n = 2 * l
    k = np.arange(n)[:, None]
    t = np.arange(l)[None, :]
    ang = 2.0 * np.pi * k * t / n
    c, s = np.cos(ang), np.sin(ang)
    fwd = np.block([[c, s], [-s, c]])
    inv = np.block([[c.T, -s.T], [s.T, c.T]])
    n_all = np.arange(n)[None, :]
    angg = 2.0 * np.pi * k * n_all / n
    filt = np.concatenate([np.cos(angg), -np.sin(angg)], axis=0) / n
    return _split_np(fwd), _split_np(inv), _split_np(filt)


def _filt_ctx_call(l, w1, b1, freq, w2, skip):
    n = 2 * l
    z_ext = _filter_positions(l)
    _, _, (fh, fl) = _ctx_dft_tables(l)
    w1p = jnp.pad(w1, ((0, 0), (0, 32 - HY_POS_FEAT), (0, 0)))
    lay = lambda *shape: pl.BlockSpec((1,) + shape, lambda d: (d,) + (0,) * len(shape))
    full = lambda a: pl.BlockSpec(a.shape, lambda d: (0,) * a.ndim)
    dl = _hyena_deltas()
    return pl.pallas_call(
        _filt_ctx_kernel,
        grid=(DEPTH,),
        in_specs=[full(z_ext), lay(32, HY_FILT_HID), lay(1, HY_FILT_HID), lay(1, HY_FILT_HID),
                  lay(HY_FILT_HID, 4 * W_HY), full(dl), lay(2, W_HY), full(fh), full(fl)],
        out_specs=pl.BlockSpec((1, 2, 2 * n, W_HY), lambda d: (d, 0, 0, 0)),
        out_shape=jax.ShapeDtypeStruct((DEPTH, 2, 2 * n, W_HY), F32),
        compiler_params=_cparams(("arbitrary",)),
        name="hyena_filter_ctx",
    )(z_ext, w1p, b1.reshape(DEPTH, 1, -1), freq.reshape(DEPTH, 1, -1), w2, dl, skip, fh, fl)


def _lat_dft_tables():
    ka = np.arange(NA)[:, None]
    b = np.arange(NB)[:, None, None]
    kb = np.arange(NB)[:, None]
    bb = np.arange(NB)[None, :]
    a_half = np.arange(NA // 2)[None, :]
    a_full = np.arange(NA)[None, :]
    phi = 2.0 * np.pi * (ka * a_half / NA + b * ka / LAT_N)
    c, s = np.cos(phi), np.sin(phi)
    a_fwd = np.concatenate([c, s], axis=2)
    ct, st = np.swapaxes(c, 1, 2), np.swapaxes(s, 1, 2)
    a_inv = np.concatenate([ct, st], axis=2)
    phig = 2.0 * np.pi * (ka * a_full / NA + b * ka / LAT_N)
    a_flt = np.concatenate([np.cos(phig), -np.sin(phig)], axis=1) / LAT_N
    ang = 2.0 * np.pi * kb * bb / NB
    c2, s2 = np.cos(ang), np.sin(ang)
    f_fwd = np.block([[c2, s2], [-s2, c2]])
    f_inv = np.block([[c2, -s2], [s2, c2]])
    return (_split_np(a_fwd), _split_np(a_inv), _split_np(a_flt), _split_np(f_fwd), _split_np(f_inv))


def _stage_b_rows(ka):
    re = pl.ds(ka, NB, stride=Y_PITCH)
    im = pl.ds(NA + ka, NB, stride=Y_PITCH)
    return re, im


def _filt_lat_kernel(z_ref, w1_ref, b1_ref, fr_ref, w2f_ref, w2b_ref, dl_ref, sk_ref, ah_ref, f2h_ref,
                     g_ref, hid_ref, gt_ref, y_ref):
    step = pl.program_id(1)
    rch = 1024
    nch = LAT_N // rch
    rows_of = lambda i: pl.ds(pl.multiple_of(i * rch, rch), rch)

    @pl.when(step == 0)
    def _():
        def hid_chunk(i, carry):
            r = rows_of(i)
            pre = jnp.dot(z_ref[r, :], w1_ref[0], precision=HIGHEST, preferred_element_type=F32)
            hid_ref[r, :] = jnp.sin(fr_ref[0] * (pre + b1_ref[0]))
            return carry

        lax.fori_loop(0, nch, hid_chunk, 0)

    w2f_hl = _split(w2f_ref[0])
    w2b_hl = _split(w2b_ref[0])

    def raw_chunk(w2_hl, i, nrm):
        r = rows_of(i)
        hh, hl = _split(hid_ref[r, :])
        h = _dot(hh, w2_hl[0]) + _dot(hl, w2_hl[0]) + _dot(hh, w2_hl[1])
        h = h * jnp.exp(-z_ref[r, 0:1] * dl_ref[...])
        row = i * rch + lax.broadcasted_iota(jnp.int32, h.shape, 0)
        h = jnp.where(row == LAT_L, 0.0, h)
        for s in range(rch // NB):
            slab = pl.ds(pl.multiple_of((i * (rch // NB) + s) * X_PITCH, 8), NB)
            gt_ref[slab, :] = h[s * NB:(s + 1) * NB]
        return nrm + jnp.sum(jnp.abs(h), axis=0, keepdims=True)

    nrm = lax.fori_loop(0, nch // 2, functools.partial(raw_chunk, w2f_hl), jnp.zeros((1, LANES), F32))
    nrm = lax.fori_loop(nch // 2, nch, functools.partial(raw_chunk, w2b_hl), nrm)
    hh, hl = _split(hid_ref[0:8, :])
    hb0 = _dot(hh, w2b_hl[0]) + _dot(hl, w2b_hl[0]) + _dot(hh, w2b_hl[1])
    hb0 = hb0 * jnp.exp(-z_ref[0:8, 0:1] * dl_ref[...])
    hb0 = jnp.where(lax.broadcasted_iota(jnp.int32, hb0.shape, 0) == 0, hb0, 0.0)
    gt_ref[0:8, :] = gt_ref[0:8, :] + hb0
    nrm = nrm + jnp.sum(jnp.abs(hb0), axis=0, keepdims=True)

    def norm_slab(a, carry):
        slab = pl.ds(pl.multiple_of(a * X_PITCH, 8), NB)
        gt_ref[slab, :] = gt_ref[slab, :] / nrm
        return carry

    lax.fori_loop(0, NA, norm_slab, 0, unroll=8)
    order = step // (W_HY // LANES)
    gt_ref[0:8, :] = _with_skip(gt_ref[0:8, :], sk_ref[0, pl.ds(order, 1), :])

    def stage_a(b, carry):
        rows = gt_ref[pl.ds(b, NA, stride=X_PITCH), :]
        y_ref[pl.ds(pl.multiple_of(b * Y_PITCH, 8), 2 * NA), :] = _dot1c(ah_ref[b], rows)
        return carry

    lax.fori_loop(0, NB, stage_a, 0, unroll=32)

    def stage_b(j, carry):
        ka = 2 * j
        re, im = _stage_b_rows(ka)
        re1, im1 = _stage_b_rows(ka + 1)
        z = jnp.concatenate([jnp.concatenate([y_ref[re, :], y_ref[im, :]], axis=0),
                             jnp.concatenate([y_ref[re1, :], y_ref[im1, :]], axis=0)], axis=1)
        x = _dot1c(f2h_ref[...], z)
        g0 = pl.multiple_of(ka * 2 * NB, 2 * NB)
        g_ref[0, 0, pl.ds(g0, 2 * NB), :] = x[:, :LANES]
        g_ref[0, 0, pl.ds(g0 + 2 * NB, 2 * NB), :] = x[:, LANES:]
        return carry

    lax.fori_loop(0, NA // 2, stage_b, 0, unroll=4)


def _filt_lat_call(w1, b1, freq, w2, skip):
    z_ext = _filter_positions(LAT_L)
    _, _, (ah, _), (f2h, _), _ = _lat_dft_tables()
    w1p = jnp.pad(w1, ((0, 0), (0, 32 - HY_POS_FEAT), (0, 0)))
    nct = W_HY // LANES
    one = pl.Buffered(1)
    lay = lambda *shape: pl.BlockSpec((1,) + shape, lambda d, s: (d,) + (0,) * len(shape))
    full = lambda a: pl.BlockSpec(a.shape, lambda d, s: (0,) * a.ndim, pipeline_mode=one)
    dl = _hyena_deltas()
    return pl.pallas_call(
        _filt_lat_kernel,
        grid=(DEPTH, 2 * nct),
        in_specs=[full(z_ext), lay(32, HY_FILT_HID), lay(1, HY_FILT_HID), lay(1, HY_FILT_HID),
                  pl.BlockSpec((1, HY_FILT_HID, LANES), lambda d, s: (d, 0, (s // nct) * 2 * nct + s % nct)),
                  pl.BlockSpec((1, HY_FILT_HID, LANES),
                               lambda d, s: (d, 0, (s // nct) * 2 * nct + nct + s % nct)),
                  pl.BlockSpec((1, LANES), lambda d, s: (0, s % nct)),
                  pl.BlockSpec((1, 2, LANES), lambda d, s: (d, 0, s % nct)),
                  full(ah), full(f2h)],
        out_specs=pl.BlockSpec((1, 1, NA * 2 * NB, LANES), lambda d, s: (d, s, 0, 0)),
        out_shape=jax.ShapeDtypeStruct((DEPTH, 2 * nct, NA * 2 * NB, LANES), F32),
        scratch_shapes=[pltpu.VMEM((LAT_N, HY_FILT_HID), F32), pltpu.VMEM((NA * X_PITCH, LANES), F32),
                        pltpu.VMEM((NB * Y_PITCH, LANES), F32)],
        compiler_params=_cparams(("arbitrary", "arbitrary")),
        name="hyena_filter_lat",
    )(z_ext, w1p, b1.reshape(DEPTH, 1, -1), freq.reshape(DEPTH, 1, -1), w2, w2, dl, skip, ah, f2h)


def _short_conv_rows(ref, bi, r0, rows, first, last, w):
    total = ref.shape[1]
    cur = ref[bi, pl.ds(r0, rows), :]
    before = ref[bi, pl.ds(jnp.maximum(r0 - 1, 0), 1), :]
    after = ref[bi, pl.ds(jnp.minimum(r0 + rows, total - 1), 1), :]
    before = jnp.where(first, 0.0, before)
    after = jnp.where(last, 0.0, after)
    rid = lax.broadcasted_iota(jnp.int32, cur.shape, 0)
    prev = jnp.where(rid == 0, before, pltpu.roll(cur, 1, 0))
    nxt = jnp.where(rid == rows - 1, after, pltpu.roll(cur, rows - 1, 0))
    return prev * w[0:1] + cur * w[1:2] + nxt * w[2:3]


def _short_conv_interior(ref, bi, r0, rows, w):
    prev = ref[bi, pl.ds(r0 - 1, rows), :]
    cur = ref[bi, pl.ds(r0, rows), :]
    nxt = ref[bi, pl.ds(r0 + 1, rows), :]
    return prev * w[0:1] + cur * w[1:2] + nxt * w[2:3]


def _cmul(xr, xi, gr, gi):
    return xr * gr - xi * gi, xr * gi + xi * gr


def _hy_ctx_kernel(v_ref, x1_ref, x2_ref, cw_ref, g_ref, fh_ref, ih_ref, o_ref):
    l = v_ref.shape[1]
    n = 2 * l

    def sc(ref, bi, grp):
        w = cw_ref[:, grp * W_HY:(grp + 1) * W_HY]
        return _short_conv_rows(ref, bi, 0, l, True, True, w)

    def conv(zr, zi, order):
        x = _dot1c(fh_ref[...], jnp.concatenate([zr, zi], axis=0))
        pr, pi = _cmul(x[:n], x[n:], g_ref[order, :n], g_ref[order, n:])
        y = _dot1c(ih_ref[...], jnp.concatenate([pr, pi], axis=0))
        return y[:l], y[l:]

    yr, yi = conv(sc(v_ref, 0, 0), sc(v_ref, 1, 0), 0)
    yr, yi = conv(sc(x1_ref, 0, 1) * yr, sc(x1_ref, 1, 1) * yi, 1)
    o_ref[0] = sc(x2_ref, 0, 2) * yr
    o_ref[1] = sc(x2_ref, 1, 2) * yi


def _hy_ctx_call(u, conv_w, g_spec, layer):
    b, l, _ = u.shape
    (fh, _), (ih, _), _ = _ctx_dft_tables(l)
    grp = lambda g: pl.BlockSpec((2, l, W_HY), lambda i: (i, 0, g))
    full = lambda a: pl.BlockSpec(a.shape, lambda i: (0,) * a.ndim)
    return pl.pallas_call(
        _hy_ctx_kernel,
        grid=(b // 2,),
        in_specs=[grp(0), grp(1), grp(2), full(conv_w),
                  pl.BlockSpec((None,) + g_spec.shape[1:], lambda i: (layer, 0, 0, 0)), full(fh), full(ih)],
        out_specs=pl.BlockSpec((2, l, W_HY), lambda i: (i, 0, 0)),
        out_shape=jax.ShapeDtypeStruct((b, l, W_HY), F32),
        compiler_params=_cparams(("parallel",)),
        name="hyena_ctx",
    )(u, u, u, conv_w, g_spec, fh, ih)


def _hy_lat_kernel(z_ref, m_ref, cw_ref, g_ref, af_ref, ai_ref, f2_ref, f3_ref, o_ref, xr_scr, xi_scr,
                   y_scr, *, conv_in):
    x_scr = (xr_scr, xi_scr)
    na_half = NA // 2
    w_in = cw_ref[0] if conv_in else None
    w_mul = cw_ref[1]

    def conv_slab(ref, bi, a, w):
        if isinstance(a, int):
            return _short_conv_rows(ref, bi, a * NB, NB, a == 0, a == na_half - 1, w)
        return _short_conv_interior(ref, bi, pl.multiple_of(a * NB, NB), NB, w)

    def edges_then_interior(body):
        body(0, 0)
        body(na_half - 1, 0)
        lax.fori_loop(1, na_half - 1, body, 0, unroll=2)

    def load_in(a, carry):
        for bi in range(2):
            if conv_in:
                val = conv_slab(z_ref, bi, a, w_in)
            else:
                val = z_ref[bi, pl.ds(pl.multiple_of(a * NB, NB), NB), :]
            x_scr[bi][pl.ds(pl.multiple_of(a * X_PITCH, 8), NB), :] = val
        return carry

    edges_then_interior(load_in)

    def stage_a(b, carry):
        zr = xr_scr[pl.ds(b, na_half, stride=X_PITCH), :]
        zi = xi_scr[pl.ds(b, na_half, stride=X_PITCH), :]
        rhs = jnp.concatenate([jnp.concatenate([zr, zi], axis=0), jnp.concatenate([zi, -zr], axis=0)],
                              axis=1)
        y = _dot1c(af_ref[b], rhs)
        r0 = pl.multiple_of(b * Y_PITCH, 8)
        y_scr[pl.ds(r0, NA), :] = y[:, :LANES]
        y_scr[pl.ds(r0 + NA, NA), :] = y[:, LANES:]
        return carry

    lax.fori_loop(0, NB, stage_a, 0, unroll=32)

    def spectrum_product(j):
        ka = 2 * j
        re, im = _stage_b_rows(ka)
        re1, im1 = _stage_b_rows(ka + 1)
        z = jnp.concatenate([jnp.concatenate([y_scr[re, :], y_scr[im, :]], axis=0),
                             jnp.concatenate([y_scr[re1, :], y_scr[im1, :]], axis=0)], axis=1)
        x = _dot1c(f2_ref[...], z)
        g0 = pl.multiple_of(ka * 2 * NB, 2 * NB)
        gr = jnp.concatenate([g_ref[pl.ds(g0, NB), :], g_ref[pl.ds(g0 + 2 * NB, NB), :]], axis=1)
        gi = jnp.concatenate([g_ref[pl.ds(g0 + NB, NB), :], g_ref[pl.ds(g0 + 3 * NB, NB), :]], axis=1)
        pr, pi = _cmul(x[:NB], x[NB:], gr, gi)
        return jnp.concatenate([pr, pi], axis=0).astype(BF16)

    def inverse_b(j, prod):
        ka = 2 * j
        re, im = _stage_b_rows(ka)
        re1, im1 = _stage_b_rows(ka + 1)
        u = _dot(f3_ref[...], prod)
        y_scr[re, :] = u[:NB, :LANES]
        y_scr[im, :] = u[NB:, :LANES]
        y_scr[re1, :] = u[:NB, LANES:]
        y_scr[im1, :] = u[NB:, LANES:]

    def stage_b(j, prod):
        nxt = spectrum_product(j + 1)
        inverse_b(j, prod)
        return nxt

    last = lax.fori_loop(0, NA // 2 - 1, stage_b, spectrum_product(0), unroll=4)
    inverse_b(NA // 2 - 1, last)

    def stage_c(b, carry):
        r0 = pl.multiple_of(b * Y_PITCH, 8)
        ur = y_scr[pl.ds(r0, NA), :]
        ui = y_scr[pl.ds(r0 + NA, NA), :]
        rhs = jnp.concatenate([jnp.concatenate([ur, -ui], axis=0), jnp.concatenate([ui, ur], axis=0)],
                              axis=1)
        y = _dot1c(ai_ref[b], rhs)
        xr_scr[pl.ds(b, na_half, stride=X_PITCH), :] = y[:, :LANES]
        xi_scr[pl.ds(b, na_half, stride=X_PITCH), :] = y[:, LANES:]
        return carry

    lax.fori_loop(0, NB, stage_c, 0, unroll=32)

    def store_out(a, carry):
        for bi in range(2):
            mul = conv_slab(m_ref, bi, a, w_mul)
            o_ref[bi, pl.ds(pl.multiple_of(a * NB, NB), NB), :] = (
                x_scr[bi][pl.ds(pl.multiple_of(a * X_PITCH, 8), NB), :] * mul)
        return carry

    edges_then_interior(store_out)


def _hy_lat_call(src, src_col, u, mul_col, conv_w2, g_spec, layer, order, *, conv_in):
    b, l, _ = u.shape
    nct = W_HY // LANES
    (af, _), (ai, _), _, (f2, _), (f3, _) = _lat_dft_tables()
    one = pl.Buffered(1)
    blk = lambda col: pl.BlockSpec((2, l, LANES), lambda c, p: (p, 0, col + c))
    const = lambda a: pl.BlockSpec(a.shape, lambda c, p: (0,) * a.ndim, pipeline_mode=one)
    return pl.pallas_call(
        functools.partial(_hy_lat_kernel, conv_in=conv_in),
        grid=(nct, b // 2),
        in_specs=[blk(src_col), blk(mul_col),
                  pl.BlockSpec((2, 3, LANES), lambda c, p: (0, 0, c)),
                  pl.BlockSpec((None, None, NA * 2 * NB, LANES), lambda c, p: (layer, order * nct + c, 0, 0)),
                  const(af), const(ai), const(f2), const(f3)],
        out_specs=pl.BlockSpec((2, l, LANES), lambda c, p: (p, 0, c)),
        out_shape=jax.ShapeDtypeStruct((b, l, W_HY), F32),
        scratch_shapes=[pltpu.VMEM(((NA // 2) * X_PITCH, LANES), F32),
                        pltpu.VMEM(((NA // 2) * X_PITCH, LANES), F32),
                        pltpu.VMEM((NB * Y_PITCH, LANES), F32)],
        compiler_params=_cparams(("arbitrary", "arbitrary")),
        name="hyena_lat_conv_in" if conv_in else "hyena_lat",
    )(src, u, conv_w2, g_spec, af, ai, f2, f3)


def _rope_tables(l):
    f32 = np.float32
    rows = l // GRID_W
    row = np.repeat(np.arange(rows), GRID_W).astype(f32)
    col = np.tile(np.arange(GRID_W), rows).astype(f32)
    quarter = HEAD_DIM // 4
    inv = np.power(f32(ROPE_BASE), -np.arange(quarter, dtype=f32) / f32(quarter)).astype(f32)
    ang = np.concatenate([row[:, None] * inv, col[:, None] * inv], axis=-1).astype(f32)
    cos, sin = np.cos(ang), np.sin(ang)
    q = quarter
    cos_h = np.concatenate([cos[:, :q], cos[:, :q], cos[:, q:], cos[:, q:]], axis=-1)
    sin_h = np.concatenate([-sin[:, :q], sin[:, :q], -sin[:, q:], sin[:, q:]], axis=-1)
    return jnp.asarray(np.tile(cos_h, (1, 2))), jnp.asarray(np.tile(sin_h, (1, 2)))


def kernel(x_prompt, x_sample, c, cache_k, cache_v, state_ret, c_ctx, norm_w, w_mod, b_mod, w_in, hy_conv,
           hy_filt_w1, hy_filt_b1, hy_filt_freq, hy_filt_w2, hy_skip, attn_sink, ret_theta, ret_gn,
           w_branch_a, w_branch_b, w_branch_c, w_merge, b_merge, w_out, final_norm_w):
    d = D_MODEL
    bc, lc, _ = x_prompt.shape
    bl, ll, _ = x_sample.shape
    assert ll == LAT_L and bc % 2 == 0 and bl % 2 == 0
    past = cache_k.shape[2]

    cond = jnp.zeros((16, d), F32).at[:bl].set(c).at[bl].set(c_ctx)
    mod = _mod_call(cond, w_mod, b_mod)

    g_ctx = _filt_ctx_call(lc, hy_filt_w1, hy_filt_b1, hy_filt_freq, hy_filt_w2, hy_skip)
    g_lat = _filt_lat_call(hy_filt_w1, hy_filt_b1, hy_filt_freq, hy_filt_w2, hy_skip)

    cos_t, sin_t = _rope_tables(ll)
    w_in_b = w_in.astype(BF16)
    wm_b = w_merge.astype(BF16)
    wa_b = w_branch_a.astype(BF16)
    wb_b = w_branch_b.astype(BF16)
    wc_b = w_branch_c.astype(BF16)
    wo_b = w_out.astype(BF16)
    fnw = final_norm_w.reshape(1, d)
    k_ctx = cache_k.reshape(bl, DEPTH, past, W_KV)
    v_ctx = cache_v.reshape(bl, DEPTH, past, W_KV)
    hy_cols = COL_HY // LANES
    nct = W_HY // LANES

    xp, xs = x_prompt, x_sample
    ks_out, vs_out, ss_out = [], [], []
    for l in range(DEPTH):
        final = l == DEPTH - 1
        nw = norm_w[l].reshape(1, d)
        bm = b_merge[l].reshape(1, -1)
        gn = ret_gn[l].reshape(1, W_RET)
        shift, scale, gate = (mod[l, :, i * d:(i + 1) * d][:, None, :] for i in range(3))
        conv_w = hy_conv[l]
        cw = lambda g: conv_w[:, g * W_HY:(g + 1) * W_HY]

        sl = slice(bl, bl + 1)
        u = _in_call(xp, shift[sl], scale[sl], nw, w_in_b[l], cos_t, sin_t, rope=False)
        ya = _hy_ctx_call(u, conv_w, g_ctx, l)
        yb = _attn_ctx_call(u, attn_sink[l])
        yc, sfin = _ret_call(u, ret_theta[l], None)
        res = _out_call(xp, shift[sl], scale[sl], gate[sl], nw, ya, u, yb, yc, gn, wm_b[l], bm, wa_b[l], wb_b[l],
                        wc_b[l], wo_b[l], fnw, final=final)
        xp = res[0]
        if final:
            y_prompt = res[1]
        ks_out.append(u[:, :, COL_KA:COL_KA + W_KV])
        vs_out.append(u[:, :, COL_VA:COL_VA + W_KV])
        ss_out.append(sfin)

        sl = slice(0, bl)
        u = _in_call(xs, shift[sl], scale[sl], nw, w_in_b[l], cos_t, sin_t, rope=True)
        z1 = _hy_lat_call(u, hy_cols, u, hy_cols + nct, jnp.stack([cw(0), cw(1)]), g_lat, l, 0, conv_in=True)
        ya = _hy_lat_call(z1, 0, u, hy_cols + 2 * nct, jnp.stack([cw(2), cw(2)]), g_lat, l, 1, conv_in=False)
        yb = _attn_lat_call(u, k_ctx, v_ctx, attn_sink[l], l)
        yc, _ = _ret_call(u, ret_theta[l], state_ret, l)
        res = _out_call(xs, shift[sl], scale[sl], gate[sl], nw, ya, u, yb, yc, gn, wm_b[l], bm, wa_b[l], wb_b[l],
                        wc_b[l], wo_b[l], fnw, final=final)
        xs = res[0]
        if final:
            y_sample = res[1]

    kv_shape = (bc, DEPTH, lc, ATT_KV_HEADS, HEAD_DIM)
    new_cache_k = jnp.stack(ks_out, axis=1).reshape(kv_shape)
    new_cache_v = jnp.stack(vs_out, axis=1).reshape(kv_shape)
    new_state_ret = jnp.stack(ss_out, axis=1)
    return (y_prompt, y_sample, new_cache_k, new_cache_v, new_state_ret)
```

```python
import functools
import math

import numpy as np
import jax
import jax.numpy as jnp
from jax import lax
from jax.experimental import pallas as pl
from jax.experimental.pallas import tpu as pltpu

F32 = jnp.float32
BF16 = jnp.bfloat16
HIGHEST = lax.Precision.HIGHEST

D_MODEL = 1024
DEPTH = 4
GRID_W = 64
W_HY = 512
HY_BANDS = 8
HY_POS_FEAT = 1 + 2 * HY_BANDS
HY_FILT_HID = 64
HY_DECAY_TARGET = 1e-2
HY_FAST_PCT = 0.3
HY_SLOW_PCT = 1.5
ATT_HEADS = 8
ATT_KV_HEADS = 2
ATT_GROUP = ATT_HEADS // ATT_KV_HEADS
HEAD_DIM = 64
W_ATT = ATT_HEADS * HEAD_DIM
W_KV = ATT_KV_HEADS * HEAD_DIM
ATT_BLOCK = 128
ATT_QB = 8
RET_HEADS = 8
RET_DIM = 64
W_RET = RET_HEADS * RET_DIM
RET_CHUNK = 128
ROPE_BASE = 10000.0
EPS = 1e-6
NEG = -1e30

LANES = 128
MXU_ROWS = 512
IN_PARTS = 2
VMEM_LIMIT = 58 * 1024 * 1024

IN_DIM = 5376
COL_HY = 0
COL_GH = 1536
COL_QA = 2048
COL_GA = 2560
COL_QR = 3072
COL_KR = 3584
COL_VR = 4096
COL_GR = 4608
COL_KA = 5120
COL_VA = 5248
_W_COL = {COL_HY: 0, COL_HY + 512: 512, COL_HY + 1024: 1024, COL_GH: 1536, COL_QA: 2048, COL_KA: 2560,
          COL_VA: 2688, COL_GA: 2816, COL_QR: 3328, COL_KR: 3840, COL_VR: 4352, COL_GR: 4864}

LAT_L = 4096
LAT_N = 2 * LAT_L
NA = 64
NB = 128
Y_PITCH = 136
X_PITCH = 136


def _cparams(sem):
    return pltpu.CompilerParams(dimension_semantics=sem, vmem_limit_bytes=VMEM_LIMIT)


def _split_np(a):
    a32 = np.asarray(a, np.float32)
    hi = a32.astype(BF16)
    lo = (a32 - hi.astype(np.float32)).astype(BF16)
    return jnp.asarray(hi), jnp.asarray(lo)


def _split(x):
    hi = x.astype(BF16)
    lo = (x - hi.astype(F32)).astype(BF16)
    return hi, lo


def _dot(a, b):
    return jnp.dot(a, b, preferred_element_type=F32)


def _dot3c(chi, clo, x):
    xh, xl = _split(x)
    return _dot(chi, xh) + _dot(clo, xh) + _dot(chi, xl)


def _dot1c(chi, x):
    return _dot(chi, x.astype(BF16))


def _silu(x):
    return x * jax.nn.sigmoid(x)


def _mod_kernel(c_ref, w_ref, b_ref, o_ref):
    s = _silu(c_ref[...])
    o_ref[0] = jnp.dot(s, w_ref[0], precision=HIGHEST, preferred_element_type=F32) + b_ref[0]


def _mod_call(cond, w_mod, b_mod):
    rows, d = cond.shape
    n = w_mod.shape[-1]
    tn = 1024
    return pl.pallas_call(
        _mod_kernel,
        grid=(DEPTH, n // tn),
        in_specs=[pl.BlockSpec((rows, d), lambda l, j: (0, 0)),
                  pl.BlockSpec((1, d, tn), lambda l, j: (l, 0, j)),
                  pl.BlockSpec((1, 1, tn), lambda l, j: (l, 0, j))],
        out_specs=pl.BlockSpec((1, rows, tn), lambda l, j: (l, 0, j)),
        out_shape=jax.ShapeDtypeStruct((DEPTH, rows, n), F32),
        compiler_params=_cparams(("arbitrary", "arbitrary")),
        name="adaln_mod",
    )(cond, w_mod, b_mod.reshape(DEPTH, 1, n))


def _modulated(x, nw, scale, shift):
    ms = jnp.mean(x * x, axis=-1, keepdims=True)
    h = x * lax.rsqrt(ms + EPS) * nw
    return h * (1.0 + scale) + shift


def _rope128(x, cos, sin_signed, first_half):
    up = pltpu.roll(x, LANES - 16, 1)
    dn = pltpu.roll(x, 16, 1)
    return x * cos + jnp.where(first_half, up, dn) * sin_signed


def _rows(ref):
    bt, tm, w = ref.shape
    return ref[...].reshape(bt * tm, w)


def _put(ref, c0, val):
    bt, tm, _ = ref.shape
    ref[:, :, c0:c0 + val.shape[1]] = val.reshape(bt, tm, val.shape[1])


def _in_kernel(x_ref, shift_ref, scale_ref, nw_ref, w_ref, cos_ref, sin_ref, o_ref, *, rope):
    x = _rows(x_ref)
    rows = x.shape[0]
    part = rows // IN_PARTS
    hbs = [_modulated(x[r:r + part], nw_ref[...], scale_ref[0], shift_ref[0]).astype(BF16)
           for r in range(0, rows, part)]
    if rope:
        cos = cos_ref[...]
        sin = sin_ref[...]
        lane = lax.broadcasted_iota(jnp.int32, (rows, LANES), 1)
        first_half = (lane % 32) < 16

    def seg(c0, width):
        w0 = _W_COL[c0]
        return jnp.concatenate([_dot(hb, w_ref[:, w0:w0 + width]) for hb in hbs], axis=0)

    def put_rope(c0, val, mul):
        for i in range(val.shape[1] // LANES):
            piece = val[:, i * LANES:(i + 1) * LANES]
            if rope:
                piece = _rope128(piece, cos, sin, first_half)
            if mul is not None:
                piece = piece * mul
            _put(o_ref, c0 + i * LANES, piece)

    for g in range(3):
        _put(o_ref, COL_HY + g * 512, seg(COL_HY + g * 512, 512))
    _put(o_ref, COL_GH, _silu(seg(COL_GH, 512)))
    put_rope(COL_QA, seg(COL_QA, 512), None)
    _put(o_ref, COL_GA, _silu(seg(COL_GA, 512)))
    put_rope(COL_QR, seg(COL_QR, 512), None)
    put_rope(COL_KR, seg(COL_KR, 512), RET_DIM ** -0.5)
    _put(o_ref, COL_VR, seg(COL_VR, 512))
    _put(o_ref, COL_GR, _silu(seg(COL_GR, 512)))
    put_rope(COL_KA, seg(COL_KA, 128), None)
    _put(o_ref, COL_VA, seg(COL_VA, 128))


def _token_tiling(b, l, per_batch):
    if l >= MXU_ROWS:
        return 1, MXU_ROWS
    bt = 1 if per_batch else min(b, MXU_ROWS // l)
    return bt, l


def _in_call(x, shift, scale, nw, w, cos_t, sin_t, *, rope):
    b, l, d = x.shape
    per_batch = shift.shape[0] > 1
    bt, tm = _token_tiling(b, l, per_batch)
    assert not rope or bt == 1
    mod_map = (lambda i, j: (i, 0, 0)) if per_batch else (lambda i, j: (0, 0, 0))
    return pl.pallas_call(
        functools.partial(_in_kernel, rope=rope),
        grid=(b // bt, l // tm),
        in_specs=[pl.BlockSpec((bt, tm, d), lambda i, j: (i, j, 0)),
                  pl.BlockSpec((1, 1, d), mod_map),
                  pl.BlockSpec((1, 1, d), mod_map),
                  pl.BlockSpec((1, d), lambda i, j: (0, 0)),
                  pl.BlockSpec((d, IN_DIM), lambda i, j: (0, 0), pipeline_mode=pl.Buffered(1)),
                  pl.BlockSpec((tm, LANES), lambda i, j: (j, 0)),
                  pl.BlockSpec((tm, LANES), lambda i, j: (j, 0))],
        out_specs=pl.BlockSpec((bt, tm, IN_DIM), lambda i, j: (i, j, 0)),
        out_shape=jax.ShapeDtypeStruct((b, l, IN_DIM), F32),
        compiler_params=_cparams(("parallel", "parallel")),
        name="in_proj_rope" if rope else "in_proj",
    )(x, shift, scale, nw, w, cos_t, sin_t)


def _retention_post(o, gn, gate):
    lane = lax.broadcasted_iota(jnp.int32, (1, LANES), 1)
    lo_head = lane < RET_DIM
    outs = []
    for t in range(W_RET // LANES):
        sl = slice(t * LANES, (t + 1) * LANES)
        ot = o[:, sl]
        o2 = ot * ot
        s_lo = jnp.sum(jnp.where(lo_head, o2, 0.0), axis=-1, keepdims=True)
        s_hi = jnp.sum(jnp.where(lo_head, 0.0, o2), axis=-1, keepdims=True)
        ms = jnp.where(lo_head, s_lo, s_hi) * (1.0 / RET_DIM)
        outs.append(ot * lax.rsqrt(ms + EPS) * gn[:, sl] * gate[:, sl])
    return jnp.concatenate(outs, axis=1)


def _out_kernel(x_ref, shift_ref, scale_ref, gate_ref, nw_ref, ya_ref, gh_ref, yb_ref, yc_ref, gr_ref, gn_ref,
                wm_ref, bm_ref, wa_ref, wb_ref, wc_ref, wo_ref, fnw_ref, *out_refs, final):
    x = _rows(x_ref)
    d = x.shape[1]
    hb = _modulated(x, nw_ref[...], scale_ref[0], shift_ref[0]).astype(BF16)
    branches = (_rows(ya_ref) * _rows(gh_ref), _rows(yb_ref),
                _retention_post(_rows(yc_ref), gn_ref[...], _rows(gr_ref)))
    merged = None
    for i, (y, w_ref) in enumerate(zip(branches, (wa_ref, wb_ref, wc_ref))):
        g = jax.nn.sigmoid(_dot(hb, wm_ref[:, i * d:(i + 1) * d]) + bm_ref[:, i * d:(i + 1) * d])
        term = g * _dot(y.astype(BF16), w_ref[...])
        merged = term if merged is None else merged + term
    out = _dot(merged.astype(BF16), wo_ref[...])
    xn = x + gate_ref[0] * out
    _put(out_refs[0], 0, xn)
    if final:
        ms = jnp.mean(xn * xn, axis=-1, keepdims=True)
        _put(out_refs[1], 0, xn * lax.rsqrt(ms + EPS) * fnw_ref[...])


def _out_call(x, shift, scale, gate, nw, ya, u, yb, yc, gn, wm, bm, wa, wb, wc, wo, fnw, *, final):
    b, l, d = x.shape
    per_batch = shift.shape[0] > 1
    bt, tm = _token_tiling(b, l, per_batch)
    mod_map = (lambda i, j: (i, 0, 0)) if per_batch else (lambda i, j: (0, 0, 0))
    tok = lambda w: pl.BlockSpec((bt, tm, w), lambda i, j: (i, j, 0))
    full = lambda a: pl.BlockSpec(a.shape, lambda i, j: (0,) * a.ndim, pipeline_mode=pl.Buffered(1))
    n_out = 2 if final else 1
    res = pl.pallas_call(
        functools.partial(_out_kernel, final=final),
        grid=(b // bt, l // tm),
        in_specs=[tok(d), pl.BlockSpec((1, 1, d), mod_map), pl.BlockSpec((1, 1, d), mod_map),
                  pl.BlockSpec((1, 1, d), mod_map), full(nw), tok(W_HY),
                  pl.BlockSpec((bt, tm, W_HY), lambda i, j: (i, j, COL_GH // W_HY)), tok(W_ATT), tok(W_RET),
                  pl.BlockSpec((bt, tm, W_RET), lambda i, j: (i, j, COL_GR // W_RET)), full(gn),
                  full(wm), full(bm), full(wa), full(wb), full(wc), full(wo), full(fnw)],
        out_specs=[tok(d)] * n_out,
        out_shape=[jax.ShapeDtypeStruct((b, l, d), F32)] * n_out,
        compiler_params=_cparams(("parallel", "parallel")),
        name="merge_out_final" if final else "merge_out",
    )(x, shift, scale, gate, nw, ya, u, yb, yc, u, gn, wm, bm, wa, wb, wc, wo, fnw)
    return res


_NT = (((1,), (1,)), ((), ()))


_TN = (((0,), (0,)), ((), ()))
LOG2E = 1.4426950408889634
Q_SCALE = (HEAD_DIM ** -0.5) * LOG2E


def _attn_scores_t(q, kh, kv):
    h0 = kv * ATT_GROUP
    qs = jnp.concatenate([q[:, (h0 + g) * HEAD_DIM:(h0 + g + 1) * HEAD_DIM] for g in range(ATT_GROUP)],
                         axis=0).astype(BF16)
    return lax.dot_general(kh, qs, _NT, preferred_element_type=F32)


def _attn_finish_t(sink_ref, s, vh, g_ref, o_ref, kv, row0=0):
    tk, cols = s.shape
    t = cols // ATT_GROUP
    h0 = kv * ATT_GROUP
    head = lax.broadcasted_iota(jnp.int32, (1, cols), 1) // t
    sink = jnp.full((1, cols), sink_ref[h0], F32)
    for g in range(1, ATT_GROUP):
        sink = jnp.where(head == g, sink_ref[h0 + g], sink)
    sink = sink * LOG2E
    m = jnp.maximum(jnp.max(s, axis=0, keepdims=True), sink)
    p = jnp.exp2(s - m).astype(BF16)
    v_ext = jnp.concatenate([vh, jnp.ones((tk, HEAD_DIM), BF16)], axis=1)
    o_ext = lax.dot_general(v_ext, p, _TN, preferred_element_type=F32)
    denom = o_ext[HEAD_DIM:HEAD_DIM + 1] + jnp.exp2(sink - m)
    o = o_ext[:HEAD_DIM] / denom
    for gp in range(ATT_GROUP // 2):
        pair = jnp.concatenate([o[:, (2 * gp) * t:(2 * gp + 1) * t], o[:, (2 * gp + 1) * t:(2 * gp + 2) * t]],
                               axis=0)
        c0 = (h0 + 2 * gp) * HEAD_DIM
        o_ref[0, row0:row0 + t, c0:c0 + 2 * HEAD_DIM] = pair.T * g_ref[0, row0:row0 + t, c0:c0 + 2 * HEAD_DIM]


def _attn_ctx_kernel(sink_ref, q_ref, k_ref, v_ref, g_ref, o_ref):
    q = q_ref[0] * Q_SCALE
    k = k_ref[0].astype(BF16)
    v = v_ref[0].astype(BF16)
    scores = [_attn_scores_t(q, k[:, kv * HEAD_DIM:(kv + 1) * HEAD_DIM], kv) for kv in range(ATT_KV_HEADS)]
    for kv in range(ATT_KV_HEADS):
        _attn_finish_t(sink_ref, scores[kv], v[:, kv * HEAD_DIM:(kv + 1) * HEAD_DIM], g_ref, o_ref, kv)


def _attn_ctx_call(u, sink):
    b, l, _ = u.shape
    return pl.pallas_call(
        _attn_ctx_kernel,
        grid=(b,),
        in_specs=[pl.BlockSpec(memory_space=pltpu.SMEM),
                  pl.BlockSpec((1, l, W_ATT), lambda i: (i, 0, COL_QA // W_ATT)),
                  pl.BlockSpec((1, l, W_KV), lambda i: (i, 0, COL_KA // W_KV)),
                  pl.BlockSpec((1, l, W_KV), lambda i: (i, 0, COL_VA // W_KV)),
                  pl.BlockSpec((1, l, W_ATT), lambda i: (i, 0, COL_GA // W_ATT))],
        out_specs=pl.BlockSpec((1, l, W_ATT), lambda i: (i, 0, 0)),
        out_shape=jax.ShapeDtypeStruct((b, l, W_ATT), F32),
        compiler_params=_cparams(("parallel",)),
        name="attn_ctx",
    )(sink, u, u, u, u)


def _attn_lat_kernel(sink_ref, q_ref, kp_ref, kc_ref, kn_ref, vp_ref, vc_ref, vn_ref, kx_ref, vx_ref,
                     g_ref, o_ref):
    j = pl.program_id(1)
    last = pl.num_programs(1) - 1
    b = ATT_BLOCK
    nq = ATT_QB
    bf = lambda ref: ref[0].astype(BF16)
    kx, vx = bf(kx_ref), bf(vx_ref)
    kblk = [bf(kp_ref)] + [kc_ref[0, t * b:(t + 1) * b, :].astype(BF16) for t in range(nq)] + [bf(kn_ref)]
    vblk = [bf(vp_ref)] + [vc_ref[0, t * b:(t + 1) * b, :].astype(BF16) for t in range(nq)] + [bf(vn_ref)]
    keys = [jnp.concatenate(kblk[t:t + 3] + [kx], axis=0) for t in range(nq)]
    vals = [jnp.concatenate(vblk[t:t + 3] + [vx], axis=0) for t in range(nq)]
    cols = ATT_GROUP * b
    c = lax.broadcasted_iota(jnp.int32, (b, cols), 0)
    r = lax.broadcasted_iota(jnp.int32, (b, cols), 1) % b
    ok_prev = [(c >= r) & (j > 0)] + [c >= r] * (nq - 1)
    ok_next = [c <= r] * (nq - 1) + [(c <= r) & (j < last)]

    def band(s, t):
        return jnp.concatenate([jnp.where(ok_prev[t], s[:b], NEG), s[b:2 * b],
                                jnp.where(ok_next[t], s[2 * b:3 * b], NEG), s[3 * b:]], axis=0)

    chains = [(t, kv) for t in range(nq) for kv in range(ATT_KV_HEADS)]
    scores = []
    for t, kv in chains:
        q = q_ref[0, t * b:(t + 1) * b, :] * Q_SCALE
        scores.append(band(_attn_scores_t(q, keys[t][:, kv * HEAD_DIM:(kv + 1) * HEAD_DIM], kv), t))
    for (t, kv), s in zip(chains, scores):
        _attn_finish_t(sink_ref, s, vals[t][:, kv * HEAD_DIM:(kv + 1) * HEAD_DIM], g_ref, o_ref, kv, t * b)


def _attn_lat_call(u, kctx, vctx, sink, layer):
    b, l, _ = u.shape
    nb = l // ATT_BLOCK
    past = kctx.shape[2]
    kcol = COL_KA // W_KV
    vcol = COL_VA // W_KV
    nq = ATT_QB
    prev = lambda col: pl.BlockSpec((1, ATT_BLOCK, W_KV), lambda i, j: (i, jnp.maximum(nq * j - 1, 0), col))
    cur = lambda col: pl.BlockSpec((1, nq * ATT_BLOCK, W_KV), lambda i, j: (i, j, col))
    nxt = lambda col: pl.BlockSpec((1, ATT_BLOCK, W_KV),
                                   lambda i, j: (i, jnp.minimum(nq * j + nq, nb - 1), col))
    ctx = pl.BlockSpec((1, None, past, W_KV), lambda i, j: (i, layer, 0, 0))
    return pl.pallas_call(
        _attn_lat_kernel,
        grid=(b, nb // nq),
        in_specs=[pl.BlockSpec(memory_space=pltpu.SMEM),
                  pl.BlockSpec((1, nq * ATT_BLOCK, W_ATT), lambda i, j: (i, j, COL_QA // W_ATT)),
                  prev(kcol), cur(kcol), nxt(kcol), prev(vcol), cur(vcol), nxt(vcol), ctx, ctx,
                  pl.BlockSpec((1, nq * ATT_BLOCK, W_ATT), lambda i, j: (i, j, COL_GA // W_ATT))],
        out_specs=pl.BlockSpec((1, nq * ATT_BLOCK, W_ATT), lambda i, j: (i, j, 0)),
        out_shape=jax.ShapeDtypeStruct((b, l, W_ATT), F32),
        compiler_params=_cparams(("parallel", "parallel")),
        name="attn_lat",
    )(sink, u, u, u, u, u, u, u, kctx, vctx, u)


def _log_sigmoid(x):
    return jnp.minimum(x, 0.0) - jnp.log1p(jnp.exp(-jnp.abs(x)))


_TAB_DMAT = 0
_TAB_QDEC = 2 * RET_CHUNK
_TAB_KDEC = 3 * RET_CHUNK
_TAB_CDEC = 4 * RET_CHUNK
_TAB_ROWS = 4 * RET_CHUNK + 8


def _ret_kernel(q_ref, k_ref, v_ref, thl_ref, thb_ref, s0_ref, o_ref, sfin_ref, tab_ref, *, nc, cpt,
                npairs, has_s0):
    grp = pl.program_id(0)
    c = RET_CHUNK
    lane = lax.broadcasted_iota(jnp.int32, (1, LANES), 1)
    lo_head = lane < RET_DIM
    dd = lax.broadcasted_iota(jnp.int32, (LANES, LANES), 0)
    ee = lax.broadcasted_iota(jnp.int32, (LANES, LANES), 1)
    same_head = (dd < RET_DIM) == (ee < RET_DIM)
    lanes_of = lambda p: slice(p * LANES, (p + 1) * LANES)

    @pl.when(pl.program_id(1) == 0)
    def _():
        rowf = lax.broadcasted_iota(jnp.int32, (c, LANES), 0).astype(F32)
        ii = lax.broadcasted_iota(jnp.int32, (c, c), 0)
        jj = lax.broadcasted_iota(jnp.int32, (c, c), 1)
        for p in range(npairs):
            for d in range(2):
                lg_lane = _log_sigmoid(thl_ref[d, :, lanes_of(p)])
                dist = (ii - jj) if d == 0 else (jj - ii)
                for hh in range(2):
                    head = 2 * (grp * npairs + p) + hh
                    lg_h = _log_sigmoid(thb_ref[d, pl.ds(head, 1), :])
                    dm = jnp.where(dist >= 0, jnp.exp(lg_h * jnp.maximum(dist, 0).astype(F32)), 0.0)
                    tab_ref[p, d, _TAB_DMAT + hh * c:_TAB_DMAT + (hh + 1) * c, :] = dm
                if d == 0:
                    q_dec = jnp.exp(lg_lane * (rowf + 1.0))
                    k_dec = jnp.exp(lg_lane * (c - 1.0 - rowf))
                else:
                    q_dec = jnp.exp(lg_lane * (c - rowf))
                    k_dec = jnp.exp(lg_lane * rowf)
                tab_ref[p, d, _TAB_QDEC:_TAB_QDEC + c, :] = q_dec
                tab_ref[p, d, _TAB_KDEC:_TAB_KDEC + c, :] = k_dec
                tab_ref[p, d, _TAB_CDEC:_TAB_CDEC + 8, :] = jnp.broadcast_to(
                    jnp.exp(lg_lane * float(c)), (8, LANES))

    def first_level(p, d, r0):
        qc = q_ref[0, pl.ds(r0, c), lanes_of(p)]
        kc = k_ref[0, pl.ds(r0, c), lanes_of(p)]
        vcb = v_ref[0, pl.ds(r0, c), lanes_of(p)].astype(BF16)
        qs = jnp.concatenate([jnp.where(lo_head, qc, 0.0), jnp.where(lo_head, 0.0, qc)], axis=0)
        sc = lax.dot_general(qs.astype(BF16), kc.astype(BF16), _NT, preferred_element_type=F32)
        sc = sc * tab_ref[p, d, _TAB_DMAT:_TAB_DMAT + 2 * c, :]
        kd = kc * tab_ref[p, d, _TAB_KDEC:_TAB_KDEC + c, :]
        upd = jnp.where(same_head, _dot(kd.T.astype(BF16), vcb), 0.0)
        return qc, vcb, sc.astype(BF16), upd

    def second_level(p, d, lvl1, s):
        qc, vcb, scb, upd = lvl1
        pv = _dot(scb, vcb)
        qd = qc * tab_ref[p, d, _TAB_QDEC:_TAB_QDEC + c, :]
        o = _dot(qd.astype(BF16), s.astype(BF16)) + jnp.where(lo_head, pv[:c], pv[c:])
        return o, tab_ref[p, d, _TAB_CDEC:_TAB_CDEC + 1, :] * s + upd

    def init_state(p, d):
        if not has_s0:
            return jnp.zeros((LANES, LANES), F32)
        z = jnp.zeros((RET_DIM, RET_DIM), F32)
        return jnp.concatenate([jnp.concatenate([s0_ref[0, d, 2 * p], z], axis=1),
                                jnp.concatenate([z, s0_ref[0, d, 2 * p + 1]], axis=1)], axis=0)

    units = [(p, d) for p in range(npairs) for d in range(2)]

    def scan_body(second, n, states):
        def row0(d, j):
            idx = n * cpt + j
            return pl.multiple_of((idx if d == 0 else nc - 1 - idx) * c, c)

        lvl = {(p, d, j): first_level(p, d, row0(d, j)) for j in range(cpt) for p, d in units}
        states = list(states)
        for j in range(cpt):
            for ui, (p, d) in enumerate(units):
                o, states[ui] = second_level(p, d, lvl[(p, d, j)], states[ui])
                dst = (0, pl.ds(row0(d, j), c), lanes_of(p))
                if second(j):
                    o = o + o_ref[dst]
                o_ref[dst] = o
        return tuple(states)

    assert nc % cpt == 0
    trips = nc // cpt
    states = tuple(init_state(p, d) for p, d in units)
    if trips == 1:
        states = scan_body(lambda j: 2 * j > nc - 1, 0, states)
    else:
        assert trips % 2 == 0
        states = lax.fori_loop(0, trips // 2, functools.partial(scan_body, lambda j: False), states)
        states = lax.fori_loop(trips // 2, trips, functools.partial(scan_body, lambda j: True), states)
    for ui, (p, d) in enumerate(units):
        sfin_ref[0, d, 2 * p] = states[ui][:RET_DIM, :RET_DIM]
        sfin_ref[0, d, 2 * p + 1] = states[ui][RET_DIM:, RET_DIM:]


def _ret_call(u, theta, s0bd, layer=0):
    b, l, _ = u.shape
    nc = l // RET_CHUNK
    has_s0 = s0bd is not None
    cpt = 8 if nc >= 8 else nc
    npairs = max(1, 8 // cpt)
    ngrp = RET_HEADS // 2 // npairs
    w = npairs * LANES
    st_block = (2, 2 * npairs, RET_DIM, RET_DIM)
    if not has_s0:
        s0bd = jnp.zeros((1,) + st_block, F32)
        s0_spec = pl.BlockSpec((1,) + st_block, lambda g, i: (0, 0, 0, 0, 0))
    else:
        s0_spec = pl.BlockSpec((1, None) + st_block, lambda g, i: (i, layer, 0, g, 0, 0))
    th_lane = jnp.repeat(theta, RET_DIM, axis=1).reshape(2, 1, W_RET)
    th_bcast = jnp.broadcast_to(theta[:, :, None], (2, RET_HEADS, LANES))
    col = lambda c0: pl.BlockSpec((1, l, w), lambda g, i: (i, 0, c0 // w + g))
    o, sfin = pl.pallas_call(
        functools.partial(_ret_kernel, nc=nc, cpt=cpt, npairs=npairs, has_s0=has_s0),
        grid=(ngrp, b),
        in_specs=[col(COL_QR), col(COL_KR), col(COL_VR),
                  pl.BlockSpec((2, 1, w), lambda g, i: (0, 0, g)),
                  pl.BlockSpec((2, RET_HEADS, LANES), lambda g, i: (0, 0, 0)),
                  s0_spec],
        out_specs=[pl.BlockSpec((1, l, w), lambda g, i: (i, 0, g)),
                   pl.BlockSpec((1,) + st_block, lambda g, i: (i, 0, g, 0, 0))],
        out_shape=[jax.ShapeDtypeStruct((b, l, W_RET), F32),
                   jax.ShapeDtypeStruct((b, 2, RET_HEADS, RET_DIM, RET_DIM), F32)],
        scratch_shapes=[pltpu.VMEM((npairs, 2, _TAB_ROWS, LANES), F32)],
        compiler_params=_cparams(("arbitrary", "arbitrary")),
        name="retention_s0" if has_s0 else "retention",
    )(u, u, u, th_lane, th_bcast, s0bd)
    return o, sfin


def _filter_positions(l):
    f32 = np.float32
    t = np.linspace(0.0, 1.0, l, dtype=f32)[:, None]
    w = (f32(2.0 * math.pi) * np.arange(l, dtype=f32)[:, None] / f32(l)).astype(f32)
    f = np.linspace(1e-4, HY_BANDS - 1, HY_BANDS, dtype=f32)[None, :]
    z = np.concatenate([t, np.cos(f * w), -np.sin(f * w)], axis=-1).astype(f32)
    z = np.pad(z, ((0, 0), (0, 32 - HY_POS_FEAT)))
    return jnp.asarray(np.concatenate([z, z[:1], z[1:][::-1]], axis=0))


def _hyena_deltas():
    max_decay = math.log(HY_DECAY_TARGET) / HY_FAST_PCT
    min_decay = math.log(HY_DECAY_TARGET) / HY_SLOW_PCT
    return jnp.asarray(np.abs(np.linspace(min_decay, max_decay, W_HY, dtype=np.float32))[None, :])


def _filter_hidden(z_ref, w1_ref, b1_ref, fr_ref):
    pre = jnp.dot(z_ref[...], w1_ref[0], precision=HIGHEST, preferred_element_type=F32) + b1_ref[0]
    return jnp.sin(fr_ref[0] * pre)


def _filter_raw(hid, w2f, w2b, tp, dl, row0, l):
    win = jnp.exp(-tp * dl)
    row = row0 + lax.broadcasted_iota(jnp.int32, win.shape, 0)
    hf = jnp.dot(hid, w2f, precision=HIGHEST, preferred_element_type=F32) * win
    hb = jnp.dot(hid, w2b, precision=HIGHEST, preferred_element_type=F32) * win
    hf = jnp.where(row < l, hf, 0.0)
    hb = jnp.where((row > l) | (row == 0), hb, 0.0)
    return hf + hb, jnp.sum(jnp.abs(hf) + jnp.abs(hb), axis=0, keepdims=True)


def _with_skip(g, skip):
    row = lax.broadcasted_iota(jnp.int32, g.shape, 0)
    return g + jnp.where(row == 0, skip, 0.0)


def _filt_ctx_kernel(z_ref, w1_ref, b1_ref, fr_ref, w2_ref, dl_ref, sk_ref, fh_ref, fl_ref, g_ref):
    hid = _filter_hidden(z_ref, w1_ref, b1_ref, fr_ref)
    tp = z_ref[:, 0:1]
    for o in range(2):
        w2f = w2_ref[0, :, (2 * o) * W_HY:(2 * o + 1) * W_HY]
        w2b = w2_ref[0, :, (2 * o + 1) * W_HY:(2 * o + 2) * W_HY]
        raw, nrm = _filter_raw(hid, w2f, w2b, tp, dl_ref[...], 0, z_ref.shape[0] // 2)
        g = _with_skip(raw / nrm, sk_ref[0, pl.ds(o, 1), :])
        g_ref[0, o] = _dot3c(fh_ref[...], fl_ref[...], g)


def _ctx_dft_tables(l):
    n = 2 * l
    k = np.arange(n)[:, None]
    t = np.arange(l)[None, :]
    ang = 2.0 * np.pi * k * t / n
    c, s = np.cos(ang), np.sin(ang)
    fwd = np.block([[c, s], [-s, c]])
    inv = np.block([[c.T, -s.T], [s.T, c.T]])
    n_all = np.arange(n)[None, :]
    angg = 2.0 * np.pi * k * n_all / n
    filt = np.concatenate([np.cos(angg), -np.sin(angg)], axis=0) / n
    return _split_np(fwd), _split_np(inv), _split_np(filt)


def _filt_ctx_call(l, w1, b1, freq, w2, skip):
    n = 2 * l
    z_ext = _filter_positions(l)
    _, _, (fh, fl) = _ctx_dft_tables(l)
    w1p = jnp.pad(w1, ((0, 0), (0, 32 - HY_POS_FEAT), (0, 0)))
    lay = lambda *shape: pl.BlockSpec((1,) + shape, lambda d: (d,) + (0,) * len(shape))
    full = lambda a: pl.BlockSpec(a.shape, lambda d: (0,) * a.ndim)
    dl = _hyena_deltas()
    return pl.pallas_call(
        _filt_ctx_kernel,
        grid=(DEPTH,),
        in_specs=[full(z_ext), lay(32, HY_FILT_HID), lay(1, HY_FILT_HID), lay(1, HY_FILT_HID),
                  lay(HY_FILT_HID, 4 * W_HY), full(dl), lay(2, W_HY), full(fh), full(fl)],
        out_specs=pl.BlockSpec((1, 2, 2 * n, W_HY), lambda d: (d, 0, 0, 0)),
        out_shape=jax.ShapeDtypeStruct((DEPTH, 2, 2 * n, W_HY), F32),
        compiler_params=_cparams(("arbitrary",)),
        name="hyena_filter_ctx",
    )(z_ext, w1p, b1.reshape(DEPTH, 1, -1), freq.reshape(DEPTH, 1, -1), w2, dl, skip, fh, fl)


def _lat_dft_tables():
    ka = np.arange(NA)[:, None]
    b = np.arange(NB)[:, None, None]
    kb = np.arange(NB)[:, None]
    bb = np.arange(NB)[None, :]
    a_half = np.arange(NA // 2)[None, :]
    a_full = np.arange(NA)[None, :]
    phi = 2.0 * np.pi * (ka * a_half / NA + b * ka / LAT_N)
    c, s = np.cos(phi), np.sin(phi)
    a_fwd = np.concatenate([c, s], axis=2)
    ct, st = np.swapaxes(c, 1, 2), np.swapaxes(s, 1, 2)
    a_inv = np.concatenate([ct, st], axis=2)
    phig = 2.0 * np.pi * (ka * a_full / NA + b * ka / LAT_N)
    a_flt = np.concatenate([np.cos(phig), -np.sin(phig)], axis=1) / LAT_N
    ang = 2.0 * np.pi * kb * bb / NB
    c2, s2 = np.cos(ang), np.sin(ang)
    f_fwd = np.block([[c2, s2], [-s2, c2]])
    f_inv = np.block([[c2, -s2], [s2, c2]])
    return (_split_np(a_fwd), _split_np(a_inv), _split_np(a_flt), _split_np(f_fwd), _split_np(f_inv))


def _stage_b_rows(ka):
    re = pl.ds(ka, NB, stride=Y_PITCH)
    im = pl.ds(NA + ka, NB, stride=Y_PITCH)
    return re, im


def _filt_lat_kernel(z_ref, w1_ref, b1_ref, fr_ref, w2f_ref, w2b_ref, dl_ref, sk_ref, ah_ref, f2h_ref,
                     g_ref, hid_ref, gt_ref, y_ref):
    step = pl.program_id(1)
    rch = 1024
    nch = LAT_N // rch
    rows_of = lambda i: pl.ds(pl.multiple_of(i * rch, rch), rch)

    @pl.when(step == 0)
    def _():
        def hid_chunk(i, carry):
            r = rows_of(i)
            pre = jnp.dot(z_ref[r, :], w1_ref[0], precision=HIGHEST, preferred_element_type=F32)
            hid_ref[r, :] = jnp.sin(fr_ref[0] * (pre + b1_ref[0]))
            return carry

        lax.fori_loop(0, nch, hid_chunk, 0)

    w2f_hl = _split(w2f_ref[0])
    w2b_hl = _split(w2b_ref[0])

    def raw_chunk(w2_hl, i, nrm):
        r = rows_of(i)
        hh, hl = _split(hid_ref[r, :])
        h = _dot(hh, w2_hl[0]) + _dot(hl, w2_hl[0]) + _dot(hh, w2_hl[1])
        h = h * jnp.exp(-z_ref[r, 0:1] * dl_ref[...])
        row = i * rch + lax.broadcasted_iota(jnp.int32, h.shape, 0)
        h = jnp.where(row == LAT_L, 0.0, h)
        for s in range(rch // NB):
            slab = pl.ds(pl.multiple_of((i * (rch // NB) + s) * X_PITCH, 8), NB)
            gt_ref[slab, :] = h[s * NB:(s + 1) * NB]
        return nrm + jnp.sum(jnp.abs(h), axis=0, keepdims=True)

    nrm = lax.fori_loop(0, nch // 2, functools.partial(raw_chunk, w2f_hl), jnp.zeros((1, LANES), F32))
    nrm = lax.fori_loop(nch // 2, nch, functools.partial(raw_chunk, w2b_hl), nrm)
    hh, hl = _split(hid_ref[0:8, :])
    hb0 = _dot(hh, w2b_hl[0]) + _dot(hl, w2b_hl[0]) + _dot(hh, w2b_hl[1])
    hb0 = hb0 * jnp.exp(-z_ref[0:8, 0:1] * dl_ref[...])
    hb0 = jnp.where(lax.broadcasted_iota(jnp.int32, hb0.shape, 0) == 0, hb0, 0.0)
    gt_ref[0:8, :] = gt_ref[0:8, :] + hb0
    nrm = nrm + jnp.sum(jnp.abs(hb0), axis=0, keepdims=True)

    def norm_slab(a, carry):
        slab = pl.ds(pl.multiple_of(a * X_PITCH, 8), NB)
        gt_ref[slab, :] = gt_ref[slab, :] / nrm
        return carry

    lax.fori_loop(0, NA, norm_slab, 0, unroll=8)
    order = step // (W_HY // LANES)
    gt_ref[0:8, :] = _with_skip(gt_ref[0:8, :], sk_ref[0, pl.ds(order, 1), :])

    def stage_a(b, carry):
        rows = gt_ref[pl.ds(b, NA, stride=X_PITCH), :]
        y_ref[pl.ds(pl.multiple_of(b * Y_PITCH, 8), 2 * NA), :] = _dot1c(ah_ref[b], rows)
        return carry

    lax.fori_loop(0, NB, stage_a, 0, unroll=32)

    def stage_b(j, carry):
        ka = 2 * j
        re, im = _stage_b_rows(ka)
        re1, im1 = _stage_b_rows(ka + 1)
        z = jnp.concatenate([jnp.concatenate([y_ref[re, :], y_ref[im, :]], axis=0),
                             jnp.concatenate([y_ref[re1, :], y_ref[im1, :]], axis=0)], axis=1)
        x = _dot1c(f2h_ref[...], z)
        g0 = pl.multiple_of(ka * 2 * NB, 2 * NB)
        g_ref[0, 0, pl.ds(g0, 2 * NB), :] = x[:, :LANES]
        g_ref[0, 0, pl.ds(g0 + 2 * NB, 2 * NB), :] = x[:, LANES:]
        return carry

    lax.fori_loop(0, NA // 2, stage_b, 0, unroll=4)


def _filt_lat_call(w1, b1, freq, w2, skip):
    z_ext = _filter_positions(LAT_L)
    _, _, (ah, _), (f2h, _), _ = _lat_dft_tables()
    w1p = jnp.pad(w1, ((0, 0), (0, 32 - HY_POS_FEAT), (0, 0)))
    nct = W_HY // LANES
    one = pl.Buffered(1)
    lay = lambda *shape: pl.BlockSpec((1,) + shape, lambda d, s: (d,) + (0,) * len(shape))
    full = lambda a: pl.BlockSpec(a.shape, lambda d, s: (0,) * a.ndim, pipeline_mode=one)
    dl = _hyena_deltas()
    return pl.pallas_call(
        _filt_lat_kernel,
        grid=(DEPTH, 2 * nct),
        in_specs=[full(z_ext), lay(32, HY_FILT_HID), lay(1, HY_FILT_HID), lay(1, HY_FILT_HID),
                  pl.BlockSpec((1, HY_FILT_HID, LANES), lambda d, s: (d, 0, (s // nct) * 2 * nct + s % nct)),
                  pl.BlockSpec((1, HY_FILT_HID, LANES),
                               lambda d, s: (d, 0, (s // nct) * 2 * nct + nct + s % nct)),
                  pl.BlockSpec((1, LANES), lambda d, s: (0, s % nct)),
                  pl.BlockSpec((1, 2, LANES), lambda d, s: (d, 0, s % nct)),
                  full(ah), full(f2h)],
        out_specs=pl.BlockSpec((1, 1, NA * 2 * NB, LANES), lambda d, s: (d, s, 0, 0)),
        out_shape=jax.ShapeDtypeStruct((DEPTH, 2 * nct, NA * 2 * NB, LANES), F32),
        scratch_shapes=[pltpu.VMEM((LAT_N, HY_FILT_HID), F32), pltpu.VMEM((NA * X_PITCH, LANES), F32),
                        pltpu.VMEM((NB * Y_PITCH, LANES), F32)],
        compiler_params=_cparams(("arbitrary", "arbitrary")),
        name="hyena_filter_lat",
    )(z_ext, w1p, b1.reshape(DEPTH, 1, -1), freq.reshape(DEPTH, 1, -1), w2, w2, dl, skip, ah, f2h)


def _short_conv_rows(ref, bi, r0, rows, first, last, w):
    total = ref.shape[1]
    cur = ref[bi, pl.ds(r0, rows), :]
    before = ref[bi, pl.ds(jnp.maximum(r0 - 1, 0), 1), :]
    after = ref[bi, pl.ds(jnp.minimum(r0 + rows, total - 1), 1), :]
    before = jnp.where(first, 0.0, before)
    after = jnp.where(last, 0.0, after)
    rid = lax.broadcasted_iota(jnp.int32, cur.shape, 0)
    prev = jnp.where(rid == 0, before, pltpu.roll(cur, 1, 0))
    nxt = jnp.where(rid == rows - 1, after, pltpu.roll(cur, rows - 1, 0))
    return prev * w[0:1] + cur * w[1:2] + nxt * w[2:3]


def _short_conv_interior(ref, bi, r0, rows, w):
    prev = ref[bi, pl.ds(r0 - 1, rows), :]
    cur = ref[bi, pl.ds(r0, rows), :]
    nxt = ref[bi, pl.ds(r0 + 1, rows), :]
    return prev * w[0:1] + cur * w[1:2] + nxt * w[2:3]


def _cmul(xr, xi, gr, gi):
    return xr * gr - xi * gi, xr * gi + xi * gr


def _hy_ctx_kernel(v_ref, x1_ref, x2_ref, cw_ref, g_ref, fh_ref, ih_ref, o_ref):
    l = v_ref.shape[1]
    n = 2 * l

    def sc(ref, bi, grp):
        w = cw_ref[:, grp * W_HY:(grp + 1) * W_HY]
        return _short_conv_rows(ref, bi, 0, l, True, True, w)

    def conv(zr, zi, order):
        x = _dot1c(fh_ref[...], jnp.concatenate([zr, zi], axis=0))
        pr, pi = _cmul(x[:n], x[n:], g_ref[order, :n], g_ref[order, n:])
        y = _dot1c(ih_ref[...], jnp.concatenate([pr, pi], axis=0))
        return y[:l], y[l:]

    yr, yi = conv(sc(v_ref, 0, 0), sc(v_ref, 1, 0), 0)
    yr, yi = conv(sc(x1_ref, 0, 1) * yr, sc(x1_ref, 1, 1) * yi, 1)
    o_ref[0] = sc(x2_ref, 0, 2) * yr
    o_ref[1] = sc(x2_ref, 1, 2) * yi


def _hy_ctx_call(u, conv_w, g_spec, layer):
    b, l, _ = u.shape
    (fh, _), (ih, _), _ = _ctx_dft_tables(l)
    grp = lambda g: pl.BlockSpec((2, l, W_HY), lambda i: (i, 0, g))
    full = lambda a: pl.BlockSpec(a.shape, lambda i: (0,) * a.ndim)
    return pl.pallas_call(
        _hy_ctx_kernel,
        grid=(b // 2,),
        in_specs=[grp(0), grp(1), grp(2), full(conv_w),
                  pl.BlockSpec((None,) + g_spec.shape[1:], lambda i: (layer, 0, 0, 0)), full(fh), full(ih)],
        out_specs=pl.BlockSpec((2, l, W_HY), lambda i: (i, 0, 0)),
        out_shape=jax.ShapeDtypeStruct((b, l, W_HY), F32),
        compiler_params=_cparams(("parallel",)),
        name="hyena_ctx",
    )(u, u, u, conv_w, g_spec, fh, ih)


def _hy_lat_kernel(z_ref, m_ref, cw_ref, g_ref, af_ref, ai_ref, f2_ref, f3_ref, o_ref, xr_scr, xi_scr,
                   y_scr, *, conv_in):
    x_scr = (xr_scr, xi_scr)
    na_half = NA // 2
    w_in = cw_ref[0] if conv_in else None
    w_mul = cw_ref[1]

    def conv_slab(ref, bi, a, w):
        if isinstance(a, int):
            return _short_conv_rows(ref, bi, a * NB, NB, a == 0, a == na_half - 1, w)
        return _short_conv_interior(ref, bi, pl.multiple_of(a * NB, NB), NB, w)

    def edges_then_interior(body):
        body(0, 0)
        body(na_half - 1, 0)
        lax.fori_loop(1, na_half - 1, body, 0, unroll=2)

    def load_in(a, carry):
        for bi in range(2):
            if conv_in:
                val = conv_slab(z_ref, bi, a, w_in)
            else:
                val = z_ref[bi, pl.ds(pl.multiple_of(a * NB, NB), NB), :]
            x_scr[bi][pl.ds(pl.multiple_of(a * X_PITCH, 8), NB), :] = val
        return carry

    edges_then_interior(load_in)

    def stage_a(b, carry):
        zr = xr_scr[pl.ds(b, na_half, stride=X_PITCH), :]
        zi = xi_scr[pl.ds(b, na_half, stride=X_PITCH), :]
        rhs = jnp.concatenate([jnp.concatenate([zr, zi], axis=0), jnp.concatenate([zi, -zr], axis=0)],
                              axis=1)
        y = _dot1c(af_ref[b], rhs)
        r0 = pl.multiple_of(b * Y_PITCH, 8)
        y_scr[pl.ds(r0, NA), :] = y[:, :LANES]
        y_scr[pl.ds(r0 + NA, NA), :] = y[:, LANES:]
        return carry

    lax.fori_loop(0, NB, stage_a, 0, unroll=64)

    def spectrum_product(j):
        ka = 2 * j
        re, im = _stage_b_rows(ka)
        re1, im1 = _stage_b_rows(ka + 1)
        z = jnp.concatenate([jnp.concatenate([y_scr[re, :], y_scr[im, :]], axis=0),
                             jnp.concatenate([y_scr[re1, :], y_scr[im1, :]], axis=0)], axis=1)
        x = _dot1c(f2_ref[...], z)
        g0 = pl.multiple_of(ka * 2 * NB, 2 * NB)
        gr = jnp.concatenate([g_ref[pl.ds(g0, NB), :], g_ref[pl.ds(g0 + 2 * NB, NB), :]], axis=1)
        gi = jnp.concatenate([g_ref[pl.ds(g0 + NB, NB), :], g_ref[pl.ds(g0 + 3 * NB, NB), :]], axis=1)
        pr, pi = _cmul(x[:NB], x[NB:], gr, gi)
        return jnp.concatenate([pr, pi], axis=0).astype(BF16)

    def inverse_b(j, prod):
        ka = 2 * j
        re, im = _stage_b_rows(ka)
        re1, im1 = _stage_b_rows(ka + 1)
        u = _dot(f3_ref[...], prod)
        y_scr[re, :] = u[:NB, :LANES]
        y_scr[im, :] = u[NB:, :LANES]
        y_scr[re1, :] = u[:NB, LANES:]
        y_scr[im1, :] = u[NB:, LANES:]

    def stage_b(j, prod):
        nxt = spectrum_product(j + 1)
        inverse_b(j, prod)
        return nxt

    last = lax.fori_loop(0, NA // 2 - 1, stage_b, spectrum_product(0), unroll=4)
    inverse_b(NA // 2 - 1, last)

    def stage_c(b, carry):
        r0 = pl.multiple_of(b * Y_PITCH, 8)
        ur = y_scr[pl.ds(r0, NA), :]
        ui = y_scr[pl.ds(r0 + NA, NA), :]
        rhs = jnp.concatenate([jnp.concatenate([ur, -ui], axis=0), jnp.concatenate([ui, ur], axis=0)],
                              axis=1)
        y = _dot1c(ai_ref[b], rhs)
        xr_scr[pl.ds(b, na_half, stride=X_PITCH), :] = y[:, :LANES]
        xi_scr[pl.ds(b, na_half, stride=X_PITCH), :] = y[:, LANES:]
        return carry

    lax.fori_loop(0, NB, stage_c, 0, unroll=64)

    def store_out(a, carry):
        for bi in range(2):
            mul = conv_slab(m_ref, bi, a, w_mul)
            o_ref[bi, pl.ds(pl.multiple_of(a * NB, NB), NB), :] = (
                x_scr[bi][pl.ds(pl.multiple_of(a * X_PITCH, 8), NB), :] * mul)
        return carry

    edges_then_interior(store_out)


def _hy_lat_call(src, src_col, u, mul_col, conv_w2, g_spec, layer, order, *, conv_in):
    b, l, _ = u.shape
    nct = W_HY // LANES
    (af, _), (ai, _), _, (f2, _), (f3, _) = _lat_dft_tables()
    one = pl.Buffered(1)
    blk = lambda col: pl.BlockSpec((2, l, LANES), lambda c, p: (p, 0, col + c))
    const = lambda a: pl.BlockSpec(a.shape, lambda c, p: (0,) * a.ndim, pipeline_mode=one)
    return pl.pallas_call(
        functools.partial(_hy_lat_kernel, conv_in=conv_in),
        grid=(nct, b // 2),
        in_specs=[blk(src_col), blk(mul_col),
                  pl.BlockSpec((2, 3, LANES), lambda c, p: (0, 0, c)),
                  pl.BlockSpec((None, None, NA * 2 * NB, LANES), lambda c, p: (layer, order * nct + c, 0, 0)),
                  const(af), const(ai), const(f2), const(f3)],
        out_specs=pl.BlockSpec((2, l, LANES), lambda c, p: (p, 0, c)),
        out_shape=jax.ShapeDtypeStruct((b, l, W_HY), F32),
        scratch_shapes=[pltpu.VMEM(((NA // 2) * X_PITCH, LANES), F32),
                        pltpu.VMEM(((NA // 2) * X_PITCH, LANES), F32),
                        pltpu.VMEM((NB * Y_PITCH, LANES), F32)],
        compiler_params=_cparams(("arbitrary", "arbitrary")),
        name="hyena_lat_conv_in" if conv_in else "hyena_lat",
    )(src, u, conv_w2, g_spec, af, ai, f2, f3)


def _rope_tables(l):
    f32 = np.float32
    rows = l // GRID_W
    row = np.repeat(np.arange(rows), GRID_W).astype(f32)
    col = np.tile(np.arange(GRID_W), rows).astype(f32)
    quarter = HEAD_DIM // 4
    inv = np.power(f32(ROPE_BASE), -np.arange(quarter, dtype=f32) / f32(quarter)).astype(f32)
    ang = np.concatenate([row[:, None] * inv, col[:, None] * inv], axis=-1).astype(f32)
    cos, sin = np.cos(ang), np.sin(ang)
    q = quarter
    cos_h = np.concatenate([cos[:, :q], cos[:, :q], cos[:, q:], cos[:, q:]], axis=-1)
    sin_h = np.concatenate([-sin[:, :q], sin[:, :q], -sin[:, q:], sin[:, q:]], axis=-1)
    return jnp.asarray(np.tile(cos_h, (1, 2))), jnp.asarray(np.tile(sin_h, (1, 2)))


def kernel(x_prompt, x_sample, c, cache_k, cache_v, state_ret, c_ctx, norm_w, w_mod, b_mod, w_in, hy_conv,
           hy_filt_w1, hy_filt_b1, hy_filt_freq, hy_filt_w2, hy_skip, attn_sink, ret_theta, ret_gn,
           w_branch_a, w_branch_b, w_branch_c, w_merge, b_merge, w_out, final_norm_w):
    d = D_MODEL
    bc, lc, _ = x_prompt.shape
    bl, ll, _ = x_sample.shape
    assert ll == LAT_L and bc % 2 == 0 and bl % 2 == 0
    past = cache_k.shape[2]

    cond = jnp.zeros((16, d), F32).at[:bl].set(c).at[bl].set(c_ctx)
    mod = _mod_call(cond, w_mod, b_mod)

    g_ctx = _filt_ctx_call(lc, hy_filt_w1, hy_filt_b1, hy_filt_freq, hy_filt_w2, hy_skip)
    g_lat = _filt_lat_call(hy_filt_w1, hy_filt_b1, hy_filt_freq, hy_filt_w2, hy_skip)

    cos_t, sin_t = _rope_tables(ll)
    w_in_b = w_in.astype(BF16)
    wm_b = w_merge.astype(BF16)
    wa_b = w_branch_a.astype(BF16)
    wb_b = w_branch_b.astype(BF16)
    wc_b = w_branch_c.astype(BF16)
    wo_b = w_out.astype(BF16)
    fnw = final_norm_w.reshape(1, d)
    k_ctx = cache_k.reshape(bl, DEPTH, past, W_KV)
    v_ctx = cache_v.reshape(bl, DEPTH, past, W_KV)
    hy_cols = COL_HY // LANES
    nct = W_HY // LANES

    xp, xs = x_prompt, x_sample
    ks_out, vs_out, ss_out = [], [], []
    for l in range(DEPTH):
        final = l == DEPTH - 1
        nw = norm_w[l].reshape(1, d)
        bm = b_merge[l].reshape(1, -1)
        gn = ret_gn[l].reshape(1, W_RET)
        shift, scale, gate = (mod[l, :, i * d:(i + 1) * d][:, None, :] for i in range(3))
        conv_w = hy_conv[l]
        cw = lambda g: conv_w[:, g * W_HY:(g + 1) * W_HY]

        sl = slice(bl, bl + 1)
        u = _in_call(xp, shift[sl], scale[sl], nw, w_in_b[l], cos_t, sin_t, rope=False)
        ya = _hy_ctx_call(u, conv_w, g_ctx, l)
        yb = _attn_ctx_call(u, attn_sink[l])
        yc, sfin = _ret_call(u, ret_theta[l], None)
        res = _out_call(xp, shift[sl], scale[sl], gate[sl], nw, ya, u, yb, yc, gn, wm_b[l], bm, wa_b[l], wb_b[l],
                        wc_b[l], wo_b[l], fnw, final=final)
        xp = res[0]
        if final:
            y_prompt = res[1]
        ks_out.append(u[:, :, COL_KA:COL_KA + W_KV])
        vs_out.append(u[:, :, COL_VA:COL_VA + W_KV])
        ss_out.append(sfin)

        sl = slice(0, bl)
        u = _in_call(xs, shift[sl], scale[sl], nw, w_in_b[l], cos_t, sin_t, rope=True)
        z1 = _hy_lat_call(u, hy_cols, u, hy_cols + nct, jnp.stack([cw(0), cw(1)]), g_lat, l, 0, conv_in=True)
        ya = _hy_lat_call(z1, 0, u, hy_cols + 2 * nct, jnp.stack([cw(2), cw(2)]), g_lat, l, 1, conv_in=False)
        yb = _attn_lat_call(u, k_ctx, v_ctx, attn_sink[l], l)
        yc, _ = _ret_call(u, ret_theta[l], state_ret, l)
        res = _out_call(xs, shift[sl], scale[sl], gate[sl], nw, ya, u, yb, yc, gn, wm_b[l], bm, wa_b[l], wb_b[l],
                        wc_b[l], wo_b[l], fnw, final=final)
        xs = res[0]
        if final:
            y_sample = res[1]

    kv_shape = (bc, DEPTH, lc, ATT_KV_HEADS, HEAD_DIM)
    new_cache_k = jnp.stack(ks_out, axis=1).reshape(kv_shape)
    new_cache_v = jnp.stack(vs_out, axis=1).reshape(kv_shape)
    new_state_ret = jnp.stack(ss_out, axis=1)
    return (y_prompt, y_sample, new_cache_k, new_cache_v, new_state_ret)
```

```python
import functools
import math

import numpy as np
import jax
import jax.numpy as jnp
from jax import lax
from jax.experimental import pallas as pl
from jax.experimental.pallas import tpu as pltpu

F32 = jnp.float32
BF16 = jnp.bfloat16
HIGHEST = lax.Precision.HIGHEST

D_MODEL = 1024
DEPTH = 4
GRID_W = 64
W_HY = 512
HY_BANDS = 8
HY_POS_FEAT = 1 + 2 * HY_BANDS
HY_FILT_HID = 64
HY_DECAY_TARGET = 1e-2
HY_FAST_PCT = 0.3
HY_SLOW_PCT = 1.5
ATT_HEADS = 8
ATT_KV_HEADS = 2
ATT_GROUP = ATT_HEADS // ATT_KV_HEADS
HEAD_DIM = 64
W_ATT = ATT_HEADS * HEAD_DIM
W_KV = ATT_KV_HEADS * HEAD_DIM
ATT_BLOCK = 128
ATT_QB = 8
RET_HEADS = 8
RET_DIM = 64
W_RET = RET_HEADS * RET_DIM
RET_CHUNK = 128
ROPE_BASE = 10000.0
EPS = 1e-6
NEG = -1e30

LANES = 128
MXU_ROWS = 512
IN_PARTS = 2
VMEM_LIMIT = 58 * 1024 * 1024

IN_DIM = 5376
COL_HY = 0
COL_GH = 1536
COL_QA = 2048
COL_GA = 2560
COL_QR = 3072
COL_KR = 3584
COL_VR = 4096
COL_GR = 4608
COL_KA = 5120
COL_VA = 5248
_W_COL = {COL_HY: 0, COL_HY + 512: 512, COL_HY + 1024: 1024, COL_GH: 1536, COL_QA: 2048, COL_KA: 2560,
          COL_VA: 2688, COL_GA: 2816, COL_QR: 3328, COL_KR: 3840, COL_VR: 4352, COL_GR: 4864}

LAT_L = 4096
LAT_N = 2 * LAT_L
NA = 64
NB = 128
Y_PITCH = 136
X_PITCH = 136


def _cparams(sem):
    return pltpu.CompilerParams(dimension_semantics=sem, vmem_limit_bytes=VMEM_LIMIT)


def _split_np(a):
    a32 = np.asarray(a, np.float32)
    hi = a32.astype(BF16)
    lo = (a32 - hi.astype(np.float32)).astype(BF16)
    return jnp.asarray(hi), jnp.asarray(lo)


def _split(x):
    hi = x.astype(BF16)
    lo = (x - hi.astype(F32)).astype(BF16)
    return hi, lo


def _dot(a, b):
    return jnp.dot(a, b, preferred_element_type=F32)


def _dot3c(chi, clo, x):
    xh, xl = _split(x)
    return _dot(chi, xh) + _dot(clo, xh) + _dot(chi, xl)


def _dot1c(chi, x):
    return _dot(chi, x.astype(BF16))


def _silu(x):
    return x * jax.nn.sigmoid(x)


def _mod_kernel(c_ref, w_ref, b_ref, o_ref):
    s = _silu(c_ref[...])
    o_ref[0] = jnp.dot(s, w_ref[0], precision=HIGHEST, preferred_element_type=F32) + b_ref[0]


def _mod_call(cond, w_mod, b_mod):
    rows, d = cond.shape
    n = w_mod.shape[-1]
    tn = 1024
    return pl.pallas_call(
        _mod_kernel,
        grid=(DEPTH, n // tn),
        in_specs=[pl.BlockSpec((rows, d), lambda l, j: (0, 0)),
                  pl.BlockSpec((1, d, tn), lambda l, j: (l, 0, j)),
                  pl.BlockSpec((1, 1, tn), lambda l, j: (l, 0, j))],
        out_specs=pl.BlockSpec((1, rows, tn), lambda l, j: (l, 0, j)),
        out_shape=jax.ShapeDtypeStruct((DEPTH, rows, n), F32),
        compiler_params=_cparams(("arbitrary", "arbitrary")),
        name="adaln_mod",
    )(cond, w_mod, b_mod.reshape(DEPTH, 1, n))


def _modulated(x, nw, scale, shift):
    ms = jnp.mean(x * x, axis=-1, keepdims=True)
    h = x * lax.rsqrt(ms + EPS) * nw
    return h * (1.0 + scale) + shift


def _rope128(x, cos, sin_signed, first_half):
    up = pltpu.roll(x, LANES - 16, 1)
    dn = pltpu.roll(x, 16, 1)
    return x * cos + jnp.where(first_half, up, dn) * sin_signed


def _rows(ref):
    bt, tm, w = ref.shape
    return ref[...].reshape(bt * tm, w)


def _put(ref, c0, val):
    bt, tm, _ = ref.shape
    ref[:, :, c0:c0 + val.shape[1]] = val.reshape(bt, tm, val.shape[1])


def _in_kernel(x_ref, shift_ref, scale_ref, nw_ref, w_ref, cos_ref, sin_ref, o_ref, *, rope):
    x = _rows(x_ref)
    rows = x.shape[0]
    part = rows // IN_PARTS
    hbs = [_modulated(x[r:r + part], nw_ref[...], scale_ref[0], shift_ref[0]).astype(BF16)
           for r in range(0, rows, part)]
    if rope:
        cos = cos_ref[...]
        sin = sin_ref[...]
        lane = lax.broadcasted_iota(jnp.int32, (rows, LANES), 1)
        first_half = (lane % 32) < 16

    def seg(c0, width):
        w0 = _W_COL[c0]
        return jnp.concatenate([_dot(hb, w_ref[:, w0:w0 + width]) for hb in hbs], axis=0)

    def put_rope(c0, val, mul):
        for i in range(val.shape[1] // LANES):
            piece = val[:, i * LANES:(i + 1) * LANES]
            if rope:
                piece = _rope128(piece, cos, sin, first_half)
            if mul is not None:
                piece = piece * mul
            _put(o_ref, c0 + i * LANES, piece)

    for g in range(3):
        _put(o_ref, COL_HY + g * 512, seg(COL_HY + g * 512, 512))
    _put(o_ref, COL_GH, _silu(seg(COL_GH, 512)))
    put_rope(COL_QA, seg(COL_QA, 512), None)
    _put(o_ref, COL_GA, _silu(seg(COL_GA, 512)))
    put_rope(COL_QR, seg(COL_QR, 512), None)
    put_rope(COL_KR, seg(COL_KR, 512), RET_DIM ** -0.5)
    _put(o_ref, COL_VR, seg(COL_VR, 512))
    _put(o_ref, COL_GR, _silu(seg(COL_GR, 512)))
    put_rope(COL_KA, seg(COL_KA, 128), None)
    _put(o_ref, COL_VA, seg(COL_VA, 128))


def _token_tiling(b, l, per_batch):
    if l >= MXU_ROWS:
        return 1, MXU_ROWS
    bt = 1 if per_batch else min(b, MXU_ROWS // l)
    return bt, l


def _in_call(x, shift, scale, nw, w, cos_t, sin_t, *, rope):
    b, l, d = x.shape
    per_batch = shift.shape[0] > 1
    bt, tm = _token_tiling(b, l, per_batch)
    assert not rope or bt == 1
    mod_map = (lambda i, j: (i, 0, 0)) if per_batch else (lambda i, j: (0, 0, 0))
    return pl.pallas_call(
        functools.partial(_in_kernel, rope=rope),
        grid=(b // bt, l // tm),
        in_specs=[pl.BlockSpec((bt, tm, d), lambda i, j: (i, j, 0)),
                  pl.BlockSpec((1, 1, d), mod_map),
                  pl.BlockSpec((1, 1, d), mod_map),
                  pl.BlockSpec((1, d), lambda i, j: (0, 0)),
                  pl.BlockSpec((d, IN_DIM), lambda i, j: (0, 0), pipeline_mode=pl.Buffered(1)),
                  pl.BlockSpec((tm, LANES), lambda i, j: (j, 0)),
                  pl.BlockSpec((tm, LANES), lambda i, j: (j, 0))],
        out_specs=pl.BlockSpec((bt, tm, IN_DIM), lambda i, j: (i, j, 0)),
        out_shape=jax.ShapeDtypeStruct((b, l, IN_DIM), F32),
        compiler_params=_cparams(("parallel", "parallel")),
        name="in_proj_rope" if rope else "in_proj",
    )(x, shift, scale, nw, w, cos_t, sin_t)


def _retention_post(o, gn, gate):
    lane = lax.broadcasted_iota(jnp.int32, (1, LANES), 1)
    lo_head = lane < RET_DIM
    outs = []
    for t in range(W_RET // LANES):
        sl = slice(t * LANES, (t + 1) * LANES)
        ot = o[:, sl]
        o2 = ot * ot
        s_lo = jnp.sum(jnp.where(lo_head, o2, 0.0), axis=-1, keepdims=True)
        s_hi = jnp.sum(jnp.where(lo_head, 0.0, o2), axis=-1, keepdims=True)
        ms = jnp.where(lo_head, s_lo, s_hi) * (1.0 / RET_DIM)
        outs.append(ot * lax.rsqrt(ms + EPS) * gn[:, sl] * gate[:, sl])
    return jnp.concatenate(outs, axis=1)


def _out_kernel(x_ref, shift_ref, scale_ref, gate_ref, nw_ref, ya_ref, gh_ref, yb_ref, yc_ref, gr_ref, gn_ref,
                wm_ref, bm_ref, wa_ref, wb_ref, wc_ref, wo_ref, fnw_ref, *out_refs, final):
    x = _rows(x_ref)
    d = x.shape[1]
    hb = _modulated(x, nw_ref[...], scale_ref[0], shift_ref[0]).astype(BF16)
    branches = (_rows(ya_ref) * _rows(gh_ref), _rows(yb_ref),
                _retention_post(_rows(yc_ref), gn_ref[...], _rows(gr_ref)))
    merged = None
    for i, (y, w_ref) in enumerate(zip(branches, (wa_ref, wb_ref, wc_ref))):
        g = jax.nn.sigmoid(_dot(hb, wm_ref[:, i * d:(i + 1) * d]) + bm_ref[:, i * d:(i + 1) * d])
        term = g * _dot(y.astype(BF16), w_ref[...])
        merged = term if merged is None else merged + term
    out = _dot(merged.astype(BF16), wo_ref[...])
    xn = x + gate_ref[0] * out
    _put(out_refs[0], 0, xn)
    if final:
        ms = jnp.mean(xn * xn, axis=-1, keepdims=True)
        _put(out_refs[1], 0, xn * lax.rsqrt(ms + EPS) * fnw_ref[...])


def _out_call(x, shift, scale, gate, nw, ya, u, yb, yc, gn, wm, bm, wa, wb, wc, wo, fnw, *, final):
    b, l, d = x.shape
    per_batch = shift.shape[0] > 1
    bt, tm = _token_tiling(b, l, per_batch)
    mod_map = (lambda i, j: (i, 0, 0)) if per_batch else (lambda i, j: (0, 0, 0))
    tok = lambda w: pl.BlockSpec((bt, tm, w), lambda i, j: (i, j, 0))
    full = lambda a: pl.BlockSpec(a.shape, lambda i, j: (0,) * a.ndim, pipeline_mode=pl.Buffered(1))
    n_out = 2 if final else 1
    res = pl.pallas_call(
        functools.partial(_out_kernel, final=final),
        grid=(b // bt, l // tm),
        in_specs=[tok(d), pl.BlockSpec((1, 1, d), mod_map), pl.BlockSpec((1, 1, d), mod_map),
                  pl.BlockSpec((1, 1, d), mod_map), full(nw), tok(W_HY),
                  pl.BlockSpec((bt, tm, W_HY), lambda i, j: (i, j, COL_GH // W_HY)), tok(W_ATT), tok(W_RET),
                  pl.BlockSpec((bt, tm, W_RET), lambda i, j: (i, j, COL_GR // W_RET)), full(gn),
                  full(wm), full(bm), full(wa), full(wb), full(wc), full(wo), full(fnw)],
        out_specs=[tok(d)] * n_out,
        out_shape=[jax.ShapeDtypeStruct((b, l, d), F32)] * n_out,
        compiler_params=_cparams(("parallel", "parallel")),
        name="merge_out_final" if final else "merge_out",
    )(x, shift, scale, gate, nw, ya, u, yb, yc, u, gn, wm, bm, wa, wb, wc, wo, fnw)
    return res


_NT = (((1,), (1,)), ((), ()))


_TN = (((0,), (0,)), ((), ()))
LOG2E = 1.4426950408889634
Q_SCALE = (HEAD_DIM ** -0.5) * LOG2E


def _attn_scores_t(q, kh, kv):
    h0 = kv * ATT_GROUP
    qs = jnp.concatenate([q[:, (h0 + g) * HEAD_DIM:(h0 + g + 1) * HEAD_DIM] for g in range(ATT_GROUP)],
                         axis=0).astype(BF16)
    return lax.dot_general(kh, qs, _NT, preferred_element_type=F32)


def _attn_finish_t(sink_ref, s, vh, g_ref, o_ref, kv, row0=0):
    tk, cols = s.shape
    t = cols // ATT_GROUP
    h0 = kv * ATT_GROUP
    head = lax.broadcasted_iota(jnp.int32, (1, cols), 1) // t
    sink = jnp.full((1, cols), sink_ref[h0], F32)
    for g in range(1, ATT_GROUP):
        sink = jnp.where(head == g, sink_ref[h0 + g], sink)
    sink = sink * LOG2E
    m = jnp.maximum(jnp.max(s, axis=0, keepdims=True), sink)
    p = jnp.exp2(s - m).astype(BF16)
    v_ext = jnp.concatenate([vh, jnp.ones((tk, HEAD_DIM), BF16)], axis=1)
    o_ext = lax.dot_general(v_ext, p, _TN, preferred_element_type=F32)
    denom = o_ext[HEAD_DIM:HEAD_DIM + 1] + jnp.exp2(sink - m)
    o = o_ext[:HEAD_DIM] / denom
    for gp in range(ATT_GROUP // 2):
        pair = jnp.concatenate([o[:, (2 * gp) * t:(2 * gp + 1) * t], o[:, (2 * gp + 1) * t:(2 * gp + 2) * t]],
                               axis=0)
        c0 = (h0 + 2 * gp) * HEAD_DIM
        o_ref[0, row0:row0 + t, c0:c0 + 2 * HEAD_DIM] = pair.T * g_ref[0, row0:row0 + t, c0:c0 + 2 * HEAD_DIM]


def _attn_ctx_kernel(sink_ref, q_ref, k_ref, v_ref, g_ref, o_ref):
    q = q_ref[0] * Q_SCALE
    k = k_ref[0].astype(BF16)
    v = v_ref[0].astype(BF16)
    scores = [_attn_scores_t(q, k[:, kv * HEAD_DIM:(kv + 1) * HEAD_DIM], kv) for kv in range(ATT_KV_HEADS)]
    for kv in range(ATT_KV_HEADS):
        _attn_finish_t(sink_ref, scores[kv], v[:, kv * HEAD_DIM:(kv + 1) * HEAD_DIM], g_ref, o_ref, kv)


def _attn_ctx_call(u, sink):
    b, l, _ = u.shape
    return pl.pallas_call(
        _attn_ctx_kernel,
        grid=(b,),
        in_specs=[pl.BlockSpec(memory_space=pltpu.SMEM),
                  pl.BlockSpec((1, l, W_ATT), lambda i: (i, 0, COL_QA // W_ATT)),
                  pl.BlockSpec((1, l, W_KV), lambda i: (i, 0, COL_KA // W_KV)),
                  pl.BlockSpec((1, l, W_KV), lambda i: (i, 0, COL_VA // W_KV)),
                  pl.BlockSpec((1, l, W_ATT), lambda i: (i, 0, COL_GA // W_ATT))],
        out_specs=pl.BlockSpec((1, l, W_ATT), lambda i: (i, 0, 0)),
        out_shape=jax.ShapeDtypeStruct((b, l, W_ATT), F32),
        compiler_params=_cparams(("parallel",)),
        name="attn_ctx",
    )(sink, u, u, u, u)


def _attn_lat_kernel(sink_ref, q_ref, kp_ref, kc_ref, kn_ref, vp_ref, vc_ref, vn_ref, kx_ref, vx_ref,
                     g_ref, o_ref):
    j = pl.program_id(1)
    last = pl.num_programs(1) - 1
    b = ATT_BLOCK
    nq = ATT_QB
    bf = lambda ref: ref[0].astype(BF16)
    kx, vx = bf(kx_ref), bf(vx_ref)
    kblk = [bf(kp_ref)] + [kc_ref[0, t * b:(t + 1) * b, :].astype(BF16) for t in range(nq)] + [bf(kn_ref)]
    vblk = [bf(vp_ref)] + [vc_ref[0, t * b:(t + 1) * b, :].astype(BF16) for t in range(nq)] + [bf(vn_ref)]
    keys = [jnp.concatenate(kblk[t:t + 3] + [kx], axis=0) for t in range(nq)]
    vals = [jnp.concatenate(vblk[t:t + 3] + [vx], axis=0) for t in range(nq)]
    cols = ATT_GROUP * b
    c = lax.broadcasted_iota(jnp.int32, (b, cols), 0)
    r = lax.broadcasted_iota(jnp.int32, (b, cols), 1) % b
    ok_prev = [(c >= r) & (j > 0)] + [c >= r] * (nq - 1)
    ok_next = [c <= r] * (nq - 1) + [(c <= r) & (j < last)]

    def band(s, t):
        return jnp.concatenate([jnp.where(ok_prev[t], s[:b], NEG), s[b:2 * b],
                                jnp.where(ok_next[t], s[2 * b:3 * b], NEG), s[3 * b:]], axis=0)

    chains = [(t, kv) for t in range(nq) for kv in range(ATT_KV_HEADS)]
    scores = []
    for t, kv in chains:
        q = q_ref[0, t * b:(t + 1) * b, :] * Q_SCALE
        scores.append(band(_attn_scores_t(q, keys[t][:, kv * HEAD_DIM:(kv + 1) * HEAD_DIM], kv), t))
    for (t, kv), s in zip(chains, scores):
        _attn_finish_t(sink_ref, s, vals[t][:, kv * HEAD_DIM:(kv + 1) * HEAD_DIM], g_ref, o_ref, kv, t * b)


def _attn_lat_call(u, kctx, vctx, sink, layer):
    b, l, _ = u.shape
    nb = l // ATT_BLOCK
    past = kctx.shape[2]
    kcol = COL_KA // W_KV
    vcol = COL_VA // W_KV
    nq = ATT_QB
    prev = lambda col: pl.BlockSpec((1, ATT_BLOCK, W_KV), lambda i, j: (i, jnp.maximum(nq * j - 1, 0), col))
    cur = lambda col: pl.BlockSpec((1, nq * ATT_BLOCK, W_KV), lambda i, j: (i, j, col))
    nxt = lambda col: pl.BlockSpec((1, ATT_BLOCK, W_KV),
                                   lambda i, j: (i, jnp.minimum(nq * j + nq, nb - 1), col))
    ctx = pl.BlockSpec((1, None, past, W_KV), lambda i, j: (i, layer, 0, 0))
    return pl.pallas_call(
        _attn_lat_kernel,
        grid=(b, nb // nq),
        in_specs=[pl.BlockSpec(memory_space=pltpu.SMEM),
                  pl.BlockSpec((1, nq * ATT_BLOCK, W_ATT), lambda i, j: (i, j, COL_QA // W_ATT)),
                  prev(kcol), cur(kcol), nxt(kcol), prev(vcol), cur(vcol), nxt(vcol), ctx, ctx,
                  pl.BlockSpec((1, nq * ATT_BLOCK, W_ATT), lambda i, j: (i, j, COL_GA // W_ATT))],
        out_specs=pl.BlockSpec((1, nq * ATT_BLOCK, W_ATT), lambda i, j: (i, j, 0)),
        out_shape=jax.ShapeDtypeStruct((b, l, W_ATT), F32),
        compiler_params=_cparams(("parallel", "parallel")),
        name="attn_lat",
    )(sink, u, u, u, u, u, u, u, kctx, vctx, u)


def _log_sigmoid(x):
    return jnp.minimum(x, 0.0) - jnp.log1p(jnp.exp(-jnp.abs(x)))


_TAB_DMAT = 0
_TAB_QDEC = 2 * RET_CHUNK
_TAB_KDEC = 3 * RET_CHUNK
_TAB_CDEC = 4 * RET_CHUNK
_TAB_ROWS = 4 * RET_CHUNK + 8


def _ret_kernel(q_ref, k_ref, v_ref, thl_ref, thb_ref, s0_ref, o_ref, sfin_ref, tab_ref, *, nc, cpt,
                npairs, has_s0):
    grp = pl.program_id(0)
    c = RET_CHUNK
    lane = lax.broadcasted_iota(jnp.int32, (1, LANES), 1)
    lo_head = lane < RET_DIM
    dd = lax.broadcasted_iota(jnp.int32, (LANES, LANES), 0)
    ee = lax.broadcasted_iota(jnp.int32, (LANES, LANES), 1)
    same_head = (dd < RET_DIM) == (ee < RET_DIM)
    lanes_of = lambda p: slice(p * LANES, (p + 1) * LANES)

    @pl.when(pl.program_id(1) == 0)
    def _():
        rowf = lax.broadcasted_iota(jnp.int32, (c, LANES), 0).astype(F32)
        ii = lax.broadcasted_iota(jnp.int32, (c, c), 0)
        jj = lax.broadcasted_iota(jnp.int32, (c, c), 1)
        for p in range(npairs):
            for d in range(2):
                lg_lane = _log_sigmoid(thl_ref[d, :, lanes_of(p)])
                dist = (ii - jj) if d == 0 else (jj - ii)
                for hh in range(2):
                    head = 2 * (grp * npairs + p) + hh
                    lg_h = _log_sigmoid(thb_ref[d, pl.ds(head, 1), :])
                    dm = jnp.where(dist >= 0, jnp.exp(lg_h * jnp.maximum(dist, 0).astype(F32)), 0.0)
                    tab_ref[p, d, _TAB_DMAT + hh * c:_TAB_DMAT + (hh + 1) * c, :] = dm
                if d == 0:
                    q_dec = jnp.exp(lg_lane * (rowf + 1.0))
                    k_dec = jnp.exp(lg_lane * (c - 1.0 - rowf))
                else:
                    q_dec = jnp.exp(lg_lane * (c - rowf))
                    k_dec = jnp.exp(lg_lane * rowf)
                tab_ref[p, d, _TAB_QDEC:_TAB_QDEC + c, :] = q_dec
                tab_ref[p, d, _TAB_KDEC:_TAB_KDEC + c, :] = k_dec
                tab_ref[p, d, _TAB_CDEC:_TAB_CDEC + 8, :] = jnp.broadcast_to(
                    jnp.exp(lg_lane * float(c)), (8, LANES))

    def first_level(p, d, r0):
        qc = q_ref[0, pl.ds(r0, c), lanes_of(p)]
        kc = k_ref[0, pl.ds(r0, c), lanes_of(p)]
        vcb = v_ref[0, pl.ds(r0, c), lanes_of(p)].astype(BF16)
        qs = jnp.concatenate([jnp.where(lo_head, qc, 0.0), jnp.where(lo_head, 0.0, qc)], axis=0)
        sc = lax.dot_general(qs.astype(BF16), kc.astype(BF16), _NT, preferred_element_type=F32)
        sc = sc * tab_ref[p, d, _TAB_DMAT:_TAB_DMAT + 2 * c, :]
        kd = kc * tab_ref[p, d, _TAB_KDEC:_TAB_KDEC + c, :]
        upd = jnp.where(same_head, _dot(kd.T.astype(BF16), vcb), 0.0)
        return qc, vcb, sc.astype(BF16), upd

    def second_level(p, d, lvl1, s):
        qc, vcb, scb, upd = lvl1
        pv = _dot(scb, vcb)
        qd = qc * tab_ref[p, d, _TAB_QDEC:_TAB_QDEC + c, :]
        o = _dot(qd.astype(BF16), s.astype(BF16)) + jnp.where(lo_head, pv[:c], pv[c:])
        return o, tab_ref[p, d, _TAB_CDEC:_TAB_CDEC + 1, :] * s + upd

    def init_state(p, d):
        if not has_s0:
            return jnp.zeros((LANES, LANES), F32)
        z = jnp.zeros((RET_DIM, RET_DIM), F32)
        return jnp.concatenate([jnp.concatenate([s0_ref[0, d, 2 * p], z], axis=1),
                                jnp.concatenate([z, s0_ref[0, d, 2 * p + 1]], axis=1)], axis=0)

    units = [(p, d) for p in range(npairs) for d in range(2)]

    def scan_body(second, n, states):
        def row0(d, j):
            idx = n * cpt + j
            return pl.multiple_of((idx if d == 0 else nc - 1 - idx) * c, c)

        lvl = {(p, d, j): first_level(p, d, row0(d, j)) for j in range(cpt) for p, d in units}
        states = list(states)
        for j in range(cpt):
            for ui, (p, d) in enumerate(units):
                o, states[ui] = second_level(p, d, lvl[(p, d, j)], states[ui])
                dst = (0, pl.ds(row0(d, j), c), lanes_of(p))
                if second(j):
                    o = o + o_ref[dst]
                o_ref[dst] = o
        return tuple(states)

    assert nc % cpt == 0
    trips = nc // cpt
    states = tuple(init_state(p, d) for p, d in units)
    if trips == 1:
        states = scan_body(lambda j: 2 * j > nc - 1, 0, states)
    else:
        assert trips % 2 == 0
        states = lax.fori_loop(0, trips // 2, functools.partial(scan_body, lambda j: False), states)
        states = lax.fori_loop(trips // 2, trips, functools.partial(scan_body, lambda j: True), states)
    for ui, (p, d) in enumerate(units):
        sfin_ref[0, d, 2 * p] = states[ui][:RET_DIM, :RET_DIM]
        sfin_ref[0, d, 2 * p + 1] = states[ui][RET_DIM:, RET_DIM:]


def _ret_call(u, theta, s0bd, layer=0):
    b, l, _ = u.shape
    nc = l // RET_CHUNK
    has_s0 = s0bd is not None
    cpt = 8 if nc >= 8 else nc
    npairs = max(1, 8 // cpt)
    ngrp = RET_HEADS // 2 // npairs
    w = npairs * LANES
    st_block = (2, 2 * npairs, RET_DIM, RET_DIM)
    if not has_s0:
        s0bd = jnp.zeros((1,) + st_block, F32)
        s0_spec = pl.BlockSpec((1,) + st_block, lambda g, i: (0, 0, 0, 0, 0))
    else:
        s0_spec = pl.BlockSpec((1, None) + st_block, lambda g, i: (i, layer, 0, g, 0, 0))
    th_lane = jnp.repeat(theta, RET_DIM, axis=1).reshape(2, 1, W_RET)
    th_bcast = jnp.broadcast_to(theta[:, :, None], (2, RET_HEADS, LANES))
    col = lambda c0: pl.BlockSpec((1, l, w), lambda g, i: (i, 0, c0 // w + g))
    o, sfin = pl.pallas_call(
        functools.partial(_ret_kernel, nc=nc, cpt=cpt, npairs=npairs, has_s0=has_s0),
        grid=(ngrp, b),
        in_specs=[col(COL_QR), col(COL_KR), col(COL_VR),
                  pl.BlockSpec((2, 1, w), lambda g, i: (0, 0, g)),
                  pl.BlockSpec((2, RET_HEADS, LANES), lambda g, i: (0, 0, 0)),
                  s0_spec],
        out_specs=[pl.BlockSpec((1, l, w), lambda g, i: (i, 0, g)),
                   pl.BlockSpec((1,) + st_block, lambda g, i: (i, 0, g, 0, 0))],
        out_shape=[jax.ShapeDtypeStruct((b, l, W_RET), F32),
                   jax.ShapeDtypeStruct((b, 2, RET_HEADS, RET_DIM, RET_DIM), F32)],
        scratch_shapes=[pltpu.VMEM((npairs, 2, _TAB_ROWS, LANES), F32)],
        compiler_params=_cparams(("arbitrary", "arbitrary")),
        name="retention_s0" if has_s0 else "retention",
    )(u, u, u, th_lane, th_bcast, s0bd)
    return o, sfin


def _filter_positions(l):
    f32 = np.float32
    t = np.linspace(0.0, 1.0, l, dtype=f32)[:, None]
    w = (f32(2.0 * math.pi) * np.arange(l, dtype=f32)[:, None] / f32(l)).astype(f32)
    f = np.linspace(1e-4, HY_BANDS - 1, HY_BANDS, dtype=f32)[None, :]
    z = np.concatenate([t, np.cos(f * w), -np.sin(f * w)], axis=-1).astype(f32)
    z = np.pad(z, ((0, 0), (0, 32 - HY_POS_FEAT)))
    return jnp.asarray(np.concatenate([z, z[:1], z[1:][::-1]], axis=0))


def _hyena_deltas():
    max_decay = math.log(HY_DECAY_TARGET) / HY_FAST_PCT
    min_decay = math.log(HY_DECAY_TARGET) / HY_SLOW_PCT
    return jnp.asarray(np.abs(np.linspace(min_decay, max_decay, W_HY, dtype=np.float32))[None, :])


def _filter_hidden(z_ref, w1_ref, b1_ref, fr_ref):
    pre = jnp.dot(z_ref[...], w1_ref[0], precision=HIGHEST, preferred_element_type=F32) + b1_ref[0]
    return jnp.sin(fr_ref[0] * pre)


def _filter_raw(hid, w2f, w2b, tp, dl, row0, l):
    win = jnp.exp(-tp * dl)
    row = row0 + lax.broadcasted_iota(jnp.int32, win.shape, 0)
    hf = jnp.dot(hid, w2f, precision=HIGHEST, preferred_element_type=F32) * win
    hb = jnp.dot(hid, w2b, precision=HIGHEST, preferred_element_type=F32) * win
    hf = jnp.where(row < l, hf, 0.0)
    hb = jnp.where((row > l) | (row == 0), hb, 0.0)
    return hf + hb, jnp.sum(jnp.abs(hf) + jnp.abs(hb), axis=0, keepdims=True)


def _with_skip(g, skip):
    row = lax.broadcasted_iota(jnp.int32, g.shape, 0)
    return g + jnp.where(row == 0, skip, 0.0)


def _filt_ctx_kernel(z_ref, w1_ref, b1_ref, fr_ref, w2_ref, dl_ref, sk_ref, fh_ref, fl_ref, g_ref):
    hid = _filter_hidden(z_ref, w1_ref, b1_ref, fr_ref)
    tp = z_ref[:, 0:1]
    for o in range(2):
        w2f = w2_ref[0, :, (2 * o) * W_HY:(2 * o + 1) * W_HY]
        w2b = w2_ref[0, :, (2 * o + 1) * W_HY:(2 * o + 2) * W_HY]
        raw, nrm = _filter_raw(hid, w2f, w2b, tp, dl_ref[...], 0, z_ref.shape[0] // 2)
        g = _with_skip(raw / nrm, sk_ref[0, pl.ds(o, 1), :])
        g_ref[0, o] = _dot3c(fh_ref[...], fl_ref[...], g)


def _ctx_dft_tables(l):
    n = 2 * l
    k = np.arange(n)[:, None]
    t = np.arange(l)[None, :]
    ang = 2.0 * np.pi * k * t / n
    c, s = np.cos(ang), np.sin(ang)
    fwd = np.block([[c, s], [-s, c]])
    inv = np.block([[c.T, -s.T], [s.T, c.T]])
    n_all = np.arange(n)[None, :]
    angg = 2.0 * np.pi * k * n_all / n
    filt = np.concatenate([np.cos(angg), -np.sin(angg)], axis=0) / n
    return _split_np(fwd), _split_np(inv), _split_np(filt)


def _filt_ctx_call(l, w1, b1, freq, w2, skip):
    n = 2 * l
    z_ext = _filter_positions(l)
    _, _, (fh, fl) = _ctx_dft_tables(l)
    w1p = jnp.pad(w1, ((0, 0), (0, 32 - HY_POS_FEAT), (0, 0)))
    lay = lambda *shape: pl.BlockSpec((1,) + shape, lambda d: (d,) + (0,) * len(shape))
    full = lambda a: pl.BlockSpec(a.shape, lambda d: (0,) * a.ndim)
    dl = _hyena_deltas()
    return pl.pallas_call(
        _filt_ctx_kernel,
        grid=(DEPTH,),
        in_specs=[full(z_ext), lay(32, HY_FILT_HID), lay(1, HY_FILT_HID), lay(1, HY_FILT_HID),
                  lay(HY_FILT_HID, 4 * W_HY), full(dl), lay(2, W_HY), full(fh), full(fl)],
        out_specs=pl.BlockSpec((1, 2, 2 * n, W_HY), lambda d: (d, 0, 0, 0)),
        out_shape=jax.ShapeDtypeStruct((DEPTH, 2, 2 * n, W_HY), F32),
        compiler_params=_cparams(("arbitrary",)),
        name="hyena_filter_ctx",
    )(z_ext, w1p, b1.reshape(DEPTH, 1, -1), freq.reshape(DEPTH, 1, -1), w2, dl, skip, fh, fl)


def _lat_dft_tables():
    ka = np.arange(NA)[:, None]
    b = np.arange(NB)[:, None, None]
    kb = np.arange(NB)[:, None]
    bb = np.arange(NB)[None, :]
    a_half = np.arange(NA // 2)[None, :]
    a_full = np.arange(NA)[None, :]
    phi = 2.0 * np.pi * (ka * a_half / NA + b * ka / LAT_N)
    c, s = np.cos(phi), np.sin(phi)
    a_fwd = np.concatenate([c, s], axis=2)
    ct, st = np.swapaxes(c, 1, 2), np.swapaxes(s, 1, 2)
    a_inv = np.concatenate([ct, st], axis=2)
    phig = 2.0 * np.pi * (ka * a_full / NA + b * ka / LAT_N)
    a_flt = np.concatenate([np.cos(phig), -np.sin(phig)], axis=1) / LAT_N
    ang = 2.0 * np.pi * kb * bb / NB
    c2, s2 = np.cos(ang), np.sin(ang)
    f_fwd = np.block([[c2, s2], [-s2, c2]])
    f_inv = np.block([[c2, -s2], [s2, c2]])
    return (_split_np(a_fwd), _split_np(a_inv), _split_np(a_flt), _split_np(f_fwd), _split_np(f_inv))


def _stage_b_rows(ka):
    re = pl.ds(ka, NB, stride=Y_PITCH)
    im = pl.ds(NA + ka, NB, stride=Y_PITCH)
    return re, im


def _filt_lat_kernel(z_ref, w1_ref, b1_ref, fr_ref, w2f_ref, w2b_ref, dl_ref, sk_ref, ah_ref, f2h_ref,
                     g_ref, hid_ref, gt_ref, y_ref):
    step = pl.program_id(1)
    rch = 1024
    nch = LAT_N // rch
    rows_of = lambda i: pl.ds(pl.multiple_of(i * rch, rch), rch)

    @pl.when(step == 0)
    def _():
        def hid_chunk(i, carry):
            r = rows_of(i)
            pre = jnp.dot(z_ref[r, :], w1_ref[0], precision=HIGHEST, preferred_element_type=F32)
            hid_ref[r, :] = jnp.sin(fr_ref[0] * (pre + b1_ref[0]))
            return carry

        lax.fori_loop(0, nch, hid_chunk, 0)

    w2f_hl = _split(w2f_ref[0])
    w2b_hl = _split(w2b_ref[0])

    def raw_chunk(w2_hl, i, nrm):
        r = rows_of(i)
        hh, hl = _split(hid_ref[r, :])
        h = _dot(hh, w2_hl[0]) + _dot(hl, w2_hl[0]) + _dot(hh, w2_hl[1])
        h = h * jnp.exp(-z_ref[r, 0:1] * dl_ref[...])
        row = i * rch + lax.broadcasted_iota(jnp.int32, h.shape, 0)
        h = jnp.where(row == LAT_L, 0.0, h)
        for s in range(rch // NB):
            slab = pl.ds(pl.multiple_of((i * (rch // NB) + s) * X_PITCH, 8), NB)
            gt_ref[slab, :] = h[s * NB:(s + 1) * NB]
        return nrm + jnp.sum(jnp.abs(h), axis=0, keepdims=True)

    nrm = lax.fori_loop(0, nch // 2, functools.partial(raw_chunk, w2f_hl), jnp.zeros((1, LANES), F32))
    nrm = lax.fori_loop(nch // 2, nch, functools.partial(raw_chunk, w2b_hl), nrm)
    hh, hl = _split(hid_ref[0:8, :])
    hb0 = _dot(hh, w2b_hl[0]) + _dot(hl, w2b_hl[0]) + _dot(hh, w2b_hl[1])
    hb0 = hb0 * jnp.exp(-z_ref[0:8, 0:1] * dl_ref[...])
    hb0 = jnp.where(lax.broadcasted_iota(jnp.int32, hb0.shape, 0) == 0, hb0, 0.0)
    gt_ref[0:8, :] = gt_ref[0:8, :] + hb0
    nrm = nrm + jnp.sum(jnp.abs(hb0), axis=0, keepdims=True)

    def norm_slab(a, carry):
        slab = pl.ds(pl.multiple_of(a * X_PITCH, 8), NB)
        gt_ref[slab, :] = gt_ref[slab, :] / nrm
        return carry

    lax.fori_loop(0, NA, norm_slab, 0, unroll=8)
    order = step // (W_HY // LANES)
    gt_ref[0:8, :] = _with_skip(gt_ref[0:8, :], sk_ref[0, pl.ds(order, 1), :])

    def stage_a(b, carry):
        rows = gt_ref[pl.ds(b, NA, stride=X_PITCH), :]
        y_ref[pl.ds(pl.multiple_of(b * Y_PITCH, 8), 2 * NA), :] = _dot1c(ah_ref[b], rows)
        return carry

    lax.fori_loop(0, NB, stage_a, 0, unroll=32)

    def stage_b(j, carry):
        ka = 2 * j
        re, im = _stage_b_rows(ka)
        re1, im1 = _stage_b_rows(ka + 1)
        z = jnp.concatenate([jnp.concatenate([y_ref[re, :], y_ref[im, :]], axis=0),
                             jnp.concatenate([y_ref[re1, :], y_ref[im1, :]], axis=0)], axis=1)
        x = _dot1c(f2h_ref[...], z)
        g0 = pl.multiple_of(ka * 2 * NB, 2 * NB)
        g_ref[0, 0, pl.ds(g0, 2 * NB), :] = x[:, :LANES]
        g_ref[0, 0, pl.ds(g0 + 2 * NB, 2 * NB), :] = x[:, LANES:]
        return carry

    lax.fori_loop(0, NA // 2, stage_b, 0, unroll=8)


def _filt_lat_call(w1, b1, freq, w2, skip):
    z_ext = _filter_positions(LAT_L)
    _, _, (ah, _), (f2h, _), _ = _lat_dft_tables()
    w1p = jnp.pad(w1, ((0, 0), (0, 32 - HY_POS_FEAT), (0, 0)))
    nct = W_HY // LANES
    one = pl.Buffered(1)
    lay = lambda *shape: pl.BlockSpec((1,) + shape, lambda d, s: (d,) + (0,) * len(shape))
    full = lambda a: pl.BlockSpec(a.shape, lambda d, s: (0,) * a.ndim, pipeline_mode=one)
    dl = _hyena_deltas()
    return pl.pallas_call(
        _filt_lat_kernel,
        grid=(DEPTH, 2 * nct),
        in_specs=[full(z_ext), lay(32, HY_FILT_HID), lay(1, HY_FILT_HID), lay(1, HY_FILT_HID),
                  pl.BlockSpec((1, HY_FILT_HID, LANES), lambda d, s: (d, 0, (s // nct) * 2 * nct + s % nct)),
                  pl.BlockSpec((1, HY_FILT_HID, LANES),
                               lambda d, s: (d, 0, (s // nct) * 2 * nct + nct + s % nct)),
                  pl.BlockSpec((1, LANES), lambda d, s: (0, s % nct)),
                  pl.BlockSpec((1, 2, LANES), lambda d, s: (d, 0, s % nct)),
                  full(ah), full(f2h)],
        out_specs=pl.BlockSpec((1, 1, NA * 2 * NB, LANES), lambda d, s: (d, s, 0, 0)),
        out_shape=jax.ShapeDtypeStruct((DEPTH, 2 * nct, NA * 2 * NB, LANES), F32),
        scratch_shapes=[pltpu.VMEM((LAT_N, HY_FILT_HID), F32), pltpu.VMEM((NA * X_PITCH, LANES), F32),
                        pltpu.VMEM((NB * Y_PITCH, LANES), F32)],
        compiler_params=_cparams(("arbitrary", "arbitrary")),
        name="hyena_filter_lat",
    )(z_ext, w1p, b1.reshape(DEPTH, 1, -1), freq.reshape(DEPTH, 1, -1), w2, w2, dl, skip, ah, f2h)


def _short_conv_rows(ref, bi, r0, rows, first, last, w):
    total = ref.shape[1]
    cur = ref[bi, pl.ds(r0, rows), :]
    before = ref[bi, pl.ds(jnp.maximum(r0 - 1, 0), 1), :]
    after = ref[bi, pl.ds(jnp.minimum(r0 + rows, total - 1), 1), :]
    before = jnp.where(first, 0.0, before)
    after = jnp.where(last, 0.0, after)
    rid = lax.broadcasted_iota(jnp.int32, cur.shape, 0)
    prev = jnp.where(rid == 0, before, pltpu.roll(cur, 1, 0))
    nxt = jnp.where(rid == rows - 1, after, pltpu.roll(cur, rows - 1, 0))
    return prev * w[0:1] + cur * w[1:2] + nxt * w[2:3]


def _short_conv_interior(ref, bi, r0, rows, w):
    prev = ref[bi, pl.ds(r0 - 1, rows), :]
    cur = ref[bi, pl.ds(r0, rows), :]
    nxt = ref[bi, pl.ds(r0 + 1, rows), :]
    return prev * w[0:1] + cur * w[1:2] + nxt * w[2:3]


def _cmul(xr, xi, gr, gi):
    return xr * gr - xi * gi, xr * gi + xi * gr


def _hy_ctx_kernel(v_ref, x1_ref, x2_ref, cw_ref, g_ref, fh_ref, ih_ref, o_ref):
    l = v_ref.shape[1]
    n = 2 * l

    def sc(ref, bi, grp):
        w = cw_ref[:, grp * W_HY:(grp + 1) * W_HY]
        return _short_conv_rows(ref, bi, 0, l, True, True, w)

    def conv(zr, zi, order):
        x = _dot1c(fh_ref[...], jnp.concatenate([zr, zi], axis=0))
        pr, pi = _cmul(x[:n], x[n:], g_ref[order, :n], g_ref[order, n:])
        y = _dot1c(ih_ref[...], jnp.concatenate([pr, pi], axis=0))
        return y[:l], y[l:]

    yr, yi = conv(sc(v_ref, 0, 0), sc(v_ref, 1, 0), 0)
    yr, yi = conv(sc(x1_ref, 0, 1) * yr, sc(x1_ref, 1, 1) * yi, 1)
    o_ref[0] = sc(x2_ref, 0, 2) * yr
    o_ref[1] = sc(x2_ref, 1, 2) * yi


def _hy_ctx_call(u, conv_w, g_spec, layer):
    b, l, _ = u.shape
    (fh, _), (ih, _), _ = _ctx_dft_tables(l)
    grp = lambda g: pl.BlockSpec((2, l, W_HY), lambda i: (i, 0, g))
    full = lambda a: pl.BlockSpec(a.shape, lambda i: (0,) * a.ndim)
    return pl.pallas_call(
        _hy_ctx_kernel,
        grid=(b // 2,),
        in_specs=[grp(0), grp(1), grp(2), full(conv_w),
                  pl.BlockSpec((None,) + g_spec.shape[1:], lambda i: (layer, 0, 0, 0)), full(fh), full(ih)],
        out_specs=pl.BlockSpec((2, l, W_HY), lambda i: (i, 0, 0)),
        out_shape=jax.ShapeDtypeStruct((b, l, W_HY), F32),
        compiler_params=_cparams(("parallel",)),
        name="hyena_ctx",
    )(u, u, u, conv_w, g_spec, fh, ih)


def _hy_lat_kernel(z_ref, m_ref, cw_ref, g_ref, af_ref, ai_ref, f2_ref, f3_ref, o_ref, xr_scr, xi_scr,
                   y_scr, *, conv_in):
    x_scr = (xr_scr, xi_scr)
    na_half = NA // 2
    w_in = cw_ref[0] if conv_in else None
    w_mul = cw_ref[1]

    def conv_slab(ref, bi, a, w):
        if isinstance(a, int):
            return _short_conv_rows(ref, bi, a * NB, NB, a == 0, a == na_half - 1, w)
        return _short_conv_interior(ref, bi, pl.multiple_of(a * NB, NB), NB, w)

    def edges_then_interior(body):
        body(0, 0)
        body(na_half - 1, 0)
        lax.fori_loop(1, na_half - 1, body, 0, unroll=2)

    def load_in(a, carry):
        for bi in range(2):
            if conv_in:
                val = conv_slab(z_ref, bi, a, w_in)
            else:
                val = z_ref[bi, pl.ds(pl.multiple_of(a * NB, NB), NB), :]
            x_scr[bi][pl.ds(pl.multiple_of(a * X_PITCH, 8), NB), :] = val
        return carry

    edges_then_interior(load_in)

    def stage_a(b, carry):
        zr = xr_scr[pl.ds(b, na_half, stride=X_PITCH), :]
        zi = xi_scr[pl.ds(b, na_half, stride=X_PITCH), :]
        rhs = jnp.concatenate([jnp.concatenate([zr, zi], axis=0), jnp.concatenate([zi, -zr], axis=0)],
                              axis=1)
        y = _dot1c(af_ref[b], rhs)
        r0 = pl.multiple_of(b * Y_PITCH, 8)
        y_scr[pl.ds(r0, NA), :] = y[:, :LANES]
        y_scr[pl.ds(r0 + NA, NA), :] = y[:, LANES:]
        return carry

    lax.fori_loop(0, NB, stage_a, 0, unroll=64)

    def spectrum_product(j):
        ka = 2 * j
        re, im = _stage_b_rows(ka)
        re1, im1 = _stage_b_rows(ka + 1)
        z = jnp.concatenate([jnp.concatenate([y_scr[re, :], y_scr[im, :]], axis=0),
                             jnp.concatenate([y_scr[re1, :], y_scr[im1, :]], axis=0)], axis=1)
        x = _dot1c(f2_ref[...], z)
        g0 = pl.multiple_of(ka * 2 * NB, 2 * NB)
        gr = jnp.concatenate([g_ref[pl.ds(g0, NB), :], g_ref[pl.ds(g0 + 2 * NB, NB), :]], axis=1)
        gi = jnp.concatenate([g_ref[pl.ds(g0 + NB, NB), :], g_ref[pl.ds(g0 + 3 * NB, NB), :]], axis=1)
        pr, pi = _cmul(x[:NB], x[NB:], gr, gi)
        return jnp.concatenate([pr, pi], axis=0).astype(BF16)

    def inverse_b(j, prod):
        ka = 2 * j
        re, im = _stage_b_rows(ka)
        re1, im1 = _stage_b_rows(ka + 1)
        u = _dot(f3_ref[...], prod)
        y_scr[re, :] = u[:NB, :LANES]
        y_scr[im, :] = u[NB:, :LANES]
        y_scr[re1, :] = u[:NB, LANES:]
        y_scr[im1, :] = u[NB:, LANES:]

    def stage_b(j, prod):
        nxt = spectrum_product(j + 1)
        inverse_b(j, prod)
        return nxt

    last = lax.fori_loop(0, NA // 2 - 1, stage_b, spectrum_product(0), unroll=10)
    inverse_b(NA // 2 - 1, last)

    def stage_c(b, carry):
        r0 = pl.multiple_of(b * Y_PITCH, 8)
        ur = y_scr[pl.ds(r0, NA), :]
        ui = y_scr[pl.ds(r0 + NA, NA), :]
        rhs = jnp.concatenate([jnp.concatenate([ur, -ui], axis=0), jnp.concatenate([ui, ur], axis=0)],
                              axis=1)
        y = _dot1c(ai_ref[b], rhs)
        xr_scr[pl.ds(b, na_half, stride=X_PITCH), :] = y[:, :LANES]
        xi_scr[pl.ds(b, na_half, stride=X_PITCH), :] = y[:, LANES:]
        return carry

    lax.fori_loop(0, NB, stage_c, 0, unroll=64)

    def store_out(a, carry):
        for bi in range(2):
            mul = conv_slab(m_ref, bi, a, w_mul)
            o_ref[bi, pl.ds(pl.multiple_of(a * NB, NB), NB), :] = (
                x_scr[bi][pl.ds(pl.multiple_of(a * X_PITCH, 8), NB), :] * mul)
        return carry

    edges_then_interior(store_out)


def _hy_lat_call(src, src_col, u, mul_col, conv_w2, g_spec, layer, order, *, conv_in):
    b, l, _ = u.shape
    nct = W_HY // LANES
    (af, _), (ai, _), _, (f2, _), (f3, _) = _lat_dft_tables()
    one = pl.Buffered(1)
    blk = lambda col: pl.BlockSpec((2, l, LANES), lambda c, p: (p, 0, col + c))
    const = lambda a: pl.BlockSpec(a.shape, lambda c, p: (0,) * a.ndim, pipeline_mode=one)
    return pl.pallas_call(
        functools.partial(_hy_lat_kernel, conv_in=conv_in),
        grid=(nct, b // 2),
        in_specs=[blk(src_col), blk(mul_col),
                  pl.BlockSpec((2, 3, LANES), lambda c, p: (0, 0, c)),
                  pl.BlockSpec((None, None, NA * 2 * NB, LANES), lambda c, p: (layer, order * nct + c, 0, 0)),
                  const(af), const(ai), const(f2), const(f3)],
        out_specs=pl.BlockSpec((2, l, LANES), lambda c, p: (p, 0, c)),
        out_shape=jax.ShapeDtypeStruct((b, l, W_HY), F32),
        scratch_shapes=[pltpu.VMEM(((NA // 2) * X_PITCH, LANES), F32),
                        pltpu.VMEM(((NA // 2) * X_PITCH, LANES), F32),
                        pltpu.VMEM((NB * Y_PITCH, LANES), F32)],
        compiler_params=_cparams(("arbitrary", "arbitrary")),
        name="hyena_lat_conv_in" if conv_in else "hyena_lat",
    )(src, u, conv_w2, g_spec, af, ai, f2, f3)


def _rope_tables(l):
    f32 = np.float32
    rows = l // GRID_W
    row = np.repeat(np.arange(rows), GRID_W).astype(f32)
    col = np.tile(np.arange(GRID_W), rows).astype(f32)
    quarter = HEAD_DIM // 4
    inv = np.power(f32(ROPE_BASE), -np.arange(quarter, dtype=f32) / f32(quarter)).astype(f32)
    ang = np.concatenate([row[:, None] * inv, col[:, None] * inv], axis=-1).astype(f32)
    cos, sin = np.cos(ang), np.sin(ang)
    q = quarter
    cos_h = np.concatenate([cos[:, :q], cos[:, :q], cos[:, q:], cos[:, q:]], axis=-1)
    sin_h = np.concatenate([-sin[:, :q], sin[:, :q], -sin[:, q:], sin[:, q:]], axis=-1)
    return jnp.asarray(np.tile(cos_h, (1, 2))), jnp.asarray(np.tile(sin_h, (1, 2)))


def kernel(x_prompt, x_sample, c, cache_k, cache_v, state_ret, c_ctx, norm_w, w_mod, b_mod, w_in, hy_conv,
           hy_filt_w1, hy_filt_b1, hy_filt_freq, hy_filt_w2, hy_skip, attn_sink, ret_theta, ret_gn,
           w_branch_a, w_branch_b, w_branch_c, w_merge, b_merge, w_out, final_norm_w):
    d = D_MODEL
    bc, lc, _ = x_prompt.shape
    bl, ll, _ = x_sample.shape
    assert ll == LAT_L and bc % 2 == 0 and bl % 2 == 0
    past = cache_k.shape[2]

    cond = jnp.zeros((16, d), F32).at[:bl].set(c).at[bl].set(c_ctx)
    mod = _mod_call(cond, w_mod, b_mod)

    g_ctx = _filt_ctx_call(lc, hy_filt_w1, hy_filt_b1, hy_filt_freq, hy_filt_w2, hy_skip)
    g_lat = _filt_lat_call(hy_filt_w1, hy_filt_b1, hy_filt_freq, hy_filt_w2, hy_skip)

    cos_t, sin_t = _rope_tables(ll)
    w_in_b = w_in.astype(BF16)
    wm_b = w_merge.astype(BF16)
    wa_b = w_branch_a.astype(BF16)
    wb_b = w_branch_b.astype(BF16)
    wc_b = w_branch_c.astype(BF16)
    wo_b = w_out.astype(BF16)
    fnw = final_norm_w.reshape(1, d)
    k_ctx = cache_k.reshape(bl, DEPTH, past, W_KV)
    v_ctx = cache_v.reshape(bl, DEPTH, past, W_KV)
    hy_cols = COL_HY // LANES
    nct = W_HY // LANES

    xp, xs = x_prompt, x_sample
    ks_out, vs_out, ss_out = [], [], []
    for l in range(DEPTH):
        final = l == DEPTH - 1
        nw = norm_w[l].reshape(1, d)
        bm = b_merge[l].reshape(1, -1)
        gn = ret_gn[l].reshape(1, W_RET)
        shift, scale, gate = (mod[l, :, i * d:(i + 1) * d][:, None, :] for i in range(3))
        conv_w = hy_conv[l]
        cw = lambda g: conv_w[:, g * W_HY:(g + 1) * W_HY]

        sl = slice(bl, bl + 1)
        u = _in_call(xp, shift[sl], scale[sl], nw, w_in_b[l], cos_t, sin_t, rope=False)
        ya = _hy_ctx_call(u, conv_w, g_ctx, l)
        yb = _attn_ctx_call(u, attn_sink[l])
        yc, sfin = _ret_call(u, ret_theta[l], None)
        res = _out_call(xp, shift[sl], scale[sl], gate[sl], nw, ya, u, yb, yc, gn, wm_b[l], bm, wa_b[l], wb_b[l],
                        wc_b[l], wo_b[l], fnw, final=final)
        xp = res[0]
        if final:
            y_prompt = res[1]
        ks_out.append(u[:, :, COL_KA:COL_KA + W_KV])
        vs_out.append(u[:, :, COL_VA:COL_VA + W_KV])
        ss_out.append(sfin)

        sl = slice(0, bl)
        u = _in_call(xs, shift[sl], scale[sl], nw, w_in_b[l], cos_t, sin_t, rope=True)
        z1 = _hy_lat_call(u, hy_cols, u, hy_cols + nct, jnp.stack([cw(0), cw(1)]), g_lat, l, 0, conv_in=True)
        ya = _hy_lat_call(z1, 0, u, hy_cols + 2 * nct, jnp.stack([cw(2), cw(2)]), g_lat, l, 1, conv_in=False)
        yb = _attn_lat_call(u, k_ctx, v_ctx, attn_sink[l], l)
        yc, _ = _ret_call(u, ret_theta[l], state_ret, l)
        res = _out_call(xs, shift[sl], scale[sl], gate[sl], nw, ya, u, yb, yc, gn, wm_b[l], bm, wa_b[l], wb_b[l],
                        wc_b[l], wo_b[l], fnw, final=final)
        xs = res[0]
        if final:
            y_sample = res[1]

    kv_shape = (bc, DEPTH, lc, ATT_KV_HEADS, HEAD_DIM)
    new_cache_k = jnp.stack(ks_out, axis=1).reshape(kv_shape)
    new_cache_v = jnp.stack(vs_out, axis=1).reshape(kv_shape)
    new_state_ret = jnp.stack(ss_out, axis=1)
    return (y_prompt, y_sample, new_cache_k, new_cache_v, new_state_ret)
```

```python
import functools
import math

import numpy as np
import jax
import jax.numpy as jnp
from jax import lax
from jax.experimental import pallas as pl
from jax.experimental.pallas import tpu as pltpu

F32 = jnp.float32
BF16 = jnp.bfloat16
HIGHEST = lax.Precision.HIGHEST

D_MODEL = 1024
DEPTH = 4
GRID_W = 64
W_HY = 512
HY_BANDS = 8
HY_POS_FEAT = 1 + 2 * HY_BANDS
HY_FILT_HID = 64
HY_POS_PAD = 32
COND_ROWS = 16
HY_DECAY_TARGET = 1e-2
HY_FAST_PCT = 0.3
HY_SLOW_PCT = 1.5
ATT_HEADS = 8
ATT_KV_HEADS = 2
ATT_GROUP = ATT_HEADS // ATT_KV_HEADS
HEAD_DIM = 64
W_ATT = ATT_HEADS * HEAD_DIM
W_KV = ATT_KV_HEADS * HEAD_DIM
ATT_BLOCK = 128
ATT_QB = 8
RET_HEADS = 8
RET_DIM = 64
W_RET = RET_HEADS * RET_DIM
RET_CHUNK = 128
ROPE_BASE = 10000.0
EPS = 1e-6
NEG = -1e30

LANES = 128
MXU_ROWS = 512
IN_PARTS = 2
VMEM_LIMIT = 58 * 1024 * 1024

IN_DIM = 5376
COL_HY = 0
COL_GH = 1536
COL_QA = 2048
COL_GA = 2560
COL_QR = 3072
COL_KR = 3584
COL_VR = 4096
COL_GR = 4608
COL_KA = 5120
COL_VA = 5248
_W_COL = {COL_HY: 0, COL_HY + W_HY: 512, COL_HY + 2 * W_HY: 1024, COL_GH: 1536, COL_QA: 2048, COL_KA: 2560,
          COL_VA: 2688, COL_GA: 2816, COL_QR: 3328, COL_KR: 3840, COL_VR: 4352, COL_GR: 4864}

LAT_L = 4096
LAT_N = 2 * LAT_L
NA = 64
NB = 128
Y_PITCH = 136
X_PITCH = 136


def _cparams(sem):
    return pltpu.CompilerParams(dimension_semantics=sem, vmem_limit_bytes=VMEM_LIMIT)


def _split_np(a):
    a32 = np.asarray(a, np.float32)
    hi = a32.astype(BF16)
    lo = (a32 - hi.astype(np.float32)).astype(BF16)
    return jnp.asarray(hi), jnp.asarray(lo)


def _split(x):
    hi = x.astype(BF16)
    lo = (x - hi.astype(F32)).astype(BF16)
    return hi, lo


def _dot(a, b):
    return jnp.dot(a, b, preferred_element_type=F32)


def _dot3c(chi, clo, x):
    xh, xl = _split(x)
    return _dot(chi, xh) + _dot(clo, xh) + _dot(chi, xl)


def _dot1c(chi, x):
    return _dot(chi, x.astype(BF16))


def _silu(x):
    return x * jax.nn.sigmoid(x)


def _mod_kernel(c_ref, w_ref, b_ref, o_ref):
    s = _silu(c_ref[...])
    o_ref[0] = jnp.dot(s, w_ref[0], precision=HIGHEST, preferred_element_type=F32) + b_ref[0]


def _mod_call(cond, w_mod, b_mod):
    rows, d = cond.shape
    n = w_mod.shape[-1]
    tn = 1024
    return pl.pallas_call(
        _mod_kernel,
        grid=(DEPTH, n // tn),
        in_specs=[pl.BlockSpec((rows, d), lambda l, j: (0, 0)),
                  pl.BlockSpec((1, d, tn), lambda l, j: (l, 0, j)),
                  pl.BlockSpec((1, 1, tn), lambda l, j: (l, 0, j))],
        out_specs=pl.BlockSpec((1, rows, tn), lambda l, j: (l, 0, j)),
        out_shape=jax.ShapeDtypeStruct((DEPTH, rows, n), F32),
        compiler_params=_cparams(("arbitrary", "arbitrary")),
        name="adaln_mod",
    )(cond, w_mod, b_mod.reshape(DEPTH, 1, n))


def _modulated(x, nw, scale, shift):
    ms = jnp.mean(x * x, axis=-1, keepdims=True)
    h = x * lax.rsqrt(ms + EPS) * nw
    return h * (1.0 + scale) + shift


def _rope128(x, cos, sin_signed, first_half):
    up = pltpu.roll(x, LANES - 16, 1)
    dn = pltpu.roll(x, 16, 1)
    return x * cos + jnp.where(first_half, up, dn) * sin_signed


def _rows(ref):
    bt, tm, w = ref.shape
    return ref[...].reshape(bt * tm, w)


def _put(ref, c0, val):
    bt, tm, _ = ref.shape
    ref[:, :, c0:c0 + val.shape[1]] = val.reshape(bt, tm, val.shape[1])


def _in_kernel(x_ref, shift_ref, scale_ref, nw_ref, w_ref, cos_ref, sin_ref, o_ref, *, rope):
    x = _rows(x_ref)
    rows = x.shape[0]
    part = rows // IN_PARTS
    hbs = [_modulated(x[r:r + part], nw_ref[...], scale_ref[0], shift_ref[0]).astype(BF16)
           for r in range(0, rows, part)]
    if rope:
        cos = cos_ref[...]
        sin = sin_ref[...]
        lane = lax.broadcasted_iota(jnp.int32, (rows, LANES), 1)
        first_half = (lane % 32) < 16

    def seg(c0, width):
        w0 = _W_COL[c0]
        return jnp.concatenate([_dot(hb, w_ref[:, w0:w0 + width]) for hb in hbs], axis=0)

    def put_rope(c0, val, mul):
        for i in range(val.shape[1] // LANES):
            piece = val[:, i * LANES:(i + 1) * LANES]
            if rope:
                piece = _rope128(piece, cos, sin, first_half)
            if mul is not None:
                piece = piece * mul
            _put(o_ref, c0 + i * LANES, piece)

    for g in range(3):
        _put(o_ref, COL_HY + g * W_HY, seg(COL_HY + g * W_HY, W_HY))
    _put(o_ref, COL_GH, _silu(seg(COL_GH, W_HY)))
    put_rope(COL_QA, seg(COL_QA, W_ATT), None)
    _put(o_ref, COL_GA, _silu(seg(COL_GA, W_ATT)))
    put_rope(COL_QR, seg(COL_QR, W_RET), None)
    put_rope(COL_KR, seg(COL_KR, W_RET), RET_DIM ** -0.5)
    _put(o_ref, COL_VR, seg(COL_VR, W_RET))
    _put(o_ref, COL_GR, _silu(seg(COL_GR, W_RET)))
    put_rope(COL_KA, seg(COL_KA, W_KV), None)
    _put(o_ref, COL_VA, seg(COL_VA, W_KV))


def _token_tiling(b, l, per_batch):
    if l >= MXU_ROWS:
        return 1, MXU_ROWS
    bt = 1 if per_batch else min(b, MXU_ROWS // l)
    return bt, l


def _in_call(x, shift, scale, nw, w, cos_t, sin_t, *, rope):
    b, l, d = x.shape
    per_batch = shift.shape[0] > 1
    bt, tm = _token_tiling(b, l, per_batch)
    assert not rope or bt == 1
    mod_map = (lambda i, j: (i, 0, 0)) if per_batch else (lambda i, j: (0, 0, 0))
    return pl.pallas_call(
        functools.partial(_in_kernel, rope=rope),
        grid=(b // bt, l // tm),
        in_specs=[pl.BlockSpec((bt, tm, d), lambda i, j: (i, j, 0)),
                  pl.BlockSpec((1, 1, d), mod_map),
                  pl.BlockSpec((1, 1, d), mod_map),
                  pl.BlockSpec((1, d), lambda i, j: (0, 0)),
                  pl.BlockSpec((d, IN_DIM), lambda i, j: (0, 0), pipeline_mode=pl.Buffered(1)),
                  pl.BlockSpec((tm, LANES), lambda i, j: (j, 0)),
                  pl.BlockSpec((tm, LANES), lambda i, j: (j, 0))],
        out_specs=pl.BlockSpec((bt, tm, IN_DIM), lambda i, j: (i, j, 0)),
        out_shape=jax.ShapeDtypeStruct((b, l, IN_DIM), F32),
        compiler_params=_cparams(("parallel", "parallel")),
        name="in_proj_rope" if rope else "in_proj",
    )(x, shift, scale, nw, w, cos_t, sin_t)


def _retention_post(o, gn, gate):
    lane = lax.broadcasted_iota(jnp.int32, (1, LANES), 1)
    lo_head = lane < RET_DIM
    outs = []
    for t in range(W_RET // LANES):
        sl = slice(t * LANES, (t + 1) * LANES)
        ot = o[:, sl]
        o2 = ot * ot
        s_lo = jnp.sum(jnp.where(lo_head, o2, 0.0), axis=-1, keepdims=True)
        s_hi = jnp.sum(jnp.where(lo_head, 0.0, o2), axis=-1, keepdims=True)
        ms = jnp.where(lo_head, s_lo, s_hi) * (1.0 / RET_DIM)
        outs.append(ot * lax.rsqrt(ms + EPS) * gn[:, sl] * gate[:, sl])
    return jnp.concatenate(outs, axis=1)


def _out_kernel(x_ref, shift_ref, scale_ref, gate_ref, nw_ref, ya_ref, gh_ref, yb_ref, yc_ref, gr_ref, gn_ref,
                wm_ref, bm_ref, wa_ref, wb_ref, wc_ref, wo_ref, fnw_ref, *out_refs, final):
    x = _rows(x_ref)
    d = x.shape[1]
    hb = _modulated(x, nw_ref[...], scale_ref[0], shift_ref[0]).astype(BF16)
    branches = (_rows(ya_ref) * _rows(gh_ref), _rows(yb_ref),
                _retention_post(_rows(yc_ref), gn_ref[...], _rows(gr_ref)))
    merged = None
    for i, (y, w_ref) in enumerate(zip(branches, (wa_ref, wb_ref, wc_ref))):
        g = jax.nn.sigmoid(_dot(hb, wm_ref[:, i * d:(i + 1) * d]) + bm_ref[:, i * d:(i + 1) * d])
        term = g * _dot(y.astype(BF16), w_ref[...])
        merged = term if merged is None else merged + term
    out = _dot(merged.astype(BF16), wo_ref[...])
    xn = x + gate_ref[0] * out
    _put(out_refs[0], 0, xn)
    if final:
        ms = jnp.mean(xn * xn, axis=-1, keepdims=True)
        _put(out_refs[1], 0, xn * lax.rsqrt(ms + EPS) * fnw_ref[...])


def _out_call(x, shift, scale, gate, nw, ya, u, yb, yc, gn, wm, bm, wa, wb, wc, wo, fnw, *, final):
    b, l, d = x.shape
    per_batch = shift.shape[0] > 1
    bt, tm = _token_tiling(b, l, per_batch)
    mod_map = (lambda i, j: (i, 0, 0)) if per_batch else (lambda i, j: (0, 0, 0))
    tok = lambda w: pl.BlockSpec((bt, tm, w), lambda i, j: (i, j, 0))
    full = lambda a: pl.BlockSpec(a.shape, lambda i, j: (0,) * a.ndim, pipeline_mode=pl.Buffered(1))
    n_out = 2 if final else 1
    res = pl.pallas_call(
        functools.partial(_out_kernel, final=final),
        grid=(b // bt, l // tm),
        in_specs=[tok(d), pl.BlockSpec((1, 1, d), mod_map), pl.BlockSpec((1, 1, d), mod_map),
                  pl.BlockSpec((1, 1, d), mod_map), full(nw), tok(W_HY),
                  pl.BlockSpec((bt, tm, W_HY), lambda i, j: (i, j, COL_GH // W_HY)), tok(W_ATT), tok(W_RET),
                  pl.BlockSpec((bt, tm, W_RET), lambda i, j: (i, j, COL_GR // W_RET)), full(gn),
                  full(wm), full(bm), full(wa), full(wb), full(wc), full(wo), full(fnw)],
        out_specs=[tok(d)] * n_out,
        out_shape=[jax.ShapeDtypeStruct((b, l, d), F32)] * n_out,
        compiler_params=_cparams(("parallel", "parallel")),
        name="merge_out_final" if final else "merge_out",
    )(x, shift, scale, gate, nw, ya, u, yb, yc, u, gn, wm, bm, wa, wb, wc, wo, fnw)
    return res


_NT = (((1,), (1,)), ((), ()))


_TN = (((0,), (0,)), ((), ()))
LOG2E = 1.4426950408889634
Q_SCALE = (HEAD_DIM ** -0.5) * LOG2E


def _attn_scores_t(q, kh, kv):
    h0 = kv * ATT_GROUP
    qs = jnp.concatenate([q[:, (h0 + g) * HEAD_DIM:(h0 + g + 1) * HEAD_DIM] for g in range(ATT_GROUP)],
                         axis=0).astype(BF16)
    return lax.dot_general(kh, qs, _NT, preferred_element_type=F32)


def _attn_finish_t(sink_ref, s, vh, g_ref, o_ref, kv, row0=0):
    tk, cols = s.shape
    t = cols // ATT_GROUP
    h0 = kv * ATT_GROUP
    head = lax.broadcasted_iota(jnp.int32, (1, cols), 1) // t
    sink = jnp.full((1, cols), sink_ref[h0], F32)
    for g in range(1, ATT_GROUP):
        sink = jnp.where(head == g, sink_ref[h0 + g], sink)
    sink = sink * LOG2E
    m = jnp.maximum(jnp.max(s, axis=0, keepdims=True), sink)
    p = jnp.exp2(s - m).astype(BF16)
    v_ext = jnp.concatenate([vh, jnp.ones((tk, HEAD_DIM), BF16)], axis=1)
    o_ext = lax.dot_general(v_ext, p, _TN, preferred_element_type=F32)
    denom = o_ext[HEAD_DIM:HEAD_DIM + 1] + jnp.exp2(sink - m)
    o = o_ext[:HEAD_DIM] / denom
    for gp in range(ATT_GROUP // 2):
        pair = jnp.concatenate([o[:, (2 * gp) * t:(2 * gp + 1) * t], o[:, (2 * gp + 1) * t:(2 * gp + 2) * t]],
                               axis=0)
        c0 = (h0 + 2 * gp) * HEAD_DIM
        o_ref[0, row0:row0 + t, c0:c0 + 2 * HEAD_DIM] = pair.T * g_ref[0, row0:row0 + t, c0:c0 + 2 * HEAD_DIM]


def _attn_ctx_kernel(sink_ref, q_ref, k_ref, v_ref, g_ref, o_ref):
    q = q_ref[0] * Q_SCALE
    k = k_ref[0].astype(BF16)
    v = v_ref[0].astype(BF16)
    scores = [_attn_scores_t(q, k[:, kv * HEAD_DIM:(kv + 1) * HEAD_DIM], kv) for kv in range(ATT_KV_HEADS)]
    for kv in range(ATT_KV_HEADS):
        _attn_finish_t(sink_ref, scores[kv], v[:, kv * HEAD_DIM:(kv + 1) * HEAD_DIM], g_ref, o_ref, kv)


def _attn_ctx_call(u, sink):
    b, l, _ = u.shape
    return pl.pallas_call(
        _attn_ctx_kernel,
        grid=(b,),
        in_specs=[pl.BlockSpec(memory_space=pltpu.SMEM),
                  pl.BlockSpec((1, l, W_ATT), lambda i: (i, 0, COL_QA // W_ATT)),
                  pl.BlockSpec((1, l, W_KV), lambda i: (i, 0, COL_KA // W_KV)),
                  pl.BlockSpec((1, l, W_KV), lambda i: (i, 0, COL_VA // W_KV)),
                  pl.BlockSpec((1, l, W_ATT), lambda i: (i, 0, COL_GA // W_ATT))],
        out_specs=pl.BlockSpec((1, l, W_ATT), lambda i: (i, 0, 0)),
        out_shape=jax.ShapeDtypeStruct((b, l, W_ATT), F32),
        compiler_params=_cparams(("parallel",)),
        name="attn_ctx",
    )(sink, u, u, u, u)


def _attn_lat_kernel(sink_ref, q_ref, kp_ref, kc_ref, kn_ref, vp_ref, vc_ref, vn_ref, kx_ref, vx_ref,
                     g_ref, o_ref):
    j = pl.program_id(1)
    last = pl.num_programs(1) - 1
    b = ATT_BLOCK
    nq = ATT_QB
    bf = lambda ref: ref[0].astype(BF16)
    kx, vx = bf(kx_ref), bf(vx_ref)
    kblk = [bf(kp_ref)] + [kc_ref[0, t * b:(t + 1) * b, :].astype(BF16) for t in range(nq)] + [bf(kn_ref)]
    vblk = [bf(vp_ref)] + [vc_ref[0, t * b:(t + 1) * b, :].astype(BF16) for t in range(nq)] + [bf(vn_ref)]
    keys = [jnp.concatenate(kblk[t:t + 3] + [kx], axis=0) for t in range(nq)]
    vals = [jnp.concatenate(vblk[t:t + 3] + [vx], axis=0) for t in range(nq)]
    cols = ATT_GROUP * b
    c = lax.broadcasted_iota(jnp.int32, (b, cols), 0)
    r = lax.broadcasted_iota(jnp.int32, (b, cols), 1) % b
    ok_prev = [(c >= r) & (j > 0)] + [c >= r] * (nq - 1)
    ok_next = [c <= r] * (nq - 1) + [(c <= r) & (j < last)]

    def band(s, t):
        return jnp.concatenate([jnp.where(ok_prev[t], s[:b], NEG), s[b:2 * b],
                                jnp.where(ok_next[t], s[2 * b:3 * b], NEG), s[3 * b:]], axis=0)

    chains = [(t, kv) for t in range(nq) for kv in range(ATT_KV_HEADS)]
    scores = []
    for t, kv in chains:
        q = q_ref[0, t * b:(t + 1) * b, :] * Q_SCALE
        scores.append(band(_attn_scores_t(q, keys[t][:, kv * HEAD_DIM:(kv + 1) * HEAD_DIM], kv), t))
    for (t, kv), s in zip(chains, scores):
        _attn_finish_t(sink_ref, s, vals[t][:, kv * HEAD_DIM:(kv + 1) * HEAD_DIM], g_ref, o_ref, kv, t * b)


def _attn_lat_call(u, kctx, vctx, sink, layer):
    b, l, _ = u.shape
    nb = l // ATT_BLOCK
    past = kctx.shape[2]
    kcol = COL_KA // W_KV
    vcol = COL_VA // W_KV
    nq = ATT_QB
    prev = lambda col: pl.BlockSpec((1, ATT_BLOCK, W_KV), lambda i, j: (i, jnp.maximum(nq * j - 1, 0), col))
    cur = lambda col: pl.BlockSpec((1, nq * ATT_BLOCK, W_KV), lambda i, j: (i, j, col))
    nxt = lambda col: pl.BlockSpec((1, ATT_BLOCK, W_KV),
                                   lambda i, j: (i, jnp.minimum(nq * j + nq, nb - 1), col))
    ctx = pl.BlockSpec((1, None, past, W_KV), lambda i, j: (i, layer, 0, 0))
    return pl.pallas_call(
        _attn_lat_kernel,
        grid=(b, nb // nq),
        in_specs=[pl.BlockSpec(memory_space=pltpu.SMEM),
                  pl.BlockSpec((1, nq * ATT_BLOCK, W_ATT), lambda i, j: (i, j, COL_QA // W_ATT)),
                  prev(kcol), cur(kcol), nxt(kcol), prev(vcol), cur(vcol), nxt(vcol), ctx, ctx,
                  pl.BlockSpec((1, nq * ATT_BLOCK, W_ATT), lambda i, j: (i, j, COL_GA // W_ATT))],
        out_specs=pl.BlockSpec((1, nq * ATT_BLOCK, W_ATT), lambda i, j: (i, j, 0)),
        out_shape=jax.ShapeDtypeStruct((b, l, W_ATT), F32),
        compiler_params=_cparams(("parallel", "parallel")),
        name="attn_lat",
    )(sink, u, u, u, u, u, u, u, kctx, vctx, u)


def _log_sigmoid(x):
    return jnp.minimum(x, 0.0) - jnp.log1p(jnp.exp(-jnp.abs(x)))


_TAB_DMAT = 0
_TAB_QDEC = 2 * RET_CHUNK
_TAB_KDEC = 3 * RET_CHUNK
_TAB_CDEC = 4 * RET_CHUNK
_TAB_ROWS = 4 * RET_CHUNK + 8


def _ret_kernel(q_ref, k_ref, v_ref, thl_ref, thb_ref, s0_ref, o_ref, sfin_ref, tab_ref, *, nc, cpt,
                npairs, has_s0):
    grp = pl.program_id(0)
    c = RET_CHUNK
    lane = lax.broadcasted_iota(jnp.int32, (1, LANES), 1)
    lo_head = lane < RET_DIM
    dd = lax.broadcasted_iota(jnp.int32, (LANES, LANES), 0)
    ee = lax.broadcasted_iota(jnp.int32, (LANES, LANES), 1)
    same_head = (dd < RET_DIM) == (ee < RET_DIM)
    lanes_of = lambda p: slice(p * LANES, (p + 1) * LANES)

    @pl.when(pl.program_id(1) == 0)
    def _():
        rowf = lax.broadcasted_iota(jnp.int32, (c, LANES), 0).astype(F32)
        ii = lax.broadcasted_iota(jnp.int32, (c, c), 0)
        jj = lax.broadcasted_iota(jnp.int32, (c, c), 1)
        for p in range(npairs):
            for d in range(2):
                lg_lane = _log_sigmoid(thl_ref[d, :, lanes_of(p)])
                dist = (ii - jj) if d == 0 else (jj - ii)
                for hh in range(2):
                    head = 2 * (grp * npairs + p) + hh
                    lg_h = _log_sigmoid(thb_ref[d, pl.ds(head, 1), :])
                    dm = jnp.where(dist >= 0, jnp.exp(lg_h * jnp.maximum(dist, 0).astype(F32)), 0.0)
                    tab_ref[p, d, _TAB_DMAT + hh * c:_TAB_DMAT + (hh + 1) * c, :] = dm
                if d == 0:
                    q_dec = jnp.exp(lg_lane * (rowf + 1.0))
                    k_dec = jnp.exp(lg_lane * (c - 1.0 - rowf))
                else:
                    q_dec = jnp.exp(lg_lane * (c - rowf))
                    k_dec = jnp.exp(lg_lane * rowf)
                tab_ref[p, d, _TAB_QDEC:_TAB_QDEC + c, :] = q_dec
                tab_ref[p, d, _TAB_KDEC:_TAB_KDEC + c, :] = k_dec
                tab_ref[p, d, _TAB_CDEC:_TAB_CDEC + 8, :] = jnp.broadcast_to(
                    jnp.exp(lg_lane * float(c)), (8, LANES))

    def first_level(p, d, r0):
        qc = q_ref[0, pl.ds(r0, c), lanes_of(p)]
        kc = k_ref[0, pl.ds(r0, c), lanes_of(p)]
        vcb = v_ref[0, pl.ds(r0, c), lanes_of(p)].astype(BF16)
        qs = jnp.concatenate([jnp.where(lo_head, qc, 0.0), jnp.where(lo_head, 0.0, qc)], axis=0)
        sc = lax.dot_general(qs.astype(BF16), kc.astype(BF16), _NT, preferred_element_type=F32)
        sc = sc * tab_ref[p, d, _TAB_DMAT:_TAB_DMAT + 2 * c, :]
        kd = kc * tab_ref[p, d, _TAB_KDEC:_TAB_KDEC + c, :]
        upd = jnp.where(same_head, _dot(kd.T.astype(BF16), vcb), 0.0)
        return qc, vcb, sc.astype(BF16), upd

    def second_level(p, d, lvl1, s):
        qc, vcb, scb, upd = lvl1
        pv = _dot(scb, vcb)
        qd = qc * tab_ref[p, d, _TAB_QDEC:_TAB_QDEC + c, :]
        o = _dot(qd.astype(BF16), s.astype(BF16)) + jnp.where(lo_head, pv[:c], pv[c:])
        return o, tab_ref[p, d, _TAB_CDEC:_TAB_CDEC + 1, :] * s + upd

    def init_state(p, d):
        if not has_s0:
            return jnp.zeros((LANES, LANES), F32)
        z = jnp.zeros((RET_DIM, RET_DIM), F32)
        return jnp.concatenate([jnp.concatenate([s0_ref[0, d, 2 * p], z], axis=1),
                                jnp.concatenate([z, s0_ref[0, d, 2 * p + 1]], axis=1)], axis=0)

    units = [(p, d) for p in range(npairs) for d in range(2)]

    def scan_body(second, n, states):
        def row0(d, j):
            idx = n * cpt + j
            return pl.multiple_of((idx if d == 0 else nc - 1 - idx) * c, c)

        lvl = {(p, d, j): first_level(p, d, row0(d, j)) for j in range(cpt) for p, d in units}
        states = list(states)
        for j in range(cpt):
            for ui, (p, d) in enumerate(units):
                o, states[ui] = second_level(p, d, lvl[(p, d, j)], states[ui])
                dst = (0, pl.ds(row0(d, j), c), lanes_of(p))
                if second(j):
                    o = o + o_ref[dst]
                o_ref[dst] = o
        return tuple(states)

    assert nc % cpt == 0
    trips = nc // cpt
    states = tuple(init_state(p, d) for p, d in units)
    if trips == 1:
        states = scan_body(lambda j: 2 * j > nc - 1, 0, states)
    else:
        assert trips % 2 == 0
        states = lax.fori_loop(0, trips // 2, functools.partial(scan_body, lambda j: False), states)
        states = lax.fori_loop(trips // 2, trips, functools.partial(scan_body, lambda j: True), states)
    for ui, (p, d) in enumerate(units):
        sfin_ref[0, d, 2 * p] = states[ui][:RET_DIM, :RET_DIM]
        sfin_ref[0, d, 2 * p + 1] = states[ui][RET_DIM:, RET_DIM:]


def _ret_call(u, theta, s0bd, layer=0):
    b, l, _ = u.shape
    nc = l // RET_CHUNK
    has_s0 = s0bd is not None
    cpt = 8 if nc >= 8 else nc
    npairs = max(1, 8 // cpt)
    ngrp = RET_HEADS // 2 // npairs
    w = npairs * LANES
    st_block = (2, 2 * npairs, RET_DIM, RET_DIM)
    if not has_s0:
        s0bd = jnp.zeros((1,) + st_block, F32)
        s0_spec = pl.BlockSpec((1,) + st_block, lambda g, i: (0, 0, 0, 0, 0))
    else:
        s0_spec = pl.BlockSpec((1, None) + st_block, lambda g, i: (i, layer, 0, g, 0, 0))
    th_lane = jnp.repeat(theta, RET_DIM, axis=1).reshape(2, 1, W_RET)
    th_bcast = jnp.broadcast_to(theta[:, :, None], (2, RET_HEADS, LANES))
    col = lambda c0: pl.BlockSpec((1, l, w), lambda g, i: (i, 0, c0 // w + g))
    o, sfin = pl.pallas_call(
        functools.partial(_ret_kernel, nc=nc, cpt=cpt, npairs=npairs, has_s0=has_s0),
        grid=(ngrp, b),
        in_specs=[col(COL_QR), col(COL_KR), col(COL_VR),
                  pl.BlockSpec((2, 1, w), lambda g, i: (0, 0, g)),
                  pl.BlockSpec((2, RET_HEADS, LANES), lambda g, i: (0, 0, 0)),
                  s0_spec],
        out_specs=[pl.BlockSpec((1, l, w), lambda g, i: (i, 0, g)),
                   pl.BlockSpec((1,) + st_block, lambda g, i: (i, 0, g, 0, 0))],
        out_shape=[jax.ShapeDtypeStruct((b, l, W_RET), F32),
                   jax.ShapeDtypeStruct((b, 2, RET_HEADS, RET_DIM, RET_DIM), F32)],
        scratch_shapes=[pltpu.VMEM((npairs, 2, _TAB_ROWS, LANES), F32)],
        compiler_params=_cparams(("arbitrary", "arbitrary")),
        name="retention_s0" if has_s0 else "retention",
    )(u, u, u, th_lane, th_bcast, s0bd)
    return o, sfin


def _filter_positions(l):
    f32 = np.float32
    t = np.linspace(0.0, 1.0, l, dtype=f32)[:, None]
    w = (f32(2.0 * math.pi) * np.arange(l, dtype=f32)[:, None] / f32(l)).astype(f32)
    f = np.linspace(1e-4, HY_BANDS - 1, HY_BANDS, dtype=f32)[None, :]
    z = np.concatenate([t, np.cos(f * w), -np.sin(f * w)], axis=-1).astype(f32)
    z = np.pad(z, ((0, 0), (0, HY_POS_PAD - HY_POS_FEAT)))
    return jnp.asarray(np.concatenate([z, z[:1], z[1:][::-1]], axis=0))


def _hyena_deltas():
    max_decay = math.log(HY_DECAY_TARGET) / HY_FAST_PCT
    min_decay = math.log(HY_DECAY_TARGET) / HY_SLOW_PCT
    return jnp.asarray(np.abs(np.linspace(min_decay, max_decay, W_HY, dtype=np.float32))[None, :])


def _filter_hidden(z_ref, w1_ref, b1_ref, fr_ref):
    pre = jnp.dot(z_ref[...], w1_ref[0], precision=HIGHEST, preferred_element_type=F32) + b1_ref[0]
    return jnp.sin(fr_ref[0] * pre)


def _filter_raw(hid, w2f, w2b, tp, dl, row0, l):
    win = jnp.exp(-tp * dl)
    row = row0 + lax.broadcasted_iota(jnp.int32, win.shape, 0)
    hf = jnp.dot(hid, w2f, precision=HIGHEST, preferred_element_type=F32) * win
    hb = jnp.dot(hid, w2b, precision=HIGHEST, preferred_element_type=F32) * win
    hf = jnp.where(row < l, hf, 0.0)
    hb = jnp.where((row > l) | (row == 0), hb, 0.0)
    return hf + hb, jnp.sum(jnp.abs(hf) + jnp.abs(hb), axis=0, keepdims=True)


def _with_skip(g, skip):
    row = lax.broadcasted_iota(jnp.int32, g.shape, 0)
    return g + jnp.where(row == 0, skip, 0.0)


def _filt_ctx_kernel(z_ref, w1_ref, b1_ref, fr_ref, w2_ref, dl_ref, sk_ref, fh_ref, fl_ref, g_ref):
    hid = _filter_hidden(z_ref, w1_ref, b1_ref, fr_ref)
    tp = z_ref[:, 0:1]
    for o in range(2):
        w2f = w2_ref[0, :, (2 * o) * W_HY:(2 * o + 1) * W_HY]
        w2b = w2_ref[0, :, (2 * o + 1) * W_HY:(2 * o + 2) * W_HY]
        raw, nrm = _filter_raw(hid, w2f, w2b, tp, dl_ref[...], 0, z_ref.shape[0] // 2)
        g = _with_skip(raw / nrm, sk_ref[0, pl.ds(o, 1), :])
        g_ref[0, o] = _dot3c(fh_ref[...], fl_ref[...], g)


def _ctx_dft_tables(l):
    n = 2 * l
    k = np.arange(n)[:, None]
    t = np.arange(l)[None, :]
    ang = 2.0 * np.pi * k * t / n
    c, s = np.cos(ang), np.sin(ang)
    fwd = np.block([[c, s], [-s, c]])
    inv = np.block([[c.T, -s.T], [s.T, c.T]])
    n_all = np.arange(n)[None, :]
    angg = 2.0 * np.pi * k * n_all / n
    filt = np.concatenate([np.cos(angg), -np.sin(angg)], axis=0) / n
    return _split_np(fwd), _split_np(inv), _split_np(filt)


def _filt_ctx_call(l, w1, b1, freq, w2, skip):
    n = 2 * l
    z_ext = _filter_positions(l)
    _, _, (fh, fl) = _ctx_dft_tables(l)
    w1p = jnp.pad(w1, ((0, 0), (0, HY_POS_PAD - HY_POS_FEAT), (0, 0)))
    lay = lambda *shape: pl.BlockSpec((1,) + shape, lambda d: (d,) + (0,) * len(shape))
    full = lambda a: pl.BlockSpec(a.shape, lambda d: (0,) * a.ndim)
    dl = _hyena_deltas()
    return pl.pallas_call(
        _filt_ctx_kernel,
        grid=(DEPTH,),
        in_specs=[full(z_ext), lay(HY_POS_PAD, HY_FILT_HID), lay(1, HY_FILT_HID), lay(1, HY_FILT_HID),
                  lay(HY_FILT_HID, 4 * W_HY), full(dl), lay(2, W_HY), full(fh), full(fl)],
        out_specs=pl.BlockSpec((1, 2, 2 * n, W_HY), lambda d: (d, 0, 0, 0)),
        out_shape=jax.ShapeDtypeStruct((DEPTH, 2, 2 * n, W_HY), F32),
        compiler_params=_cparams(("arbitrary",)),
        name="hyena_filter_ctx",
    )(z_ext, w1p, b1.reshape(DEPTH, 1, -1), freq.reshape(DEPTH, 1, -1), w2, dl, skip, fh, fl)


def _lat_dft_tables():
    ka = np.arange(NA)[:, None]
    b = np.arange(NB)[:, None, None]
    kb = np.arange(NB)[:, None]
    bb = np.arange(NB)[None, :]
    a_half = np.arange(NA // 2)[None, :]
    a_full = np.arange(NA)[None, :]
    phi = 2.0 * np.pi * (ka * a_half / NA + b * ka / LAT_N)
    c, s = np.cos(phi), np.sin(phi)
    a_fwd = np.concatenate([c, s], axis=2)
    ct, st = np.swapaxes(c, 1, 2), np.swapaxes(s, 1, 2)
    a_inv = np.concatenate([ct, st], axis=2)
    phig = 2.0 * np.pi * (ka * a_full / NA + b * ka / LAT_N)
    a_flt = np.concatenate([np.cos(phig), -np.sin(phig)], axis=1) / LAT_N
    ang = 2.0 * np.pi * kb * bb / NB
    c2, s2 = np.cos(ang), np.sin(ang)
    f_fwd = np.block([[c2, s2], [-s2, c2]])
    f_inv = np.block([[c2, -s2], [s2, c2]])
    return (_split_np(a_fwd), _split_np(a_inv), _split_np(a_flt), _split_np(f_fwd), _split_np(f_inv))


def _stage_b_rows(ka):
    re = pl.ds(ka, NB, stride=Y_PITCH)
    im = pl.ds(NA + ka, NB, stride=Y_PITCH)
    return re, im


def _filt_lat_kernel(z_ref, w1_ref, b1_ref, fr_ref, w2f_ref, w2b_ref, dl_ref, sk_ref, ah_ref, f2h_ref,
                     g_ref, hid_ref, gt_ref, y_ref):
    step = pl.program_id(1)
    rch = 1024
    nch = LAT_N // rch
    rows_of = lambda i: pl.ds(pl.multiple_of(i * rch, rch), rch)

    @pl.when(step == 0)
    def _():
        def hid_chunk(i, carry):
            r = rows_of(i)
            pre = jnp.dot(z_ref[r, :], w1_ref[0], precision=HIGHEST, preferred_element_type=F32)
            hid_ref[r, :] = jnp.sin(fr_ref[0] * (pre + b1_ref[0]))
            return carry

        lax.fori_loop(0, nch, hid_chunk, 0)

    w2f_hl = _split(w2f_ref[0])
    w2b_hl = _split(w2b_ref[0])

    def raw_chunk(w2_hl, i, nrm):
        r = rows_of(i)
        hh, hl = _split(hid_ref[r, :])
        h = _dot(hh, w2_hl[0]) + _dot(hl, w2_hl[0]) + _dot(hh, w2_hl[1])
        h = h * jnp.exp(-z_ref[r, 0:1] * dl_ref[...])
        row = i * rch + lax.broadcasted_iota(jnp.int32, h.shape, 0)
        h = jnp.where(row == LAT_L, 0.0, h)
        for s in range(rch // NB):
            slab = pl.ds(pl.multiple_of((i * (rch // NB) + s) * X_PITCH, 8), NB)
            gt_ref[slab, :] = h[s * NB:(s + 1) * NB]
        return nrm + jnp.sum(jnp.abs(h), axis=0, keepdims=True)

    nrm = lax.fori_loop(0, nch // 2, functools.partial(raw_chunk, w2f_hl), jnp.zeros((1, LANES), F32))
    nrm = lax.fori_loop(nch // 2, nch, functools.partial(raw_chunk, w2b_hl), nrm)
    hh, hl = _split(hid_ref[0:8, :])
    hb0 = _dot(hh, w2b_hl[0]) + _dot(hl, w2b_hl[0]) + _dot(hh, w2b_hl[1])
    hb0 = hb0 * jnp.exp(-z_ref[0:8, 0:1] * dl_ref[...])
    hb0 = jnp.where(lax.broadcasted_iota(jnp.int32, hb0.shape, 0) == 0, hb0, 0.0)
    gt_ref[0:8, :] = gt_ref[0:8, :] + hb0
    nrm = nrm + jnp.sum(jnp.abs(hb0), axis=0, keepdims=True)

    def norm_slab(a, carry):
        slab = pl.ds(pl.multiple_of(a * X_PITCH, 8), NB)
        gt_ref[slab, :] = gt_ref[slab, :] / nrm
        return carry

    lax.fori_loop(0, NA, norm_slab, 0, unroll=8)
    order = step // (W_HY // LANES)
    gt_ref[0:8, :] = _with_skip(gt_ref[0:8, :], sk_ref[0, pl.ds(order, 1), :])

    def stage_a(b, carry):
        rows = gt_ref[pl.ds(b, NA, stride=X_PITCH), :]
        y_ref[pl.ds(pl.multiple_of(b * Y_PITCH, 8), 2 * NA), :] = _dot1c(ah_ref[b], rows)
        return carry

    lax.fori_loop(0, NB, stage_a, 0, unroll=32)

    def stage_b(j, carry):
        ka = 2 * j
        re, im = _stage_b_rows(ka)
        re1, im1 = _stage_b_rows(ka + 1)
        z = jnp.concatenate([jnp.concatenate([y_ref[re, :], y_ref[im, :]], axis=0),
                             jnp.concatenate([y_ref[re1, :], y_ref[im1, :]], axis=0)], axis=1)
        x = _dot1c(f2h_ref[...], z)
        g0 = pl.multiple_of(ka * 2 * NB, 2 * NB)
        g_ref[0, 0, pl.ds(g0, 2 * NB), :] = x[:, :LANES]
        g_ref[0, 0, pl.ds(g0 + 2 * NB, 2 * NB), :] = x[:, LANES:]
        return carry

    lax.fori_loop(0, NA // 2, stage_b, 0, unroll=8)


def _filt_lat_call(w1, b1, freq, w2, skip):
    z_ext = _filter_positions(LAT_L)
    _, _, (ah, _), (f2h, _), _ = _lat_dft_tables()
    w1p = jnp.pad(w1, ((0, 0), (0, HY_POS_PAD - HY_POS_FEAT), (0, 0)))
    nct = W_HY // LANES
    one = pl.Buffered(1)
    lay = lambda *shape: pl.BlockSpec((1,) + shape, lambda d, s: (d,) + (0,) * len(shape))
    full = lambda a: pl.BlockSpec(a.shape, lambda d, s: (0,) * a.ndim, pipeline_mode=one)
    dl = _hyena_deltas()
    return pl.pallas_call(
        _filt_lat_kernel,
        grid=(DEPTH, 2 * nct),
        in_specs=[full(z_ext), lay(HY_POS_PAD, HY_FILT_HID), lay(1, HY_FILT_HID), lay(1, HY_FILT_HID),
                  pl.BlockSpec((1, HY_FILT_HID, LANES), lambda d, s: (d, 0, (s // nct) * 2 * nct + s % nct)),
                  pl.BlockSpec((1, HY_FILT_HID, LANES),
                               lambda d, s: (d, 0, (s // nct) * 2 * nct + nct + s % nct)),
                  pl.BlockSpec((1, LANES), lambda d, s: (0, s % nct)),
                  pl.BlockSpec((1, 2, LANES), lambda d, s: (d, 0, s % nct)),
                  full(ah), full(f2h)],
        out_specs=pl.BlockSpec((1, 1, NA * 2 * NB, LANES), lambda d, s: (d, s, 0, 0)),
        out_shape=jax.ShapeDtypeStruct((DEPTH, 2 * nct, NA * 2 * NB, LANES), F32),
        scratch_shapes=[pltpu.VMEM((LAT_N, HY_FILT_HID), F32), pltpu.VMEM((NA * X_PITCH, LANES), F32),
                        pltpu.VMEM((NB * Y_PITCH, LANES), F32)],
        compiler_params=_cparams(("arbitrary", "arbitrary")),
        name="hyena_filter_lat",
    )(z_ext, w1p, b1.reshape(DEPTH, 1, -1), freq.reshape(DEPTH, 1, -1), w2, w2, dl, skip, ah, f2h)


def _short_conv_rows(ref, bi, r0, rows, first, last, w):
    total = ref.shape[1]
    cur = ref[bi, pl.ds(r0, rows), :]
    before = ref[bi, pl.ds(jnp.maximum(r0 - 1, 0), 1), :]
    after = ref[bi, pl.ds(jnp.minimum(r0 + rows, total - 1), 1), :]
    before = jnp.where(first, 0.0, before)
    after = jnp.where(last, 0.0, after)
    rid = lax.broadcasted_iota(jnp.int32, cur.shape, 0)
    prev = jnp.where(rid == 0, before, pltpu.roll(cur, 1, 0))
    nxt = jnp.where(rid == rows - 1, after, pltpu.roll(cur, rows - 1, 0))
    return prev * w[0:1] + cur * w[1:2] + nxt * w[2:3]


def _short_conv_interior(ref, bi, r0, rows, w):
    prev = ref[bi, pl.ds(r0 - 1, rows), :]
    cur = ref[bi, pl.ds(r0, rows), :]
    nxt = ref[bi, pl.ds(r0 + 1, rows), :]
    return prev * w[0:1] + cur * w[1:2] + nxt * w[2:3]


def _cmul(xr, xi, gr, gi):
    return xr * gr - xi * gi, xr * gi + xi * gr


def _hy_ctx_kernel(v_ref, x1_ref, x2_ref, cw_ref, g_ref, fh_ref, ih_ref, o_ref):
    l = v_ref.shape[1]
    n = 2 * l

    def sc(ref, bi, grp):
        w = cw_ref[:, grp * W_HY:(grp + 1) * W_HY]
        return _short_conv_rows(ref, bi, 0, l, True, True, w)

    def conv(zr, zi, order):
        x = _dot1c(fh_ref[...], jnp.concatenate([zr, zi], axis=0))
        pr, pi = _cmul(x[:n], x[n:], g_ref[order, :n], g_ref[order, n:])
        y = _dot1c(ih_ref[...], jnp.concatenate([pr, pi], axis=0))
        return y[:l], y[l:]

    yr, yi = conv(sc(v_ref, 0, 0), sc(v_ref, 1, 0), 0)
    yr, yi = conv(sc(x1_ref, 0, 1) * yr, sc(x1_ref, 1, 1) * yi, 1)
    o_ref[0] = sc(x2_ref, 0, 2) * yr
    o_ref[1] = sc(x2_ref, 1, 2) * yi


def _hy_ctx_call(u, conv_w, g_spec, layer):
    b, l, _ = u.shape
    (fh, _), (ih, _), _ = _ctx_dft_tables(l)
    grp = lambda g: pl.BlockSpec((2, l, W_HY), lambda i: (i, 0, g))
    full = lambda a: pl.BlockSpec(a.shape, lambda i: (0,) * a.ndim)
    return pl.pallas_call(
        _hy_ctx_kernel,
        grid=(b // 2,),
        in_specs=[grp(0), grp(1), grp(2), full(conv_w),
                  pl.BlockSpec((None,) + g_spec.shape[1:], lambda i: (layer, 0, 0, 0)), full(fh), full(ih)],
        out_specs=pl.BlockSpec((2, l, W_HY), lambda i: (i, 0, 0)),
        out_shape=jax.ShapeDtypeStruct((b, l, W_HY), F32),
        compiler_params=_cparams(("parallel",)),
        name="hyena_ctx",
    )(u, u, u, conv_w, g_spec, fh, ih)


def _hy_lat_kernel(z_ref, m_ref, cw_ref, g_ref, af_ref, ai_ref, f2_ref, f3_ref, o_ref, xr_scr, xi_scr,
                   y_scr, *, conv_in):
    x_scr = (xr_scr, xi_scr)
    na_half = NA // 2
    w_in = cw_ref[0] if conv_in else None
    w_mul = cw_ref[1]

    def conv_slab(ref, bi, a, w):
        if isinstance(a, int):
            return _short_conv_rows(ref, bi, a * NB, NB, a == 0, a == na_half - 1, w)
        return _short_conv_interior(ref, bi, pl.multiple_of(a * NB, NB), NB, w)

    def edges_then_interior(body):
        body(0, 0)
        body(na_half - 1, 0)
        lax.fori_loop(1, na_half - 1, body, 0, unroll=2)

    def load_in(a, carry):
        for bi in range(2):
            if conv_in:
                val = conv_slab(z_ref, bi, a, w_in)
            else:
                val = z_ref[bi, pl.ds(pl.multiple_of(a * NB, NB), NB), :]
            x_scr[bi][pl.ds(pl.multiple_of(a * X_PITCH, 8), NB), :] = val
        return carry

    edges_then_interior(load_in)

    def stage_a(b, carry):
        zr = xr_scr[pl.ds(b, na_half, stride=X_PITCH), :]
        zi = xi_scr[pl.ds(b, na_half, stride=X_PITCH), :]
        rhs = jnp.concatenate([jnp.concatenate([zr, zi], axis=0), jnp.concatenate([zi, -zr], axis=0)],
                              axis=1)
        y = _dot1c(af_ref[b], rhs)
        r0 = pl.multiple_of(b * Y_PITCH, 8)
        y_scr[pl.ds(r0, NA), :] = y[:, :LANES]
        y_scr[pl.ds(r0 + NA, NA), :] = y[:, LANES:]
        return carry

    lax.fori_loop(0, NB, stage_a, 0, unroll=64)

    def spectrum_product(j):
        ka = 2 * j
        re, im = _stage_b_rows(ka)
        re1, im1 = _stage_b_rows(ka + 1)
        z = jnp.concatenate([jnp.concatenate([y_scr[re, :], y_scr[im, :]], axis=0),
                             jnp.concatenate([y_scr[re1, :], y_scr[im1, :]], axis=0)], axis=1)
        x = _dot1c(f2_ref[...], z)
        g0 = pl.multiple_of(ka * 2 * NB, 2 * NB)
        gr = jnp.concatenate([g_ref[pl.ds(g0, NB), :], g_ref[pl.ds(g0 + 2 * NB, NB), :]], axis=1)
        gi = jnp.concatenate([g_ref[pl.ds(g0 + NB, NB), :], g_ref[pl.ds(g0 + 3 * NB, NB), :]], axis=1)
        pr, pi = _cmul(x[:NB], x[NB:], gr, gi)
        return jnp.concatenate([pr, pi], axis=0).astype(BF16)

    def inverse_b(j, prod):
        ka = 2 * j
        re, im = _stage_b_rows(ka)
        re1, im1 = _stage_b_rows(ka + 1)
        u = _dot(f3_ref[...], prod)
        y_scr[re, :] = u[:NB, :LANES]
        y_scr[im, :] = u[NB:, :LANES]
        y_scr[re1, :] = u[:NB, LANES:]
        y_scr[im1, :] = u[NB:, LANES:]

    def stage_b(j, prod):
        nxt = spectrum_product(j + 1)
        inverse_b(j, prod)
        return nxt

    last = lax.fori_loop(0, NA // 2 - 1, stage_b, spectrum_product(0), unroll=10)
    inverse_b(NA // 2 - 1, last)

    def stage_c(b, carry):
        r0 = pl.multiple_of(b * Y_PITCH, 8)
        ur = y_scr[pl.ds(r0, NA), :]
        ui = y_scr[pl.ds(r0 + NA, NA), :]
        rhs = jnp.concatenate([jnp.concatenate([ur, -ui], axis=0), jnp.concatenate([ui, ur], axis=0)],
                              axis=1)
        y = _dot1c(ai_ref[b], rhs)
        xr_scr[pl.ds(b, na_half, stride=X_PITCH), :] = y[:, :LANES]
        xi_scr[pl.ds(b, na_half, stride=X_PITCH), :] = y[:, LANES:]
        return carry

    lax.fori_loop(0, NB, stage_c, 0, unroll=64)

    def store_out(a, carry):
        for bi in range(2):
            mul = conv_slab(m_ref, bi, a, w_mul)
            o_ref[bi, pl.ds(pl.multiple_of(a * NB, NB), NB), :] = (
                x_scr[bi][pl.ds(pl.multiple_of(a * X_PITCH, 8), NB), :] * mul)
        return carry

    edges_then_interior(store_out)


def _hy_lat_call(src, src_col, u, mul_col, conv_w2, g_spec, layer, order, *, conv_in):
    b, l, _ = u.shape
    nct = W_HY // LANES
    (af, _), (ai, _), _, (f2, _), (f3, _) = _lat_dft_tables()
    one = pl.Buffered(1)
    blk = lambda col: pl.BlockSpec((2, l, LANES), lambda c, p: (p, 0, col + c))
    const = lambda a: pl.BlockSpec(a.shape, lambda c, p: (0,) * a.ndim, pipeline_mode=one)
    return pl.pallas_call(
        functools.partial(_hy_lat_kernel, conv_in=conv_in),
        grid=(nct, b // 2),
        in_specs=[blk(src_col), blk(mul_col),
                  pl.BlockSpec((2, 3, LANES), lambda c, p: (0, 0, c)),
                  pl.BlockSpec((None, None, NA * 2 * NB, LANES), lambda c, p: (layer, order * nct + c, 0, 0)),
                  const(af), const(ai), const(f2), const(f3)],
        out_specs=pl.BlockSpec((2, l, LANES), lambda c, p: (p, 0, c)),
        out_shape=jax.ShapeDtypeStruct((b, l, W_HY), F32),
        scratch_shapes=[pltpu.VMEM(((NA // 2) * X_PITCH, LANES), F32),
                        pltpu.VMEM(((NA // 2) * X_PITCH, LANES), F32),
                        pltpu.VMEM((NB * Y_PITCH, LANES), F32)],
        compiler_params=_cparams(("arbitrary", "arbitrary")),
        name="hyena_lat_conv_in" if conv_in else "hyena_lat",
    )(src, u, conv_w2, g_spec, af, ai, f2, f3)


def _rope_tables(l):
    f32 = np.float32
    rows = l // GRID_W
    row = np.repeat(np.arange(rows), GRID_W).astype(f32)
    col = np.tile(np.arange(GRID_W), rows).astype(f32)
    quarter = HEAD_DIM // 4
    inv = np.power(f32(ROPE_BASE), -np.arange(quarter, dtype=f32) / f32(quarter)).astype(f32)
    ang = np.concatenate([row[:, None] * inv, col[:, None] * inv], axis=-1).astype(f32)
    cos, sin = np.cos(ang), np.sin(ang)
    q = quarter
    cos_h = np.concatenate([cos[:, :q], cos[:, :q], cos[:, q:], cos[:, q:]], axis=-1)
    sin_h = np.concatenate([-sin[:, :q], sin[:, :q], -sin[:, q:], sin[:, q:]], axis=-1)
    return jnp.asarray(np.tile(cos_h, (1, 2))), jnp.asarray(np.tile(sin_h, (1, 2)))


def kernel(x_prompt, x_sample, c, cache_k, cache_v, state_ret, c_ctx, norm_w, w_mod, b_mod, w_in, hy_conv,
           hy_filt_w1, hy_filt_b1, hy_filt_freq, hy_filt_w2, hy_skip, attn_sink, ret_theta, ret_gn,
           w_branch_a, w_branch_b, w_branch_c, w_merge, b_merge, w_out, final_norm_w):
    d = D_MODEL
    bc, lc, _ = x_prompt.shape
    bl, ll, _ = x_sample.shape
    assert ll == LAT_L and bc % 2 == 0 and bl % 2 == 0
    past = cache_k.shape[2]

    assert bl + 1 <= COND_ROWS
    cond = jnp.zeros((COND_ROWS, d), F32).at[:bl].set(c).at[bl].set(c_ctx)
    mod = _mod_call(cond, w_mod, b_mod)

    g_ctx = _filt_ctx_call(lc, hy_filt_w1, hy_filt_b1, hy_filt_freq, hy_filt_w2, hy_skip)
    g_lat = _filt_lat_call(hy_filt_w1, hy_filt_b1, hy_filt_freq, hy_filt_w2, hy_skip)

    cos_t, sin_t = _rope_tables(ll)
    w_in_b = w_in.astype(BF16)
    wm_b = w_merge.astype(BF16)
    wa_b = w_branch_a.astype(BF16)
    wb_b = w_branch_b.astype(BF16)
    wc_b = w_branch_c.astype(BF16)
    wo_b = w_out.astype(BF16)
    fnw = final_norm_w.reshape(1, d)
    k_ctx = cache_k.reshape(bl, DEPTH, past, W_KV)
    v_ctx = cache_v.reshape(bl, DEPTH, past, W_KV)
    hy_cols = COL_HY // LANES
    nct = W_HY // LANES

    xp, xs = x_prompt, x_sample
    ks_out, vs_out, ss_out = [], [], []
    for l in range(DEPTH):
        final = l == DEPTH - 1
        nw = norm_w[l].reshape(1, d)
        bm = b_merge[l].reshape(1, -1)
        gn = ret_gn[l].reshape(1, W_RET)
        shift, scale, gate = (mod[l, :, i * d:(i + 1) * d][:, None, :] for i in range(3))
        conv_w = hy_conv[l]
        cw = lambda g: conv_w[:, g * W_HY:(g + 1) * W_HY]

        sl = slice(bl, bl + 1)
        u = _in_call(xp, shift[sl], scale[sl], nw, w_in_b[l], cos_t, sin_t, rope=False)
        ya = _hy_ctx_call(u, conv_w, g_ctx, l)
        yb = _attn_ctx_call(u, attn_sink[l])
        yc, sfin = _ret_call(u, ret_theta[l], None)
        res = _out_call(xp, shift[sl], scale[sl], gate[sl], nw, ya, u, yb, yc, gn, wm_b[l], bm, wa_b[l], wb_b[l],
                        wc_b[l], wo_b[l], fnw, final=final)
        xp = res[0]
        if final:
            y_prompt = res[1]
        ks_out.append(u[:, :, COL_KA:COL_KA + W_KV])
        vs_out.append(u[:, :, COL_VA:COL_VA + W_KV])
        ss_out.append(sfin)

        sl = slice(0, bl)
        u = _in_call(xs, shift[sl], scale[sl], nw, w_in_b[l], cos_t, sin_t, rope=True)
        z1 = _hy_lat_call(u, hy_cols, u, hy_cols + nct, jnp.stack([cw(0), cw(1)]), g_lat, l, 0, conv_in=True)
        ya = _hy_lat_call(z1, 0, u, hy_cols + 2 * nct, jnp.stack([cw(2), cw(2)]), g_lat, l, 1, conv_in=False)
        yb = _attn_lat_call(u, k_ctx, v_ctx, attn_sink[l], l)
        yc, _ = _ret_call(u, ret_theta[l], state_ret, l)
        res = _out_call(xs, shift[sl], scale[sl], gate[sl], nw, ya, u, yb, yc, gn, wm_b[l], bm, wa_b[l], wb_b[l],
                        wc_b[l], wo_b[l], fnw, final=final)
        xs = res[0]
        if final:
            y_sample = res[1]

    kv_shape = (bc, DEPTH, lc, ATT_KV_HEADS, HEAD_DIM)
    new_cache_k = jnp.stack(ks_out, axis=1).reshape(kv_shape)
    new_cache_v = jnp.stack(vs_out, axis=1).reshape(kv_shape)
    new_state_ret = jnp.stack(ss_out, axis=1)
    return (y_prompt, y_sample, new_cache_k, new_cache_v, new_state_ret)
```

```python
import functools
import math

import numpy as np
import jax
import jax.numpy as jnp
from jax import lax
from jax.experimental import pallas as pl
from jax.experimental.pallas import tpu as pltpu

F32 = jnp.float32
BF16 = jnp.bfloat16
HIGHEST = lax.Precision.HIGHEST

D_MODEL = 1024
DEPTH = 4
GRID_W = 64
W_HY = 512
HY_BANDS = 8
HY_POS_FEAT = 1 + 2 * HY_BANDS
HY_FILT_HID = 64
HY_POS_PAD = 32
COND_ROWS = 16
HY_DECAY_TARGET = 1e-2
HY_FAST_PCT = 0.3
HY_SLOW_PCT = 1.5
ATT_HEADS = 8
ATT_KV_HEADS = 2
ATT_GROUP = ATT_HEADS // ATT_KV_HEADS
HEAD_DIM = 64
W_ATT = ATT_HEADS * HEAD_DIM
W_KV = ATT_KV_HEADS * HEAD_DIM
ATT_BLOCK = 128
ATT_QB = 8
RET_HEADS = 8
RET_DIM = 64
W_RET = RET_HEADS * RET_DIM
RET_CHUNK = 128
ROPE_BASE = 10000.0
EPS = 1e-6
NEG = -1e30

LANES = 128
MXU_ROWS = 512
IN_PARTS = 2
VMEM_LIMIT = 58 * 1024 * 1024

IN_DIM = 5376
COL_HY = 0
COL_GH = 1536
COL_QA = 2048
COL_GA = 2560
COL_QR = 3072
COL_KR = 3584
COL_VR = 4096
COL_GR = 4608
COL_KA = 5120
COL_VA = 5248
_W_COL = {COL_HY: 0, COL_HY + W_HY: 512, COL_HY + 2 * W_HY: 1024, COL_GH: 1536, COL_QA: 2048, COL_KA: 2560,
          COL_VA: 2688, COL_GA: 2816, COL_QR: 3328, COL_KR: 3840, COL_VR: 4352, COL_GR: 4864}

LAT_L = 4096
LAT_N = 2 * LAT_L
NA = 64
NB = 128
Y_PITCH = 136
X_PITCH = 136


def _cparams(sem):
    return pltpu.CompilerParams(dimension_semantics=sem, vmem_limit_bytes=VMEM_LIMIT)


def _split_np(a):
    a32 = np.asarray(a, np.float32)
    hi = a32.astype(BF16)
    lo = (a32 - hi.astype(np.float32)).astype(BF16)
    return jnp.asarray(hi), jnp.asarray(lo)


def _split(x):
    hi = x.astype(BF16)
    lo = (x - hi.astype(F32)).astype(BF16)
    return hi, lo


def _dot(a, b):
    return jnp.dot(a, b, preferred_element_type=F32)


def _dot3c(chi, clo, x):
    xh, xl = _split(x)
    return _dot(chi, xh) + _dot(clo, xh) + _dot(chi, xl)


def _dot1c(chi, x):
    return _dot(chi, x.astype(BF16))


def _silu(x):
    return x * jax.nn.sigmoid(x)


def _mod_kernel(c_ref, w_ref, b_ref, o_ref):
    s = _silu(c_ref[...])
    o_ref[0] = jnp.dot(s, w_ref[0], precision=HIGHEST, preferred_element_type=F32) + b_ref[0]


def _mod_call(cond, w_mod, b_mod):
    rows, d = cond.shape
    n = w_mod.shape[-1]
    tn = 1024
    return pl.pallas_call(
        _mod_kernel,
        grid=(DEPTH, n // tn),
        in_specs=[pl.BlockSpec((rows, d), lambda l, j: (0, 0)),
                  pl.BlockSpec((1, d, tn), lambda l, j: (l, 0, j)),
                  pl.BlockSpec((1, 1, tn), lambda l, j: (l, 0, j))],
        out_specs=pl.BlockSpec((1, rows, tn), lambda l, j: (l, 0, j)),
        out_shape=jax.ShapeDtypeStruct((DEPTH, rows, n), F32),
        compiler_params=_cparams(("arbitrary", "arbitrary")),
        name="adaln_mod",
    )(cond, w_mod, b_mod.reshape(DEPTH, 1, n))


def _modulated(x, nw, scale, shift):
    ms = jnp.mean(x * x, axis=-1, keepdims=True)
    h = x * lax.rsqrt(ms + EPS) * nw
    return h * (1.0 + scale) + shift


def _rope128(x, cos, sin_signed, first_half):
    up = pltpu.roll(x, LANES - 16, 1)
    dn = pltpu.roll(x, 16, 1)
    return x * cos + jnp.where(first_half, up, dn) * sin_signed


def _rows(ref):
    bt, tm, w = ref.shape
    return ref[...].reshape(bt * tm, w)


def _put(ref, c0, val):
    bt, tm, _ = ref.shape
    ref[:, :, c0:c0 + val.shape[1]] = val.reshape(bt, tm, val.shape[1])


def _in_kernel(x_ref, shift_ref, scale_ref, nw_ref, w_ref, cos_ref, sin_ref, o_ref, *, rope):
    x = _rows(x_ref)
    rows = x.shape[0]
    part = rows // IN_PARTS
    hbs = [_modulated(x[r:r + part], nw_ref[...], scale_ref[0], shift_ref[0]).astype(BF16)
           for r in range(0, rows, part)]
    if rope:
        cos = cos_ref[...]
        sin = sin_ref[...]
        lane = lax.broadcasted_iota(jnp.int32, (rows, LANES), 1)
        first_half = (lane % 32) < 16

    def seg(c0, width):
        w0 = _W_COL[c0]
        return jnp.concatenate([_dot(hb, w_ref[:, w0:w0 + width]) for hb in hbs], axis=0)

    def put_rope(c0, val, mul):
        for i in range(val.shape[1] // LANES):
            piece = val[:, i * LANES:(i + 1) * LANES]
            if rope:
                piece = _rope128(piece, cos, sin, first_half)
            if mul is not None:
                piece = piece * mul
            _put(o_ref, c0 + i * LANES, piece)

    for g in range(3):
        _put(o_ref, COL_HY + g * W_HY, seg(COL_HY + g * W_HY, W_HY))
    _put(o_ref, COL_GH, _silu(seg(COL_GH, W_HY)))
    put_rope(COL_QA, seg(COL_QA, W_ATT), None)
    _put(o_ref, COL_GA, _silu(seg(COL_GA, W_ATT)))
    put_rope(COL_QR, seg(COL_QR, W_RET), None)
    put_rope(COL_KR, seg(COL_KR, W_RET), RET_DIM ** -0.5)
    _put(o_ref, COL_VR, seg(COL_VR, W_RET))
    _put(o_ref, COL_GR, _silu(seg(COL_GR, W_RET)))
    put_rope(COL_KA, seg(COL_KA, W_KV), None)
    _put(o_ref, COL_VA, seg(COL_VA, W_KV))


def _token_tiling(b, l, per_batch):
    if l >= MXU_ROWS:
        return 1, MXU_ROWS
    bt = 1 if per_batch else min(b, MXU_ROWS // l)
    return bt, l


def _in_call(x, shift, scale, nw, w, cos_t, sin_t, *, rope):
    b, l, d = x.shape
    per_batch = shift.shape[0] > 1
    bt, tm = _token_tiling(b, l, per_batch)
    assert not rope or bt == 1
    mod_map = (lambda i, j: (i, 0, 0)) if per_batch else (lambda i, j: (0, 0, 0))
    return pl.pallas_call(
        functools.partial(_in_kernel, rope=rope),
        grid=(b // bt, l // tm),
        in_specs=[pl.BlockSpec((bt, tm, d), lambda i, j: (i, j, 0)),
                  pl.BlockSpec((1, 1, d), mod_map),
                  pl.BlockSpec((1, 1, d), mod_map),
                  pl.BlockSpec((1, d), lambda i, j: (0, 0)),
                  pl.BlockSpec((d, IN_DIM), lambda i, j: (0, 0), pipeline_mode=pl.Buffered(1)),
                  pl.BlockSpec((tm, LANES), lambda i, j: (j, 0)),
                  pl.BlockSpec((tm, LANES), lambda i, j: (j, 0))],
        out_specs=pl.BlockSpec((bt, tm, IN_DIM), lambda i, j: (i, j, 0)),
        out_shape=jax.ShapeDtypeStruct((b, l, IN_DIM), F32),
        compiler_params=_cparams(("parallel", "parallel")),
        name="in_proj_rope" if rope else "in_proj",
    )(x, shift, scale, nw, w, cos_t, sin_t)


def _retention_post(o, gn, gate):
    lane = lax.broadcasted_iota(jnp.int32, (1, LANES), 1)
    lo_head = lane < RET_DIM
    outs = []
    for t in range(W_RET // LANES):
        sl = slice(t * LANES, (t + 1) * LANES)
        ot = o[:, sl]
        o2 = ot * ot
        s_lo = jnp.sum(jnp.where(lo_head, o2, 0.0), axis=-1, keepdims=True)
        s_hi = jnp.sum(jnp.where(lo_head, 0.0, o2), axis=-1, keepdims=True)
        ms = jnp.where(lo_head, s_lo, s_hi) * (1.0 / RET_DIM)
        outs.append(ot * lax.rsqrt(ms + EPS) * gn[:, sl] * gate[:, sl])
    return jnp.concatenate(outs, axis=1)


def _out_kernel(x_ref, shift_ref, scale_ref, gate_ref, nw_ref, ya_ref, gh_ref, yb_ref, yc_ref, gr_ref, gn_ref,
                wm_ref, bm_ref, wa_ref, wb_ref, wc_ref, wo_ref, fnw_ref, *out_refs, final):
    x = _rows(x_ref)
    d = x.shape[1]
    hb = _modulated(x, nw_ref[...], scale_ref[0], shift_ref[0]).astype(BF16)
    branches = (_rows(ya_ref) * _rows(gh_ref), _rows(yb_ref),
                _retention_post(_rows(yc_ref), gn_ref[...], _rows(gr_ref)))
    merged = None
    for i, (y, w_ref) in enumerate(zip(branches, (wa_ref, wb_ref, wc_ref))):
        g = jax.nn.sigmoid(_dot(hb, wm_ref[:, i * d:(i + 1) * d]) + bm_ref[:, i * d:(i + 1) * d])
        term = g * _dot(y.astype(BF16), w_ref[...])
        merged = term if merged is None else merged + term
    out = _dot(merged.astype(BF16), wo_ref[...])
    xn = x + gate_ref[0] * out
    _put(out_refs[0], 0, xn)
    if final:
        ms = jnp.mean(xn * xn, axis=-1, keepdims=True)
        _put(out_refs[1], 0, xn * lax.rsqrt(ms + EPS) * fnw_ref[...])


def _out_call(x, shift, scale, gate, nw, ya, u, yb, yc, gn, wm, bm, wa, wb, wc, wo, fnw, *, final):
    b, l, d = x.shape
    per_batch = shift.shape[0] > 1
    bt, tm = _token_tiling(b, l, per_batch)
    mod_map = (lambda i, j: (i, 0, 0)) if per_batch else (lambda i, j: (0, 0, 0))
    tok = lambda w: pl.BlockSpec((bt, tm, w), lambda i, j: (i, j, 0))
    full = lambda a: pl.BlockSpec(a.shape, lambda i, j: (0,) * a.ndim, pipeline_mode=pl.Buffered(1))
    n_out = 2 if final else 1
    res = pl.pallas_call(
        functools.partial(_out_kernel, final=final),
        grid=(b // bt, l // tm),
        in_specs=[tok(d), pl.BlockSpec((1, 1, d), mod_map), pl.BlockSpec((1, 1, d), mod_map),
                  pl.BlockSpec((1, 1, d), mod_map), full(nw), tok(W_HY),
                  pl.BlockSpec((bt, tm, W_HY), lambda i, j: (i, j, COL_GH // W_HY)), tok(W_ATT), tok(W_RET),
                  pl.BlockSpec((bt, tm, W_RET), lambda i, j: (i, j, COL_GR // W_RET)), full(gn),
                  full(wm), full(bm), full(wa), full(wb), full(wc), full(wo), full(fnw)],
        out_specs=[tok(d)] * n_out,
        out_shape=[jax.ShapeDtypeStruct((b, l, d), F32)] * n_out,
        compiler_params=_cparams(("parallel", "parallel")),
        name="merge_out_final" if final else "merge_out",
    )(x, shift, scale, gate, nw, ya, u, yb, yc, u, gn, wm, bm, wa, wb, wc, wo, fnw)
    return res


_NT = (((1,), (1,)), ((), ()))


_TN = (((0,), (0,)), ((), ()))
LOG2E = 1.4426950408889634
Q_SCALE = (HEAD_DIM ** -0.5) * LOG2E


def _attn_scores_t(q, kh, kv):
    h0 = kv * ATT_GROUP
    qs = jnp.concatenate([q[:, (h0 + g) * HEAD_DIM:(h0 + g + 1) * HEAD_DIM] for g in range(ATT_GROUP)],
                         axis=0).astype(BF16)
    return lax.dot_general(kh, qs, _NT, preferred_element_type=F32)


def _attn_finish_t(sink_ref, s, vh, g_ref, o_ref, kv, row0=0):
    tk, cols = s.shape
    t = cols // ATT_GROUP
    h0 = kv * ATT_GROUP
    head = lax.broadcasted_iota(jnp.int32, (1, cols), 1) // t
    sink = jnp.full((1, cols), sink_ref[h0], F32)
    for g in range(1, ATT_GROUP):
        sink = jnp.where(head == g, sink_ref[h0 + g], sink)
    sink = sink * LOG2E
    m = jnp.maximum(jnp.max(s, axis=0, keepdims=True), sink)
    p = jnp.exp2(s - m).astype(BF16)
    v_ext = jnp.concatenate([vh, jnp.ones((tk, HEAD_DIM), BF16)], axis=1)
    o_ext = lax.dot_general(v_ext, p, _TN, preferred_element_type=F32)
    denom = o_ext[HEAD_DIM:HEAD_DIM + 1] + jnp.exp2(sink - m)
    o = o_ext[:HEAD_DIM] / denom
    for gp in range(ATT_GROUP // 2):
        pair = jnp.concatenate([o[:, (2 * gp) * t:(2 * gp + 1) * t], o[:, (2 * gp + 1) * t:(2 * gp + 2) * t]],
                               axis=0)
        c0 = (h0 + 2 * gp) * HEAD_DIM
        o_ref[0, row0:row0 + t, c0:c0 + 2 * HEAD_DIM] = pair.T * g_ref[0, row0:row0 + t, c0:c0 + 2 * HEAD_DIM]


def _attn_ctx_kernel(sink_ref, q_ref, k_ref, v_ref, g_ref, o_ref):
    q = q_ref[0] * Q_SCALE
    k = k_ref[0].astype(BF16)
    v = v_ref[0].astype(BF16)
    scores = [_attn_scores_t(q, k[:, kv * HEAD_DIM:(kv + 1) * HEAD_DIM], kv) for kv in range(ATT_KV_HEADS)]
    for kv in range(ATT_KV_HEADS):
        _attn_finish_t(sink_ref, scores[kv], v[:, kv * HEAD_DIM:(kv + 1) * HEAD_DIM], g_ref, o_ref, kv)


def _attn_ctx_call(u, sink):
    b, l, _ = u.shape
    return pl.pallas_call(
        _attn_ctx_kernel,
        grid=(b,),
        in_specs=[pl.BlockSpec(memory_space=pltpu.SMEM),
                  pl.BlockSpec((1, l, W_ATT), lambda i: (i, 0, COL_QA // W_ATT)),
                  pl.BlockSpec((1, l, W_KV), lambda i: (i, 0, COL_KA // W_KV)),
                  pl.BlockSpec((1, l, W_KV), lambda i: (i, 0, COL_VA // W_KV)),
                  pl.BlockSpec((1, l, W_ATT), lambda i: (i, 0, COL_GA // W_ATT))],
        out_specs=pl.BlockSpec((1, l, W_ATT), lambda i: (i, 0, 0)),
        out_shape=jax.ShapeDtypeStruct((b, l, W_ATT), F32),
        compiler_params=_cparams(("parallel",)),
        name="attn_ctx",
    )(sink, u, u, u, u)


def _attn_lat_kernel(sink_ref, q_ref, kp_ref, kc_ref, kn_ref, vp_ref, vc_ref, vn_ref, kx_ref, vx_ref,
                     g_ref, o_ref):
    j = pl.program_id(1)
    last = pl.num_programs(1) - 1
    b = ATT_BLOCK
    nq = ATT_QB
    bf = lambda ref: ref[0].astype(BF16)
    kx, vx = bf(kx_ref), bf(vx_ref)
    kblk = [bf(kp_ref)] + [kc_ref[0, t * b:(t + 1) * b, :].astype(BF16) for t in range(nq)] + [bf(kn_ref)]
    vblk = [bf(vp_ref)] + [vc_ref[0, t * b:(t + 1) * b, :].astype(BF16) for t in range(nq)] + [bf(vn_ref)]
    keys = [jnp.concatenate(kblk[t:t + 3] + [kx], axis=0) for t in range(nq)]
    vals = [jnp.concatenate(vblk[t:t + 3] + [vx], axis=0) for t in range(nq)]
    cols = ATT_GROUP * b
    c = lax.broadcasted_iota(jnp.int32, (b, cols), 0)
    r = lax.broadcasted_iota(jnp.int32, (b, cols), 1) % b
    ok_prev = [(c >= r) & (j > 0)] + [c >= r] * (nq - 1)
    ok_next = [c <= r] * (nq - 1) + [(c <= r) & (j < last)]

    def band(s, t):
        return jnp.concatenate([jnp.where(ok_prev[t], s[:b], NEG), s[b:2 * b],
                                jnp.where(ok_next[t], s[2 * b:3 * b], NEG), s[3 * b:]], axis=0)

    chains = [(t, kv) for t in range(nq) for kv in range(ATT_KV_HEADS)]
    scores = []
    for t, kv in chains:
        q = q_ref[0, t * b:(t + 1) * b, :] * Q_SCALE
        scores.append(band(_attn_scores_t(q, keys[t][:, kv * HEAD_DIM:(kv + 1) * HEAD_DIM], kv), t))
    for (t, kv), s in zip(chains, scores):
        _attn_finish_t(sink_ref, s, vals[t][:, kv * HEAD_DIM:(kv + 1) * HEAD_DIM], g_ref, o_ref, kv, t * b)


def _attn_lat_call(u, kctx, vctx, sink, layer):
    b, l, _ = u.shape
    nb = l // ATT_BLOCK
    past = kctx.shape[2]
    kcol = COL_KA // W_KV
    vcol = COL_VA // W_KV
    nq = ATT_QB
    prev = lambda col: pl.BlockSpec((1, ATT_BLOCK, W_KV), lambda i, j: (i, jnp.maximum(nq * j - 1, 0), col))
    cur = lambda col: pl.BlockSpec((1, nq * ATT_BLOCK, W_KV), lambda i, j: (i, j, col))
    nxt = lambda col: pl.BlockSpec((1, ATT_BLOCK, W_KV),
                                   lambda i, j: (i, jnp.minimum(nq * j + nq, nb - 1), col))
    ctx = pl.BlockSpec((1, None, past, W_KV), lambda i, j: (i, layer, 0, 0))
    return pl.pallas_call(
        _attn_lat_kernel,
        grid=(b, nb // nq),
        in_specs=[pl.BlockSpec(memory_space=pltpu.SMEM),
                  pl.BlockSpec((1, nq * ATT_BLOCK, W_ATT), lambda i, j: (i, j, COL_QA // W_ATT)),
                  prev(kcol), cur(kcol), nxt(kcol), prev(vcol), cur(vcol), nxt(vcol), ctx, ctx,
                  pl.BlockSpec((1, nq * ATT_BLOCK, W_ATT), lambda i, j: (i, j, COL_GA // W_ATT))],
        out_specs=pl.BlockSpec((1, nq * ATT_BLOCK, W_ATT), lambda i, j: (i, j, 0)),
        out_shape=jax.ShapeDtypeStruct((b, l, W_ATT), F32),
        compiler_params=_cparams(("parallel", "parallel")),
        name="attn_lat",
    )(sink, u, u, u, u, u, u, u, kctx, vctx, u)


def _log_sigmoid(x):
    return jnp.minimum(x, 0.0) - jnp.log1p(jnp.exp(-jnp.abs(x)))


_TAB_DMAT = 0
_TAB_QDEC = 2 * RET_CHUNK
_TAB_KDEC = 3 * RET_CHUNK
_TAB_CDEC = 4 * RET_CHUNK
_TAB_ROWS = 4 * RET_CHUNK + 8


def _ret_kernel(q_ref, k_ref, v_ref, thl_ref, thb_ref, s0_ref, o_ref, sfin_ref, tab_ref, *, nc, cpt,
                npairs, has_s0):
    grp = pl.program_id(0)
    c = RET_CHUNK
    lane = lax.broadcasted_iota(jnp.int32, (1, LANES), 1)
    lo_head = lane < RET_DIM
    dd = lax.broadcasted_iota(jnp.int32, (LANES, LANES), 0)
    ee = lax.broadcasted_iota(jnp.int32, (LANES, LANES), 1)
    same_head = (dd < RET_DIM) == (ee < RET_DIM)
    lanes_of = lambda p: slice(p * LANES, (p + 1) * LANES)

    @pl.when(pl.program_id(1) == 0)
    def _():
        rowf = lax.broadcasted_iota(jnp.int32, (c, LANES), 0).astype(F32)
        ii = lax.broadcasted_iota(jnp.int32, (c, c), 0)
        jj = lax.broadcasted_iota(jnp.int32, (c, c), 1)
        for p in range(npairs):
            for d in range(2):
                lg_lane = _log_sigmoid(thl_ref[d, :, lanes_of(p)])
                dist = (ii - jj) if d == 0 else (jj - ii)
                for hh in range(2):
                    head = 2 * (grp * npairs + p) + hh
                    lg_h = _log_sigmoid(thb_ref[d, pl.ds(head, 1), :])
                    dm = jnp.where(dist >= 0, jnp.exp(lg_h * jnp.maximum(dist, 0).astype(F32)), 0.0)
                    tab_ref[p, d, _TAB_DMAT + hh * c:_TAB_DMAT + (hh + 1) * c, :] = dm
                if d == 0:
                    q_dec = jnp.exp(lg_lane * (rowf + 1.0))
                    k_dec = jnp.exp(lg_lane * (c - 1.0 - rowf))
                else:
                    q_dec = jnp.exp(lg_lane * (c - rowf))
                    k_dec = jnp.exp(lg_lane * rowf)
                tab_ref[p, d, _TAB_QDEC:_TAB_QDEC + c, :] = q_dec
                tab_ref[p, d, _TAB_KDEC:_TAB_KDEC + c, :] = k_dec
                tab_ref[p, d, _TAB_CDEC:_TAB_CDEC + 8, :] = jnp.broadcast_to(
                    jnp.exp(lg_lane * float(c)), (8, LANES))

    def first_level(p, d, r0):
        qc = q_ref[0, pl.ds(r0, c), lanes_of(p)]
        kc = k_ref[0, pl.ds(r0, c), lanes_of(p)]
        vcb = v_ref[0, pl.ds(r0, c), lanes_of(p)].astype(BF16)
        qs = jnp.concatenate([jnp.where(lo_head, qc, 0.0), jnp.where(lo_head, 0.0, qc)], axis=0)
        sc = lax.dot_general(qs.astype(BF16), kc.astype(BF16), _NT, preferred_element_type=F32)
        sc = sc * tab_ref[p, d, _TAB_DMAT:_TAB_DMAT + 2 * c, :]
        kd = kc * tab_ref[p, d, _TAB_KDEC:_TAB_KDEC + c, :]
        upd = jnp.where(same_head, _dot(kd.T.astype(BF16), vcb), 0.0)
        return qc, vcb, sc.astype(BF16), upd

    def second_level(p, d, lvl1, s):
        qc, vcb, scb, upd = lvl1
        pv = _dot(scb, vcb)
        qd = qc * tab_ref[p, d, _TAB_QDEC:_TAB_QDEC + c, :]
        o = _dot(qd.astype(BF16), s.astype(BF16)) + jnp.where(lo_head, pv[:c], pv[c:])
        return o, tab_ref[p, d, _TAB_CDEC:_TAB_CDEC + 1, :] * s + upd

    def init_state(p, d):
        if not has_s0:
            return jnp.zeros((LANES, LANES), F32)
        z = jnp.zeros((RET_DIM, RET_DIM), F32)
        return jnp.concatenate([jnp.concatenate([s0_ref[0, d, 2 * p], z], axis=1),
                                jnp.concatenate([z, s0_ref[0, d, 2 * p + 1]], axis=1)], axis=0)

    units = [(p, d) for p in range(npairs) for d in range(2)]

    def scan_body(second, n, states):
        def row0(d, j):
            idx = n * cpt + j
            return pl.multiple_of((idx if d == 0 else nc - 1 - idx) * c, c)

        lvl = {(p, d, j): first_level(p, d, row0(d, j)) for j in range(cpt) for p, d in units}
        states = list(states)
        for j in range(cpt):
            for ui, (p, d) in enumerate(units):
                o, states[ui] = second_level(p, d, lvl[(p, d, j)], states[ui])
                dst = (0, pl.ds(row0(d, j), c), lanes_of(p))
                if second(j):
                    o = o + o_ref[dst]
                o_ref[dst] = o
        return tuple(states)

    assert nc % cpt == 0
    trips = nc // cpt
    states = tuple(init_state(p, d) for p, d in units)
    if trips == 1:
        states = scan_body(lambda j: 2 * j > nc - 1, 0, states)
    else:
        assert trips % 2 == 0
        states = lax.fori_loop(0, trips // 2, functools.partial(scan_body, lambda j: False), states)
        states = lax.fori_loop(trips // 2, trips, functools.partial(scan_body, lambda j: True), states)
    for ui, (p, d) in enumerate(units):
        sfin_ref[0, d, 2 * p] = states[ui][:RET_DIM, :RET_DIM]
        sfin_ref[0, d, 2 * p + 1] = states[ui][RET_DIM:, RET_DIM:]


def _ret_call(u, theta, s0bd, layer=0):
    b, l, _ = u.shape
    nc = l // RET_CHUNK
    has_s0 = s0bd is not None
    cpt = 8 if nc >= 8 else nc
    npairs = max(1, 8 // cpt)
    ngrp = RET_HEADS // 2 // npairs
    w = npairs * LANES
    st_block = (2, 2 * npairs, RET_DIM, RET_DIM)
    if not has_s0:
        s0bd = jnp.zeros((1,) + st_block, F32)
        s0_spec = pl.BlockSpec((1,) + st_block, lambda g, i: (0, 0, 0, 0, 0))
    else:
        s0_spec = pl.BlockSpec((1, None) + st_block, lambda g, i: (i, layer, 0, g, 0, 0))
    th_lane = jnp.repeat(theta, RET_DIM, axis=1).reshape(2, 1, W_RET)
    th_bcast = jnp.broadcast_to(theta[:, :, None], (2, RET_HEADS, LANES))
    col = lambda c0: pl.BlockSpec((1, l, w), lambda g, i: (i, 0, c0 // w + g))
    o, sfin = pl.pallas_call(
        functools.partial(_ret_kernel, nc=nc, cpt=cpt, npairs=npairs, has_s0=has_s0),
        grid=(ngrp, b),
        in_specs=[col(COL_QR), col(COL_KR), col(COL_VR),
                  pl.BlockSpec((2, 1, w), lambda g, i: (0, 0, g)),
                  pl.BlockSpec((2, RET_HEADS, LANES), lambda g, i: (0, 0, 0)),
                  s0_spec],
        out_specs=[pl.BlockSpec((1, l, w), lambda g, i: (i, 0, g)),
                   pl.BlockSpec((1,) + st_block, lambda g, i: (i, 0, g, 0, 0))],
        out_shape=[jax.ShapeDtypeStruct((b, l, W_RET), F32),
                   jax.ShapeDtypeStruct((b, 2, RET_HEADS, RET_DIM, RET_DIM), F32)],
        scratch_shapes=[pltpu.VMEM((npairs, 2, _TAB_ROWS, LANES), F32)],
        compiler_params=_cparams(("arbitrary", "arbitrary")),
        name="retention_s0" if has_s0 else "retention",
    )(u, u, u, th_lane, th_bcast, s0bd)
    return o, sfin


def _filter_positions(l):
    f32 = np.float32
    t = np.linspace(0.0, 1.0, l, dtype=f32)[:, None]
    w = (f32(2.0 * math.pi) * np.arange(l, dtype=f32)[:, None] / f32(l)).astype(f32)
    f = np.linspace(1e-4, HY_BANDS - 1, HY_BANDS, dtype=f32)[None, :]
    z = np.concatenate([t, np.cos(f * w), -np.sin(f * w)], axis=-1).astype(f32)
    z = np.pad(z, ((0, 0), (0, HY_POS_PAD - HY_POS_FEAT)))
    return jnp.asarray(np.concatenate([z, z[:1], z[1:][::-1]], axis=0))


def _hyena_deltas():
    max_decay = math.log(HY_DECAY_TARGET) / HY_FAST_PCT
    min_decay = math.log(HY_DECAY_TARGET) / HY_SLOW_PCT
    return jnp.asarray(np.abs(np.linspace(min_decay, max_decay, W_HY, dtype=np.float32))[None, :])


def _filter_hidden(z_ref, w1_ref, b1_ref, fr_ref):
    pre = jnp.dot(z_ref[...], w1_ref[0], precision=HIGHEST, preferred_element_type=F32) + b1_ref[0]
    return jnp.sin(fr_ref[0] * pre)


def _filter_raw(hid, w2f, w2b, tp, dl, row0, l):
    win = jnp.exp(-tp * dl)
    row = row0 + lax.broadcasted_iota(jnp.int32, win.shape, 0)
    hf = jnp.dot(hid, w2f, precision=HIGHEST, preferred_element_type=F32) * win
    hb = jnp.dot(hid, w2b, precision=HIGHEST, preferred_element_type=F32) * win
    hf = jnp.where(row < l, hf, 0.0)
    hb = jnp.where((row > l) | (row == 0), hb, 0.0)
    return hf + hb, jnp.sum(jnp.abs(hf) + jnp.abs(hb), axis=0, keepdims=True)


def _with_skip(g, skip):
    row = lax.broadcasted_iota(jnp.int32, g.shape, 0)
    return g + jnp.where(row == 0, skip, 0.0)


def _filt_ctx_kernel(z_ref, w1_ref, b1_ref, fr_ref, w2_ref, dl_ref, sk_ref, fh_ref, fl_ref, g_ref):
    hid = _filter_hidden(z_ref, w1_ref, b1_ref, fr_ref)
    tp = z_ref[:, 0:1]
    for o in range(2):
        w2f = w2_ref[0, :, (2 * o) * W_HY:(2 * o + 1) * W_HY]
        w2b = w2_ref[0, :, (2 * o + 1) * W_HY:(2 * o + 2) * W_HY]
        raw, nrm = _filter_raw(hid, w2f, w2b, tp, dl_ref[...], 0, z_ref.shape[0] // 2)
        g = _with_skip(raw / nrm, sk_ref[0, pl.ds(o, 1), :])
        g_ref[0, o] = _dot3c(fh_ref[...], fl_ref[...], g)


def _ctx_dft_tables(l):
    n = 2 * l
    k = np.arange(n)[:, None]
    t = np.arange(l)[None, :]
    ang = 2.0 * np.pi * k * t / n
    c, s = np.cos(ang), np.sin(ang)
    fwd = np.block([[c, s], [-s, c]])
    inv = np.block([[c.T, -s.T], [s.T, c.T]])
    n_all = np.arange(n)[None, :]
    angg = 2.0 * np.pi * k * n_all / n
    filt = np.concatenate([np.cos(angg), -np.sin(angg)], axis=0) / n
    return _split_np(fwd), _split_np(inv), _split_np(filt)


def _filt_ctx_call(l, w1, b1, freq, w2, skip):
    n = 2 * l
    z_ext = _filter_positions(l)
    _, _, (fh, fl) = _ctx_dft_tables(l)
    w1p = jnp.pad(w1, ((0, 0), (0, HY_POS_PAD - HY_POS_FEAT), (0, 0)))
    lay = lambda *shape: pl.BlockSpec((1,) + shape, lambda d: (d,) + (0,) * len(shape))
    full = lambda a: pl.BlockSpec(a.shape, lambda d: (0,) * a.ndim)
    dl = _hyena_deltas()
    return pl.pallas_call(
        _filt_ctx_kernel,
        grid=(DEPTH,),
        in_specs=[full(z_ext), lay(HY_POS_PAD, HY_FILT_HID), lay(1, HY_FILT_HID), lay(1, HY_FILT_HID),
                  lay(HY_FILT_HID, 4 * W_HY), full(dl), lay(2, W_HY), full(fh), full(fl)],
        out_specs=pl.BlockSpec((1, 2, 2 * n, W_HY), lambda d: (d, 0, 0, 0)),
        out_shape=jax.ShapeDtypeStruct((DEPTH, 2, 2 * n, W_HY), F32),
        compiler_params=_cparams(("arbitrary",)),
        name="hyena_filter_ctx",
    )(z_ext, w1p, b1.reshape(DEPTH, 1, -1), freq.reshape(DEPTH, 1, -1), w2, dl, skip, fh, fl)


def _lat_dft_tables():
    ka = np.arange(NA)[:, None]
    b = np.arange(NB)[:, None, None]
    kb = np.arange(NB)[:, None]
    bb = np.arange(NB)[None, :]
    a_half = np.arange(NA // 2)[None, :]
    a_full = np.arange(NA)[None, :]
    phi = 2.0 * np.pi * (ka * a_half / NA + b * ka / LAT_N)
    c, s = np.cos(phi), np.sin(phi)
    a_fwd = np.concatenate([c, s], axis=2)
    ct, st = np.swapaxes(c, 1, 2), np.swapaxes(s, 1, 2)
    a_inv = np.concatenate([ct, st], axis=2)
    phig = 2.0 * np.pi * (ka * a_full / NA + b * ka / LAT_N)
    a_flt = np.concatenate([np.cos(phig), -np.sin(phig)], axis=1) / LAT_N
    ang = 2.0 * np.pi * kb * bb / NB
    c2, s2 = np.cos(ang), np.sin(ang)
    f_fwd = np.block([[c2, s2], [-s2, c2]])
    f_inv = np.block([[c2, -s2], [s2, c2]])
    return (_split_np(a_fwd), _split_np(a_inv), _split_np(a_flt), _split_np(f_fwd), _split_np(f_inv))


def _stage_b_rows(ka):
    re = pl.ds(ka, NB, stride=Y_PITCH)
    im = pl.ds(NA + ka, NB, stride=Y_PITCH)
    return re, im


def _filt_lat_kernel(z_ref, w1_ref, b1_ref, fr_ref, w2f_ref, w2b_ref, dl_ref, sk_ref, ah_ref, f2h_ref,
                     g_ref, hid_ref, gt_ref, y_ref):
    step = pl.program_id(1)
    rch = 1024
    nch = LAT_N // rch
    rows_of = lambda i: pl.ds(pl.multiple_of(i * rch, rch), rch)

    @pl.when(step == 0)
    def _():
        def hid_chunk(i, carry):
            r = rows_of(i)
            pre = jnp.dot(z_ref[r, :], w1_ref[0], precision=HIGHEST, preferred_element_type=F32)
            hid_ref[r, :] = jnp.sin(fr_ref[0] * (pre + b1_ref[0]))
            return carry

        lax.fori_loop(0, nch, hid_chunk, 0)

    w2f_hl = _split(w2f_ref[0])
    w2b_hl = _split(w2b_ref[0])

    def raw_chunk(w2_hl, i, nrm):
        r = rows_of(i)
        hh, hl = _split(hid_ref[r, :])
        h = _dot(hh, w2_hl[0]) + _dot(hl, w2_hl[0]) + _dot(hh, w2_hl[1])
        h = h * jnp.exp(-z_ref[r, 0:1] * dl_ref[...])
        row = i * rch + lax.broadcasted_iota(jnp.int32, h.shape, 0)
        h = jnp.where(row == LAT_L, 0.0, h)
        for s in range(rch // NB):
            slab = pl.ds(pl.multiple_of((i * (rch // NB) + s) * X_PITCH, 8), NB)
            gt_ref[slab, :] = h[s * NB:(s + 1) * NB]
        return nrm + jnp.sum(jnp.abs(h), axis=0, keepdims=True)

    nrm = lax.fori_loop(0, nch // 2, functools.partial(raw_chunk, w2f_hl), jnp.zeros((1, LANES), F32))
    nrm = lax.fori_loop(nch // 2, nch, functools.partial(raw_chunk, w2b_hl), nrm)
    hh, hl = _split(hid_ref[0:8, :])
    hb0 = _dot(hh, w2b_hl[0]) + _dot(hl, w2b_hl[0]) + _dot(hh, w2b_hl[1])
    hb0 = hb0 * jnp.exp(-z_ref[0:8, 0:1] * dl_ref[...])
    hb0 = jnp.where(lax.broadcasted_iota(jnp.int32, hb0.shape, 0) == 0, hb0, 0.0)
    gt_ref[0:8, :] = gt_ref[0:8, :] + hb0
    nrm = nrm + jnp.sum(jnp.abs(hb0), axis=0, keepdims=True)

    def norm_slab(a, carry):
        slab = pl.ds(pl.multiple_of(a * X_PITCH, 8), NB)
        gt_ref[slab, :] = gt_ref[slab, :] / nrm
        return carry

    lax.fori_loop(0, NA, norm_slab, 0, unroll=8)
    order = step // (W_HY // LANES)
    gt_ref[0:8, :] = _with_skip(gt_ref[0:8, :], sk_ref[0, pl.ds(order, 1), :])

    def stage_a(b, carry):
        rows = gt_ref[pl.ds(b, NA, stride=X_PITCH), :]
        y_ref[pl.ds(pl.multiple_of(b * Y_PITCH, 8), 2 * NA), :] = _dot1c(ah_ref[b], rows)
        return carry

    lax.fori_loop(0, NB, stage_a, 0, unroll=32)

    def stage_b(j, carry):
        ka = 2 * j
        re, im = _stage_b_rows(ka)
        re1, im1 = _stage_b_rows(ka + 1)
        z = jnp.concatenate([jnp.concatenate([y_ref[re, :], y_ref[im, :]], axis=0),
                             jnp.concatenate([y_ref[re1, :], y_ref[im1, :]], axis=0)], axis=1)
        x = _dot1c(f2h_ref[...], z)
        g0 = pl.multiple_of(ka * 2 * NB, 2 * NB)
        g_ref[0, 0, pl.ds(g0, 2 * NB), :] = x[:, :LANES]
        g_ref[0, 0, pl.ds(g0 + 2 * NB, 2 * NB), :] = x[:, LANES:]
        return carry

    lax.fori_loop(0, NA // 2, stage_b, 0, unroll=8)


def _filt_lat_call(w1, b1, freq, w2, skip):
    z_ext = _filter_positions(LAT_L)
    _, _, (ah, _), (f2h, _), _ = _lat_dft_tables()
    w1p = jnp.pad(w1, ((0, 0), (0, HY_POS_PAD - HY_POS_FEAT), (0, 0)))
    nct = W_HY // LANES
    one = pl.Buffered(1)
    lay = lambda *shape: pl.BlockSpec((1,) + shape, lambda d, s: (d,) + (0,) * len(shape))
    full = lambda a: pl.BlockSpec(a.shape, lambda d, s: (0,) * a.ndim, pipeline_mode=one)
    dl = _hyena_deltas()
    return pl.pallas_call(
        _filt_lat_kernel,
        grid=(DEPTH, 2 * nct),
        in_specs=[full(z_ext), lay(HY_POS_PAD, HY_FILT_HID), lay(1, HY_FILT_HID), lay(1, HY_FILT_HID),
                  pl.BlockSpec((1, HY_FILT_HID, LANES), lambda d, s: (d, 0, (s // nct) * 2 * nct + s % nct)),
                  pl.BlockSpec((1, HY_FILT_HID, LANES),
                               lambda d, s: (d, 0, (s // nct) * 2 * nct + nct + s % nct)),
                  pl.BlockSpec((1, LANES), lambda d, s: (0, s % nct)),
                  pl.BlockSpec((1, 2, LANES), lambda d, s: (d, 0, s % nct)),
                  full(ah), full(f2h)],
        out_specs=pl.BlockSpec((1, 1, NA * 2 * NB, LANES), lambda d, s: (d, s, 0, 0)),
        out_shape=jax.ShapeDtypeStruct((DEPTH, 2 * nct, NA * 2 * NB, LANES), F32),
        scratch_shapes=[pltpu.VMEM((LAT_N, HY_FILT_HID), F32), pltpu.VMEM((NA * X_PITCH, LANES), F32),
                        pltpu.VMEM((NB * Y_PITCH, LANES), F32)],
        compiler_params=_cparams(("arbitrary", "arbitrary")),
        name="hyena_filter_lat",
    )(z_ext, w1p, b1.reshape(DEPTH, 1, -1), freq.reshape(DEPTH, 1, -1), w2, w2, dl, skip, ah, f2h)


def _short_conv_rows(ref, bi, r0, rows, first, last, w):
    total = ref.shape[1]
    cur = ref[bi, pl.ds(r0, rows), :]
    before = ref[bi, pl.ds(jnp.maximum(r0 - 1, 0), 1), :]
    after = ref[bi, pl.ds(jnp.minimum(r0 + rows, total - 1), 1), :]
    before = jnp.where(first, 0.0, before)
    after = jnp.where(last, 0.0, after)
    rid = lax.broadcasted_iota(jnp.int32, cur.shape, 0)
    prev = jnp.where(rid == 0, before, pltpu.roll(cur, 1, 0))
    nxt = jnp.where(rid == rows - 1, after, pltpu.roll(cur, rows - 1, 0))
    return prev * w[0:1] + cur * w[1:2] + nxt * w[2:3]


def _short_conv_interior(ref, bi, r0, rows, w):
    prev = ref[bi, pl.ds(r0 - 1, rows), :]
    cur = ref[bi, pl.ds(r0, rows), :]
    nxt = ref[bi, pl.ds(r0 + 1, rows), :]
    return prev * w[0:1] + cur * w[1:2] + nxt * w[2:3]


def _cmul(xr, xi, gr, gi):
    return xr * gr - xi * gi, xr * gi + xi * gr


def _hy_ctx_kernel(v_ref, x1_ref, x2_ref, cw_ref, g_ref, fh_ref, ih_ref, o_ref):
    l = v_ref.shape[1]
    n = 2 * l

    def sc(ref, bi, grp):
        w = cw_ref[:, grp * W_HY:(grp + 1) * W_HY]
        return _short_conv_rows(ref, bi, 0, l, True, True, w)

    def conv(zr, zi, order):
        x = _dot1c(fh_ref[...], jnp.concatenate([zr, zi], axis=0))
        pr, pi = _cmul(x[:n], x[n:], g_ref[order, :n], g_ref[order, n:])
        y = _dot1c(ih_ref[...], jnp.concatenate([pr, pi], axis=0))
        return y[:l], y[l:]

    yr, yi = conv(sc(v_ref, 0, 0), sc(v_ref, 1, 0), 0)
    yr, yi = conv(sc(x1_ref, 0, 1) * yr, sc(x1_ref, 1, 1) * yi, 1)
    o_ref[0] = sc(x2_ref, 0, 2) * yr
    o_ref[1] = sc(x2_ref, 1, 2) * yi


def _hy_ctx_call(u, conv_w, g_spec, layer):
    b, l, _ = u.shape
    (fh, _), (ih, _), _ = _ctx_dft_tables(l)
    grp = lambda g: pl.BlockSpec((2, l, W_HY), lambda i: (i, 0, g))
    full = lambda a: pl.BlockSpec(a.shape, lambda i: (0,) * a.ndim)
    return pl.pallas_call(
        _hy_ctx_kernel,
        grid=(b // 2,),
        in_specs=[grp(0), grp(1), grp(2), full(conv_w),
                  pl.BlockSpec((None,) + g_spec.shape[1:], lambda i: (layer, 0, 0, 0)), full(fh), full(ih)],
        out_specs=pl.BlockSpec((2, l, W_HY), lambda i: (i, 0, 0)),
        out_shape=jax.ShapeDtypeStruct((b, l, W_HY), F32),
        compiler_params=_cparams(("parallel",)),
        name="hyena_ctx",
    )(u, u, u, conv_w, g_spec, fh, ih)


def _hy_lat_kernel(z_ref, m_ref, cw_ref, g_ref, af_ref, ai_ref, f2_ref, f3_ref, o_ref, xr_scr, xi_scr,
                   y_scr, *, conv_in):
    x_scr = (xr_scr, xi_scr)
    na_half = NA // 2
    w_in = cw_ref[0] if conv_in else None
    w_mul = cw_ref[1]

    def conv_slab(ref, bi, a, w):
        if isinstance(a, int):
            return _short_conv_rows(ref, bi, a * NB, NB, a == 0, a == na_half - 1, w)
        return _short_conv_interior(ref, bi, pl.multiple_of(a * NB, NB), NB, w)

    def edges_then_interior(body):
        body(0, 0)
        body(na_half - 1, 0)
        lax.fori_loop(1, na_half - 1, body, 0, unroll=2)

    def load_in(a, carry):
        for bi in range(2):
            if conv_in:
                val = conv_slab(z_ref, bi, a, w_in)
            else:
                val = z_ref[bi, pl.ds(pl.multiple_of(a * NB, NB), NB), :]
            x_scr[bi][pl.ds(pl.multiple_of(a * X_PITCH, 8), NB), :] = val
        return carry

    edges_then_interior(load_in)

    def stage_a(b, carry):
        zr = xr_scr[pl.ds(b, na_half, stride=X_PITCH), :]
        zi = xi_scr[pl.ds(b, na_half, stride=X_PITCH), :]
        rhs = jnp.concatenate([jnp.concatenate([zr, zi], axis=0), jnp.concatenate([zi, -zr], axis=0)],
                              axis=1)
        y = _dot1c(af_ref[b], rhs)
        r0 = pl.multiple_of(b * Y_PITCH, 8)
        y_scr[pl.ds(r0, NA), :] = y[:, :LANES]
        y_scr[pl.ds(r0 + NA, NA), :] = y[:, LANES:]
        return carry

    lax.fori_loop(0, NB, stage_a, 0, unroll=NB)

    def spectrum_product(j):
        ka = 2 * j
        re, im = _stage_b_rows(ka)
        re1, im1 = _stage_b_rows(ka + 1)
        z = jnp.concatenate([jnp.concatenate([y_scr[re, :], y_scr[im, :]], axis=0),
                             jnp.concatenate([y_scr[re1, :], y_scr[im1, :]], axis=0)], axis=1)
        x = _dot1c(f2_ref[...], z)
        g0 = pl.multiple_of(ka * 2 * NB, 2 * NB)
        gr = jnp.concatenate([g_ref[pl.ds(g0, NB), :], g_ref[pl.ds(g0 + 2 * NB, NB), :]], axis=1)
        gi = jnp.concatenate([g_ref[pl.ds(g0 + NB, NB), :], g_ref[pl.ds(g0 + 3 * NB, NB), :]], axis=1)
        pr, pi = _cmul(x[:NB], x[NB:], gr, gi)
        return jnp.concatenate([pr, pi], axis=0).astype(BF16)

    def inverse_b(j, prod):
        ka = 2 * j
        re, im = _stage_b_rows(ka)
        re1, im1 = _stage_b_rows(ka + 1)
        u = _dot(f3_ref[...], prod)
        y_scr[re, :] = u[:NB, :LANES]
        y_scr[im, :] = u[NB:, :LANES]
        y_scr[re1, :] = u[:NB, LANES:]
        y_scr[im1, :] = u[NB:, LANES:]

    def stage_b(j, prod):
        nxt = spectrum_product(j + 1)
        inverse_b(j, prod)
        return nxt

    last = lax.fori_loop(0, NA // 2 - 1, stage_b, spectrum_product(0), unroll=10)
    inverse_b(NA // 2 - 1, last)

    def stage_c(b, carry):
        r0 = pl.multiple_of(b * Y_PITCH, 8)
        ur = y_scr[pl.ds(r0, NA), :]
        ui = y_scr[pl.ds(r0 + NA, NA), :]
        rhs = jnp.concatenate([jnp.concatenate([ur, -ui], axis=0), jnp.concatenate([ui, ur], axis=0)],
                              axis=1)
        y = _dot1c(ai_ref[b], rhs)
        xr_scr[pl.ds(b, na_half, stride=X_PITCH), :] = y[:, :LANES]
        xi_scr[pl.ds(b, na_half, stride=X_PITCH), :] = y[:, LANES:]
        return carry

    lax.fori_loop(0, NB, stage_c, 0, unroll=NB)

    def store_out(a, carry):
        for bi in range(2):
            mul = conv_slab(m_ref, bi, a, w_mul)
            o_ref[bi, pl.ds(pl.multiple_of(a * NB, NB), NB), :] = (
                x_scr[bi][pl.ds(pl.multiple_of(a * X_PITCH, 8), NB), :] * mul)
        return carry

    edges_then_interior(store_out)


def _hy_lat_call(src, src_col, u, mul_col, conv_w2, g_spec, layer, order, *, conv_in):
    b, l, _ = u.shape
    nct = W_HY // LANES
    (af, _), (ai, _), _, (f2, _), (f3, _) = _lat_dft_tables()
    one = pl.Buffered(1)
    blk = lambda col: pl.BlockSpec((2, l, LANES), lambda c, p: (p, 0, col + c))
    const = lambda a: pl.BlockSpec(a.shape, lambda c, p: (0,) * a.ndim, pipeline_mode=one)
    return pl.pallas_call(
        functools.partial(_hy_lat_kernel, conv_in=conv_in),
        grid=(nct, b // 2),
        in_specs=[blk(src_col), blk(mul_col),
                  pl.BlockSpec((2, 3, LANES), lambda c, p: (0, 0, c)),
                  pl.BlockSpec((None, None, NA * 2 * NB, LANES), lambda c, p: (layer, order * nct + c, 0, 0)),
                  const(af), const(ai), const(f2), const(f3)],
        out_specs=pl.BlockSpec((2, l, LANES), lambda c, p: (p, 0, c)),
        out_shape=jax.ShapeDtypeStruct((b, l, W_HY), F32),
        scratch_shapes=[pltpu.VMEM(((NA // 2) * X_PITCH, LANES), F32),
                        pltpu.VMEM(((NA // 2) * X_PITCH, LANES), F32),
                        pltpu.VMEM((NB * Y_PITCH, LANES), F32)],
        compiler_params=_cparams(("arbitrary", "arbitrary")),
        name="hyena_lat_conv_in" if conv_in else "hyena_lat",
    )(src, u, conv_w2, g_spec, af, ai, f2, f3)


def _rope_tables(l):
    f32 = np.float32
    rows = l // GRID_W
    row = np.repeat(np.arange(rows), GRID_W).astype(f32)
    col = np.tile(np.arange(GRID_W), rows).astype(f32)
    quarter = HEAD_DIM // 4
    inv = np.power(f32(ROPE_BASE), -np.arange(quarter, dtype=f32) / f32(quarter)).astype(f32)
    ang = np.concatenate([row[:, None] * inv, col[:, None] * inv], axis=-1).astype(f32)
    cos, sin = np.cos(ang), np.sin(ang)
    q = quarter
    cos_h = np.concatenate([cos[:, :q], cos[:, :q], cos[:, q:], cos[:, q:]], axis=-1)
    sin_h = np.concatenate([-sin[:, :q], sin[:, :q], -sin[:, q:], sin[:, q:]], axis=-1)
    return jnp.asarray(np.tile(cos_h, (1, 2))), jnp.asarray(np.tile(sin_h, (1, 2)))


def kernel(x_prompt, x_sample, c, cache_k, cache_v, state_ret, c_ctx, norm_w, w_mod, b_mod, w_in, hy_conv,
           hy_filt_w1, hy_filt_b1, hy_filt_freq, hy_filt_w2, hy_skip, attn_sink, ret_theta, ret_gn,
           w_branch_a, w_branch_b, w_branch_c, w_merge, b_merge, w_out, final_norm_w):
    d = D_MODEL
    bc, lc, _ = x_prompt.shape
    bl, ll, _ = x_sample.shape
    assert ll == LAT_L and bc % 2 == 0 and bl % 2 == 0
    past = cache_k.shape[2]

    assert bl + 1 <= COND_ROWS
    cond = jnp.zeros((COND_ROWS, d), F32).at[:bl].set(c).at[bl].set(c_ctx)
    mod = _mod_call(cond, w_mod, b_mod)

    g_ctx = _filt_ctx_call(lc, hy_filt_w1, hy_filt_b1, hy_filt_freq, hy_filt_w2, hy_skip)
    g_lat = _filt_lat_call(hy_filt_w1, hy_filt_b1, hy_filt_freq, hy_filt_w2, hy_skip)

    cos_t, sin_t = _rope_tables(ll)
    w_in_b = w_in.astype(BF16)
    wm_b = w_merge.astype(BF16)
    wa_b = w_branch_a.astype(BF16)
    wb_b = w_branch_b.astype(BF16)
    wc_b = w_branch_c.astype(BF16)
    wo_b = w_out.astype(BF16)
    fnw = final_norm_w.reshape(1, d)
    k_ctx = cache_k.reshape(bl, DEPTH, past, W_KV)
    v_ctx = cache_v.reshape(bl, DEPTH, past, W_KV)
    hy_cols = COL_HY // LANES
    nct = W_HY // LANES

    xp, xs = x_prompt, x_sample
    ks_out, vs_out, ss_out = [], [], []
    for l in range(DEPTH):
        final = l == DEPTH - 1
        nw = norm_w[l].reshape(1, d)
        bm = b_merge[l].reshape(1, -1)
        gn = ret_gn[l].reshape(1, W_RET)
        shift, scale, gate = (mod[l, :, i * d:(i + 1) * d][:, None, :] for i in range(3))
        conv_w = hy_conv[l]
        cw = lambda g: conv_w[:, g * W_HY:(g + 1) * W_HY]

        sl = slice(bl, bl + 1)
        u = _in_call(xp, shift[sl], scale[sl], nw, w_in_b[l], cos_t, sin_t, rope=False)
        ya = _hy_ctx_call(u, conv_w, g_ctx, l)
        yb = _attn_ctx_call(u, attn_sink[l])
        yc, sfin = _ret_call(u, ret_theta[l], None)
        res = _out_call(xp, shift[sl], scale[sl], gate[sl], nw, ya, u, yb, yc, gn, wm_b[l], bm, wa_b[l], wb_b[l],
                        wc_b[l], wo_b[l], fnw, final=final)
        xp = res[0]
        if final:
            y_prompt = res[1]
        ks_out.append(u[:, :, COL_KA:COL_KA + W_KV])
        vs_out.append(u[:, :, COL_VA:COL_VA + W_KV])
        ss_out.append(sfin)

        sl = slice(0, bl)
        u = _in_call(xs, shift[sl], scale[sl], nw, w_in_b[l], cos_t, sin_t, rope=True)
        z1 = _hy_lat_call(u, hy_cols, u, hy_cols + nct, jnp.stack([cw(0), cw(1)]), g_lat, l, 0, conv_in=True)
        ya = _hy_lat_call(z1, 0, u, hy_cols + 2 * nct, jnp.stack([cw(2), cw(2)]), g_lat, l, 1, conv_in=False)
        yb = _attn_lat_call(u, k_ctx, v_ctx, attn_sink[l], l)
        yc, _ = _ret_call(u, ret_theta[l], state_ret, l)
        res = _out_call(xs, shift[sl], scale[sl], gate[sl], nw, ya, u, yb, yc, gn, wm_b[l], bm, wa_b[l], wb_b[l],
                        wc_b[l], wo_b[l], fnw, final=final)
        xs = res[0]
        if final:
            y_sample = res[1]

    kv_shape = (bc, DEPTH, lc, ATT_KV_HEADS, HEAD_DIM)
    new_cache_k = jnp.stack(ks_out, axis=1).reshape(kv_shape)
    new_cache_v = jnp.stack(vs_out, axis=1).reshape(kv_shape)
    new_state_ret = jnp.stack(ss_out, axis=1)
    return (y_prompt, y_sample, new_cache_k, new_cache_v, new_state_ret)
```

```python
import functools
import math

import numpy as np
import jax
import jax.numpy as jnp
from jax import lax
from jax.experimental import pallas as pl
from jax.experimental.pallas import tpu as pltpu

F32 = jnp.float32
BF16 = jnp.bfloat16
HIGHEST = lax.Precision.HIGHEST

D_MODEL = 1024
DEPTH = 4
GRID_W = 64
W_HY = 512
HY_BANDS = 8
HY_POS_FEAT = 1 + 2 * HY_BANDS
HY_FILT_HID = 64
HY_POS_PAD = 32
COND_ROWS = 16
HY_DECAY_TARGET = 1e-2
HY_FAST_PCT = 0.3
HY_SLOW_PCT = 1.5
ATT_HEADS = 8
ATT_KV_HEADS = 2
ATT_GROUP = ATT_HEADS // ATT_KV_HEADS
HEAD_DIM = 64
W_ATT = ATT_HEADS * HEAD_DIM
W_KV = ATT_KV_HEADS * HEAD_DIM
ATT_BLOCK = 128
ATT_QB = 8
RET_HEADS = 8
RET_DIM = 64
W_RET = RET_HEADS * RET_DIM
RET_CHUNK = 128
ROPE_BASE = 10000.0
EPS = 1e-6
NEG = -1e30

LANES = 128
MXU_ROWS = 512
IN_PARTS = 2
VMEM_LIMIT = 58 * 1024 * 1024

IN_DIM = 5376
COL_HY = 0
COL_GH = 1536
COL_QA = 2048
COL_GA = 2560
COL_QR = 3072
COL_KR = 3584
COL_VR = 4096
COL_GR = 4608
COL_KA = 5120
COL_VA = 5248
_W_COL = {COL_HY: 0, COL_HY + W_HY: 512, COL_HY + 2 * W_HY: 1024, COL_GH: 1536, COL_QA: 2048, COL_KA: 2560,
          COL_VA: 2688, COL_GA: 2816, COL_QR: 3328, COL_KR: 3840, COL_VR: 4352, COL_GR: 4864}

LAT_L = 4096
LAT_N = 2 * LAT_L
NA = 64
NB = 128
Y_PITCH = 136
X_PITCH = 136


def _cparams(sem):
    return pltpu.CompilerParams(dimension_semantics=sem, vmem_limit_bytes=VMEM_LIMIT)


def _split_np(a):
    a32 = np.asarray(a, np.float32)
    hi = a32.astype(BF16)
    lo = (a32 - hi.astype(np.float32)).astype(BF16)
    return jnp.asarray(hi), jnp.asarray(lo)


def _split(x):
    hi = x.astype(BF16)
    lo = (x - hi.astype(F32)).astype(BF16)
    return hi, lo


def _dot(a, b):
    return jnp.dot(a, b, preferred_element_type=F32)


def _dot3c(chi, clo, x):
    xh, xl = _split(x)
    return _dot(chi, xh) + _dot(clo, xh) + _dot(chi, xl)


def _dot1c(chi, x):
    return _dot(chi, x.astype(BF16))


def _silu(x):
    return x * jax.nn.sigmoid(x)


def _mod_kernel(c_ref, w_ref, b_ref, o_ref):
    s = _silu(c_ref[...])
    o_ref[0] = jnp.dot(s, w_ref[0], precision=HIGHEST, preferred_element_type=F32) + b_ref[0]


def _mod_call(cond, w_mod, b_mod):
    rows, d = cond.shape
    n = w_mod.shape[-1]
    tn = 1024
    return pl.pallas_call(
        _mod_kernel,
        grid=(DEPTH, n // tn),
        in_specs=[pl.BlockSpec((rows, d), lambda l, j: (0, 0)),
                  pl.BlockSpec((1, d, tn), lambda l, j: (l, 0, j)),
                  pl.BlockSpec((1, 1, tn), lambda l, j: (l, 0, j))],
        out_specs=pl.BlockSpec((1, rows, tn), lambda l, j: (l, 0, j)),
        out_shape=jax.ShapeDtypeStruct((DEPTH, rows, n), F32),
        compiler_params=_cparams(("arbitrary", "arbitrary")),
        name="adaln_mod",
    )(cond, w_mod, b_mod.reshape(DEPTH, 1, n))


def _modulated(x, nw, scale, shift):
    ms = jnp.mean(x * x, axis=-1, keepdims=True)
    h = x * lax.rsqrt(ms + EPS) * nw
    return h * (1.0 + scale) + shift


def _rope128(x, cos, sin_signed, first_half):
    up = pltpu.roll(x, LANES - 16, 1)
    dn = pltpu.roll(x, 16, 1)
    return x * cos + jnp.where(first_half, up, dn) * sin_signed


def _rows(ref):
    bt, tm, w = ref.shape
    return ref[...].reshape(bt * tm, w)


def _put(ref, c0, val):
    bt, tm, _ = ref.shape
    ref[:, :, c0:c0 + val.shape[1]] = val.reshape(bt, tm, val.shape[1])


def _in_kernel(x_ref, shift_ref, scale_ref, nw_ref, w_ref, cos_ref, sin_ref, o_ref, *, rope):
    x = _rows(x_ref)
    rows = x.shape[0]
    part = rows // IN_PARTS
    hbs = [_modulated(x[r:r + part], nw_ref[...], scale_ref[0], shift_ref[0]).astype(BF16)
           for r in range(0, rows, part)]
    if rope:
        cos = cos_ref[...]
        sin = sin_ref[...]
        lane = lax.broadcasted_iota(jnp.int32, (rows, LANES), 1)
        first_half = (lane % 32) < 16

    def seg(c0, width):
        w0 = _W_COL[c0]
        return jnp.concatenate([_dot(hb, w_ref[:, w0:w0 + width]) for hb in hbs], axis=0)

    def put_rope(c0, val, mul):
        for i in range(val.shape[1] // LANES):
            piece = val[:, i * LANES:(i + 1) * LANES]
            if rope:
                piece = _rope128(piece, cos, sin, first_half)
            if mul is not None:
                piece = piece * mul
            _put(o_ref, c0 + i * LANES, piece)

    for g in range(3):
        _put(o_ref, COL_HY + g * W_HY, seg(COL_HY + g * W_HY, W_HY))
    _put(o_ref, COL_GH, _silu(seg(COL_GH, W_HY)))
    put_rope(COL_QA, seg(COL_QA, W_ATT), None)
    _put(o_ref, COL_GA, _silu(seg(COL_GA, W_ATT)))
    put_rope(COL_QR, seg(COL_QR, W_RET), None)
    put_rope(COL_KR, seg(COL_KR, W_RET), RET_DIM ** -0.5)
    _put(o_ref, COL_VR, seg(COL_VR, W_RET))
    _put(o_ref, COL_GR, _silu(seg(COL_GR, W_RET)))
    put_rope(COL_KA, seg(COL_KA, W_KV), None)
    _put(o_ref, COL_VA, seg(COL_VA, W_KV))


def _token_tiling(b, l, per_batch):
    if l >= MXU_ROWS:
        return 1, MXU_ROWS
    bt = 1 if per_batch else min(b, MXU_ROWS // l)
    return bt, l


def _in_call(x, shift, scale, nw, w, cos_t, sin_t, *, rope):
    b, l, d = x.shape
    per_batch = shift.shape[0] > 1
    bt, tm = _token_tiling(b, l, per_batch)
    assert not rope or bt == 1
    mod_map = (lambda i, j: (i, 0, 0)) if per_batch else (lambda i, j: (0, 0, 0))
    return pl.pallas_call(
        functools.partial(_in_kernel, rope=rope),
        grid=(b // bt, l // tm),
        in_specs=[pl.BlockSpec((bt, tm, d), lambda i, j: (i, j, 0)),
                  pl.BlockSpec((1, 1, d), mod_map),
                  pl.BlockSpec((1, 1, d), mod_map),
                  pl.BlockSpec((1, d), lambda i, j: (0, 0)),
                  pl.BlockSpec((d, IN_DIM), lambda i, j: (0, 0), pipeline_mode=pl.Buffered(1)),
                  pl.BlockSpec((tm, LANES), lambda i, j: (j, 0)),
                  pl.BlockSpec((tm, LANES), lambda i, j: (j, 0))],
        out_specs=pl.BlockSpec((bt, tm, IN_DIM), lambda i, j: (i, j, 0)),
        out_shape=jax.ShapeDtypeStruct((b, l, IN_DIM), F32),
        compiler_params=_cparams(("parallel", "parallel")),
        name="in_proj_rope" if rope else "in_proj",
    )(x, shift, scale, nw, w, cos_t, sin_t)


def _retention_post(o, gn, gate):
    lane = lax.broadcasted_iota(jnp.int32, (1, LANES), 1)
    lo_head = lane < RET_DIM
    outs = []
    for t in range(W_RET // LANES):
        sl = slice(t * LANES, (t + 1) * LANES)
        ot = o[:, sl]
        o2 = ot * ot
        s_lo = jnp.sum(jnp.where(lo_head, o2, 0.0), axis=-1, keepdims=True)
        s_hi = jnp.sum(jnp.where(lo_head, 0.0, o2), axis=-1, keepdims=True)
        ms = jnp.where(lo_head, s_lo, s_hi) * (1.0 / RET_DIM)
        outs.append(ot * lax.rsqrt(ms + EPS) * gn[:, sl] * gate[:, sl])
    return jnp.concatenate(outs, axis=1)


def _out_kernel(x_ref, shift_ref, scale_ref, gate_ref, nw_ref, ya_ref, gh_ref, yb_ref, yc_ref, gr_ref, gn_ref,
                wm_ref, bm_ref, wa_ref, wb_ref, wc_ref, wo_ref, fnw_ref, *out_refs, final):
    x = _rows(x_ref)
    d = x.shape[1]
    hb = _modulated(x, nw_ref[...], scale_ref[0], shift_ref[0]).astype(BF16)
    branches = (_rows(ya_ref) * _rows(gh_ref), _rows(yb_ref),
                _retention_post(_rows(yc_ref), gn_ref[...], _rows(gr_ref)))
    merged = None
    for i, (y, w_ref) in enumerate(zip(branches, (wa_ref, wb_ref, wc_ref))):
        g = jax.nn.sigmoid(_dot(hb, wm_ref[:, i * d:(i + 1) * d]) + bm_ref[:, i * d:(i + 1) * d])
        term = g * _dot(y.astype(BF16), w_ref[...])
        merged = term if merged is None else merged + term
    out = _dot(merged.astype(BF16), wo_ref[...])
    xn = x + gate_ref[0] * out
    _put(out_refs[0], 0, xn)
    if final:
        ms = jnp.mean(xn * xn, axis=-1, keepdims=True)
        _put(out_refs[1], 0, xn * lax.rsqrt(ms + EPS) * fnw_ref[...])


def _out_call(x, shift, scale, gate, nw, ya, u, yb, yc, gn, wm, bm, wa, wb, wc, wo, fnw, *, final):
    b, l, d = x.shape
    per_batch = shift.shape[0] > 1
    bt, tm = _token_tiling(b, l, per_batch)
    mod_map = (lambda i, j: (i, 0, 0)) if per_batch else (lambda i, j: (0, 0, 0))
    tok = lambda w: pl.BlockSpec((bt, tm, w), lambda i, j: (i, j, 0))
    full = lambda a: pl.BlockSpec(a.shape, lambda i, j: (0,) * a.ndim, pipeline_mode=pl.Buffered(1))
    n_out = 2 if final else 1
    res = pl.pallas_call(
        functools.partial(_out_kernel, final=final),
        grid=(b // bt, l // tm),
        in_specs=[tok(d), pl.BlockSpec((1, 1, d), mod_map), pl.BlockSpec((1, 1, d), mod_map),
                  pl.BlockSpec((1, 1, d), mod_map), full(nw), tok(W_HY),
                  pl.BlockSpec((bt, tm, W_HY), lambda i, j: (i, j, COL_GH // W_HY)), tok(W_ATT), tok(W_RET),
                  pl.BlockSpec((bt, tm, W_RET), lambda i, j: (i, j, COL_GR // W_RET)), full(gn),
                  full(wm), full(bm), full(wa), full(wb), full(wc), full(wo), full(fnw)],
        out_specs=[tok(d)] * n_out,
        out_shape=[jax.ShapeDtypeStruct((b, l, d), F32)] * n_out,
        compiler_params=_cparams(("parallel", "parallel")),
        name="merge_out_final" if final else "merge_out",
    )(x, shift, scale, gate, nw, ya, u, yb, yc, u, gn, wm, bm, wa, wb, wc, wo, fnw)
    return res


_NT = (((1,), (1,)), ((), ()))


_TN = (((0,), (0,)), ((), ()))
LOG2E = 1.4426950408889634
Q_SCALE = (HEAD_DIM ** -0.5) * LOG2E


def _attn_scores_t(q, kh, kv):
    h0 = kv * ATT_GROUP
    qs = jnp.concatenate([q[:, (h0 + g) * HEAD_DIM:(h0 + g + 1) * HEAD_DIM] for g in range(ATT_GROUP)],
                         axis=0).astype(BF16)
    return lax.dot_general(kh, qs, _NT, preferred_element_type=F32)


def _attn_finish_t(sink_ref, s, vh, g_ref, o_ref, kv, row0=0):
    tk, cols = s.shape
    t = cols // ATT_GROUP
    h0 = kv * ATT_GROUP
    head = lax.broadcasted_iota(jnp.int32, (1, cols), 1) // t
    sink = jnp.full((1, cols), sink_ref[h0], F32)
    for g in range(1, ATT_GROUP):
        sink = jnp.where(head == g, sink_ref[h0 + g], sink)
    sink = sink * LOG2E
    m = jnp.maximum(jnp.max(s, axis=0, keepdims=True), sink)
    p = jnp.exp2(s - m).astype(BF16)
    v_ext = jnp.concatenate([vh, jnp.ones((tk, HEAD_DIM), BF16)], axis=1)
    o_ext = lax.dot_general(v_ext, p, _TN, preferred_element_type=F32)
    denom = o_ext[HEAD_DIM:HEAD_DIM + 1] + jnp.exp2(sink - m)
    o = o_ext[:HEAD_DIM] / denom
    for gp in range(ATT_GROUP // 2):
        pair = jnp.concatenate([o[:, (2 * gp) * t:(2 * gp + 1) * t], o[:, (2 * gp + 1) * t:(2 * gp + 2) * t]],
                               axis=0)
        c0 = (h0 + 2 * gp) * HEAD_DIM
        o_ref[0, row0:row0 + t, c0:c0 + 2 * HEAD_DIM] = pair.T * g_ref[0, row0:row0 + t, c0:c0 + 2 * HEAD_DIM]


def _attn_ctx_kernel(sink_ref, q_ref, k_ref, v_ref, g_ref, o_ref):
    q = q_ref[0] * Q_SCALE
    k = k_ref[0].astype(BF16)
    v = v_ref[0].astype(BF16)
    scores = [_attn_scores_t(q, k[:, kv * HEAD_DIM:(kv + 1) * HEAD_DIM], kv) for kv in range(ATT_KV_HEADS)]
    for kv in range(ATT_KV_HEADS):
        _attn_finish_t(sink_ref, scores[kv], v[:, kv * HEAD_DIM:(kv + 1) * HEAD_DIM], g_ref, o_ref, kv)


def _attn_ctx_call(u, sink):
    b, l, _ = u.shape
    return pl.pallas_call(
        _attn_ctx_kernel,
        grid=(b,),
        in_specs=[pl.BlockSpec(memory_space=pltpu.SMEM),
                  pl.BlockSpec((1, l, W_ATT), lambda i: (i, 0, COL_QA // W_ATT)),
                  pl.BlockSpec((1, l, W_KV), lambda i: (i, 0, COL_KA // W_KV)),
                  pl.BlockSpec((1, l, W_KV), lambda i: (i, 0, COL_VA // W_KV)),
                  pl.BlockSpec((1, l, W_ATT), lambda i: (i, 0, COL_GA // W_ATT))],
        out_specs=pl.BlockSpec((1, l, W_ATT), lambda i: (i, 0, 0)),
        out_shape=jax.ShapeDtypeStruct((b, l, W_ATT), F32),
        compiler_params=_cparams(("parallel",)),
        name="attn_ctx",
    )(sink, u, u, u, u)


def _attn_lat_kernel(sink_ref, q_ref, kp_ref, kc_ref, kn_ref, vp_ref, vc_ref, vn_ref, kx_ref, vx_ref,
                     g_ref, o_ref):
    j = pl.program_id(1)
    last = pl.num_programs(1) - 1
    b = ATT_BLOCK
    nq = ATT_QB
    bf = lambda ref: ref[0].astype(BF16)
    kx, vx = bf(kx_ref), bf(vx_ref)
    kblk = [bf(kp_ref)] + [kc_ref[0, t * b:(t + 1) * b, :].astype(BF16) for t in range(nq)] + [bf(kn_ref)]
    vblk = [bf(vp_ref)] + [vc_ref[0, t * b:(t + 1) * b, :].astype(BF16) for t in range(nq)] + [bf(vn_ref)]
    keys = [jnp.concatenate(kblk[t:t + 3] + [kx], axis=0) for t in range(nq)]
    vals = [jnp.concatenate(vblk[t:t + 3] + [vx], axis=0) for t in range(nq)]
    cols = ATT_GROUP * b
    c = lax.broadcasted_iota(jnp.int32, (b, cols), 0)
    r = lax.broadcasted_iota(jnp.int32, (b, cols), 1) % b
    ok_prev = [(c >= r) & (j > 0)] + [c >= r] * (nq - 1)
    ok_next = [c <= r] * (nq - 1) + [(c <= r) & (j < last)]

    def band(s, t):
        return jnp.concatenate([jnp.where(ok_prev[t], s[:b], NEG), s[b:2 * b],
                                jnp.where(ok_next[t], s[2 * b:3 * b], NEG), s[3 * b:]], axis=0)

    chains = [(t, kv) for t in range(nq) for kv in range(ATT_KV_HEADS)]
    scores = []
    for t, kv in chains:
        q = q_ref[0, t * b:(t + 1) * b, :] * Q_SCALE
        scores.append(band(_attn_scores_t(q, keys[t][:, kv * HEAD_DIM:(kv + 1) * HEAD_DIM], kv), t))
    for (t, kv), s in zip(chains, scores):
        _attn_finish_t(sink_ref, s, vals[t][:, kv * HEAD_DIM:(kv + 1) * HEAD_DIM], g_ref, o_ref, kv, t * b)


def _attn_lat_call(u, kctx, vctx, sink, layer):
    b, l, _ = u.shape
    nb = l // ATT_BLOCK
    past = kctx.shape[2]
    kcol = COL_KA // W_KV
    vcol = COL_VA // W_KV
    nq = ATT_QB
    prev = lambda col: pl.BlockSpec((1, ATT_BLOCK, W_KV), lambda i, j: (i, jnp.maximum(nq * j - 1, 0), col))
    cur = lambda col: pl.BlockSpec((1, nq * ATT_BLOCK, W_KV), lambda i, j: (i, j, col))
    nxt = lambda col: pl.BlockSpec((1, ATT_BLOCK, W_KV),
                                   lambda i, j: (i, jnp.minimum(nq * j + nq, nb - 1), col))
    ctx = pl.BlockSpec((1, None, past, W_KV), lambda i, j: (i, layer, 0, 0))
    return pl.pallas_call(
        _attn_lat_kernel,
        grid=(b, nb // nq),
        in_specs=[pl.BlockSpec(memory_space=pltpu.SMEM),
                  pl.BlockSpec((1, nq * ATT_BLOCK, W_ATT), lambda i, j: (i, j, COL_QA // W_ATT)),
                  prev(kcol), cur(kcol), nxt(kcol), prev(vcol), cur(vcol), nxt(vcol), ctx, ctx,
                  pl.BlockSpec((1, nq * ATT_BLOCK, W_ATT), lambda i, j: (i, j, COL_GA // W_ATT))],
        out_specs=pl.BlockSpec((1, nq * ATT_BLOCK, W_ATT), lambda i, j: (i, j, 0)),
        out_shape=jax.ShapeDtypeStruct((b, l, W_ATT), F32),
        compiler_params=_cparams(("parallel", "parallel")),
        name="attn_lat",
    )(sink, u, u, u, u, u, u, u, kctx, vctx, u)


def _log_sigmoid(x):
    return jnp.minimum(x, 0.0) - jnp.log1p(jnp.exp(-jnp.abs(x)))


_TAB_DMAT = 0
_TAB_QDEC = 2 * RET_CHUNK
_TAB_KDEC = 3 * RET_CHUNK
_TAB_CDEC = 4 * RET_CHUNK
_TAB_ROWS = 4 * RET_CHUNK + 8


def _ret_kernel(q_ref, k_ref, v_ref, thl_ref, thb_ref, s0_ref, o_ref, sfin_ref, tab_ref, *, nc, cpt,
                npairs, has_s0):
    grp = pl.program_id(0)
    c = RET_CHUNK
    lane = lax.broadcasted_iota(jnp.int32, (1, LANES), 1)
    lo_head = lane < RET_DIM
    dd = lax.broadcasted_iota(jnp.int32, (LANES, LANES), 0)
    ee = lax.broadcasted_iota(jnp.int32, (LANES, LANES), 1)
    same_head = (dd < RET_DIM) == (ee < RET_DIM)
    lanes_of = lambda p: slice(p * LANES, (p + 1) * LANES)

    @pl.when(pl.program_id(1) == 0)
    def _():
        rowf = lax.broadcasted_iota(jnp.int32, (c, LANES), 0).astype(F32)
        ii = lax.broadcasted_iota(jnp.int32, (c, c), 0)
        jj = lax.broadcasted_iota(jnp.int32, (c, c), 1)
        for p in range(npairs):
            for d in range(2):
                lg_lane = _log_sigmoid(thl_ref[d, :, lanes_of(p)])
                dist = (ii - jj) if d == 0 else (jj - ii)
                for hh in range(2):
                    head = 2 * (grp * npairs + p) + hh
                    lg_h = _log_sigmoid(thb_ref[d, pl.ds(head, 1), :])
                    dm = jnp.where(dist >= 0, jnp.exp(lg_h * jnp.maximum(dist, 0).astype(F32)), 0.0)
                    tab_ref[p, d, _TAB_DMAT + hh * c:_TAB_DMAT + (hh + 1) * c, :] = dm
                if d == 0:
                    q_dec = jnp.exp(lg_lane * (rowf + 1.0))
                    k_dec = jnp.exp(lg_lane * (c - 1.0 - rowf))
                else:
                    q_dec = jnp.exp(lg_lane * (c - rowf))
                    k_dec = jnp.exp(lg_lane * rowf)
                tab_ref[p, d, _TAB_QDEC:_TAB_QDEC + c, :] = q_dec
                tab_ref[p, d, _TAB_KDEC:_TAB_KDEC + c, :] = k_dec
                tab_ref[p, d, _TAB_CDEC:_TAB_CDEC + 8, :] = jnp.broadcast_to(
                    jnp.exp(lg_lane * float(c)), (8, LANES))

    def first_level(p, d, r0):
        qc = q_ref[0, pl.ds(r0, c), lanes_of(p)]
        kc = k_ref[0, pl.ds(r0, c), lanes_of(p)]
        vcb = v_ref[0, pl.ds(r0, c), lanes_of(p)].astype(BF16)
        qs = jnp.concatenate([jnp.where(lo_head, qc, 0.0), jnp.where(lo_head, 0.0, qc)], axis=0)
        sc = lax.dot_general(qs.astype(BF16), kc.astype(BF16), _NT, preferred_element_type=F32)
        sc = sc * tab_ref[p, d, _TAB_DMAT:_TAB_DMAT + 2 * c, :]
        kd = kc * tab_ref[p, d, _TAB_KDEC:_TAB_KDEC + c, :]
        upd = jnp.where(same_head, _dot(kd.T.astype(BF16), vcb), 0.0)
        return qc, vcb, sc.astype(BF16), upd

    def second_level(p, d, lvl1, s):
        qc, vcb, scb, upd = lvl1
        pv = _dot(scb, vcb)
        qd = qc * tab_ref[p, d, _TAB_QDEC:_TAB_QDEC + c, :]
        o = _dot(qd.astype(BF16), s.astype(BF16)) + jnp.where(lo_head, pv[:c], pv[c:])
        return o, tab_ref[p, d, _TAB_CDEC:_TAB_CDEC + 1, :] * s + upd

    def init_state(p, d):
        if not has_s0:
            return jnp.zeros((LANES, LANES), F32)
        z = jnp.zeros((RET_DIM, RET_DIM), F32)
        return jnp.concatenate([jnp.concatenate([s0_ref[0, d, 2 * p], z], axis=1),
                                jnp.concatenate([z, s0_ref[0, d, 2 * p + 1]], axis=1)], axis=0)

    units = [(p, d) for p in range(npairs) for d in range(2)]

    def scan_body(second, n, states):
        def row0(d, j):
            idx = n * cpt + j
            return pl.multiple_of((idx if d == 0 else nc - 1 - idx) * c, c)

        lvl = {(p, d, j): first_level(p, d, row0(d, j)) for j in range(cpt) for p, d in units}
        states = list(states)
        for j in range(cpt):
            for ui, (p, d) in enumerate(units):
                o, states[ui] = second_level(p, d, lvl[(p, d, j)], states[ui])
                dst = (0, pl.ds(row0(d, j), c), lanes_of(p))
                if second(j):
                    o = o + o_ref[dst]
                o_ref[dst] = o
        return tuple(states)

    assert nc % cpt == 0
    trips = nc // cpt
    states = tuple(init_state(p, d) for p, d in units)
    if trips == 1:
        states = scan_body(lambda j: 2 * j > nc - 1, 0, states)
    else:
        assert trips % 2 == 0
        states = lax.fori_loop(0, trips // 2, functools.partial(scan_body, lambda j: False), states)
        states = lax.fori_loop(trips // 2, trips, functools.partial(scan_body, lambda j: True), states)
    for ui, (p, d) in enumerate(units):
        sfin_ref[0, d, 2 * p] = states[ui][:RET_DIM, :RET_DIM]
        sfin_ref[0, d, 2 * p + 1] = states[ui][RET_DIM:, RET_DIM:]


def _ret_call(u, theta, s0bd, layer=0):
    b, l, _ = u.shape
    nc = l // RET_CHUNK
    has_s0 = s0bd is not None
    cpt = 16 if nc >= 16 else min(nc, 8)
    npairs = max(1, 8 // cpt)
    ngrp = RET_HEADS // 2 // npairs
    w = npairs * LANES
    st_block = (2, 2 * npairs, RET_DIM, RET_DIM)
    if not has_s0:
        s0bd = jnp.zeros((1,) + st_block, F32)
        s0_spec = pl.BlockSpec((1,) + st_block, lambda g, i: (0, 0, 0, 0, 0))
    else:
        s0_spec = pl.BlockSpec((1, None) + st_block, lambda g, i: (i, layer, 0, g, 0, 0))
    th_lane = jnp.repeat(theta, RET_DIM, axis=1).reshape(2, 1, W_RET)
    th_bcast = jnp.broadcast_to(theta[:, :, None], (2, RET_HEADS, LANES))
    col = lambda c0: pl.BlockSpec((1, l, w), lambda g, i: (i, 0, c0 // w + g))
    o, sfin = pl.pallas_call(
        functools.partial(_ret_kernel, nc=nc, cpt=cpt, npairs=npairs, has_s0=has_s0),
        grid=(ngrp, b),
        in_specs=[col(COL_QR), col(COL_KR), col(COL_VR),
                  pl.BlockSpec((2, 1, w), lambda g, i: (0, 0, g)),
                  pl.BlockSpec((2, RET_HEADS, LANES), lambda g, i: (0, 0, 0)),
                  s0_spec],
        out_specs=[pl.BlockSpec((1, l, w), lambda g, i: (i, 0, g)),
                   pl.BlockSpec((1,) + st_block, lambda g, i: (i, 0, g, 0, 0))],
        out_shape=[jax.ShapeDtypeStruct((b, l, W_RET), F32),
                   jax.ShapeDtypeStruct((b, 2, RET_HEADS, RET_DIM, RET_DIM), F32)],
        scratch_shapes=[pltpu.VMEM((npairs, 2, _TAB_ROWS, LANES), F32)],
        compiler_params=_cparams(("arbitrary", "arbitrary")),
        name="retention_s0" if has_s0 else "retention",
    )(u, u, u, th_lane, th_bcast, s0bd)
    return o, sfin


def _filter_positions(l):
    f32 = np.float32
    t = np.linspace(0.0, 1.0, l, dtype=f32)[:, None]
    w = (f32(2.0 * math.pi) * np.arange(l, dtype=f32)[:, None] / f32(l)).astype(f32)
    f = np.linspace(1e-4, HY_BANDS - 1, HY_BANDS, dtype=f32)[None, :]
    z = np.concatenate([t, np.cos(f * w), -np.sin(f * w)], axis=-1).astype(f32)
    z = np.pad(z, ((0, 0), (0, HY_POS_PAD - HY_POS_FEAT)))
    return jnp.asarray(np.concatenate([z, z[:1], z[1:][::-1]], axis=0))


def _hyena_deltas():
    max_decay = math.log(HY_DECAY_TARGET) / HY_FAST_PCT
    min_decay = math.log(HY_DECAY_TARGET) / HY_SLOW_PCT
    return jnp.asarray(np.abs(np.linspace(min_decay, max_decay, W_HY, dtype=np.float32))[None, :])


def _filter_hidden(z_ref, w1_ref, b1_ref, fr_ref):
    pre = jnp.dot(z_ref[...], w1_ref[0], precision=HIGHEST, preferred_element_type=F32) + b1_ref[0]
    return jnp.sin(fr_ref[0] * pre)


def _filter_raw(hid, w2f, w2b, tp, dl, row0, l):
    win = jnp.exp(-tp * dl)
    row = row0 + lax.broadcasted_iota(jnp.int32, win.shape, 0)
    hf = jnp.dot(hid, w2f, precision=HIGHEST, preferred_element_type=F32) * win
    hb = jnp.dot(hid, w2b, precision=HIGHEST, preferred_element_type=F32) * win
    hf = jnp.where(row < l, hf, 0.0)
    hb = jnp.where((row > l) | (row == 0), hb, 0.0)
    return hf + hb, jnp.sum(jnp.abs(hf) + jnp.abs(hb), axis=0, keepdims=True)


def _with_skip(g, skip):
    row = lax.broadcasted_iota(jnp.int32, g.shape, 0)
    return g + jnp.where(row == 0, skip, 0.0)


def _filt_ctx_kernel(z_ref, w1_ref, b1_ref, fr_ref, w2_ref, dl_ref, sk_ref, fh_ref, fl_ref, g_ref):
    hid = _filter_hidden(z_ref, w1_ref, b1_ref, fr_ref)
    tp = z_ref[:, 0:1]
    for o in range(2):
        w2f = w2_ref[0, :, (2 * o) * W_HY:(2 * o + 1) * W_HY]
        w2b = w2_ref[0, :, (2 * o + 1) * W_HY:(2 * o + 2) * W_HY]
        raw, nrm = _filter_raw(hid, w2f, w2b, tp, dl_ref[...], 0, z_ref.shape[0] // 2)
        g = _with_skip(raw / nrm, sk_ref[0, pl.ds(o, 1), :])
        g_ref[0, o] = _dot3c(fh_ref[...], fl_ref[...], g)


def _ctx_dft_tables(l):
    n = 2 * l
    k = np.arange(n)[:, None]
    t = np.arange(l)[None, :]
    ang = 2.0 * np.pi * k * t / n
    c, s = np.cos(ang), np.sin(ang)
    fwd = np.block([[c, s], [-s, c]])
    inv = np.block([[c.T, -s.T], [s.T, c.T]])
    n_all = np.arange(n)[None, :]
    angg = 2.0 * np.pi * k * n_all / n
    filt = np.concatenate([np.cos(angg), -np.sin(angg)], axis=0) / n
    return _split_np(fwd), _split_np(inv), _split_np(filt)


def _filt_ctx_call(l, w1, b1, freq, w2, skip):
    n = 2 * l
    z_ext = _filter_positions(l)
    _, _, (fh, fl) = _ctx_dft_tables(l)
    w1p = jnp.pad(w1, ((0, 0), (0, HY_POS_PAD - HY_POS_FEAT), (0, 0)))
    lay = lambda *shape: pl.BlockSpec((1,) + shape, lambda d: (d,) + (0,) * len(shape))
    full = lambda a: pl.BlockSpec(a.shape, lambda d: (0,) * a.ndim)
    dl = _hyena_deltas()
    return pl.pallas_call(
        _filt_ctx_kernel,
        grid=(DEPTH,),
        in_specs=[full(z_ext), lay(HY_POS_PAD, HY_FILT_HID), lay(1, HY_FILT_HID), lay(1, HY_FILT_HID),
                  lay(HY_FILT_HID, 4 * W_HY), full(dl), lay(2, W_HY), full(fh), full(fl)],
        out_specs=pl.BlockSpec((1, 2, 2 * n, W_HY), lambda d: (d, 0, 0, 0)),
        out_shape=jax.ShapeDtypeStruct((DEPTH, 2, 2 * n, W_HY), F32),
        compiler_params=_cparams(("arbitrary",)),
        name="hyena_filter_ctx",
    )(z_ext, w1p, b1.reshape(DEPTH, 1, -1), freq.reshape(DEPTH, 1, -1), w2, dl, skip, fh, fl)


def _lat_dft_tables():
    ka = np.arange(NA)[:, None]
    b = np.arange(NB)[:, None, None]
    kb = np.arange(NB)[:, None]
    bb = np.arange(NB)[None, :]
    a_half = np.arange(NA // 2)[None, :]
    a_full = np.arange(NA)[None, :]
    phi = 2.0 * np.pi * (ka * a_half / NA + b * ka / LAT_N)
    c, s = np.cos(phi), np.sin(phi)
    a_fwd = np.concatenate([c, s], axis=2)
    ct, st = np.swapaxes(c, 1, 2), np.swapaxes(s, 1, 2)
    a_inv = np.concatenate([ct, st], axis=2)
    phig = 2.0 * np.pi * (ka * a_full / NA + b * ka / LAT_N)
    a_flt = np.concatenate([np.cos(phig), -np.sin(phig)], axis=1) / LAT_N
    ang = 2.0 * np.pi * kb * bb / NB
    c2, s2 = np.cos(ang), np.sin(ang)
    f_fwd = np.block([[c2, s2], [-s2, c2]])
    f_inv = np.block([[c2, -s2], [s2, c2]])
    return (_split_np(a_fwd), _split_np(a_inv), _split_np(a_flt), _split_np(f_fwd), _split_np(f_inv))


def _stage_b_rows(ka):
    re = pl.ds(ka, NB, stride=Y_PITCH)
    im = pl.ds(NA + ka, NB, stride=Y_PITCH)
    return re, im


def _filt_lat_kernel(z_ref, w1_ref, b1_ref, fr_ref, w2f_ref, w2b_ref, dl_ref, sk_ref, ah_ref, f2h_ref,
                     g_ref, hid_ref, gt_ref, y_ref):
    step = pl.program_id(1)
    rch = 1024
    nch = LAT_N // rch
    rows_of = lambda i: pl.ds(pl.multiple_of(i * rch, rch), rch)

    @pl.when(step == 0)
    def _():
        def hid_chunk(i, carry):
            r = rows_of(i)
            pre = jnp.dot(z_ref[r, :], w1_ref[0], precision=HIGHEST, preferred_element_type=F32)
            hid_ref[r, :] = jnp.sin(fr_ref[0] * (pre + b1_ref[0]))
            return carry

        lax.fori_loop(0, nch, hid_chunk, 0)

    w2f_hl = _split(w2f_ref[0])
    w2b_hl = _split(w2b_ref[0])

    def raw_chunk(w2_hl, i, nrm):
        r = rows_of(i)
        hh, hl = _split(hid_ref[r, :])
        h = _dot(hh, w2_hl[0]) + _dot(hl, w2_hl[0]) + _dot(hh, w2_hl[1])
        h = h * jnp.exp(-z_ref[r, 0:1] * dl_ref[...])
        row = i * rch + lax.broadcasted_iota(jnp.int32, h.shape, 0)
        h = jnp.where(row == LAT_L, 0.0, h)
        for s in range(rch // NB):
            slab = pl.ds(pl.multiple_of((i * (rch // NB) + s) * X_PITCH, 8), NB)
            gt_ref[slab, :] = h[s * NB:(s + 1) * NB]
        return nrm + jnp.sum(jnp.abs(h), axis=0, keepdims=True)

    nrm = lax.fori_loop(0, nch // 2, functools.partial(raw_chunk, w2f_hl), jnp.zeros((1, LANES), F32))
    nrm = lax.fori_loop(nch // 2, nch, functools.partial(raw_chunk, w2b_hl), nrm)
    hh, hl = _split(hid_ref[0:8, :])
    hb0 = _dot(hh, w2b_hl[0]) + _dot(hl, w2b_hl[0]) + _dot(hh, w2b_hl[1])
    hb0 = hb0 * jnp.exp(-z_ref[0:8, 0:1] * dl_ref[...])
    hb0 = jnp.where(lax.broadcasted_iota(jnp.int32, hb0.shape, 0) == 0, hb0, 0.0)
    gt_ref[0:8, :] = gt_ref[0:8, :] + hb0
    nrm = nrm + jnp.sum(jnp.abs(hb0), axis=0, keepdims=True)

    def norm_slab(a, carry):
        slab = pl.ds(pl.multiple_of(a * X_PITCH, 8), NB)
        gt_ref[slab, :] = gt_ref[slab, :] / nrm
        return carry

    lax.fori_loop(0, NA, norm_slab, 0, unroll=8)
    order = step // (W_HY // LANES)
    gt_ref[0:8, :] = _with_skip(gt_ref[0:8, :], sk_ref[0, pl.ds(order, 1), :])

    def stage_a(b, carry):
        rows = gt_ref[pl.ds(b, NA, stride=X_PITCH), :]
        y_ref[pl.ds(pl.multiple_of(b * Y_PITCH, 8), 2 * NA), :] = _dot1c(ah_ref[b], rows)
        return carry

    lax.fori_loop(0, NB, stage_a, 0, unroll=32)

    def stage_b(j, carry):
        ka = 2 * j
        re, im = _stage_b_rows(ka)
        re1, im1 = _stage_b_rows(ka + 1)
        z = jnp.concatenate([jnp.concatenate([y_ref[re, :], y_ref[im, :]], axis=0),
                             jnp.concatenate([y_ref[re1, :], y_ref[im1, :]], axis=0)], axis=1)
        x = _dot1c(f2h_ref[...], z)
        g0 = pl.multiple_of(ka * 2 * NB, 2 * NB)
        g_ref[0, 0, pl.ds(g0, 2 * NB), :] = x[:, :LANES]
        g_ref[0, 0, pl.ds(g0 + 2 * NB, 2 * NB), :] = x[:, LANES:]
        return carry

    lax.fori_loop(0, NA // 2, stage_b, 0, unroll=8)


def _filt_lat_call(w1, b1, freq, w2, skip):
    z_ext = _filter_positions(LAT_L)
    _, _, (ah, _), (f2h, _), _ = _lat_dft_tables()
    w1p = jnp.pad(w1, ((0, 0), (0, HY_POS_PAD - HY_POS_FEAT), (0, 0)))
    nct = W_HY // LANES
    one = pl.Buffered(1)
    lay = lambda *shape: pl.BlockSpec((1,) + shape, lambda d, s: (d,) + (0,) * len(shape))
    full = lambda a: pl.BlockSpec(a.shape, lambda d, s: (0,) * a.ndim, pipeline_mode=one)
    dl = _hyena_deltas()
    return pl.pallas_call(
        _filt_lat_kernel,
        grid=(DEPTH, 2 * nct),
        in_specs=[full(z_ext), lay(HY_POS_PAD, HY_FILT_HID), lay(1, HY_FILT_HID), lay(1, HY_FILT_HID),
                  pl.BlockSpec((1, HY_FILT_HID, LANES), lambda d, s: (d, 0, (s // nct) * 2 * nct + s % nct)),
                  pl.BlockSpec((1, HY_FILT_HID, LANES),
                               lambda d, s: (d, 0, (s // nct) * 2 * nct + nct + s % nct)),
                  pl.BlockSpec((1, LANES), lambda d, s: (0, s % nct)),
                  pl.BlockSpec((1, 2, LANES), lambda d, s: (d, 0, s % nct)),
                  full(ah), full(f2h)],
        out_specs=pl.BlockSpec((1, 1, NA * 2 * NB, LANES), lambda d, s: (d, s, 0, 0)),
        out_shape=jax.ShapeDtypeStruct((DEPTH, 2 * nct, NA * 2 * NB, LANES), F32),
        scratch_shapes=[pltpu.VMEM((LAT_N, HY_FILT_HID), F32), pltpu.VMEM((NA * X_PITCH, LANES), F32),
                        pltpu.VMEM((NB * Y_PITCH, LANES), F32)],
        compiler_params=_cparams(("arbitrary", "arbitrary")),
        name="hyena_filter_lat",
    )(z_ext, w1p, b1.reshape(DEPTH, 1, -1), freq.reshape(DEPTH, 1, -1), w2, w2, dl, skip, ah, f2h)


def _short_conv_rows(ref, bi, r0, rows, first, last, w):
    total = ref.shape[1]
    cur = ref[bi, pl.ds(r0, rows), :]
    before = ref[bi, pl.ds(jnp.maximum(r0 - 1, 0), 1), :]
    after = ref[bi, pl.ds(jnp.minimum(r0 + rows, total - 1), 1), :]
    before = jnp.where(first, 0.0, before)
    after = jnp.where(last, 0.0, after)
    rid = lax.broadcasted_iota(jnp.int32, cur.shape, 0)
    prev = jnp.where(rid == 0, before, pltpu.roll(cur, 1, 0))
    nxt = jnp.where(rid == rows - 1, after, pltpu.roll(cur, rows - 1, 0))
    return prev * w[0:1] + cur * w[1:2] + nxt * w[2:3]


def _short_conv_interior(ref, bi, r0, rows, w):
    prev = ref[bi, pl.ds(r0 - 1, rows), :]
    cur = ref[bi, pl.ds(r0, rows), :]
    nxt = ref[bi, pl.ds(r0 + 1, rows), :]
    return prev * w[0:1] + cur * w[1:2] + nxt * w[2:3]


def _cmul(xr, xi, gr, gi):
    return xr * gr - xi * gi, xr * gi + xi * gr


def _hy_ctx_kernel(v_ref, x1_ref, x2_ref, cw_ref, g_ref, fh_ref, ih_ref, o_ref):
    l = v_ref.shape[1]
    n = 2 * l

    def sc(ref, bi, grp):
        w = cw_ref[:, grp * W_HY:(grp + 1) * W_HY]
        return _short_conv_rows(ref, bi, 0, l, True, True, w)

    def conv(zr, zi, order):
        x = _dot1c(fh_ref[...], jnp.concatenate([zr, zi], axis=0))
        pr, pi = _cmul(x[:n], x[n:], g_ref[order, :n], g_ref[order, n:])
        y = _dot1c(ih_ref[...], jnp.concatenate([pr, pi], axis=0))
        return y[:l], y[l:]

    yr, yi = conv(sc(v_ref, 0, 0), sc(v_ref, 1, 0), 0)
    yr, yi = conv(sc(x1_ref, 0, 1) * yr, sc(x1_ref, 1, 1) * yi, 1)
    o_ref[0] = sc(x2_ref, 0, 2) * yr
    o_ref[1] = sc(x2_ref, 1, 2) * yi


def _hy_ctx_call(u, conv_w, g_spec, layer):
    b, l, _ = u.shape
    (fh, _), (ih, _), _ = _ctx_dft_tables(l)
    grp = lambda g: pl.BlockSpec((2, l, W_HY), lambda i: (i, 0, g))
    full = lambda a: pl.BlockSpec(a.shape, lambda i: (0,) * a.ndim)
    return pl.pallas_call(
        _hy_ctx_kernel,
        grid=(b // 2,),
        in_specs=[grp(0), grp(1), grp(2), full(conv_w),
                  pl.BlockSpec((None,) + g_spec.shape[1:], lambda i: (layer, 0, 0, 0)), full(fh), full(ih)],
        out_specs=pl.BlockSpec((2, l, W_HY), lambda i: (i, 0, 0)),
        out_shape=jax.ShapeDtypeStruct((b, l, W_HY), F32),
        compiler_params=_cparams(("parallel",)),
        name="hyena_ctx",
    )(u, u, u, conv_w, g_spec, fh, ih)


def _hy_lat_kernel(z_ref, m_ref, cw_ref, g_ref, af_ref, ai_ref, f2_ref, f3_ref, o_ref, xr_scr, xi_scr,
                   y_scr, *, conv_in):
    x_scr = (xr_scr, xi_scr)
    na_half = NA // 2
    w_in = cw_ref[0] if conv_in else None
    w_mul = cw_ref[1]

    def conv_slab(ref, bi, a, w):
        if isinstance(a, int):
            return _short_conv_rows(ref, bi, a * NB, NB, a == 0, a == na_half - 1, w)
        return _short_conv_interior(ref, bi, pl.multiple_of(a * NB, NB), NB, w)

    def edges_then_interior(body):
        body(0, 0)
        body(na_half - 1, 0)
        lax.fori_loop(1, na_half - 1, body, 0, unroll=2)

    def load_in(a, carry):
        for bi in range(2):
            if conv_in:
                val = conv_slab(z_ref, bi, a, w_in)
            else:
                val = z_ref[bi, pl.ds(pl.multiple_of(a * NB, NB), NB), :]
            x_scr[bi][pl.ds(pl.multiple_of(a * X_PITCH, 8), NB), :] = val
        return carry

    edges_then_interior(load_in)

    def stage_a(b, carry):
        zr = xr_scr[pl.ds(b, na_half, stride=X_PITCH), :]
        zi = xi_scr[pl.ds(b, na_half, stride=X_PITCH), :]
        rhs = jnp.concatenate([jnp.concatenate([zr, zi], axis=0), jnp.concatenate([zi, -zr], axis=0)],
                              axis=1)
        y = _dot1c(af_ref[b], rhs)
        r0 = pl.multiple_of(b * Y_PITCH, 8)
        y_scr[pl.ds(r0, NA), :] = y[:, :LANES]
        y_scr[pl.ds(r0 + NA, NA), :] = y[:, LANES:]
        return carry

    lax.fori_loop(0, NB, stage_a, 0, unroll=NB)

    def spectrum_product(j):
        ka = 2 * j
        re, im = _stage_b_rows(ka)
        re1, im1 = _stage_b_rows(ka + 1)
        z = jnp.concatenate([jnp.concatenate([y_scr[re, :], y_scr[im, :]], axis=0),
                             jnp.concatenate([y_scr[re1, :], y_scr[im1, :]], axis=0)], axis=1)
        x = _dot1c(f2_ref[...], z)
        g0 = pl.multiple_of(ka * 2 * NB, 2 * NB)
        gr = jnp.concatenate([g_ref[pl.ds(g0, NB), :], g_ref[pl.ds(g0 + 2 * NB, NB), :]], axis=1)
        gi = jnp.concatenate([g_ref[pl.ds(g0 + NB, NB), :], g_ref[pl.ds(g0 + 3 * NB, NB), :]], axis=1)
        pr, pi = _cmul(x[:NB], x[NB:], gr, gi)
        return jnp.concatenate([pr, pi], axis=0).astype(BF16)

    def inverse_b(j, prod):
        ka = 2 * j
        re, im = _stage_b_rows(ka)
        re1, im1 = _stage_b_rows(ka + 1)
        u = _dot(f3_ref[...], prod)
        y_scr[re, :] = u[:NB, :LANES]
        y_scr[im, :] = u[NB:, :LANES]
        y_scr[re1, :] = u[:NB, LANES:]
        y_scr[im1, :] = u[NB:, LANES:]

    def stage_b(j, prod):
        nxt = spectrum_product(j + 1)
        inverse_b(j, prod)
        return nxt

    last = lax.fori_loop(0, NA // 2 - 1, stage_b, spectrum_product(0), unroll=10)
    inverse_b(NA // 2 - 1, last)

    def stage_c(b, carry):
        r0 = pl.multiple_of(b * Y_PITCH, 8)
        ur = y_scr[pl.ds(r0, NA), :]
        ui = y_scr[pl.ds(r0 + NA, NA), :]
        rhs = jnp.concatenate([jnp.concatenate([ur, -ui], axis=0), jnp.concatenate([ui, ur], axis=0)],
                              axis=1)
        y = _dot1c(ai_ref[b], rhs)
        xr_scr[pl.ds(b, na_half, stride=X_PITCH), :] = y[:, :LANES]
        xi_scr[pl.ds(b, na_half, stride=X_PITCH), :] = y[:, LANES:]
        return carry

    lax.fori_loop(0, NB, stage_c, 0, unroll=NB)

    def store_out(a, carry):
        for bi in range(2):
            mul = conv_slab(m_ref, bi, a, w_mul)
            o_ref[bi, pl.ds(pl.multiple_of(a * NB, NB), NB), :] = (
                x_scr[bi][pl.ds(pl.multiple_of(a * X_PITCH, 8), NB), :] * mul)
        return carry

    edges_then_interior(store_out)


def _hy_lat_call(src, src_col, u, mul_col, conv_w2, g_spec, layer, order, *, conv_in):
    b, l, _ = u.shape
    nct = W_HY // LANES
    (af, _), (ai, _), _, (f2, _), (f3, _) = _lat_dft_tables()
    one = pl.Buffered(1)
    blk = lambda col: pl.BlockSpec((2, l, LANES), lambda c, p: (p, 0, col + c))
    const = lambda a: pl.BlockSpec(a.shape, lambda c, p: (0,) * a.ndim, pipeline_mode=one)
    return pl.pallas_call(
        functools.partial(_hy_lat_kernel, conv_in=conv_in),
        grid=(nct, b // 2),
        in_specs=[blk(src_col), blk(mul_col),
                  pl.BlockSpec((2, 3, LANES), lambda c, p: (0, 0, c)),
                  pl.BlockSpec((None, None, NA * 2 * NB, LANES), lambda c, p: (layer, order * nct + c, 0, 0)),
                  const(af), const(ai), const(f2), const(f3)],
        out_specs=pl.BlockSpec((2, l, LANES), lambda c, p: (p, 0, c)),
        out_shape=jax.ShapeDtypeStruct((b, l, W_HY), F32),
        scratch_shapes=[pltpu.VMEM(((NA // 2) * X_PITCH, LANES), F32),
                        pltpu.VMEM(((NA // 2) * X_PITCH, LANES), F32),
                        pltpu.VMEM((NB * Y_PITCH, LANES), F32)],
        compiler_params=_cparams(("arbitrary", "arbitrary")),
        name="hyena_lat_conv_in" if conv_in else "hyena_lat",
    )(src, u, conv_w2, g_spec, af, ai, f2, f3)


def _rope_tables(l):
    f32 = np.float32
    rows = l // GRID_W
    row = np.repeat(np.arange(rows), GRID_W).astype(f32)
    col = np.tile(np.arange(GRID_W), rows).astype(f32)
    quarter = HEAD_DIM // 4
    inv = np.power(f32(ROPE_BASE), -np.arange(quarter, dtype=f32) / f32(quarter)).astype(f32)
    ang = np.concatenate([row[:, None] * inv, col[:, None] * inv], axis=-1).astype(f32)
    cos, sin = np.cos(ang), np.sin(ang)
    q = quarter
    cos_h = np.concatenate([cos[:, :q], cos[:, :q], cos[:, q:], cos[:, q:]], axis=-1)
    sin_h = np.concatenate([-sin[:, :q], sin[:, :q], -sin[:, q:], sin[:, q:]], axis=-1)
    return jnp.asarray(np.tile(cos_h, (1, 2))), jnp.asarray(np.tile(sin_h, (1, 2)))


def kernel(x_prompt, x_sample, c, cache_k, cache_v, state_ret, c_ctx, norm_w, w_mod, b_mod, w_in, hy_conv,
           hy_filt_w1, hy_filt_b1, hy_filt_freq, hy_filt_w2, hy_skip, attn_sink, ret_theta, ret_gn,
           w_branch_a, w_branch_b, w_branch_c, w_merge, b_merge, w_out, final_norm_w):
    d = D_MODEL
    bc, lc, _ = x_prompt.shape
    bl, ll, _ = x_sample.shape
    assert ll == LAT_L and bc % 2 == 0 and bl % 2 == 0
    past = cache_k.shape[2]

    assert bl + 1 <= COND_ROWS
    cond = jnp.zeros((COND_ROWS, d), F32).at[:bl].set(c).at[bl].set(c_ctx)
    mod = _mod_call(cond, w_mod, b_mod)

    g_ctx = _filt_ctx_call(lc, hy_filt_w1, hy_filt_b1, hy_filt_freq, hy_filt_w2, hy_skip)
    g_lat = _filt_lat_call(hy_filt_w1, hy_filt_b1, hy_filt_freq, hy_filt_w2, hy_skip)

    cos_t, sin_t = _rope_tables(ll)
    w_in_b = w_in.astype(BF16)
    wm_b = w_merge.astype(BF16)
    wa_b = w_branch_a.astype(BF16)
    wb_b = w_branch_b.astype(BF16)
    wc_b = w_branch_c.astype(BF16)
    wo_b = w_out.astype(BF16)
    fnw = final_norm_w.reshape(1, d)
    k_ctx = cache_k.reshape(bl, DEPTH, past, W_KV)
    v_ctx = cache_v.reshape(bl, DEPTH, past, W_KV)
    hy_cols = COL_HY // LANES
    nct = W_HY // LANES

    xp, xs = x_prompt, x_sample
    ks_out, vs_out, ss_out = [], [], []
    for l in range(DEPTH):
        final = l == DEPTH - 1
        nw = norm_w[l].reshape(1, d)
        bm = b_merge[l].reshape(1, -1)
        gn = ret_gn[l].reshape(1, W_RET)
        shift, scale, gate = (mod[l, :, i * d:(i + 1) * d][:, None, :] for i in range(3))
        conv_w = hy_conv[l]
        cw = lambda g: conv_w[:, g * W_HY:(g + 1) * W_HY]

        sl = slice(bl, bl + 1)
        u = _in_call(xp, shift[sl], scale[sl], nw, w_in_b[l], cos_t, sin_t, rope=False)
        ya = _hy_ctx_call(u, conv_w, g_ctx, l)
        yb = _attn_ctx_call(u, attn_sink[l])
        yc, sfin = _ret_call(u, ret_theta[l], None)
        res = _out_call(xp, shift[sl], scale[sl], gate[sl], nw, ya, u, yb, yc, gn, wm_b[l], bm, wa_b[l], wb_b[l],
                        wc_b[l], wo_b[l], fnw, final=final)
        xp = res[0]
        if final:
            y_prompt = res[1]
        ks_out.append(u[:, :, COL_KA:COL_KA + W_KV])
        vs_out.append(u[:, :, COL_VA:COL_VA + W_KV])
        ss_out.append(sfin)

        sl = slice(0, bl)
        u = _in_call(xs, shift[sl], scale[sl], nw, w_in_b[l], cos_t, sin_t, rope=True)
        z1 = _hy_lat_call(u, hy_cols, u, hy_cols + nct, jnp.stack([cw(0), cw(1)]), g_lat, l, 0, conv_in=True)
        ya = _hy_lat_call(z1, 0, u, hy_cols + 2 * nct, jnp.stack([cw(2), cw(2)]), g_lat, l, 1, conv_in=False)
        yb = _attn_lat_call(u, k_ctx, v_ctx, attn_sink[l], l)
        yc, _ = _ret_call(u, ret_theta[l], state_ret, l)
        res = _out_call(xs, shift[sl], scale[sl], gate[sl], nw, ya, u, yb, yc, gn, wm_b[l], bm, wa_b[l], wb_b[l],
                        wc_b[l], wo_b[l], fnw, final=final)
        xs = res[0]
        if final:
            y_sample = res[1]

    kv_shape = (bc, DEPTH, lc, ATT_KV_HEADS, HEAD_DIM)
    new_cache_k = jnp.stack(ks_out, axis=1).reshape(kv_shape)
    new_cache_v = jnp.stack(vs_out, axis=1).reshape(kv_shape)
    new_state_ret = jnp.stack(ss_out, axis=1)
    return (y_prompt, y_sample, new_cache_k, new_cache_v, new_state_ret)
```
